```python
import math
import jax, jax.numpy as jnp
from jax import lax
import numpy as np

D_MODEL = 1024
BATCH = 8
SEQ = 8192
DEPTH = 1

CHUNK = 64
MIX_WIDTH = D_MODEL
A_HEADS = 4
A_HEAD_DIM = 128
A_WIDTH = A_HEADS * A_HEAD_DIM
SGU_BLOCK = 128
B_GROUPS = 4
B_GROUP_DIM = 128
B_WIDTH = B_GROUPS * B_GROUP_DIM
POOL_WINDOWS = (2, 4, 8, 16)
IN_WIDTH = 2 * A_WIDTH + B_WIDTH
D_FF = 2816
CONV_W = 3
N_MOD = 6
EPS = 1e-6

kernel_name = "hybrid_sgu_pool_convffn_block"


def rmsnorm(x, g):
    xf = x.astype(jnp.float32)
    y = xf * lax.rsqrt(jnp.mean(xf * xf, axis=-1, keepdims=True) + EPS)
    return (y * g.astype(jnp.float32)).astype(x.dtype)


def chunk_causal_block_mask():
    pos = jnp.arange(SGU_BLOCK)
    return (pos[None, :] // CHUNK) <= (pos[:, None] // CHUNK)


def sgu_mixer(a_in, norm_g, w_s, b_s):
    bsz, seq, _ = a_in.shape
    a = jax.nn.gelu(a_in)
    u, v = a[..., :A_WIDTH], a[..., A_WIDTH:]
    nb = seq // SGU_BLOCK
    v = v.reshape(bsz, nb, SGU_BLOCK, A_HEADS, A_HEAD_DIM)
    v = rmsnorm(v, norm_g[None, None, None])
    w_m = jnp.where(chunk_causal_block_mask()[None], w_s, jnp.zeros_like(w_s))
    z = jnp.einsum('hij,bnjhc->bnihc', w_m, v) + b_s.T[None, None, :, :, None]
    return u * z.reshape(bsz, seq, A_WIDTH)


def pool_mixer(p_in, w_pool, pool_scale):
    bsz, seq, _ = p_in.shape
    pg = p_in.reshape(bsz, seq, B_GROUPS, B_GROUP_DIM)
    t = jnp.arange(seq)
    outs = []
    for g, w in enumerate(POOL_WINDOWS):
        xf = pg[:, :, g].astype(jnp.float32)
        cs = jnp.cumsum(xf, axis=1)
        cs_full = jnp.pad(cs, ((0, 0), (w, 0), (0, 0)))
        win_sum = cs_full[:, w:] - cs_full[:, :seq]
        cnt = jnp.minimum(t + 1, w).astype(jnp.float32)[None, :, None]
        outs.append((win_sum / cnt - xf).astype(p_in.dtype))
    pooled = jnp.stack(outs, axis=2)
    y = jnp.einsum('bsgc,gcd->bsgd', pooled, w_pool)
    return y.reshape(bsz, seq, B_WIDTH) * pool_scale


def conv_ffn(h, w_up, conv_w, conv_b, w_down):
    up = jnp.einsum('bsd,df->bsf', h, w_up)
    upp = jnp.pad(up, ((0, 0), (CONV_W - 1, 0), (0, 0)))
    seq = up.shape[1]
    y = conv_b + upp[:, 0:seq] * conv_w[0] + upp[:, 1:seq + 1] * conv_w[1] + upp[:, 2:seq + 2] * conv_w[2]
    gate, val = y[..., :D_FF], y[..., D_FF:]
    return jnp.einsum('bsf,fd->bsd', jax.nn.silu(gate) * val, w_down)


def _fwd_setup_inputs(seed: int = 0) -> dict:
    key = jax.random.key(seed)
    ks = jax.random.split(key, 20)
    f32 = jnp.float32
    nrm = lambda k, shape, s: jax.random.normal(k, shape, f32) * s
    L = DEPTH
    return {
        "x": nrm(ks[0], (BATCH, SEQ, D_MODEL), 1.0),
        "c": nrm(ks[1], (BATCH, D_MODEL), 1.0),
        "w_ada": nrm(ks[2], (L, D_MODEL, N_MOD * D_MODEL), D_MODEL ** -0.5),
        "b_ada": nrm(ks[3], (L, N_MOD * D_MODEL), 0.02),
        "pre_mix_g": 1.0 + nrm(ks[4], (L, D_MODEL), 0.02),
        "post_mix_g": 1.0 + nrm(ks[5], (L, D_MODEL), 0.02),
        "w_in": nrm(ks[6], (L, D_MODEL, IN_WIDTH), D_MODEL ** -0.5),
        "sgu_norm_g": 1.0 + nrm(ks[7], (L, A_HEADS, A_HEAD_DIM), 0.02),
        "w_spatial": nrm(ks[8], (L, A_HEADS, SGU_BLOCK, SGU_BLOCK), SGU_BLOCK ** -0.5),
        "b_spatial": 1.0 + nrm(ks[9], (L, A_HEADS, SGU_BLOCK), 0.02),
        "w_pool": nrm(ks[10], (L, B_GROUPS, B_GROUP_DIM, B_GROUP_DIM), B_GROUP_DIM ** -0.5),
        "pool_scale": 1.0 + nrm(ks[11], (L, B_WIDTH), 0.02),
        "w_out": nrm(ks[12], (L, MIX_WIDTH, D_MODEL), MIX_WIDTH ** -0.5),
        "pre_ffn_g": 1.0 + nrm(ks[13], (L, D_MODEL), 0.02),
        "post_ffn_g": 1.0 + nrm(ks[14], (L, D_MODEL), 0.02),
        "w_up": nrm(ks[15], (L, D_MODEL, 2 * D_FF), D_MODEL ** -0.5),
        "conv_w": nrm(ks[16], (L, CONV_W, 2 * D_FF), CONV_W ** -0.5),
        "conv_b": nrm(ks[17], (L, 2 * D_FF), 0.02),
        "w_down": nrm(ks[18], (L, D_FF, D_MODEL), D_FF ** -0.5),
    }


def _fwd_reference(x, c, w_ada, b_ada, pre_mix_g, post_mix_g, w_in, sgu_norm_g, w_spatial,
              b_spatial, w_pool, pool_scale, w_out, pre_ffn_g, post_ffn_g, w_up, conv_w,
              conv_b, w_down):
    sc = jax.nn.silu(c)
    for l in range(DEPTH):
        mod = (jnp.einsum('bd,de->be', sc, w_ada[l]) + b_ada[l])[:, None, :]
        sh_m, sc_m, g_m, sh_f, sc_f, g_f = jnp.split(mod, N_MOD, axis=-1)

        h = rmsnorm(x, pre_mix_g[l]) * (1.0 + sc_m) + sh_m
        proj = jnp.einsum('bsd,de->bse', h, w_in[l])
        out_a = sgu_mixer(proj[..., :2 * A_WIDTH], sgu_norm_g[l], w_spatial[l], b_spatial[l])
        out_b = pool_mixer(proj[..., 2 * A_WIDTH:], w_pool[l], pool_scale[l])
        mixed = jnp.einsum('bse,ed->bsd', jnp.concatenate([out_a, out_b], axis=-1), w_out[l])
        x = x + g_m * rmsnorm(mixed, post_mix_g[l])

        h = rmsnorm(x, pre_ffn_g[l]) * (1.0 + sc_f) + sh_f
        f = conv_ffn(h, w_up[l], conv_w[l], conv_b[l], w_down[l])
        x = x + g_f * rmsnorm(f, post_ffn_g[l])
    return x


import jax as _jax
import jax.numpy as _jnp

TWIN_FORMAT = 'train_step'
FWD_PARAMS = ['x', 'c', 'w_ada', 'b_ada', 'pre_mix_g', 'post_mix_g', 'w_in', 'sgu_norm_g', 'w_spatial', 'b_spatial', 'w_pool', 'pool_scale', 'w_out', 'pre_ffn_g', 'post_ffn_g', 'w_up', 'conv_w', 'conv_b', 'w_down']
TWIN_WEIGHTS = ['w_ada', 'b_ada', 'pre_mix_g', 'post_mix_g', 'w_in', 'sgu_norm_g', 'w_spatial', 'b_spatial', 'w_pool', 'pool_scale', 'w_out', 'pre_ffn_g', 'post_ffn_g', 'w_up', 'conv_w', 'conv_b', 'w_down']
TWIN_DIFF_INPUT = 'x'
TWIN_INPUTS = ['x', 'c', 'w_ada', 'b_ada', 'pre_mix_g', 'post_mix_g', 'w_in', 'sgu_norm_g', 'w_spatial', 'b_spatial', 'w_pool', 'pool_scale', 'w_out', 'pre_ffn_g', 'post_ffn_g', 'w_up', 'conv_w', 'conv_b', 'w_down', 'loss_target', 'm_w_ada', 'm_b_ada', 'm_pre_mix_g', 'm_post_mix_g', 'm_w_in', 'm_sgu_norm_g', 'm_w_spatial', 'm_b_spatial', 'm_w_pool', 'm_pool_scale', 'm_w_out', 'm_pre_ffn_g', 'm_post_ffn_g', 'm_w_up', 'm_conv_w', 'm_conv_b', 'm_w_down', 'v_w_ada', 'v_b_ada', 'v_pre_mix_g', 'v_post_mix_g', 'v_w_in', 'v_sgu_norm_g', 'v_w_spatial', 'v_b_spatial', 'v_w_pool', 'v_pool_scale', 'v_w_out', 'v_pre_ffn_g', 'v_post_ffn_g', 'v_w_up', 'v_conv_w', 'v_conv_b', 'v_w_down']
TWIN_OUTPUTS = ['loss', 'grad_x', 'grad_w_ada', 'grad_b_ada', 'grad_pre_mix_g', 'grad_post_mix_g', 'grad_w_in', 'grad_sgu_norm_g', 'grad_w_spatial', 'grad_b_spatial', 'grad_w_pool', 'grad_pool_scale', 'grad_w_out', 'grad_pre_ffn_g', 'grad_post_ffn_g', 'grad_w_up', 'grad_conv_w', 'grad_conv_b', 'grad_w_down', 'delta_w_ada', 'delta_b_ada', 'delta_pre_mix_g', 'delta_post_mix_g', 'delta_w_in', 'delta_sgu_norm_g', 'delta_w_spatial', 'delta_b_spatial', 'delta_w_pool', 'delta_pool_scale', 'delta_w_out', 'delta_pre_ffn_g', 'delta_post_ffn_g', 'delta_w_up', 'delta_conv_w', 'delta_conv_b', 'delta_w_down', 'new_m_w_ada', 'new_m_b_ada', 'new_m_pre_mix_g', 'new_m_post_mix_g', 'new_m_w_in', 'new_m_sgu_norm_g', 'new_m_w_spatial', 'new_m_b_spatial', 'new_m_w_pool', 'new_m_pool_scale', 'new_m_w_out', 'new_m_pre_ffn_g', 'new_m_post_ffn_g', 'new_m_w_up', 'new_m_conv_w', 'new_m_conv_b', 'new_m_w_down', 'new_v_w_ada', 'new_v_b_ada', 'new_v_pre_mix_g', 'new_v_post_mix_g', 'new_v_w_in', 'new_v_sgu_norm_g', 'new_v_w_spatial', 'new_v_b_spatial', 'new_v_w_pool', 'new_v_pool_scale', 'new_v_w_out', 'new_v_pre_ffn_g', 'new_v_post_ffn_g', 'new_v_w_up', 'new_v_conv_w', 'new_v_conv_b', 'new_v_w_down']
TWIN_LEAF_KINDS = {'loss': 'loss', 'grad_x': 'grad_x', 'grad_w_ada': 'grad_w', 'grad_b_ada': 'grad_w', 'grad_pre_mix_g': 'grad_w', 'grad_post_mix_g': 'grad_w', 'grad_w_in': 'grad_w', 'grad_sgu_norm_g': 'grad_w', 'grad_w_spatial': 'grad_w', 'grad_b_spatial': 'grad_w', 'grad_w_pool': 'grad_w', 'grad_pool_scale': 'grad_w', 'grad_w_out': 'grad_w', 'grad_pre_ffn_g': 'grad_w', 'grad_post_ffn_g': 'grad_w', 'grad_w_up': 'grad_w', 'grad_conv_w': 'grad_w', 'grad_conv_b': 'grad_w', 'grad_w_down': 'grad_w', 'delta_w_ada': 'delta_w', 'delta_b_ada': 'delta_w', 'delta_pre_mix_g': 'delta_w', 'delta_post_mix_g': 'delta_w', 'delta_w_in': 'delta_w', 'delta_sgu_norm_g': 'delta_w', 'delta_w_spatial': 'delta_w', 'delta_b_spatial': 'delta_w', 'delta_w_pool': 'delta_w', 'delta_pool_scale': 'delta_w', 'delta_w_out': 'delta_w', 'delta_pre_ffn_g': 'delta_w', 'delta_post_ffn_g': 'delta_w', 'delta_w_up': 'delta_w', 'delta_conv_w': 'delta_w', 'delta_conv_b': 'delta_w', 'delta_w_down': 'delta_w', 'new_m_w_ada': 'new_m', 'new_m_b_ada': 'new_m', 'new_m_pre_mix_g': 'new_m', 'new_m_post_mix_g': 'new_m', 'new_m_w_in': 'new_m', 'new_m_sgu_norm_g': 'new_m', 'new_m_w_spatial': 'new_m', 'new_m_b_spatial': 'new_m', 'new_m_w_pool': 'new_m', 'new_m_pool_scale': 'new_m', 'new_m_w_out': 'new_m', 'new_m_pre_ffn_g': 'new_m', 'new_m_post_ffn_g': 'new_m', 'new_m_w_up': 'new_m', 'new_m_conv_w': 'new_m', 'new_m_conv_b': 'new_m', 'new_m_w_down': 'new_m', 'new_v_w_ada': 'new_v', 'new_v_b_ada': 'new_v', 'new_v_pre_mix_g': 'new_v', 'new_v_post_mix_g': 'new_v', 'new_v_w_in': 'new_v', 'new_v_sgu_norm_g': 'new_v', 'new_v_w_spatial': 'new_v', 'new_v_b_spatial': 'new_v', 'new_v_w_pool': 'new_v', 'new_v_pool_scale': 'new_v', 'new_v_w_out': 'new_v', 'new_v_pre_ffn_g': 'new_v', 'new_v_post_ffn_g': 'new_v', 'new_v_w_up': 'new_v', 'new_v_conv_w': 'new_v', 'new_v_conv_b': 'new_v', 'new_v_w_down': 'new_v'}


def _forward(args):
    return _fwd_reference(*[args[k] for k in FWD_PARAMS])


def _output_shape():
    def fwd():
        inp = _fwd_setup_inputs(0)
        return _fwd_reference(*[inp[k] for k in FWD_PARAMS])
    out = _jax.eval_shape(fwd)
    return out.shape, out.dtype

N_MICROBATCH = 1
ADAM_LR = 0.001
ADAM_B1 = 0.9
ADAM_B2 = 0.999
ADAM_EPS = 1e-08
ADAM_WD = 0.01
ADAM_STEP = 10
PER_EXAMPLE_BATCH_AXIS = {'x': 0, 'c': 0, 'loss_target': 0}
SHARED_INPUTS = []
_WEIGHT_DTYPES = {'w_ada': _jnp.float32, 'b_ada': _jnp.float32, 'pre_mix_g': _jnp.float32, 'post_mix_g': _jnp.float32, 'w_in': _jnp.float32, 'sgu_norm_g': _jnp.float32, 'w_spatial': _jnp.float32, 'b_spatial': _jnp.float32, 'w_pool': _jnp.float32, 'pool_scale': _jnp.float32, 'w_out': _jnp.float32, 'pre_ffn_g': _jnp.float32, 'post_ffn_g': _jnp.float32, 'w_up': _jnp.float32, 'conv_w': _jnp.float32, 'conv_b': _jnp.float32, 'w_down': _jnp.float32}
MOMENT_SCALE = {'w_ada': 5.113276e+00, 'b_ada': 1.112254e+01, 'pre_mix_g': 4.376612e-01, 'post_mix_g': 2.735836e+01, 'w_in': 7.020674e-01, 'sgu_norm_g': 4.891001e-01, 'w_spatial': 2.403004e-01, 'b_spatial': 2.834328e-01, 'w_pool': 5.119008e-01, 'pool_scale': 7.175942e-01, 'w_out': 1.507744e+00, 'pre_ffn_g': 4.848869e-01, 'post_ffn_g': 2.675827e+01, 'w_up': 5.073875e-01, 'conv_w': 5.375699e-01, 'conv_b': 8.637777e-01, 'w_down': 1.043744e+00}


def _to_microbatches(a, axis):
    t = _jnp.moveaxis(a, axis, 0)
    t = t.reshape((N_MICROBATCH, t.shape[0] // N_MICROBATCH) + t.shape[1:])
    return _jnp.moveaxis(t, 1, axis + 1)


def setup_inputs(seed: int = 0) -> dict:
    inp = _fwd_setup_inputs(seed)
    key = _jax.random.fold_in(_jax.random.key(seed), 7919)
    shape, _ = _output_shape()
    out = dict(inp)
    out["loss_target"] = _jax.random.normal(_jax.random.fold_in(key, 0), shape, _jnp.float32)
    for i, name in enumerate(TWIN_WEIGHTS):
        w = inp[name].astype(_jnp.float32)
        if MOMENT_SCALE is None:
            s = _jnp.sqrt(_jnp.mean(_jnp.square(w)) + 1e-30)
        else:
            s = MOMENT_SCALE[name]
        km, kv = _jax.random.split(_jax.random.fold_in(key, i + 1))
        out[name] = w
        out["m_" + name] = s * _jax.random.normal(km, w.shape, _jnp.float32)
        out["v_" + name] = (s * s) * _jax.random.uniform(kv, w.shape, _jnp.float32, 0.5, 1.5)
    if N_MICROBATCH > 1:
        for name, axis in PER_EXAMPLE_BATCH_AXIS.items():
            out[name] = _to_microbatches(out[name], axis)
    return {'x': out['x'], 'c': out['c'], 'w_ada': out['w_ada'], 'b_ada': out['b_ada'], 'pre_mix_g': out['pre_mix_g'], 'post_mix_g': out['post_mix_g'], 'w_in': out['w_in'], 'sgu_norm_g': out['sgu_norm_g'], 'w_spatial': out['w_spatial'], 'b_spatial': out['b_spatial'], 'w_pool': out['w_pool'], 'pool_scale': out['pool_scale'], 'w_out': out['w_out'], 'pre_ffn_g': out['pre_ffn_g'], 'post_ffn_g': out['post_ffn_g'], 'w_up': out['w_up'], 'conv_w': out['conv_w'], 'conv_b': out['conv_b'], 'w_down': out['w_down'], 'loss_target': out['loss_target'], 'm_w_ada': out['m_w_ada'], 'm_b_ada': out['m_b_ada'], 'm_pre_mix_g': out['m_pre_mix_g'], 'm_post_mix_g': out['m_post_mix_g'], 'm_w_in': out['m_w_in'], 'm_sgu_norm_g': out['m_sgu_norm_g'], 'm_w_spatial': out['m_w_spatial'], 'm_b_spatial': out['m_b_spatial'], 'm_w_pool': out['m_w_pool'], 'm_pool_scale': out['m_pool_scale'], 'm_w_out': out['m_w_out'], 'm_pre_ffn_g': out['m_pre_ffn_g'], 'm_post_ffn_g': out['m_post_ffn_g'], 'm_w_up': out['m_w_up'], 'm_conv_w': out['m_conv_w'], 'm_conv_b': out['m_conv_b'], 'm_w_down': out['m_w_down'], 'v_w_ada': out['v_w_ada'], 'v_b_ada': out['v_b_ada'], 'v_pre_mix_g': out['v_pre_mix_g'], 'v_post_mix_g': out['v_post_mix_g'], 'v_w_in': out['v_w_in'], 'v_sgu_norm_g': out['v_sgu_norm_g'], 'v_w_spatial': out['v_w_spatial'], 'v_b_spatial': out['v_b_spatial'], 'v_w_pool': out['v_w_pool'], 'v_pool_scale': out['v_pool_scale'], 'v_w_out': out['v_w_out'], 'v_pre_ffn_g': out['v_pre_ffn_g'], 'v_post_ffn_g': out['v_post_ffn_g'], 'v_w_up': out['v_w_up'], 'v_conv_w': out['v_conv_w'], 'v_conv_b': out['v_conv_b'], 'v_w_down': out['v_w_down']}


def _loss(weights, diff, rest, loss_target):
    with _jax.named_scope("forward"):
        args = {**rest, TWIN_DIFF_INPUT: diff, **{k: w.astype(_WEIGHT_DTYPES[k]) for k, w in weights.items()}}
        y = _forward(args)
    with _jax.named_scope("loss_head"):
        err = _jnp.square(y.astype(_jnp.float32) - loss_target)
        return 0.5 * _jnp.sum(_jnp.mean(err, axis=-1)) if err.ndim else 0.5 * err


def _adamw(w, g, m, v):
    m = ADAM_B1 * m + (1.0 - ADAM_B1) * g
    v = ADAM_B2 * v + (1.0 - ADAM_B2) * _jnp.square(g)
    m_hat = m / (1.0 - ADAM_B1 ** ADAM_STEP)
    v_hat = v / (1.0 - ADAM_B2 ** ADAM_STEP)
    delta = -ADAM_LR * (m_hat / (_jnp.sqrt(v_hat) + ADAM_EPS) + ADAM_WD * w)
    return delta, m, v


def reference(x, c, w_ada, b_ada, pre_mix_g, post_mix_g, w_in, sgu_norm_g, w_spatial, b_spatial, w_pool, pool_scale, w_out, pre_ffn_g, post_ffn_g, w_up, conv_w, conv_b, w_down, loss_target, m_w_ada, m_b_ada, m_pre_mix_g, m_post_mix_g, m_w_in, m_sgu_norm_g, m_w_spatial, m_b_spatial, m_w_pool, m_pool_scale, m_w_out, m_pre_ffn_g, m_post_ffn_g, m_w_up, m_conv_w, m_conv_b, m_w_down, v_w_ada, v_b_ada, v_pre_mix_g, v_post_mix_g, v_w_in, v_sgu_norm_g, v_w_spatial, v_b_spatial, v_w_pool, v_pool_scale, v_w_out, v_pre_ffn_g, v_post_ffn_g, v_w_up, v_conv_w, v_conv_b, v_w_down):
    given = dict(x=x, c=c, w_ada=w_ada, b_ada=b_ada, pre_mix_g=pre_mix_g, post_mix_g=post_mix_g, w_in=w_in, sgu_norm_g=sgu_norm_g, w_spatial=w_spatial, b_spatial=b_spatial, w_pool=w_pool, pool_scale=pool_scale, w_out=w_out, pre_ffn_g=pre_ffn_g, post_ffn_g=post_ffn_g, w_up=w_up, conv_w=conv_w, conv_b=conv_b, w_down=w_down, loss_target=loss_target, m_w_ada=m_w_ada, m_b_ada=m_b_ada, m_pre_mix_g=m_pre_mix_g, m_post_mix_g=m_post_mix_g, m_w_in=m_w_in, m_sgu_norm_g=m_sgu_norm_g, m_w_spatial=m_w_spatial, m_b_spatial=m_b_spatial, m_w_pool=m_w_pool, m_pool_scale=m_pool_scale, m_w_out=m_w_out, m_pre_ffn_g=m_pre_ffn_g, m_post_ffn_g=m_post_ffn_g, m_w_up=m_w_up, m_conv_w=m_conv_w, m_conv_b=m_conv_b, m_w_down=m_w_down, v_w_ada=v_w_ada, v_b_ada=v_b_ada, v_pre_mix_g=v_pre_mix_g, v_post_mix_g=v_post_mix_g, v_w_in=v_w_in, v_sgu_norm_g=v_sgu_norm_g, v_w_spatial=v_w_spatial, v_b_spatial=v_b_spatial, v_w_pool=v_w_pool, v_pool_scale=v_pool_scale, v_w_out=v_w_out, v_pre_ffn_g=v_pre_ffn_g, v_post_ffn_g=v_post_ffn_g, v_w_up=v_w_up, v_conv_w=v_conv_w, v_conv_b=v_conv_b, v_w_down=v_w_down)
    weights = {n: given[n] for n in TWIN_WEIGHTS}
    shared = {n: given[n] for n in SHARED_INPUTS}
    per_example = {n: given[n] for n in ['x', 'c']}
    grad_fn = _jax.value_and_grad(_loss, argnums=(0, 1))

    def one_microbatch(ex, loss_target):
        ex = dict(ex)
        diff = ex.pop(TWIN_DIFF_INPUT)
        return grad_fn(weights, diff, {**shared, **ex}, loss_target)

    if N_MICROBATCH == 1:
        loss, (grad_w, grad_x) = one_microbatch(per_example, given["loss_target"])
    else:
        def body(carry, xs):
            loss_sum, grad_sum = carry
            l_k, (gw_k, gx_k) = one_microbatch(xs[0], xs[1])
            with _jax.named_scope("update"):
                return (loss_sum + l_k, _jax.tree.map(_jnp.add, grad_sum, gw_k)), gx_k

        init = (_jnp.zeros((), _jnp.float32), _jax.tree.map(_jnp.zeros_like, weights))
        (loss, grad_w), grad_x = _jax.lax.scan(body, init, (per_example, given["loss_target"]))
    with _jax.named_scope("update"):
        delta_w, new_m, new_v = {}, {}, {}
        for n in TWIN_WEIGHTS:
            delta_w[n], new_m[n], new_v[n] = _adamw(weights[n], grad_w[n], given["m_" + n], given["v_" + n])
    return (loss, grad_x, *[grad_w[n] for n in TWIN_WEIGHTS], *[delta_w[n] for n in TWIN_WEIGHTS],
            *[new_m[n] for n in TWIN_WEIGHTS], *[new_v[n] for n in TWIN_WEIGHTS])
```

```python
import functools

import jax
import jax.numpy as jnp
from jax import lax
from jax.experimental import pallas as pl
from jax.experimental.pallas import tpu as pltpu

F32 = jnp.float32
BF16 = jnp.bfloat16
MESH = pl.DeviceIdType.MESH

EPS = 1e-6
HEAD = 128
N_HEADS = 4
A_WIDTH = N_HEADS * HEAD
CHUNK = 64
POOL_WINDOWS = (2, 4, 8, 16)
POOL_HALO = 16
CONV_HALO = 8

ADAM_LR = 0.001
ADAM_B1 = 0.9
ADAM_B2 = 0.999
ADAM_EPS = 1e-08
ADAM_WD = 0.01
ADAM_STEP = 10

VMEM_LIMIT_BYTES = 58 * 1024 * 1024
N_DEV = 8
N_CHIP = 4


def _dot(a, b):
    return jnp.dot(a, b, preferred_element_type=F32)


def _dot_nt(a, b):
    return lax.dot_general(a, b, (((1,), (1,)), ((), ())), preferred_element_type=F32)


def _dot_tn(a, b):
    return lax.dot_general(a, b, (((0,), (0,)), ((), ())), preferred_element_type=F32)


def _gelu(x):
    return x * (0.5 * (1.0 + jnp.tanh(0.7978845608028654 * (x + 0.044715 * (x * x * x)))))


def _gelu_grad(x):
    t = jnp.tanh(0.7978845608028654 * (x + 0.044715 * (x * x * x)))
    return 0.5 * (1.0 + t) + (0.5 * x) * (1.0 - t * t) * (0.7978845608028654 * (1.0 + 0.134145 * (x * x)))


def _sigmoid(x):
    return 1.0 / (1.0 + jnp.exp(-x))


def _rms(x):
    return lax.rsqrt(jnp.mean(x * x, axis=-1, keepdims=True) + EPS)


def _colsum(x):
    return jnp.sum(x, axis=0, keepdims=True)


def _rowmean(x):
    return jnp.mean(x, axis=-1, keepdims=True)


def _tiled(shape, index_map):
    return pl.BlockSpec(shape, index_map)


def _resident(shape):
    nd = len(shape)
    return pl.BlockSpec(shape, lambda *_: (0,) * nd, pipeline_mode=pl.Buffered(1))


def _whole(shape):
    nd = len(shape)
    return pl.BlockSpec(shape, lambda *_: (0,) * nd)


def _seq_params():
    return pltpu.CompilerParams(dimension_semantics=("arbitrary",), vmem_limit_bytes=VMEM_LIMIT_BYTES)


def _ff_chunks(f):
    out, o = [], 0
    while o < f:
        w = min(768, f - o)
        out.append((o, w))
        o += w
    return out


def _pool_fwd(p, halo, g, t_glob):
    ext = jnp.concatenate([halo, p], axis=0)
    s = ext
    for step in range(g + 1):
        s = s + pltpu.roll(s, 1 << step, 0)
    cnt = jnp.minimum(t_glob + 1, POOL_WINDOWS[g]).astype(F32)
    return s[POOL_HALO:] / cnt - p


def _adamw_math(w, g, m, v):
    m = ADAM_B1 * m + (1.0 - ADAM_B1) * g
    v = ADAM_B2 * v + (1.0 - ADAM_B2) * (g * g)
    m_hat = m / (1.0 - ADAM_B1 ** ADAM_STEP)
    v_hat = v / (1.0 - ADAM_B2 ** ADAM_STEP)
    delta = -ADAM_LR * (m_hat / (jnp.sqrt(v_hat) + ADAM_EPS) + ADAM_WD * w)
    return delta, m, v


def _mesh_pos():
    return lax.axis_index("x"), lax.axis_index("y"), lax.axis_index("c")


def _allgather_small(arrs, name):
    n = len(arrs)

    def body(*refs):
        ins, outs = refs[:n], refs[n:2 * n]
        send_sems, recv_sems, local_sems = refs[2 * n:]
        x, y, c = _mesh_pos()
        me, sibling = (x, y, c), (x, y, 1 - c)
        chips = [(1 - x, y), (x, 1 - y), (1 - x, 1 - y)]

        def rows(a, px, py, pc):
            r = ins[a].shape[0]
            return outs[a].at[pl.ds(pl.multiple_of((4 * px + 2 * py + pc) * r, 8), r), :]

        def copy(a, k, block, to, src=None):
            return pltpu.make_async_remote_copy(
                src_ref=rows(a, *block) if src is None else src, dst_ref=rows(a, *block),
                send_sem=send_sems.at[a * 7 + k], recv_sem=recv_sems.at[a * 7 + k],
                device_id=to, device_id_type=MESH)

        started, local = [], []
        for a in range(n):
            mine = pltpu.make_async_copy(ins[a], rows(a, *me), local_sems.at[a])
            mine.start()
            local.append(mine)
            first = [copy(a, 0, me, sibling, src=ins[a])]
            first += [copy(a, 1 + j, me, (*chip, c), src=ins[a]) for j, chip in enumerate(chips)]
            for cp in first:
                cp.start()
            started += first
        for a in range(n):
            for j, chip in enumerate(chips):
                copy(a, 1 + j, (*chip, c), me).wait_recv()
                fwd = copy(a, 4 + j, (*chip, c), sibling)
                fwd.start()
                started.append(fwd)
        for a in range(n):
            copy(a, 0, sibling, me).wait_recv()
            for j, chip in enumerate(chips):
                copy(a, 4 + j, (*chip, 1 - c), me).wait_recv()
        for cp in started:
            cp.wait_send()
        for mine in local:
            mine.wait()

    vm = pl.BlockSpec(memory_space=pltpu.VMEM)
    return pl.pallas_call(
        body, name=name,
        out_shape=[jax.ShapeDtypeStruct((N_DEV * a.shape[0], a.shape[1]), a.dtype) for a in arrs],
        in_specs=[vm] * n, out_specs=[vm] * n,
        scratch_shapes=[pltpu.SemaphoreType.DMA((7 * n,)), pltpu.SemaphoreType.DMA((7 * n,)),
                        pltpu.SemaphoreType.DMA((n,))],
        compiler_params=pltpu.CompilerParams(vmem_limit_bytes=VMEM_LIMIT_BYTES),
    )(*arrs)


def _piece(ref, col_sharded, k, h):
    m, n = ref.shape
    if col_sharded:
        mh, nc = m // 2, n // N_CHIP
        return ref.at[pl.ds(pl.multiple_of(h * mh, 16), mh), pl.ds(pl.multiple_of(k * nc, 128), nc)]
    rp = m // (2 * N_CHIP)
    return ref.at[pl.ds(pl.multiple_of((2 * k + h) * rp, 16), rp), :]


def _piece_shape(shape, col_sharded):
    m, n = shape
    return (m // 2, n // N_CHIP) if col_sharded else (m // (2 * N_CHIP), n)


def _gather_weights(shards, col_flags):
    n = len(shards)
    full_shapes = []
    for s, col in zip(shards, col_flags):
        full_shapes.append((s.shape[0], s.shape[1] * N_CHIP) if col else (s.shape[0] * N_CHIP, s.shape[1]))

    def body(*refs):
        ins, outs, stage = refs[:n], refs[n:2 * n], refs[2 * n:3 * n]
        send_sems, recv_sems, local_sems = refs[3 * n:]
        x, y, c = _mesh_pos()
        sibling = (x, y, 1 - c)
        chips = [(1 - x, y), (x, 1 - y), (1 - x, 1 - y)]
        k_me = 2 * x + y

        def half_of_stage(a, h):
            rows = stage[a].shape[0] // 2
            return stage[a].at[pl.ds(pl.multiple_of(h * rows, 16), rows), :]

        def own_window(a):
            m, nn = stage[a].shape
            if col_flags[a]:
                return outs[a].at[:, pl.ds(pl.multiple_of(k_me * nn, 128), nn)]
            return outs[a].at[pl.ds(pl.multiple_of(k_me * m, 16), m), :]

        def copy(a, s, src, k, h, to):
            return pltpu.make_async_remote_copy(
                src_ref=src, dst_ref=_piece(outs[a], col_flags[a], k, h),
                send_sem=send_sems.at[a * 6 + s], recv_sem=recv_sems.at[a * 6 + s],
                device_id=to, device_id_type=MESH)

        started, local = [], []
        for a in range(n):
            stage[a][...] = ins[a][...].astype(BF16)
            mine = pltpu.make_async_copy(stage[a], own_window(a), local_sems.at[a])
            mine.start()
            local.append(mine)
            for j, chip in enumerate(chips):
                cp = copy(a, j, half_of_stage(a, c), k_me, c, (*chip, c))
                cp.start()
                started.append(cp)
        for a in range(n):
            for j, chip in enumerate(chips):
                k_from = 2 * chip[0] + chip[1]
                win = _piece(outs[a], col_flags[a], k_from, c)
                copy(a, j, win, k_from, c, (*chip, c)).wait_recv()
                fwd = copy(a, 3 + j, win, k_from, c, sibling)
                fwd.start()
                started.append(fwd)
        for a in range(n):
            for j, chip in enumerate(chips):
                k_from = 2 * chip[0] + chip[1]
                win = _piece(outs[a], col_flags[a], k_from, 1 - c)
                copy(a, 3 + j, win, k_from, 1 - c, sibling).wait_recv()
        for cp in started:
            cp.wait_send()
        for mine in local:
            mine.wait()

    return pl.pallas_call(
        body, name="gather_weights",
        out_shape=[jax.ShapeDtypeStruct(fs, BF16) for fs in full_shapes],
        in_specs=[pl.BlockSpec(memory_space=pltpu.VMEM)] * n,
        out_specs=[pl.BlockSpec(memory_space=pl.ANY)] * n,
        scratch_shapes=[pltpu.VMEM(s.shape, BF16) for s in shards]
        + [pltpu.SemaphoreType.DMA((6 * n,)), pltpu.SemaphoreType.DMA((6 * n,)), pltpu.SemaphoreType.DMA((n,))],
        compiler_params=pltpu.CompilerParams(vmem_limit_bytes=VMEM_LIMIT_BYTES),
    )(*shards)


def _reduce_to_sibling(grads, col_flags):
    n = len(grads)
    pshapes = [_piece_shape(g.shape, col) for g, col in zip(grads, col_flags)]

    def body(*refs):
        ins, outs = refs[:n], refs[n:2 * n]
        send_sems, recv_sems = refs[2 * n:]
        x, y, c = _mesh_pos()
        sibling = (x, y, 1 - c)
        copies = []
        for a in range(n):
            for k in range(N_CHIP):
                cp = pltpu.make_async_remote_copy(
                    src_ref=_piece(ins[a], col_flags[a], k, 1 - c), dst_ref=outs[a].at[k],
                    send_sem=send_sems.at[a * N_CHIP + k], recv_sem=recv_sems.at[a * N_CHIP + k],
                    device_id=sibling, device_id_type=MESH)
                cp.start()
                copies.append(cp)
        for cp in copies:
            cp.wait_recv()
        for cp in copies:
            cp.wait_send()

    anyspec = pl.BlockSpec(memory_space=pl.ANY)
    return pl.pallas_call(
        body, name="reduce_to_sibling",
        out_shape=[jax.ShapeDtypeStruct((N_CHIP,) + ps, F32) for ps in pshapes],
        in_specs=[anyspec] * n, out_specs=[anyspec] * n,
        scratch_shapes=[pltpu.SemaphoreType.DMA((N_CHIP * n,)), pltpu.SemaphoreType.DMA((N_CHIP * n,))],
    )(*grads)


def _sum_with_sibling(grads, landed, col_flags, c_idx):
    n = len(grads)
    pshapes = [_piece_shape(g.shape, col) for g, col in zip(grads, col_flags)]

    def body(c_ref, *refs):
        ins, lands, outs = refs[:n], refs[n:2 * n], refs[2 * n:]
        for a in range(n):
            outs[a][0] = ins[a][...] + lands[a][0]

    in_specs = []
    for ps, col in zip(pshapes, col_flags):
        if col:
            in_specs.append(pl.BlockSpec(ps, lambda k, c_ref: (c_ref[0], k)))
        else:
            in_specs.append(pl.BlockSpec(ps, lambda k, c_ref: (2 * k + c_ref[0], 0)))
    land_specs = [pl.BlockSpec((1,) + ps, lambda k, c_ref: (k, 0, 0)) for ps in pshapes]
    return pl.pallas_call(
        body, name="sum_with_sibling",
        grid_spec=pltpu.PrefetchScalarGridSpec(
            num_scalar_prefetch=1, grid=(N_CHIP,),
            in_specs=in_specs + land_specs, out_specs=land_specs),
        out_shape=[jax.ShapeDtypeStruct((N_CHIP,) + ps, F32) for ps in pshapes],
        compiler_params=pltpu.CompilerParams(dimension_semantics=("arbitrary",), vmem_limit_bytes=VMEM_LIMIT_BYTES),
    )(c_idx, *grads, *landed)


def _reduce_over_chips(parts):
    n = len(parts)

    def body(*refs):
        ins, outs = refs[:n], refs[n:2 * n]
        send_sems, recv_sems, local_sems = refs[2 * n:]
        x, y, c = _mesh_pos()
        chips = [(1 - x, y), (x, 1 - y), (1 - x, 1 - y)]
        k_me = 2 * x + y
        copies, local = [], []
        for a in range(n):
            mine = pltpu.make_async_copy(ins[a].at[k_me], outs[a].at[k_me], local_sems.at[a])
            mine.start()
            local.append(mine)
            for j, chip in enumerate(chips):
                k_to = 2 * chip[0] + chip[1]
                cp = pltpu.make_async_remote_copy(
                    src_ref=ins[a].at[k_to], dst_ref=outs[a].at[k_me],
                    send_sem=send_sems.at[a * 3 + j], recv_sem=recv_sems.at[a * 3 + j],
                    device_id=(*chip, c), device_id_type=MESH)
                cp.start()
                copies.append((cp, a, j, chip))
        for cp, a, j, chip in copies:
            k_from = 2 * chip[0] + chip[1]
            pltpu.make_async_remote_copy(
                src_ref=ins[a].at[k_from], dst_ref=outs[a].at[k_from],
                send_sem=send_sems.at[a * 3 + j], recv_sem=recv_sems.at[a * 3 + j],
                device_id=(*chip, c), device_id_type=MESH).wait_recv()
        for cp, _, _, _ in copies:
            cp.wait_send()
        for mine in local:
            mine.wait()

    anyspec = pl.BlockSpec(memory_space=pl.ANY)
    return pl.pallas_call(
        body, name="reduce_over_chips",
        out_shape=[jax.ShapeDtypeStruct(p.shape, F32) for p in parts],
        in_specs=[anyspec] * n, out_specs=[anyspec] * n,
        scratch_shapes=[pltpu.SemaphoreType.DMA((3 * n,)), pltpu.SemaphoreType.DMA((3 * n,)),
                        pltpu.SemaphoreType.DMA((n,))],
    )(*parts)


def _sum_chips_and_share(landed):
    n = len(landed)

    def body(*refs):
        ins, outs, red = refs[:n], refs[n:2 * n], refs[2 * n:3 * n]
        send_sems, recv_sems, local_sems = refs[3 * n:]
        x, y, c = _mesh_pos()
        sibling = (x, y, 1 - c)
        copies, local = [], []
        for a in range(n):
            red[a][...] = ((ins[a][0] + ins[a][1]) + ins[a][2]) + ins[a][3]
            mine = pltpu.make_async_copy(red[a], outs[a].at[c], local_sems.at[a])
            mine.start()
            local.append(mine)
            cp = pltpu.make_async_remote_copy(
                src_ref=red[a], dst_ref=outs[a].at[c],
                send_sem=send_sems.at[a], recv_sem=recv_sems.at[a],
                device_id=sibling, device_id_type=MESH)
            cp.start()
            copies.append(cp)
        for a in range(n):
            pltpu.make_async_remote_copy(
                src_ref=red[a], dst_ref=outs[a].at[1 - c],
                send_sem=send_sems.at[a], recv_sem=recv_sems.at[a],
                device_id=sibling, device_id_type=MESH).wait_recv()
        for cp in copies:
            cp.wait_send()
        for mine in local:
            mine.wait()

    return pl.pallas_call(
        body, name="sum_chips_and_share",
        out_shape=[jax.ShapeDtypeStruct((2,) + l.shape[1:], F32) for l in landed],
        in_specs=[pl.BlockSpec(memory_space=pltpu.VMEM)] * n,
        out_specs=[pl.BlockSpec(memory_space=pl.ANY)] * n,
        scratch_shapes=[pltpu.VMEM(l.shape[1:], F32) for l in landed]
        + [pltpu.SemaphoreType.DMA((n,)), pltpu.SemaphoreType.DMA((n,)), pltpu.SemaphoreType.DMA((n,))],
        compiler_params=pltpu.CompilerParams(vmem_limit_bytes=VMEM_LIMIT_BYTES),
    )(*landed)


def _mod_shard(c_all, w_ada, b_shard):
    def body(c_ref, w_ref, b_ref, o_ref, sc_ref):
        cc = c_ref[...]
        sc = cc * _sigmoid(cc)
        sc_ref[...] = sc
        o_ref[...] = _dot(sc, w_ref[...]) + b_ref[...]

    nb, d = c_all.shape
    nn = w_ada.shape[1]
    return pl.pallas_call(
        body, name="mod_shard",
        out_shape=[jax.ShapeDtypeStruct((nb, nn), F32), jax.ShapeDtypeStruct((nb, d), F32)],
        compiler_params=pltpu.CompilerParams(vmem_limit_bytes=VMEM_LIMIT_BYTES),
    )(c_all, w_ada, b_shard)


V_SH_M, V_SC_M, V_G_M, V_SH_F, V_SC_F, V_G_F, V_PRE_MIX, V_POST_MIX, V_PRE_FFN, V_POST_FFN = range(10)


def _vrow(vec_ref, r):
    return vec_ref[r:r + 1, :]


def _mix_fwd(x, vecs, w_in_b, w_out_b, sgu_g, wm_b, bsb, wp_b, ps, ts):
    s_len, d = x.shape
    nt = s_len // ts
    nblk = ts // HEAD
    n_proj = w_in_b.shape[1]

    def body(x_ref, vec_ref, win_ref, wout_ref, sg_ref, wm_ref, bs_ref, wp_ref, ps_ref,
             h1_ref, proj_ref, cat_ref, mixed_ref, x2_ref, h2_ref, carry_ref):
        i = pl.program_id(0)

        @pl.when(i == 0)
        def _():
            carry_ref[...] = jnp.zeros_like(carry_ref)

        x = x_ref[...]
        h1 = (((x * _rms(x)) * _vrow(vec_ref, V_PRE_MIX)) * (1.0 + _vrow(vec_ref, V_SC_M))
              + _vrow(vec_ref, V_SH_M)).astype(BF16)
        h1_ref[...] = h1
        proj = _dot(h1, win_ref[...])
        proj_ref[...] = proj.astype(BF16)
        t_glob = lax.broadcasted_iota(jnp.int32, (ts, HEAD), 0) + i * ts
        for h in range(N_HEADS):
            u = _gelu(proj[:, h * HEAD:(h + 1) * HEAD])
            v = _gelu(proj[:, A_WIDTH + h * HEAD:A_WIDTH + (h + 1) * HEAD])
            vn = ((v * _rms(v)) * sg_ref[h:h + 1, :]).astype(BF16)
            for b in range(nblk):
                rs = slice(b * HEAD, (b + 1) * HEAD)
                z = _dot(wm_ref[h], vn[rs]) + bs_ref[h]
                cat_ref[rs, h * HEAD:(h + 1) * HEAD] = (u[rs] * z).astype(BF16)
        for g in range(len(POOL_WINDOWS)):
            gs = slice(g * HEAD, (g + 1) * HEAD)
            p = proj[:, 2 * A_WIDTH + g * HEAD:2 * A_WIDTH + (g + 1) * HEAD]
            pooled = _pool_fwd(p, carry_ref[:, gs], g, t_glob)
            yb = _dot(pooled.astype(BF16), wp_ref[g]) * ps_ref[0:1, gs]
            cat_ref[:, A_WIDTH + g * HEAD:A_WIDTH + (g + 1) * HEAD] = yb.astype(BF16)
        carry_ref[...] = proj[ts - POOL_HALO:ts, 2 * A_WIDTH:]
        mixed = _dot(cat_ref[...], wout_ref[...])
        mixed_ref[...] = mixed
        x2 = x + _vrow(vec_ref, V_G_M) * ((mixed * _rms(mixed)) * _vrow(vec_ref, V_POST_MIX))
        x2_ref[...] = x2
        h2_ref[...] = (((x2 * _rms(x2)) * _vrow(vec_ref, V_PRE_FFN)) * (1.0 + _vrow(vec_ref, V_SC_F))
                       + _vrow(vec_ref, V_SH_F)).astype(BF16)

    row = lambda i: (i, 0)
    return pl.pallas_call(
        body, name="mix_fwd", grid=(nt,),
        in_specs=[_tiled((ts, d), row), _whole(vecs.shape), _resident(w_in_b.shape), _resident(w_out_b.shape),
                  _whole(sgu_g.shape), _whole(wm_b.shape), _whole(bsb.shape), _whole(wp_b.shape), _whole(ps.shape)],
        out_specs=[_tiled((ts, d), row), _tiled((ts, n_proj), row), _tiled((ts, d), row),
                   _tiled((ts, d), row), _tiled((ts, d), row), _tiled((ts, d), row)],
        out_shape=[jax.ShapeDtypeStruct((s_len, d), BF16), jax.ShapeDtypeStruct((s_len, n_proj), BF16),
                   jax.ShapeDtypeStruct((s_len, d), BF16), jax.ShapeDtypeStruct((s_len, d), F32),
                   jax.ShapeDtypeStruct((s_len, d), F32), jax.ShapeDtypeStruct((s_len, d), BF16)],
        scratch_shapes=[pltpu.VMEM((POOL_HALO, A_WIDTH), F32)],
        compiler_params=_seq_params(),
    )(x, vecs, w_in_b, w_out_b, sgu_g, wm_b, bsb, wp_b, ps)


def _conv_taps(u, halo):
    ext = jnp.concatenate([halo, u], axis=0)
    return pltpu.roll(ext, 2, 0)[CONV_HALO:], pltpu.roll(ext, 1, 0)[CONV_HALO:]


def _conv_out(u, um2, um1, cv_ref, cols):
    return (cv_ref[3:4, cols] + um2 * cv_ref[0:1, cols] + um1 * cv_ref[1:2, cols] + u * cv_ref[2:3, cols])


def _ffn_up(h2, w_up_b, cvec, ts):
    s_len, d = h2.shape
    ff2 = w_up_b.shape[1]
    ff = ff2 // 2
    nt = s_len // ts
    chunks = _ff_chunks(ff)

    def body(h2_ref, w_ref, cv_ref, up_ref, act_ref, carry_ref):
        i = pl.program_id(0)

        @pl.when(i == 0)
        def _():
            carry_ref[...] = jnp.zeros_like(carry_ref)

        h2v = h2_ref[...]
        for o, w in chunks:
            ys = []
            for base in (0, ff):
                cols = slice(base + o, base + o + w)
                u = _dot(h2v, w_ref[:, cols])
                up_ref[:, cols] = u.astype(BF16)
                um2, um1 = _conv_taps(u, carry_ref[:, cols])
                ys.append(_conv_out(u, um2, um1, cv_ref, cols))
                carry_ref[:, cols] = u[ts - CONV_HALO:ts]
            gate, val = ys
            act_ref[:, o:o + w] = ((gate * _sigmoid(gate)) * val).astype(BF16)

    row = lambda i: (i, 0)
    return pl.pallas_call(
        body, name="ffn_up", grid=(nt,),
        in_specs=[_tiled((ts, d), row), _resident(w_up_b.shape), _whole(cvec.shape)],
        out_specs=[_tiled((ts, ff2), row), _tiled((ts, ff), row)],
        out_shape=[jax.ShapeDtypeStruct((s_len, ff2), BF16), jax.ShapeDtypeStruct((s_len, ff), BF16)],
        scratch_shapes=[pltpu.VMEM((CONV_HALO, ff2), F32)],
        compiler_params=_seq_params(),
    )(h2, w_up_b, cvec)


F_LOSS, F_DGF, F_DPOSTFFN = 0, 1, 2
B_DSHF, B_DSCF, B_DPREFFN, B_DGM, B_DPOSTMIX = 0, 1, 2, 3, 4
M_DSHM, M_DSCM, M_DPREMIX = 0, 1, 2
C_DCB, C_DCW = 0, 1


def _ffn_down(act, w_down_b, x2, tgt, vecs, ts):
    s_len, ff = act.shape
    d = x2.shape[1]
    nt = s_len // ts

    def body(act_ref, w_ref, x2_ref, t_ref, vec_ref, dy_ref, df_ref, acc_ref):
        @pl.when(pl.program_id(0) == 0)
        def _():
            acc_ref[...] = jnp.zeros_like(acc_ref)

        f = _dot(act_ref[...], w_ref[...])
        r3 = _rms(f)
        fhat = f * r3
        post = _vrow(vec_ref, V_POST_FFN)
        g_f = _vrow(vec_ref, V_G_F)
        fn = fhat * post
        e = (x2_ref[...] + g_f * fn) - t_ref[...]
        dy = e * (1.0 / d)
        dy_ref[...] = dy
        dfn = dy * g_f
        acc_ref[F_LOSS:F_LOSS + 1, :] += _colsum(e * e)
        acc_ref[F_DGF:F_DGF + 1, :] += _colsum(dy * fn)
        acc_ref[F_DPOSTFFN:F_DPOSTFFN + 1, :] += _colsum(dfn * fhat)
        dfhat = dfn * post
        df_ref[...] = (r3 * (dfhat - fhat * _rowmean(dfhat * fhat))).astype(BF16)

    row = lambda i: (i, 0)
    return pl.pallas_call(
        body, name="ffn_down", grid=(nt,),
        in_specs=[_tiled((ts, ff), row), _resident(w_down_b.shape), _tiled((ts, d), row), _tiled((ts, d), row),
                  _whole(vecs.shape)],
        out_specs=[_tiled((ts, d), row), _tiled((ts, d), row), _whole((8, d))],
        out_shape=[jax.ShapeDtypeStruct((s_len, d), F32), jax.ShapeDtypeStruct((s_len, d), BF16),
                   jax.ShapeDtypeStruct((8, d), F32)],
        compiler_params=_seq_params(),
    )(act, w_down_b, x2, tgt, vecs)


def _ffn_bwd_a(df, up, w_down_b, cvec, ts):
    s_len, d = df.shape
    ff2 = up.shape[1]
    ff = ff2 // 2
    nt = s_len // ts
    chunks = _ff_chunks(ff)
    halo_rows = 16
    per = ts // halo_rows
    ext_len = ts + CONV_HALO

    def body(df_ref, up_ref, uph_ref, w_ref, cv_ref, dup_ref, acc_ref, carry_ref):
        i = pl.program_id(0)
        tile = nt - 1 - i

        @pl.when(i == 0)
        def _():
            carry_ref[...] = jnp.zeros_like(carry_ref)
            acc_ref[...] = jnp.zeros_like(acc_ref)

        dfv = df_ref[...]
        for o, w in chunks:
            dact = _dot_nt(dfv, w_ref[o:o + w, :])
            ys, taps = [], []
            for base in (0, ff):
                cols = slice(base + o, base + o + w)
                u = up_ref[:, cols].astype(F32)
                halo = uph_ref[:, cols].astype(F32)[halo_rows - CONV_HALO:]
                halo = jnp.where(tile > 0, halo, 0.0)
                um2, um1 = _conv_taps(u, halo)
                ys.append(_conv_out(u, um2, um1, cv_ref, cols))
                taps.append((um2, um1, u))
            gate, val = ys
            sg = _sigmoid(gate)
            dys = (dact * val * (sg * (1.0 + gate * (1.0 - sg))), dact * (gate * sg))
            for base, dyv, (um2, um1, u) in zip((0, ff), dys, taps):
                cols = slice(base + o, base + o + w)
                acc_ref[C_DCB:C_DCB + 1, cols] += _colsum(dyv)
                acc_ref[C_DCW + 0:C_DCW + 1, cols] += _colsum(dyv * um2)
                acc_ref[C_DCW + 1:C_DCW + 2, cols] += _colsum(dyv * um1)
                acc_ref[C_DCW + 2:C_DCW + 3, cols] += _colsum(dyv * u)
                ext = jnp.concatenate([dyv, carry_ref[:, cols]], axis=0)
                dup = (dyv * cv_ref[2:3, cols]
                       + pltpu.roll(ext, ext_len - 1, 0)[:ts] * cv_ref[1:2, cols]
                       + pltpu.roll(ext, ext_len - 2, 0)[:ts] * cv_ref[0:1, cols])
                dup_ref[:, cols] = dup.astype(BF16)
                carry_ref[:, cols] = dyv[0:CONV_HALO]

    rev = lambda i: (nt - 1 - i, 0)
    halo_map = lambda i: (jnp.maximum((nt - 1 - i) * per - 1, 0), 0)
    return pl.pallas_call(
        body, name="ffn_bwd_a", grid=(nt,),
        in_specs=[_tiled((ts, d), rev), _tiled((ts, ff2), rev), _tiled((halo_rows, ff2), halo_map),
                  _resident(w_down_b.shape), _whole(cvec.shape)],
        out_specs=[_tiled((ts, ff2), rev), _whole((8, ff2))],
        out_shape=[jax.ShapeDtypeStruct((s_len, ff2), BF16), jax.ShapeDtypeStruct((8, ff2), F32)],
        scratch_shapes=[pltpu.VMEM((CONV_HALO, ff2), F32)],
        compiler_params=_seq_params(),
    )(df, up, up, w_down_b, cvec)


def _ffn_bwd_b(dup, w_up_b, x2, dy, mixed, vecs, ts):
    s_len, ff2 = dup.shape
    d = x2.shape[1]
    nt = s_len // ts

    def body(dup_ref, w_ref, x2_ref, dy_ref, mx_ref, vec_ref, dx2_ref, dmx_ref, acc_ref):
        @pl.when(pl.program_id(0) == 0)
        def _():
            acc_ref[...] = jnp.zeros_like(acc_ref)

        dh2 = _dot_nt(dup_ref[...], w_ref[...])
        x2 = x2_ref[...]
        r2 = _rms(x2)
        xn = x2 * r2
        pre = _vrow(vec_ref, V_PRE_FFN)
        one_sc = 1.0 + _vrow(vec_ref, V_SC_F)
        acc_ref[B_DSHF:B_DSHF + 1, :] += _colsum(dh2)
        acc_ref[B_DSCF:B_DSCF + 1, :] += _colsum(dh2 * (xn * pre))
        acc_ref[B_DPREFFN:B_DPREFFN + 1, :] += _colsum(dh2 * xn * one_sc)
        dxn = dh2 * pre * one_sc
        dx2 = dy_ref[...] + r2 * (dxn - xn * _rowmean(dxn * xn))
        dx2_ref[...] = dx2
        mixed = mx_ref[...]
        rm = _rms(mixed)
        mhat = mixed * rm
        post = _vrow(vec_ref, V_POST_MIX)
        acc_ref[B_DGM:B_DGM + 1, :] += _colsum(dx2 * (mhat * post))
        dmn = dx2 * _vrow(vec_ref, V_G_M)
        acc_ref[B_DPOSTMIX:B_DPOSTMIX + 1, :] += _colsum(dmn * mhat)
        dmhat = dmn * post
        dmx_ref[...] = (rm * (dmhat - mhat * _rowmean(dmhat * mhat))).astype(BF16)

    row = lambda i: (i, 0)
    return pl.pallas_call(
        body, name="ffn_bwd_b", grid=(nt,),
        in_specs=[_tiled((ts, ff2), row), _resident(w_up_b.shape), _tiled((ts, d), row), _tiled((ts, d), row),
                  _tiled((ts, d), row), _whole(vecs.shape)],
        out_specs=[_tiled((ts, d), row), _tiled((ts, d), row), _whole((8, d))],
        out_shape=[jax.ShapeDtypeStruct((s_len, d), F32), jax.ShapeDtypeStruct((s_len, d), BF16),
                   jax.ShapeDtypeStruct((8, d), F32)],
        compiler_params=_seq_params(),
    )(dup, w_up_b, x2, dy, mixed, vecs)


def _mix_bwd(dmixed, dx2, x, proj, vecs, w_in_b, w_out_b, sgu_g, wm_b, wmt_b, bsb, wp_b, ps, ts):
    s_len, d = x.shape
    nt = s_len // ts
    nblk = ts // HEAD
    n_proj = proj.shape[1]
    per = ts // POOL_HALO
    ext_len = ts + POOL_HALO

    def body(dmx_ref, dx2_ref, x_ref, proj_ref, projh_ref, vec_ref, win_ref, wout_ref, sg_ref, wm_ref, wmt_ref,
             bs_ref, wp_ref, ps_ref,
             gx_ref, dproj_ref, acc_ref, dwm_ref, dwp_ref, dbz_ref, dg_ref, dps_ref, carry_ref):
        i = pl.program_id(0)
        tile = nt - 1 - i

        @pl.when(i == 0)
        def _():
            carry_ref[...] = jnp.zeros_like(carry_ref)
            for r in (acc_ref, dwm_ref, dwp_ref, dbz_ref, dg_ref, dps_ref):
                r[...] = jnp.zeros_like(r)

        dcat = _dot_nt(dmx_ref[...], wout_ref[...])
        t_glob = lax.broadcasted_iota(jnp.int32, (ts, HEAD), 0) + tile * ts
        for h in range(N_HEADS):
            hs = slice(h * HEAD, (h + 1) * HEAD)
            vs = slice(A_WIDTH + h * HEAD, A_WIDTH + (h + 1) * HEAD)
            au = proj_ref[:, hs].astype(F32)
            av = proj_ref[:, vs].astype(F32)
            u = _gelu(au)
            v = _gelu(av)
            rv = _rms(v)
            vhat = v * rv
            gain = sg_ref[h:h + 1, :]
            vn = (vhat * gain).astype(BF16)
            dout = dcat[:, hs]
            du_parts, dvn_parts = [], []
            for b in range(nblk):
                rs = slice(b * HEAD, (b + 1) * HEAD)
                z = _dot(wm_ref[h], vn[rs]) + bs_ref[h]
                du_parts.append(dout[rs] * z)
                dz = dout[rs] * u[rs]
                dbz_ref[h] += dz
                dzb = dz.astype(BF16)
                dwm_ref[h] += _dot_nt(dzb, vn[rs])
                dvn_parts.append(_dot(wmt_ref[h], dzb))
            du = jnp.concatenate(du_parts, axis=0)
            dvn = jnp.concatenate(dvn_parts, axis=0)
            dg_ref[h:h + 1, :] += _colsum(dvn * vhat)
            dvhat = dvn * gain
            dv = rv * (dvhat - vhat * _rowmean(dvhat * vhat))
            dproj_ref[:, hs] = (du * _gelu_grad(au)).astype(BF16)
            dproj_ref[:, vs] = (dv * _gelu_grad(av)).astype(BF16)
        for g in range(len(POOL_WINDOWS)):
            gs = slice(g * HEAD, (g + 1) * HEAD)
            pcols = slice(2 * A_WIDTH + g * HEAD, 2 * A_WIDTH + (g + 1) * HEAD)
            p = proj_ref[:, pcols].astype(F32)
            halo = jnp.where(tile > 0, projh_ref[:, pcols].astype(F32), 0.0)
            pb = _pool_fwd(p, halo, g, t_glob).astype(BF16)
            dyb = dcat[:, A_WIDTH + g * HEAD:A_WIDTH + (g + 1) * HEAD]
            dps_ref[0:1, gs] += _colsum(dyb * _dot(pb, wp_ref[g]))
            dyl = (dyb * ps_ref[0:1, gs]).astype(BF16)
            dwp_ref[g] += _dot_tn(pb, dyl)
            dpooled = _dot_nt(dyl, wp_ref[g])
            cnt = jnp.minimum(t_glob + 1, POOL_WINDOWS[g]).astype(F32)
            q = dpooled / cnt
            s = jnp.concatenate([q, carry_ref[:, gs]], axis=0)
            for step in range(g + 1):
                s = s + pltpu.roll(s, ext_len - (1 << step), 0)
            dproj_ref[:, pcols] = (s[:ts] - dpooled).astype(BF16)
            carry_ref[:, gs] = q[0:POOL_HALO]
        dh1 = _dot_nt(dproj_ref[...], win_ref[...])
        x = x_ref[...]
        r1 = _rms(x)
        xn = x * r1
        pre = _vrow(vec_ref, V_PRE_MIX)
        one_sc = 1.0 + _vrow(vec_ref, V_SC_M)
        acc_ref[M_DSHM:M_DSHM + 1, :] += _colsum(dh1)
        acc_ref[M_DSCM:M_DSCM + 1, :] += _colsum(dh1 * (xn * pre))
        acc_ref[M_DPREMIX:M_DPREMIX + 1, :] += _colsum(dh1 * xn * one_sc)
        dxn = dh1 * pre * one_sc
        gx_ref[...] = dx2_ref[...] + r1 * (dxn - xn * _rowmean(dxn * xn))

    rev = lambda i: (nt - 1 - i, 0)
    halo_map = lambda i: (jnp.maximum((nt - 1 - i) * per - 1, 0), 0)
    hshape = (N_HEADS, HEAD, HEAD)
    return pl.pallas_call(
        body, name="mix_bwd", grid=(nt,),
        in_specs=[_tiled((ts, d), rev), _tiled((ts, d), rev), _tiled((ts, d), rev), _tiled((ts, n_proj), rev),
                  _tiled((POOL_HALO, n_proj), halo_map), _whole(vecs.shape), _resident(w_in_b.shape),
                  _resident(w_out_b.shape), _whole(sgu_g.shape), _whole(wm_b.shape), _whole(wmt_b.shape),
                  _whole(bsb.shape), _whole(wp_b.shape), _whole(ps.shape)],
        out_specs=[_tiled((ts, d), rev), _tiled((ts, n_proj), rev), _whole((8, d)), _whole(hshape), _whole(hshape),
                   _whole(hshape), _whole((8, HEAD)), _whole((8, A_WIDTH))],
        out_shape=[jax.ShapeDtypeStruct((s_len, d), F32), jax.ShapeDtypeStruct((s_len, n_proj), BF16),
                   jax.ShapeDtypeStruct((8, d), F32), jax.ShapeDtypeStruct(hshape, F32),
                   jax.ShapeDtypeStruct(hshape, F32), jax.ShapeDtypeStruct(hshape, F32),
                   jax.ShapeDtypeStruct((8, HEAD), F32), jax.ShapeDtypeStruct((8, A_WIDTH), F32)],
        scratch_shapes=[pltpu.VMEM((POOL_HALO, A_WIDTH), F32)],
        compiler_params=_seq_params(),
    )(dmixed, dx2, x, proj, proj, vecs, w_in_b, w_out_b, sgu_g, wm_b, wmt_b, bsb, wp_b, ps)


def _wgrad(a, b, tn, ts, name):
    s_len, m = a.shape
    n = b.shape[1]

    def body(a_ref, b_ref, o_ref):
        @pl.when(pl.program_id(1) == 0)
        def _():
            o_ref[...] = jnp.zeros_like(o_ref)

        o_ref[...] += _dot_tn(a_ref[...], b_ref[...])

    return pl.pallas_call(
        body, name=name, grid=(n // tn, s_len // ts),
        in_specs=[pl.BlockSpec((ts, m), lambda j, s: (s, 0)), pl.BlockSpec((ts, tn), lambda j, s: (s, j))],
        out_specs=pl.BlockSpec((m, tn), lambda j, s: (0, j)),
        out_shape=jax.ShapeDtypeStruct((m, n), F32),
        compiler_params=pltpu.CompilerParams(dimension_semantics=("parallel", "arbitrary"),
                                             vmem_limit_bytes=VMEM_LIMIT_BYTES),
    )(a, b)


def _adamw_big(g, w, m, v, name):
    r, cdim = g.shape
    tr = r
    while tr * cdim * 4 > (3 << 19) and tr % 16 == 0:
        tr //= 2

    def body(g_ref, w_ref, m_ref, v_ref, d_ref, nm_ref, nv_ref):
        delta, m2, v2 = _adamw_math(w_ref[0], g_ref[...], m_ref[0], v_ref[0])
        d_ref[0] = delta
        nm_ref[0] = m2
        nv_ref[0] = v2

    s3 = pl.BlockSpec((1, tr, cdim), lambda i: (0, i, 0))
    return pl.pallas_call(
        body, name=name, grid=(r // tr,),
        in_specs=[pl.BlockSpec((tr, cdim), lambda i: (i, 0)), s3, s3, s3],
        out_specs=[s3, s3, s3],
        out_shape=[jax.ShapeDtypeStruct(w.shape, F32)] * 3,
        compiler_params=pltpu.CompilerParams(dimension_semantics=("parallel",), vmem_limit_bytes=VMEM_LIMIT_BYTES),
    )(g, w, m, v)


def _wada_update(sct, gm, w, m, v):
    _, r, cdim = w.shape
    tr = 256
    kp = sct.shape[1]

    def body(s_ref, g_ref, w_ref, m_ref, v_ref, gw_ref, d_ref, nm_ref, nv_ref):
        g = _dot(s_ref[...], g_ref[...])
        gw_ref[0] = g
        delta, m2, v2 = _adamw_math(w_ref[0], g, m_ref[0], v_ref[0])
        d_ref[0] = delta
        nm_ref[0] = m2
        nv_ref[0] = v2

    s3 = pl.BlockSpec((1, tr, cdim), lambda i: (0, i, 0))
    return pl.pallas_call(
        body, name="wada_update", grid=(r // tr,),
        in_specs=[pl.BlockSpec((tr, kp), lambda i: (i, 0)), _whole(gm.shape), s3, s3, s3],
        out_specs=[s3, s3, s3, s3],
        out_shape=[jax.ShapeDtypeStruct(w.shape, F32)] * 4,
        compiler_params=pltpu.CompilerParams(dimension_semantics=("parallel",), vmem_limit_bytes=VMEM_LIMIT_BYTES),
    )(sct, gm, w, m, v)


def _small_update(g1, g2, g2s, gwm, gwp, gbz, gsg, gps, params):
    names = ["b_ada", "pre_mix_g", "post_mix_g", "sgu_norm_g", "w_spatial", "b_spatial", "w_pool", "pool_scale",
             "pre_ffn_g", "post_ffn_g", "conv_w", "conv_b"]
    d = g1.shape[2]
    flat_in = [g1, g2, g2s, gwm, gwp, gbz, gsg, gps]
    n_g = len(flat_in)
    for nm in names:
        flat_in += list(params[nm])

    def body(*refs):
        g1_ref, g2_ref, g2s_ref, gwm_ref, gwp_ref, gbz_ref, gsg_ref, gps_ref = refs[:n_g]
        wmv = refs[n_g:n_g + 3 * len(names)]
        loss_ref = refs[n_g + 3 * len(names)]
        outs = refs[n_g + 3 * len(names) + 1:]

        def dsum(ref, idx):
            acc = ref[(0,) + idx]
            for dev in range(1, N_DEV):
                acc = acc + ref[(dev,) + idx]
            return acc

        def apply(pi, g, widx, oidx):
            w_ref, m_ref, v_ref = wmv[3 * pi:3 * pi + 3]
            g_ref, d_ref, nm_ref, nv_ref = outs[4 * pi:4 * pi + 4]
            delta, m2, v2 = _adamw_math(w_ref[widx], g, m_ref[widx], v_ref[widx])
            g_ref[oidx] = g
            d_ref[oidx] = delta
            nm_ref[oidx] = m2
            nv_ref[oidx] = v2

        def row1(base, r):
            return (slice(base + r, base + r + 1), slice(None))

        tot = dsum(g1_ref, row1(0, F_LOSS))
        loss_ref[...] = jnp.zeros(loss_ref.shape, F32) + jnp.sum(tot) * (0.5 / d)
        mod_rows = [row1(16, M_DSHM), row1(16, M_DSCM), row1(8, B_DGM), row1(8, B_DSHF), row1(8, B_DSCF),
                    row1(0, F_DGF)]
        for j, rr in enumerate(mod_rows):
            cs = (slice(None), slice(j * d, (j + 1) * d))
            apply(0, dsum(g1_ref, rr), cs, cs)
        full = (slice(None), slice(None))
        apply(1, dsum(g1_ref, row1(16, M_DPREMIX)), full, full)
        apply(2, dsum(g1_ref, row1(8, B_DPOSTMIX)), full, full)
        apply(3, dsum(gsg_ref, (slice(0, N_HEADS), slice(None))), (0,), (0,))
        pos_i = lax.broadcasted_iota(jnp.int32, (HEAD, HEAD), 0)
        pos_j = lax.broadcasted_iota(jnp.int32, (HEAD, HEAD), 1)
        causal = (pos_j // CHUNK) <= (pos_i // CHUNK)
        for h in range(N_HEADS):
            blk = (slice(h * HEAD, (h + 1) * HEAD), slice(None))
            apply(4, jnp.where(causal, dsum(gwm_ref, blk), 0.0), (0, h), (0, h))
            bz = dsum(gbz_ref, blk)
            apply(5, jnp.sum(bz.T, axis=0, keepdims=True), (0, slice(h, h + 1)), (0, slice(h, h + 1)))
            apply(6, dsum(gwp_ref, blk), (0, h), (0, h))
        apply(7, dsum(gps_ref, (slice(0, 1), slice(None))), full, full)
        apply(8, dsum(g1_ref, row1(8, B_DPREFFN)), full, full)
        apply(9, dsum(g1_ref, row1(0, F_DPOSTFFN)), full, full)
        apply(10, dsum(g2s_ref, (slice(C_DCW, C_DCW + 3), slice(None))), (0,), (0,))
        apply(11, dsum(g2_ref, (slice(C_DCB, C_DCB + 1), slice(None))), full, full)

    out_shape = [jax.ShapeDtypeStruct((8, HEAD), F32)]
    for nm in names:
        out_shape += [jax.ShapeDtypeStruct(params[nm][0].shape, F32)] * 4
    res = pl.pallas_call(
        body, name="small_update", out_shape=out_shape,
        compiler_params=pltpu.CompilerParams(vmem_limit_bytes=VMEM_LIMIT_BYTES),
    )(*flat_in)
    out = {nm: tuple(res[1 + 4 * i:5 + 4 * i]) for i, nm in enumerate(names)}
    return res[0], out


def kernel(x, c, w_ada, b_ada, pre_mix_g, post_mix_g, w_in, sgu_norm_g, w_spatial, b_spatial, w_pool, pool_scale, w_out, pre_ffn_g, post_ffn_g, w_up, conv_w, conv_b, w_down, loss_target, m_w_ada, m_b_ada, m_pre_mix_g, m_post_mix_g, m_w_in, m_sgu_norm_g, m_w_spatial, m_b_spatial, m_w_pool, m_pool_scale, m_w_out, m_pre_ffn_g, m_post_ffn_g, m_w_up, m_conv_w, m_conv_b, m_w_down, v_w_ada, v_b_ada, v_pre_mix_g, v_post_mix_g, v_w_in, v_sgu_norm_g, v_w_spatial, v_b_spatial, v_w_pool, v_pool_scale, v_w_out, v_pre_ffn_g, v_post_ffn_g, v_w_up, v_conv_w, v_conv_b, v_w_down):
    xi, yi, ci = _mesh_pos()
    k_me = 2 * xi + yi
    dev = 2 * k_me + ci
    s_len, d = x.shape[1], x.shape[2]
    x2d = x[0]
    tgt = loss_target[0]
    ff2 = conv_b.shape[1]
    n_ada = w_ada.shape[2]
    n_cw = conv_w.shape[2]

    cw_blk = jnp.concatenate([conv_w[0], jnp.zeros((5, n_cw), F32)], axis=0)
    c_all, cw_all = _allgather_small([c.reshape(8, d // 8), cw_blk], "gather_c_convw")
    c_all = c_all.reshape(N_DEV, 8, d // 8).reshape(N_DEV, d)
    cw_full = jnp.concatenate([cw_all[16 * k:16 * k + 8] for k in range(N_CHIP)], axis=1)
    cvec = jnp.concatenate([cw_full[0:3], conv_b, jnp.zeros((4, ff2), F32)], axis=0)
    b_shard = lax.dynamic_slice_in_dim(b_ada, k_me * n_ada, n_ada, axis=1)
    mod_k, sc_all = _mod_shard(c_all, w_ada[0], b_shard)
    (mod_g,) = _allgather_small([mod_k], "gather_mod")
    mod_all = jnp.concatenate([mod_g[16 * k:16 * k + 8] for k in range(N_CHIP)], axis=1)
    mod_me = lax.dynamic_slice_in_dim(mod_all, dev, 1, axis=0).reshape(6, d)
    vecs = jnp.concatenate([mod_me, pre_mix_g, post_mix_g, pre_ffn_g, post_ffn_g, jnp.zeros((6, d), F32)], axis=0)

    col_flags = (True, False, True, False)
    w_in_b, w_out_b, w_up_b, w_down_b = _gather_weights([w_in[0], w_out[0], w_up[0], w_down[0]], col_flags)

    pos = jnp.arange(HEAD)
    causal = (pos[None, :] // CHUNK) <= (pos[:, None] // CHUNK)
    wm = jnp.where(causal[None], w_spatial[0], 0.0)
    wm_b = wm.astype(BF16)
    wmt_b = jnp.swapaxes(wm, 1, 2).astype(BF16)
    bsb = jnp.broadcast_to(b_spatial[0][:, :, None], (N_HEADS, HEAD, HEAD))
    wp_b = w_pool[0].astype(BF16)
    sgu_g = jnp.concatenate([sgu_norm_g[0], jnp.zeros((4, HEAD), F32)], axis=0)
    ps = jnp.concatenate([pool_scale, jnp.zeros((7, A_WIDTH), F32)], axis=0)

    h1, proj, cat, mixed, x2, h2 = _mix_fwd(x2d, vecs, w_in_b, w_out_b, sgu_g, wm_b, bsb, wp_b, ps, ts=512)
    up, act = _ffn_up(h2, w_up_b, cvec, ts=256)
    dy, df, acc_f = _ffn_down(act, w_down_b, x2, tgt, vecs, ts=512)

    dup, acc_c = _ffn_bwd_a(df, up, w_down_b, cvec, ts=256)
    dx2, dmixed, acc_b = _ffn_bwd_b(dup, w_up_b, x2, dy, mixed, vecs, ts=512)
    gx, dproj, acc_m, dwm, dwp, dbz, dsg, dps = _mix_bwd(
        dmixed, dx2, x2d, proj, vecs, w_in_b, w_out_b, sgu_g, wm_b, wmt_b, bsb, wp_b, ps, ts=256)
    g_w_down = _wgrad(act, df, d, 512, "wgrad_down")
    g_w_up = _wgrad(h2, dup, ff2 // N_CHIP, 512, "wgrad_up")
    g_w_out = _wgrad(cat, dmixed, d, 512, "wgrad_out")
    g_w_in = _wgrad(h1, dproj, proj.shape[1] // N_CHIP, 512, "wgrad_in")

    grads = [g_w_in, g_w_out, g_w_up, g_w_down]
    landed = _reduce_to_sibling(grads, col_flags)
    parts = _sum_with_sibling(grads, landed, col_flags, ci.reshape(1).astype(jnp.int32))
    reduced = _sum_chips_and_share(_reduce_over_chips(parts))
    big = {}
    for nm, red, (w, m, v) in zip(("w_in", "w_out", "w_up", "w_down"), reduced,
                                  ((w_in, m_w_in, v_w_in), (w_out, m_w_out, v_w_out),
                                   (w_up, m_w_up, v_w_up), (w_down, m_w_down, v_w_down))):
        g = red.reshape(w.shape[1], w.shape[2])
        big[nm] = (g.reshape(w.shape),) + tuple(_adamw_big(g, w, m, v, "adamw_" + nm))

    g1 = jnp.concatenate([acc_f, acc_b, acc_m], axis=0)
    hflat = (N_HEADS * HEAD, HEAD)
    gathered = _allgather_small(
        [g1, acc_c, dwm.reshape(hflat), dwp.reshape(hflat), dbz.reshape(hflat), dsg, dps], "gather_small_grads")
    g1a, g2a, gwm, gwp, gbz, gsg, gps = [t.reshape((N_DEV, t.shape[0] // N_DEV, t.shape[1])) for t in gathered]
    g2s = lax.dynamic_slice_in_dim(g2a, k_me * n_cw, n_cw, axis=2)
    params = {
        "b_ada": (b_ada, m_b_ada, v_b_ada), "pre_mix_g": (pre_mix_g, m_pre_mix_g, v_pre_mix_g),
        "post_mix_g": (post_mix_g, m_post_mix_g, v_post_mix_g),
        "sgu_norm_g": (sgu_norm_g, m_sgu_norm_g, v_sgu_norm_g), "w_spatial": (w_spatial, m_w_spatial, v_w_spatial),
        "b_spatial": (b_spatial, m_b_spatial, v_b_spatial), "w_pool": (w_pool, m_w_pool, v_w_pool),
        "pool_scale": (pool_scale, m_pool_scale, v_pool_scale), "pre_ffn_g": (pre_ffn_g, m_pre_ffn_g, v_pre_ffn_g),
        "post_ffn_g": (post_ffn_g, m_post_ffn_g, v_post_ffn_g), "conv_w": (conv_w, m_conv_w, v_conv_w),
        "conv_b": (conv_b, m_conv_b, v_conv_b),
    }
    loss_slab, small = _small_update(g1a, g2a, g2s, gwm, gwp, gbz, gsg, gps, params)

    gmod_all = jnp.concatenate(
        [g1a[:, 16 + M_DSHM], g1a[:, 16 + M_DSCM], g1a[:, 8 + B_DGM], g1a[:, 8 + B_DSHF], g1a[:, 8 + B_DSCF],
         g1a[:, F_DGF]], axis=1)
    gm = lax.dynamic_slice_in_dim(gmod_all, k_me * n_ada, n_ada, axis=1)
    gm = jnp.concatenate([gm, jnp.zeros((HEAD - N_DEV, n_ada), F32)], axis=0)
    sct = jnp.concatenate([sc_all.T, jnp.zeros((d, HEAD - N_DEV), F32)], axis=1)
    ada = tuple(_wada_update(sct, gm, w_ada, m_w_ada, v_w_ada))

    everything = dict(small)
    everything.update(big)
    everything["w_ada"] = ada
    order = ["w_ada", "b_ada", "pre_mix_g", "post_mix_g", "w_in", "sgu_norm_g", "w_spatial", "b_spatial", "w_pool",
             "pool_scale", "w_out", "pre_ffn_g", "post_ffn_g", "w_up", "conv_w", "conv_b", "w_down"]
    outs = [loss_slab[0, 0], gx.reshape(x.shape)]
    for j in range(4):
        outs += [everything[nm][j] for nm in order]
    return tuple(outs)
```

```python
import functools

import jax
import jax.numpy as jnp
from jax import lax
from jax.experimental import pallas as pl
from jax.experimental.pallas import tpu as pltpu

F32 = jnp.float32
BF16 = jnp.bfloat16
MESH = pl.DeviceIdType.MESH

EPS = 1e-6
HEAD = 128
N_HEADS = 4
A_WIDTH = N_HEADS * HEAD
CHUNK = 64
POOL_WINDOWS = (2, 4, 8, 16)
POOL_HALO = 16
FFN_TS = 256

ADAM_LR = 0.001
ADAM_B1 = 0.9
ADAM_B2 = 0.999
ADAM_EPS = 1e-08
ADAM_WD = 0.01
ADAM_STEP = 10

VMEM_LIMIT_BYTES = 58 * 1024 * 1024
N_DEV = 8
N_CHIP = 4


def _dot(a, b):
    return jnp.dot(a, b, preferred_element_type=F32)


def _dot_nt(a, b):
    return lax.dot_general(a, b, (((1,), (1,)), ((), ())), preferred_element_type=F32)


def _dot_tn(a, b):
    return lax.dot_general(a, b, (((0,), (0,)), ((), ())), preferred_element_type=F32)


def _gelu(x):
    return x * (0.5 * (1.0 + jnp.tanh(0.7978845608028654 * (x + 0.044715 * (x * x * x)))))


def _gelu_grad(x):
    t = jnp.tanh(0.7978845608028654 * (x + 0.044715 * (x * x * x)))
    return 0.5 * (1.0 + t) + (0.5 * x) * (1.0 - t * t) * (0.7978845608028654 * (1.0 + 0.134145 * (x * x)))


def _sigmoid(x):
    return 1.0 / (1.0 + jnp.exp(-x))


def _rms(x):
    return lax.rsqrt(jnp.mean(x * x, axis=-1, keepdims=True) + EPS)


def _colsum(x):
    return jnp.sum(x, axis=0, keepdims=True)


def _rowmean(x):
    return jnp.mean(x, axis=-1, keepdims=True)


def _tiled(shape, index_map):
    return pl.BlockSpec(shape, index_map)


def _resident(shape):
    nd = len(shape)
    return pl.BlockSpec(shape, lambda *_: (0,) * nd, pipeline_mode=pl.Buffered(1))


def _whole(shape):
    nd = len(shape)
    return pl.BlockSpec(shape, lambda *_: (0,) * nd)


def _seq_params():
    return pltpu.CompilerParams(dimension_semantics=("arbitrary",), vmem_limit_bytes=VMEM_LIMIT_BYTES)


def _ff_chunks(f, width=768):
    out, o = [], 0
    while o < f:
        w = min(width, f - o)
        out.append((o, w))
        o += w
    return out


def _pool_fwd(p, halo, g, t_glob):
    ext = jnp.concatenate([halo, p], axis=0)
    s = ext
    for step in range(g + 1):
        s = s + pltpu.roll(s, 1 << step, 0)
    cnt = jnp.minimum(t_glob + 1, POOL_WINDOWS[g]).astype(F32)
    return s[POOL_HALO:] / cnt - p


def _adamw_math(w, g, m, v):
    m = ADAM_B1 * m + (1.0 - ADAM_B1) * g
    v = ADAM_B2 * v + (1.0 - ADAM_B2) * (g * g)
    m_hat = m / (1.0 - ADAM_B1 ** ADAM_STEP)
    v_hat = v / (1.0 - ADAM_B2 ** ADAM_STEP)
    delta = -ADAM_LR * (m_hat / (jnp.sqrt(v_hat) + ADAM_EPS) + ADAM_WD * w)
    return delta, m, v


def _mesh_pos():
    return lax.axis_index("x"), lax.axis_index("y"), lax.axis_index("c")


def _allgather_small(arrs, name):
    n = len(arrs)

    def body(*refs):
        ins, outs = refs[:n], refs[n:2 * n]
        send_sems, recv_sems, local_sems = refs[2 * n:]
        x, y, c = _mesh_pos()
        me, sibling = (x, y, c), (x, y, 1 - c)
        chips = [(1 - x, y), (x, 1 - y), (1 - x, 1 - y)]

        def rows(a, px, py, pc):
            r = ins[a].shape[0]
            return outs[a].at[pl.ds(pl.multiple_of((4 * px + 2 * py + pc) * r, 8), r), :]

        def copy(a, k, block, to, src=None):
            return pltpu.make_async_remote_copy(
                src_ref=rows(a, *block) if src is None else src, dst_ref=rows(a, *block),
                send_sem=send_sems.at[a * 7 + k], recv_sem=recv_sems.at[a * 7 + k],
                device_id=to, device_id_type=MESH)

        started, local = [], []
        for a in range(n):
            mine = pltpu.make_async_copy(ins[a], rows(a, *me), local_sems.at[a])
            mine.start()
            local.append(mine)
            first = [copy(a, 0, me, sibling, src=ins[a])]
            first += [copy(a, 1 + j, me, (*chip, c), src=ins[a]) for j, chip in enumerate(chips)]
            for cp in first:
                cp.start()
            started += first
        for a in range(n):
            for j, chip in enumerate(chips):
                copy(a, 1 + j, (*chip, c), me).wait_recv()
                fwd = copy(a, 4 + j, (*chip, c), sibling)
                fwd.start()
                started.append(fwd)
        for a in range(n):
            copy(a, 0, sibling, me).wait_recv()
            for j, chip in enumerate(chips):
                copy(a, 4 + j, (*chip, 1 - c), me).wait_recv()
        for cp in started:
            cp.wait_send()
        for mine in local:
            mine.wait()

    vm = pl.BlockSpec(memory_space=pltpu.VMEM)
    return pl.pallas_call(
        body, name=name,
        out_shape=[jax.ShapeDtypeStruct((N_DEV * a.shape[0], a.shape[1]), a.dtype) for a in arrs],
        in_specs=[vm] * n, out_specs=[vm] * n,
        scratch_shapes=[pltpu.SemaphoreType.DMA((7 * n,)), pltpu.SemaphoreType.DMA((7 * n,)),
                        pltpu.SemaphoreType.DMA((n,))],
        compiler_params=pltpu.CompilerParams(vmem_limit_bytes=VMEM_LIMIT_BYTES),
    )(*arrs)


def _piece(ref, col_sharded, k, h):
    m, n = ref.shape
    if col_sharded:
        mh, nc = m // 2, n // N_CHIP
        return ref.at[pl.ds(pl.multiple_of(h * mh, 16), mh), pl.ds(pl.multiple_of(k * nc, 128), nc)]
    rp = m // (2 * N_CHIP)
    return ref.at[pl.ds(pl.multiple_of((2 * k + h) * rp, 16), rp), :]


def _piece_shape(shape, col_sharded):
    m, n = shape
    return (m // 2, n // N_CHIP) if col_sharded else (m // (2 * N_CHIP), n)


def _gather_weights(shards, col_flags):
    n = len(shards)
    full_shapes = []
    for s, col in zip(shards, col_flags):
        full_shapes.append((s.shape[0], s.shape[1] * N_CHIP) if col else (s.shape[0] * N_CHIP, s.shape[1]))

    def body(*refs):
        ins, outs, stage = refs[:n], refs[n:2 * n], refs[2 * n:3 * n]
        send_sems, recv_sems, local_sems = refs[3 * n:]
        x, y, c = _mesh_pos()
        sibling = (x, y, 1 - c)
        chips = [(1 - x, y), (x, 1 - y), (1 - x, 1 - y)]
        k_me = 2 * x + y

        def half_of_stage(a, h):
            rows = stage[a].shape[0] // 2
            return stage[a].at[pl.ds(pl.multiple_of(h * rows, 16), rows), :]

        def own_window(a):
            m, nn = stage[a].shape
            if col_flags[a]:
                return outs[a].at[:, pl.ds(pl.multiple_of(k_me * nn, 128), nn)]
            return outs[a].at[pl.ds(pl.multiple_of(k_me * m, 16), m), :]

        def copy(a, s, src, k, h, to):
            return pltpu.make_async_remote_copy(
                src_ref=src, dst_ref=_piece(outs[a], col_flags[a], k, h),
                send_sem=send_sems.at[a * 6 + s], recv_sem=recv_sems.at[a * 6 + s],
                device_id=to, device_id_type=MESH)

        started, local = [], []
        for a in range(n):
            stage[a][...] = ins[a][...].astype(BF16)
            mine = pltpu.make_async_copy(stage[a], own_window(a), local_sems.at[a])
            mine.start()
            local.append(mine)
            for j, chip in enumerate(chips):
                cp = copy(a, j, half_of_stage(a, c), k_me, c, (*chip, c))
                cp.start()
                started.append(cp)
        for a in range(n):
            for j, chip in enumerate(chips):
                k_from = 2 * chip[0] + chip[1]
                win = _piece(outs[a], col_flags[a], k_from, c)
                copy(a, j, win, k_from, c, (*chip, c)).wait_recv()
                fwd = copy(a, 3 + j, win, k_from, c, sibling)
                fwd.start()
                started.append(fwd)
        for a in range(n):
            for j, chip in enumerate(chips):
                k_from = 2 * chip[0] + chip[1]
                win = _piece(outs[a], col_flags[a], k_from, 1 - c)
                copy(a, 3 + j, win, k_from, 1 - c, sibling).wait_recv()
        for cp in started:
            cp.wait_send()
        for mine in local:
            mine.wait()

    return pl.pallas_call(
        body, name="gather_weights",
        out_shape=[jax.ShapeDtypeStruct(fs, BF16) for fs in full_shapes],
        in_specs=[pl.BlockSpec(memory_space=pltpu.VMEM)] * n,
        out_specs=[pl.BlockSpec(memory_space=pl.ANY)] * n,
        scratch_shapes=[pltpu.VMEM(s.shape, BF16) for s in shards]
        + [pltpu.SemaphoreType.DMA((6 * n,)), pltpu.SemaphoreType.DMA((6 * n,)), pltpu.SemaphoreType.DMA((n,))],
        compiler_params=pltpu.CompilerParams(vmem_limit_bytes=VMEM_LIMIT_BYTES),
    )(*shards)


def _reduce_to_sibling(grads, col_flags):
    n = len(grads)
    pshapes = [_piece_shape(g.shape, col) for g, col in zip(grads, col_flags)]

    def body(*refs):
        ins, outs = refs[:n], refs[n:2 * n]
        send_sems, recv_sems = refs[2 * n:]
        x, y, c = _mesh_pos()
        sibling = (x, y, 1 - c)
        copies = []
        for a in range(n):
            for k in range(N_CHIP):
                cp = pltpu.make_async_remote_copy(
                    src_ref=_piece(ins[a], col_flags[a], k, 1 - c), dst_ref=outs[a].at[k],
                    send_sem=send_sems.at[a * N_CHIP + k], recv_sem=recv_sems.at[a * N_CHIP + k],
                    device_id=sibling, device_id_type=MESH)
                cp.start()
                copies.append(cp)
        for cp in copies:
            cp.wait_recv()
        for cp in copies:
            cp.wait_send()

    anyspec = pl.BlockSpec(memory_space=pl.ANY)
    return pl.pallas_call(
        body, name="reduce_to_sibling",
        out_shape=[jax.ShapeDtypeStruct((N_CHIP,) + ps, F32) for ps in pshapes],
        in_specs=[anyspec] * n, out_specs=[anyspec] * n,
        scratch_shapes=[pltpu.SemaphoreType.DMA((N_CHIP * n,)), pltpu.SemaphoreType.DMA((N_CHIP * n,))],
    )(*grads)


def _sum_with_sibling(grads, landed, col_flags, c_idx):
    n = len(grads)
    pshapes = [_piece_shape(g.shape, col) for g, col in zip(grads, col_flags)]

    def body(c_ref, *refs):
        ins, lands, outs = refs[:n], refs[n:2 * n], refs[2 * n:]
        for a in range(n):
            outs[a][0] = (ins[a][...] + lands[a][0]).astype(BF16)

    in_specs = []
    for ps, col in zip(pshapes, col_flags):
        if col:
            in_specs.append(pl.BlockSpec(ps, lambda k, c_ref: (c_ref[0], k)))
        else:
            in_specs.append(pl.BlockSpec(ps, lambda k, c_ref: (2 * k + c_ref[0], 0)))
    land_specs = [pl.BlockSpec((1,) + ps, lambda k, c_ref: (k, 0, 0)) for ps in pshapes]
    return pl.pallas_call(
        body, name="sum_with_sibling",
        grid_spec=pltpu.PrefetchScalarGridSpec(
            num_scalar_prefetch=1, grid=(N_CHIP,),
            in_specs=in_specs + land_specs, out_specs=land_specs),
        out_shape=[jax.ShapeDtypeStruct((N_CHIP,) + ps, BF16) for ps in pshapes],
        compiler_params=pltpu.CompilerParams(dimension_semantics=("arbitrary",), vmem_limit_bytes=VMEM_LIMIT_BYTES),
    )(c_idx, *grads, *landed)


def _reduce_over_chips(parts):
    n = len(parts)

    def body(*refs):
        ins, outs = refs[:n], refs[n:2 * n]
        send_sems, recv_sems, local_sems = refs[2 * n:]
        x, y, c = _mesh_pos()
        chips = [(1 - x, y), (x, 1 - y), (1 - x, 1 - y)]
        k_me = 2 * x + y
        copies, local = [], []
        for a in range(n):
            mine = pltpu.make_async_copy(ins[a].at[k_me], outs[a].at[k_me], local_sems.at[a])
            mine.start()
            local.append(mine)
            for j, chip in enumerate(chips):
                k_to = 2 * chip[0] + chip[1]
                cp = pltpu.make_async_remote_copy(
                    src_ref=ins[a].at[k_to], dst_ref=outs[a].at[k_me],
                    send_sem=send_sems.at[a * 3 + j], recv_sem=recv_sems.at[a * 3 + j],
                    device_id=(*chip, c), device_id_type=MESH)
                cp.start()
                copies.append((cp, a, j, chip))
        for cp, a, j, chip in copies:
            k_from = 2 * chip[0] + chip[1]
            pltpu.make_async_remote_copy(
                src_ref=ins[a].at[k_from], dst_ref=outs[a].at[k_from],
                send_sem=send_sems.at[a * 3 + j], recv_sem=recv_sems.at[a * 3 + j],
                device_id=(*chip, c), device_id_type=MESH).wait_recv()
        for cp, _, _, _ in copies:
            cp.wait_send()
        for mine in local:
            mine.wait()

    anyspec = pl.BlockSpec(memory_space=pl.ANY)
    return pl.pallas_call(
        body, name="reduce_over_chips",
        out_shape=[jax.ShapeDtypeStruct(p.shape, p.dtype) for p in parts],
        in_specs=[anyspec] * n, out_specs=[anyspec] * n,
        scratch_shapes=[pltpu.SemaphoreType.DMA((3 * n,)), pltpu.SemaphoreType.DMA((3 * n,)),
                        pltpu.SemaphoreType.DMA((n,))],
    )(*parts)


def _sum_chips_and_share(landed):
    n = len(landed)

    def body(*refs):
        ins, outs, red = refs[:n], refs[n:2 * n], refs[2 * n:3 * n]
        send_sems, recv_sems, local_sems = refs[3 * n:]
        x, y, c = _mesh_pos()
        sibling = (x, y, 1 - c)
        copies, local = [], []
        for a in range(n):
            red[a][...] = ((ins[a][0].astype(F32) + ins[a][1].astype(F32)) + ins[a][2].astype(F32)
                           + ins[a][3].astype(F32))
            mine = pltpu.make_async_copy(red[a], outs[a].at[c], local_sems.at[a])
            mine.start()
            local.append(mine)
            cp = pltpu.make_async_remote_copy(
                src_ref=red[a], dst_ref=outs[a].at[c],
                send_sem=send_sems.at[a], recv_sem=recv_sems.at[a],
                device_id=sibling, device_id_type=MESH)
            cp.start()
            copies.append(cp)
        for a in range(n):
            pltpu.make_async_remote_copy(
                src_ref=red[a], dst_ref=outs[a].at[1 - c],
                send_sem=send_sems.at[a], recv_sem=recv_sems.at[a],
                device_id=sibling, device_id_type=MESH).wait_recv()
        for cp in copies:
            cp.wait_send()
        for mine in local:
            mine.wait()

    return pl.pallas_call(
        body, name="sum_chips_and_share",
        out_shape=[jax.ShapeDtypeStruct((2,) + l.shape[1:], F32) for l in landed],
        in_specs=[pl.BlockSpec(memory_space=pltpu.VMEM)] * n,
        out_specs=[pl.BlockSpec(memory_space=pl.ANY)] * n,
        scratch_shapes=[pltpu.VMEM(l.shape[1:], F32) for l in landed]
        + [pltpu.SemaphoreType.DMA((n,)), pltpu.SemaphoreType.DMA((n,)), pltpu.SemaphoreType.DMA((n,))],
        compiler_params=pltpu.CompilerParams(vmem_limit_bytes=VMEM_LIMIT_BYTES),
    )(*landed)


def _mod_shard(c_all, w_ada, b_shard):
    def body(c_ref, w_ref, b_ref, o_ref, sc_ref):
        cc = c_ref[...]
        sc = cc * _sigmoid(cc)
        sc_ref[...] = sc
        o_ref[...] = _dot(sc, w_ref[...]) + b_ref[...]

    nb, d = c_all.shape
    nn = w_ada.shape[1]
    return pl.pallas_call(
        body, name="mod_shard",
        out_shape=[jax.ShapeDtypeStruct((nb, nn), F32), jax.ShapeDtypeStruct((nb, d), F32)],
        compiler_params=pltpu.CompilerParams(vmem_limit_bytes=VMEM_LIMIT_BYTES),
    )(c_all, w_ada, b_shard)


V_SH_M, V_SC_M, V_G_M, V_SH_F, V_SC_F, V_G_F, V_PRE_MIX, V_POST_MIX, V_PRE_FFN, V_POST_FFN = range(10)


def _vrow(vec_ref, r):
    return vec_ref[r:r + 1, :]


def _mix_fwd(x, vecs, w_in_b, w_out_b, sgu_g, wm_b, bsb, wp_b, ps, pmats, ts):
    s_len, d = x.shape
    nt = s_len // ts
    nblk = ts // HEAD
    n_proj = w_in_b.shape[1]

    def body(x_ref, vec_ref, win_ref, wout_ref, sg_ref, wm_ref, bs_ref, wp_ref, ps_ref, pm_ref,
             h1_ref, proj_ref, cat_ref, mixed_ref, x2_ref, h2_ref, carry_ref):
        i = pl.program_id(0)

        @pl.when(i == 0)
        def _():
            carry_ref[...] = jnp.zeros_like(carry_ref)

        x = x_ref[...]
        h1 = (((x * _rms(x)) * _vrow(vec_ref, V_PRE_MIX)) * (1.0 + _vrow(vec_ref, V_SC_M))
              + _vrow(vec_ref, V_SH_M)).astype(BF16)
        h1_ref[...] = h1
        proj = _dot(h1, win_ref[...])
        proj_ref[...] = proj.astype(BF16)
        t_glob = lax.broadcasted_iota(jnp.int32, (ts, HEAD), 0) + i * ts
        for h in range(N_HEADS):
            u = _gelu(proj[:, h * HEAD:(h + 1) * HEAD])
            v = _gelu(proj[:, A_WIDTH + h * HEAD:A_WIDTH + (h + 1) * HEAD])
            vn = ((v * _rms(v)) * sg_ref[h:h + 1, :]).astype(BF16)
            for b in range(nblk):
                rs = slice(b * HEAD, (b + 1) * HEAD)
                z = _dot(wm_ref[h], vn[rs]) + bs_ref[h]
                cat_ref[rs, h * HEAD:(h + 1) * HEAD] = (u[rs] * z).astype(BF16)
        for g in range(len(POOL_WINDOWS)):
            gs = slice(g * HEAD, (g + 1) * HEAD)
            p = proj[:, 2 * A_WIDTH + g * HEAD:2 * A_WIDTH + (g + 1) * HEAD]
            pooled = _pool_fwd(p, carry_ref[:, gs], g, t_glob)
            yb = _dot(pooled.astype(BF16), wp_ref[g]) * ps_ref[0:1, gs]
            cat_ref[:, A_WIDTH + g * HEAD:A_WIDTH + (g + 1) * HEAD] = yb.astype(BF16)
        carry_ref[...] = proj[ts - POOL_HALO:ts, 2 * A_WIDTH:]
        mixed = _dot(cat_ref[...], wout_ref[...])
        mixed_ref[...] = mixed
        x2 = x + _vrow(vec_ref, V_G_M) * ((mixed * _rms(mixed)) * _vrow(vec_ref, V_POST_MIX))
        x2_ref[...] = x2
        h2 = (((x2 * _rms(x2)) * _vrow(vec_ref, V_PRE_FFN)) * (1.0 + _vrow(vec_ref, V_SC_F))
              + _vrow(vec_ref, V_SH_F)).astype(BF16)
        for b in range(ts // FFN_TS):
            rs = slice(b * FFN_TS, (b + 1) * FFN_TS)
            h2_ref[rs, :] = _permute_bf16(pm_ref[0], h2[rs])

    row = lambda i: (i, 0)
    return pl.pallas_call(
        body, name="mix_fwd", grid=(nt,),
        in_specs=[_tiled((ts, d), row), _whole(vecs.shape), _resident(w_in_b.shape), _resident(w_out_b.shape),
                  _whole(sgu_g.shape), _whole(wm_b.shape), _whole(bsb.shape), _whole(wp_b.shape), _whole(ps.shape),
                  _whole(pmats.shape)],
        out_specs=[_tiled((ts, d), row), _tiled((ts, n_proj), row), _tiled((ts, d), row),
                   _tiled((ts, d), row), _tiled((ts, d), row), _tiled((ts, d), row)],
        out_shape=[jax.ShapeDtypeStruct((s_len, d), BF16), jax.ShapeDtypeStruct((s_len, n_proj), BF16),
                   jax.ShapeDtypeStruct((s_len, d), BF16), jax.ShapeDtypeStruct((s_len, d), F32),
                   jax.ShapeDtypeStruct((s_len, d), F32), jax.ShapeDtypeStruct((s_len, d), BF16)],
        scratch_shapes=[pltpu.VMEM((POOL_HALO, A_WIDTH), F32)],
        compiler_params=_seq_params(),
    )(x, vecs, w_in_b, w_out_b, sgu_g, wm_b, bsb, wp_b, ps, pmats)


def _perm_mats(ts):
    p = jnp.arange(ts)
    pm = (((p % 8) * (ts // 8) + p // 8)[:, None] == p[None, :]).astype(BF16)
    return jnp.stack([pm, pm.T])


def _permute_bf16(pm, xb):
    return _dot(pm, xb).astype(BF16)


def _permute_f32(pm, x):
    hi = x.astype(BF16)
    lo = (x - hi.astype(F32)).astype(BF16)
    return _dot(pm, hi) + _dot(pm, lo)


def _conv_out(u, um2, um1, cv_ref, cols):
    return (cv_ref[3:4, cols] + um2 * cv_ref[0:1, cols] + um1 * cv_ref[1:2, cols] + u * cv_ref[2:3, cols])


F_LOSS, F_DGF, F_DPOSTFFN = 0, 1, 2
B_DSHF, B_DSCF, B_DPREFFN, B_DGM, B_DPOSTMIX = 0, 1, 2, 3, 4
M_DSHM, M_DSCM, M_DPREMIX = 0, 1, 2
C_DCB, C_DCW = 0, 1


def _ffn_fwd(h2p, x2, tgt, w_up_b, w_down_b, cvec, vecs, pmats, ts):
    s_len, d = x2.shape
    ff2 = w_up_b.shape[1]
    ff = ff2 // 2
    nt = s_len // ts
    chunks = _ff_chunks(ff)

    def body(h2_ref, x2_ref, t_ref, wu_ref, wd_ref, cv_ref, vec_ref, pm_ref,
             up_ref, y_ref, act_ref, dy_ref, df_ref, acc_ref, carry_ref):
        @pl.when(pl.program_id(0) == 0)
        def _():
            carry_ref[...] = jnp.zeros_like(carry_ref)
            acc_ref[...] = jnp.zeros_like(acc_ref)

        h2v = h2_ref[...]
        f = None
        for o, w in chunks:
            sub0 = lax.broadcasted_iota(jnp.int32, (8, w), 0) == 0
            ys = []
            for base in (0, ff):
                cols = slice(base + o, base + o + w)
                u = _dot(h2v, wu_ref[:, cols])
                up_ref[:, cols] = u.astype(BF16)
                b1 = jnp.where(sub0, pltpu.roll(carry_ref[8:16, cols], 1, 0), pltpu.roll(u[ts - 8:ts], 1, 0))
                b2 = jnp.where(sub0, pltpu.roll(carry_ref[0:8, cols], 1, 0), pltpu.roll(u[ts - 16:ts - 8], 1, 0))
                um1 = jnp.concatenate([b1, u[:ts - 8]], axis=0)
                um2 = jnp.concatenate([b2, b1, u[:ts - 16]], axis=0)
                yv = _conv_out(u, um2, um1, cv_ref, cols)
                y_ref[:, cols] = yv.astype(BF16)
                carry_ref[:, cols] = u[ts - 16:ts]
                ys.append(yv)
            gate, val = ys
            act = ((gate * _sigmoid(gate)) * val).astype(BF16)
            act_ref[:, o:o + w] = act
            part = _dot(act, wd_ref[o:o + w, :])
            f = part if f is None else f + part
        f = _permute_f32(pm_ref[1], f)
        r3 = _rms(f)
        fhat = f * r3
        post = _vrow(vec_ref, V_POST_FFN)
        g_f = _vrow(vec_ref, V_G_F)
        fn = fhat * post
        e = (x2_ref[...] + g_f * fn) - t_ref[...]
        dy = e * (1.0 / d)
        dy_ref[...] = dy
        dfn = dy * g_f
        acc_ref[F_LOSS:F_LOSS + 1, :] += _colsum(e * e)
        acc_ref[F_DGF:F_DGF + 1, :] += _colsum(dy * fn)
        acc_ref[F_DPOSTFFN:F_DPOSTFFN + 1, :] += _colsum(dfn * fhat)
        dfhat = dfn * post
        df = (r3 * (dfhat - fhat * _rowmean(dfhat * fhat))).astype(BF16)
        df_ref[...] = _permute_bf16(pm_ref[0], df)

    row = lambda i: (i, 0)
    return pl.pallas_call(
        body, name="ffn_fwd", grid=(nt,),
        in_specs=[_tiled((ts, d), row), _tiled((ts, d), row), _tiled((ts, d), row), _resident(w_up_b.shape),
                  _resident(w_down_b.shape), _whole(cvec.shape), _whole(vecs.shape), _whole(pmats.shape)],
        out_specs=[_tiled((ts, ff2), row), _tiled((ts, ff2), row), _tiled((ts, ff), row), _tiled((ts, d), row),
                   _tiled((ts, d), row), _whole((8, d))],
        out_shape=[jax.ShapeDtypeStruct((s_len, ff2), BF16), jax.ShapeDtypeStruct((s_len, ff2), BF16),
                   jax.ShapeDtypeStruct((s_len, ff), BF16), jax.ShapeDtypeStruct((s_len, d), F32),
                   jax.ShapeDtypeStruct((s_len, d), BF16), jax.ShapeDtypeStruct((8, d), F32)],
        scratch_shapes=[pltpu.VMEM((16, ff2), F32)],
        compiler_params=_seq_params(),
    )(h2p, x2, tgt, w_up_b, w_down_b, cvec, vecs, pmats)


def _ffn_bwd(dfp, upp, yp, x2, dy, mixed, w_up_b, w_down_b, cvec, vecs, pmats, ts):
    s_len, d = x2.shape
    ff2 = w_up_b.shape[1]
    ff = ff2 // 2
    nt = s_len // ts
    chunks = _ff_chunks(ff, 512)

    def body(df_ref, up_ref, y_ref, x2_ref, dy_ref, mx_ref, wu_ref, wd_ref, cv_ref, vec_ref, pm_ref,
             dup_ref, dx2_ref, dmx_ref, accc_ref, acc_ref, carry_ref):
        @pl.when(pl.program_id(0) == 0)
        def _():
            carry_ref[...] = jnp.zeros_like(carry_ref)
            accc_ref[...] = jnp.zeros_like(accc_ref)
            acc_ref[...] = jnp.zeros_like(acc_ref)

        dfv = df_ref[...]
        dh2 = None
        for o, w in chunks:
            sub7 = lax.broadcasted_iota(jnp.int32, (8, w), 0) == 7
            dact = _dot_nt(dfv, wd_ref[o:o + w, :])
            gate = y_ref[:, o:o + w].astype(F32)
            val = y_ref[:, ff + o:ff + o + w].astype(F32)
            sg = _sigmoid(gate)
            gs = gate * sg
            dys = ((dact * val) * (sg + gs * (1.0 - sg)), dact * gs)
            for base, dyv in zip((0, ff), dys):
                cols = slice(base + o, base + o + w)
                u = up_ref[:, cols].astype(F32)
                e0 = jnp.where(sub7, pltpu.roll(carry_ref[0:8, cols], 7, 0), pltpu.roll(dyv[0:8], 7, 0))
                e1 = jnp.where(sub7, pltpu.roll(carry_ref[8:16, cols], 7, 0), pltpu.roll(dyv[8:16], 7, 0))
                dyp1 = jnp.concatenate([dyv[8:], e0], axis=0)
                dyp2 = jnp.concatenate([dyv[16:], e0, e1], axis=0)
                accc_ref[C_DCB:C_DCB + 1, cols] += _colsum(dyv)
                accc_ref[C_DCW + 0:C_DCW + 1, cols] += _colsum(dyp2 * u)
                accc_ref[C_DCW + 1:C_DCW + 2, cols] += _colsum(dyp1 * u)
                accc_ref[C_DCW + 2:C_DCW + 3, cols] += _colsum(dyv * u)
                dup = (dyv * cv_ref[2:3, cols] + dyp1 * cv_ref[1:2, cols] + dyp2 * cv_ref[0:1, cols]).astype(BF16)
                dup_ref[:, cols] = dup
                part = _dot_nt(dup, wu_ref[:, cols])
                dh2 = part if dh2 is None else dh2 + part
                carry_ref[:, cols] = dyv[0:16]
        dh2 = _permute_f32(pm_ref[1], dh2)
        x2 = x2_ref[...]
        r2 = _rms(x2)
        xn = x2 * r2
        pre = _vrow(vec_ref, V_PRE_FFN)
        one_sc = 1.0 + _vrow(vec_ref, V_SC_F)
        acc_ref[B_DSHF:B_DSHF + 1, :] += _colsum(dh2)
        acc_ref[B_DSCF:B_DSCF + 1, :] += _colsum(dh2 * (xn * pre))
        acc_ref[B_DPREFFN:B_DPREFFN + 1, :] += _colsum(dh2 * xn * one_sc)
        dxn = dh2 * pre * one_sc
        dx2 = dy_ref[...] + r2 * (dxn - xn * _rowmean(dxn * xn))
        dx2_ref[...] = dx2
        mixed = mx_ref[...]
        rm = _rms(mixed)
        mhat = mixed * rm
        post = _vrow(vec_ref, V_POST_MIX)
        acc_ref[B_DGM:B_DGM + 1, :] += _colsum(dx2 * (mhat * post))
        dmn = dx2 * _vrow(vec_ref, V_G_M)
        acc_ref[B_DPOSTMIX:B_DPOSTMIX + 1, :] += _colsum(dmn * mhat)
        dmhat = dmn * post
        dmx_ref[...] = (rm * (dmhat - mhat * _rowmean(dmhat * mhat))).astype(BF16)

    rev = lambda i: (nt - 1 - i, 0)
    return pl.pallas_call(
        body, name="ffn_bwd", grid=(nt,),
        in_specs=[_tiled((ts, d), rev), _tiled((ts, ff2), rev), _tiled((ts, ff2), rev), _tiled((ts, d), rev),
                  _tiled((ts, d), rev), _tiled((ts, d), rev), _resident(w_up_b.shape), _resident(w_down_b.shape),
                  _whole(cvec.shape), _whole(vecs.shape), _whole(pmats.shape)],
        out_specs=[_tiled((ts, ff2), rev), _tiled((ts, d), rev), _tiled((ts, d), rev), _whole((8, ff2)),
                   _whole((8, d))],
        out_shape=[jax.ShapeDtypeStruct((s_len, ff2), BF16), jax.ShapeDtypeStruct((s_len, d), F32),
                   jax.ShapeDtypeStruct((s_len, d), BF16), jax.ShapeDtypeStruct((8, ff2), F32),
                   jax.ShapeDtypeStruct((8, d), F32)],
        scratch_shapes=[pltpu.VMEM((16, ff2), F32)],
        compiler_params=_seq_params(),
    )(dfp, upp, yp, x2, dy, mixed, w_up_b, w_down_b, cvec, vecs, pmats)


def _mix_bwd(dmixed, dx2, x, proj, vecs, w_in_b, w_out_b, sgu_g, wm_b, wmt_b, bsb, wp_b, ps, ts):
    s_len, d = x.shape
    nt = s_len // ts
    nblk = ts // HEAD
    n_proj = proj.shape[1]
    per = ts // POOL_HALO
    ext_len = ts + POOL_HALO

    def body(dmx_ref, dx2_ref, x_ref, proj_ref, projh_ref, vec_ref, win_ref, wout_ref, sg_ref, wm_ref, wmt_ref,
             bs_ref, wp_ref, ps_ref,
             gx_ref, dproj_ref, acc_ref, dwm_ref, dwp_ref, dbz_ref, dg_ref, dps_ref, carry_ref):
        i = pl.program_id(0)
        tile = nt - 1 - i

        @pl.when(i == 0)
        def _():
            carry_ref[...] = jnp.zeros_like(carry_ref)
            for r in (acc_ref, dwm_ref, dwp_ref, dbz_ref, dg_ref, dps_ref):
                r[...] = jnp.zeros_like(r)

        dcat = _dot_nt(dmx_ref[...], wout_ref[...])
        t_glob = lax.broadcasted_iota(jnp.int32, (ts, HEAD), 0) + tile * ts
        for h in range(N_HEADS):
            hs = slice(h * HEAD, (h + 1) * HEAD)
            vs = slice(A_WIDTH + h * HEAD, A_WIDTH + (h + 1) * HEAD)
            au = proj_ref[:, hs].astype(F32)
            av = proj_ref[:, vs].astype(F32)
            u = _gelu(au)
            v = _gelu(av)
            rv = _rms(v)
            vhat = v * rv
            gain = sg_ref[h:h + 1, :]
            vn = (vhat * gain).astype(BF16)
            dout = dcat[:, hs]
            du_parts, dvn_parts = [], []
            for b in range(nblk):
                rs = slice(b * HEAD, (b + 1) * HEAD)
                z = _dot(wm_ref[h], vn[rs]) + bs_ref[h]
                du_parts.append(dout[rs] * z)
                dz = dout[rs] * u[rs]
                dbz_ref[h] += dz
                dzb = dz.astype(BF16)
                dwm_ref[h] += _dot_nt(dzb, vn[rs])
                dvn_parts.append(_dot(wmt_ref[h], dzb))
            du = jnp.concatenate(du_parts, axis=0)
            dvn = jnp.concatenate(dvn_parts, axis=0)
            dg_ref[h:h + 1, :] += _colsum(dvn * vhat)
            dvhat = dvn * gain
            dv = rv * (dvhat - vhat * _rowmean(dvhat * vhat))
            dproj_ref[:, hs] = (du * _gelu_grad(au)).astype(BF16)
            dproj_ref[:, vs] = (dv * _gelu_grad(av)).astype(BF16)
        for g in range(len(POOL_WINDOWS)):
            gs = slice(g * HEAD, (g + 1) * HEAD)
            pcols = slice(2 * A_WIDTH + g * HEAD, 2 * A_WIDTH + (g + 1) * HEAD)
            p = proj_ref[:, pcols].astype(F32)
            halo = jnp.where(tile > 0, projh_ref[:, pcols].astype(F32), 0.0)
            pb = _pool_fwd(p, halo, g, t_glob).astype(BF16)
            dyb = dcat[:, A_WIDTH + g * HEAD:A_WIDTH + (g + 1) * HEAD]
            dps_ref[0:1, gs] += _colsum(dyb * _dot(pb, wp_ref[g]))
            dyl = (dyb * ps_ref[0:1, gs]).astype(BF16)
            dwp_ref[g] += _dot_tn(pb, dyl)
            dpooled = _dot_nt(dyl, wp_ref[g])
            cnt = jnp.minimum(t_glob + 1, POOL_WINDOWS[g]).astype(F32)
            q = dpooled / cnt
            s = jnp.concatenate([q, carry_ref[:, gs]], axis=0)
            for step in range(g + 1):
                s = s + pltpu.roll(s, ext_len - (1 << step), 0)
            dproj_ref[:, pcols] = (s[:ts] - dpooled).astype(BF16)
            carry_ref[:, gs] = q[0:POOL_HALO]
        dh1 = _dot_nt(dproj_ref[...], win_ref[...])
        x = x_ref[...]
        r1 = _rms(x)
        xn = x * r1
        pre = _vrow(vec_ref, V_PRE_MIX)
        one_sc = 1.0 + _vrow(vec_ref, V_SC_M)
        acc_ref[M_DSHM:M_DSHM + 1, :] += _colsum(dh1)
        acc_ref[M_DSCM:M_DSCM + 1, :] += _colsum(dh1 * (xn * pre))
        acc_ref[M_DPREMIX:M_DPREMIX + 1, :] += _colsum(dh1 * xn * one_sc)
        dxn = dh1 * pre * one_sc
        gx_ref[...] = dx2_ref[...] + r1 * (dxn - xn * _rowmean(dxn * xn))

    rev = lambda i: (nt - 1 - i, 0)
    halo_map = lambda i: (jnp.maximum((nt - 1 - i) * per - 1, 0), 0)
    hshape = (N_HEADS, HEAD, HEAD)
    return pl.pallas_call(
        body, name="mix_bwd", grid=(nt,),
        in_specs=[_tiled((ts, d), rev), _tiled((ts, d), rev), _tiled((ts, d), rev), _tiled((ts, n_proj), rev),
                  _tiled((POOL_HALO, n_proj), halo_map), _whole(vecs.shape), _resident(w_in_b.shape),
                  _resident(w_out_b.shape), _whole(sgu_g.shape), _whole(wm_b.shape), _whole(wmt_b.shape),
                  _whole(bsb.shape), _whole(wp_b.shape), _whole(ps.shape)],
        out_specs=[_tiled((ts, d), rev), _tiled((ts, n_proj), rev), _whole((8, d)), _whole(hshape), _whole(hshape),
                   _whole(hshape), _whole((8, HEAD)), _whole((8, A_WIDTH))],
        out_shape=[jax.ShapeDtypeStruct((s_len, d), F32), jax.ShapeDtypeStruct((s_len, n_proj), BF16),
                   jax.ShapeDtypeStruct((8, d), F32), jax.ShapeDtypeStruct(hshape, F32),
                   jax.ShapeDtypeStruct(hshape, F32), jax.ShapeDtypeStruct(hshape, F32),
                   jax.ShapeDtypeStruct((8, HEAD), F32), jax.ShapeDtypeStruct((8, A_WIDTH), F32)],
        scratch_shapes=[pltpu.VMEM((POOL_HALO, A_WIDTH), F32)],
        compiler_params=_seq_params(),
    )(dmixed, dx2, x, proj, proj, vecs, w_in_b, w_out_b, sgu_g, wm_b, wmt_b, bsb, wp_b, ps)


def _wgrad(a, b, tn, ts, name):
    s_len, m = a.shape
    n = b.shape[1]
    ts = min(ts, s_len)

    def body(a_ref, b_ref, o_ref):
        @pl.when(pl.program_id(1) == 0)
        def _():
            o_ref[...] = jnp.zeros_like(o_ref)

        o_ref[...] += _dot_tn(a_ref[...], b_ref[...])

    return pl.pallas_call(
        body, name=name, grid=(n // tn, s_len // ts),
        in_specs=[pl.BlockSpec((ts, m), lambda j, s: (s, 0)), pl.BlockSpec((ts, tn), lambda j, s: (s, j))],
        out_specs=pl.BlockSpec((m, tn), lambda j, s: (0, j)),
        out_shape=jax.ShapeDtypeStruct((m, n), F32),
        compiler_params=pltpu.CompilerParams(dimension_semantics=("parallel", "arbitrary"),
                                             vmem_limit_bytes=VMEM_LIMIT_BYTES),
    )(a, b)


def _adamw_big(g, w, m, v, name):
    r, cdim = g.shape
    tr = r
    while tr * cdim * 4 > (3 << 19) and tr % 16 == 0:
        tr //= 2

    def body(g_ref, w_ref, m_ref, v_ref, d_ref, nm_ref, nv_ref):
        delta, m2, v2 = _adamw_math(w_ref[0], g_ref[...], m_ref[0], v_ref[0])
        d_ref[0] = delta
        nm_ref[0] = m2
        nv_ref[0] = v2

    s3 = pl.BlockSpec((1, tr, cdim), lambda i: (0, i, 0))
    return pl.pallas_call(
        body, name=name, grid=(r // tr,),
        in_specs=[pl.BlockSpec((tr, cdim), lambda i: (i, 0)), s3, s3, s3],
        out_specs=[s3, s3, s3],
        out_shape=[jax.ShapeDtypeStruct(w.shape, F32)] * 3,
        compiler_params=pltpu.CompilerParams(dimension_semantics=("parallel",), vmem_limit_bytes=VMEM_LIMIT_BYTES),
    )(g, w, m, v)


def _wada_update(sct, gm, w, m, v):
    _, r, cdim = w.shape
    tr = 256
    kp = sct.shape[1]

    def body(s_ref, g_ref, w_ref, m_ref, v_ref, gw_ref, d_ref, nm_ref, nv_ref):
        g = _dot(s_ref[...], g_ref[...])
        gw_ref[0] = g
        delta, m2, v2 = _adamw_math(w_ref[0], g, m_ref[0], v_ref[0])
        d_ref[0] = delta
        nm_ref[0] = m2
        nv_ref[0] = v2

    s3 = pl.BlockSpec((1, tr, cdim), lambda i: (0, i, 0))
    return pl.pallas_call(
        body, name="wada_update", grid=(r // tr,),
        in_specs=[pl.BlockSpec((tr, kp), lambda i: (i, 0)), _whole(gm.shape), s3, s3, s3],
        out_specs=[s3, s3, s3, s3],
        out_shape=[jax.ShapeDtypeStruct(w.shape, F32)] * 4,
        compiler_params=pltpu.CompilerParams(dimension_semantics=("parallel",), vmem_limit_bytes=VMEM_LIMIT_BYTES),
    )(sct, gm, w, m, v)


def _small_update(g1, g2, g2s, gwm, gwp, gbz, gsg, gps, params):
    names = ["b_ada", "pre_mix_g", "post_mix_g", "sgu_norm_g", "w_spatial", "b_spatial", "w_pool", "pool_scale",
             "pre_ffn_g", "post_ffn_g", "conv_w", "conv_b"]
    d = g1.shape[2]
    flat_in = [g1, g2, g2s, gwm, gwp, gbz, gsg, gps]
    n_g = len(flat_in)
    for nm in names:
        flat_in += list(params[nm])

    def body(*refs):
        g1_ref, g2_ref, g2s_ref, gwm_ref, gwp_ref, gbz_ref, gsg_ref, gps_ref = refs[:n_g]
        wmv = refs[n_g:n_g + 3 * len(names)]
        loss_ref = refs[n_g + 3 * len(names)]
        outs = refs[n_g + 3 * len(names) + 1:]

        def dsum(ref, idx):
            acc = ref[(0,) + idx]
            for dev in range(1, N_DEV):
                acc = acc + ref[(dev,) + idx]
            return acc

        def apply(pi, g, widx, oidx):
            w_ref, m_ref, v_ref = wmv[3 * pi:3 * pi + 3]
            g_ref, d_ref, nm_ref, nv_ref = outs[4 * pi:4 * pi + 4]
            delta, m2, v2 = _adamw_math(w_ref[widx], g, m_ref[widx], v_ref[widx])
            g_ref[oidx] = g
            d_ref[oidx] = delta
            nm_ref[oidx] = m2
            nv_ref[oidx] = v2

        def row1(base, r):
            return (slice(base + r, base + r + 1), slice(None))

        tot = dsum(g1_ref, row1(0, F_LOSS))
        loss_ref[...] = jnp.zeros(loss_ref.shape, F32) + jnp.sum(tot) * (0.5 / d)
        mod_rows = [row1(16, M_DSHM), row1(16, M_DSCM), row1(8, B_DGM), row1(8, B_DSHF), row1(8, B_DSCF),
                    row1(0, F_DGF)]
        for j, rr in enumerate(mod_rows):
            cs = (slice(None), slice(j * d, (j + 1) * d))
            apply(0, dsum(g1_ref, rr), cs, cs)
        full = (slice(None), slice(None))
        apply(1, dsum(g1_ref, row1(16, M_DPREMIX)), full, full)
        apply(2, dsum(g1_ref, row1(8, B_DPOSTMIX)), full, full)
        apply(3, dsum(gsg_ref, (slice(0, N_HEADS), slice(None))), (0,), (0,))
        pos_i = lax.broadcasted_iota(jnp.int32, (HEAD, HEAD), 0)
        pos_j = lax.broadcasted_iota(jnp.int32, (HEAD, HEAD), 1)
        causal = (pos_j // CHUNK) <= (pos_i // CHUNK)
        for h in range(N_HEADS):
            blk = (slice(h * HEAD, (h + 1) * HEAD), slice(None))
            apply(4, jnp.where(causal, dsum(gwm_ref, blk), 0.0), (0, h), (0, h))
            bz = dsum(gbz_ref, blk)
            apply(5, jnp.sum(bz.T, axis=0, keepdims=True), (0, slice(h, h + 1)), (0, slice(h, h + 1)))
            apply(6, dsum(gwp_ref, blk), (0, h), (0, h))
        apply(7, dsum(gps_ref, (slice(0, 1), slice(None))), full, full)
        apply(8, dsum(g1_ref, row1(8, B_DPREFFN)), full, full)
        apply(9, dsum(g1_ref, row1(0, F_DPOSTFFN)), full, full)
        apply(10, dsum(g2s_ref, (slice(C_DCW, C_DCW + 3), slice(None))), (0,), (0,))
        apply(11, dsum(g2_ref, (slice(C_DCB, C_DCB + 1), slice(None))), full, full)

    out_shape = [jax.ShapeDtypeStruct((8, HEAD), F32)]
    for nm in names:
        out_shape += [jax.ShapeDtypeStruct(params[nm][0].shape, F32)] * 4
    res = pl.pallas_call(
        body, name="small_update", out_shape=out_shape,
        compiler_params=pltpu.CompilerParams(vmem_limit_bytes=VMEM_LIMIT_BYTES),
    )(*flat_in)
    out = {nm: tuple(res[1 + 4 * i:5 + 4 * i]) for i, nm in enumerate(names)}
    return res[0], out


def kernel(x, c, w_ada, b_ada, pre_mix_g, post_mix_g, w_in, sgu_norm_g, w_spatial, b_spatial, w_pool, pool_scale, w_out, pre_ffn_g, post_ffn_g, w_up, conv_w, conv_b, w_down, loss_target, m_w_ada, m_b_ada, m_pre_mix_g, m_post_mix_g, m_w_in, m_sgu_norm_g, m_w_spatial, m_b_spatial, m_w_pool, m_pool_scale, m_w_out, m_pre_ffn_g, m_post_ffn_g, m_w_up, m_conv_w, m_conv_b, m_w_down, v_w_ada, v_b_ada, v_pre_mix_g, v_post_mix_g, v_w_in, v_sgu_norm_g, v_w_spatial, v_b_spatial, v_w_pool, v_pool_scale, v_w_out, v_pre_ffn_g, v_post_ffn_g, v_w_up, v_conv_w, v_conv_b, v_w_down):
    xi, yi, ci = _mesh_pos()
    k_me = 2 * xi + yi
    dev = 2 * k_me + ci
    s_len, d = x.shape[1], x.shape[2]
    x2d = x[0]
    tgt = loss_target[0]
    ff2 = conv_b.shape[1]
    n_ada = w_ada.shape[2]
    n_cw = conv_w.shape[2]

    cw_blk = jnp.concatenate([conv_w[0], jnp.zeros((5, n_cw), F32)], axis=0)
    c_all, cw_all = _allgather_small([c.reshape(8, d // 8), cw_blk], "gather_c_convw")
    c_all = c_all.reshape(N_DEV, 8, d // 8).reshape(N_DEV, d)
    cw_full = jnp.concatenate([cw_all[16 * k:16 * k + 8] for k in range(N_CHIP)], axis=1)
    cvec = jnp.concatenate([cw_full[0:3], conv_b, jnp.zeros((4, ff2), F32)], axis=0)
    b_shard = lax.dynamic_slice_in_dim(b_ada, k_me * n_ada, n_ada, axis=1)
    mod_k, sc_all = _mod_shard(c_all, w_ada[0], b_shard)
    (mod_g,) = _allgather_small([mod_k], "gather_mod")
    mod_all = jnp.concatenate([mod_g[16 * k:16 * k + 8] for k in range(N_CHIP)], axis=1)
    mod_me = lax.dynamic_slice_in_dim(mod_all, dev, 1, axis=0).reshape(6, d)
    vecs = jnp.concatenate([mod_me, pre_mix_g, post_mix_g, pre_ffn_g, post_ffn_g, jnp.zeros((6, d), F32)], axis=0)

    col_flags = (True, False, True, False)
    w_in_b, w_out_b, w_up_b, w_down_b = _gather_weights([w_in[0], w_out[0], w_up[0], w_down[0]], col_flags)

    pos = jnp.arange(HEAD)
    causal = (pos[None, :] // CHUNK) <= (pos[:, None] // CHUNK)
    wm = jnp.where(causal[None], w_spatial[0], 0.0)
    wm_b = wm.astype(BF16)
    wmt_b = jnp.swapaxes(wm, 1, 2).astype(BF16)
    bsb = jnp.broadcast_to(b_spatial[0][:, :, None], (N_HEADS, HEAD, HEAD))
    wp_b = w_pool[0].astype(BF16)
    sgu_g = jnp.concatenate([sgu_norm_g[0], jnp.zeros((4, HEAD), F32)], axis=0)
    ps = jnp.concatenate([pool_scale, jnp.zeros((7, A_WIDTH), F32)], axis=0)

    pmats = _perm_mats(FFN_TS)
    h1, proj, cat, mixed, x2, h2p = _mix_fwd(x2d, vecs, w_in_b, w_out_b, sgu_g, wm_b, bsb, wp_b, ps, pmats, ts=512)
    up, yv, act, dy, dfp, acc_f = _ffn_fwd(h2p, x2, tgt, w_up_b, w_down_b, cvec, vecs, pmats, ts=FFN_TS)

    dup, dx2, dmixed, acc_c, acc_b = _ffn_bwd(dfp, up, yv, x2, dy, mixed, w_up_b, w_down_b, cvec, vecs, pmats,
                                              ts=FFN_TS)
    gx, dproj, acc_m, dwm, dwp, dbz, dsg, dps = _mix_bwd(
        dmixed, dx2, x2d, proj, vecs, w_in_b, w_out_b, sgu_g, wm_b, wmt_b, bsb, wp_b, ps, ts=256)
    g_w_down = _wgrad(act, dfp, d, 1024, "wgrad_down")
    g_w_up = _wgrad(h2p, dup, ff2 // 2, 2048, "wgrad_up")
    g_w_out = _wgrad(cat, dmixed, d, 2048, "wgrad_out")
    g_w_in = _wgrad(h1, dproj, proj.shape[1], 2048, "wgrad_in")

    grads = [g_w_in, g_w_out, g_w_up, g_w_down]
    landed = _reduce_to_sibling(grads, col_flags)
    parts = _sum_with_sibling(grads, landed, col_flags, ci.reshape(1).astype(jnp.int32))
    reduced = _sum_chips_and_share(_reduce_over_chips(parts))
    big = {}
    for nm, red, (w, m, v) in zip(("w_in", "w_out", "w_up", "w_down"), reduced,
                                  ((w_in, m_w_in, v_w_in), (w_out, m_w_out, v_w_out),
                                   (w_up, m_w_up, v_w_up), (w_down, m_w_down, v_w_down))):
        g = red.reshape(w.shape[1], w.shape[2])
        big[nm] = (g.reshape(w.shape),) + tuple(_adamw_big(g, w, m, v, "adamw_" + nm))

    g1 = jnp.concatenate([acc_f, acc_b, acc_m], axis=0)
    hflat = (N_HEADS * HEAD, HEAD)
    gathered = _allgather_small(
        [g1, acc_c, dwm.reshape(hflat), dwp.reshape(hflat), dbz.reshape(hflat), dsg, dps], "gather_small_grads")
    g1a, g2a, gwm, gwp, gbz, gsg, gps = [t.reshape((N_DEV, t.shape[0] // N_DEV, t.shape[1])) for t in gathered]
    g2s = lax.dynamic_slice_in_dim(g2a, k_me * n_cw, n_cw, axis=2)
    params = {
        "b_ada": (b_ada, m_b_ada, v_b_ada), "pre_mix_g": (pre_mix_g, m_pre_mix_g, v_pre_mix_g),
        "post_mix_g": (post_mix_g, m_post_mix_g, v_post_mix_g),
        "sgu_norm_g": (sgu_norm_g, m_sgu_norm_g, v_sgu_norm_g), "w_spatial": (w_spatial, m_w_spatial, v_w_spatial),
        "b_spatial": (b_spatial, m_b_spatial, v_b_spatial), "w_pool": (w_pool, m_w_pool, v_w_pool),
        "pool_scale": (pool_scale, m_pool_scale, v_pool_scale), "pre_ffn_g": (pre_ffn_g, m_pre_ffn_g, v_pre_ffn_g),
        "post_ffn_g": (post_ffn_g, m_post_ffn_g, v_post_ffn_g), "conv_w": (conv_w, m_conv_w, v_conv_w),
        "conv_b": (conv_b, m_conv_b, v_conv_b),
    }
    loss_slab, small = _small_update(g1a, g2a, g2s, gwm, gwp, gbz, gsg, gps, params)

    gmod_all = jnp.concatenate(
        [g1a[:, 16 + M_DSHM], g1a[:, 16 + M_DSCM], g1a[:, 8 + B_DGM], g1a[:, 8 + B_DSHF], g1a[:, 8 + B_DSCF],
         g1a[:, F_DGF]], axis=1)
    gm = lax.dynamic_slice_in_dim(gmod_all, k_me * n_ada, n_ada, axis=1)
    gm = jnp.concatenate([gm, jnp.zeros((HEAD - N_DEV, n_ada), F32)], axis=0)
    sct = jnp.concatenate([sc_all.T, jnp.zeros((d, HEAD - N_DEV), F32)], axis=1)
    ada = tuple(_wada_update(sct, gm, w_ada, m_w_ada, v_w_ada))

    everything = dict(small)
    everything.update(big)
    everything["w_ada"] = ada
    order = ["w_ada", "b_ada", "pre_mix_g", "post_mix_g", "w_in", "sgu_norm_g", "w_spatial", "b_spatial", "w_pool",
             "pool_scale", "w_out", "pre_ffn_g", "post_ffn_g", "w_up", "conv_w", "conv_b", "w_down"]
    outs = [loss_slab[0, 0], gx.reshape(x.shape)]
    for j in range(4):
        outs += [everything[nm][j] for nm in order]
    return tuple(outs)
```

```python
import functools

import jax
import jax.numpy as jnp
from jax import lax
from jax.experimental import pallas as pl
from jax.experimental.pallas import tpu as pltpu

F32 = jnp.float32
BF16 = jnp.bfloat16
MESH = pl.DeviceIdType.MESH

EPS = 1e-6
HEAD = 128
N_HEADS = 4
A_WIDTH = N_HEADS * HEAD
CHUNK = 64
POOL_WINDOWS = (2, 4, 8, 16)
POOL_HALO = 16
FFN_TS = 256

ADAM_LR = 0.001
ADAM_B1 = 0.9
ADAM_B2 = 0.999
ADAM_EPS = 1e-08
ADAM_WD = 0.01
ADAM_STEP = 10

VMEM_LIMIT_BYTES = 58 * 1024 * 1024
N_DEV = 8
N_CHIP = 4


def _dot(a, b):
    return jnp.dot(a, b, preferred_element_type=F32)


def _dot_nt(a, b):
    return lax.dot_general(a, b, (((1,), (1,)), ((), ())), preferred_element_type=F32)


def _dot_tn(a, b):
    return lax.dot_general(a, b, (((0,), (0,)), ((), ())), preferred_element_type=F32)


def _gelu(x):
    return x * (0.5 * (1.0 + jnp.tanh(0.7978845608028654 * (x + 0.044715 * (x * x * x)))))


def _gelu_grad(x):
    t = jnp.tanh(0.7978845608028654 * (x + 0.044715 * (x * x * x)))
    return 0.5 * (1.0 + t) + (0.5 * x) * (1.0 - t * t) * (0.7978845608028654 * (1.0 + 0.134145 * (x * x)))


def _sigmoid(x):
    return 1.0 / (1.0 + jnp.exp(-x))


def _rms(x):
    return lax.rsqrt(jnp.mean(x * x, axis=-1, keepdims=True) + EPS)


def _colsum(x):
    return jnp.sum(x, axis=0, keepdims=True)


def _rowmean(x):
    return jnp.mean(x, axis=-1, keepdims=True)


def _tiled(shape, index_map):
    return pl.BlockSpec(shape, index_map)


def _resident(shape):
    nd = len(shape)
    return pl.BlockSpec(shape, lambda *_: (0,) * nd, pipeline_mode=pl.Buffered(1))


def _whole(shape):
    nd = len(shape)
    return pl.BlockSpec(shape, lambda *_: (0,) * nd)


def _seq_params():
    return pltpu.CompilerParams(dimension_semantics=("arbitrary",), vmem_limit_bytes=VMEM_LIMIT_BYTES)


def _ff_chunks(f, width=768):
    out, o = [], 0
    while o < f:
        w = min(width, f - o)
        out.append((o, w))
        o += w
    return out


def _pool_fwd(p, halo, g, t_glob):
    ext = jnp.concatenate([halo, p], axis=0)
    s = ext
    for step in range(g + 1):
        s = s + pltpu.roll(s, 1 << step, 0)
    cnt = jnp.minimum(t_glob + 1, POOL_WINDOWS[g]).astype(F32)
    return s[POOL_HALO:] / cnt - p


def _adamw_math(w, g, m, v):
    m = ADAM_B1 * m + (1.0 - ADAM_B1) * g
    v = ADAM_B2 * v + (1.0 - ADAM_B2) * (g * g)
    m_hat = m / (1.0 - ADAM_B1 ** ADAM_STEP)
    v_hat = v / (1.0 - ADAM_B2 ** ADAM_STEP)
    delta = -ADAM_LR * (m_hat / (jnp.sqrt(v_hat) + ADAM_EPS) + ADAM_WD * w)
    return delta, m, v


def _mesh_pos():
    return lax.axis_index("x"), lax.axis_index("y"), lax.axis_index("c")


class _Rider:
    def __init__(self, inputs, out_shape, sems, start, finish):
        self.inputs, self.out_shape, self.sems = list(inputs), list(out_shape), list(sems)
        self.start, self.finish = start, finish


def _call(body, *, name, grid, in_specs, out_specs, out_shape, scratch_shapes, args, rider=None):
    params = pltpu.CompilerParams(dimension_semantics=("arbitrary",) * len(grid), vmem_limit_bytes=VMEM_LIMIT_BYTES)
    if rider is None:
        res = pl.pallas_call(body, name=name, grid=grid, in_specs=in_specs, out_specs=out_specs, out_shape=out_shape,
                             scratch_shapes=scratch_shapes, compiler_params=params)(*args)
        return tuple(res), ()
    cuts = [len(in_specs), len(rider.inputs), len(out_specs), len(rider.out_shape), len(scratch_shapes),
            len(rider.sems)]

    def hosted(*refs):
        groups, a = [], 0
        for cnt in cuts:
            groups.append(refs[a:a + cnt])
            a += cnt
        ins, r_in, outs, r_out, scr, r_sem = groups
        first = functools.reduce(jnp.logical_and, [pl.program_id(k) == 0 for k in range(len(grid))])
        last = functools.reduce(jnp.logical_and, [pl.program_id(k) == grid[k] - 1 for k in range(len(grid))])

        @pl.when(first)
        def _():
            rider.start(r_in, r_out, r_sem)

        body(*ins, *outs, *scr)

        @pl.when(last)
        def _():
            rider.finish(r_in, r_out, r_sem)

    anyspec = pl.BlockSpec(memory_space=pl.ANY)
    res = pl.pallas_call(
        hosted, name=name, grid=grid,
        in_specs=list(in_specs) + [anyspec] * cuts[1], out_specs=list(out_specs) + [anyspec] * cuts[3],
        out_shape=list(out_shape) + rider.out_shape, scratch_shapes=list(scratch_shapes) + rider.sems,
        compiler_params=params)(*args, *rider.inputs)
    return tuple(res[:cuts[2]]), tuple(res[cuts[2]:])


def _run_rider(rider, name):
    n_in, n_out = len(rider.inputs), len(rider.out_shape)

    def body(*refs):
        r_in, r_out, r_sem = refs[:n_in], refs[n_in:n_in + n_out], refs[n_in + n_out:]
        rider.start(r_in, r_out, r_sem)
        rider.finish(r_in, r_out, r_sem)

    anyspec = pl.BlockSpec(memory_space=pl.ANY)
    return pl.pallas_call(body, name=name, out_shape=rider.out_shape, in_specs=[anyspec] * n_in,
                          out_specs=[anyspec] * n_out, scratch_shapes=rider.sems)(*rider.inputs)


def _allgather_rider(arrs):
    n = len(arrs)

    def plan(ins, outs, sems):
        send_sems, recv_sems, local_sems = sems
        x, y, c = _mesh_pos()
        me, sibling = (x, y, c), (x, y, 1 - c)
        chips = [(1 - x, y), (x, 1 - y), (1 - x, 1 - y)]

        def rows(a, px, py, pc):
            r = ins[a].shape[0]
            return outs[a].at[pl.ds(pl.multiple_of((4 * px + 2 * py + pc) * r, 8), r), :]

        def copy(a, k, block, to, src=None):
            return pltpu.make_async_remote_copy(
                src_ref=rows(a, *block) if src is None else src, dst_ref=rows(a, *block),
                send_sem=send_sems.at[a * 7 + k], recv_sem=recv_sems.at[a * 7 + k],
                device_id=to, device_id_type=MESH)

        local = [pltpu.make_async_copy(ins[a], rows(a, *me), local_sems.at[a]) for a in range(n)]
        first = []
        for a in range(n):
            first.append(copy(a, 0, me, sibling, src=ins[a]))
            first += [copy(a, 1 + j, me, (*chip, c), src=ins[a]) for j, chip in enumerate(chips)]
        return c, me, sibling, chips, copy, local, first

    def start(ins, outs, sems):
        *_, local, first = plan(ins, outs, sems)
        for cp in local + first:
            cp.start()

    def finish(ins, outs, sems):
        c, me, sibling, chips, copy, local, first = plan(ins, outs, sems)
        passed = []
        for a in range(n):
            for j, chip in enumerate(chips):
                copy(a, 1 + j, (*chip, c), me).wait_recv()
                fwd = copy(a, 4 + j, (*chip, c), sibling)
                fwd.start()
                passed.append(fwd)
        for a in range(n):
            copy(a, 0, sibling, me).wait_recv()
            for j, chip in enumerate(chips):
                copy(a, 4 + j, (*chip, 1 - c), me).wait_recv()
        for cp in first + passed:
            cp.wait_send()
        for mine in local:
            mine.wait()

    return _Rider(arrs, [jax.ShapeDtypeStruct((N_DEV * a.shape[0], a.shape[1]), a.dtype) for a in arrs],
                  [pltpu.SemaphoreType.DMA((7 * n,)), pltpu.SemaphoreType.DMA((7 * n,)),
                   pltpu.SemaphoreType.DMA((n,))], start, finish)


def _piece(ref, col_sharded, k, h):
    m, n = ref.shape
    if col_sharded:
        mh, nc = m // 2, n // N_CHIP
        return ref.at[pl.ds(pl.multiple_of(h * mh, 16), mh), pl.ds(pl.multiple_of(k * nc, 128), nc)]
    rp = m // (2 * N_CHIP)
    return ref.at[pl.ds(pl.multiple_of((2 * k + h) * rp, 16), rp), :]


def _piece_shape(shape, col_sharded):
    m, n = shape
    return (m // 2, n // N_CHIP) if col_sharded else (m // (2 * N_CHIP), n)


def _cast_bf16(arrs):
    def body(*refs):
        for i_ref, o_ref in zip(refs[:len(arrs)], refs[len(arrs):]):
            o_ref[...] = i_ref[...].astype(BF16)

    return pl.pallas_call(body, name="cast_weights", out_shape=[jax.ShapeDtypeStruct(a.shape, BF16) for a in arrs],
                          compiler_params=pltpu.CompilerParams(vmem_limit_bytes=VMEM_LIMIT_BYTES))(*arrs)


def _gather_rider(shards, col_flags):
    n = len(shards)
    full_shapes = []
    for s, col in zip(shards, col_flags):
        full_shapes.append((s.shape[0], s.shape[1] * N_CHIP) if col else (s.shape[0] * N_CHIP, s.shape[1]))

    def plan(ins, outs, sems):
        send_sems, recv_sems, local_sems = sems
        x, y, c = _mesh_pos()
        sibling = (x, y, 1 - c)
        chips = [(1 - x, y), (x, 1 - y), (1 - x, 1 - y)]
        k_me = 2 * x + y

        def my_half(a):
            rows = ins[a].shape[0] // 2
            return ins[a].at[pl.ds(pl.multiple_of(c * rows, 16), rows), :]

        def own_window(a):
            m, nn = ins[a].shape
            if col_flags[a]:
                return outs[a].at[:, pl.ds(pl.multiple_of(k_me * nn, 128), nn)]
            return outs[a].at[pl.ds(pl.multiple_of(k_me * m, 16), m), :]

        def copy(a, s, src, k, h, to):
            return pltpu.make_async_remote_copy(
                src_ref=src, dst_ref=_piece(outs[a], col_flags[a], k, h),
                send_sem=send_sems.at[a * 6 + s], recv_sem=recv_sems.at[a * 6 + s],
                device_id=to, device_id_type=MESH)

        local = [pltpu.make_async_copy(ins[a], own_window(a), local_sems.at[a]) for a in range(n)]
        first = [copy(a, j, my_half(a), k_me, c, (*chip, c)) for a in range(n) for j, chip in enumerate(chips)]
        return c, sibling, chips, copy, local, first

    def start(ins, outs, sems):
        *_, local, first = plan(ins, outs, sems)
        for cp in local + first:
            cp.start()

    def finish(ins, outs, sems):
        c, sibling, chips, copy, local, first = plan(ins, outs, sems)
        passed = []
        for a in range(n):
            for j, chip in enumerate(chips):
                k_from = 2 * chip[0] + chip[1]
                win = _piece(outs[a], col_flags[a], k_from, c)
                copy(a, j, win, k_from, c, (*chip, c)).wait_recv()
                fwd = copy(a, 3 + j, win, k_from, c, sibling)
                fwd.start()
                passed.append(fwd)
        for a in range(n):
            for j, chip in enumerate(chips):
                k_from = 2 * chip[0] + chip[1]
                win = _piece(outs[a], col_flags[a], k_from, 1 - c)
                copy(a, 3 + j, win, k_from, 1 - c, sibling).wait_recv()
        for cp in first + passed:
            cp.wait_send()
        for mine in local:
            mine.wait()

    return _Rider(shards, [jax.ShapeDtypeStruct(fs, BF16) for fs in full_shapes],
                  [pltpu.SemaphoreType.DMA((6 * n,)), pltpu.SemaphoreType.DMA((6 * n,)),
                   pltpu.SemaphoreType.DMA((n,))], start, finish)


def _sibling_rider(grads, col_flags):
    n = len(grads)
    pshapes = [_piece_shape(g.shape, col) for g, col in zip(grads, col_flags)]

    def copies(ins, outs, sems):
        send_sems, recv_sems = sems
        x, y, c = _mesh_pos()
        return [pltpu.make_async_remote_copy(
            src_ref=_piece(ins[a], col_flags[a], k, 1 - c), dst_ref=outs[a].at[k],
            send_sem=send_sems.at[a * N_CHIP + k], recv_sem=recv_sems.at[a * N_CHIP + k],
            device_id=(x, y, 1 - c), device_id_type=MESH) for a in range(n) for k in range(N_CHIP)]

    def start(ins, outs, sems):
        for cp in copies(ins, outs, sems):
            cp.start()

    def finish(ins, outs, sems):
        cps = copies(ins, outs, sems)
        for cp in cps:
            cp.wait_recv()
        for cp in cps:
            cp.wait_send()

    return _Rider(grads, [jax.ShapeDtypeStruct((N_CHIP,) + ps, g.dtype) for ps, g in zip(pshapes, grads)],
                  [pltpu.SemaphoreType.DMA((N_CHIP * n,)), pltpu.SemaphoreType.DMA((N_CHIP * n,))], start, finish)


def _sum_with_sibling(grads, landed, col_flags, c_idx, name):
    n = len(grads)
    pshapes = [_piece_shape(g.shape, col) for g, col in zip(grads, col_flags)]

    def body(c_ref, *refs):
        ins, lands, outs = refs[:n], refs[n:2 * n], refs[2 * n:]
        for a in range(n):
            outs[a][0] = (ins[a][...] + lands[a][0]).astype(BF16)

    in_specs = []
    for ps, col in zip(pshapes, col_flags):
        if col:
            in_specs.append(pl.BlockSpec(ps, lambda k, c_ref: (c_ref[0], k)))
        else:
            in_specs.append(pl.BlockSpec(ps, lambda k, c_ref: (2 * k + c_ref[0], 0)))
    land_specs = [pl.BlockSpec((1,) + ps, lambda k, c_ref: (k, 0, 0)) for ps in pshapes]
    return pl.pallas_call(
        body, name=name,
        grid_spec=pltpu.PrefetchScalarGridSpec(
            num_scalar_prefetch=1, grid=(N_CHIP,),
            in_specs=in_specs + land_specs, out_specs=land_specs),
        out_shape=[jax.ShapeDtypeStruct((N_CHIP,) + ps, BF16) for ps in pshapes],
        compiler_params=pltpu.CompilerParams(dimension_semantics=("arbitrary",), vmem_limit_bytes=VMEM_LIMIT_BYTES),
    )(c_idx, *grads, *landed)


def _chips_rider(parts):
    n = len(parts)

    def plan(ins, outs, sems, arriving):
        send_sems, recv_sems, local_sems = sems
        x, y, c = _mesh_pos()
        chips = [(1 - x, y), (x, 1 - y), (1 - x, 1 - y)]
        k_me = 2 * x + y
        local = [pltpu.make_async_copy(ins[a].at[k_me], outs[a].at[k_me], local_sems.at[a]) for a in range(n)]
        copies = []
        for a in range(n):
            for j, chip in enumerate(chips):
                k_peer = 2 * chip[0] + chip[1]
                copies.append(pltpu.make_async_remote_copy(
                    src_ref=ins[a].at[k_peer], dst_ref=outs[a].at[k_peer if arriving else k_me],
                    send_sem=send_sems.at[a * 3 + j], recv_sem=recv_sems.at[a * 3 + j],
                    device_id=(*chip, c), device_id_type=MESH))
        return local, copies

    def start(ins, outs, sems):
        local, sends = plan(ins, outs, sems, False)
        for cp in local + sends:
            cp.start()

    def finish(ins, outs, sems):
        local, arrivals = plan(ins, outs, sems, True)
        for cp in arrivals:
            cp.wait_recv()
        for cp in arrivals:
            cp.wait_send()
        for mine in local:
            mine.wait()

    return _Rider(parts, [jax.ShapeDtypeStruct(p.shape, p.dtype) for p in parts],
                  [pltpu.SemaphoreType.DMA((3 * n,)), pltpu.SemaphoreType.DMA((3 * n,)),
                   pltpu.SemaphoreType.DMA((n,))], start, finish)


def _sum_chips_and_share(landed):
    n = len(landed)

    def body(*refs):
        ins, outs, red = refs[:n], refs[n:2 * n], refs[2 * n:3 * n]
        send_sems, recv_sems, local_sems = refs[3 * n:]
        x, y, c = _mesh_pos()
        sibling = (x, y, 1 - c)
        copies, local = [], []
        for a in range(n):
            red[a][...] = ((ins[a][0].astype(F32) + ins[a][1].astype(F32)) + ins[a][2].astype(F32)
                           + ins[a][3].astype(F32))
            mine = pltpu.make_async_copy(red[a], outs[a].at[c], local_sems.at[a])
            mine.start()
            local.append(mine)
            cp = pltpu.make_async_remote_copy(
                src_ref=red[a], dst_ref=outs[a].at[c],
                send_sem=send_sems.at[a], recv_sem=recv_sems.at[a],
                device_id=sibling, device_id_type=MESH)
            cp.start()
            copies.append(cp)
        for a in range(n):
            pltpu.make_async_remote_copy(
                src_ref=red[a], dst_ref=outs[a].at[1 - c],
                send_sem=send_sems.at[a], recv_sem=recv_sems.at[a],
                device_id=sibling, device_id_type=MESH).wait_recv()
        for cp in copies:
            cp.wait_send()
        for mine in local:
            mine.wait()

    return pl.pallas_call(
        body, name="sum_chips_and_share",
        out_shape=[jax.ShapeDtypeStruct((2,) + l.shape[1:], F32) for l in landed],
        in_specs=[pl.BlockSpec(memory_space=pltpu.VMEM)] * n,
        out_specs=[pl.BlockSpec(memory_space=pl.ANY)] * n,
        scratch_shapes=[pltpu.VMEM(l.shape[1:], F32) for l in landed]
        + [pltpu.SemaphoreType.DMA((n,)), pltpu.SemaphoreType.DMA((n,)), pltpu.SemaphoreType.DMA((n,))],
        compiler_params=pltpu.CompilerParams(vmem_limit_bytes=VMEM_LIMIT_BYTES),
    )(*landed)


def _mod_shard(c_all, w_ada, b_shard):
    def body(c_ref, w_ref, b_ref, o_ref, sc_ref):
        cc = c_ref[...]
        sc = cc * _sigmoid(cc)
        sc_ref[...] = sc
        o_ref[...] = _dot(sc, w_ref[...]) + b_ref[...]

    nb, d = c_all.shape
    nn = w_ada.shape[1]
    return pl.pallas_call(
        body, name="mod_shard",
        out_shape=[jax.ShapeDtypeStruct((nb, nn), F32), jax.ShapeDtypeStruct((nb, d), F32)],
        compiler_params=pltpu.CompilerParams(vmem_limit_bytes=VMEM_LIMIT_BYTES),
    )(c_all, w_ada, b_shard)


V_SH_M, V_SC_M, V_G_M, V_SH_F, V_SC_F, V_G_F, V_PRE_MIX, V_POST_MIX, V_PRE_FFN, V_POST_FFN = range(10)


def _vrow(vec_ref, r):
    return vec_ref[r:r + 1, :]


def _mix_fwd(x, vecs, w_in_b, w_out_b, sgu_g, wm_b, bsb, wp_b, ps, pmats, ts, rider=None):
    s_len, d = x.shape
    nt = s_len // ts
    nblk = ts // HEAD
    n_proj = w_in_b.shape[1]

    def body(x_ref, vec_ref, win_ref, wout_ref, sg_ref, wm_ref, bs_ref, wp_ref, ps_ref, pm_ref,
             h1_ref, proj_ref, cat_ref, mixed_ref, x2_ref, h2_ref, carry_ref):
        i = pl.program_id(0)

        @pl.when(i == 0)
        def _():
            carry_ref[...] = jnp.zeros_like(carry_ref)

        x = x_ref[...]
        h1 = (((x * _rms(x)) * _vrow(vec_ref, V_PRE_MIX)) * (1.0 + _vrow(vec_ref, V_SC_M))
              + _vrow(vec_ref, V_SH_M)).astype(BF16)
        h1_ref[...] = h1
        proj = _dot(h1, win_ref[...])
        proj_ref[...] = proj.astype(BF16)
        t_glob = lax.broadcasted_iota(jnp.int32, (ts, HEAD), 0) + i * ts
        for h in range(N_HEADS):
            u = _gelu(proj[:, h * HEAD:(h + 1) * HEAD])
            v = _gelu(proj[:, A_WIDTH + h * HEAD:A_WIDTH + (h + 1) * HEAD])
            vn = ((v * _rms(v)) * sg_ref[h:h + 1, :]).astype(BF16)
            for b in range(nblk):
                rs = slice(b * HEAD, (b + 1) * HEAD)
                z = _dot(wm_ref[h], vn[rs]) + bs_ref[h]
                cat_ref[rs, h * HEAD:(h + 1) * HEAD] = (u[rs] * z).astype(BF16)
        for g in range(len(POOL_WINDOWS)):
            gs = slice(g * HEAD, (g + 1) * HEAD)
            p = proj[:, 2 * A_WIDTH + g * HEAD:2 * A_WIDTH + (g + 1) * HEAD]
            pooled = _pool_fwd(p, carry_ref[:, gs], g, t_glob)
            yb = _dot(pooled.astype(BF16), wp_ref[g]) * ps_ref[0:1, gs]
            cat_ref[:, A_WIDTH + g * HEAD:A_WIDTH + (g + 1) * HEAD] = yb.astype(BF16)
        carry_ref[...] = proj[ts - POOL_HALO:ts, 2 * A_WIDTH:]
        mixed = _dot(cat_ref[...], wout_ref[...])
        mixed_ref[...] = mixed
        x2 = x + _vrow(vec_ref, V_G_M) * ((mixed * _rms(mixed)) * _vrow(vec_ref, V_POST_MIX))
        x2_ref[...] = x2
        h2 = (((x2 * _rms(x2)) * _vrow(vec_ref, V_PRE_FFN)) * (1.0 + _vrow(vec_ref, V_SC_F))
              + _vrow(vec_ref, V_SH_F)).astype(BF16)
        for b in range(ts // FFN_TS):
            rs = slice(b * FFN_TS, (b + 1) * FFN_TS)
            h2_ref[rs, :] = _permute_bf16(pm_ref[0], h2[rs])

    row = lambda i: (i, 0)
    return _call(
        body, name="mix_fwd", grid=(nt,), rider=rider,
        in_specs=[_tiled((ts, d), row), _whole(vecs.shape), _resident(w_in_b.shape), _resident(w_out_b.shape),
                  _whole(sgu_g.shape), _whole(wm_b.shape), _whole(bsb.shape), _whole(wp_b.shape), _whole(ps.shape),
                  _whole(pmats.shape)],
        out_specs=[_tiled((ts, d), row), _tiled((ts, n_proj), row), _tiled((ts, d), row),
                   _tiled((ts, d), row), _tiled((ts, d), row), _tiled((ts, d), row)],
        out_shape=[jax.ShapeDtypeStruct((s_len, d), BF16), jax.ShapeDtypeStruct((s_len, n_proj), BF16),
                   jax.ShapeDtypeStruct((s_len, d), BF16), jax.ShapeDtypeStruct((s_len, d), F32),
                   jax.ShapeDtypeStruct((s_len, d), F32), jax.ShapeDtypeStruct((s_len, d), BF16)],
        scratch_shapes=[pltpu.VMEM((POOL_HALO, A_WIDTH), F32)],
        args=(x, vecs, w_in_b, w_out_b, sgu_g, wm_b, bsb, wp_b, ps, pmats))


def _perm_mats(ts):
    p = jnp.arange(ts)
    pm = (((p % 8) * (ts // 8) + p // 8)[:, None] == p[None, :]).astype(BF16)
    return jnp.stack([pm, pm.T])


def _permute_bf16(pm, xb):
    return _dot(pm, xb).astype(BF16)


def _permute_f32(pm, x):
    hi = x.astype(BF16)
    lo = (x - hi.astype(F32)).astype(BF16)
    return _dot(pm, hi) + _dot(pm, lo)


def _conv_out(u, um2, um1, cv_ref, cols):
    return (cv_ref[3:4, cols] + um2 * cv_ref[0:1, cols] + um1 * cv_ref[1:2, cols] + u * cv_ref[2:3, cols])


F_LOSS, F_DGF, F_DPOSTFFN = 0, 1, 2
B_DSHF, B_DSCF, B_DPREFFN, B_DGM, B_DPOSTMIX = 0, 1, 2, 3, 4
M_DSHM, M_DSCM, M_DPREMIX = 0, 1, 2
C_DCB, C_DCW = 0, 1


def _ffn_fwd(h2p, x2, tgt, w_up_b, w_down_b, cvec, vecs, pmats, ts):
    s_len, d = x2.shape
    ff2 = w_up_b.shape[1]
    ff = ff2 // 2
    nt = s_len // ts
    chunks = _ff_chunks(ff)

    def body(h2_ref, x2_ref, t_ref, wu_ref, wd_ref, cv_ref, vec_ref, pm_ref,
             up_ref, y_ref, act_ref, dy_ref, df_ref, acc_ref, carry_ref):
        @pl.when(pl.program_id(0) == 0)
        def _():
            carry_ref[...] = jnp.zeros_like(carry_ref)
            acc_ref[...] = jnp.zeros_like(acc_ref)

        h2v = h2_ref[...]
        f = None
        for o, w in chunks:
            sub0 = lax.broadcasted_iota(jnp.int32, (8, w), 0) == 0
            ys = []
            for base in (0, ff):
                cols = slice(base + o, base + o + w)
                u = _dot(h2v, wu_ref[:, cols])
                up_ref[:, cols] = u.astype(BF16)
                b1 = jnp.where(sub0, pltpu.roll(carry_ref[8:16, cols], 1, 0), pltpu.roll(u[ts - 8:ts], 1, 0))
                b2 = jnp.where(sub0, pltpu.roll(carry_ref[0:8, cols], 1, 0), pltpu.roll(u[ts - 16:ts - 8], 1, 0))
                um1 = jnp.concatenate([b1, u[:ts - 8]], axis=0)
                um2 = jnp.concatenate([b2, b1, u[:ts - 16]], axis=0)
                yv = _conv_out(u, um2, um1, cv_ref, cols)
                y_ref[:, cols] = yv.astype(BF16)
                carry_ref[:, cols] = u[ts - 16:ts]
                ys.append(yv)
            gate, val = ys
            act = ((gate * _sigmoid(gate)) * val).astype(BF16)
            act_ref[:, o:o + w] = act
            part = _dot(act, wd_ref[o:o + w, :])
            f = part if f is None else f + part
        f = _permute_f32(pm_ref[1], f)
        r3 = _rms(f)
        fhat = f * r3
        post = _vrow(vec_ref, V_POST_FFN)
        g_f = _vrow(vec_ref, V_G_F)
        fn = fhat * post
        e = (x2_ref[...] + g_f * fn) - t_ref[...]
        dy = e * (1.0 / d)
        dy_ref[...] = dy
        dfn = dy * g_f
        acc_ref[F_LOSS:F_LOSS + 1, :] += _colsum(e * e)
        acc_ref[F_DGF:F_DGF + 1, :] += _colsum(dy * fn)
        acc_ref[F_DPOSTFFN:F_DPOSTFFN + 1, :] += _colsum(dfn * fhat)
        dfhat = dfn * post
        df = (r3 * (dfhat - fhat * _rowmean(dfhat * fhat))).astype(BF16)
        df_ref[...] = _permute_bf16(pm_ref[0], df)

    row = lambda i: (i, 0)
    return pl.pallas_call(
        body, name="ffn_fwd", grid=(nt,),
        in_specs=[_tiled((ts, d), row), _tiled((ts, d), row), _tiled((ts, d), row), _resident(w_up_b.shape),
                  _resident(w_down_b.shape), _whole(cvec.shape), _whole(vecs.shape), _whole(pmats.shape)],
        out_specs=[_tiled((ts, ff2), row), _tiled((ts, ff2), row), _tiled((ts, ff), row), _tiled((ts, d), row),
                   _tiled((ts, d), row), _whole((8, d))],
        out_shape=[jax.ShapeDtypeStruct((s_len, ff2), BF16), jax.ShapeDtypeStruct((s_len, ff2), BF16),
                   jax.ShapeDtypeStruct((s_len, ff), BF16), jax.ShapeDtypeStruct((s_len, d), F32),
                   jax.ShapeDtypeStruct((s_len, d), BF16), jax.ShapeDtypeStruct((8, d), F32)],
        scratch_shapes=[pltpu.VMEM((16, ff2), F32)],
        compiler_params=_seq_params(),
    )(h2p, x2, tgt, w_up_b, w_down_b, cvec, vecs, pmats)


def _ffn_bwd(dfp, upp, yp, x2, dy, mixed, w_up_b, w_down_b, cvec, vecs, pmats, ts, rider=None):
    s_len, d = x2.shape
    ff2 = w_up_b.shape[1]
    ff = ff2 // 2
    nt = s_len // ts
    chunks = _ff_chunks(ff, 512)

    def body(df_ref, up_ref, y_ref, x2_ref, dy_ref, mx_ref, wu_ref, wd_ref, cv_ref, vec_ref, pm_ref,
             dup_ref, dx2_ref, dmx_ref, accc_ref, acc_ref, carry_ref):
        @pl.when(pl.program_id(0) == 0)
        def _():
            carry_ref[...] = jnp.zeros_like(carry_ref)
            accc_ref[...] = jnp.zeros_like(accc_ref)
            acc_ref[...] = jnp.zeros_like(acc_ref)

        dfv = df_ref[...]
        dh2 = None
        for o, w in chunks:
            sub7 = lax.broadcasted_iota(jnp.int32, (8, w), 0) == 7
            dact = _dot_nt(dfv, wd_ref[o:o + w, :])
            gate = y_ref[:, o:o + w].astype(F32)
            val = y_ref[:, ff + o:ff + o + w].astype(F32)
            sg = _sigmoid(gate)
            gs = gate * sg
            dys = ((dact * val) * (sg + gs * (1.0 - sg)), dact * gs)
            for base, dyv in zip((0, ff), dys):
                cols = slice(base + o, base + o + w)
                u = up_ref[:, cols].astype(F32)
                e0 = jnp.where(sub7, pltpu.roll(carry_ref[0:8, cols], 7, 0), pltpu.roll(dyv[0:8], 7, 0))
                e1 = jnp.where(sub7, pltpu.roll(carry_ref[8:16, cols], 7, 0), pltpu.roll(dyv[8:16], 7, 0))
                dyp1 = jnp.concatenate([dyv[8:], e0], axis=0)
                dyp2 = jnp.concatenate([dyv[16:], e0, e1], axis=0)
                accc_ref[C_DCB:C_DCB + 1, cols] += _colsum(dyv)
                accc_ref[C_DCW + 0:C_DCW + 1, cols] += _colsum(dyp2 * u)
                accc_ref[C_DCW + 1:C_DCW + 2, cols] += _colsum(dyp1 * u)
                accc_ref[C_DCW + 2:C_DCW + 3, cols] += _colsum(dyv * u)
                dup = (dyv * cv_ref[2:3, cols] + dyp1 * cv_ref[1:2, cols] + dyp2 * cv_ref[0:1, cols]).astype(BF16)
                dup_ref[:, cols] = dup
                part = _dot_nt(dup, wu_ref[:, cols])
                dh2 = part if dh2 is None else dh2 + part
                carry_ref[:, cols] = dyv[0:16]
        dh2 = _permute_f32(pm_ref[1], dh2)
        x2 = x2_ref[...]
        r2 = _rms(x2)
        xn = x2 * r2
        pre = _vrow(vec_ref, V_PRE_FFN)
        one_sc = 1.0 + _vrow(vec_ref, V_SC_F)
        acc_ref[B_DSHF:B_DSHF + 1, :] += _colsum(dh2)
        acc_ref[B_DSCF:B_DSCF + 1, :] += _colsum(dh2 * (xn * pre))
        acc_ref[B_DPREFFN:B_DPREFFN + 1, :] += _colsum(dh2 * xn * one_sc)
        dxn = dh2 * pre * one_sc
        dx2 = dy_ref[...] + r2 * (dxn - xn * _rowmean(dxn * xn))
        dx2_ref[...] = dx2
        mixed = mx_ref[...]
        rm = _rms(mixed)
        mhat = mixed * rm
        post = _vrow(vec_ref, V_POST_MIX)
        acc_ref[B_DGM:B_DGM + 1, :] += _colsum(dx2 * (mhat * post))
        dmn = dx2 * _vrow(vec_ref, V_G_M)
        acc_ref[B_DPOSTMIX:B_DPOSTMIX + 1, :] += _colsum(dmn * mhat)
        dmhat = dmn * post
        dmx_ref[...] = (rm * (dmhat - mhat * _rowmean(dmhat * mhat))).astype(BF16)

    rev = lambda i: (nt - 1 - i, 0)
    return _call(
        body, name="ffn_bwd", grid=(nt,), rider=rider,
        in_specs=[_tiled((ts, d), rev), _tiled((ts, ff2), rev), _tiled((ts, ff2), rev), _tiled((ts, d), rev),
                  _tiled((ts, d), rev), _tiled((ts, d), rev), _resident(w_up_b.shape), _resident(w_down_b.shape),
                  _whole(cvec.shape), _whole(vecs.shape), _whole(pmats.shape)],
        out_specs=[_tiled((ts, ff2), rev), _tiled((ts, d), rev), _tiled((ts, d), rev), _whole((8, ff2)),
                   _whole((8, d))],
        out_shape=[jax.ShapeDtypeStruct((s_len, ff2), BF16), jax.ShapeDtypeStruct((s_len, d), F32),
                   jax.ShapeDtypeStruct((s_len, d), BF16), jax.ShapeDtypeStruct((8, ff2), F32),
                   jax.ShapeDtypeStruct((8, d), F32)],
        scratch_shapes=[pltpu.VMEM((16, ff2), F32)],
        args=(dfp, upp, yp, x2, dy, mixed, w_up_b, w_down_b, cvec, vecs, pmats))


def _mix_bwd(dmixed, dx2, x, proj, vecs, w_in_b, w_out_b, sgu_g, wm_b, wmt_b, bsb, wp_b, ps, ts, rider=None):
    s_len, d = x.shape
    nt = s_len // ts
    nblk = ts // HEAD
    n_proj = proj.shape[1]
    per = ts // POOL_HALO
    ext_len = ts + POOL_HALO

    def body(dmx_ref, dx2_ref, x_ref, proj_ref, projh_ref, vec_ref, win_ref, wout_ref, sg_ref, wm_ref, wmt_ref,
             bs_ref, wp_ref, ps_ref,
             gx_ref, dproj_ref, acc_ref, dwm_ref, dwp_ref, dbz_ref, dg_ref, dps_ref, carry_ref):
        i = pl.program_id(0)
        tile = nt - 1 - i

        @pl.when(i == 0)
        def _():
            carry_ref[...] = jnp.zeros_like(carry_ref)
            for r in (acc_ref, dwm_ref, dwp_ref, dbz_ref, dg_ref, dps_ref):
                r[...] = jnp.zeros_like(r)

        dcat = _dot_nt(dmx_ref[...], wout_ref[...])
        t_glob = lax.broadcasted_iota(jnp.int32, (ts, HEAD), 0) + tile * ts
        for h in range(N_HEADS):
            hs = slice(h * HEAD, (h + 1) * HEAD)
            vs = slice(A_WIDTH + h * HEAD, A_WIDTH + (h + 1) * HEAD)
            au = proj_ref[:, hs].astype(F32)
            av = proj_ref[:, vs].astype(F32)
            u = _gelu(au)
            v = _gelu(av)
            rv = _rms(v)
            vhat = v * rv
            gain = sg_ref[h:h + 1, :]
            vn = (vhat * gain).astype(BF16)
            dout = dcat[:, hs]
            du_parts, dvn_parts = [], []
            for b in range(nblk):
                rs = slice(b * HEAD, (b + 1) * HEAD)
                z = _dot(wm_ref[h], vn[rs]) + bs_ref[h]
                du_parts.append(dout[rs] * z)
                dz = dout[rs] * u[rs]
                dbz_ref[h] += dz
                dzb = dz.astype(BF16)
                dwm_ref[h] += _dot_nt(dzb, vn[rs])
                dvn_parts.append(_dot(wmt_ref[h], dzb))
            du = jnp.concatenate(du_parts, axis=0)
            dvn = jnp.concatenate(dvn_parts, axis=0)
            dg_ref[h:h + 1, :] += _colsum(dvn * vhat)
            dvhat = dvn * gain
            dv = rv * (dvhat - vhat * _rowmean(dvhat * vhat))
            dproj_ref[:, hs] = (du * _gelu_grad(au)).astype(BF16)
            dproj_ref[:, vs] = (dv * _gelu_grad(av)).astype(BF16)
        for g in range(len(POOL_WINDOWS)):
            gs = slice(g * HEAD, (g + 1) * HEAD)
            pcols = slice(2 * A_WIDTH + g * HEAD, 2 * A_WIDTH + (g + 1) * HEAD)
            p = proj_ref[:, pcols].astype(F32)
            halo = jnp.where(tile > 0, projh_ref[:, pcols].astype(F32), 0.0)
            pb = _pool_fwd(p, halo, g, t_glob).astype(BF16)
            dyb = dcat[:, A_WIDTH + g * HEAD:A_WIDTH + (g + 1) * HEAD]
            dps_ref[0:1, gs] += _colsum(dyb * _dot(pb, wp_ref[g]))
            dyl = (dyb * ps_ref[0:1, gs]).astype(BF16)
            dwp_ref[g] += _dot_tn(pb, dyl)
            dpooled = _dot_nt(dyl, wp_ref[g])
            cnt = jnp.minimum(t_glob + 1, POOL_WINDOWS[g]).astype(F32)
            q = dpooled / cnt
            s = jnp.concatenate([q, carry_ref[:, gs]], axis=0)
            for step in range(g + 1):
                s = s + pltpu.roll(s, ext_len - (1 << step), 0)
            dproj_ref[:, pcols] = (s[:ts] - dpooled).astype(BF16)
            carry_ref[:, gs] = q[0:POOL_HALO]
        dh1 = _dot_nt(dproj_ref[...], win_ref[...])
        x = x_ref[...]
        r1 = _rms(x)
        xn = x * r1
        pre = _vrow(vec_ref, V_PRE_MIX)
        one_sc = 1.0 + _vrow(vec_ref, V_SC_M)
        acc_ref[M_DSHM:M_DSHM + 1, :] += _colsum(dh1)
        acc_ref[M_DSCM:M_DSCM + 1, :] += _colsum(dh1 * (xn * pre))
        acc_ref[M_DPREMIX:M_DPREMIX + 1, :] += _colsum(dh1 * xn * one_sc)
        dxn = dh1 * pre * one_sc
        gx_ref[...] = dx2_ref[...] + r1 * (dxn - xn * _rowmean(dxn * xn))

    rev = lambda i: (nt - 1 - i, 0)
    halo_map = lambda i: (jnp.maximum((nt - 1 - i) * per - 1, 0), 0)
    hshape = (N_HEADS, HEAD, HEAD)
    return _call(
        body, name="mix_bwd", grid=(nt,), rider=rider,
        in_specs=[_tiled((ts, d), rev), _tiled((ts, d), rev), _tiled((ts, d), rev), _tiled((ts, n_proj), rev),
                  _tiled((POOL_HALO, n_proj), halo_map), _whole(vecs.shape), _resident(w_in_b.shape),
                  _resident(w_out_b.shape), _whole(sgu_g.shape), _whole(wm_b.shape), _whole(wmt_b.shape),
                  _whole(bsb.shape), _whole(wp_b.shape), _whole(ps.shape)],
        out_specs=[_tiled((ts, d), rev), _tiled((ts, n_proj), rev), _whole((8, d)), _whole(hshape), _whole(hshape),
                   _whole(hshape), _whole((8, HEAD)), _whole((8, A_WIDTH))],
        out_shape=[jax.ShapeDtypeStruct((s_len, d), F32), jax.ShapeDtypeStruct((s_len, n_proj), BF16),
                   jax.ShapeDtypeStruct((8, d), F32), jax.ShapeDtypeStruct(hshape, F32),
                   jax.ShapeDtypeStruct(hshape, F32), jax.ShapeDtypeStruct(hshape, F32),
                   jax.ShapeDtypeStruct((8, HEAD), F32), jax.ShapeDtypeStruct((8, A_WIDTH), F32)],
        scratch_shapes=[pltpu.VMEM((POOL_HALO, A_WIDTH), F32)],
        args=(dmixed, dx2, x, proj, proj, vecs, w_in_b, w_out_b, sgu_g, wm_b, wmt_b, bsb, wp_b, ps))


def _wgrad(a, b, tn, ts, name, rider=None):
    s_len, m = a.shape
    n = b.shape[1]
    ts = min(ts, s_len)

    def body(a_ref, b_ref, o_ref):
        @pl.when(pl.program_id(1) == 0)
        def _():
            o_ref[...] = jnp.zeros_like(o_ref)

        o_ref[...] += _dot_tn(a_ref[...], b_ref[...])

    (g,), r_out = _call(
        body, name=name, grid=(n // tn, s_len // ts), rider=rider,
        in_specs=[pl.BlockSpec((ts, m), lambda j, s: (s, 0)), pl.BlockSpec((ts, tn), lambda j, s: (s, j))],
        out_specs=[pl.BlockSpec((m, tn), lambda j, s: (0, j))],
        out_shape=[jax.ShapeDtypeStruct((m, n), F32)], scratch_shapes=[], args=(a, b))
    return g, r_out


def _wgrad_pair(a1, b1, a2, b2, ts, name, rider=None):
    s_len = a1.shape[0]
    ts = min(ts, s_len)
    shapes = [(a1.shape[1], b1.shape[1]), (a2.shape[1], b2.shape[1])]

    def body(a1_ref, b1_ref, a2_ref, b2_ref, o1_ref, o2_ref):
        @pl.when(pl.program_id(0) == 0)
        def _():
            o1_ref[...] = jnp.zeros_like(o1_ref)
            o2_ref[...] = jnp.zeros_like(o2_ref)

        o1_ref[...] += _dot_tn(a1_ref[...], b1_ref[...])
        o2_ref[...] += _dot_tn(a2_ref[...], b2_ref[...])

    row = lambda s: (s, 0)
    return _call(
        body, name=name, grid=(s_len // ts,), rider=rider,
        in_specs=[pl.BlockSpec((ts, t.shape[1]), row) for t in (a1, b1, a2, b2)],
        out_specs=[_whole(sh) for sh in shapes],
        out_shape=[jax.ShapeDtypeStruct(sh, F32) for sh in shapes], scratch_shapes=[], args=(a1, b1, a2, b2))


def _adamw_big(g, w, m, v, name):
    r, cdim = g.shape
    tr = r
    while tr * cdim * 4 > (3 << 19) and tr % 16 == 0:
        tr //= 2

    def body(g_ref, w_ref, m_ref, v_ref, d_ref, nm_ref, nv_ref):
        delta, m2, v2 = _adamw_math(w_ref[0], g_ref[...], m_ref[0], v_ref[0])
        d_ref[0] = delta
        nm_ref[0] = m2
        nv_ref[0] = v2

    s3 = pl.BlockSpec((1, tr, cdim), lambda i: (0, i, 0))
    return pl.pallas_call(
        body, name=name, grid=(r // tr,),
        in_specs=[pl.BlockSpec((tr, cdim), lambda i: (i, 0)), s3, s3, s3],
        out_specs=[s3, s3, s3],
        out_shape=[jax.ShapeDtypeStruct(w.shape, F32)] * 3,
        compiler_params=pltpu.CompilerParams(dimension_semantics=("parallel",), vmem_limit_bytes=VMEM_LIMIT_BYTES),
    )(g, w, m, v)


def _wada_update(sct, gm, w, m, v):
    _, r, cdim = w.shape
    tr = 256
    kp = sct.shape[1]

    def body(s_ref, g_ref, w_ref, m_ref, v_ref, gw_ref, d_ref, nm_ref, nv_ref):
        g = _dot(s_ref[...], g_ref[...])
        gw_ref[0] = g
        delta, m2, v2 = _adamw_math(w_ref[0], g, m_ref[0], v_ref[0])
        d_ref[0] = delta
        nm_ref[0] = m2
        nv_ref[0] = v2

    s3 = pl.BlockSpec((1, tr, cdim), lambda i: (0, i, 0))
    return pl.pallas_call(
        body, name="wada_update", grid=(r // tr,),
        in_specs=[pl.BlockSpec((tr, kp), lambda i: (i, 0)), _whole(gm.shape), s3, s3, s3],
        out_specs=[s3, s3, s3, s3],
        out_shape=[jax.ShapeDtypeStruct(w.shape, F32)] * 4,
        compiler_params=pltpu.CompilerParams(dimension_semantics=("parallel",), vmem_limit_bytes=VMEM_LIMIT_BYTES),
    )(sct, gm, w, m, v)


def _small_update(g1, g2, g2s, gwm, gwp, gbz, gsg, gps, params):
    names = ["b_ada", "pre_mix_g", "post_mix_g", "sgu_norm_g", "w_spatial", "b_spatial", "w_pool", "pool_scale",
             "pre_ffn_g", "post_ffn_g", "conv_w", "conv_b"]
    d = g1.shape[2]
    flat_in = [g1, g2, g2s, gwm, gwp, gbz, gsg, gps]
    n_g = len(flat_in)
    for nm in names:
        flat_in += list(params[nm])

    def body(*refs):
        g1_ref, g2_ref, g2s_ref, gwm_ref, gwp_ref, gbz_ref, gsg_ref, gps_ref = refs[:n_g]
        wmv = refs[n_g:n_g + 3 * len(names)]
        loss_ref = refs[n_g + 3 * len(names)]
        outs = refs[n_g + 3 * len(names) + 1:]

        def dsum(ref, idx):
            acc = ref[(0,) + idx]
            for dev in range(1, N_DEV):
                acc = acc + ref[(dev,) + idx]
            return acc

        def apply(pi, g, widx, oidx):
            w_ref, m_ref, v_ref = wmv[3 * pi:3 * pi + 3]
            g_ref, d_ref, nm_ref, nv_ref = outs[4 * pi:4 * pi + 4]
            delta, m2, v2 = _adamw_math(w_ref[widx], g, m_ref[widx], v_ref[widx])
            g_ref[oidx] = g
            d_ref[oidx] = delta
            nm_ref[oidx] = m2
            nv_ref[oidx] = v2

        def row1(base, r):
            return (slice(base + r, base + r + 1), slice(None))

        tot = dsum(g1_ref, row1(0, F_LOSS))
        loss_ref[...] = jnp.zeros(loss_ref.shape, F32) + jnp.sum(tot) * (0.5 / d)
        mod_rows = [row1(16, M_DSHM), row1(16, M_DSCM), row1(8, B_DGM), row1(8, B_DSHF), row1(8, B_DSCF),
                    row1(0, F_DGF)]
        for j, rr in enumerate(mod_rows):
            cs = (slice(None), slice(j * d, (j + 1) * d))
            apply(0, dsum(g1_ref, rr), cs, cs)
        full = (slice(None), slice(None))
        apply(1, dsum(g1_ref, row1(16, M_DPREMIX)), full, full)
        apply(2, dsum(g1_ref, row1(8, B_DPOSTMIX)), full, full)
        apply(3, dsum(gsg_ref, (slice(0, N_HEADS), slice(None))), (0,), (0,))
        pos_i = lax.broadcasted_iota(jnp.int32, (HEAD, HEAD), 0)
        pos_j = lax.broadcasted_iota(jnp.int32, (HEAD, HEAD), 1)
        causal = (pos_j // CHUNK) <= (pos_i // CHUNK)
        for h in range(N_HEADS):
            blk = (slice(h * HEAD, (h + 1) * HEAD), slice(None))
            apply(4, jnp.where(causal, dsum(gwm_ref, blk), 0.0), (0, h), (0, h))
            bz = dsum(gbz_ref, blk)
            apply(5, jnp.sum(bz.T, axis=0, keepdims=True), (0, slice(h, h + 1)), (0, slice(h, h + 1)))
            apply(6, dsum(gwp_ref, blk), (0, h), (0, h))
        apply(7, dsum(gps_ref, (slice(0, 1), slice(None))), full, full)
        apply(8, dsum(g1_ref, row1(8, B_DPREFFN)), full, full)
        apply(9, dsum(g1_ref, row1(0, F_DPOSTFFN)), full, full)
        apply(10, dsum(g2s_ref, (slice(C_DCW, C_DCW + 3), slice(None))), (0,), (0,))
        apply(11, dsum(g2_ref, (slice(C_DCB, C_DCB + 1), slice(None))), full, full)

    out_shape = [jax.ShapeDtypeStruct((8, HEAD), F32)]
    for nm in names:
        out_shape += [jax.ShapeDtypeStruct(params[nm][0].shape, F32)] * 4
    res = pl.pallas_call(
        body, name="small_update", out_shape=out_shape,
        compiler_params=pltpu.CompilerParams(vmem_limit_bytes=VMEM_LIMIT_BYTES),
    )(*flat_in)
    out = {nm: tuple(res[1 + 4 * i:5 + 4 * i]) for i, nm in enumerate(names)}
    return res[0], out


def kernel(x, c, w_ada, b_ada, pre_mix_g, post_mix_g, w_in, sgu_norm_g, w_spatial, b_spatial, w_pool, pool_scale, w_out, pre_ffn_g, post_ffn_g, w_up, conv_w, conv_b, w_down, loss_target, m_w_ada, m_b_ada, m_pre_mix_g, m_post_mix_g, m_w_in, m_sgu_norm_g, m_w_spatial, m_b_spatial, m_w_pool, m_pool_scale, m_w_out, m_pre_ffn_g, m_post_ffn_g, m_w_up, m_conv_w, m_conv_b, m_w_down, v_w_ada, v_b_ada, v_pre_mix_g, v_post_mix_g, v_w_in, v_sgu_norm_g, v_w_spatial, v_b_spatial, v_w_pool, v_pool_scale, v_w_out, v_pre_ffn_g, v_post_ffn_g, v_w_up, v_conv_w, v_conv_b, v_w_down):
    xi, yi, ci = _mesh_pos()
    k_me = 2 * xi + yi
    dev = 2 * k_me + ci
    s_len, d = x.shape[1], x.shape[2]
    x2d = x[0]
    tgt = loss_target[0]
    ff2 = conv_b.shape[1]
    n_ada = w_ada.shape[2]
    n_cw = conv_w.shape[2]

    cw_blk = jnp.concatenate([conv_w[0], jnp.zeros((5, n_cw), F32)], axis=0)
    c_all, cw_all = _run_rider(_allgather_rider([c.reshape(8, d // 8), cw_blk]), "gather_c_convw")
    c_all = c_all.reshape(N_DEV, 8, d // 8).reshape(N_DEV, d)
    cw_full = jnp.concatenate([cw_all[16 * k:16 * k + 8] for k in range(N_CHIP)], axis=1)
    cvec = jnp.concatenate([cw_full[0:3], conv_b, jnp.zeros((4, ff2), F32)], axis=0)
    b_shard = lax.dynamic_slice_in_dim(b_ada, k_me * n_ada, n_ada, axis=1)
    mod_k, sc_all = _mod_shard(c_all, w_ada[0], b_shard)
    (mod_g,) = _run_rider(_allgather_rider([mod_k]), "gather_mod")
    mod_all = jnp.concatenate([mod_g[16 * k:16 * k + 8] for k in range(N_CHIP)], axis=1)
    mod_me = lax.dynamic_slice_in_dim(mod_all, dev, 1, axis=0).reshape(6, d)
    vecs = jnp.concatenate([mod_me, pre_mix_g, post_mix_g, pre_ffn_g, post_ffn_g, jnp.zeros((6, d), F32)], axis=0)

    w_in_s, w_out_s, w_up_s, w_down_s = _cast_bf16([w_in[0], w_out[0], w_up[0], w_down[0]])
    w_in_b, w_out_b = _run_rider(_gather_rider([w_in_s, w_out_s], (True, False)), "gather_w_mix")

    pos = jnp.arange(HEAD)
    causal = (pos[None, :] // CHUNK) <= (pos[:, None] // CHUNK)
    wm = jnp.where(causal[None], w_spatial[0], 0.0)
    wm_b = wm.astype(BF16)
    wmt_b = jnp.swapaxes(wm, 1, 2).astype(BF16)
    bsb = jnp.broadcast_to(b_spatial[0][:, :, None], (N_HEADS, HEAD, HEAD))
    wp_b = w_pool[0].astype(BF16)
    sgu_g = jnp.concatenate([sgu_norm_g[0], jnp.zeros((4, HEAD), F32)], axis=0)
    ps = jnp.concatenate([pool_scale, jnp.zeros((7, A_WIDTH), F32)], axis=0)

    pmats = _perm_mats(FFN_TS)
    (h1, proj, cat, mixed, x2, h2p), (w_up_b, w_down_b) = _mix_fwd(
        x2d, vecs, w_in_b, w_out_b, sgu_g, wm_b, bsb, wp_b, ps, pmats, ts=512,
        rider=_gather_rider([w_up_s, w_down_s], (True, False)))
    up, yv, act, dy, dfp, acc_f = _ffn_fwd(h2p, x2, tgt, w_up_b, w_down_b, cvec, vecs, pmats, ts=FFN_TS)

    c_idx = ci.reshape(1).astype(jnp.int32)
    g_w_down, _ = _wgrad(act, dfp, d, 1024, "wgrad_down")
    (dup, dx2, dmixed, acc_c, acc_b), (land_down,) = _ffn_bwd(
        dfp, up, yv, x2, dy, mixed, w_up_b, w_down_b, cvec, vecs, pmats, ts=FFN_TS,
        rider=_sibling_rider([g_w_down], (False,)))
    (part_down,) = _sum_with_sibling([g_w_down], [land_down], (False,), c_idx, "pair_sum_down")
    g_w_up, (chips_down,) = _wgrad(h2p, dup, ff2 // 2, 2048, "wgrad_up", rider=_chips_rider([part_down]))
    (gx, dproj, acc_m, dwm, dwp, dbz, dsg, dps), (land_up,) = _mix_bwd(
        dmixed, dx2, x2d, proj, vecs, w_in_b, w_out_b, sgu_g, wm_b, wmt_b, bsb, wp_b, ps, ts=256,
        rider=_sibling_rider([g_w_up], (True,)))
    (part_up,) = _sum_with_sibling([g_w_up], [land_up], (True,), c_idx, "pair_sum_up")
    (g_w_out, g_w_in), (chips_up,) = _wgrad_pair(cat, dmixed, h1, dproj, 1024, "wgrad_mix",
                                                 rider=_chips_rider([part_up]))
    land_mix = _run_rider(_sibling_rider([g_w_in, g_w_out], (True, False)), "reduce_to_sibling")
    parts_mix = _sum_with_sibling([g_w_in, g_w_out], land_mix, (True, False), c_idx, "pair_sum_mix")
    chips_in, chips_out = _run_rider(_chips_rider(parts_mix), "reduce_over_chips")
    reduced = _sum_chips_and_share([chips_in, chips_out, chips_up, chips_down])
    big = {}
    for nm, red, (w, m, v) in zip(("w_in", "w_out", "w_up", "w_down"), reduced,
                                  ((w_in, m_w_in, v_w_in), (w_out, m_w_out, v_w_out),
                                   (w_up, m_w_up, v_w_up), (w_down, m_w_down, v_w_down))):
        g = red.reshape(w.shape[1], w.shape[2])
        big[nm] = (g.reshape(w.shape),) + tuple(_adamw_big(g, w, m, v, "adamw_" + nm))

    g1 = jnp.concatenate([acc_f, acc_b, acc_m], axis=0)
    hflat = (N_HEADS * HEAD, HEAD)
    gathered = _run_rider(_allgather_rider(
        [g1, acc_c, dwm.reshape(hflat), dwp.reshape(hflat), dbz.reshape(hflat), dsg, dps]), "gather_small_grads")
    g1a, g2a, gwm, gwp, gbz, gsg, gps = [t.reshape((N_DEV, t.shape[0] // N_DEV, t.shape[1])) for t in gathered]
    g2s = lax.dynamic_slice_in_dim(g2a, k_me * n_cw, n_cw, axis=2)
    params = {
        "b_ada": (b_ada, m_b_ada, v_b_ada), "pre_mix_g": (pre_mix_g, m_pre_mix_g, v_pre_mix_g),
        "post_mix_g": (post_mix_g, m_post_mix_g, v_post_mix_g),
        "sgu_norm_g": (sgu_norm_g, m_sgu_norm_g, v_sgu_norm_g), "w_spatial": (w_spatial, m_w_spatial, v_w_spatial),
        "b_spatial": (b_spatial, m_b_spatial, v_b_spatial), "w_pool": (w_pool, m_w_pool, v_w_pool),
        "pool_scale": (pool_scale, m_pool_scale, v_pool_scale), "pre_ffn_g": (pre_ffn_g, m_pre_ffn_g, v_pre_ffn_g),
        "post_ffn_g": (post_ffn_g, m_post_ffn_g, v_post_ffn_g), "conv_w": (conv_w, m_conv_w, v_conv_w),
        "conv_b": (conv_b, m_conv_b, v_conv_b),
    }
    loss_slab, small = _small_update(g1a, g2a, g2s, gwm, gwp, gbz, gsg, gps, params)

    gmod_all = jnp.concatenate(
        [g1a[:, 16 + M_DSHM], g1a[:, 16 + M_DSCM], g1a[:, 8 + B_DGM], g1a[:, 8 + B_DSHF], g1a[:, 8 + B_DSCF],
         g1a[:, F_DGF]], axis=1)
    gm = lax.dynamic_slice_in_dim(gmod_all, k_me * n_ada, n_ada, axis=1)
    gm = jnp.concatenate([gm, jnp.zeros((HEAD - N_DEV, n_ada), F32)], axis=0)
    sct = jnp.concatenate([sc_all.T, jnp.zeros((d, HEAD - N_DEV), F32)], axis=1)
    ada = tuple(_wada_update(sct, gm, w_ada, m_w_ada, v_w_ada))

    everything = dict(small)
    everything.update(big)
    everything["w_ada"] = ada
    order = ["w_ada", "b_ada", "pre_mix_g", "post_mix_g", "w_in", "sgu_norm_g", "w_spatial", "b_spatial", "w_pool",
             "pool_scale", "w_out", "pre_ffn_g", "post_ffn_g", "w_up", "conv_w", "conv_b", "w_down"]
    outs = [loss_slab[0, 0], gx.reshape(x.shape)]
    for j in range(4):
        outs += [everything[nm][j] for nm in order]
    return tuple(outs)
```

```python
import functools

import jax
import jax.numpy as jnp
from jax import lax
from jax.experimental import pallas as pl
from jax.experimental.pallas import tpu as pltpu

F32 = jnp.float32
BF16 = jnp.bfloat16
MESH = pl.DeviceIdType.MESH

EPS = 1e-6
HEAD = 128
N_HEADS = 4
A_WIDTH = N_HEADS * HEAD
CHUNK = 64
POOL_WINDOWS = (2, 4, 8, 16)
POOL_HALO = 16
FFN_TS = 256

ADAM_LR = 0.001
ADAM_B1 = 0.9
ADAM_B2 = 0.999
ADAM_EPS = 1e-08
ADAM_WD = 0.01
ADAM_STEP = 10

VMEM_LIMIT_BYTES = 58 * 1024 * 1024
N_DEV = 8
N_CHIP = 4


def _dot(a, b):
    return jnp.dot(a, b, preferred_element_type=F32)


def _dot_nt(a, b):
    return lax.dot_general(a, b, (((1,), (1,)), ((), ())), preferred_element_type=F32)


def _dot_tn(a, b):
    return lax.dot_general(a, b, (((0,), (0,)), ((), ())), preferred_element_type=F32)


def _gelu(x):
    return x * (0.5 * (1.0 + jnp.tanh(0.7978845608028654 * (x + 0.044715 * (x * x * x)))))


def _gelu_grad(x):
    t = jnp.tanh(0.7978845608028654 * (x + 0.044715 * (x * x * x)))
    return 0.5 * (1.0 + t) + (0.5 * x) * (1.0 - t * t) * (0.7978845608028654 * (1.0 + 0.134145 * (x * x)))


def _sigmoid(x):
    return 1.0 / (1.0 + jnp.exp(-x))


def _rms(x):
    return lax.rsqrt(jnp.mean(x * x, axis=-1, keepdims=True) + EPS)


def _colsum(x):
    return jnp.sum(x, axis=0, keepdims=True)


def _rowmean(x):
    return jnp.mean(x, axis=-1, keepdims=True)


def _tiled(shape, index_map):
    return pl.BlockSpec(shape, index_map)


def _resident(shape):
    nd = len(shape)
    return pl.BlockSpec(shape, lambda *_: (0,) * nd, pipeline_mode=pl.Buffered(1))


def _whole(shape):
    nd = len(shape)
    return pl.BlockSpec(shape, lambda *_: (0,) * nd)


def _seq_params():
    return pltpu.CompilerParams(dimension_semantics=("arbitrary",), vmem_limit_bytes=VMEM_LIMIT_BYTES)


def _ff_chunks(f, width=768):
    out, o = [], 0
    while o < f:
        w = min(width, f - o)
        out.append((o, w))
        o += w
    return out


def _pool_fwd(p, halo, g, t_glob):
    ext = jnp.concatenate([halo, p], axis=0)
    s = ext
    for step in range(g + 1):
        s = s + pltpu.roll(s, 1 << step, 0)
    cnt = jnp.minimum(t_glob + 1, POOL_WINDOWS[g]).astype(F32)
    return s[POOL_HALO:] / cnt - p


def _adamw_math(w, g, m, v):
    m = ADAM_B1 * m + (1.0 - ADAM_B1) * g
    v = ADAM_B2 * v + (1.0 - ADAM_B2) * (g * g)
    m_hat = m / (1.0 - ADAM_B1 ** ADAM_STEP)
    v_hat = v / (1.0 - ADAM_B2 ** ADAM_STEP)
    delta = -ADAM_LR * (m_hat / (jnp.sqrt(v_hat) + ADAM_EPS) + ADAM_WD * w)
    return delta, m, v


def _mesh_pos():
    return lax.axis_index("x"), lax.axis_index("y"), lax.axis_index("c")


class _Rider:
    def __init__(self, inputs, out_shape, sems, start, finish):
        self.inputs, self.out_shape, self.sems = list(inputs), list(out_shape), list(sems)
        self.start, self.finish = start, finish


def _call(body, *, name, grid, in_specs, out_specs, out_shape, scratch_shapes, args, rider=None):
    params = pltpu.CompilerParams(dimension_semantics=("arbitrary",) * len(grid), vmem_limit_bytes=VMEM_LIMIT_BYTES)
    if rider is None:
        res = pl.pallas_call(body, name=name, grid=grid, in_specs=in_specs, out_specs=out_specs, out_shape=out_shape,
                             scratch_shapes=scratch_shapes, compiler_params=params)(*args)
        return tuple(res), ()
    cuts = [len(in_specs), len(rider.inputs), len(out_specs), len(rider.out_shape), len(scratch_shapes),
            len(rider.sems)]

    def hosted(*refs):
        groups, a = [], 0
        for cnt in cuts:
            groups.append(refs[a:a + cnt])
            a += cnt
        ins, r_in, outs, r_out, scr, r_sem = groups
        first = functools.reduce(jnp.logical_and, [pl.program_id(k) == 0 for k in range(len(grid))])
        last = functools.reduce(jnp.logical_and, [pl.program_id(k) == grid[k] - 1 for k in range(len(grid))])

        @pl.when(first)
        def _():
            rider.start(r_in, r_out, r_sem)

        body(*ins, *outs, *scr)

        @pl.when(last)
        def _():
            rider.finish(r_in, r_out, r_sem)

    anyspec = pl.BlockSpec(memory_space=pl.ANY)
    res = pl.pallas_call(
        hosted, name=name, grid=grid,
        in_specs=list(in_specs) + [anyspec] * cuts[1], out_specs=list(out_specs) + [anyspec] * cuts[3],
        out_shape=list(out_shape) + rider.out_shape, scratch_shapes=list(scratch_shapes) + rider.sems,
        compiler_params=params)(*args, *rider.inputs)
    return tuple(res[:cuts[2]]), tuple(res[cuts[2]:])


def _run_rider(rider, name):
    n_in, n_out = len(rider.inputs), len(rider.out_shape)

    def body(*refs):
        r_in, r_out, r_sem = refs[:n_in], refs[n_in:n_in + n_out], refs[n_in + n_out:]
        rider.start(r_in, r_out, r_sem)
        rider.finish(r_in, r_out, r_sem)

    anyspec = pl.BlockSpec(memory_space=pl.ANY)
    return pl.pallas_call(body, name=name, out_shape=rider.out_shape, in_specs=[anyspec] * n_in,
                          out_specs=[anyspec] * n_out, scratch_shapes=rider.sems)(*rider.inputs)


def _allgather_rider(arrs):
    n = len(arrs)

    def plan(ins, outs, sems):
        send_sems, recv_sems, local_sems = sems
        x, y, c = _mesh_pos()
        me, sibling = (x, y, c), (x, y, 1 - c)
        chips = [(1 - x, y), (x, 1 - y), (1 - x, 1 - y)]

        def rows(a, px, py, pc):
            r = ins[a].shape[0]
            return outs[a].at[pl.ds(pl.multiple_of((4 * px + 2 * py + pc) * r, 8), r), :]

        def copy(a, k, block, to, src=None):
            return pltpu.make_async_remote_copy(
                src_ref=rows(a, *block) if src is None else src, dst_ref=rows(a, *block),
                send_sem=send_sems.at[a * 7 + k], recv_sem=recv_sems.at[a * 7 + k],
                device_id=to, device_id_type=MESH)

        local = [pltpu.make_async_copy(ins[a], rows(a, *me), local_sems.at[a]) for a in range(n)]
        first = []
        for a in range(n):
            first.append(copy(a, 0, me, sibling, src=ins[a]))
            first += [copy(a, 1 + j, me, (*chip, c), src=ins[a]) for j, chip in enumerate(chips)]
        return c, me, sibling, chips, copy, local, first

    def start(ins, outs, sems):
        *_, local, first = plan(ins, outs, sems)
        for cp in local + first:
            cp.start()

    def finish(ins, outs, sems):
        c, me, sibling, chips, copy, local, first = plan(ins, outs, sems)
        passed = []
        for a in range(n):
            for j, chip in enumerate(chips):
                copy(a, 1 + j, (*chip, c), me).wait_recv()
                fwd = copy(a, 4 + j, (*chip, c), sibling)
                fwd.start()
                passed.append(fwd)
        for a in range(n):
            copy(a, 0, sibling, me).wait_recv()
            for j, chip in enumerate(chips):
                copy(a, 4 + j, (*chip, 1 - c), me).wait_recv()
        for cp in first + passed:
            cp.wait_send()
        for mine in local:
            mine.wait()

    return _Rider(arrs, [jax.ShapeDtypeStruct((N_DEV * a.shape[0], a.shape[1]), a.dtype) for a in arrs],
                  [pltpu.SemaphoreType.DMA((7 * n,)), pltpu.SemaphoreType.DMA((7 * n,)),
                   pltpu.SemaphoreType.DMA((n,))], start, finish)


def _piece(ref, col_sharded, k, h):
    m, n = ref.shape
    if col_sharded:
        mh, nc = m // 2, n // N_CHIP
        return ref.at[pl.ds(pl.multiple_of(h * mh, 16), mh), pl.ds(pl.multiple_of(k * nc, 128), nc)]
    rp = m // (2 * N_CHIP)
    return ref.at[pl.ds(pl.multiple_of((2 * k + h) * rp, 16), rp), :]


def _piece_shape(shape, col_sharded):
    m, n = shape
    return (m // 2, n // N_CHIP) if col_sharded else (m // (2 * N_CHIP), n)


def _cast_bf16(arrs):
    def body(*refs):
        for i_ref, o_ref in zip(refs[:len(arrs)], refs[len(arrs):]):
            o_ref[...] = i_ref[...].astype(BF16)

    return pl.pallas_call(body, name="cast_weights", out_shape=[jax.ShapeDtypeStruct(a.shape, BF16) for a in arrs],
                          compiler_params=pltpu.CompilerParams(vmem_limit_bytes=VMEM_LIMIT_BYTES))(*arrs)


def _gather_rider(shards, col_flags):
    n = len(shards)
    full_shapes = []
    for s, col in zip(shards, col_flags):
        full_shapes.append((s.shape[0], s.shape[1] * N_CHIP) if col else (s.shape[0] * N_CHIP, s.shape[1]))

    def plan(ins, outs, sems):
        send_sems, recv_sems, local_sems = sems
        x, y, c = _mesh_pos()
        sibling = (x, y, 1 - c)
        chips = [(1 - x, y), (x, 1 - y), (1 - x, 1 - y)]
        k_me = 2 * x + y

        def my_half(a):
            rows = ins[a].shape[0] // 2
            return ins[a].at[pl.ds(pl.multiple_of(c * rows, 16), rows), :]

        def own_window(a):
            m, nn = ins[a].shape
            if col_flags[a]:
                return outs[a].at[:, pl.ds(pl.multiple_of(k_me * nn, 128), nn)]
            return outs[a].at[pl.ds(pl.multiple_of(k_me * m, 16), m), :]

        def copy(a, s, src, k, h, to):
            return pltpu.make_async_remote_copy(
                src_ref=src, dst_ref=_piece(outs[a], col_flags[a], k, h),
                send_sem=send_sems.at[a * 6 + s], recv_sem=recv_sems.at[a * 6 + s],
                device_id=to, device_id_type=MESH)

        local = [pltpu.make_async_copy(ins[a], own_window(a), local_sems.at[a]) for a in range(n)]
        first = [copy(a, j, my_half(a), k_me, c, (*chip, c)) for a in range(n) for j, chip in enumerate(chips)]
        return c, sibling, chips, copy, local, first

    def start(ins, outs, sems):
        *_, local, first = plan(ins, outs, sems)
        for cp in local + first:
            cp.start()

    def finish(ins, outs, sems):
        c, sibling, chips, copy, local, first = plan(ins, outs, sems)
        passed = []
        for a in range(n):
            for j, chip in enumerate(chips):
                k_from = 2 * chip[0] + chip[1]
                win = _piece(outs[a], col_flags[a], k_from, c)
                copy(a, j, win, k_from, c, (*chip, c)).wait_recv()
                fwd = copy(a, 3 + j, win, k_from, c, sibling)
                fwd.start()
                passed.append(fwd)
        for a in range(n):
            for j, chip in enumerate(chips):
                k_from = 2 * chip[0] + chip[1]
                win = _piece(outs[a], col_flags[a], k_from, 1 - c)
                copy(a, 3 + j, win, k_from, 1 - c, sibling).wait_recv()
        for cp in first + passed:
            cp.wait_send()
        for mine in local:
            mine.wait()

    return _Rider(shards, [jax.ShapeDtypeStruct(fs, BF16) for fs in full_shapes],
                  [pltpu.SemaphoreType.DMA((6 * n,)), pltpu.SemaphoreType.DMA((6 * n,)),
                   pltpu.SemaphoreType.DMA((n,))], start, finish)


def _sibling_rider(grads, col_flags):
    n = len(grads)
    pshapes = [_piece_shape(g.shape, col) for g, col in zip(grads, col_flags)]

    def copies(ins, outs, sems):
        send_sems, recv_sems = sems
        x, y, c = _mesh_pos()
        return [pltpu.make_async_remote_copy(
            src_ref=_piece(ins[a], col_flags[a], k, 1 - c), dst_ref=outs[a].at[k],
            send_sem=send_sems.at[a * N_CHIP + k], recv_sem=recv_sems.at[a * N_CHIP + k],
            device_id=(x, y, 1 - c), device_id_type=MESH) for a in range(n) for k in range(N_CHIP)]

    def start(ins, outs, sems):
        for cp in copies(ins, outs, sems):
            cp.start()

    def finish(ins, outs, sems):
        cps = copies(ins, outs, sems)
        for cp in cps:
            cp.wait_recv()
        for cp in cps:
            cp.wait_send()

    return _Rider(grads, [jax.ShapeDtypeStruct((N_CHIP,) + ps, g.dtype) for ps, g in zip(pshapes, grads)],
                  [pltpu.SemaphoreType.DMA((N_CHIP * n,)), pltpu.SemaphoreType.DMA((N_CHIP * n,))], start, finish)


def _sum_with_sibling(grads, landed, col_flags, c_idx, name):
    n = len(grads)
    pshapes = [_piece_shape(g.shape, col) for g, col in zip(grads, col_flags)]

    def body(c_ref, *refs):
        ins, lands, outs = refs[:n], refs[n:2 * n], refs[2 * n:]
        for a in range(n):
            outs[a][0] = (ins[a][...] + lands[a][0]).astype(BF16)

    in_specs = []
    for ps, col in zip(pshapes, col_flags):
        if col:
            in_specs.append(pl.BlockSpec(ps, lambda k, c_ref: (c_ref[0], k)))
        else:
            in_specs.append(pl.BlockSpec(ps, lambda k, c_ref: (2 * k + c_ref[0], 0)))
    land_specs = [pl.BlockSpec((1,) + ps, lambda k, c_ref: (k, 0, 0)) for ps in pshapes]
    return pl.pallas_call(
        body, name=name,
        grid_spec=pltpu.PrefetchScalarGridSpec(
            num_scalar_prefetch=1, grid=(N_CHIP,),
            in_specs=in_specs + land_specs, out_specs=land_specs),
        out_shape=[jax.ShapeDtypeStruct((N_CHIP,) + ps, BF16) for ps in pshapes],
        compiler_params=pltpu.CompilerParams(dimension_semantics=("arbitrary",), vmem_limit_bytes=VMEM_LIMIT_BYTES),
    )(c_idx, *grads, *landed)


def _chips_rider(parts):
    n = len(parts)

    def plan(ins, outs, sems, arriving):
        send_sems, recv_sems, local_sems = sems
        x, y, c = _mesh_pos()
        chips = [(1 - x, y), (x, 1 - y), (1 - x, 1 - y)]
        k_me = 2 * x + y
        local = [pltpu.make_async_copy(ins[a].at[k_me], outs[a].at[k_me], local_sems.at[a]) for a in range(n)]
        copies = []
        for a in range(n):
            for j, chip in enumerate(chips):
                k_peer = 2 * chip[0] + chip[1]
                copies.append(pltpu.make_async_remote_copy(
                    src_ref=ins[a].at[k_peer], dst_ref=outs[a].at[k_peer if arriving else k_me],
                    send_sem=send_sems.at[a * 3 + j], recv_sem=recv_sems.at[a * 3 + j],
                    device_id=(*chip, c), device_id_type=MESH))
        return local, copies

    def start(ins, outs, sems):
        local, sends = plan(ins, outs, sems, False)
        for cp in local + sends:
            cp.start()

    def finish(ins, outs, sems):
        local, arrivals = plan(ins, outs, sems, True)
        for cp in arrivals:
            cp.wait_recv()
        for cp in arrivals:
            cp.wait_send()
        for mine in local:
            mine.wait()

    return _Rider(parts, [jax.ShapeDtypeStruct(p.shape, p.dtype) for p in parts],
                  [pltpu.SemaphoreType.DMA((3 * n,)), pltpu.SemaphoreType.DMA((3 * n,)),
                   pltpu.SemaphoreType.DMA((n,))], start, finish)


def _sum_chips_and_share(landed):
    n = len(landed)

    def body(*refs):
        ins, outs, red = refs[:n], refs[n:2 * n], refs[2 * n:3 * n]
        send_sems, recv_sems, local_sems = refs[3 * n:]
        x, y, c = _mesh_pos()
        sibling = (x, y, 1 - c)
        copies, local = [], []
        for a in range(n):
            red[a][...] = ((ins[a][0].astype(F32) + ins[a][1].astype(F32)) + ins[a][2].astype(F32)
                           + ins[a][3].astype(F32))
            mine = pltpu.make_async_copy(red[a], outs[a].at[c], local_sems.at[a])
            mine.start()
            local.append(mine)
            cp = pltpu.make_async_remote_copy(
                src_ref=red[a], dst_ref=outs[a].at[c],
                send_sem=send_sems.at[a], recv_sem=recv_sems.at[a],
                device_id=sibling, device_id_type=MESH)
            cp.start()
            copies.append(cp)
        for a in range(n):
            pltpu.make_async_remote_copy(
                src_ref=red[a], dst_ref=outs[a].at[1 - c],
                send_sem=send_sems.at[a], recv_sem=recv_sems.at[a],
                device_id=sibling, device_id_type=MESH).wait_recv()
        for cp in copies:
            cp.wait_send()
        for mine in local:
            mine.wait()

    return pl.pallas_call(
        body, name="sum_chips_and_share",
        out_shape=[jax.ShapeDtypeStruct((2,) + l.shape[1:], F32) for l in landed],
        in_specs=[pl.BlockSpec(memory_space=pltpu.VMEM)] * n,
        out_specs=[pl.BlockSpec(memory_space=pl.ANY)] * n,
        scratch_shapes=[pltpu.VMEM(l.shape[1:], F32) for l in landed]
        + [pltpu.SemaphoreType.DMA((n,)), pltpu.SemaphoreType.DMA((n,)), pltpu.SemaphoreType.DMA((n,))],
        compiler_params=pltpu.CompilerParams(vmem_limit_bytes=VMEM_LIMIT_BYTES),
    )(*landed)


def _mod_shard(c_all, w_ada, b_shard):
    def body(c_ref, w_ref, b_ref, o_ref, sc_ref):
        cc = c_ref[...]
        sc = cc * _sigmoid(cc)
        sc_ref[...] = sc
        o_ref[...] = _dot(sc, w_ref[...]) + b_ref[...]

    nb, d = c_all.shape
    nn = w_ada.shape[1]
    return pl.pallas_call(
        body, name="mod_shard",
        out_shape=[jax.ShapeDtypeStruct((nb, nn), F32), jax.ShapeDtypeStruct((nb, d), F32)],
        compiler_params=pltpu.CompilerParams(vmem_limit_bytes=VMEM_LIMIT_BYTES),
    )(c_all, w_ada, b_shard)


V_SH_M, V_SC_M, V_G_M, V_SH_F, V_SC_F, V_G_F, V_PRE_MIX, V_POST_MIX, V_PRE_FFN, V_POST_FFN = range(10)


def _vrow(vec_ref, r):
    return vec_ref[r:r + 1, :]


def _mix_fwd(x, vecs, w_in_b, w_out_b, sgu_g, wm_b, bsb, wp_b, ps, pmats, ts, rider=None):
    s_len, d = x.shape
    nt = s_len // ts
    nblk = ts // HEAD
    n_proj = w_in_b.shape[1]

    def body(x_ref, vec_ref, win_ref, wout_ref, sg_ref, wm_ref, bs_ref, wp_ref, ps_ref, pm_ref,
             h1_ref, proj_ref, cat_ref, mixed_ref, x2_ref, h2_ref, carry_ref):
        i = pl.program_id(0)

        @pl.when(i == 0)
        def _():
            carry_ref[...] = jnp.zeros_like(carry_ref)

        x = x_ref[...]
        h1 = (((x * _rms(x)) * _vrow(vec_ref, V_PRE_MIX)) * (1.0 + _vrow(vec_ref, V_SC_M))
              + _vrow(vec_ref, V_SH_M)).astype(BF16)
        h1_ref[...] = h1
        proj = _dot(h1, win_ref[...])
        proj_ref[...] = proj.astype(BF16)
        t_glob = lax.broadcasted_iota(jnp.int32, (ts, HEAD), 0) + i * ts
        for h in range(N_HEADS):
            u = _gelu(proj[:, h * HEAD:(h + 1) * HEAD])
            v = _gelu(proj[:, A_WIDTH + h * HEAD:A_WIDTH + (h + 1) * HEAD])
            vn = ((v * _rms(v)) * sg_ref[h:h + 1, :]).astype(BF16)
            for b in range(nblk):
                rs = slice(b * HEAD, (b + 1) * HEAD)
                z = _dot(wm_ref[h], vn[rs]) + bs_ref[h]
                cat_ref[rs, h * HEAD:(h + 1) * HEAD] = (u[rs] * z).astype(BF16)
        for g in range(len(POOL_WINDOWS)):
            gs = slice(g * HEAD, (g + 1) * HEAD)
            p = proj[:, 2 * A_WIDTH + g * HEAD:2 * A_WIDTH + (g + 1) * HEAD]
            pooled = _pool_fwd(p, carry_ref[:, gs], g, t_glob)
            yb = _dot(pooled.astype(BF16), wp_ref[g]) * ps_ref[0:1, gs]
            cat_ref[:, A_WIDTH + g * HEAD:A_WIDTH + (g + 1) * HEAD] = yb.astype(BF16)
        carry_ref[...] = proj[ts - POOL_HALO:ts, 2 * A_WIDTH:]
        mixed = _dot(cat_ref[...], wout_ref[...])
        mixed_ref[...] = mixed
        x2 = x + _vrow(vec_ref, V_G_M) * ((mixed * _rms(mixed)) * _vrow(vec_ref, V_POST_MIX))
        x2_ref[...] = x2
        h2 = (((x2 * _rms(x2)) * _vrow(vec_ref, V_PRE_FFN)) * (1.0 + _vrow(vec_ref, V_SC_F))
              + _vrow(vec_ref, V_SH_F)).astype(BF16)
        for b in range(ts // FFN_TS):
            rs = slice(b * FFN_TS, (b + 1) * FFN_TS)
            h2_ref[rs, :] = _permute_bf16(pm_ref[0], h2[rs])

    row = lambda i: (i, 0)
    return _call(
        body, name="mix_fwd", grid=(nt,), rider=rider,
        in_specs=[_tiled((ts, d), row), _whole(vecs.shape), _resident(w_in_b.shape), _resident(w_out_b.shape),
                  _whole(sgu_g.shape), _whole(wm_b.shape), _whole(bsb.shape), _whole(wp_b.shape), _whole(ps.shape),
                  _whole(pmats.shape)],
        out_specs=[_tiled((ts, d), row), _tiled((ts, n_proj), row), _tiled((ts, d), row),
                   _tiled((ts, d), row), _tiled((ts, d), row), _tiled((ts, d), row)],
        out_shape=[jax.ShapeDtypeStruct((s_len, d), BF16), jax.ShapeDtypeStruct((s_len, n_proj), BF16),
                   jax.ShapeDtypeStruct((s_len, d), BF16), jax.ShapeDtypeStruct((s_len, d), F32),
                   jax.ShapeDtypeStruct((s_len, d), F32), jax.ShapeDtypeStruct((s_len, d), BF16)],
        scratch_shapes=[pltpu.VMEM((POOL_HALO, A_WIDTH), F32)],
        args=(x, vecs, w_in_b, w_out_b, sgu_g, wm_b, bsb, wp_b, ps, pmats))


def _perm_mats(ts):
    p = jnp.arange(ts)
    pm = (((p % 8) * (ts // 8) + p // 8)[:, None] == p[None, :]).astype(BF16)
    return jnp.stack([pm, pm.T])


def _permute_bf16(pm, xb):
    return _dot(pm, xb).astype(BF16)


def _permute_f32(pm, x):
    hi = x.astype(BF16)
    lo = (x - hi.astype(F32)).astype(BF16)
    return _dot(pm, hi) + _dot(pm, lo)


def _conv_out(u, um2, um1, cv_ref, cols):
    return (cv_ref[3:4, cols] + um2 * cv_ref[0:1, cols] + um1 * cv_ref[1:2, cols] + u * cv_ref[2:3, cols])


F_LOSS, F_DGF, F_DPOSTFFN = 0, 1, 2
B_DSHF, B_DSCF, B_DPREFFN, B_DGM, B_DPOSTMIX = 0, 1, 2, 3, 4
M_DSHM, M_DSCM, M_DPREMIX = 0, 1, 2
C_DCB, C_DCW = 0, 1


def _ffn_fwd(h2p, x2, tgt, w_up_b, w_down_b, cvec, vecs, pmats, ts):
    s_len, d = x2.shape
    ff2 = w_up_b.shape[1]
    ff = ff2 // 2
    nt = s_len // ts
    chunks = _ff_chunks(ff)

    def body(h2_ref, x2_ref, t_ref, wu_ref, wd_ref, cv_ref, vec_ref, pm_ref,
             up_ref, y_ref, act_ref, dy_ref, df_ref, acc_ref, carry_ref):
        @pl.when(pl.program_id(0) == 0)
        def _():
            carry_ref[...] = jnp.zeros_like(carry_ref)
            acc_ref[...] = jnp.zeros_like(acc_ref)

        h2v = h2_ref[...]

        def up_dots(o, w):
            return [_dot(h2v, wu_ref[:, base + o:base + o + w]) for base in (0, ff)]

        f = None
        pending = None
        nxt = up_dots(*chunks[0])
        for ci, (o, w) in enumerate(chunks):
            us = nxt
            if ci + 1 < len(chunks):
                nxt = up_dots(*chunks[ci + 1])
            if pending is not None:
                part = _dot(pending[0], wd_ref[pending[1]:pending[1] + pending[2], :])
                f = part if f is None else f + part
            sub0 = lax.broadcasted_iota(jnp.int32, (8, w), 0) == 0
            ys = []
            for base, u in zip((0, ff), us):
                cols = slice(base + o, base + o + w)
                up_ref[:, cols] = u.astype(BF16)
                b1 = jnp.where(sub0, pltpu.roll(carry_ref[8:16, cols], 1, 0), pltpu.roll(u[ts - 8:ts], 1, 0))
                b2 = jnp.where(sub0, pltpu.roll(carry_ref[0:8, cols], 1, 0), pltpu.roll(u[ts - 16:ts - 8], 1, 0))
                um1 = jnp.concatenate([b1, u[:ts - 8]], axis=0)
                um2 = jnp.concatenate([b2, b1, u[:ts - 16]], axis=0)
                yv = _conv_out(u, um2, um1, cv_ref, cols)
                y_ref[:, cols] = yv.astype(BF16)
                carry_ref[:, cols] = u[ts - 16:ts]
                ys.append(yv)
            gate, val = ys
            act = ((gate * _sigmoid(gate)) * val).astype(BF16)
            act_ref[:, o:o + w] = act
            pending = (act, o, w)
        f = f + _dot(pending[0], wd_ref[pending[1]:pending[1] + pending[2], :])
        f = _permute_f32(pm_ref[1], f)
        r3 = _rms(f)
        fhat = f * r3
        post = _vrow(vec_ref, V_POST_FFN)
        g_f = _vrow(vec_ref, V_G_F)
        fn = fhat * post
        e = (x2_ref[...] + g_f * fn) - t_ref[...]
        dy = e * (1.0 / d)
        dy_ref[...] = dy
        dfn = dy * g_f
        acc_ref[F_LOSS:F_LOSS + 1, :] += _colsum(e * e)
        acc_ref[F_DGF:F_DGF + 1, :] += _colsum(dy * fn)
        acc_ref[F_DPOSTFFN:F_DPOSTFFN + 1, :] += _colsum(dfn * fhat)
        dfhat = dfn * post
        df = (r3 * (dfhat - fhat * _rowmean(dfhat * fhat))).astype(BF16)
        df_ref[...] = _permute_bf16(pm_ref[0], df)

    row = lambda i: (i, 0)
    return pl.pallas_call(
        body, name="ffn_fwd", grid=(nt,),
        in_specs=[_tiled((ts, d), row), _tiled((ts, d), row), _tiled((ts, d), row), _resident(w_up_b.shape),
                  _resident(w_down_b.shape), _whole(cvec.shape), _whole(vecs.shape), _whole(pmats.shape)],
        out_specs=[_tiled((ts, ff2), row), _tiled((ts, ff2), row), _tiled((ts, ff), row), _tiled((ts, d), row),
                   _tiled((ts, d), row), _whole((8, d))],
        out_shape=[jax.ShapeDtypeStruct((s_len, ff2), BF16), jax.ShapeDtypeStruct((s_len, ff2), BF16),
                   jax.ShapeDtypeStruct((s_len, ff), BF16), jax.ShapeDtypeStruct((s_len, d), F32),
                   jax.ShapeDtypeStruct((s_len, d), BF16), jax.ShapeDtypeStruct((8, d), F32)],
        scratch_shapes=[pltpu.VMEM((16, ff2), F32)],
        compiler_params=_seq_params(),
    )(h2p, x2, tgt, w_up_b, w_down_b, cvec, vecs, pmats)


def _ffn_bwd(dfp, upp, yp, x2, dy, mixed, w_up_b, w_down_b, cvec, vecs, pmats, ts, rider=None):
    s_len, d = x2.shape
    ff2 = w_up_b.shape[1]
    ff = ff2 // 2
    nt = s_len // ts
    chunks = _ff_chunks(ff, 512)

    def body(df_ref, up_ref, y_ref, x2_ref, dy_ref, mx_ref, wu_ref, wd_ref, cv_ref, vec_ref, pm_ref,
             dup_ref, dx2_ref, dmx_ref, accc_ref, acc_ref, carry_ref):
        @pl.when(pl.program_id(0) == 0)
        def _():
            carry_ref[...] = jnp.zeros_like(carry_ref)
            accc_ref[...] = jnp.zeros_like(accc_ref)
            acc_ref[...] = jnp.zeros_like(acc_ref)

        dfv = df_ref[...]

        def dh2_add(acc, dups, o, w):
            for base, dup in zip((0, ff), dups):
                part = _dot_nt(dup, wu_ref[:, base + o:base + o + w])
                acc = part if acc is None else acc + part
            return acc

        dh2 = None
        pending = None
        nxt = _dot_nt(dfv, wd_ref[chunks[0][0]:chunks[0][0] + chunks[0][1], :])
        for ci, (o, w) in enumerate(chunks):
            dact = nxt
            if ci + 1 < len(chunks):
                o2, w2 = chunks[ci + 1]
                nxt = _dot_nt(dfv, wd_ref[o2:o2 + w2, :])
            if pending is not None:
                dh2 = dh2_add(dh2, *pending)
            sub7 = lax.broadcasted_iota(jnp.int32, (8, w), 0) == 7
            dups = []
            gate = y_ref[:, o:o + w].astype(F32)
            val = y_ref[:, ff + o:ff + o + w].astype(F32)
            sg = _sigmoid(gate)
            gs = gate * sg
            dys = ((dact * val) * (sg + gs * (1.0 - sg)), dact * gs)
            for base, dyv in zip((0, ff), dys):
                cols = slice(base + o, base + o + w)
                u = up_ref[:, cols].astype(F32)
                e0 = jnp.where(sub7, pltpu.roll(carry_ref[0:8, cols], 7, 0), pltpu.roll(dyv[0:8], 7, 0))
                e1 = jnp.where(sub7, pltpu.roll(carry_ref[8:16, cols], 7, 0), pltpu.roll(dyv[8:16], 7, 0))
                dyp1 = jnp.concatenate([dyv[8:], e0], axis=0)
                dyp2 = jnp.concatenate([dyv[16:], e0, e1], axis=0)
                accc_ref[C_DCB:C_DCB + 1, cols] += _colsum(dyv)
                accc_ref[C_DCW + 0:C_DCW + 1, cols] += _colsum(dyp2 * u)
                accc_ref[C_DCW + 1:C_DCW + 2, cols] += _colsum(dyp1 * u)
                accc_ref[C_DCW + 2:C_DCW + 3, cols] += _colsum(dyv * u)
                dup = (dyv * cv_ref[2:3, cols] + dyp1 * cv_ref[1:2, cols] + dyp2 * cv_ref[0:1, cols]).astype(BF16)
                dup_ref[:, cols] = dup
                dups.append(dup)
                carry_ref[:, cols] = dyv[0:16]
            pending = (dups, o, w)
        dh2 = dh2_add(dh2, *pending)
        dh2 = _permute_f32(pm_ref[1], dh2)
        x2 = x2_ref[...]
        r2 = _rms(x2)
        xn = x2 * r2
        pre = _vrow(vec_ref, V_PRE_FFN)
        one_sc = 1.0 + _vrow(vec_ref, V_SC_F)
        acc_ref[B_DSHF:B_DSHF + 1, :] += _colsum(dh2)
        acc_ref[B_DSCF:B_DSCF + 1, :] += _colsum(dh2 * (xn * pre))
        acc_ref[B_DPREFFN:B_DPREFFN + 1, :] += _colsum(dh2 * xn * one_sc)
        dxn = dh2 * pre * one_sc
        dx2 = dy_ref[...] + r2 * (dxn - xn * _rowmean(dxn * xn))
        dx2_ref[...] = dx2
        mixed = mx_ref[...]
        rm = _rms(mixed)
        mhat = mixed * rm
        post = _vrow(vec_ref, V_POST_MIX)
        acc_ref[B_DGM:B_DGM + 1, :] += _colsum(dx2 * (mhat * post))
        dmn = dx2 * _vrow(vec_ref, V_G_M)
        acc_ref[B_DPOSTMIX:B_DPOSTMIX + 1, :] += _colsum(dmn * mhat)
        dmhat = dmn * post
        dmx_ref[...] = (rm * (dmhat - mhat * _rowmean(dmhat * mhat))).astype(BF16)

    rev = lambda i: (nt - 1 - i, 0)
    return _call(
        body, name="ffn_bwd", grid=(nt,), rider=rider,
        in_specs=[_tiled((ts, d), rev), _tiled((ts, ff2), rev), _tiled((ts, ff2), rev), _tiled((ts, d), rev),
                  _tiled((ts, d), rev), _tiled((ts, d), rev), _resident(w_up_b.shape), _resident(w_down_b.shape),
                  _whole(cvec.shape), _whole(vecs.shape), _whole(pmats.shape)],
        out_specs=[_tiled((ts, ff2), rev), _tiled((ts, d), rev), _tiled((ts, d), rev), _whole((8, ff2)),
                   _whole((8, d))],
        out_shape=[jax.ShapeDtypeStruct((s_len, ff2), BF16), jax.ShapeDtypeStruct((s_len, d), F32),
                   jax.ShapeDtypeStruct((s_len, d), BF16), jax.ShapeDtypeStruct((8, ff2), F32),
                   jax.ShapeDtypeStruct((8, d), F32)],
        scratch_shapes=[pltpu.VMEM((16, ff2), F32)],
        args=(dfp, upp, yp, x2, dy, mixed, w_up_b, w_down_b, cvec, vecs, pmats))


def _mix_bwd(dmixed, dx2, x, proj, vecs, w_in_b, w_out_b, sgu_g, wm_b, wmt_b, bsb, wp_b, ps, ts, rider=None):
    s_len, d = x.shape
    nt = s_len // ts
    nblk = ts // HEAD
    n_proj = proj.shape[1]
    per = ts // POOL_HALO
    ext_len = ts + POOL_HALO

    def body(dmx_ref, dx2_ref, x_ref, proj_ref, projh_ref, vec_ref, win_ref, wout_ref, sg_ref, wm_ref, wmt_ref,
             bs_ref, wp_ref, ps_ref,
             gx_ref, dproj_ref, acc_ref, dwm_ref, dwp_ref, dbz_ref, dg_ref, dps_ref, carry_ref):
        i = pl.program_id(0)
        tile = nt - 1 - i

        @pl.when(i == 0)
        def _():
            carry_ref[...] = jnp.zeros_like(carry_ref)
            for r in (acc_ref, dwm_ref, dwp_ref, dbz_ref, dg_ref, dps_ref):
                r[...] = jnp.zeros_like(r)

        dcat = _dot_nt(dmx_ref[...], wout_ref[...])
        t_glob = lax.broadcasted_iota(jnp.int32, (ts, HEAD), 0) + tile * ts
        for h in range(N_HEADS):
            hs = slice(h * HEAD, (h + 1) * HEAD)
            vs = slice(A_WIDTH + h * HEAD, A_WIDTH + (h + 1) * HEAD)
            au = proj_ref[:, hs].astype(F32)
            av = proj_ref[:, vs].astype(F32)
            u = _gelu(au)
            v = _gelu(av)
            rv = _rms(v)
            vhat = v * rv
            gain = sg_ref[h:h + 1, :]
            vn = (vhat * gain).astype(BF16)
            dout = dcat[:, hs]
            du_parts, dvn_parts = [], []
            for b in range(nblk):
                rs = slice(b * HEAD, (b + 1) * HEAD)
                z = _dot(wm_ref[h], vn[rs]) + bs_ref[h]
                du_parts.append(dout[rs] * z)
                dz = dout[rs] * u[rs]
                dbz_ref[h] += dz
                dzb = dz.astype(BF16)
                dwm_ref[h] += _dot_nt(dzb, vn[rs])
                dvn_parts.append(_dot(wmt_ref[h], dzb))
            du = jnp.concatenate(du_parts, axis=0)
            dvn = jnp.concatenate(dvn_parts, axis=0)
            dg_ref[h:h + 1, :] += _colsum(dvn * vhat)
            dvhat = dvn * gain
            dv = rv * (dvhat - vhat * _rowmean(dvhat * vhat))
            dproj_ref[:, hs] = (du * _gelu_grad(au)).astype(BF16)
            dproj_ref[:, vs] = (dv * _gelu_grad(av)).astype(BF16)
        for g in range(len(POOL_WINDOWS)):
            gs = slice(g * HEAD, (g + 1) * HEAD)
            pcols = slice(2 * A_WIDTH + g * HEAD, 2 * A_WIDTH + (g + 1) * HEAD)
            p = proj_ref[:, pcols].astype(F32)
            halo = jnp.where(tile > 0, projh_ref[:, pcols].astype(F32), 0.0)
            pb = _pool_fwd(p, halo, g, t_glob).astype(BF16)
            dyb = dcat[:, A_WIDTH + g * HEAD:A_WIDTH + (g + 1) * HEAD]
            dps_ref[0:1, gs] += _colsum(dyb * _dot(pb, wp_ref[g]))
            dyl = (dyb * ps_ref[0:1, gs]).astype(BF16)
            dwp_ref[g] += _dot_tn(pb, dyl)
            dpooled = _dot_nt(dyl, wp_ref[g])
            cnt = jnp.minimum(t_glob + 1, POOL_WINDOWS[g]).astype(F32)
            q = dpooled / cnt
            s = jnp.concatenate([q, carry_ref[:, gs]], axis=0)
            for step in range(g + 1):
                s = s + pltpu.roll(s, ext_len - (1 << step), 0)
            dproj_ref[:, pcols] = (s[:ts] - dpooled).astype(BF16)
            carry_ref[:, gs] = q[0:POOL_HALO]
        dh1 = _dot_nt(dproj_ref[...], win_ref[...])
        x = x_ref[...]
        r1 = _rms(x)
        xn = x * r1
        pre = _vrow(vec_ref, V_PRE_MIX)
        one_sc = 1.0 + _vrow(vec_ref, V_SC_M)
        acc_ref[M_DSHM:M_DSHM + 1, :] += _colsum(dh1)
        acc_ref[M_DSCM:M_DSCM + 1, :] += _colsum(dh1 * (xn * pre))
        acc_ref[M_DPREMIX:M_DPREMIX + 1, :] += _colsum(dh1 * xn * one_sc)
        dxn = dh1 * pre * one_sc
        gx_ref[...] = dx2_ref[...] + r1 * (dxn - xn * _rowmean(dxn * xn))

    rev = lambda i: (nt - 1 - i, 0)
    halo_map = lambda i: (jnp.maximum((nt - 1 - i) * per - 1, 0), 0)
    hshape = (N_HEADS, HEAD, HEAD)
    return _call(
        body, name="mix_bwd", grid=(nt,), rider=rider,
        in_specs=[_tiled((ts, d), rev), _tiled((ts, d), rev), _tiled((ts, d), rev), _tiled((ts, n_proj), rev),
                  _tiled((POOL_HALO, n_proj), halo_map), _whole(vecs.shape), _resident(w_in_b.shape),
                  _resident(w_out_b.shape), _whole(sgu_g.shape), _whole(wm_b.shape), _whole(wmt_b.shape),
                  _whole(bsb.shape), _whole(wp_b.shape), _whole(ps.shape)],
        out_specs=[_tiled((ts, d), rev), _tiled((ts, n_proj), rev), _whole((8, d)), _whole(hshape), _whole(hshape),
                   _whole(hshape), _whole((8, HEAD)), _whole((8, A_WIDTH))],
        out_shape=[jax.ShapeDtypeStruct((s_len, d), F32), jax.ShapeDtypeStruct((s_len, n_proj), BF16),
                   jax.ShapeDtypeStruct((8, d), F32), jax.ShapeDtypeStruct(hshape, F32),
                   jax.ShapeDtypeStruct(hshape, F32), jax.ShapeDtypeStruct(hshape, F32),
                   jax.ShapeDtypeStruct((8, HEAD), F32), jax.ShapeDtypeStruct((8, A_WIDTH), F32)],
        scratch_shapes=[pltpu.VMEM((POOL_HALO, A_WIDTH), F32)],
        args=(dmixed, dx2, x, proj, proj, vecs, w_in_b, w_out_b, sgu_g, wm_b, wmt_b, bsb, wp_b, ps))


def _wgrad(a, b, tn, ts, name, rider=None):
    s_len, m = a.shape
    n = b.shape[1]
    ts = min(ts, s_len)

    def body(a_ref, b_ref, o_ref):
        @pl.when(pl.program_id(1) == 0)
        def _():
            o_ref[...] = jnp.zeros_like(o_ref)

        o_ref[...] += _dot_tn(a_ref[...], b_ref[...])

    (g,), r_out = _call(
        body, name=name, grid=(n // tn, s_len // ts), rider=rider,
        in_specs=[pl.BlockSpec((ts, m), lambda j, s: (s, 0)), pl.BlockSpec((ts, tn), lambda j, s: (s, j))],
        out_specs=[pl.BlockSpec((m, tn), lambda j, s: (0, j))],
        out_shape=[jax.ShapeDtypeStruct((m, n), F32)], scratch_shapes=[], args=(a, b))
    return g, r_out


def _wgrad_pair(a1, b1, a2, b2, ts, name, rider=None):
    s_len = a1.shape[0]
    ts = min(ts, s_len)
    shapes = [(a1.shape[1], b1.shape[1]), (a2.shape[1], b2.shape[1])]

    def body(a1_ref, b1_ref, a2_ref, b2_ref, o1_ref, o2_ref):
        @pl.when(pl.program_id(0) == 0)
        def _():
            o1_ref[...] = jnp.zeros_like(o1_ref)
            o2_ref[...] = jnp.zeros_like(o2_ref)

        o1_ref[...] += _dot_tn(a1_ref[...], b1_ref[...])
        o2_ref[...] += _dot_tn(a2_ref[...], b2_ref[...])

    row = lambda s: (s, 0)
    return _call(
        body, name=name, grid=(s_len // ts,), rider=rider,
        in_specs=[pl.BlockSpec((ts, t.shape[1]), row) for t in (a1, b1, a2, b2)],
        out_specs=[_whole(sh) for sh in shapes],
        out_shape=[jax.ShapeDtypeStruct(sh, F32) for sh in shapes], scratch_shapes=[], args=(a1, b1, a2, b2))


def _adamw_big(g, w, m, v, name):
    r, cdim = g.shape
    tr = r
    while tr * cdim * 4 > (3 << 19) and tr % 16 == 0:
        tr //= 2

    def body(g_ref, w_ref, m_ref, v_ref, d_ref, nm_ref, nv_ref):
        delta, m2, v2 = _adamw_math(w_ref[0], g_ref[...], m_ref[0], v_ref[0])
        d_ref[0] = delta
        nm_ref[0] = m2
        nv_ref[0] = v2

    s3 = pl.BlockSpec((1, tr, cdim), lambda i: (0, i, 0))
    return pl.pallas_call(
        body, name=name, grid=(r // tr,),
        in_specs=[pl.BlockSpec((tr, cdim), lambda i: (i, 0)), s3, s3, s3],
        out_specs=[s3, s3, s3],
        out_shape=[jax.ShapeDtypeStruct(w.shape, F32)] * 3,
        compiler_params=pltpu.CompilerParams(dimension_semantics=("parallel",), vmem_limit_bytes=VMEM_LIMIT_BYTES),
    )(g, w, m, v)


def _wada_update(sct, gm, w, m, v):
    _, r, cdim = w.shape
    tr = 256
    kp = sct.shape[1]

    def body(s_ref, g_ref, w_ref, m_ref, v_ref, gw_ref, d_ref, nm_ref, nv_ref):
        g = _dot(s_ref[...], g_ref[...])
        gw_ref[0] = g
        delta, m2, v2 = _adamw_math(w_ref[0], g, m_ref[0], v_ref[0])
        d_ref[0] = delta
        nm_ref[0] = m2
        nv_ref[0] = v2

    s3 = pl.BlockSpec((1, tr, cdim), lambda i: (0, i, 0))
    return pl.pallas_call(
        body, name="wada_update", grid=(r // tr,),
        in_specs=[pl.BlockSpec((tr, kp), lambda i: (i, 0)), _whole(gm.shape), s3, s3, s3],
        out_specs=[s3, s3, s3, s3],
        out_shape=[jax.ShapeDtypeStruct(w.shape, F32)] * 4,
        compiler_params=pltpu.CompilerParams(dimension_semantics=("parallel",), vmem_limit_bytes=VMEM_LIMIT_BYTES),
    )(sct, gm, w, m, v)


def _small_update(g1, g2, g2s, gwm, gwp, gbz, gsg, gps, params):
    names = ["b_ada", "pre_mix_g", "post_mix_g", "sgu_norm_g", "w_spatial", "b_spatial", "w_pool", "pool_scale",
             "pre_ffn_g", "post_ffn_g", "conv_w", "conv_b"]
    d = g1.shape[2]
    flat_in = [g1, g2, g2s, gwm, gwp, gbz, gsg, gps]
    n_g = len(flat_in)
    for nm in names:
        flat_in += list(params[nm])

    def body(*refs):
        g1_ref, g2_ref, g2s_ref, gwm_ref, gwp_ref, gbz_ref, gsg_ref, gps_ref = refs[:n_g]
        wmv = refs[n_g:n_g + 3 * len(names)]
        loss_ref = refs[n_g + 3 * len(names)]
        outs = refs[n_g + 3 * len(names) + 1:]

        def dsum(ref, idx):
            acc = ref[(0,) + idx]
            for dev in range(1, N_DEV):
                acc = acc + ref[(dev,) + idx]
            return acc

        def apply(pi, g, widx, oidx):
            w_ref, m_ref, v_ref = wmv[3 * pi:3 * pi + 3]
            g_ref, d_ref, nm_ref, nv_ref = outs[4 * pi:4 * pi + 4]
            delta, m2, v2 = _adamw_math(w_ref[widx], g, m_ref[widx], v_ref[widx])
            g_ref[oidx] = g
            d_ref[oidx] = delta
            nm_ref[oidx] = m2
            nv_ref[oidx] = v2

        def row1(base, r):
            return (slice(base + r, base + r + 1), slice(None))

        tot = dsum(g1_ref, row1(0, F_LOSS))
        loss_ref[...] = jnp.zeros(loss_ref.shape, F32) + jnp.sum(tot) * (0.5 / d)
        mod_rows = [row1(16, M_DSHM), row1(16, M_DSCM), row1(8, B_DGM), row1(8, B_DSHF), row1(8, B_DSCF),
                    row1(0, F_DGF)]
        for j, rr in enumerate(mod_rows):
            cs = (slice(None), slice(j * d, (j + 1) * d))
            apply(0, dsum(g1_ref, rr), cs, cs)
        full = (slice(None), slice(None))
        apply(1, dsum(g1_ref, row1(16, M_DPREMIX)), full, full)
        apply(2, dsum(g1_ref, row1(8, B_DPOSTMIX)), full, full)
        apply(3, dsum(gsg_ref, (slice(0, N_HEADS), slice(None))), (0,), (0,))
        pos_i = lax.broadcasted_iota(jnp.int32, (HEAD, HEAD), 0)
        pos_j = lax.broadcasted_iota(jnp.int32, (HEAD, HEAD), 1)
        causal = (pos_j // CHUNK) <= (pos_i // CHUNK)
        for h in range(N_HEADS):
            blk = (slice(h * HEAD, (h + 1) * HEAD), slice(None))
            apply(4, jnp.where(causal, dsum(gwm_ref, blk), 0.0), (0, h), (0, h))
            bz = dsum(gbz_ref, blk)
            apply(5, jnp.sum(bz.T, axis=0, keepdims=True), (0, slice(h, h + 1)), (0, slice(h, h + 1)))
            apply(6, dsum(gwp_ref, blk), (0, h), (0, h))
        apply(7, dsum(gps_ref, (slice(0, 1), slice(None))), full, full)
        apply(8, dsum(g1_ref, row1(8, B_DPREFFN)), full, full)
        apply(9, dsum(g1_ref, row1(0, F_DPOSTFFN)), full, full)
        apply(10, dsum(g2s_ref, (slice(C_DCW, C_DCW + 3), slice(None))), (0,), (0,))
        apply(11, dsum(g2_ref, (slice(C_DCB, C_DCB + 1), slice(None))), full, full)

    out_shape = [jax.ShapeDtypeStruct((8, HEAD), F32)]
    for nm in names:
        out_shape += [jax.ShapeDtypeStruct(params[nm][0].shape, F32)] * 4
    res = pl.pallas_call(
        body, name="small_update", out_shape=out_shape,
        compiler_params=pltpu.CompilerParams(vmem_limit_bytes=VMEM_LIMIT_BYTES),
    )(*flat_in)
    out = {nm: tuple(res[1 + 4 * i:5 + 4 * i]) for i, nm in enumerate(names)}
    return res[0], out


def kernel(x, c, w_ada, b_ada, pre_mix_g, post_mix_g, w_in, sgu_norm_g, w_spatial, b_spatial, w_pool, pool_scale, w_out, pre_ffn_g, post_ffn_g, w_up, conv_w, conv_b, w_down, loss_target, m_w_ada, m_b_ada, m_pre_mix_g, m_post_mix_g, m_w_in, m_sgu_norm_g, m_w_spatial, m_b_spatial, m_w_pool, m_pool_scale, m_w_out, m_pre_ffn_g, m_post_ffn_g, m_w_up, m_conv_w, m_conv_b, m_w_down, v_w_ada, v_b_ada, v_pre_mix_g, v_post_mix_g, v_w_in, v_sgu_norm_g, v_w_spatial, v_b_spatial, v_w_pool, v_pool_scale, v_w_out, v_pre_ffn_g, v_post_ffn_g, v_w_up, v_conv_w, v_conv_b, v_w_down):
    xi, yi, ci = _mesh_pos()
    k_me = 2 * xi + yi
    dev = 2 * k_me + ci
    s_len, d = x.shape[1], x.shape[2]
    x2d = x[0]
    tgt = loss_target[0]
    ff2 = conv_b.shape[1]
    n_ada = w_ada.shape[2]
    n_cw = conv_w.shape[2]

    cw_blk = jnp.concatenate([conv_w[0], jnp.zeros((5, n_cw), F32)], axis=0)
    c_all, cw_all = _run_rider(_allgather_rider([c.reshape(8, d // 8), cw_blk]), "gather_c_convw")
    c_all = c_all.reshape(N_DEV, 8, d // 8).reshape(N_DEV, d)
    cw_full = jnp.concatenate([cw_all[16 * k:16 * k + 8] for k in range(N_CHIP)], axis=1)
    cvec = jnp.concatenate([cw_full[0:3], conv_b, jnp.zeros((4, ff2), F32)], axis=0)
    b_shard = lax.dynamic_slice_in_dim(b_ada, k_me * n_ada, n_ada, axis=1)
    mod_k, sc_all = _mod_shard(c_all, w_ada[0], b_shard)
    (mod_g,) = _run_rider(_allgather_rider([mod_k]), "gather_mod")
    mod_all = jnp.concatenate([mod_g[16 * k:16 * k + 8] for k in range(N_CHIP)], axis=1)
    mod_me = lax.dynamic_slice_in_dim(mod_all, dev, 1, axis=0).reshape(6, d)
    vecs = jnp.concatenate([mod_me, pre_mix_g, post_mix_g, pre_ffn_g, post_ffn_g, jnp.zeros((6, d), F32)], axis=0)

    w_in_s, w_out_s, w_up_s, w_down_s = _cast_bf16([w_in[0], w_out[0], w_up[0], w_down[0]])
    w_in_b, w_out_b = _run_rider(_gather_rider([w_in_s, w_out_s], (True, False)), "gather_w_mix")

    pos = jnp.arange(HEAD)
    causal = (pos[None, :] // CHUNK) <= (pos[:, None] // CHUNK)
    wm = jnp.where(causal[None], w_spatial[0], 0.0)
    wm_b = wm.astype(BF16)
    wmt_b = jnp.swapaxes(wm, 1, 2).astype(BF16)
    bsb = jnp.broadcast_to(b_spatial[0][:, :, None], (N_HEADS, HEAD, HEAD))
    wp_b = w_pool[0].astype(BF16)
    sgu_g = jnp.concatenate([sgu_norm_g[0], jnp.zeros((4, HEAD), F32)], axis=0)
    ps = jnp.concatenate([pool_scale, jnp.zeros((7, A_WIDTH), F32)], axis=0)

    pmats = _perm_mats(FFN_TS)
    (h1, proj, cat, mixed, x2, h2p), (w_up_b, w_down_b) = _mix_fwd(
        x2d, vecs, w_in_b, w_out_b, sgu_g, wm_b, bsb, wp_b, ps, pmats, ts=512,
        rider=_gather_rider([w_up_s, w_down_s], (True, False)))
    up, yv, act, dy, dfp, acc_f = _ffn_fwd(h2p, x2, tgt, w_up_b, w_down_b, cvec, vecs, pmats, ts=FFN_TS)

    c_idx = ci.reshape(1).astype(jnp.int32)
    g_w_down, _ = _wgrad(act, dfp, d, 1024, "wgrad_down")
    (dup, dx2, dmixed, acc_c, acc_b), (land_down,) = _ffn_bwd(
        dfp, up, yv, x2, dy, mixed, w_up_b, w_down_b, cvec, vecs, pmats, ts=FFN_TS,
        rider=_sibling_rider([g_w_down], (False,)))
    (part_down,) = _sum_with_sibling([g_w_down], [land_down], (False,), c_idx, "pair_sum_down")
    g_w_up, (chips_down,) = _wgrad(h2p, dup, ff2 // 2, 2048, "wgrad_up", rider=_chips_rider([part_down]))
    (gx, dproj, acc_m, dwm, dwp, dbz, dsg, dps), (land_up,) = _mix_bwd(
        dmixed, dx2, x2d, proj, vecs, w_in_b, w_out_b, sgu_g, wm_b, wmt_b, bsb, wp_b, ps, ts=256,
        rider=_sibling_rider([g_w_up], (True,)))
    (part_up,) = _sum_with_sibling([g_w_up], [land_up], (True,), c_idx, "pair_sum_up")
    (g_w_out, g_w_in), (chips_up,) = _wgrad_pair(cat, dmixed, h1, dproj, 1024, "wgrad_mix",
                                                 rider=_chips_rider([part_up]))
    land_mix = _run_rider(_sibling_rider([g_w_in, g_w_out], (True, False)), "reduce_to_sibling")
    parts_mix = _sum_with_sibling([g_w_in, g_w_out], land_mix, (True, False), c_idx, "pair_sum_mix")
    chips_in, chips_out = _run_rider(_chips_rider(parts_mix), "reduce_over_chips")
    reduced = _sum_chips_and_share([chips_in, chips_out, chips_up, chips_down])
    big = {}
    for nm, red, (w, m, v) in zip(("w_in", "w_out", "w_up", "w_down"), reduced,
                                  ((w_in, m_w_in, v_w_in), (w_out, m_w_out, v_w_out),
                                   (w_up, m_w_up, v_w_up), (w_down, m_w_down, v_w_down))):
        g = red.reshape(w.shape[1], w.shape[2])
        big[nm] = (g.reshape(w.shape),) + tuple(_adamw_big(g, w, m, v, "adamw_" + nm))

    g1 = jnp.concatenate([acc_f, acc_b, acc_m], axis=0)
    hflat = (N_HEADS * HEAD, HEAD)
    gathered = _run_rider(_allgather_rider(
        [g1, acc_c, dwm.reshape(hflat), dwp.reshape(hflat), dbz.reshape(hflat), dsg, dps]), "gather_small_grads")
    g1a, g2a, gwm, gwp, gbz, gsg, gps = [t.reshape((N_DEV, t.shape[0] // N_DEV, t.shape[1])) for t in gathered]
    g2s = lax.dynamic_slice_in_dim(g2a, k_me * n_cw, n_cw, axis=2)
    params = {
        "b_ada": (b_ada, m_b_ada, v_b_ada), "pre_mix_g": (pre_mix_g, m_pre_mix_g, v_pre_mix_g),
        "post_mix_g": (post_mix_g, m_post_mix_g, v_post_mix_g),
        "sgu_norm_g": (sgu_norm_g, m_sgu_norm_g, v_sgu_norm_g), "w_spatial": (w_spatial, m_w_spatial, v_w_spatial),
        "b_spatial": (b_spatial, m_b_spatial, v_b_spatial), "w_pool": (w_pool, m_w_pool, v_w_pool),
        "pool_scale": (pool_scale, m_pool_scale, v_pool_scale), "pre_ffn_g": (pre_ffn_g, m_pre_ffn_g, v_pre_ffn_g),
        "post_ffn_g": (post_ffn_g, m_post_ffn_g, v_post_ffn_g), "conv_w": (conv_w, m_conv_w, v_conv_w),
        "conv_b": (conv_b, m_conv_b, v_conv_b),
    }
    loss_slab, small = _small_update(g1a, g2a, g2s, gwm, gwp, gbz, gsg, gps, params)

    gmod_all = jnp.concatenate(
        [g1a[:, 16 + M_DSHM], g1a[:, 16 + M_DSCM], g1a[:, 8 + B_DGM], g1a[:, 8 + B_DSHF], g1a[:, 8 + B_DSCF],
         g1a[:, F_DGF]], axis=1)
    gm = lax.dynamic_slice_in_dim(gmod_all, k_me * n_ada, n_ada, axis=1)
    gm = jnp.concatenate([gm, jnp.zeros((HEAD - N_DEV, n_ada), F32)], axis=0)
    sct = jnp.concatenate([sc_all.T, jnp.zeros((d, HEAD - N_DEV), F32)], axis=1)
    ada = tuple(_wada_update(sct, gm, w_ada, m_w_ada, v_w_ada))

    everything = dict(small)
    everything.update(big)
    everything["w_ada"] = ada
    order = ["w_ada", "b_ada", "pre_mix_g", "post_mix_g", "w_in", "sgu_norm_g", "w_spatial", "b_spatial", "w_pool",
             "pool_scale", "w_out", "pre_ffn_g", "post_ffn_g", "w_up", "conv_w", "conv_b", "w_down"]
    outs = [loss_slab[0, 0], gx.reshape(x.shape)]
    for j in range(4):
        outs += [everything[nm][j] for nm in order]
    return tuple(outs)
```

```python
import functools

import jax
import jax.numpy as jnp
from jax import lax
from jax.experimental import pallas as pl
from jax.experimental.pallas import tpu as pltpu

F32 = jnp.float32
BF16 = jnp.bfloat16
MESH = pl.DeviceIdType.MESH

EPS = 1e-6
HEAD = 128
N_HEADS = 4
A_WIDTH = N_HEADS * HEAD
CHUNK = 64
POOL_WINDOWS = (2, 4, 8, 16)
POOL_HALO = 16
FFN_TS = 256

ADAM_LR = 0.001
ADAM_B1 = 0.9
ADAM_B2 = 0.999
ADAM_EPS = 1e-08
ADAM_WD = 0.01
ADAM_STEP = 10

VMEM_LIMIT_BYTES = 58 * 1024 * 1024
N_DEV = 8
N_CHIP = 4


def _dot(a, b):
    return jnp.dot(a, b, preferred_element_type=F32)


def _dot_nt(a, b):
    return lax.dot_general(a, b, (((1,), (1,)), ((), ())), preferred_element_type=F32)


def _dot_tn(a, b):
    return lax.dot_general(a, b, (((0,), (0,)), ((), ())), preferred_element_type=F32)


def _gelu(x):
    return x * (0.5 * (1.0 + jnp.tanh(0.7978845608028654 * (x + 0.044715 * (x * x * x)))))


def _gelu_grad(x):
    t = jnp.tanh(0.7978845608028654 * (x + 0.044715 * (x * x * x)))
    return 0.5 * (1.0 + t) + (0.5 * x) * (1.0 - t * t) * (0.7978845608028654 * (1.0 + 0.134145 * (x * x)))


def _sigmoid(x):
    return 1.0 / (1.0 + jnp.exp(-x))


def _rms(x):
    return lax.rsqrt(jnp.mean(x * x, axis=-1, keepdims=True) + EPS)


def _colsum(x):
    return jnp.sum(x, axis=0, keepdims=True)


def _rowmean(x):
    return jnp.mean(x, axis=-1, keepdims=True)


def _tiled(shape, index_map):
    return pl.BlockSpec(shape, index_map)


def _resident(shape):
    nd = len(shape)
    return pl.BlockSpec(shape, lambda *_: (0,) * nd, pipeline_mode=pl.Buffered(1))


def _whole(shape):
    nd = len(shape)
    return pl.BlockSpec(shape, lambda *_: (0,) * nd)


def _seq_params():
    return pltpu.CompilerParams(dimension_semantics=("arbitrary",), vmem_limit_bytes=VMEM_LIMIT_BYTES)


def _ff_chunks(f, width=768):
    out, o = [], 0
    while o < f:
        w = min(width, f - o)
        out.append((o, w))
        o += w
    return out


def _pool_fwd(p, halo, g, t_glob):
    ext = jnp.concatenate([halo, p], axis=0)
    s = ext
    for step in range(g + 1):
        s = s + pltpu.roll(s, 1 << step, 0)
    cnt = jnp.minimum(t_glob + 1, POOL_WINDOWS[g]).astype(F32)
    return s[POOL_HALO:] / cnt - p


def _adamw_math(w, g, m, v):
    m = ADAM_B1 * m + (1.0 - ADAM_B1) * g
    v = ADAM_B2 * v + (1.0 - ADAM_B2) * (g * g)
    m_hat = m / (1.0 - ADAM_B1 ** ADAM_STEP)
    v_hat = v / (1.0 - ADAM_B2 ** ADAM_STEP)
    delta = -ADAM_LR * (m_hat / (jnp.sqrt(v_hat) + ADAM_EPS) + ADAM_WD * w)
    return delta, m, v


def _mesh_pos():
    return lax.axis_index("x"), lax.axis_index("y"), lax.axis_index("c")


class _Rider:
    def __init__(self, inputs, out_shape, sems, start, finish):
        self.inputs, self.out_shape, self.sems = list(inputs), list(out_shape), list(sems)
        self.start, self.finish = start, finish


def _call(body, *, name, grid, in_specs, out_specs, out_shape, scratch_shapes, args, rider=None):
    params = pltpu.CompilerParams(dimension_semantics=("arbitrary",) * len(grid), vmem_limit_bytes=VMEM_LIMIT_BYTES)
    if rider is None:
        res = pl.pallas_call(body, name=name, grid=grid, in_specs=in_specs, out_specs=out_specs, out_shape=out_shape,
                             scratch_shapes=scratch_shapes, compiler_params=params)(*args)
        return tuple(res), ()
    cuts = [len(in_specs), len(rider.inputs), len(out_specs), len(rider.out_shape), len(scratch_shapes),
            len(rider.sems)]

    def hosted(*refs):
        groups, a = [], 0
        for cnt in cuts:
            groups.append(refs[a:a + cnt])
            a += cnt
        ins, r_in, outs, r_out, scr, r_sem = groups
        first = functools.reduce(jnp.logical_and, [pl.program_id(k) == 0 for k in range(len(grid))])
        last = functools.reduce(jnp.logical_and, [pl.program_id(k) == grid[k] - 1 for k in range(len(grid))])

        @pl.when(first)
        def _():
            rider.start(r_in, r_out, r_sem)

        body(*ins, *outs, *scr)

        @pl.when(last)
        def _():
            rider.finish(r_in, r_out, r_sem)

    anyspec = pl.BlockSpec(memory_space=pl.ANY)
    res = pl.pallas_call(
        hosted, name=name, grid=grid,
        in_specs=list(in_specs) + [anyspec] * cuts[1], out_specs=list(out_specs) + [anyspec] * cuts[3],
        out_shape=list(out_shape) + rider.out_shape, scratch_shapes=list(scratch_shapes) + rider.sems,
        compiler_params=params)(*args, *rider.inputs)
    return tuple(res[:cuts[2]]), tuple(res[cuts[2]:])


def _run_rider(rider, name):
    n_in, n_out = len(rider.inputs), len(rider.out_shape)

    def body(*refs):
        r_in, r_out, r_sem = refs[:n_in], refs[n_in:n_in + n_out], refs[n_in + n_out:]
        rider.start(r_in, r_out, r_sem)
        rider.finish(r_in, r_out, r_sem)

    anyspec = pl.BlockSpec(memory_space=pl.ANY)
    return pl.pallas_call(body, name=name, out_shape=rider.out_shape, in_specs=[anyspec] * n_in,
                          out_specs=[anyspec] * n_out, scratch_shapes=rider.sems)(*rider.inputs)


def _allgather_rider(arrs):
    n = len(arrs)

    def plan(ins, outs, sems):
        send_sems, recv_sems, local_sems = sems
        x, y, c = _mesh_pos()
        me, sibling = (x, y, c), (x, y, 1 - c)
        chips = [(1 - x, y), (x, 1 - y), (1 - x, 1 - y)]

        def rows(a, px, py, pc):
            r = ins[a].shape[0]
            return outs[a].at[pl.ds(pl.multiple_of((4 * px + 2 * py + pc) * r, 8), r), :]

        def copy(a, k, block, to, src=None):
            return pltpu.make_async_remote_copy(
                src_ref=rows(a, *block) if src is None else src, dst_ref=rows(a, *block),
                send_sem=send_sems.at[a * 7 + k], recv_sem=recv_sems.at[a * 7 + k],
                device_id=to, device_id_type=MESH)

        local = [pltpu.make_async_copy(ins[a], rows(a, *me), local_sems.at[a]) for a in range(n)]
        first = []
        for a in range(n):
            first.append(copy(a, 0, me, sibling, src=ins[a]))
            first += [copy(a, 1 + j, me, (*chip, c), src=ins[a]) for j, chip in enumerate(chips)]
        return c, me, sibling, chips, copy, local, first

    def start(ins, outs, sems):
        *_, local, first = plan(ins, outs, sems)
        for cp in local + first:
            cp.start()

    def finish(ins, outs, sems):
        c, me, sibling, chips, copy, local, first = plan(ins, outs, sems)
        passed = []
        for a in range(n):
            for j, chip in enumerate(chips):
                copy(a, 1 + j, (*chip, c), me).wait_recv()
                fwd = copy(a, 4 + j, (*chip, c), sibling)
                fwd.start()
                passed.append(fwd)
        for a in range(n):
            copy(a, 0, sibling, me).wait_recv()
            for j, chip in enumerate(chips):
                copy(a, 4 + j, (*chip, 1 - c), me).wait_recv()
        for cp in first + passed:
            cp.wait_send()
        for mine in local:
            mine.wait()

    return _Rider(arrs, [jax.ShapeDtypeStruct((N_DEV * a.shape[0], a.shape[1]), a.dtype) for a in arrs],
                  [pltpu.SemaphoreType.DMA((7 * n,)), pltpu.SemaphoreType.DMA((7 * n,)),
                   pltpu.SemaphoreType.DMA((n,))], start, finish)


def _piece(ref, col_sharded, k, h):
    m, n = ref.shape
    if col_sharded:
        mh, nc = m // 2, n // N_CHIP
        return ref.at[pl.ds(pl.multiple_of(h * mh, 16), mh), pl.ds(pl.multiple_of(k * nc, 128), nc)]
    rp = m // (2 * N_CHIP)
    return ref.at[pl.ds(pl.multiple_of((2 * k + h) * rp, 16), rp), :]


def _piece_shape(shape, col_sharded):
    m, n = shape
    return (m // 2, n // N_CHIP) if col_sharded else (m // (2 * N_CHIP), n)


def _cast_bf16(arrs):
    def body(*refs):
        for i_ref, o_ref in zip(refs[:len(arrs)], refs[len(arrs):]):
            o_ref[...] = i_ref[...].astype(BF16)

    return pl.pallas_call(body, name="cast_weights", out_shape=[jax.ShapeDtypeStruct(a.shape, BF16) for a in arrs],
                          compiler_params=pltpu.CompilerParams(vmem_limit_bytes=VMEM_LIMIT_BYTES))(*arrs)


def _gather_rider(shards, col_flags):
    n = len(shards)
    full_shapes = []
    for s, col in zip(shards, col_flags):
        full_shapes.append((s.shape[0], s.shape[1] * N_CHIP) if col else (s.shape[0] * N_CHIP, s.shape[1]))

    def plan(ins, outs, sems):
        send_sems, recv_sems, local_sems = sems
        x, y, c = _mesh_pos()
        sibling = (x, y, 1 - c)
        chips = [(1 - x, y), (x, 1 - y), (1 - x, 1 - y)]
        k_me = 2 * x + y

        def my_half(a):
            rows = ins[a].shape[0] // 2
            return ins[a].at[pl.ds(pl.multiple_of(c * rows, 16), rows), :]

        def own_window(a):
            m, nn = ins[a].shape
            if col_flags[a]:
                return outs[a].at[:, pl.ds(pl.multiple_of(k_me * nn, 128), nn)]
            return outs[a].at[pl.ds(pl.multiple_of(k_me * m, 16), m), :]

        def copy(a, s, src, k, h, to):
            return pltpu.make_async_remote_copy(
                src_ref=src, dst_ref=_piece(outs[a], col_flags[a], k, h),
                send_sem=send_sems.at[a * 6 + s], recv_sem=recv_sems.at[a * 6 + s],
                device_id=to, device_id_type=MESH)

        local = [pltpu.make_async_copy(ins[a], own_window(a), local_sems.at[a]) for a in range(n)]
        first = [copy(a, j, my_half(a), k_me, c, (*chip, c)) for a in range(n) for j, chip in enumerate(chips)]
        return c, sibling, chips, copy, local, first

    def start(ins, outs, sems):
        *_, local, first = plan(ins, outs, sems)
        for cp in local + first:
            cp.start()

    def finish(ins, outs, sems):
        c, sibling, chips, copy, local, first = plan(ins, outs, sems)
        passed = []
        for a in range(n):
            for j, chip in enumerate(chips):
                k_from = 2 * chip[0] + chip[1]
                win = _piece(outs[a], col_flags[a], k_from, c)
                copy(a, j, win, k_from, c, (*chip, c)).wait_recv()
                fwd = copy(a, 3 + j, win, k_from, c, sibling)
                fwd.start()
                passed.append(fwd)
        for a in range(n):
            for j, chip in enumerate(chips):
                k_from = 2 * chip[0] + chip[1]
                win = _piece(outs[a], col_flags[a], k_from, 1 - c)
                copy(a, 3 + j, win, k_from, 1 - c, sibling).wait_recv()
        for cp in first + passed:
            cp.wait_send()
        for mine in local:
            mine.wait()

    return _Rider(shards, [jax.ShapeDtypeStruct(fs, BF16) for fs in full_shapes],
                  [pltpu.SemaphoreType.DMA((6 * n,)), pltpu.SemaphoreType.DMA((6 * n,)),
                   pltpu.SemaphoreType.DMA((n,))], start, finish)


def _sibling_rider(grads, col_flags):
    n = len(grads)
    pshapes = [_piece_shape(g.shape, col) for g, col in zip(grads, col_flags)]

    def copies(ins, outs, sems):
        send_sems, recv_sems = sems
        x, y, c = _mesh_pos()
        return [pltpu.make_async_remote_copy(
            src_ref=_piece(ins[a], col_flags[a], k, 1 - c), dst_ref=outs[a].at[k],
            send_sem=send_sems.at[a * N_CHIP + k], recv_sem=recv_sems.at[a * N_CHIP + k],
            device_id=(x, y, 1 - c), device_id_type=MESH) for a in range(n) for k in range(N_CHIP)]

    def start(ins, outs, sems):
        for cp in copies(ins, outs, sems):
            cp.start()

    def finish(ins, outs, sems):
        cps = copies(ins, outs, sems)
        for cp in cps:
            cp.wait_recv()
        for cp in cps:
            cp.wait_send()

    return _Rider(grads, [jax.ShapeDtypeStruct((N_CHIP,) + ps, g.dtype) for ps, g in zip(pshapes, grads)],
                  [pltpu.SemaphoreType.DMA((N_CHIP * n,)), pltpu.SemaphoreType.DMA((N_CHIP * n,))], start, finish)


def _sum_with_sibling(grads, landed, col_flags, c_idx, name):
    n = len(grads)
    pshapes = [_piece_shape(g.shape, col) for g, col in zip(grads, col_flags)]

    def body(c_ref, *refs):
        ins, lands, outs = refs[:n], refs[n:2 * n], refs[2 * n:]
        for a in range(n):
            outs[a][0] = (ins[a][...] + lands[a][0]).astype(BF16)

    in_specs = []
    for ps, col in zip(pshapes, col_flags):
        if col:
            in_specs.append(pl.BlockSpec(ps, lambda k, c_ref: (c_ref[0], k)))
        else:
            in_specs.append(pl.BlockSpec(ps, lambda k, c_ref: (2 * k + c_ref[0], 0)))
    land_specs = [pl.BlockSpec((1,) + ps, lambda k, c_ref: (k, 0, 0)) for ps in pshapes]
    return pl.pallas_call(
        body, name=name,
        grid_spec=pltpu.PrefetchScalarGridSpec(
            num_scalar_prefetch=1, grid=(N_CHIP,),
            in_specs=in_specs + land_specs, out_specs=land_specs),
        out_shape=[jax.ShapeDtypeStruct((N_CHIP,) + ps, BF16) for ps in pshapes],
        compiler_params=pltpu.CompilerParams(dimension_semantics=("arbitrary",), vmem_limit_bytes=VMEM_LIMIT_BYTES),
    )(c_idx, *grads, *landed)


def _chips_rider(parts):
    n = len(parts)

    def plan(ins, outs, sems, arriving):
        send_sems, recv_sems, local_sems = sems
        x, y, c = _mesh_pos()
        chips = [(1 - x, y), (x, 1 - y), (1 - x, 1 - y)]
        k_me = 2 * x + y
        local = [pltpu.make_async_copy(ins[a].at[k_me], outs[a].at[k_me], local_sems.at[a]) for a in range(n)]
        copies = []
        for a in range(n):
            for j, chip in enumerate(chips):
                k_peer = 2 * chip[0] + chip[1]
                copies.append(pltpu.make_async_remote_copy(
                    src_ref=ins[a].at[k_peer], dst_ref=outs[a].at[k_peer if arriving else k_me],
                    send_sem=send_sems.at[a * 3 + j], recv_sem=recv_sems.at[a * 3 + j],
                    device_id=(*chip, c), device_id_type=MESH))
        return local, copies

    def start(ins, outs, sems):
        local, sends = plan(ins, outs, sems, False)
        for cp in local + sends:
            cp.start()

    def finish(ins, outs, sems):
        local, arrivals = plan(ins, outs, sems, True)
        for cp in arrivals:
            cp.wait_recv()
        for cp in arrivals:
            cp.wait_send()
        for mine in local:
            mine.wait()

    return _Rider(parts, [jax.ShapeDtypeStruct(p.shape, p.dtype) for p in parts],
                  [pltpu.SemaphoreType.DMA((3 * n,)), pltpu.SemaphoreType.DMA((3 * n,)),
                   pltpu.SemaphoreType.DMA((n,))], start, finish)


def _sum_chips_and_share(landed):
    n = len(landed)

    def body(*refs):
        ins, outs, red = refs[:n], refs[n:2 * n], refs[2 * n:3 * n]
        send_sems, recv_sems, local_sems = refs[3 * n:]
        x, y, c = _mesh_pos()
        sibling = (x, y, 1 - c)
        copies, local = [], []
        for a in range(n):
            red[a][...] = ((ins[a][0].astype(F32) + ins[a][1].astype(F32)) + ins[a][2].astype(F32)
                           + ins[a][3].astype(F32))
            mine = pltpu.make_async_copy(red[a], outs[a].at[c], local_sems.at[a])
            mine.start()
            local.append(mine)
            cp = pltpu.make_async_remote_copy(
                src_ref=red[a], dst_ref=outs[a].at[c],
                send_sem=send_sems.at[a], recv_sem=recv_sems.at[a],
                device_id=sibling, device_id_type=MESH)
            cp.start()
            copies.append(cp)
        for a in range(n):
            pltpu.make_async_remote_copy(
                src_ref=red[a], dst_ref=outs[a].at[1 - c],
                send_sem=send_sems.at[a], recv_sem=recv_sems.at[a],
                device_id=sibling, device_id_type=MESH).wait_recv()
        for cp in copies:
            cp.wait_send()
        for mine in local:
            mine.wait()

    return pl.pallas_call(
        body, name="sum_chips_and_share",
        out_shape=[jax.ShapeDtypeStruct((2,) + l.shape[1:], F32) for l in landed],
        in_specs=[pl.BlockSpec(memory_space=pltpu.VMEM)] * n,
        out_specs=[pl.BlockSpec(memory_space=pl.ANY)] * n,
        scratch_shapes=[pltpu.VMEM(l.shape[1:], F32) for l in landed]
        + [pltpu.SemaphoreType.DMA((n,)), pltpu.SemaphoreType.DMA((n,)), pltpu.SemaphoreType.DMA((n,))],
        compiler_params=pltpu.CompilerParams(vmem_limit_bytes=VMEM_LIMIT_BYTES),
    )(*landed)


def _mod_shard(c_all, w_ada, b_shard):
    def body(c_ref, w_ref, b_ref, o_ref, sc_ref):
        cc = c_ref[...]
        sc = cc * _sigmoid(cc)
        sc_ref[...] = sc
        o_ref[...] = _dot(sc, w_ref[...]) + b_ref[...]

    nb, d = c_all.shape
    nn = w_ada.shape[1]
    return pl.pallas_call(
        body, name="mod_shard",
        out_shape=[jax.ShapeDtypeStruct((nb, nn), F32), jax.ShapeDtypeStruct((nb, d), F32)],
        compiler_params=pltpu.CompilerParams(vmem_limit_bytes=VMEM_LIMIT_BYTES),
    )(c_all, w_ada, b_shard)


V_SH_M, V_SC_M, V_G_M, V_SH_F, V_SC_F, V_G_F, V_PRE_MIX, V_POST_MIX, V_PRE_FFN, V_POST_FFN = range(10)


def _vrow(vec_ref, r):
    return vec_ref[r:r + 1, :]


def _mix_fwd(x, vecs, w_in_b, w_out_b, sgu_g, wm_b, bsb, wp_b, ps, pmats, ts, rider=None):
    s_len, d = x.shape
    nt = s_len // ts
    nblk = ts // HEAD
    n_proj = w_in_b.shape[1]

    def body(x_ref, vec_ref, win_ref, wout_ref, sg_ref, wm_ref, bs_ref, wp_ref, ps_ref, pm_ref,
             h1_ref, proj_ref, cat_ref, mixed_ref, x2_ref, h2_ref, carry_ref):
        i = pl.program_id(0)

        @pl.when(i == 0)
        def _():
            carry_ref[...] = jnp.zeros_like(carry_ref)

        x = x_ref[...]
        h1 = (((x * _rms(x)) * _vrow(vec_ref, V_PRE_MIX)) * (1.0 + _vrow(vec_ref, V_SC_M))
              + _vrow(vec_ref, V_SH_M)).astype(BF16)
        h1_ref[...] = h1
        proj = _dot(h1, win_ref[...])
        proj_ref[...] = proj.astype(BF16)
        t_glob = lax.broadcasted_iota(jnp.int32, (ts, HEAD), 0) + i * ts
        for h in range(N_HEADS):
            u = _gelu(proj[:, h * HEAD:(h + 1) * HEAD])
            v = _gelu(proj[:, A_WIDTH + h * HEAD:A_WIDTH + (h + 1) * HEAD])
            vn = ((v * _rms(v)) * sg_ref[h:h + 1, :]).astype(BF16)
            for b in range(nblk):
                rs = slice(b * HEAD, (b + 1) * HEAD)
                z = _dot(wm_ref[h], vn[rs]) + bs_ref[h]
                cat_ref[rs, h * HEAD:(h + 1) * HEAD] = (u[rs] * z).astype(BF16)
        for g in range(len(POOL_WINDOWS)):
            gs = slice(g * HEAD, (g + 1) * HEAD)
            p = proj[:, 2 * A_WIDTH + g * HEAD:2 * A_WIDTH + (g + 1) * HEAD]
            pooled = _pool_fwd(p, carry_ref[:, gs], g, t_glob)
            yb = _dot(pooled.astype(BF16), wp_ref[g]) * ps_ref[0:1, gs]
            cat_ref[:, A_WIDTH + g * HEAD:A_WIDTH + (g + 1) * HEAD] = yb.astype(BF16)
        carry_ref[...] = proj[ts - POOL_HALO:ts, 2 * A_WIDTH:]
        mixed = _dot(cat_ref[...], wout_ref[...])
        mixed_ref[...] = mixed
        x2 = x + _vrow(vec_ref, V_G_M) * ((mixed * _rms(mixed)) * _vrow(vec_ref, V_POST_MIX))
        x2_ref[...] = x2
        h2 = (((x2 * _rms(x2)) * _vrow(vec_ref, V_PRE_FFN)) * (1.0 + _vrow(vec_ref, V_SC_F))
              + _vrow(vec_ref, V_SH_F)).astype(BF16)
        for b in range(ts // FFN_TS):
            rs = slice(b * FFN_TS, (b + 1) * FFN_TS)
            h2_ref[rs, :] = _permute_bf16(pm_ref[0], h2[rs])

    row = lambda i: (i, 0)
    return _call(
        body, name="mix_fwd", grid=(nt,), rider=rider,
        in_specs=[_tiled((ts, d), row), _whole(vecs.shape), _resident(w_in_b.shape), _resident(w_out_b.shape),
                  _whole(sgu_g.shape), _whole(wm_b.shape), _whole(bsb.shape), _whole(wp_b.shape), _whole(ps.shape),
                  _whole(pmats.shape)],
        out_specs=[_tiled((ts, d), row), _tiled((ts, n_proj), row), _tiled((ts, d), row),
                   _tiled((ts, d), row), _tiled((ts, d), row), _tiled((ts, d), row)],
        out_shape=[jax.ShapeDtypeStruct((s_len, d), BF16), jax.ShapeDtypeStruct((s_len, n_proj), BF16),
                   jax.ShapeDtypeStruct((s_len, d), BF16), jax.ShapeDtypeStruct((s_len, d), F32),
                   jax.ShapeDtypeStruct((s_len, d), F32), jax.ShapeDtypeStruct((s_len, d), BF16)],
        scratch_shapes=[pltpu.VMEM((POOL_HALO, A_WIDTH), F32)],
        args=(x, vecs, w_in_b, w_out_b, sgu_g, wm_b, bsb, wp_b, ps, pmats))


def _perm_mats(ts):
    p = jnp.arange(ts)
    pm = (((p % 8) * (ts // 8) + p // 8)[:, None] == p[None, :]).astype(BF16)
    return jnp.stack([pm, pm.T])


def _permute_bf16(pm, xb):
    return _dot(pm, xb).astype(BF16)


def _permute_f32(pm, x):
    hi = x.astype(BF16)
    lo = (x - hi.astype(F32)).astype(BF16)
    return _dot(pm, hi) + _dot(pm, lo)


def _conv_out(u, um2, um1, cv_ref, cols):
    return (cv_ref[3:4, cols] + um2 * cv_ref[0:1, cols] + um1 * cv_ref[1:2, cols] + u * cv_ref[2:3, cols])


F_LOSS, F_DGF, F_DPOSTFFN = 0, 1, 2
B_DSHF, B_DSCF, B_DPREFFN, B_DGM, B_DPOSTMIX = 0, 1, 2, 3, 4
M_DSHM, M_DSCM, M_DPREMIX = 0, 1, 2
C_DCB, C_DCW = 0, 1


def _ffn_fwd(h2p, x2, tgt, w_up_b, w_down_b, cvec, vecs, pmats, ts):
    s_len, d = x2.shape
    ff2 = w_up_b.shape[1]
    ff = ff2 // 2
    nt = s_len // ts
    chunks = _ff_chunks(ff)

    def body(h2_ref, x2_ref, t_ref, wu_ref, wd_ref, cv_ref, vec_ref, pm_ref,
             up_ref, y_ref, act_ref, dy_ref, df_ref, acc_ref, carry_ref):
        @pl.when(pl.program_id(0) == 0)
        def _():
            carry_ref[...] = jnp.zeros_like(carry_ref)
            acc_ref[...] = jnp.zeros_like(acc_ref)

        h2v = h2_ref[...]

        def up_dots(o, w):
            return [_dot(h2v, wu_ref[:, base + o:base + o + w]) for base in (0, ff)]

        f = None
        pending = None
        nxt = up_dots(*chunks[0])
        for ci, (o, w) in enumerate(chunks):
            us = nxt
            if ci + 1 < len(chunks):
                nxt = up_dots(*chunks[ci + 1])
            if pending is not None:
                part = _dot(pending[0], wd_ref[pending[1]:pending[1] + pending[2], :])
                f = part if f is None else f + part
            sub0 = lax.broadcasted_iota(jnp.int32, (8, w), 0) == 0
            ys = []
            for base, u in zip((0, ff), us):
                cols = slice(base + o, base + o + w)
                up_ref[:, cols] = u.astype(BF16)
                b1 = jnp.where(sub0, pltpu.roll(carry_ref[8:16, cols], 1, 0), pltpu.roll(u[ts - 8:ts], 1, 0))
                b2 = jnp.where(sub0, pltpu.roll(carry_ref[0:8, cols], 1, 0), pltpu.roll(u[ts - 16:ts - 8], 1, 0))
                um1 = jnp.concatenate([b1, u[:ts - 8]], axis=0)
                um2 = jnp.concatenate([b2, b1, u[:ts - 16]], axis=0)
                ys.append(_conv_out(u, um2, um1, cv_ref, cols))
                carry_ref[:, cols] = u[ts - 16:ts]
            gate, val = ys
            sg = _sigmoid(gate)
            gs = gate * sg
            act = (gs * val).astype(BF16)
            act_ref[:, o:o + w] = act
            y_ref[:, o:o + w] = (val * (sg + gs * (1.0 - sg))).astype(BF16)
            y_ref[:, ff + o:ff + o + w] = gs.astype(BF16)
            pending = (act, o, w)
        f = f + _dot(pending[0], wd_ref[pending[1]:pending[1] + pending[2], :])
        f = _permute_f32(pm_ref[1], f)
        r3 = _rms(f)
        fhat = f * r3
        post = _vrow(vec_ref, V_POST_FFN)
        g_f = _vrow(vec_ref, V_G_F)
        fn = fhat * post
        e = (x2_ref[...] + g_f * fn) - t_ref[...]
        dy = e * (1.0 / d)
        dy_ref[...] = dy
        dfn = dy * g_f
        acc_ref[F_LOSS:F_LOSS + 1, :] += _colsum(e * e)
        acc_ref[F_DGF:F_DGF + 1, :] += _colsum(dy * fn)
        acc_ref[F_DPOSTFFN:F_DPOSTFFN + 1, :] += _colsum(dfn * fhat)
        dfhat = dfn * post
        df = (r3 * (dfhat - fhat * _rowmean(dfhat * fhat))).astype(BF16)
        df_ref[...] = _permute_bf16(pm_ref[0], df)

    row = lambda i: (i, 0)
    return pl.pallas_call(
        body, name="ffn_fwd", grid=(nt,),
        in_specs=[_tiled((ts, d), row), _tiled((ts, d), row), _tiled((ts, d), row), _resident(w_up_b.shape),
                  _resident(w_down_b.shape), _whole(cvec.shape), _whole(vecs.shape), _whole(pmats.shape)],
        out_specs=[_tiled((ts, ff2), row), _tiled((ts, ff2), row), _tiled((ts, ff), row), _tiled((ts, d), row),
                   _tiled((ts, d), row), _whole((8, d))],
        out_shape=[jax.ShapeDtypeStruct((s_len, ff2), BF16), jax.ShapeDtypeStruct((s_len, ff2), BF16),
                   jax.ShapeDtypeStruct((s_len, ff), BF16), jax.ShapeDtypeStruct((s_len, d), F32),
                   jax.ShapeDtypeStruct((s_len, d), BF16), jax.ShapeDtypeStruct((8, d), F32)],
        scratch_shapes=[pltpu.VMEM((16, ff2), F32)],
        compiler_params=_seq_params(),
    )(h2p, x2, tgt, w_up_b, w_down_b, cvec, vecs, pmats)


def _ffn_bwd(dfp, upp, yp, x2, dy, mixed, w_up_b, w_down_b, cvec, vecs, pmats, ts, rider=None):
    s_len, d = x2.shape
    ff2 = w_up_b.shape[1]
    ff = ff2 // 2
    nt = s_len // ts
    chunks = _ff_chunks(ff, 512)

    def body(df_ref, up_ref, y_ref, x2_ref, dy_ref, mx_ref, wu_ref, wd_ref, cv_ref, vec_ref, pm_ref,
             dup_ref, dx2_ref, dmx_ref, accc_ref, acc_ref, carry_ref):
        @pl.when(pl.program_id(0) == 0)
        def _():
            carry_ref[...] = jnp.zeros_like(carry_ref)
            accc_ref[...] = jnp.zeros_like(accc_ref)
            acc_ref[...] = jnp.zeros_like(acc_ref)

        dfv = df_ref[...]

        def dh2_add(acc, dups, o, w):
            for base, dup in zip((0, ff), dups):
                part = _dot_nt(dup, wu_ref[:, base + o:base + o + w])
                acc = part if acc is None else acc + part
            return acc

        dh2 = None
        pending = None
        nxt = _dot_nt(dfv, wd_ref[chunks[0][0]:chunks[0][0] + chunks[0][1], :])
        for ci, (o, w) in enumerate(chunks):
            dact = nxt
            if ci + 1 < len(chunks):
                o2, w2 = chunks[ci + 1]
                nxt = _dot_nt(dfv, wd_ref[o2:o2 + w2, :])
            if pending is not None:
                dh2 = dh2_add(dh2, *pending)
            sub7 = lax.broadcasted_iota(jnp.int32, (8, w), 0) == 7
            dups = []
            dys = (dact * y_ref[:, o:o + w].astype(F32), dact * y_ref[:, ff + o:ff + o + w].astype(F32))
            for base, dyv in zip((0, ff), dys):
                cols = slice(base + o, base + o + w)
                u = up_ref[:, cols].astype(F32)
                e0 = jnp.where(sub7, pltpu.roll(carry_ref[0:8, cols], 7, 0), pltpu.roll(dyv[0:8], 7, 0))
                e1 = jnp.where(sub7, pltpu.roll(carry_ref[8:16, cols], 7, 0), pltpu.roll(dyv[8:16], 7, 0))
                dyp1 = jnp.concatenate([dyv[8:], e0], axis=0)
                dyp2 = jnp.concatenate([dyv[16:], e0, e1], axis=0)
                accc_ref[C_DCB:C_DCB + 1, cols] += _colsum(dyv)
                accc_ref[C_DCW + 0:C_DCW + 1, cols] += _colsum(dyp2 * u)
                accc_ref[C_DCW + 1:C_DCW + 2, cols] += _colsum(dyp1 * u)
                accc_ref[C_DCW + 2:C_DCW + 3, cols] += _colsum(dyv * u)
                dup = (dyv * cv_ref[2:3, cols] + dyp1 * cv_ref[1:2, cols] + dyp2 * cv_ref[0:1, cols]).astype(BF16)
                dup_ref[:, cols] = dup
                dups.append(dup)
                carry_ref[:, cols] = dyv[0:16]
            pending = (dups, o, w)
        dh2 = dh2_add(dh2, *pending)
        dh2 = _permute_f32(pm_ref[1], dh2)
        x2 = x2_ref[...]
        r2 = _rms(x2)
        xn = x2 * r2
        pre = _vrow(vec_ref, V_PRE_FFN)
        one_sc = 1.0 + _vrow(vec_ref, V_SC_F)
        acc_ref[B_DSHF:B_DSHF + 1, :] += _colsum(dh2)
        acc_ref[B_DSCF:B_DSCF + 1, :] += _colsum(dh2 * (xn * pre))
        acc_ref[B_DPREFFN:B_DPREFFN + 1, :] += _colsum(dh2 * xn * one_sc)
        dxn = dh2 * pre * one_sc
        dx2 = dy_ref[...] + r2 * (dxn - xn * _rowmean(dxn * xn))
        dx2_ref[...] = dx2
        mixed = mx_ref[...]
        rm = _rms(mixed)
        mhat = mixed * rm
        post = _vrow(vec_ref, V_POST_MIX)
        acc_ref[B_DGM:B_DGM + 1, :] += _colsum(dx2 * (mhat * post))
        dmn = dx2 * _vrow(vec_ref, V_G_M)
        acc_ref[B_DPOSTMIX:B_DPOSTMIX + 1, :] += _colsum(dmn * mhat)
        dmhat = dmn * post
        dmx_ref[...] = (rm * (dmhat - mhat * _rowmean(dmhat * mhat))).astype(BF16)

    rev = lambda i: (nt - 1 - i, 0)
    return _call(
        body, name="ffn_bwd", grid=(nt,), rider=rider,
        in_specs=[_tiled((ts, d), rev), _tiled((ts, ff2), rev), _tiled((ts, ff2), rev), _tiled((ts, d), rev),
                  _tiled((ts, d), rev), _tiled((ts, d), rev), _resident(w_up_b.shape), _resident(w_down_b.shape),
                  _whole(cvec.shape), _whole(vecs.shape), _whole(pmats.shape)],
        out_specs=[_tiled((ts, ff2), rev), _tiled((ts, d), rev), _tiled((ts, d), rev), _whole((8, ff2)),
                   _whole((8, d))],
        out_shape=[jax.ShapeDtypeStruct((s_len, ff2), BF16), jax.ShapeDtypeStruct((s_len, d), F32),
                   jax.ShapeDtypeStruct((s_len, d), BF16), jax.ShapeDtypeStruct((8, ff2), F32),
                   jax.ShapeDtypeStruct((8, d), F32)],
        scratch_shapes=[pltpu.VMEM((16, ff2), F32)],
        args=(dfp, upp, yp, x2, dy, mixed, w_up_b, w_down_b, cvec, vecs, pmats))


def _mix_bwd(dmixed, dx2, x, proj, vecs, w_in_b, w_out_b, sgu_g, wm_b, wmt_b, bsb, wp_b, ps, ts, rider=None):
    s_len, d = x.shape
    nt = s_len // ts
    nblk = ts // HEAD
    n_proj = proj.shape[1]
    per = ts // POOL_HALO
    ext_len = ts + POOL_HALO

    def body(dmx_ref, dx2_ref, x_ref, proj_ref, projh_ref, vec_ref, win_ref, wout_ref, sg_ref, wm_ref, wmt_ref,
             bs_ref, wp_ref, ps_ref,
             gx_ref, dproj_ref, acc_ref, dwm_out, dwp_out, db_ref, dg_ref, dps_ref,
             carry_ref, dwm_ref, dwp_ref, dbz_ref):
        i = pl.program_id(0)
        tile = nt - 1 - i

        @pl.when(i == 0)
        def _():
            carry_ref[...] = jnp.zeros_like(carry_ref)
            for r in (acc_ref, dwm_ref, dwp_ref, dbz_ref, dg_ref, dps_ref):
                r[...] = jnp.zeros_like(r)

        dcat = _dot_nt(dmx_ref[...], wout_ref[...])
        t_glob = lax.broadcasted_iota(jnp.int32, (ts, HEAD), 0) + tile * ts
        for h in range(N_HEADS):
            hs = slice(h * HEAD, (h + 1) * HEAD)
            vs = slice(A_WIDTH + h * HEAD, A_WIDTH + (h + 1) * HEAD)
            au = proj_ref[:, hs].astype(F32)
            av = proj_ref[:, vs].astype(F32)
            u = _gelu(au)
            v = _gelu(av)
            rv = _rms(v)
            vhat = v * rv
            gain = sg_ref[h:h + 1, :]
            vn = (vhat * gain).astype(BF16)
            dout = dcat[:, hs]
            du_parts, dvn_parts = [], []
            for b in range(nblk):
                rs = slice(b * HEAD, (b + 1) * HEAD)
                z = _dot(wm_ref[h], vn[rs]) + bs_ref[h]
                du_parts.append(dout[rs] * z)
                dz = dout[rs] * u[rs]
                dbz_ref[h] += dz
                dzb = dz.astype(BF16)
                dwm_ref[h] += _dot_nt(dzb, vn[rs])
                dvn_parts.append(_dot(wmt_ref[h], dzb))
            du = jnp.concatenate(du_parts, axis=0)
            dvn = jnp.concatenate(dvn_parts, axis=0)
            dg_ref[h:h + 1, :] += _colsum(dvn * vhat)
            dvhat = dvn * gain
            dv = rv * (dvhat - vhat * _rowmean(dvhat * vhat))
            dproj_ref[:, hs] = (du * _gelu_grad(au)).astype(BF16)
            dproj_ref[:, vs] = (dv * _gelu_grad(av)).astype(BF16)
        for g in range(len(POOL_WINDOWS)):
            gs = slice(g * HEAD, (g + 1) * HEAD)
            pcols = slice(2 * A_WIDTH + g * HEAD, 2 * A_WIDTH + (g + 1) * HEAD)
            p = proj_ref[:, pcols].astype(F32)
            halo = jnp.where(tile > 0, projh_ref[:, pcols].astype(F32), 0.0)
            pb = _pool_fwd(p, halo, g, t_glob).astype(BF16)
            dyb = dcat[:, A_WIDTH + g * HEAD:A_WIDTH + (g + 1) * HEAD]
            dps_ref[0:1, gs] += _colsum(dyb * _dot(pb, wp_ref[g]))
            dyl = (dyb * ps_ref[0:1, gs]).astype(BF16)
            dwp_ref[g] += _dot_tn(pb, dyl)
            dpooled = _dot_nt(dyl, wp_ref[g])
            cnt = jnp.minimum(t_glob + 1, POOL_WINDOWS[g]).astype(F32)
            q = dpooled / cnt
            s = jnp.concatenate([q, carry_ref[:, gs]], axis=0)
            for step in range(g + 1):
                s = s + pltpu.roll(s, ext_len - (1 << step), 0)
            dproj_ref[:, pcols] = (s[:ts] - dpooled).astype(BF16)
            carry_ref[:, gs] = q[0:POOL_HALO]
        dh1 = _dot_nt(dproj_ref[...], win_ref[...])
        x = x_ref[...]
        r1 = _rms(x)
        xn = x * r1
        pre = _vrow(vec_ref, V_PRE_MIX)
        one_sc = 1.0 + _vrow(vec_ref, V_SC_M)
        acc_ref[M_DSHM:M_DSHM + 1, :] += _colsum(dh1)
        acc_ref[M_DSCM:M_DSCM + 1, :] += _colsum(dh1 * (xn * pre))
        acc_ref[M_DPREMIX:M_DPREMIX + 1, :] += _colsum(dh1 * xn * one_sc)
        dxn = dh1 * pre * one_sc
        gx_ref[...] = dx2_ref[...] + r1 * (dxn - xn * _rowmean(dxn * xn))

        @pl.when(i == nt - 1)
        def _():
            dwm_out[...] = dwm_ref[...].astype(BF16)
            dwp_out[...] = dwp_ref[...].astype(BF16)
            db_ref[...] = jnp.zeros_like(db_ref)
            for h in range(N_HEADS):
                db_ref[h:h + 1, :] = jnp.sum(dbz_ref[h].T, axis=0, keepdims=True)

    rev = lambda i: (nt - 1 - i, 0)
    halo_map = lambda i: (jnp.maximum((nt - 1 - i) * per - 1, 0), 0)
    hshape = (N_HEADS, HEAD, HEAD)
    return _call(
        body, name="mix_bwd", grid=(nt,), rider=rider,
        in_specs=[_tiled((ts, d), rev), _tiled((ts, d), rev), _tiled((ts, d), rev), _tiled((ts, n_proj), rev),
                  _tiled((POOL_HALO, n_proj), halo_map), _whole(vecs.shape), _resident(w_in_b.shape),
                  _resident(w_out_b.shape), _whole(sgu_g.shape), _whole(wm_b.shape), _whole(wmt_b.shape),
                  _whole(bsb.shape), _whole(wp_b.shape), _whole(ps.shape)],
        out_specs=[_tiled((ts, d), rev), _tiled((ts, n_proj), rev), _whole((8, d)), _whole(hshape), _whole(hshape),
                   _whole((8, HEAD)), _whole((8, HEAD)), _whole((8, A_WIDTH))],
        out_shape=[jax.ShapeDtypeStruct((s_len, d), F32), jax.ShapeDtypeStruct((s_len, n_proj), BF16),
                   jax.ShapeDtypeStruct((8, d), F32), jax.ShapeDtypeStruct(hshape, BF16),
                   jax.ShapeDtypeStruct(hshape, BF16), jax.ShapeDtypeStruct((8, HEAD), F32),
                   jax.ShapeDtypeStruct((8, HEAD), F32), jax.ShapeDtypeStruct((8, A_WIDTH), F32)],
        scratch_shapes=[pltpu.VMEM((POOL_HALO, A_WIDTH), F32), pltpu.VMEM(hshape, F32), pltpu.VMEM(hshape, F32),
                        pltpu.VMEM(hshape, F32)],
        args=(dmixed, dx2, x, proj, proj, vecs, w_in_b, w_out_b, sgu_g, wm_b, wmt_b, bsb, wp_b, ps))


def _wgrad(a, b, tn, ts, name, rider=None):
    s_len, m = a.shape
    n = b.shape[1]
    ts = min(ts, s_len)

    def body(a_ref, b_ref, o_ref):
        @pl.when(pl.program_id(1) == 0)
        def _():
            o_ref[...] = jnp.zeros_like(o_ref)

        o_ref[...] += _dot_tn(a_ref[...], b_ref[...])

    (g,), r_out = _call(
        body, name=name, grid=(n // tn, s_len // ts), rider=rider,
        in_specs=[pl.BlockSpec((ts, m), lambda j, s: (s, 0)), pl.BlockSpec((ts, tn), lambda j, s: (s, j))],
        out_specs=[pl.BlockSpec((m, tn), lambda j, s: (0, j))],
        out_shape=[jax.ShapeDtypeStruct((m, n), F32)], scratch_shapes=[], args=(a, b))
    return g, r_out


def _wgrad_pair(a1, b1, a2, b2, ts, name, rider=None):
    s_len = a1.shape[0]
    ts = min(ts, s_len)
    shapes = [(a1.shape[1], b1.shape[1]), (a2.shape[1], b2.shape[1])]

    def body(a1_ref, b1_ref, a2_ref, b2_ref, o1_ref, o2_ref):
        @pl.when(pl.program_id(0) == 0)
        def _():
            o1_ref[...] = jnp.zeros_like(o1_ref)
            o2_ref[...] = jnp.zeros_like(o2_ref)

        o1_ref[...] += _dot_tn(a1_ref[...], b1_ref[...])
        o2_ref[...] += _dot_tn(a2_ref[...], b2_ref[...])

    row = lambda s: (s, 0)
    return _call(
        body, name=name, grid=(s_len // ts,), rider=rider,
        in_specs=[pl.BlockSpec((ts, t.shape[1]), row) for t in (a1, b1, a2, b2)],
        out_specs=[_whole(sh) for sh in shapes],
        out_shape=[jax.ShapeDtypeStruct(sh, F32) for sh in shapes], scratch_shapes=[], args=(a1, b1, a2, b2))


def _adamw_big(g, w, m, v, name):
    r, cdim = g.shape
    tr = r
    while tr * cdim * 4 > (3 << 19) and tr % 16 == 0:
        tr //= 2

    def body(g_ref, w_ref, m_ref, v_ref, d_ref, nm_ref, nv_ref):
        delta, m2, v2 = _adamw_math(w_ref[0], g_ref[...], m_ref[0], v_ref[0])
        d_ref[0] = delta
        nm_ref[0] = m2
        nv_ref[0] = v2

    s3 = pl.BlockSpec((1, tr, cdim), lambda i: (0, i, 0))
    return pl.pallas_call(
        body, name=name, grid=(r // tr,),
        in_specs=[pl.BlockSpec((tr, cdim), lambda i: (i, 0)), s3, s3, s3],
        out_specs=[s3, s3, s3],
        out_shape=[jax.ShapeDtypeStruct(w.shape, F32)] * 3,
        compiler_params=pltpu.CompilerParams(dimension_semantics=("parallel",), vmem_limit_bytes=VMEM_LIMIT_BYTES),
    )(g, w, m, v)


def _wada_update(sct, gm, w, m, v):
    _, r, cdim = w.shape
    tr = 256
    kp = sct.shape[1]

    def body(s_ref, g_ref, w_ref, m_ref, v_ref, gw_ref, d_ref, nm_ref, nv_ref):
        g = _dot(s_ref[...], g_ref[...])
        gw_ref[0] = g
        delta, m2, v2 = _adamw_math(w_ref[0], g, m_ref[0], v_ref[0])
        d_ref[0] = delta
        nm_ref[0] = m2
        nv_ref[0] = v2

    s3 = pl.BlockSpec((1, tr, cdim), lambda i: (0, i, 0))
    return pl.pallas_call(
        body, name="wada_update", grid=(r // tr,),
        in_specs=[pl.BlockSpec((tr, kp), lambda i: (i, 0)), _whole(gm.shape), s3, s3, s3],
        out_specs=[s3, s3, s3, s3],
        out_shape=[jax.ShapeDtypeStruct(w.shape, F32)] * 4,
        compiler_params=pltpu.CompilerParams(dimension_semantics=("parallel",), vmem_limit_bytes=VMEM_LIMIT_BYTES),
    )(sct, gm, w, m, v)


def _small_update(g1, g2, g2s, gwm, gwp, gbz, gsg, gps, params):
    names = ["b_ada", "pre_mix_g", "post_mix_g", "sgu_norm_g", "w_spatial", "b_spatial", "w_pool", "pool_scale",
             "pre_ffn_g", "post_ffn_g", "conv_w", "conv_b"]
    d = g1.shape[2]
    flat_in = [g1, g2, g2s, gwm, gwp, gbz, gsg, gps]
    n_g = len(flat_in)
    for nm in names:
        flat_in += list(params[nm])

    def body(*refs):
        g1_ref, g2_ref, g2s_ref, gwm_ref, gwp_ref, gbz_ref, gsg_ref, gps_ref = refs[:n_g]
        wmv = refs[n_g:n_g + 3 * len(names)]
        loss_ref = refs[n_g + 3 * len(names)]
        outs = refs[n_g + 3 * len(names) + 1:]

        def dsum(ref, idx):
            acc = ref[(0,) + idx].astype(F32)
            for dev in range(1, N_DEV):
                acc = acc + ref[(dev,) + idx].astype(F32)
            return acc

        def apply(pi, g, widx, oidx):
            w_ref, m_ref, v_ref = wmv[3 * pi:3 * pi + 3]
            g_ref, d_ref, nm_ref, nv_ref = outs[4 * pi:4 * pi + 4]
            delta, m2, v2 = _adamw_math(w_ref[widx], g, m_ref[widx], v_ref[widx])
            g_ref[oidx] = g
            d_ref[oidx] = delta
            nm_ref[oidx] = m2
            nv_ref[oidx] = v2

        def row1(base, r):
            return (slice(base + r, base + r + 1), slice(None))

        tot = dsum(g1_ref, row1(0, F_LOSS))
        loss_ref[...] = jnp.zeros(loss_ref.shape, F32) + jnp.sum(tot) * (0.5 / d)
        mod_rows = [row1(16, M_DSHM), row1(16, M_DSCM), row1(8, B_DGM), row1(8, B_DSHF), row1(8, B_DSCF),
                    row1(0, F_DGF)]
        for j, rr in enumerate(mod_rows):
            cs = (slice(None), slice(j * d, (j + 1) * d))
            apply(0, dsum(g1_ref, rr), cs, cs)
        full = (slice(None), slice(None))
        apply(1, dsum(g1_ref, row1(16, M_DPREMIX)), full, full)
        apply(2, dsum(g1_ref, row1(8, B_DPOSTMIX)), full, full)
        apply(3, dsum(gsg_ref, (slice(0, N_HEADS), slice(None))), (0,), (0,))
        pos_i = lax.broadcasted_iota(jnp.int32, (HEAD, HEAD), 0)
        pos_j = lax.broadcasted_iota(jnp.int32, (HEAD, HEAD), 1)
        causal = (pos_j // CHUNK) <= (pos_i // CHUNK)
        for h in range(N_HEADS):
            blk = (slice(h * HEAD, (h + 1) * HEAD), slice(None))
            apply(4, jnp.where(causal, dsum(gwm_ref, blk), 0.0), (0, h), (0, h))
            apply(5, dsum(gbz_ref, (slice(h, h + 1), slice(None))), (0, slice(h, h + 1)), (0, slice(h, h + 1)))
            apply(6, dsum(gwp_ref, blk), (0, h), (0, h))
        apply(7, dsum(gps_ref, (slice(0, 1), slice(None))), full, full)
        apply(8, dsum(g1_ref, row1(8, B_DPREFFN)), full, full)
        apply(9, dsum(g1_ref, row1(0, F_DPOSTFFN)), full, full)
        apply(10, dsum(g2s_ref, (slice(C_DCW, C_DCW + 3), slice(None))), (0,), (0,))
        apply(11, dsum(g2_ref, (slice(C_DCB, C_DCB + 1), slice(None))), full, full)

    out_shape = [jax.ShapeDtypeStruct((8, HEAD), F32)]
    for nm in names:
        out_shape += [jax.ShapeDtypeStruct(params[nm][0].shape, F32)] * 4
    res = pl.pallas_call(
        body, name="small_update", out_shape=out_shape,
        compiler_params=pltpu.CompilerParams(vmem_limit_bytes=VMEM_LIMIT_BYTES),
    )(*flat_in)
    out = {nm: tuple(res[1 + 4 * i:5 + 4 * i]) for i, nm in enumerate(names)}
    return res[0], out


def kernel(x, c, w_ada, b_ada, pre_mix_g, post_mix_g, w_in, sgu_norm_g, w_spatial, b_spatial, w_pool, pool_scale, w_out, pre_ffn_g, post_ffn_g, w_up, conv_w, conv_b, w_down, loss_target, m_w_ada, m_b_ada, m_pre_mix_g, m_post_mix_g, m_w_in, m_sgu_norm_g, m_w_spatial, m_b_spatial, m_w_pool, m_pool_scale, m_w_out, m_pre_ffn_g, m_post_ffn_g, m_w_up, m_conv_w, m_conv_b, m_w_down, v_w_ada, v_b_ada, v_pre_mix_g, v_post_mix_g, v_w_in, v_sgu_norm_g, v_w_spatial, v_b_spatial, v_w_pool, v_pool_scale, v_w_out, v_pre_ffn_g, v_post_ffn_g, v_w_up, v_conv_w, v_conv_b, v_w_down):
    xi, yi, ci = _mesh_pos()
    k_me = 2 * xi + yi
    dev = 2 * k_me + ci
    s_len, d = x.shape[1], x.shape[2]
    x2d = x[0]
    tgt = loss_target[0]
    ff2 = conv_b.shape[1]
    n_ada = w_ada.shape[2]
    n_cw = conv_w.shape[2]

    cw_blk = jnp.concatenate([conv_w[0], jnp.zeros((5, n_cw), F32)], axis=0)
    c_all, cw_all = _run_rider(_allgather_rider([c.reshape(8, d // 8), cw_blk]), "gather_c_convw")
    c_all = c_all.reshape(N_DEV, 8, d // 8).reshape(N_DEV, d)
    cw_full = jnp.concatenate([cw_all[16 * k:16 * k + 8] for k in range(N_CHIP)], axis=1)
    cvec = jnp.concatenate([cw_full[0:3], conv_b, jnp.zeros((4, ff2), F32)], axis=0)
    b_shard = lax.dynamic_slice_in_dim(b_ada, k_me * n_ada, n_ada, axis=1)
    mod_k, sc_all = _mod_shard(c_all, w_ada[0], b_shard)
    (mod_g,) = _run_rider(_allgather_rider([mod_k]), "gather_mod")
    mod_all = jnp.concatenate([mod_g[16 * k:16 * k + 8] for k in range(N_CHIP)], axis=1)
    mod_me = lax.dynamic_slice_in_dim(mod_all, dev, 1, axis=0).reshape(6, d)
    vecs = jnp.concatenate([mod_me, pre_mix_g, post_mix_g, pre_ffn_g, post_ffn_g, jnp.zeros((6, d), F32)], axis=0)

    w_in_s, w_out_s, w_up_s, w_down_s = _cast_bf16([w_in[0], w_out[0], w_up[0], w_down[0]])
    w_in_b, w_out_b = _run_rider(_gather_rider([w_in_s, w_out_s], (True, False)), "gather_w_mix")

    pos = jnp.arange(HEAD)
    causal = (pos[None, :] // CHUNK) <= (pos[:, None] // CHUNK)
    wm = jnp.where(causal[None], w_spatial[0], 0.0)
    wm_b = wm.astype(BF16)
    wmt_b = jnp.swapaxes(wm, 1, 2).astype(BF16)
    bsb = jnp.broadcast_to(b_spatial[0][:, :, None], (N_HEADS, HEAD, HEAD))
    wp_b = w_pool[0].astype(BF16)
    sgu_g = jnp.concatenate([sgu_norm_g[0], jnp.zeros((4, HEAD), F32)], axis=0)
    ps = jnp.concatenate([pool_scale, jnp.zeros((7, A_WIDTH), F32)], axis=0)

    pmats = _perm_mats(FFN_TS)
    (h1, proj, cat, mixed, x2, h2p), (w_up_b, w_down_b) = _mix_fwd(
        x2d, vecs, w_in_b, w_out_b, sgu_g, wm_b, bsb, wp_b, ps, pmats, ts=512,
        rider=_gather_rider([w_up_s, w_down_s], (True, False)))
    up, yv, act, dy, dfp, acc_f = _ffn_fwd(h2p, x2, tgt, w_up_b, w_down_b, cvec, vecs, pmats, ts=FFN_TS)

    c_idx = ci.reshape(1).astype(jnp.int32)
    g_w_down, _ = _wgrad(act, dfp, d, 1024, "wgrad_down")
    (dup, dx2, dmixed, acc_c, acc_b), (land_down,) = _ffn_bwd(
        dfp, up, yv, x2, dy, mixed, w_up_b, w_down_b, cvec, vecs, pmats, ts=FFN_TS,
        rider=_sibling_rider([g_w_down], (False,)))
    (part_down,) = _sum_with_sibling([g_w_down], [land_down], (False,), c_idx, "pair_sum_down")
    g_w_up, (chips_down,) = _wgrad(h2p, dup, ff2 // 2, 2048, "wgrad_up", rider=_chips_rider([part_down]))
    (gx, dproj, acc_m, dwm, dwp, dbz, dsg, dps), (land_up,) = _mix_bwd(
        dmixed, dx2, x2d, proj, vecs, w_in_b, w_out_b, sgu_g, wm_b, wmt_b, bsb, wp_b, ps, ts=256,
        rider=_sibling_rider([g_w_up], (True,)))
    (part_up,) = _sum_with_sibling([g_w_up], [land_up], (True,), c_idx, "pair_sum_up")
    (g_w_out, g_w_in), (chips_up,) = _wgrad_pair(cat, dmixed, h1, dproj, 1024, "wgrad_mix",
                                                 rider=_chips_rider([part_up]))
    land_mix = _run_rider(_sibling_rider([g_w_in, g_w_out], (True, False)), "reduce_to_sibling")
    parts_mix = _sum_with_sibling([g_w_in, g_w_out], land_mix, (True, False), c_idx, "pair_sum_mix")
    chips_in, chips_out = _run_rider(_chips_rider(parts_mix), "reduce_over_chips")
    reduced = _sum_chips_and_share([chips_in, chips_out, chips_up, chips_down])
    big = {}
    for nm, red, (w, m, v) in zip(("w_in", "w_out", "w_up", "w_down"), reduced,
                                  ((w_in, m_w_in, v_w_in), (w_out, m_w_out, v_w_out),
                                   (w_up, m_w_up, v_w_up), (w_down, m_w_down, v_w_down))):
        g = red.reshape(w.shape[1], w.shape[2])
        big[nm] = (g.reshape(w.shape),) + tuple(_adamw_big(g, w, m, v, "adamw_" + nm))

    g1 = jnp.concatenate([acc_f, acc_b, acc_m], axis=0)
    hflat = (N_HEADS * HEAD, HEAD)
    gathered = _run_rider(_allgather_rider(
        [g1, acc_c, dwm.reshape(hflat), dwp.reshape(hflat), dbz, dsg, dps]), "gather_small_grads")
    g1a, g2a, gwm, gwp, gbz, gsg, gps = [t.reshape((N_DEV, t.shape[0] // N_DEV, t.shape[1])) for t in gathered]
    g2s = lax.dynamic_slice_in_dim(g2a, k_me * n_cw, n_cw, axis=2)
    params = {
        "b_ada": (b_ada, m_b_ada, v_b_ada), "pre_mix_g": (pre_mix_g, m_pre_mix_g, v_pre_mix_g),
        "post_mix_g": (post_mix_g, m_post_mix_g, v_post_mix_g),
        "sgu_norm_g": (sgu_norm_g, m_sgu_norm_g, v_sgu_norm_g), "w_spatial": (w_spatial, m_w_spatial, v_w_spatial),
        "b_spatial": (b_spatial, m_b_spatial, v_b_spatial), "w_pool": (w_pool, m_w_pool, v_w_pool),
        "pool_scale": (pool_scale, m_pool_scale, v_pool_scale), "pre_ffn_g": (pre_ffn_g, m_pre_ffn_g, v_pre_ffn_g),
        "post_ffn_g": (post_ffn_g, m_post_ffn_g, v_post_ffn_g), "conv_w": (conv_w, m_conv_w, v_conv_w),
        "conv_b": (conv_b, m_conv_b, v_conv_b),
    }
    loss_slab, small = _small_update(g1a, g2a, g2s, gwm, gwp, gbz, gsg, gps, params)

    gmod_all = jnp.concatenate(
        [g1a[:, 16 + M_DSHM], g1a[:, 16 + M_DSCM], g1a[:, 8 + B_DGM], g1a[:, 8 + B_DSHF], g1a[:, 8 + B_DSCF],
         g1a[:, F_DGF]], axis=1)
    gm = lax.dynamic_slice_in_dim(gmod_all, k_me * n_ada, n_ada, axis=1)
    gm = jnp.concatenate([gm, jnp.zeros((HEAD - N_DEV, n_ada), F32)], axis=0)
    sct = jnp.concatenate([sc_all.T, jnp.zeros((d, HEAD - N_DEV), F32)], axis=1)
    ada = tuple(_wada_update(sct, gm, w_ada, m_w_ada, v_w_ada))

    everything = dict(small)
    everything.update(big)
    everything["w_ada"] = ada
    order = ["w_ada", "b_ada", "pre_mix_g", "post_mix_g", "w_in", "sgu_norm_g", "w_spatial", "b_spatial", "w_pool",
             "pool_scale", "w_out", "pre_ffn_g", "post_ffn_g", "w_up", "conv_w", "conv_b", "w_down"]
    outs = [loss_slab[0, 0], gx.reshape(x.shape)]
    for j in range(4):
        outs += [everything[nm][j] for nm in order]
    return tuple(outs)
```

```python
import functools

import jax
import jax.numpy as jnp
from jax import lax
from jax.experimental import pallas as pl
from jax.experimental.pallas import tpu as pltpu

F32 = jnp.float32
BF16 = jnp.bfloat16
MESH = pl.DeviceIdType.MESH

EPS = 1e-6
HEAD = 128
N_HEADS = 4
A_WIDTH = N_HEADS * HEAD
CHUNK = 64
POOL_WINDOWS = (2, 4, 8, 16)
POOL_HALO = 16
FFN_TS = 256

ADAM_LR = 0.001
ADAM_B1 = 0.9
ADAM_B2 = 0.999
ADAM_EPS = 1e-08
ADAM_WD = 0.01
ADAM_STEP = 10

VMEM_LIMIT_BYTES = 58 * 1024 * 1024
N_DEV = 8
N_CHIP = 4


def _dot(a, b):
    return jnp.dot(a, b, preferred_element_type=F32)


def _dot_nt(a, b):
    return lax.dot_general(a, b, (((1,), (1,)), ((), ())), preferred_element_type=F32)


def _dot_tn(a, b):
    return lax.dot_general(a, b, (((0,), (0,)), ((), ())), preferred_element_type=F32)


def _gelu(x):
    return x * (0.5 * (1.0 + jnp.tanh(0.7978845608028654 * (x + 0.044715 * (x * x * x)))))


def _gelu_grad(x):
    t = jnp.tanh(0.7978845608028654 * (x + 0.044715 * (x * x * x)))
    return 0.5 * (1.0 + t) + (0.5 * x) * (1.0 - t * t) * (0.7978845608028654 * (1.0 + 0.134145 * (x * x)))


def _sigmoid(x):
    return 1.0 / (1.0 + jnp.exp(-x))


def _rms(x):
    return lax.rsqrt(jnp.mean(x * x, axis=-1, keepdims=True) + EPS)


def _colsum(x):
    return jnp.sum(x, axis=0, keepdims=True)


def _rowmean(x):
    return jnp.mean(x, axis=-1, keepdims=True)


def _tiled(shape, index_map):
    return pl.BlockSpec(shape, index_map)


def _resident(shape):
    nd = len(shape)
    return pl.BlockSpec(shape, lambda *_: (0,) * nd, pipeline_mode=pl.Buffered(1))


def _whole(shape):
    nd = len(shape)
    return pl.BlockSpec(shape, lambda *_: (0,) * nd)


def _seq_params():
    return pltpu.CompilerParams(dimension_semantics=("arbitrary",), vmem_limit_bytes=VMEM_LIMIT_BYTES)


def _ff_chunks(f, width=768):
    out, o = [], 0
    while o < f:
        w = min(width, f - o)
        out.append((o, w))
        o += w
    return out


def _pool_fwd(p, halo, g, t_glob):
    ext = jnp.concatenate([halo, p], axis=0)
    s = ext
    for step in range(g + 1):
        s = s + pltpu.roll(s, 1 << step, 0)
    cnt = jnp.minimum(t_glob + 1, POOL_WINDOWS[g]).astype(F32)
    return s[POOL_HALO:] / cnt - p


def _adamw_math(w, g, m, v):
    m = ADAM_B1 * m + (1.0 - ADAM_B1) * g
    v = ADAM_B2 * v + (1.0 - ADAM_B2) * (g * g)
    m_hat = m / (1.0 - ADAM_B1 ** ADAM_STEP)
    v_hat = v / (1.0 - ADAM_B2 ** ADAM_STEP)
    delta = -ADAM_LR * (m_hat / (jnp.sqrt(v_hat) + ADAM_EPS) + ADAM_WD * w)
    return delta, m, v


def _mesh_pos():
    return lax.axis_index("x"), lax.axis_index("y"), lax.axis_index("c")


class _Rider:
    def __init__(self, inputs, out_shape, sems, start, finish):
        self.inputs, self.out_shape, self.sems = list(inputs), list(out_shape), list(sems)
        self.start, self.finish = start, finish


def _call(body, *, name, grid, in_specs, out_specs, out_shape, scratch_shapes, args, rider=None):
    params = pltpu.CompilerParams(dimension_semantics=("arbitrary",) * len(grid), vmem_limit_bytes=VMEM_LIMIT_BYTES)
    if rider is None:
        res = pl.pallas_call(body, name=name, grid=grid, in_specs=in_specs, out_specs=out_specs, out_shape=out_shape,
                             scratch_shapes=scratch_shapes, compiler_params=params)(*args)
        return tuple(res), ()
    cuts = [len(in_specs), len(rider.inputs), len(out_specs), len(rider.out_shape), len(scratch_shapes),
            len(rider.sems)]

    def hosted(*refs):
        groups, a = [], 0
        for cnt in cuts:
            groups.append(refs[a:a + cnt])
            a += cnt
        ins, r_in, outs, r_out, scr, r_sem = groups
        first = functools.reduce(jnp.logical_and, [pl.program_id(k) == 0 for k in range(len(grid))])
        last = functools.reduce(jnp.logical_and, [pl.program_id(k) == grid[k] - 1 for k in range(len(grid))])

        @pl.when(first)
        def _():
            rider.start(r_in, r_out, r_sem)

        body(*ins, *outs, *scr)

        @pl.when(last)
        def _():
            rider.finish(r_in, r_out, r_sem)

    anyspec = pl.BlockSpec(memory_space=pl.ANY)
    res = pl.pallas_call(
        hosted, name=name, grid=grid,
        in_specs=list(in_specs) + [anyspec] * cuts[1], out_specs=list(out_specs) + [anyspec] * cuts[3],
        out_shape=list(out_shape) + rider.out_shape, scratch_shapes=list(scratch_shapes) + rider.sems,
        compiler_params=params)(*args, *rider.inputs)
    return tuple(res[:cuts[2]]), tuple(res[cuts[2]:])


def _run_rider(rider, name):
    n_in, n_out = len(rider.inputs), len(rider.out_shape)

    def body(*refs):
        r_in, r_out, r_sem = refs[:n_in], refs[n_in:n_in + n_out], refs[n_in + n_out:]
        rider.start(r_in, r_out, r_sem)
        rider.finish(r_in, r_out, r_sem)

    anyspec = pl.BlockSpec(memory_space=pl.ANY)
    return pl.pallas_call(body, name=name, out_shape=rider.out_shape, in_specs=[anyspec] * n_in,
                          out_specs=[anyspec] * n_out, scratch_shapes=rider.sems)(*rider.inputs)


def _allgather_rider(arrs):
    n = len(arrs)

    def plan(ins, outs, sems):
        send_sems, recv_sems, local_sems = sems
        x, y, c = _mesh_pos()
        me, sibling = (x, y, c), (x, y, 1 - c)
        chips = [(1 - x, y), (x, 1 - y), (1 - x, 1 - y)]

        def rows(a, px, py, pc):
            r = ins[a].shape[0]
            return outs[a].at[pl.ds(pl.multiple_of((4 * px + 2 * py + pc) * r, 8), r), :]

        def copy(a, k, block, to, src=None):
            return pltpu.make_async_remote_copy(
                src_ref=rows(a, *block) if src is None else src, dst_ref=rows(a, *block),
                send_sem=send_sems.at[a * 7 + k], recv_sem=recv_sems.at[a * 7 + k],
                device_id=to, device_id_type=MESH)

        local = [pltpu.make_async_copy(ins[a], rows(a, *me), local_sems.at[a]) for a in range(n)]
        first = []
        for a in range(n):
            first.append(copy(a, 0, me, sibling, src=ins[a]))
            first += [copy(a, 1 + j, me, (*chip, c), src=ins[a]) for j, chip in enumerate(chips)]
        return c, me, sibling, chips, copy, local, first

    def start(ins, outs, sems):
        *_, local, first = plan(ins, outs, sems)
        for cp in local + first:
            cp.start()

    def finish(ins, outs, sems):
        c, me, sibling, chips, copy, local, first = plan(ins, outs, sems)
        passed = []
        for a in range(n):
            for j, chip in enumerate(chips):
                copy(a, 1 + j, (*chip, c), me).wait_recv()
                fwd = copy(a, 4 + j, (*chip, c), sibling)
                fwd.start()
                passed.append(fwd)
        for a in range(n):
            copy(a, 0, sibling, me).wait_recv()
            for j, chip in enumerate(chips):
                copy(a, 4 + j, (*chip, 1 - c), me).wait_recv()
        for cp in first + passed:
            cp.wait_send()
        for mine in local:
            mine.wait()

    return _Rider(arrs, [jax.ShapeDtypeStruct((N_DEV * a.shape[0], a.shape[1]), a.dtype) for a in arrs],
                  [pltpu.SemaphoreType.DMA((7 * n,)), pltpu.SemaphoreType.DMA((7 * n,)),
                   pltpu.SemaphoreType.DMA((n,))], start, finish)


def _piece(ref, col_sharded, k, h):
    m, n = ref.shape
    if col_sharded:
        mh, nc = m // 2, n // N_CHIP
        return ref.at[pl.ds(pl.multiple_of(h * mh, 16), mh), pl.ds(pl.multiple_of(k * nc, 128), nc)]
    rp = m // (2 * N_CHIP)
    return ref.at[pl.ds(pl.multiple_of((2 * k + h) * rp, 16), rp), :]


def _piece_shape(shape, col_sharded):
    m, n = shape
    return (m // 2, n // N_CHIP) if col_sharded else (m // (2 * N_CHIP), n)


def _cast_bf16(arrs):
    def body(*refs):
        for i_ref, o_ref in zip(refs[:len(arrs)], refs[len(arrs):]):
            o_ref[...] = i_ref[...].astype(BF16)

    return pl.pallas_call(body, name="cast_weights", out_shape=[jax.ShapeDtypeStruct(a.shape, BF16) for a in arrs],
                          compiler_params=pltpu.CompilerParams(vmem_limit_bytes=VMEM_LIMIT_BYTES))(*arrs)


def _full_shapes(shards, col_flags):
    return [(s.shape[0], s.shape[1] * N_CHIP) if col else (s.shape[0] * N_CHIP, s.shape[1])
            for s, col in zip(shards, col_flags)]


def _ici_copies(shard_refs, full_refs, send_sems, recv_sems, col_flags):
    x, y, c = _mesh_pos()
    k_me = 2 * x + y
    copies = []
    for a, (s_ref, f_ref) in enumerate(zip(shard_refs, full_refs)):
        rows = s_ref.shape[0] // 2
        src = s_ref.at[pl.ds(pl.multiple_of(c * rows, 16), rows), :]
        for j, chip in enumerate([(1 - x, y), (x, 1 - y), (1 - x, 1 - y)]):
            copies.append(pltpu.make_async_remote_copy(
                src_ref=src, dst_ref=_piece(f_ref, col_flags[a], k_me, c),
                send_sem=send_sems.at[a * 3 + j], recv_sem=recv_sems.at[a * 3 + j],
                device_id=(*chip, c), device_id_type=MESH))
    return copies


def _gather_begin(shards, col_flags, after, name):
    n = len(shards)
    hbm = pl.BlockSpec(memory_space=pltpu.HBM)
    sem = pl.BlockSpec(memory_space=pltpu.SEMAPHORE)
    fulls = [lax.empty(fs, BF16) for fs in _full_shapes(shards, col_flags)]

    def body(*refs):
        shard_refs, full_refs = refs[:n], refs[n:2 * n]
        send_sems, recv_sems = refs[2 * n + 1], refs[2 * n + 2]
        token = refs[-1]
        for cp in _ici_copies(shard_refs, full_refs, send_sems, recv_sems, col_flags):
            cp.start()
        token[...] = jnp.zeros_like(token)

    args = [pltpu.with_memory_space_constraint(t, pltpu.HBM) for t in list(shards) + fulls]
    res = pl.pallas_call(
        body, name=name,
        out_shape=[pltpu.SemaphoreType.DMA((3 * n,)), pltpu.SemaphoreType.DMA((3 * n,))]
        + [pltpu.HBM(t.shape, t.dtype) for t in args] + [jax.ShapeDtypeStruct((8, HEAD), F32)],
        in_specs=[hbm] * (2 * n) + [pl.BlockSpec(memory_space=pl.ANY)],
        out_specs=[sem, sem] + [hbm] * (2 * n) + [pl.BlockSpec(memory_space=pltpu.VMEM)],
        input_output_aliases={i: 2 + i for i in range(2 * n)},
        compiler_params=pltpu.CompilerParams(has_side_effects=pltpu.SideEffectType.DATAFLOW_SIDE_EFFECTING),
    )(*args, after)
    return res[0], res[1], list(res[2:2 + n]), list(res[2 + n:2 + 2 * n]), res[-1]


def _gather_end(handle, col_flags, after, name):
    send_sems, recv_sems, shards, fulls, _ = handle
    n = len(shards)
    hbm = pl.BlockSpec(memory_space=pltpu.HBM)
    sem = pl.BlockSpec(memory_space=pltpu.SEMAPHORE)

    def body(*refs):
        shard_refs, full_refs = refs[:n], refs[n:2 * n]
        for cp in _ici_copies(shard_refs, full_refs, refs[2 * n], refs[2 * n + 1], col_flags):
            cp.wait_send()
            cp.wait_recv()

    res = pl.pallas_call(
        body, name=name,
        out_shape=[pltpu.HBM(t.shape, t.dtype) for t in shards + fulls],
        in_specs=[hbm] * (2 * n) + [sem, sem, pl.BlockSpec(memory_space=pl.ANY)],
        out_specs=[hbm] * (2 * n),
        input_output_aliases={i: i for i in range(2 * n)},
        compiler_params=pltpu.CompilerParams(has_side_effects=pltpu.SideEffectType.DATAFLOW_SIDE_EFFECTING),
    )(*shards, *fulls, send_sems, recv_sems, after)
    return list(res[:n]), list(res[n:])


def _gather_finish(shards, fulls, col_flags, name):
    n = len(shards)

    def body(*refs):
        shard_refs, full_refs = refs[:n], refs[2 * n:3 * n]
        send_sems, recv_sems, local_sems = refs[3 * n:]
        x, y, c = _mesh_pos()
        chips = [(1 - x, y), (x, 1 - y), (1 - x, 1 - y)]
        k_me = 2 * x + y
        local, passed, arriving = [], [], []
        for a in range(n):
            m, nn = shard_refs[a].shape
            own = (full_refs[a].at[:, pl.ds(pl.multiple_of(k_me * nn, 128), nn)] if col_flags[a]
                   else full_refs[a].at[pl.ds(pl.multiple_of(k_me * m, 16), m), :])
            local.append(pltpu.make_async_copy(shard_refs[a], own, local_sems.at[a]))
            for j, chip in enumerate(chips):
                k_from = 2 * chip[0] + chip[1]
                for h, group in ((c, passed), (1 - c, arriving)):
                    win = _piece(full_refs[a], col_flags[a], k_from, h)
                    group.append(pltpu.make_async_remote_copy(
                        src_ref=win, dst_ref=win, send_sem=send_sems.at[a * 3 + j],
                        recv_sem=recv_sems.at[a * 3 + j], device_id=(x, y, 1 - c), device_id_type=MESH))
        for cp in local + passed:
            cp.start()
        for cp in arriving:
            cp.wait_recv()
        for cp in passed:
            cp.wait_send()
        for cp in local:
            cp.wait()

    anyspec = pl.BlockSpec(memory_space=pl.ANY)
    return pl.pallas_call(
        body, name=name,
        out_shape=[jax.ShapeDtypeStruct(f.shape, f.dtype) for f in fulls],
        in_specs=[anyspec] * (2 * n), out_specs=[anyspec] * n,
        input_output_aliases={n + a: a for a in range(n)},
        scratch_shapes=[pltpu.SemaphoreType.DMA((3 * n,)), pltpu.SemaphoreType.DMA((3 * n,)),
                        pltpu.SemaphoreType.DMA((n,))],
    )(*shards, *fulls)


def _sibling_rider(grads, col_flags):
    n = len(grads)
    pshapes = [_piece_shape(g.shape, col) for g, col in zip(grads, col_flags)]

    def copies(ins, outs, sems):
        send_sems, recv_sems = sems
        x, y, c = _mesh_pos()
        return [pltpu.make_async_remote_copy(
            src_ref=_piece(ins[a], col_flags[a], k, 1 - c), dst_ref=outs[a].at[k],
            send_sem=send_sems.at[a * N_CHIP + k], recv_sem=recv_sems.at[a * N_CHIP + k],
            device_id=(x, y, 1 - c), device_id_type=MESH) for a in range(n) for k in range(N_CHIP)]

    def start(ins, outs, sems):
        for cp in copies(ins, outs, sems):
            cp.start()

    def finish(ins, outs, sems):
        cps = copies(ins, outs, sems)
        for cp in cps:
            cp.wait_recv()
        for cp in cps:
            cp.wait_send()

    return _Rider(grads, [jax.ShapeDtypeStruct((N_CHIP,) + ps, g.dtype) for ps, g in zip(pshapes, grads)],
                  [pltpu.SemaphoreType.DMA((N_CHIP * n,)), pltpu.SemaphoreType.DMA((N_CHIP * n,))], start, finish)


def _sum_with_sibling(grads, landed, col_flags, c_idx, name):
    n = len(grads)
    pshapes = [_piece_shape(g.shape, col) for g, col in zip(grads, col_flags)]

    def body(c_ref, *refs):
        ins, lands, outs = refs[:n], refs[n:2 * n], refs[2 * n:]
        for a in range(n):
            outs[a][0] = (ins[a][...] + lands[a][0]).astype(BF16)

    in_specs = []
    for ps, col in zip(pshapes, col_flags):
        if col:
            in_specs.append(pl.BlockSpec(ps, lambda k, c_ref: (c_ref[0], k)))
        else:
            in_specs.append(pl.BlockSpec(ps, lambda k, c_ref: (2 * k + c_ref[0], 0)))
    land_specs = [pl.BlockSpec((1,) + ps, lambda k, c_ref: (k, 0, 0)) for ps in pshapes]
    return pl.pallas_call(
        body, name=name,
        grid_spec=pltpu.PrefetchScalarGridSpec(
            num_scalar_prefetch=1, grid=(N_CHIP,),
            in_specs=in_specs + land_specs, out_specs=land_specs),
        out_shape=[jax.ShapeDtypeStruct((N_CHIP,) + ps, BF16) for ps in pshapes],
        compiler_params=pltpu.CompilerParams(dimension_semantics=("arbitrary",), vmem_limit_bytes=VMEM_LIMIT_BYTES),
    )(c_idx, *grads, *landed)


def _chips_rider(parts):
    n = len(parts)

    def plan(ins, outs, sems, arriving):
        send_sems, recv_sems, local_sems = sems
        x, y, c = _mesh_pos()
        chips = [(1 - x, y), (x, 1 - y), (1 - x, 1 - y)]
        k_me = 2 * x + y
        local = [pltpu.make_async_copy(ins[a].at[k_me], outs[a].at[k_me], local_sems.at[a]) for a in range(n)]
        copies = []
        for a in range(n):
            for j, chip in enumerate(chips):
                k_peer = 2 * chip[0] + chip[1]
                copies.append(pltpu.make_async_remote_copy(
                    src_ref=ins[a].at[k_peer], dst_ref=outs[a].at[k_peer if arriving else k_me],
                    send_sem=send_sems.at[a * 3 + j], recv_sem=recv_sems.at[a * 3 + j],
                    device_id=(*chip, c), device_id_type=MESH))
        return local, copies

    def start(ins, outs, sems):
        local, sends = plan(ins, outs, sems, False)
        for cp in local + sends:
            cp.start()

    def finish(ins, outs, sems):
        local, arrivals = plan(ins, outs, sems, True)
        for cp in arrivals:
            cp.wait_recv()
        for cp in arrivals:
            cp.wait_send()
        for mine in local:
            mine.wait()

    return _Rider(parts, [jax.ShapeDtypeStruct(p.shape, p.dtype) for p in parts],
                  [pltpu.SemaphoreType.DMA((3 * n,)), pltpu.SemaphoreType.DMA((3 * n,)),
                   pltpu.SemaphoreType.DMA((n,))], start, finish)


def _sum_chips_and_share(landed):
    n = len(landed)

    def body(*refs):
        ins, outs, red = refs[:n], refs[n:2 * n], refs[2 * n:3 * n]
        send_sems, recv_sems, local_sems = refs[3 * n:]
        x, y, c = _mesh_pos()
        sibling = (x, y, 1 - c)
        copies, local = [], []
        for a in range(n):
            red[a][...] = ((ins[a][0].astype(F32) + ins[a][1].astype(F32)) + ins[a][2].astype(F32)
                           + ins[a][3].astype(F32))
            mine = pltpu.make_async_copy(red[a], outs[a].at[c], local_sems.at[a])
            mine.start()
            local.append(mine)
            cp = pltpu.make_async_remote_copy(
                src_ref=red[a], dst_ref=outs[a].at[c],
                send_sem=send_sems.at[a], recv_sem=recv_sems.at[a],
                device_id=sibling, device_id_type=MESH)
            cp.start()
            copies.append(cp)
        for a in range(n):
            pltpu.make_async_remote_copy(
                src_ref=red[a], dst_ref=outs[a].at[1 - c],
                send_sem=send_sems.at[a], recv_sem=recv_sems.at[a],
                device_id=sibling, device_id_type=MESH).wait_recv()
        for cp in copies:
            cp.wait_send()
        for mine in local:
            mine.wait()

    return pl.pallas_call(
        body, name="sum_chips_and_share",
        out_shape=[jax.ShapeDtypeStruct((2,) + l.shape[1:], F32) for l in landed],
        in_specs=[pl.BlockSpec(memory_space=pltpu.VMEM)] * n,
        out_specs=[pl.BlockSpec(memory_space=pl.ANY)] * n,
        scratch_shapes=[pltpu.VMEM(l.shape[1:], F32) for l in landed]
        + [pltpu.SemaphoreType.DMA((n,)), pltpu.SemaphoreType.DMA((n,)), pltpu.SemaphoreType.DMA((n,))],
        compiler_params=pltpu.CompilerParams(vmem_limit_bytes=VMEM_LIMIT_BYTES),
    )(*landed)


def _mod_shard(c_all, w_ada, b_shard):
    def body(c_ref, w_ref, b_ref, o_ref, sc_ref):
        cc = c_ref[...]
        sc = cc * _sigmoid(cc)
        sc_ref[...] = sc
        o_ref[...] = _dot(sc, w_ref[...]) + b_ref[...]

    nb, d = c_all.shape
    nn = w_ada.shape[1]
    return pl.pallas_call(
        body, name="mod_shard",
        out_shape=[jax.ShapeDtypeStruct((nb, nn), F32), jax.ShapeDtypeStruct((nb, d), F32)],
        compiler_params=pltpu.CompilerParams(vmem_limit_bytes=VMEM_LIMIT_BYTES),
    )(c_all, w_ada, b_shard)


V_SH_M, V_SC_M, V_G_M, V_SH_F, V_SC_F, V_G_F, V_PRE_MIX, V_POST_MIX, V_PRE_FFN, V_POST_FFN = range(10)


def _vrow(vec_ref, r):
    return vec_ref[r:r + 1, :]


def _mix_fwd(x, vecs, w_in_b, w_out_b, sgu_g, wm_b, bsb, wp_b, ps, pmats, ts):
    s_len, d = x.shape
    nt = s_len // ts
    nblk = ts // HEAD
    n_proj = w_in_b.shape[1]

    def body(x_ref, vec_ref, win_ref, wout_ref, sg_ref, wm_ref, bs_ref, wp_ref, ps_ref, pm_ref,
             h1_ref, proj_ref, cat_ref, mixed_ref, x2_ref, h2_ref, carry_ref):
        i = pl.program_id(0)

        @pl.when(i == 0)
        def _():
            carry_ref[...] = jnp.zeros_like(carry_ref)

        x = x_ref[...]
        h1 = (((x * _rms(x)) * _vrow(vec_ref, V_PRE_MIX)) * (1.0 + _vrow(vec_ref, V_SC_M))
              + _vrow(vec_ref, V_SH_M)).astype(BF16)
        h1_ref[...] = h1
        proj = _dot(h1, win_ref[...])
        proj_ref[...] = proj.astype(BF16)
        t_glob = lax.broadcasted_iota(jnp.int32, (ts, HEAD), 0) + i * ts
        for h in range(N_HEADS):
            u = _gelu(proj[:, h * HEAD:(h + 1) * HEAD])
            v = _gelu(proj[:, A_WIDTH + h * HEAD:A_WIDTH + (h + 1) * HEAD])
            vn = ((v * _rms(v)) * sg_ref[h:h + 1, :]).astype(BF16)
            for b in range(nblk):
                rs = slice(b * HEAD, (b + 1) * HEAD)
                z = _dot(wm_ref[h], vn[rs]) + bs_ref[h]
                cat_ref[rs, h * HEAD:(h + 1) * HEAD] = (u[rs] * z).astype(BF16)
        for g in range(len(POOL_WINDOWS)):
            gs = slice(g * HEAD, (g + 1) * HEAD)
            p = proj[:, 2 * A_WIDTH + g * HEAD:2 * A_WIDTH + (g + 1) * HEAD]
            pooled = _pool_fwd(p, carry_ref[:, gs], g, t_glob)
            yb = _dot(pooled.astype(BF16), wp_ref[g]) * ps_ref[0:1, gs]
            cat_ref[:, A_WIDTH + g * HEAD:A_WIDTH + (g + 1) * HEAD] = yb.astype(BF16)
        carry_ref[...] = proj[ts - POOL_HALO:ts, 2 * A_WIDTH:]
        mixed = _dot(cat_ref[...], wout_ref[...])
        mixed_ref[...] = mixed
        x2 = x + _vrow(vec_ref, V_G_M) * ((mixed * _rms(mixed)) * _vrow(vec_ref, V_POST_MIX))
        x2_ref[...] = x2
        h2 = (((x2 * _rms(x2)) * _vrow(vec_ref, V_PRE_FFN)) * (1.0 + _vrow(vec_ref, V_SC_F))
              + _vrow(vec_ref, V_SH_F)).astype(BF16)
        for b in range(ts // FFN_TS):
            rs = slice(b * FFN_TS, (b + 1) * FFN_TS)
            h2_ref[rs, :] = _permute_bf16(pm_ref[0], h2[rs])

    row = lambda i: (i, 0)
    return _call(
        body, name="mix_fwd", grid=(nt,),
        in_specs=[_tiled((ts, d), row), _whole(vecs.shape), _resident(w_in_b.shape), _resident(w_out_b.shape),
                  _whole(sgu_g.shape), _whole(wm_b.shape), _whole(bsb.shape), _whole(wp_b.shape), _whole(ps.shape),
                  _whole(pmats.shape)],
        out_specs=[_tiled((ts, d), row), _tiled((ts, n_proj), row), _tiled((ts, d), row),
                   _tiled((ts, d), row), _tiled((ts, d), row), _tiled((ts, d), row)],
        out_shape=[jax.ShapeDtypeStruct((s_len, d), BF16), jax.ShapeDtypeStruct((s_len, n_proj), BF16),
                   jax.ShapeDtypeStruct((s_len, d), BF16), jax.ShapeDtypeStruct((s_len, d), F32),
                   jax.ShapeDtypeStruct((s_len, d), F32), jax.ShapeDtypeStruct((s_len, d), BF16)],
        scratch_shapes=[pltpu.VMEM((POOL_HALO, A_WIDTH), F32)],
        args=(x, vecs, w_in_b, w_out_b, sgu_g, wm_b, bsb, wp_b, ps, pmats))


def _perm_mats(ts):
    p = jnp.arange(ts)
    pm = (((p % 8) * (ts // 8) + p // 8)[:, None] == p[None, :]).astype(BF16)
    return jnp.stack([pm, pm.T])


def _permute_bf16(pm, xb):
    return _dot(pm, xb).astype(BF16)


def _permute_f32(pm, x):
    hi = x.astype(BF16)
    lo = (x - hi.astype(F32)).astype(BF16)
    return _dot(pm, hi) + _dot(pm, lo)


def _conv_out(u, um2, um1, cv_ref, cols):
    return (cv_ref[3:4, cols] + um2 * cv_ref[0:1, cols] + um1 * cv_ref[1:2, cols] + u * cv_ref[2:3, cols])


F_LOSS, F_DGF, F_DPOSTFFN = 0, 1, 2
B_DSHF, B_DSCF, B_DPREFFN, B_DGM, B_DPOSTMIX = 0, 1, 2, 3, 4
M_DSHM, M_DSCM, M_DPREMIX = 0, 1, 2
C_DCB, C_DCW = 0, 1


def _ffn_fwd(h2p, x2, tgt, w_up_b, w_down_b, cvec, vecs, pmats, ts):
    s_len, d = x2.shape
    ff2 = w_up_b.shape[1]
    ff = ff2 // 2
    nt = s_len // ts
    chunks = _ff_chunks(ff)

    def body(h2_ref, x2_ref, t_ref, wu_ref, wd_ref, cv_ref, vec_ref, pm_ref,
             up_ref, y_ref, act_ref, dy_ref, df_ref, acc_ref, carry_ref):
        @pl.when(pl.program_id(0) == 0)
        def _():
            carry_ref[...] = jnp.zeros_like(carry_ref)
            acc_ref[...] = jnp.zeros_like(acc_ref)

        h2v = h2_ref[...]

        def up_dots(o, w):
            return [_dot(h2v, wu_ref[:, base + o:base + o + w]) for base in (0, ff)]

        f = None
        pending = None
        nxt = up_dots(*chunks[0])
        for ci, (o, w) in enumerate(chunks):
            us = nxt
            if ci + 1 < len(chunks):
                nxt = up_dots(*chunks[ci + 1])
            if pending is not None:
                part = _dot(pending[0], wd_ref[pending[1]:pending[1] + pending[2], :])
                f = part if f is None else f + part
            sub0 = lax.broadcasted_iota(jnp.int32, (8, w), 0) == 0
            ys = []
            for base, u in zip((0, ff), us):
                cols = slice(base + o, base + o + w)
                up_ref[:, cols] = u.astype(BF16)
                b1 = jnp.where(sub0, pltpu.roll(carry_ref[8:16, cols], 1, 0), pltpu.roll(u[ts - 8:ts], 1, 0))
                b2 = jnp.where(sub0, pltpu.roll(carry_ref[0:8, cols], 1, 0), pltpu.roll(u[ts - 16:ts - 8], 1, 0))
                um1 = jnp.concatenate([b1, u[:ts - 8]], axis=0)
                um2 = jnp.concatenate([b2, b1, u[:ts - 16]], axis=0)
                ys.append(_conv_out(u, um2, um1, cv_ref, cols))
                carry_ref[:, cols] = u[ts - 16:ts]
            gate, val = ys
            sg = _sigmoid(gate)
            gs = gate * sg
            act = (gs * val).astype(BF16)
            act_ref[:, o:o + w] = act
            y_ref[:, o:o + w] = (val * (sg + gs * (1.0 - sg))).astype(BF16)
            y_ref[:, ff + o:ff + o + w] = gs.astype(BF16)
            pending = (act, o, w)
        f = f + _dot(pending[0], wd_ref[pending[1]:pending[1] + pending[2], :])
        f = _permute_f32(pm_ref[1], f)
        r3 = _rms(f)
        fhat = f * r3
        post = _vrow(vec_ref, V_POST_FFN)
        g_f = _vrow(vec_ref, V_G_F)
        fn = fhat * post
        e = (x2_ref[...] + g_f * fn) - t_ref[...]
        dy = e * (1.0 / d)
        dy_ref[...] = dy
        dfn = dy * g_f
        acc_ref[F_LOSS:F_LOSS + 1, :] += _colsum(e * e)
        acc_ref[F_DGF:F_DGF + 1, :] += _colsum(dy * fn)
        acc_ref[F_DPOSTFFN:F_DPOSTFFN + 1, :] += _colsum(dfn * fhat)
        dfhat = dfn * post
        df = (r3 * (dfhat - fhat * _rowmean(dfhat * fhat))).astype(BF16)
        df_ref[...] = _permute_bf16(pm_ref[0], df)

    row = lambda i: (i, 0)
    return pl.pallas_call(
        body, name="ffn_fwd", grid=(nt,),
        in_specs=[_tiled((ts, d), row), _tiled((ts, d), row), _tiled((ts, d), row), _resident(w_up_b.shape),
                  _resident(w_down_b.shape), _whole(cvec.shape), _whole(vecs.shape), _whole(pmats.shape)],
        out_specs=[_tiled((ts, ff2), row), _tiled((ts, ff2), row), _tiled((ts, ff), row), _tiled((ts, d), row),
                   _tiled((ts, d), row), _whole((8, d))],
        out_shape=[jax.ShapeDtypeStruct((s_len, ff2), BF16), jax.ShapeDtypeStruct((s_len, ff2), BF16),
                   jax.ShapeDtypeStruct((s_len, ff), BF16), jax.ShapeDtypeStruct((s_len, d), F32),
                   jax.ShapeDtypeStruct((s_len, d), BF16), jax.ShapeDtypeStruct((8, d), F32)],
        scratch_shapes=[pltpu.VMEM((16, ff2), F32)],
        compiler_params=_seq_params(),
    )(h2p, x2, tgt, w_up_b, w_down_b, cvec, vecs, pmats)


def _ffn_bwd(dfp, upp, yp, x2, dy, mixed, w_up_b, w_down_b, cvec, vecs, pmats, ts, rider=None):
    s_len, d = x2.shape
    ff2 = w_up_b.shape[1]
    ff = ff2 // 2
    nt = s_len // ts
    chunks = _ff_chunks(ff, 512)

    def body(df_ref, up_ref, y_ref, x2_ref, dy_ref, mx_ref, wu_ref, wd_ref, cv_ref, vec_ref, pm_ref,
             dup_ref, dx2_ref, dmx_ref, accc_ref, acc_ref, carry_ref):
        @pl.when(pl.program_id(0) == 0)
        def _():
            carry_ref[...] = jnp.zeros_like(carry_ref)
            accc_ref[...] = jnp.zeros_like(accc_ref)
            acc_ref[...] = jnp.zeros_like(acc_ref)

        dfv = df_ref[...]

        def dh2_add(acc, dups, o, w):
            for base, dup in zip((0, ff), dups):
                part = _dot_nt(dup, wu_ref[:, base + o:base + o + w])
                acc = part if acc is None else acc + part
            return acc

        dh2 = None
        pending = None
        nxt = _dot_nt(dfv, wd_ref[chunks[0][0]:chunks[0][0] + chunks[0][1], :])
        for ci, (o, w) in enumerate(chunks):
            dact = nxt
            if ci + 1 < len(chunks):
                o2, w2 = chunks[ci + 1]
                nxt = _dot_nt(dfv, wd_ref[o2:o2 + w2, :])
            if pending is not None:
                dh2 = dh2_add(dh2, *pending)
            sub7 = lax.broadcasted_iota(jnp.int32, (8, w), 0) == 7
            dups = []
            dys = (dact * y_ref[:, o:o + w].astype(F32), dact * y_ref[:, ff + o:ff + o + w].astype(F32))
            for base, dyv in zip((0, ff), dys):
                cols = slice(base + o, base + o + w)
                u = up_ref[:, cols].astype(F32)
                e0 = jnp.where(sub7, pltpu.roll(carry_ref[0:8, cols], 7, 0), pltpu.roll(dyv[0:8], 7, 0))
                e1 = jnp.where(sub7, pltpu.roll(carry_ref[8:16, cols], 7, 0), pltpu.roll(dyv[8:16], 7, 0))
                dyp1 = jnp.concatenate([dyv[8:], e0], axis=0)
                dyp2 = jnp.concatenate([dyv[16:], e0, e1], axis=0)
                accc_ref[C_DCB:C_DCB + 1, cols] += _colsum(dyv)
                accc_ref[C_DCW + 0:C_DCW + 1, cols] += _colsum(dyp2 * u)
                accc_ref[C_DCW + 1:C_DCW + 2, cols] += _colsum(dyp1 * u)
                accc_ref[C_DCW + 2:C_DCW + 3, cols] += _colsum(dyv * u)
                dup = (dyv * cv_ref[2:3, cols] + dyp1 * cv_ref[1:2, cols] + dyp2 * cv_ref[0:1, cols]).astype(BF16)
                dup_ref[:, cols] = dup
                dups.append(dup)
                carry_ref[:, cols] = dyv[0:16]
            pending = (dups, o, w)
        dh2 = dh2_add(dh2, *pending)
        dh2 = _permute_f32(pm_ref[1], dh2)
        x2 = x2_ref[...]
        r2 = _rms(x2)
        xn = x2 * r2
        pre = _vrow(vec_ref, V_PRE_FFN)
        one_sc = 1.0 + _vrow(vec_ref, V_SC_F)
        acc_ref[B_DSHF:B_DSHF + 1, :] += _colsum(dh2)
        acc_ref[B_DSCF:B_DSCF + 1, :] += _colsum(dh2 * (xn * pre))
        acc_ref[B_DPREFFN:B_DPREFFN + 1, :] += _colsum(dh2 * xn * one_sc)
        dxn = dh2 * pre * one_sc
        dx2 = dy_ref[...] + r2 * (dxn - xn * _rowmean(dxn * xn))
        dx2_ref[...] = dx2
        mixed = mx_ref[...]
        rm = _rms(mixed)
        mhat = mixed * rm
        post = _vrow(vec_ref, V_POST_MIX)
        acc_ref[B_DGM:B_DGM + 1, :] += _colsum(dx2 * (mhat * post))
        dmn = dx2 * _vrow(vec_ref, V_G_M)
        acc_ref[B_DPOSTMIX:B_DPOSTMIX + 1, :] += _colsum(dmn * mhat)
        dmhat = dmn * post
        dmx_ref[...] = (rm * (dmhat - mhat * _rowmean(dmhat * mhat))).astype(BF16)

    rev = lambda i: (nt - 1 - i, 0)
    return _call(
        body, name="ffn_bwd", grid=(nt,), rider=rider,
        in_specs=[_tiled((ts, d), rev), _tiled((ts, ff2), rev), _tiled((ts, ff2), rev), _tiled((ts, d), rev),
                  _tiled((ts, d), rev), _tiled((ts, d), rev), _resident(w_up_b.shape), _resident(w_down_b.shape),
                  _whole(cvec.shape), _whole(vecs.shape), _whole(pmats.shape)],
        out_specs=[_tiled((ts, ff2), rev), _tiled((ts, d), rev), _tiled((ts, d), rev), _whole((8, ff2)),
                   _whole((8, d))],
        out_shape=[jax.ShapeDtypeStruct((s_len, ff2), BF16), jax.ShapeDtypeStruct((s_len, d), F32),
                   jax.ShapeDtypeStruct((s_len, d), BF16), jax.ShapeDtypeStruct((8, ff2), F32),
                   jax.ShapeDtypeStruct((8, d), F32)],
        scratch_shapes=[pltpu.VMEM((16, ff2), F32)],
        args=(dfp, upp, yp, x2, dy, mixed, w_up_b, w_down_b, cvec, vecs, pmats))


def _mix_bwd(dmixed, dx2, x, proj, vecs, w_in_b, w_out_b, sgu_g, wm_b, wmt_b, bsb, wp_b, ps, ts, rider=None):
    s_len, d = x.shape
    nt = s_len // ts
    nblk = ts // HEAD
    n_proj = proj.shape[1]
    per = ts // POOL_HALO
    ext_len = ts + POOL_HALO

    def body(dmx_ref, dx2_ref, x_ref, proj_ref, projh_ref, vec_ref, win_ref, wout_ref, sg_ref, wm_ref, wmt_ref,
             bs_ref, wp_ref, ps_ref,
             gx_ref, dproj_ref, acc_ref, dwm_out, dwp_out, db_ref, dg_ref, dps_ref,
             carry_ref, dwm_ref, dwp_ref, dbz_ref):
        i = pl.program_id(0)
        tile = nt - 1 - i

        @pl.when(i == 0)
        def _():
            carry_ref[...] = jnp.zeros_like(carry_ref)
            for r in (acc_ref, dwm_ref, dwp_ref, dbz_ref, dg_ref, dps_ref):
                r[...] = jnp.zeros_like(r)

        dcat = _dot_nt(dmx_ref[...], wout_ref[...])
        t_glob = lax.broadcasted_iota(jnp.int32, (ts, HEAD), 0) + tile * ts
        for h in range(N_HEADS):
            hs = slice(h * HEAD, (h + 1) * HEAD)
            vs = slice(A_WIDTH + h * HEAD, A_WIDTH + (h + 1) * HEAD)
            au = proj_ref[:, hs].astype(F32)
            av = proj_ref[:, vs].astype(F32)
            u = _gelu(au)
            v = _gelu(av)
            rv = _rms(v)
            vhat = v * rv
            gain = sg_ref[h:h + 1, :]
            vn = (vhat * gain).astype(BF16)
            dout = dcat[:, hs]
            du_parts, dvn_parts = [], []
            for b in range(nblk):
                rs = slice(b * HEAD, (b + 1) * HEAD)
                z = _dot(wm_ref[h], vn[rs]) + bs_ref[h]
                du_parts.append(dout[rs] * z)
                dz = dout[rs] * u[rs]
                dbz_ref[h] += dz
                dzb = dz.astype(BF16)
                dwm_ref[h] += _dot_nt(dzb, vn[rs])
                dvn_parts.append(_dot(wmt_ref[h], dzb))
            du = jnp.concatenate(du_parts, axis=0)
            dvn = jnp.concatenate(dvn_parts, axis=0)
            dg_ref[h:h + 1, :] += _colsum(dvn * vhat)
            dvhat = dvn * gain
            dv = rv * (dvhat - vhat * _rowmean(dvhat * vhat))
            dproj_ref[:, hs] = (du * _gelu_grad(au)).astype(BF16)
            dproj_ref[:, vs] = (dv * _gelu_grad(av)).astype(BF16)
        for g in range(len(POOL_WINDOWS)):
            gs = slice(g * HEAD, (g + 1) * HEAD)
            pcols = slice(2 * A_WIDTH + g * HEAD, 2 * A_WIDTH + (g + 1) * HEAD)
            p = proj_ref[:, pcols].astype(F32)
            halo = jnp.where(tile > 0, projh_ref[:, pcols].astype(F32), 0.0)
            pb = _pool_fwd(p, halo, g, t_glob).astype(BF16)
            dyb = dcat[:, A_WIDTH + g * HEAD:A_WIDTH + (g + 1) * HEAD]
            dps_ref[0:1, gs] += _colsum(dyb * _dot(pb, wp_ref[g]))
            dyl = (dyb * ps_ref[0:1, gs]).astype(BF16)
            dwp_ref[g] += _dot_tn(pb, dyl)
            dpooled = _dot_nt(dyl, wp_ref[g])
            cnt = jnp.minimum(t_glob + 1, POOL_WINDOWS[g]).astype(F32)
            q = dpooled / cnt
            s = jnp.concatenate([q, carry_ref[:, gs]], axis=0)
            for step in range(g + 1):
                s = s + pltpu.roll(s, ext_len - (1 << step), 0)
            dproj_ref[:, pcols] = (s[:ts] - dpooled).astype(BF16)
            carry_ref[:, gs] = q[0:POOL_HALO]
        dh1 = _dot_nt(dproj_ref[...], win_ref[...])
        x = x_ref[...]
        r1 = _rms(x)
        xn = x * r1
        pre = _vrow(vec_ref, V_PRE_MIX)
        one_sc = 1.0 + _vrow(vec_ref, V_SC_M)
        acc_ref[M_DSHM:M_DSHM + 1, :] += _colsum(dh1)
        acc_ref[M_DSCM:M_DSCM + 1, :] += _colsum(dh1 * (xn * pre))
        acc_ref[M_DPREMIX:M_DPREMIX + 1, :] += _colsum(dh1 * xn * one_sc)
        dxn = dh1 * pre * one_sc
        gx_ref[...] = dx2_ref[...] + r1 * (dxn - xn * _rowmean(dxn * xn))

        @pl.when(i == nt - 1)
        def _():
            dwm_out[...] = dwm_ref[...].astype(BF16)
            dwp_out[...] = dwp_ref[...].astype(BF16)
            db_ref[...] = jnp.zeros_like(db_ref)
            for h in range(N_HEADS):
                db_ref[h:h + 1, :] = jnp.sum(dbz_ref[h].T, axis=0, keepdims=True)

    rev = lambda i: (nt - 1 - i, 0)
    halo_map = lambda i: (jnp.maximum((nt - 1 - i) * per - 1, 0), 0)
    hshape = (N_HEADS, HEAD, HEAD)
    return _call(
        body, name="mix_bwd", grid=(nt,), rider=rider,
        in_specs=[_tiled((ts, d), rev), _tiled((ts, d), rev), _tiled((ts, d), rev), _tiled((ts, n_proj), rev),
                  _tiled((POOL_HALO, n_proj), halo_map), _whole(vecs.shape), _resident(w_in_b.shape),
                  _resident(w_out_b.shape), _whole(sgu_g.shape), _whole(wm_b.shape), _whole(wmt_b.shape),
                  _whole(bsb.shape), _whole(wp_b.shape), _whole(ps.shape)],
        out_specs=[_tiled((ts, d), rev), _tiled((ts, n_proj), rev), _whole((8, d)), _whole(hshape), _whole(hshape),
                   _whole((8, HEAD)), _whole((8, HEAD)), _whole((8, A_WIDTH))],
        out_shape=[jax.ShapeDtypeStruct((s_len, d), F32), jax.ShapeDtypeStruct((s_len, n_proj), BF16),
                   jax.ShapeDtypeStruct((8, d), F32), jax.ShapeDtypeStruct(hshape, BF16),
                   jax.ShapeDtypeStruct(hshape, BF16), jax.ShapeDtypeStruct((8, HEAD), F32),
                   jax.ShapeDtypeStruct((8, HEAD), F32), jax.ShapeDtypeStruct((8, A_WIDTH), F32)],
        scratch_shapes=[pltpu.VMEM((POOL_HALO, A_WIDTH), F32), pltpu.VMEM(hshape, F32), pltpu.VMEM(hshape, F32),
                        pltpu.VMEM(hshape, F32)],
        args=(dmixed, dx2, x, proj, proj, vecs, w_in_b, w_out_b, sgu_g, wm_b, wmt_b, bsb, wp_b, ps))


def _wgrad(a, b, tn, ts, name, rider=None):
    s_len, m = a.shape
    n = b.shape[1]
    ts = min(ts, s_len)

    def body(a_ref, b_ref, o_ref):
        @pl.when(pl.program_id(1) == 0)
        def _():
            o_ref[...] = jnp.zeros_like(o_ref)

        o_ref[...] += _dot_tn(a_ref[...], b_ref[...])

    (g,), r_out = _call(
        body, name=name, grid=(n // tn, s_len // ts), rider=rider,
        in_specs=[pl.BlockSpec((ts, m), lambda j, s: (s, 0)), pl.BlockSpec((ts, tn), lambda j, s: (s, j))],
        out_specs=[pl.BlockSpec((m, tn), lambda j, s: (0, j))],
        out_shape=[jax.ShapeDtypeStruct((m, n), F32)], scratch_shapes=[], args=(a, b))
    return g, r_out


def _wgrad_pair(a1, b1, a2, b2, ts, name, rider=None):
    s_len = a1.shape[0]
    ts = min(ts, s_len)
    shapes = [(a1.shape[1], b1.shape[1]), (a2.shape[1], b2.shape[1])]

    def body(a1_ref, b1_ref, a2_ref, b2_ref, o1_ref, o2_ref):
        @pl.when(pl.program_id(0) == 0)
        def _():
            o1_ref[...] = jnp.zeros_like(o1_ref)
            o2_ref[...] = jnp.zeros_like(o2_ref)

        o1_ref[...] += _dot_tn(a1_ref[...], b1_ref[...])
        o2_ref[...] += _dot_tn(a2_ref[...], b2_ref[...])

    row = lambda s: (s, 0)
    return _call(
        body, name=name, grid=(s_len // ts,), rider=rider,
        in_specs=[pl.BlockSpec((ts, t.shape[1]), row) for t in (a1, b1, a2, b2)],
        out_specs=[_whole(sh) for sh in shapes],
        out_shape=[jax.ShapeDtypeStruct(sh, F32) for sh in shapes], scratch_shapes=[], args=(a1, b1, a2, b2))


def _adamw_big(g, w, m, v, name):
    r, cdim = g.shape
    tr = r
    while tr * cdim * 4 > (3 << 19) and tr % 16 == 0:
        tr //= 2

    def body(g_ref, w_ref, m_ref, v_ref, d_ref, nm_ref, nv_ref):
        delta, m2, v2 = _adamw_math(w_ref[0], g_ref[...], m_ref[0], v_ref[0])
        d_ref[0] = delta
        nm_ref[0] = m2
        nv_ref[0] = v2

    s3 = pl.BlockSpec((1, tr, cdim), lambda i: (0, i, 0))
    return pl.pallas_call(
        body, name=name, grid=(r // tr,),
        in_specs=[pl.BlockSpec((tr, cdim), lambda i: (i, 0)), s3, s3, s3],
        out_specs=[s3, s3, s3],
        out_shape=[jax.ShapeDtypeStruct(w.shape, F32)] * 3,
        compiler_params=pltpu.CompilerParams(dimension_semantics=("parallel",), vmem_limit_bytes=VMEM_LIMIT_BYTES),
    )(g, w, m, v)


def _wada_update(sct, gm, w, m, v):
    _, r, cdim = w.shape
    tr = 256
    kp = sct.shape[1]

    def body(s_ref, g_ref, w_ref, m_ref, v_ref, gw_ref, d_ref, nm_ref, nv_ref):
        g = _dot(s_ref[...], g_ref[...])
        gw_ref[0] = g
        delta, m2, v2 = _adamw_math(w_ref[0], g, m_ref[0], v_ref[0])
        d_ref[0] = delta
        nm_ref[0] = m2
        nv_ref[0] = v2

    s3 = pl.BlockSpec((1, tr, cdim), lambda i: (0, i, 0))
    return pl.pallas_call(
        body, name="wada_update", grid=(r // tr,),
        in_specs=[pl.BlockSpec((tr, kp), lambda i: (i, 0)), _whole(gm.shape), s3, s3, s3],
        out_specs=[s3, s3, s3, s3],
        out_shape=[jax.ShapeDtypeStruct(w.shape, F32)] * 4,
        compiler_params=pltpu.CompilerParams(dimension_semantics=("parallel",), vmem_limit_bytes=VMEM_LIMIT_BYTES),
    )(sct, gm, w, m, v)


def _small_update(g1, g2, g2s, gwm, gwp, gbz, gsg, gps, params):
    names = ["b_ada", "pre_mix_g", "post_mix_g", "sgu_norm_g", "w_spatial", "b_spatial", "w_pool", "pool_scale",
             "pre_ffn_g", "post_ffn_g", "conv_w", "conv_b"]
    d = g1.shape[2]
    flat_in = [g1, g2, g2s, gwm, gwp, gbz, gsg, gps]
    n_g = len(flat_in)
    for nm in names:
        flat_in += list(params[nm])

    def body(*refs):
        g1_ref, g2_ref, g2s_ref, gwm_ref, gwp_ref, gbz_ref, gsg_ref, gps_ref = refs[:n_g]
        wmv = refs[n_g:n_g + 3 * len(names)]
        loss_ref = refs[n_g + 3 * len(names)]
        outs = refs[n_g + 3 * len(names) + 1:]

        def dsum(ref, idx):
            acc = ref[(0,) + idx].astype(F32)
            for dev in range(1, N_DEV):
                acc = acc + ref[(dev,) + idx].astype(F32)
            return acc

        def apply(pi, g, widx, oidx):
            w_ref, m_ref, v_ref = wmv[3 * pi:3 * pi + 3]
            g_ref, d_ref, nm_ref, nv_ref = outs[4 * pi:4 * pi + 4]
            delta, m2, v2 = _adamw_math(w_ref[widx], g, m_ref[widx], v_ref[widx])
            g_ref[oidx] = g
            d_ref[oidx] = delta
            nm_ref[oidx] = m2
            nv_ref[oidx] = v2

        def row1(base, r):
            return (slice(base + r, base + r + 1), slice(None))

        tot = dsum(g1_ref, row1(0, F_LOSS))
        loss_ref[...] = jnp.zeros(loss_ref.shape, F32) + jnp.sum(tot) * (0.5 / d)
        mod_rows = [row1(16, M_DSHM), row1(16, M_DSCM), row1(8, B_DGM), row1(8, B_DSHF), row1(8, B_DSCF),
                    row1(0, F_DGF)]
        for j, rr in enumerate(mod_rows):
            cs = (slice(None), slice(j * d, (j + 1) * d))
            apply(0, dsum(g1_ref, rr), cs, cs)
        full = (slice(None), slice(None))
        apply(1, dsum(g1_ref, row1(16, M_DPREMIX)), full, full)
        apply(2, dsum(g1_ref, row1(8, B_DPOSTMIX)), full, full)
        apply(3, dsum(gsg_ref, (slice(0, N_HEADS), slice(None))), (0,), (0,))
        pos_i = lax.broadcasted_iota(jnp.int32, (HEAD, HEAD), 0)
        pos_j = lax.broadcasted_iota(jnp.int32, (HEAD, HEAD), 1)
        causal = (pos_j // CHUNK) <= (pos_i // CHUNK)
        for h in range(N_HEADS):
            blk = (slice(h * HEAD, (h + 1) * HEAD), slice(None))
            apply(4, jnp.where(causal, dsum(gwm_ref, blk), 0.0), (0, h), (0, h))
            apply(5, dsum(gbz_ref, (slice(h, h + 1), slice(None))), (0, slice(h, h + 1)), (0, slice(h, h + 1)))
            apply(6, dsum(gwp_ref, blk), (0, h), (0, h))
        apply(7, dsum(gps_ref, (slice(0, 1), slice(None))), full, full)
        apply(8, dsum(g1_ref, row1(8, B_DPREFFN)), full, full)
        apply(9, dsum(g1_ref, row1(0, F_DPOSTFFN)), full, full)
        apply(10, dsum(g2s_ref, (slice(C_DCW, C_DCW + 3), slice(None))), (0,), (0,))
        apply(11, dsum(g2_ref, (slice(C_DCB, C_DCB + 1), slice(None))), full, full)

    out_shape = [jax.ShapeDtypeStruct((8, HEAD), F32)]
    for nm in names:
        out_shape += [jax.ShapeDtypeStruct(params[nm][0].shape, F32)] * 4
    res = pl.pallas_call(
        body, name="small_update", out_shape=out_shape,
        compiler_params=pltpu.CompilerParams(vmem_limit_bytes=VMEM_LIMIT_BYTES),
    )(*flat_in)
    out = {nm: tuple(res[1 + 4 * i:5 + 4 * i]) for i, nm in enumerate(names)}
    return res[0], out


def kernel(x, c, w_ada, b_ada, pre_mix_g, post_mix_g, w_in, sgu_norm_g, w_spatial, b_spatial, w_pool, pool_scale, w_out, pre_ffn_g, post_ffn_g, w_up, conv_w, conv_b, w_down, loss_target, m_w_ada, m_b_ada, m_pre_mix_g, m_post_mix_g, m_w_in, m_sgu_norm_g, m_w_spatial, m_b_spatial, m_w_pool, m_pool_scale, m_w_out, m_pre_ffn_g, m_post_ffn_g, m_w_up, m_conv_w, m_conv_b, m_w_down, v_w_ada, v_b_ada, v_pre_mix_g, v_post_mix_g, v_w_in, v_sgu_norm_g, v_w_spatial, v_b_spatial, v_w_pool, v_pool_scale, v_w_out, v_pre_ffn_g, v_post_ffn_g, v_w_up, v_conv_w, v_conv_b, v_w_down):
    xi, yi, ci = _mesh_pos()
    k_me = 2 * xi + yi
    dev = 2 * k_me + ci
    s_len, d = x.shape[1], x.shape[2]
    x2d = x[0]
    tgt = loss_target[0]
    ff2 = conv_b.shape[1]
    n_ada = w_ada.shape[2]
    n_cw = conv_w.shape[2]

    w_in_s, w_out_s, w_up_s, w_down_s = _cast_bf16([w_in[0], w_out[0], w_up[0], w_down[0]])
    mix_flags = (True, False)
    fly_mix = _gather_begin([w_in_s, w_out_s], mix_flags, w_in_s, "gather_begin_mix")
    fly_ffn = _gather_begin([w_up_s, w_down_s], mix_flags, fly_mix[4], "gather_begin_ffn")

    cw_blk = jnp.concatenate([conv_w[0], jnp.zeros((5, n_cw), F32)], axis=0)
    c_blk = c.reshape(8, d // 8) + fly_ffn[4][0:1, 0:1]
    c_all, cw_all = _run_rider(_allgather_rider([c_blk, cw_blk]), "gather_c_convw")
    c_all = c_all.reshape(N_DEV, 8, d // 8).reshape(N_DEV, d)
    cw_full = jnp.concatenate([cw_all[16 * k:16 * k + 8] for k in range(N_CHIP)], axis=1)
    cvec = jnp.concatenate([cw_full[0:3], conv_b, jnp.zeros((4, ff2), F32)], axis=0)
    b_shard = lax.dynamic_slice_in_dim(b_ada, k_me * n_ada, n_ada, axis=1)
    mod_k, sc_all = _mod_shard(c_all, w_ada[0], b_shard)
    (mod_g,) = _run_rider(_allgather_rider([mod_k]), "gather_mod")
    mod_all = jnp.concatenate([mod_g[16 * k:16 * k + 8] for k in range(N_CHIP)], axis=1)
    mod_me = lax.dynamic_slice_in_dim(mod_all, dev, 1, axis=0).reshape(6, d)
    vecs = jnp.concatenate([mod_me, pre_mix_g, post_mix_g, pre_ffn_g, post_ffn_g, jnp.zeros((6, d), F32)], axis=0)

    w_in_b, w_out_b = _gather_finish(*_gather_end(fly_mix, mix_flags, vecs, "gather_end_mix"), mix_flags,
                                     "gather_finish_mix")

    pos = jnp.arange(HEAD)
    causal = (pos[None, :] // CHUNK) <= (pos[:, None] // CHUNK)
    wm = jnp.where(causal[None], w_spatial[0], 0.0)
    wm_b = wm.astype(BF16)
    wmt_b = jnp.swapaxes(wm, 1, 2).astype(BF16)
    bsb = jnp.broadcast_to(b_spatial[0][:, :, None], (N_HEADS, HEAD, HEAD))
    wp_b = w_pool[0].astype(BF16)
    sgu_g = jnp.concatenate([sgu_norm_g[0], jnp.zeros((4, HEAD), F32)], axis=0)
    ps = jnp.concatenate([pool_scale, jnp.zeros((7, A_WIDTH), F32)], axis=0)

    pmats = _perm_mats(FFN_TS)
    h1, proj, cat, mixed, x2, h2p = _mix_fwd(x2d, vecs, w_in_b, w_out_b, sgu_g, wm_b, bsb, wp_b, ps, pmats, ts=512)[0]
    w_up_b, w_down_b = _gather_finish(*_gather_end(fly_ffn, mix_flags, h2p, "gather_end_ffn"), mix_flags,
                                      "gather_finish_ffn")
    up, yv, act, dy, dfp, acc_f = _ffn_fwd(h2p, x2, tgt, w_up_b, w_down_b, cvec, vecs, pmats, ts=FFN_TS)

    c_idx = ci.reshape(1).astype(jnp.int32)
    g_w_down, _ = _wgrad(act, dfp, d, 1024, "wgrad_down")
    (dup, dx2, dmixed, acc_c, acc_b), (land_down,) = _ffn_bwd(
        dfp, up, yv, x2, dy, mixed, w_up_b, w_down_b, cvec, vecs, pmats, ts=FFN_TS,
        rider=_sibling_rider([g_w_down], (False,)))
    (part_down,) = _sum_with_sibling([g_w_down], [land_down], (False,), c_idx, "pair_sum_down")
    g_w_up, (chips_down,) = _wgrad(h2p, dup, ff2 // 2, 2048, "wgrad_up", rider=_chips_rider([part_down]))
    (gx, dproj, acc_m, dwm, dwp, dbz, dsg, dps), (land_up,) = _mix_bwd(
        dmixed, dx2, x2d, proj, vecs, w_in_b, w_out_b, sgu_g, wm_b, wmt_b, bsb, wp_b, ps, ts=256,
        rider=_sibling_rider([g_w_up], (True,)))
    (part_up,) = _sum_with_sibling([g_w_up], [land_up], (True,), c_idx, "pair_sum_up")
    (g_w_out, g_w_in), (chips_up,) = _wgrad_pair(cat, dmixed, h1, dproj, 1024, "wgrad_mix",
                                                 rider=_chips_rider([part_up]))
    land_mix = _run_rider(_sibling_rider([g_w_in, g_w_out], (True, False)), "reduce_to_sibling")
    parts_mix = _sum_with_sibling([g_w_in, g_w_out], land_mix, (True, False), c_idx, "pair_sum_mix")
    chips_in, chips_out = _run_rider(_chips_rider(parts_mix), "reduce_over_chips")
    reduced = _sum_chips_and_share([chips_in, chips_out, chips_up, chips_down])
    big = {}
    for nm, red, (w, m, v) in zip(("w_in", "w_out", "w_up", "w_down"), reduced,
                                  ((w_in, m_w_in, v_w_in), (w_out, m_w_out, v_w_out),
                                   (w_up, m_w_up, v_w_up), (w_down, m_w_down, v_w_down))):
        g = red.reshape(w.shape[1], w.shape[2])
        big[nm] = (g.reshape(w.shape),) + tuple(_adamw_big(g, w, m, v, "adamw_" + nm))

    g1 = jnp.concatenate([acc_f, acc_b, acc_m], axis=0)
    hflat = (N_HEADS * HEAD, HEAD)
    gathered = _run_rider(_allgather_rider(
        [g1, acc_c, dwm.reshape(hflat), dwp.reshape(hflat), dbz, dsg, dps]), "gather_small_grads")
    g1a, g2a, gwm, gwp, gbz, gsg, gps = [t.reshape((N_DEV, t.shape[0] // N_DEV, t.shape[1])) for t in gathered]
    g2s = lax.dynamic_slice_in_dim(g2a, k_me * n_cw, n_cw, axis=2)
    params = {
        "b_ada": (b_ada, m_b_ada, v_b_ada), "pre_mix_g": (pre_mix_g, m_pre_mix_g, v_pre_mix_g),
        "post_mix_g": (post_mix_g, m_post_mix_g, v_post_mix_g),
        "sgu_norm_g": (sgu_norm_g, m_sgu_norm_g, v_sgu_norm_g), "w_spatial": (w_spatial, m_w_spatial, v_w_spatial),
        "b_spatial": (b_spatial, m_b_spatial, v_b_spatial), "w_pool": (w_pool, m_w_pool, v_w_pool),
        "pool_scale": (pool_scale, m_pool_scale, v_pool_scale), "pre_ffn_g": (pre_ffn_g, m_pre_ffn_g, v_pre_ffn_g),
        "post_ffn_g": (post_ffn_g, m_post_ffn_g, v_post_ffn_g), "conv_w": (conv_w, m_conv_w, v_conv_w),
        "conv_b": (conv_b, m_conv_b, v_conv_b),
    }
    loss_slab, small = _small_update(g1a, g2a, g2s, gwm, gwp, gbz, gsg, gps, params)

    gmod_all = jnp.concatenate(
        [g1a[:, 16 + M_DSHM], g1a[:, 16 + M_DSCM], g1a[:, 8 + B_DGM], g1a[:, 8 + B_DSHF], g1a[:, 8 + B_DSCF],
         g1a[:, F_DGF]], axis=1)
    gm = lax.dynamic_slice_in_dim(gmod_all, k_me * n_ada, n_ada, axis=1)
    gm = jnp.concatenate([gm, jnp.zeros((HEAD - N_DEV, n_ada), F32)], axis=0)
    sct = jnp.concatenate([sc_all.T, jnp.zeros((d, HEAD - N_DEV), F32)], axis=1)
    ada = tuple(_wada_update(sct, gm, w_ada, m_w_ada, v_w_ada))

    everything = dict(small)
    everything.update(big)
    everything["w_ada"] = ada
    order = ["w_ada", "b_ada", "pre_mix_g", "post_mix_g", "w_in", "sgu_norm_g", "w_spatial", "b_spatial", "w_pool",
             "pool_scale", "w_out", "pre_ffn_g", "post_ffn_g", "w_up", "conv_w", "conv_b", "w_down"]
    outs = [loss_slab[0, 0], gx.reshape(x.shape)]
    for j in range(4):
        outs += [everything[nm][j] for nm in order]
    return tuple(outs)
```

```python
import functools

import jax
import jax.numpy as jnp
from jax import lax
from jax.experimental import pallas as pl
from jax.experimental.pallas import tpu as pltpu

F32 = jnp.float32
BF16 = jnp.bfloat16
MESH = pl.DeviceIdType.MESH

EPS = 1e-6
HEAD = 128
N_HEADS = 4
A_WIDTH = N_HEADS * HEAD
CHUNK = 64
POOL_WINDOWS = (2, 4, 8, 16)
POOL_HALO = 16
FFN_TS = 256

ADAM_LR = 0.001
ADAM_B1 = 0.9
ADAM_B2 = 0.999
ADAM_EPS = 1e-08
ADAM_WD = 0.01
ADAM_STEP = 10

VMEM_LIMIT_BYTES = 58 * 1024 * 1024
N_DEV = 8
N_CHIP = 4


def _dot(a, b):
    return jnp.dot(a, b, preferred_element_type=F32)


def _dot_nt(a, b):
    return lax.dot_general(a, b, (((1,), (1,)), ((), ())), preferred_element_type=F32)


def _dot_tn(a, b):
    return lax.dot_general(a, b, (((0,), (0,)), ((), ())), preferred_element_type=F32)


def _gelu(x):
    return x * (0.5 * (1.0 + jnp.tanh(0.7978845608028654 * (x + 0.044715 * (x * x * x)))))


def _gelu_grad(x):
    t = jnp.tanh(0.7978845608028654 * (x + 0.044715 * (x * x * x)))
    return 0.5 * (1.0 + t) + (0.5 * x) * (1.0 - t * t) * (0.7978845608028654 * (1.0 + 0.134145 * (x * x)))


def _sigmoid(x):
    return 1.0 / (1.0 + jnp.exp(-x))


def _rms(x):
    return lax.rsqrt(jnp.mean(x * x, axis=-1, keepdims=True) + EPS)


def _colsum(x):
    return jnp.sum(x, axis=0, keepdims=True)


def _rowmean(x):
    return jnp.mean(x, axis=-1, keepdims=True)


def _tiled(shape, index_map):
    return pl.BlockSpec(shape, index_map)


def _resident(shape):
    nd = len(shape)
    return pl.BlockSpec(shape, lambda *_: (0,) * nd, pipeline_mode=pl.Buffered(1))


def _whole(shape):
    nd = len(shape)
    return pl.BlockSpec(shape, lambda *_: (0,) * nd)


def _seq_params():
    return pltpu.CompilerParams(dimension_semantics=("arbitrary",), vmem_limit_bytes=VMEM_LIMIT_BYTES)


def _ff_chunks(f, width=768):
    out, o = [], 0
    while o < f:
        w = min(width, f - o)
        out.append((o, w))
        o += w
    return out


def _pool_fwd(p, halo, g, t_glob):
    ext = jnp.concatenate([halo, p], axis=0)
    s = ext
    for step in range(g + 1):
        s = s + pltpu.roll(s, 1 << step, 0)
    cnt = jnp.minimum(t_glob + 1, POOL_WINDOWS[g]).astype(F32)
    return s[POOL_HALO:] / cnt - p


def _adamw_math(w, g, m, v):
    m = ADAM_B1 * m + (1.0 - ADAM_B1) * g
    v = ADAM_B2 * v + (1.0 - ADAM_B2) * (g * g)
    m_hat = m / (1.0 - ADAM_B1 ** ADAM_STEP)
    v_hat = v / (1.0 - ADAM_B2 ** ADAM_STEP)
    delta = -ADAM_LR * (m_hat / (jnp.sqrt(v_hat) + ADAM_EPS) + ADAM_WD * w)
    return delta, m, v


def _mesh_pos():
    return lax.axis_index("x"), lax.axis_index("y"), lax.axis_index("c")


class _Rider:
    def __init__(self, inputs, out_shape, sems, start, finish):
        self.inputs, self.out_shape, self.sems = list(inputs), list(out_shape), list(sems)
        self.start, self.finish = start, finish


def _call(body, *, name, grid, in_specs, out_specs, out_shape, scratch_shapes, args, rider=None):
    params = pltpu.CompilerParams(dimension_semantics=("arbitrary",) * len(grid), vmem_limit_bytes=VMEM_LIMIT_BYTES)
    if rider is None:
        res = pl.pallas_call(body, name=name, grid=grid, in_specs=in_specs, out_specs=out_specs, out_shape=out_shape,
                             scratch_shapes=scratch_shapes, compiler_params=params)(*args)
        return tuple(res), ()
    cuts = [len(in_specs), len(rider.inputs), len(out_specs), len(rider.out_shape), len(scratch_shapes),
            len(rider.sems)]

    def hosted(*refs):
        groups, a = [], 0
        for cnt in cuts:
            groups.append(refs[a:a + cnt])
            a += cnt
        ins, r_in, outs, r_out, scr, r_sem = groups
        first = functools.reduce(jnp.logical_and, [pl.program_id(k) == 0 for k in range(len(grid))])
        last = functools.reduce(jnp.logical_and, [pl.program_id(k) == grid[k] - 1 for k in range(len(grid))])

        @pl.when(first)
        def _():
            rider.start(r_in, r_out, r_sem)

        body(*ins, *outs, *scr)

        @pl.when(last)
        def _():
            rider.finish(r_in, r_out, r_sem)

    anyspec = pl.BlockSpec(memory_space=pl.ANY)
    res = pl.pallas_call(
        hosted, name=name, grid=grid,
        in_specs=list(in_specs) + [anyspec] * cuts[1], out_specs=list(out_specs) + [anyspec] * cuts[3],
        out_shape=list(out_shape) + rider.out_shape, scratch_shapes=list(scratch_shapes) + rider.sems,
        compiler_params=params)(*args, *rider.inputs)
    return tuple(res[:cuts[2]]), tuple(res[cuts[2]:])


def _run_rider(rider, name):
    n_in, n_out = len(rider.inputs), len(rider.out_shape)

    def body(*refs):
        r_in, r_out, r_sem = refs[:n_in], refs[n_in:n_in + n_out], refs[n_in + n_out:]
        rider.start(r_in, r_out, r_sem)
        rider.finish(r_in, r_out, r_sem)

    anyspec = pl.BlockSpec(memory_space=pl.ANY)
    return pl.pallas_call(body, name=name, out_shape=rider.out_shape, in_specs=[anyspec] * n_in,
                          out_specs=[anyspec] * n_out, scratch_shapes=rider.sems)(*rider.inputs)


def _allgather_rider(arrs):
    n = len(arrs)

    def plan(ins, outs, sems):
        send_sems, recv_sems, local_sems = sems
        x, y, c = _mesh_pos()
        me, sibling = (x, y, c), (x, y, 1 - c)
        chips = [(1 - x, y), (x, 1 - y), (1 - x, 1 - y)]

        def rows(a, px, py, pc):
            r = ins[a].shape[0]
            return outs[a].at[pl.ds(pl.multiple_of((4 * px + 2 * py + pc) * r, 8), r), :]

        def copy(a, k, block, to, src=None):
            return pltpu.make_async_remote_copy(
                src_ref=rows(a, *block) if src is None else src, dst_ref=rows(a, *block),
                send_sem=send_sems.at[a * 7 + k], recv_sem=recv_sems.at[a * 7 + k],
                device_id=to, device_id_type=MESH)

        local = [pltpu.make_async_copy(ins[a], rows(a, *me), local_sems.at[a]) for a in range(n)]
        first = []
        for a in range(n):
            first.append(copy(a, 0, me, sibling, src=ins[a]))
            first += [copy(a, 1 + j, me, (*chip, c), src=ins[a]) for j, chip in enumerate(chips)]
        return c, me, sibling, chips, copy, local, first

    def start(ins, outs, sems):
        *_, local, first = plan(ins, outs, sems)
        for cp in local + first:
            cp.start()

    def finish(ins, outs, sems):
        c, me, sibling, chips, copy, local, first = plan(ins, outs, sems)
        passed = []
        for a in range(n):
            for j, chip in enumerate(chips):
                copy(a, 1 + j, (*chip, c), me).wait_recv()
                fwd = copy(a, 4 + j, (*chip, c), sibling)
                fwd.start()
                passed.append(fwd)
        for a in range(n):
            copy(a, 0, sibling, me).wait_recv()
            for j, chip in enumerate(chips):
                copy(a, 4 + j, (*chip, 1 - c), me).wait_recv()
        for cp in first + passed:
            cp.wait_send()
        for mine in local:
            mine.wait()

    return _Rider(arrs, [jax.ShapeDtypeStruct((N_DEV * a.shape[0], a.shape[1]), a.dtype) for a in arrs],
                  [pltpu.SemaphoreType.DMA((7 * n,)), pltpu.SemaphoreType.DMA((7 * n,)),
                   pltpu.SemaphoreType.DMA((n,))], start, finish)


def _piece(ref, col_sharded, k, h):
    m, n = ref.shape
    if col_sharded:
        mh, nc = m // 2, n // N_CHIP
        return ref.at[pl.ds(pl.multiple_of(h * mh, 16), mh), pl.ds(pl.multiple_of(k * nc, 128), nc)]
    rp = m // (2 * N_CHIP)
    return ref.at[pl.ds(pl.multiple_of((2 * k + h) * rp, 16), rp), :]


def _piece_shape(shape, col_sharded):
    m, n = shape
    return (m // 2, n // N_CHIP) if col_sharded else (m // (2 * N_CHIP), n)


def _cast_bf16(arrs, col_flags, k_idx):
    n = len(arrs)

    def body(k_ref, *refs):
        for a in range(n):
            val = refs[a][...].astype(BF16)
            refs[n + a][...] = val
            refs[2 * n + a][...] = val

    whole = [pl.BlockSpec(a.shape, lambda i, k_ref: (0, 0)) for a in arrs]
    window = [pl.BlockSpec(a.shape, (lambda i, k_ref: (0, k_ref[0])) if col else (lambda i, k_ref: (k_ref[0], 0)))
              for a, col in zip(arrs, col_flags)]
    res = pl.pallas_call(
        body, name="cast_weights",
        grid_spec=pltpu.PrefetchScalarGridSpec(num_scalar_prefetch=1, grid=(1,), in_specs=whole,
                                               out_specs=whole + window),
        out_shape=[jax.ShapeDtypeStruct(a.shape, BF16) for a in arrs]
        + [jax.ShapeDtypeStruct(fs, BF16) for fs in _full_shapes(arrs, col_flags)],
        compiler_params=pltpu.CompilerParams(vmem_limit_bytes=VMEM_LIMIT_BYTES))(k_idx, *arrs)
    return list(res[:n]), list(res[n:])


def _full_shapes(shards, col_flags):
    return [(s.shape[0], s.shape[1] * N_CHIP) if col else (s.shape[0] * N_CHIP, s.shape[1])
            for s, col in zip(shards, col_flags)]


def _ici_copies(shard_refs, full_refs, send_sems, recv_sems, col_flags):
    x, y, c = _mesh_pos()
    k_me = 2 * x + y
    copies = []
    for a, (s_ref, f_ref) in enumerate(zip(shard_refs, full_refs)):
        rows = s_ref.shape[0] // 2
        src = s_ref.at[pl.ds(pl.multiple_of(c * rows, 16), rows), :]
        for j, chip in enumerate([(1 - x, y), (x, 1 - y), (1 - x, 1 - y)]):
            copies.append(pltpu.make_async_remote_copy(
                src_ref=src, dst_ref=_piece(f_ref, col_flags[a], k_me, c),
                send_sem=send_sems.at[a * 3 + j], recv_sem=recv_sems.at[a * 3 + j],
                device_id=(*chip, c), device_id_type=MESH))
    return copies


def _gather_begin(shards, fulls, col_flags, after, name):
    n = len(shards)
    hbm = pl.BlockSpec(memory_space=pltpu.HBM)
    sem = pl.BlockSpec(memory_space=pltpu.SEMAPHORE)

    def body(*refs):
        shard_refs, full_refs = refs[:n], refs[n:2 * n]
        send_sems, recv_sems = refs[2 * n + 1], refs[2 * n + 2]
        token = refs[-1]
        for cp in _ici_copies(shard_refs, full_refs, send_sems, recv_sems, col_flags):
            cp.start()
        token[...] = jnp.zeros_like(token)

    args = [pltpu.with_memory_space_constraint(t, pltpu.HBM) for t in list(shards) + list(fulls)]
    res = pl.pallas_call(
        body, name=name,
        out_shape=[pltpu.SemaphoreType.DMA((3 * n,)), pltpu.SemaphoreType.DMA((3 * n,))]
        + [pltpu.HBM(t.shape, t.dtype) for t in args] + [jax.ShapeDtypeStruct((8, HEAD), F32)],
        in_specs=[hbm] * (2 * n) + [pl.BlockSpec(memory_space=pl.ANY)],
        out_specs=[sem, sem] + [hbm] * (2 * n) + [pl.BlockSpec(memory_space=pltpu.VMEM)],
        input_output_aliases={i: 2 + i for i in range(2 * n)},
        compiler_params=pltpu.CompilerParams(has_side_effects=pltpu.SideEffectType.DATAFLOW_SIDE_EFFECTING),
    )(*args, after)
    return res[0], res[1], list(res[2:2 + n]), list(res[2 + n:2 + 2 * n]), res[-1]


def _gather_end(handle, col_flags, after, name):
    send_sems, recv_sems, shards, fulls, _ = handle
    n = len(shards)
    hbm = pl.BlockSpec(memory_space=pltpu.HBM)
    sem = pl.BlockSpec(memory_space=pltpu.SEMAPHORE)

    def body(*refs):
        shard_refs, full_refs = refs[:n], refs[n:2 * n]
        for cp in _ici_copies(shard_refs, full_refs, refs[2 * n], refs[2 * n + 1], col_flags):
            cp.wait_send()
            cp.wait_recv()

    res = pl.pallas_call(
        body, name=name,
        out_shape=[pltpu.HBM(t.shape, t.dtype) for t in shards + fulls],
        in_specs=[hbm] * (2 * n) + [sem, sem, pl.BlockSpec(memory_space=pl.ANY)],
        out_specs=[hbm] * (2 * n),
        input_output_aliases={i: i for i in range(2 * n)},
        compiler_params=pltpu.CompilerParams(has_side_effects=pltpu.SideEffectType.DATAFLOW_SIDE_EFFECTING),
    )(*shards, *fulls, send_sems, recv_sems, after)
    return list(res[:n]), list(res[n:])


def _gather_finish(fulls, col_flags, name):
    n = len(fulls)

    def body(*refs):
        full_refs = refs[n:2 * n]
        send_sems, recv_sems = refs[2 * n:]
        x, y, c = _mesh_pos()
        passed, arriving = [], []
        for a in range(n):
            for j, chip in enumerate([(1 - x, y), (x, 1 - y), (1 - x, 1 - y)]):
                k_from = 2 * chip[0] + chip[1]
                for h, group in ((c, passed), (1 - c, arriving)):
                    win = _piece(full_refs[a], col_flags[a], k_from, h)
                    group.append(pltpu.make_async_remote_copy(
                        src_ref=win, dst_ref=win, send_sem=send_sems.at[a * 3 + j],
                        recv_sem=recv_sems.at[a * 3 + j], device_id=(x, y, 1 - c), device_id_type=MESH))
        for cp in passed:
            cp.start()
        for cp in arriving:
            cp.wait_recv()
        for cp in passed:
            cp.wait_send()

    anyspec = pl.BlockSpec(memory_space=pl.ANY)
    return pl.pallas_call(
        body, name=name,
        out_shape=[jax.ShapeDtypeStruct(f.shape, f.dtype) for f in fulls],
        in_specs=[anyspec] * n, out_specs=[anyspec] * n,
        input_output_aliases={a: a for a in range(n)},
        scratch_shapes=[pltpu.SemaphoreType.DMA((3 * n,)), pltpu.SemaphoreType.DMA((3 * n,))],
    )(*fulls)


def _sibling_rider(grads, col_flags):
    n = len(grads)
    pshapes = [_piece_shape(g.shape, col) for g, col in zip(grads, col_flags)]

    def copies(ins, outs, sems):
        send_sems, recv_sems = sems
        x, y, c = _mesh_pos()
        return [pltpu.make_async_remote_copy(
            src_ref=_piece(ins[a], col_flags[a], k, 1 - c), dst_ref=outs[a].at[k],
            send_sem=send_sems.at[a * N_CHIP + k], recv_sem=recv_sems.at[a * N_CHIP + k],
            device_id=(x, y, 1 - c), device_id_type=MESH) for a in range(n) for k in range(N_CHIP)]

    def start(ins, outs, sems):
        for cp in copies(ins, outs, sems):
            cp.start()

    def finish(ins, outs, sems):
        cps = copies(ins, outs, sems)
        for cp in cps:
            cp.wait_recv()
        for cp in cps:
            cp.wait_send()

    return _Rider(grads, [jax.ShapeDtypeStruct((N_CHIP,) + ps, g.dtype) for ps, g in zip(pshapes, grads)],
                  [pltpu.SemaphoreType.DMA((N_CHIP * n,)), pltpu.SemaphoreType.DMA((N_CHIP * n,))], start, finish)


def _sum_with_sibling(grads, landed, col_flags, c_idx, name):
    n = len(grads)
    pshapes = [_piece_shape(g.shape, col) for g, col in zip(grads, col_flags)]

    def body(c_ref, *refs):
        ins, lands, outs = refs[:n], refs[n:2 * n], refs[2 * n:]
        for a in range(n):
            outs[a][0] = (ins[a][...] + lands[a][0]).astype(BF16)

    in_specs = []
    for ps, col in zip(pshapes, col_flags):
        if col:
            in_specs.append(pl.BlockSpec(ps, lambda k, c_ref: (c_ref[0], k)))
        else:
            in_specs.append(pl.BlockSpec(ps, lambda k, c_ref: (2 * k + c_ref[0], 0)))
    land_specs = [pl.BlockSpec((1,) + ps, lambda k, c_ref: (k, 0, 0)) for ps in pshapes]
    return pl.pallas_call(
        body, name=name,
        grid_spec=pltpu.PrefetchScalarGridSpec(
            num_scalar_prefetch=1, grid=(N_CHIP,),
            in_specs=in_specs + land_specs, out_specs=land_specs),
        out_shape=[jax.ShapeDtypeStruct((N_CHIP,) + ps, BF16) for ps in pshapes],
        compiler_params=pltpu.CompilerParams(dimension_semantics=("arbitrary",), vmem_limit_bytes=VMEM_LIMIT_BYTES),
    )(c_idx, *grads, *landed)


def _chips_rider(parts):
    n = len(parts)

    def plan(ins, outs, sems, arriving):
        send_sems, recv_sems = sems
        x, y, c = _mesh_pos()
        k_me = 2 * x + y
        copies = []
        for a in range(n):
            for j, chip in enumerate([(1 - x, y), (x, 1 - y), (1 - x, 1 - y)]):
                k_peer = 2 * chip[0] + chip[1]
                copies.append(pltpu.make_async_remote_copy(
                    src_ref=ins[a].at[k_peer], dst_ref=outs[a].at[k_peer if arriving else k_me],
                    send_sem=send_sems.at[a * 3 + j], recv_sem=recv_sems.at[a * 3 + j],
                    device_id=(*chip, c), device_id_type=MESH))
        return copies

    def start(ins, outs, sems):
        for cp in plan(ins, outs, sems, False):
            cp.start()

    def finish(ins, outs, sems):
        arrivals = plan(ins, outs, sems, True)
        for cp in arrivals:
            cp.wait_recv()
        for cp in arrivals:
            cp.wait_send()

    return _Rider(parts, [jax.ShapeDtypeStruct(p.shape, p.dtype) for p in parts],
                  [pltpu.SemaphoreType.DMA((3 * n,)), pltpu.SemaphoreType.DMA((3 * n,))], start, finish)


def _sum_chips_and_share(landed, parts):
    n = len(landed)

    def body(*refs):
        ins, own, outs, red = refs[:n], refs[n:2 * n], refs[2 * n:3 * n], refs[3 * n:4 * n]
        send_sems, recv_sems, local_sems = refs[4 * n:]
        x, y, c = _mesh_pos()
        sibling = (x, y, 1 - c)
        k_me = 2 * x + y
        copies, local = [], []
        for a in range(n):
            for k in range(N_CHIP):
                @pl.when(k_me == k)
                def _():
                    term = own[a][k].astype(F32)
                    red[a][...] = term if k == 0 else red[a][...] + term

                @pl.when(k_me != k)
                def _():
                    term = ins[a][k].astype(F32)
                    red[a][...] = term if k == 0 else red[a][...] + term

            mine = pltpu.make_async_copy(red[a], outs[a].at[c], local_sems.at[a])
            mine.start()
            local.append(mine)
            cp = pltpu.make_async_remote_copy(
                src_ref=red[a], dst_ref=outs[a].at[c],
                send_sem=send_sems.at[a], recv_sem=recv_sems.at[a],
                device_id=sibling, device_id_type=MESH)
            cp.start()
            copies.append(cp)
        for a in range(n):
            pltpu.make_async_remote_copy(
                src_ref=red[a], dst_ref=outs[a].at[1 - c],
                send_sem=send_sems.at[a], recv_sem=recv_sems.at[a],
                device_id=sibling, device_id_type=MESH).wait_recv()
        for cp in copies:
            cp.wait_send()
        for mine in local:
            mine.wait()

    return pl.pallas_call(
        body, name="sum_chips_and_share",
        out_shape=[jax.ShapeDtypeStruct((2,) + l.shape[1:], F32) for l in landed],
        in_specs=[pl.BlockSpec(memory_space=pltpu.VMEM)] * (2 * n),
        out_specs=[pl.BlockSpec(memory_space=pl.ANY)] * n,
        scratch_shapes=[pltpu.VMEM(l.shape[1:], F32) for l in landed]
        + [pltpu.SemaphoreType.DMA((n,)), pltpu.SemaphoreType.DMA((n,)), pltpu.SemaphoreType.DMA((n,))],
        compiler_params=pltpu.CompilerParams(vmem_limit_bytes=VMEM_LIMIT_BYTES),
    )(*landed, *parts)


def _mod_shard(c_all, w_ada, b_shard):
    def body(c_ref, w_ref, b_ref, o_ref, sc_ref):
        cc = c_ref[...]
        sc = cc * _sigmoid(cc)
        sc_ref[...] = sc
        o_ref[...] = _dot(sc, w_ref[...]) + b_ref[...]

    nb, d = c_all.shape
    nn = w_ada.shape[1]
    return pl.pallas_call(
        body, name="mod_shard",
        out_shape=[jax.ShapeDtypeStruct((nb, nn), F32), jax.ShapeDtypeStruct((nb, d), F32)],
        compiler_params=pltpu.CompilerParams(vmem_limit_bytes=VMEM_LIMIT_BYTES),
    )(c_all, w_ada, b_shard)


V_SH_M, V_SC_M, V_G_M, V_SH_F, V_SC_F, V_G_F, V_PRE_MIX, V_POST_MIX, V_PRE_FFN, V_POST_FFN = range(10)


def _vrow(vec_ref, r):
    return vec_ref[r:r + 1, :]


def _mix_fwd(x, vecs, w_in_b, w_out_b, sgu_g, wm_b, bsb, wp_b, ps, pmats, ts):
    s_len, d = x.shape
    nt = s_len // ts
    nblk = ts // HEAD
    n_proj = w_in_b.shape[1]

    def body(x_ref, vec_ref, win_ref, wout_ref, sg_ref, wm_ref, bs_ref, wp_ref, ps_ref, pm_ref,
             h1_ref, proj_ref, cat_ref, mixed_ref, x2_ref, h2_ref, carry_ref):
        i = pl.program_id(0)

        @pl.when(i == 0)
        def _():
            carry_ref[...] = jnp.zeros_like(carry_ref)

        x = x_ref[...]
        h1 = (((x * _rms(x)) * _vrow(vec_ref, V_PRE_MIX)) * (1.0 + _vrow(vec_ref, V_SC_M))
              + _vrow(vec_ref, V_SH_M)).astype(BF16)
        h1_ref[...] = h1
        proj = _dot(h1, win_ref[...])
        proj_ref[...] = proj.astype(BF16)
        t_glob = lax.broadcasted_iota(jnp.int32, (ts, HEAD), 0) + i * ts
        for h in range(N_HEADS):
            u = _gelu(proj[:, h * HEAD:(h + 1) * HEAD])
            v = _gelu(proj[:, A_WIDTH + h * HEAD:A_WIDTH + (h + 1) * HEAD])
            vn = ((v * _rms(v)) * sg_ref[h:h + 1, :]).astype(BF16)
            for b in range(nblk):
                rs = slice(b * HEAD, (b + 1) * HEAD)
                z = _dot(wm_ref[h], vn[rs]) + bs_ref[h]
                cat_ref[rs, h * HEAD:(h + 1) * HEAD] = (u[rs] * z).astype(BF16)
        for g in range(len(POOL_WINDOWS)):
            gs = slice(g * HEAD, (g + 1) * HEAD)
            p = proj[:, 2 * A_WIDTH + g * HEAD:2 * A_WIDTH + (g + 1) * HEAD]
            pooled = _pool_fwd(p, carry_ref[:, gs], g, t_glob)
            yb = _dot(pooled.astype(BF16), wp_ref[g]) * ps_ref[0:1, gs]
            cat_ref[:, A_WIDTH + g * HEAD:A_WIDTH + (g + 1) * HEAD] = yb.astype(BF16)
        carry_ref[...] = proj[ts - POOL_HALO:ts, 2 * A_WIDTH:]
        mixed = _dot(cat_ref[...], wout_ref[...])
        mixed_ref[...] = mixed
        x2 = x + _vrow(vec_ref, V_G_M) * ((mixed * _rms(mixed)) * _vrow(vec_ref, V_POST_MIX))
        x2_ref[...] = x2
        h2 = (((x2 * _rms(x2)) * _vrow(vec_ref, V_PRE_FFN)) * (1.0 + _vrow(vec_ref, V_SC_F))
              + _vrow(vec_ref, V_SH_F)).astype(BF16)
        for b in range(ts // FFN_TS):
            rs = slice(b * FFN_TS, (b + 1) * FFN_TS)
            h2_ref[rs, :] = _permute_bf16(pm_ref[0], h2[rs])

    row = lambda i: (i, 0)
    return _call(
        body, name="mix_fwd", grid=(nt,),
        in_specs=[_tiled((ts, d), row), _whole(vecs.shape), _resident(w_in_b.shape), _resident(w_out_b.shape),
                  _whole(sgu_g.shape), _whole(wm_b.shape), _whole(bsb.shape), _whole(wp_b.shape), _whole(ps.shape),
                  _whole(pmats.shape)],
        out_specs=[_tiled((ts, d), row), _tiled((ts, n_proj), row), _tiled((ts, d), row),
                   _tiled((ts, d), row), _tiled((ts, d), row), _tiled((ts, d), row)],
        out_shape=[jax.ShapeDtypeStruct((s_len, d), BF16), jax.ShapeDtypeStruct((s_len, n_proj), BF16),
                   jax.ShapeDtypeStruct((s_len, d), BF16), jax.ShapeDtypeStruct((s_len, d), F32),
                   jax.ShapeDtypeStruct((s_len, d), F32), jax.ShapeDtypeStruct((s_len, d), BF16)],
        scratch_shapes=[pltpu.VMEM((POOL_HALO, A_WIDTH), F32)],
        args=(x, vecs, w_in_b, w_out_b, sgu_g, wm_b, bsb, wp_b, ps, pmats))


def _perm_mats(ts):
    p = jnp.arange(ts)
    pm = (((p % 8) * (ts // 8) + p // 8)[:, None] == p[None, :]).astype(BF16)
    return jnp.stack([pm, pm.T])


def _permute_bf16(pm, xb):
    return _dot(pm, xb).astype(BF16)


def _permute_f32(pm, x):
    hi = x.astype(BF16)
    lo = (x - hi.astype(F32)).astype(BF16)
    return _dot(pm, hi) + _dot(pm, lo)


def _conv_out(u, um2, um1, cv_ref, cols):
    return (cv_ref[3:4, cols] + um2 * cv_ref[0:1, cols] + um1 * cv_ref[1:2, cols] + u * cv_ref[2:3, cols])


F_LOSS, F_DGF, F_DPOSTFFN = 0, 1, 2
B_DSHF, B_DSCF, B_DPREFFN, B_DGM, B_DPOSTMIX = 0, 1, 2, 3, 4
M_DSHM, M_DSCM, M_DPREMIX = 0, 1, 2
C_DCB, C_DCW = 0, 1


def _ffn_fwd(h2p, x2, tgt, w_up_b, w_down_b, cvec, vecs, pmats, ts):
    s_len, d = x2.shape
    ff2 = w_up_b.shape[1]
    ff = ff2 // 2
    nt = s_len // ts
    chunks = _ff_chunks(ff)

    def body(h2_ref, x2_ref, t_ref, wu_ref, wd_ref, cv_ref, vec_ref, pm_ref,
             up_ref, y_ref, act_ref, dy_ref, df_ref, acc_ref, carry_ref):
        @pl.when(pl.program_id(0) == 0)
        def _():
            carry_ref[...] = jnp.zeros_like(carry_ref)
            acc_ref[...] = jnp.zeros_like(acc_ref)

        h2v = h2_ref[...]

        def up_dots(o, w):
            return [_dot(h2v, wu_ref[:, base + o:base + o + w]) for base in (0, ff)]

        f = None
        pending = None
        nxt = up_dots(*chunks[0])
        for ci, (o, w) in enumerate(chunks):
            us = nxt
            if ci + 1 < len(chunks):
                nxt = up_dots(*chunks[ci + 1])
            if pending is not None:
                part = _dot(pending[0], wd_ref[pending[1]:pending[1] + pending[2], :])
                f = part if f is None else f + part
            sub0 = lax.broadcasted_iota(jnp.int32, (8, w), 0) == 0
            ys = []
            for base, u in zip((0, ff), us):
                cols = slice(base + o, base + o + w)
                up_ref[:, cols] = u.astype(BF16)
                b1 = jnp.where(sub0, pltpu.roll(carry_ref[8:16, cols], 1, 0), pltpu.roll(u[ts - 8:ts], 1, 0))
                b2 = jnp.where(sub0, pltpu.roll(carry_ref[0:8, cols], 1, 0), pltpu.roll(u[ts - 16:ts - 8], 1, 0))
                um1 = jnp.concatenate([b1, u[:ts - 8]], axis=0)
                um2 = jnp.concatenate([b2, b1, u[:ts - 16]], axis=0)
                ys.append(_conv_out(u, um2, um1, cv_ref, cols))
                carry_ref[:, cols] = u[ts - 16:ts]
            gate, val = ys
            sg = _sigmoid(gate)
            gs = gate * sg
            act = (gs * val).astype(BF16)
            act_ref[:, o:o + w] = act
            y_ref[:, o:o + w] = (val * (sg + gs * (1.0 - sg))).astype(BF16)
            y_ref[:, ff + o:ff + o + w] = gs.astype(BF16)
            pending = (act, o, w)
        f = f + _dot(pending[0], wd_ref[pending[1]:pending[1] + pending[2], :])
        f = _permute_f32(pm_ref[1], f)
        r3 = _rms(f)
        fhat = f * r3
        post = _vrow(vec_ref, V_POST_FFN)
        g_f = _vrow(vec_ref, V_G_F)
        fn = fhat * post
        e = (x2_ref[...] + g_f * fn) - t_ref[...]
        dy = e * (1.0 / d)
        dy_ref[...] = dy
        dfn = dy * g_f
        acc_ref[F_LOSS:F_LOSS + 1, :] += _colsum(e * e)
        acc_ref[F_DGF:F_DGF + 1, :] += _colsum(dy * fn)
        acc_ref[F_DPOSTFFN:F_DPOSTFFN + 1, :] += _colsum(dfn * fhat)
        dfhat = dfn * post
        df = (r3 * (dfhat - fhat * _rowmean(dfhat * fhat))).astype(BF16)
        df_ref[...] = _permute_bf16(pm_ref[0], df)

    row = lambda i: (i, 0)
    return pl.pallas_call(
        body, name="ffn_fwd", grid=(nt,),
        in_specs=[_tiled((ts, d), row), _tiled((ts, d), row), _tiled((ts, d), row), _resident(w_up_b.shape),
                  _resident(w_down_b.shape), _whole(cvec.shape), _whole(vecs.shape), _whole(pmats.shape)],
        out_specs=[_tiled((ts, ff2), row), _tiled((ts, ff2), row), _tiled((ts, ff), row), _tiled((ts, d), row),
                   _tiled((ts, d), row), _whole((8, d))],
        out_shape=[jax.ShapeDtypeStruct((s_len, ff2), BF16), jax.ShapeDtypeStruct((s_len, ff2), BF16),
                   jax.ShapeDtypeStruct((s_len, ff), BF16), jax.ShapeDtypeStruct((s_len, d), F32),
                   jax.ShapeDtypeStruct((s_len, d), BF16), jax.ShapeDtypeStruct((8, d), F32)],
        scratch_shapes=[pltpu.VMEM((16, ff2), F32)],
        compiler_params=_seq_params(),
    )(h2p, x2, tgt, w_up_b, w_down_b, cvec, vecs, pmats)


def _ffn_bwd(dfp, upp, yp, x2, dy, mixed, w_up_b, w_down_b, cvec, vecs, pmats, ts, rider=None):
    s_len, d = x2.shape
    ff2 = w_up_b.shape[1]
    ff = ff2 // 2
    nt = s_len // ts
    chunks = _ff_chunks(ff, 512)

    def body(df_ref, up_ref, y_ref, x2_ref, dy_ref, mx_ref, wu_ref, wd_ref, cv_ref, vec_ref, pm_ref,
             dup_ref, dx2_ref, dmx_ref, accc_ref, acc_ref, carry_ref):
        @pl.when(pl.program_id(0) == 0)
        def _():
            carry_ref[...] = jnp.zeros_like(carry_ref)
            accc_ref[...] = jnp.zeros_like(accc_ref)
            acc_ref[...] = jnp.zeros_like(acc_ref)

        dfv = df_ref[...]

        def dh2_add(acc, dups, o, w):
            for base, dup in zip((0, ff), dups):
                part = _dot_nt(dup, wu_ref[:, base + o:base + o + w])
                acc = part if acc is None else acc + part
            return acc

        dh2 = None
        pending = None
        nxt = _dot_nt(dfv, wd_ref[chunks[0][0]:chunks[0][0] + chunks[0][1], :])
        for ci, (o, w) in enumerate(chunks):
            dact = nxt
            if ci + 1 < len(chunks):
                o2, w2 = chunks[ci + 1]
                nxt = _dot_nt(dfv, wd_ref[o2:o2 + w2, :])
            if pending is not None:
                dh2 = dh2_add(dh2, *pending)
            sub7 = lax.broadcasted_iota(jnp.int32, (8, w), 0) == 7
            dups = []
            dys = (dact * y_ref[:, o:o + w].astype(F32), dact * y_ref[:, ff + o:ff + o + w].astype(F32))
            for base, dyv in zip((0, ff), dys):
                cols = slice(base + o, base + o + w)
                u = up_ref[:, cols].astype(F32)
                e0 = jnp.where(sub7, pltpu.roll(carry_ref[0:8, cols], 7, 0), pltpu.roll(dyv[0:8], 7, 0))
                e1 = jnp.where(sub7, pltpu.roll(carry_ref[8:16, cols], 7, 0), pltpu.roll(dyv[8:16], 7, 0))
                dyp1 = jnp.concatenate([dyv[8:], e0], axis=0)
                dyp2 = jnp.concatenate([dyv[16:], e0, e1], axis=0)
                accc_ref[C_DCB:C_DCB + 1, cols] += _colsum(dyv)
                accc_ref[C_DCW + 0:C_DCW + 1, cols] += _colsum(dyp2 * u)
                accc_ref[C_DCW + 1:C_DCW + 2, cols] += _colsum(dyp1 * u)
                accc_ref[C_DCW + 2:C_DCW + 3, cols] += _colsum(dyv * u)
                dup = (dyv * cv_ref[2:3, cols] + dyp1 * cv_ref[1:2, cols] + dyp2 * cv_ref[0:1, cols]).astype(BF16)
                dup_ref[:, cols] = dup
                dups.append(dup)
                carry_ref[:, cols] = dyv[0:16]
            pending = (dups, o, w)
        dh2 = dh2_add(dh2, *pending)
        dh2 = _permute_f32(pm_ref[1], dh2)
        x2 = x2_ref[...]
        r2 = _rms(x2)
        xn = x2 * r2
        pre = _vrow(vec_ref, V_PRE_FFN)
        one_sc = 1.0 + _vrow(vec_ref, V_SC_F)
        acc_ref[B_DSHF:B_DSHF + 1, :] += _colsum(dh2)
        acc_ref[B_DSCF:B_DSCF + 1, :] += _colsum(dh2 * (xn * pre))
        acc_ref[B_DPREFFN:B_DPREFFN + 1, :] += _colsum(dh2 * xn * one_sc)
        dxn = dh2 * pre * one_sc
        dx2 = dy_ref[...] + r2 * (dxn - xn * _rowmean(dxn * xn))
        dx2_ref[...] = dx2
        mixed = mx_ref[...]
        rm = _rms(mixed)
        mhat = mixed * rm
        post = _vrow(vec_ref, V_POST_MIX)
        acc_ref[B_DGM:B_DGM + 1, :] += _colsum(dx2 * (mhat * post))
        dmn = dx2 * _vrow(vec_ref, V_G_M)
        acc_ref[B_DPOSTMIX:B_DPOSTMIX + 1, :] += _colsum(dmn * mhat)
        dmhat = dmn * post
        dmx_ref[...] = (rm * (dmhat - mhat * _rowmean(dmhat * mhat))).astype(BF16)

    rev = lambda i: (nt - 1 - i, 0)
    return _call(
        body, name="ffn_bwd", grid=(nt,), rider=rider,
        in_specs=[_tiled((ts, d), rev), _tiled((ts, ff2), rev), _tiled((ts, ff2), rev), _tiled((ts, d), rev),
                  _tiled((ts, d), rev), _tiled((ts, d), rev), _resident(w_up_b.shape), _resident(w_down_b.shape),
                  _whole(cvec.shape), _whole(vecs.shape), _whole(pmats.shape)],
        out_specs=[_tiled((ts, ff2), rev), _tiled((ts, d), rev), _tiled((ts, d), rev), _whole((8, ff2)),
                   _whole((8, d))],
        out_shape=[jax.ShapeDtypeStruct((s_len, ff2), BF16), jax.ShapeDtypeStruct((s_len, d), F32),
                   jax.ShapeDtypeStruct((s_len, d), BF16), jax.ShapeDtypeStruct((8, ff2), F32),
                   jax.ShapeDtypeStruct((8, d), F32)],
        scratch_shapes=[pltpu.VMEM((16, ff2), F32)],
        args=(dfp, upp, yp, x2, dy, mixed, w_up_b, w_down_b, cvec, vecs, pmats))


def _mix_bwd(dmixed, dx2, x, proj, vecs, w_in_b, w_out_b, sgu_g, wm_b, wmt_b, bsb, wp_b, ps, ts, rider=None):
    s_len, d = x.shape
    nt = s_len // ts
    nblk = ts // HEAD
    n_proj = proj.shape[1]
    per = ts // POOL_HALO
    ext_len = ts + POOL_HALO

    def body(dmx_ref, dx2_ref, x_ref, proj_ref, projh_ref, vec_ref, win_ref, wout_ref, sg_ref, wm_ref, wmt_ref,
             bs_ref, wp_ref, ps_ref,
             gx_ref, dproj_ref, acc_ref, dwm_out, dwp_out, db_ref, dg_ref, dps_ref,
             carry_ref, dwm_ref, dwp_ref, dbz_ref):
        i = pl.program_id(0)
        tile = nt - 1 - i

        @pl.when(i == 0)
        def _():
            carry_ref[...] = jnp.zeros_like(carry_ref)
            for r in (acc_ref, dwm_ref, dwp_ref, dbz_ref, dg_ref, dps_ref):
                r[...] = jnp.zeros_like(r)

        dcat = _dot_nt(dmx_ref[...], wout_ref[...])
        t_glob = lax.broadcasted_iota(jnp.int32, (ts, HEAD), 0) + tile * ts
        for h in range(N_HEADS):
            hs = slice(h * HEAD, (h + 1) * HEAD)
            vs = slice(A_WIDTH + h * HEAD, A_WIDTH + (h + 1) * HEAD)
            au = proj_ref[:, hs].astype(F32)
            av = proj_ref[:, vs].astype(F32)
            u = _gelu(au)
            v = _gelu(av)
            rv = _rms(v)
            vhat = v * rv
            gain = sg_ref[h:h + 1, :]
            vn = (vhat * gain).astype(BF16)
            dout = dcat[:, hs]
            du_parts, dvn_parts = [], []
            for b in range(nblk):
                rs = slice(b * HEAD, (b + 1) * HEAD)
                z = _dot(wm_ref[h], vn[rs]) + bs_ref[h]
                du_parts.append(dout[rs] * z)
                dz = dout[rs] * u[rs]
                dbz_ref[h] += dz
                dzb = dz.astype(BF16)
                dwm_ref[h] += _dot_nt(dzb, vn[rs])
                dvn_parts.append(_dot(wmt_ref[h], dzb))
            du = jnp.concatenate(du_parts, axis=0)
            dvn = jnp.concatenate(dvn_parts, axis=0)
            dg_ref[h:h + 1, :] += _colsum(dvn * vhat)
            dvhat = dvn * gain
            dv = rv * (dvhat - vhat * _rowmean(dvhat * vhat))
            dproj_ref[:, hs] = (du * _gelu_grad(au)).astype(BF16)
            dproj_ref[:, vs] = (dv * _gelu_grad(av)).astype(BF16)
        for g in range(len(POOL_WINDOWS)):
            gs = slice(g * HEAD, (g + 1) * HEAD)
            pcols = slice(2 * A_WIDTH + g * HEAD, 2 * A_WIDTH + (g + 1) * HEAD)
            p = proj_ref[:, pcols].astype(F32)
            halo = jnp.where(tile > 0, projh_ref[:, pcols].astype(F32), 0.0)
            pb = _pool_fwd(p, halo, g, t_glob).astype(BF16)
            dyb = dcat[:, A_WIDTH + g * HEAD:A_WIDTH + (g + 1) * HEAD]
            dps_ref[0:1, gs] += _colsum(dyb * _dot(pb, wp_ref[g]))
            dyl = (dyb * ps_ref[0:1, gs]).astype(BF16)
            dwp_ref[g] += _dot_tn(pb, dyl)
            dpooled = _dot_nt(dyl, wp_ref[g])
            cnt = jnp.minimum(t_glob + 1, POOL_WINDOWS[g]).astype(F32)
            q = dpooled / cnt
            s = jnp.concatenate([q, carry_ref[:, gs]], axis=0)
            for step in range(g + 1):
                s = s + pltpu.roll(s, ext_len - (1 << step), 0)
            dproj_ref[:, pcols] = (s[:ts] - dpooled).astype(BF16)
            carry_ref[:, gs] = q[0:POOL_HALO]
        dh1 = _dot_nt(dproj_ref[...], win_ref[...])
        x = x_ref[...]
        r1 = _rms(x)
        xn = x * r1
        pre = _vrow(vec_ref, V_PRE_MIX)
        one_sc = 1.0 + _vrow(vec_ref, V_SC_M)
        acc_ref[M_DSHM:M_DSHM + 1, :] += _colsum(dh1)
        acc_ref[M_DSCM:M_DSCM + 1, :] += _colsum(dh1 * (xn * pre))
        acc_ref[M_DPREMIX:M_DPREMIX + 1, :] += _colsum(dh1 * xn * one_sc)
        dxn = dh1 * pre * one_sc
        gx_ref[...] = dx2_ref[...] + r1 * (dxn - xn * _rowmean(dxn * xn))

        @pl.when(i == nt - 1)
        def _():
            dwm_out[...] = dwm_ref[...].astype(BF16)
            dwp_out[...] = dwp_ref[...].astype(BF16)
            db_ref[...] = jnp.zeros_like(db_ref)
            for h in range(N_HEADS):
                db_ref[h:h + 1, :] = jnp.sum(dbz_ref[h].T, axis=0, keepdims=True)

    rev = lambda i: (nt - 1 - i, 0)
    halo_map = lambda i: (jnp.maximum((nt - 1 - i) * per - 1, 0), 0)
    hshape = (N_HEADS, HEAD, HEAD)
    return _call(
        body, name="mix_bwd", grid=(nt,), rider=rider,
        in_specs=[_tiled((ts, d), rev), _tiled((ts, d), rev), _tiled((ts, d), rev), _tiled((ts, n_proj), rev),
                  _tiled((POOL_HALO, n_proj), halo_map), _whole(vecs.shape), _resident(w_in_b.shape),
                  _resident(w_out_b.shape), _whole(sgu_g.shape), _whole(wm_b.shape), _whole(wmt_b.shape),
                  _whole(bsb.shape), _whole(wp_b.shape), _whole(ps.shape)],
        out_specs=[_tiled((ts, d), rev), _tiled((ts, n_proj), rev), _whole((8, d)), _whole(hshape), _whole(hshape),
                   _whole((8, HEAD)), _whole((8, HEAD)), _whole((8, A_WIDTH))],
        out_shape=[jax.ShapeDtypeStruct((s_len, d), F32), jax.ShapeDtypeStruct((s_len, n_proj), BF16),
                   jax.ShapeDtypeStruct((8, d), F32), jax.ShapeDtypeStruct(hshape, BF16),
                   jax.ShapeDtypeStruct(hshape, BF16), jax.ShapeDtypeStruct((8, HEAD), F32),
                   jax.ShapeDtypeStruct((8, HEAD), F32), jax.ShapeDtypeStruct((8, A_WIDTH), F32)],
        scratch_shapes=[pltpu.VMEM((POOL_HALO, A_WIDTH), F32), pltpu.VMEM(hshape, F32), pltpu.VMEM(hshape, F32),
                        pltpu.VMEM(hshape, F32)],
        args=(dmixed, dx2, x, proj, proj, vecs, w_in_b, w_out_b, sgu_g, wm_b, wmt_b, bsb, wp_b, ps))


def _wgrad(a, b, tn, ts, name, rider=None):
    s_len, m = a.shape
    n = b.shape[1]
    ts = min(ts, s_len)

    def body(a_ref, b_ref, o_ref):
        @pl.when(pl.program_id(1) == 0)
        def _():
            o_ref[...] = jnp.zeros_like(o_ref)

        o_ref[...] += _dot_tn(a_ref[...], b_ref[...])

    (g,), r_out = _call(
        body, name=name, grid=(n // tn, s_len // ts), rider=rider,
        in_specs=[pl.BlockSpec((ts, m), lambda j, s: (s, 0)), pl.BlockSpec((ts, tn), lambda j, s: (s, j))],
        out_specs=[pl.BlockSpec((m, tn), lambda j, s: (0, j))],
        out_shape=[jax.ShapeDtypeStruct((m, n), F32)], scratch_shapes=[], args=(a, b))
    return g, r_out


def _wgrad_pair(a1, b1, a2, b2, ts, name, rider=None):
    s_len = a1.shape[0]
    ts = min(ts, s_len)
    shapes = [(a1.shape[1], b1.shape[1]), (a2.shape[1], b2.shape[1])]

    def body(a1_ref, b1_ref, a2_ref, b2_ref, o1_ref, o2_ref):
        @pl.when(pl.program_id(0) == 0)
        def _():
            o1_ref[...] = jnp.zeros_like(o1_ref)
            o2_ref[...] = jnp.zeros_like(o2_ref)

        o1_ref[...] += _dot_tn(a1_ref[...], b1_ref[...])
        o2_ref[...] += _dot_tn(a2_ref[...], b2_ref[...])

    row = lambda s: (s, 0)
    return _call(
        body, name=name, grid=(s_len // ts,), rider=rider,
        in_specs=[pl.BlockSpec((ts, t.shape[1]), row) for t in (a1, b1, a2, b2)],
        out_specs=[_whole(sh) for sh in shapes],
        out_shape=[jax.ShapeDtypeStruct(sh, F32) for sh in shapes], scratch_shapes=[], args=(a1, b1, a2, b2))


def _adamw_big(g, w, m, v, name):
    r, cdim = g.shape
    tr = r
    while tr * cdim * 4 > (3 << 19) and tr % 16 == 0:
        tr //= 2

    def body(g_ref, w_ref, m_ref, v_ref, d_ref, nm_ref, nv_ref):
        delta, m2, v2 = _adamw_math(w_ref[0], g_ref[...], m_ref[0], v_ref[0])
        d_ref[0] = delta
        nm_ref[0] = m2
        nv_ref[0] = v2

    s3 = pl.BlockSpec((1, tr, cdim), lambda i: (0, i, 0))
    return pl.pallas_call(
        body, name=name, grid=(r // tr,),
        in_specs=[pl.BlockSpec((tr, cdim), lambda i: (i, 0)), s3, s3, s3],
        out_specs=[s3, s3, s3],
        out_shape=[jax.ShapeDtypeStruct(w.shape, F32)] * 3,
        compiler_params=pltpu.CompilerParams(dimension_semantics=("parallel",), vmem_limit_bytes=VMEM_LIMIT_BYTES),
    )(g, w, m, v)


def _wada_update(sct, gm, w, m, v):
    _, r, cdim = w.shape
    tr = 256
    kp = sct.shape[1]

    def body(s_ref, g_ref, w_ref, m_ref, v_ref, gw_ref, d_ref, nm_ref, nv_ref):
        g = _dot(s_ref[...], g_ref[...])
        gw_ref[0] = g
        delta, m2, v2 = _adamw_math(w_ref[0], g, m_ref[0], v_ref[0])
        d_ref[0] = delta
        nm_ref[0] = m2
        nv_ref[0] = v2

    s3 = pl.BlockSpec((1, tr, cdim), lambda i: (0, i, 0))
    return pl.pallas_call(
        body, name="wada_update", grid=(r // tr,),
        in_specs=[pl.BlockSpec((tr, kp), lambda i: (i, 0)), _whole(gm.shape), s3, s3, s3],
        out_specs=[s3, s3, s3, s3],
        out_shape=[jax.ShapeDtypeStruct(w.shape, F32)] * 4,
        compiler_params=pltpu.CompilerParams(dimension_semantics=("parallel",), vmem_limit_bytes=VMEM_LIMIT_BYTES),
    )(sct, gm, w, m, v)


def _small_update(g1, g2, g2s, gwm, gwp, gbz, gsg, gps, params):
    names = ["b_ada", "pre_mix_g", "post_mix_g", "sgu_norm_g", "w_spatial", "b_spatial", "w_pool", "pool_scale",
             "pre_ffn_g", "post_ffn_g", "conv_w", "conv_b"]
    d = g1.shape[2]
    flat_in = [g1, g2, g2s, gwm, gwp, gbz, gsg, gps]
    n_g = len(flat_in)
    for nm in names:
        flat_in += list(params[nm])

    def body(*refs):
        g1_ref, g2_ref, g2s_ref, gwm_ref, gwp_ref, gbz_ref, gsg_ref, gps_ref = refs[:n_g]
        wmv = refs[n_g:n_g + 3 * len(names)]
        loss_ref = refs[n_g + 3 * len(names)]
        outs = refs[n_g + 3 * len(names) + 1:]

        def dsum(ref, idx):
            acc = ref[(0,) + idx].astype(F32)
            for dev in range(1, N_DEV):
                acc = acc + ref[(dev,) + idx].astype(F32)
            return acc

        def apply(pi, g, widx, oidx):
            w_ref, m_ref, v_ref = wmv[3 * pi:3 * pi + 3]
            g_ref, d_ref, nm_ref, nv_ref = outs[4 * pi:4 * pi + 4]
            delta, m2, v2 = _adamw_math(w_ref[widx], g, m_ref[widx], v_ref[widx])
            g_ref[oidx] = g
            d_ref[oidx] = delta
            nm_ref[oidx] = m2
            nv_ref[oidx] = v2

        def row1(base, r):
            return (slice(base + r, base + r + 1), slice(None))

        tot = dsum(g1_ref, row1(0, F_LOSS))
        loss_ref[...] = jnp.zeros(loss_ref.shape, F32) + jnp.sum(tot) * (0.5 / d)
        mod_rows = [row1(16, M_DSHM), row1(16, M_DSCM), row1(8, B_DGM), row1(8, B_DSHF), row1(8, B_DSCF),
                    row1(0, F_DGF)]
        for j, rr in enumerate(mod_rows):
            cs = (slice(None), slice(j * d, (j + 1) * d))
            apply(0, dsum(g1_ref, rr), cs, cs)
        full = (slice(None), slice(None))
        apply(1, dsum(g1_ref, row1(16, M_DPREMIX)), full, full)
        apply(2, dsum(g1_ref, row1(8, B_DPOSTMIX)), full, full)
        apply(3, dsum(gsg_ref, (slice(0, N_HEADS), slice(None))), (0,), (0,))
        pos_i = lax.broadcasted_iota(jnp.int32, (HEAD, HEAD), 0)
        pos_j = lax.broadcasted_iota(jnp.int32, (HEAD, HEAD), 1)
        causal = (pos_j // CHUNK) <= (pos_i // CHUNK)
        for h in range(N_HEADS):
            blk = (slice(h * HEAD, (h + 1) * HEAD), slice(None))
            apply(4, jnp.where(causal, dsum(gwm_ref, blk), 0.0), (0, h), (0, h))
            apply(5, dsum(gbz_ref, (slice(h, h + 1), slice(None))), (0, slice(h, h + 1)), (0, slice(h, h + 1)))
            apply(6, dsum(gwp_ref, blk), (0, h), (0, h))
        apply(7, dsum(gps_ref, (slice(0, 1), slice(None))), full, full)
        apply(8, dsum(g1_ref, row1(8, B_DPREFFN)), full, full)
        apply(9, dsum(g1_ref, row1(0, F_DPOSTFFN)), full, full)
        apply(10, dsum(g2s_ref, (slice(C_DCW, C_DCW + 3), slice(None))), (0,), (0,))
        apply(11, dsum(g2_ref, (slice(C_DCB, C_DCB + 1), slice(None))), full, full)

    out_shape = [jax.ShapeDtypeStruct((8, HEAD), F32)]
    for nm in names:
        out_shape += [jax.ShapeDtypeStruct(params[nm][0].shape, F32)] * 4
    res = pl.pallas_call(
        body, name="small_update", out_shape=out_shape,
        compiler_params=pltpu.CompilerParams(vmem_limit_bytes=VMEM_LIMIT_BYTES),
    )(*flat_in)
    out = {nm: tuple(res[1 + 4 * i:5 + 4 * i]) for i, nm in enumerate(names)}
    return res[0], out


def kernel(x, c, w_ada, b_ada, pre_mix_g, post_mix_g, w_in, sgu_norm_g, w_spatial, b_spatial, w_pool, pool_scale, w_out, pre_ffn_g, post_ffn_g, w_up, conv_w, conv_b, w_down, loss_target, m_w_ada, m_b_ada, m_pre_mix_g, m_post_mix_g, m_w_in, m_sgu_norm_g, m_w_spatial, m_b_spatial, m_w_pool, m_pool_scale, m_w_out, m_pre_ffn_g, m_post_ffn_g, m_w_up, m_conv_w, m_conv_b, m_w_down, v_w_ada, v_b_ada, v_pre_mix_g, v_post_mix_g, v_w_in, v_sgu_norm_g, v_w_spatial, v_b_spatial, v_w_pool, v_pool_scale, v_w_out, v_pre_ffn_g, v_post_ffn_g, v_w_up, v_conv_w, v_conv_b, v_w_down):
    xi, yi, ci = _mesh_pos()
    k_me = 2 * xi + yi
    dev = 2 * k_me + ci
    s_len, d = x.shape[1], x.shape[2]
    x2d = x[0]
    tgt = loss_target[0]
    ff2 = conv_b.shape[1]
    n_ada = w_ada.shape[2]
    n_cw = conv_w.shape[2]

    k_idx = k_me.reshape(1).astype(jnp.int32)
    flags4 = (True, False, True, False)
    (w_in_s, w_out_s, w_up_s, w_down_s), (w_in_f, w_out_f, w_up_f, w_down_f) = _cast_bf16(
        [w_in[0], w_out[0], w_up[0], w_down[0]], flags4, k_idx)
    mix_flags = (True, False)
    fly_mix = _gather_begin([w_in_s, w_out_s], [w_in_f, w_out_f], mix_flags, w_in_s, "gather_begin_mix")

    cw_blk = jnp.concatenate([conv_w[0], jnp.zeros((5, n_cw), F32)], axis=0)
    c_blk = c.reshape(8, d // 8) + fly_mix[4][0:1, 0:1]
    c_all, cw_all = _run_rider(_allgather_rider([c_blk, cw_blk]), "gather_c_convw")
    c_all = c_all.reshape(N_DEV, 8, d // 8).reshape(N_DEV, d)
    cw_full = jnp.concatenate([cw_all[16 * k:16 * k + 8] for k in range(N_CHIP)], axis=1)
    cvec = jnp.concatenate([cw_full[0:3], conv_b, jnp.zeros((4, ff2), F32)], axis=0)
    b_shard = lax.dynamic_slice_in_dim(b_ada, k_me * n_ada, n_ada, axis=1)
    mod_k, sc_all = _mod_shard(c_all, w_ada[0], b_shard)
    (mod_g,) = _run_rider(_allgather_rider([mod_k]), "gather_mod")
    mod_all = jnp.concatenate([mod_g[16 * k:16 * k + 8] for k in range(N_CHIP)], axis=1)
    mod_me = lax.dynamic_slice_in_dim(mod_all, dev, 1, axis=0).reshape(6, d)
    vecs = jnp.concatenate([mod_me, pre_mix_g, post_mix_g, pre_ffn_g, post_ffn_g, jnp.zeros((6, d), F32)], axis=0)

    fly_ffn = _gather_begin([w_up_s, w_down_s], [w_up_f, w_down_f], mix_flags, mod_g, "gather_begin_ffn")
    w_in_b, w_out_b = _gather_finish(_gather_end(fly_mix, mix_flags, fly_ffn[4], "gather_end_mix")[1], mix_flags,
                                     "gather_finish_mix")

    pos = jnp.arange(HEAD)
    causal = (pos[None, :] // CHUNK) <= (pos[:, None] // CHUNK)
    wm = jnp.where(causal[None], w_spatial[0], 0.0)
    wm_b = wm.astype(BF16)
    wmt_b = jnp.swapaxes(wm, 1, 2).astype(BF16)
    bsb = jnp.broadcast_to(b_spatial[0][:, :, None], (N_HEADS, HEAD, HEAD))
    wp_b = w_pool[0].astype(BF16)
    sgu_g = jnp.concatenate([sgu_norm_g[0], jnp.zeros((4, HEAD), F32)], axis=0)
    ps = jnp.concatenate([pool_scale, jnp.zeros((7, A_WIDTH), F32)], axis=0)

    pmats = _perm_mats(FFN_TS)
    h1, proj, cat, mixed, x2, h2p = _mix_fwd(x2d, vecs, w_in_b, w_out_b, sgu_g, wm_b, bsb, wp_b, ps, pmats, ts=512)[0]
    w_up_b, w_down_b = _gather_finish(_gather_end(fly_ffn, mix_flags, h2p, "gather_end_ffn")[1], mix_flags,
                                      "gather_finish_ffn")
    up, yv, act, dy, dfp, acc_f = _ffn_fwd(h2p, x2, tgt, w_up_b, w_down_b, cvec, vecs, pmats, ts=FFN_TS)

    c_idx = ci.reshape(1).astype(jnp.int32)
    g_w_down, _ = _wgrad(act, dfp, d, 1024, "wgrad_down")
    (dup, dx2, dmixed, acc_c, acc_b), (land_down,) = _ffn_bwd(
        dfp, up, yv, x2, dy, mixed, w_up_b, w_down_b, cvec, vecs, pmats, ts=FFN_TS,
        rider=_sibling_rider([g_w_down], (False,)))
    (part_down,) = _sum_with_sibling([g_w_down], [land_down], (False,), c_idx, "pair_sum_down")
    g_w_up, (chips_down,) = _wgrad(h2p, dup, ff2 // 2, 2048, "wgrad_up", rider=_chips_rider([part_down]))
    (gx, dproj, acc_m, dwm, dwp, dbz, dsg, dps), (land_up,) = _mix_bwd(
        dmixed, dx2, x2d, proj, vecs, w_in_b, w_out_b, sgu_g, wm_b, wmt_b, bsb, wp_b, ps, ts=256,
        rider=_sibling_rider([g_w_up], (True,)))
    (part_up,) = _sum_with_sibling([g_w_up], [land_up], (True,), c_idx, "pair_sum_up")
    (g_w_out, g_w_in), (chips_up,) = _wgrad_pair(cat, dmixed, h1, dproj, 1024, "wgrad_mix",
                                                 rider=_chips_rider([part_up]))
    land_mix = _run_rider(_sibling_rider([g_w_in, g_w_out], (True, False)), "reduce_to_sibling")
    parts_mix = _sum_with_sibling([g_w_in, g_w_out], land_mix, (True, False), c_idx, "pair_sum_mix")
    chips_in, chips_out = _run_rider(_chips_rider(parts_mix), "reduce_over_chips")
    reduced = _sum_chips_and_share([chips_in, chips_out, chips_up, chips_down],
                                   [parts_mix[0], parts_mix[1], part_up, part_down])
    big = {}
    for nm, red, (w, m, v) in zip(("w_in", "w_out", "w_up", "w_down"), reduced,
                                  ((w_in, m_w_in, v_w_in), (w_out, m_w_out, v_w_out),
                                   (w_up, m_w_up, v_w_up), (w_down, m_w_down, v_w_down))):
        g = red.reshape(w.shape[1], w.shape[2])
        big[nm] = (g.reshape(w.shape),) + tuple(_adamw_big(g, w, m, v, "adamw_" + nm))

    g1 = jnp.concatenate([acc_f, acc_b, acc_m], axis=0)
    hflat = (N_HEADS * HEAD, HEAD)
    gathered = _run_rider(_allgather_rider(
        [g1, acc_c, dwm.reshape(hflat), dwp.reshape(hflat), dbz, dsg, dps]), "gather_small_grads")
    g1a, g2a, gwm, gwp, gbz, gsg, gps = [t.reshape((N_DEV, t.shape[0] // N_DEV, t.shape[1])) for t in gathered]
    g2s = lax.dynamic_slice_in_dim(g2a, k_me * n_cw, n_cw, axis=2)
    params = {
        "b_ada": (b_ada, m_b_ada, v_b_ada), "pre_mix_g": (pre_mix_g, m_pre_mix_g, v_pre_mix_g),
        "post_mix_g": (post_mix_g, m_post_mix_g, v_post_mix_g),
        "sgu_norm_g": (sgu_norm_g, m_sgu_norm_g, v_sgu_norm_g), "w_spatial": (w_spatial, m_w_spatial, v_w_spatial),
        "b_spatial": (b_spatial, m_b_spatial, v_b_spatial), "w_pool": (w_pool, m_w_pool, v_w_pool),
        "pool_scale": (pool_scale, m_pool_scale, v_pool_scale), "pre_ffn_g": (pre_ffn_g, m_pre_ffn_g, v_pre_ffn_g),
        "post_ffn_g": (post_ffn_g, m_post_ffn_g, v_post_ffn_g), "conv_w": (conv_w, m_conv_w, v_conv_w),
        "conv_b": (conv_b, m_conv_b, v_conv_b),
    }
    loss_slab, small = _small_update(g1a, g2a, g2s, gwm, gwp, gbz, gsg, gps, params)

    gmod_all = jnp.concatenate(
        [g1a[:, 16 + M_DSHM], g1a[:, 16 + M_DSCM], g1a[:, 8 + B_DGM], g1a[:, 8 + B_DSHF], g1a[:, 8 + B_DSCF],
         g1a[:, F_DGF]], axis=1)
    gm = lax.dynamic_slice_in_dim(gmod_all, k_me * n_ada, n_ada, axis=1)
    gm = jnp.concatenate([gm, jnp.zeros((HEAD - N_DEV, n_ada), F32)], axis=0)
    sct = jnp.concatenate([sc_all.T, jnp.zeros((d, HEAD - N_DEV), F32)], axis=1)
    ada = tuple(_wada_update(sct, gm, w_ada, m_w_ada, v_w_ada))

    everything = dict(small)
    everything.update(big)
    everything["w_ada"] = ada
    order = ["w_ada", "b_ada", "pre_mix_g", "post_mix_g", "w_in", "sgu_norm_g", "w_spatial", "b_spatial", "w_pool",
             "pool_scale", "w_out", "pre_ffn_g", "post_ffn_g", "w_up", "conv_w", "conv_b", "w_down"]
    outs = [loss_slab[0, 0], gx.reshape(x.shape)]
    for j in range(4):
        outs += [everything[nm][j] for nm in order]
    return tuple(outs)
```

```python
import functools

import jax
import jax.numpy as jnp
from jax import lax
from jax.experimental import pallas as pl
from jax.experimental.pallas import tpu as pltpu

F32 = jnp.float32
BF16 = jnp.bfloat16
MESH = pl.DeviceIdType.MESH

EPS = 1e-6
HEAD = 128
N_HEADS = 4
A_WIDTH = N_HEADS * HEAD
CHUNK = 64
POOL_WINDOWS = (2, 4, 8, 16)
POOL_HALO = 16
FFN_TS = 256

ADAM_LR = 0.001
ADAM_B1 = 0.9
ADAM_B2 = 0.999
ADAM_EPS = 1e-08
ADAM_WD = 0.01
ADAM_STEP = 10

VMEM_LIMIT_BYTES = 58 * 1024 * 1024
N_DEV = 8
N_CHIP = 4


def _dot(a, b):
    return jnp.dot(a, b, preferred_element_type=F32)


def _dot_nt(a, b):
    return lax.dot_general(a, b, (((1,), (1,)), ((), ())), preferred_element_type=F32)


def _dot_tn(a, b):
    return lax.dot_general(a, b, (((0,), (0,)), ((), ())), preferred_element_type=F32)


def _gelu(x):
    return x * (0.5 * (1.0 + jnp.tanh(0.7978845608028654 * (x + 0.044715 * (x * x * x)))))


def _gelu_grad(x):
    t = jnp.tanh(0.7978845608028654 * (x + 0.044715 * (x * x * x)))
    return 0.5 * (1.0 + t) + (0.5 * x) * (1.0 - t * t) * (0.7978845608028654 * (1.0 + 0.134145 * (x * x)))


def _sigmoid(x):
    return 1.0 / (1.0 + jnp.exp(-x))


def _rms(x):
    return lax.rsqrt(jnp.mean(x * x, axis=-1, keepdims=True) + EPS)


def _colsum(x):
    return jnp.sum(x, axis=0, keepdims=True)


def _rowmean(x):
    return jnp.mean(x, axis=-1, keepdims=True)


def _tiled(shape, index_map):
    return pl.BlockSpec(shape, index_map)


def _resident(shape):
    nd = len(shape)
    return pl.BlockSpec(shape, lambda *_: (0,) * nd, pipeline_mode=pl.Buffered(1))


def _whole(shape):
    nd = len(shape)
    return pl.BlockSpec(shape, lambda *_: (0,) * nd)


def _seq_params():
    return pltpu.CompilerParams(dimension_semantics=("arbitrary",), vmem_limit_bytes=VMEM_LIMIT_BYTES)


def _ff_chunks(f, width=768):
    out, o = [], 0
    while o < f:
        w = min(width, f - o)
        out.append((o, w))
        o += w
    return out


def _pool_fwd(p, halo, g, t_glob):
    ext = jnp.concatenate([halo, p], axis=0)
    s = ext
    for step in range(g + 1):
        s = s + pltpu.roll(s, 1 << step, 0)
    cnt = jnp.minimum(t_glob + 1, POOL_WINDOWS[g]).astype(F32)
    return s[POOL_HALO:] / cnt - p


def _adamw_math(w, g, m, v):
    m = ADAM_B1 * m + (1.0 - ADAM_B1) * g
    v = ADAM_B2 * v + (1.0 - ADAM_B2) * (g * g)
    m_hat = m / (1.0 - ADAM_B1 ** ADAM_STEP)
    v_hat = v / (1.0 - ADAM_B2 ** ADAM_STEP)
    delta = -ADAM_LR * (m_hat / (jnp.sqrt(v_hat) + ADAM_EPS) + ADAM_WD * w)
    return delta, m, v


def _mesh_pos():
    return lax.axis_index("x"), lax.axis_index("y"), lax.axis_index("c")


class _Rider:
    def __init__(self, inputs, out_shape, sems, start, finish):
        self.inputs, self.out_shape, self.sems = list(inputs), list(out_shape), list(sems)
        self.start, self.finish = start, finish


def _call(body, *, name, grid, in_specs, out_specs, out_shape, scratch_shapes, args, rider=None):
    params = pltpu.CompilerParams(dimension_semantics=("arbitrary",) * len(grid), vmem_limit_bytes=VMEM_LIMIT_BYTES)
    if rider is None:
        res = pl.pallas_call(body, name=name, grid=grid, in_specs=in_specs, out_specs=out_specs, out_shape=out_shape,
                             scratch_shapes=scratch_shapes, compiler_params=params)(*args)
        return tuple(res), ()
    cuts = [len(in_specs), len(rider.inputs), len(out_specs), len(rider.out_shape), len(scratch_shapes),
            len(rider.sems)]

    def hosted(*refs):
        groups, a = [], 0
        for cnt in cuts:
            groups.append(refs[a:a + cnt])
            a += cnt
        ins, r_in, outs, r_out, scr, r_sem = groups
        first = functools.reduce(jnp.logical_and, [pl.program_id(k) == 0 for k in range(len(grid))])
        last = functools.reduce(jnp.logical_and, [pl.program_id(k) == grid[k] - 1 for k in range(len(grid))])

        @pl.when(first)
        def _():
            rider.start(r_in, r_out, r_sem)

        body(*ins, *outs, *scr)

        @pl.when(last)
        def _():
            rider.finish(r_in, r_out, r_sem)

    anyspec = pl.BlockSpec(memory_space=pl.ANY)
    res = pl.pallas_call(
        hosted, name=name, grid=grid,
        in_specs=list(in_specs) + [anyspec] * cuts[1], out_specs=list(out_specs) + [anyspec] * cuts[3],
        out_shape=list(out_shape) + rider.out_shape, scratch_shapes=list(scratch_shapes) + rider.sems,
        compiler_params=params)(*args, *rider.inputs)
    return tuple(res[:cuts[2]]), tuple(res[cuts[2]:])


def _run_rider(rider, name):
    n_in, n_out = len(rider.inputs), len(rider.out_shape)

    def body(*refs):
        r_in, r_out, r_sem = refs[:n_in], refs[n_in:n_in + n_out], refs[n_in + n_out:]
        rider.start(r_in, r_out, r_sem)
        rider.finish(r_in, r_out, r_sem)

    anyspec = pl.BlockSpec(memory_space=pl.ANY)
    return pl.pallas_call(body, name=name, out_shape=rider.out_shape, in_specs=[anyspec] * n_in,
                          out_specs=[anyspec] * n_out, scratch_shapes=rider.sems)(*rider.inputs)


def _allgather_rider(arrs):
    n = len(arrs)

    def plan(ins, outs, sems):
        send_sems, recv_sems, local_sems = sems
        x, y, c = _mesh_pos()
        me, sibling = (x, y, c), (x, y, 1 - c)
        chips = [(1 - x, y), (x, 1 - y), (1 - x, 1 - y)]

        def rows(a, px, py, pc):
            r = ins[a].shape[0]
            return outs[a].at[pl.ds(pl.multiple_of((4 * px + 2 * py + pc) * r, 8), r), :]

        def copy(a, k, block, to, src=None):
            return pltpu.make_async_remote_copy(
                src_ref=rows(a, *block) if src is None else src, dst_ref=rows(a, *block),
                send_sem=send_sems.at[a * 7 + k], recv_sem=recv_sems.at[a * 7 + k],
                device_id=to, device_id_type=MESH)

        local = [pltpu.make_async_copy(ins[a], rows(a, *me), local_sems.at[a]) for a in range(n)]
        first = []
        for a in range(n):
            first.append(copy(a, 0, me, sibling, src=ins[a]))
            first += [copy(a, 1 + j, me, (*chip, c), src=ins[a]) for j, chip in enumerate(chips)]
        return c, me, sibling, chips, copy, local, first

    def start(ins, outs, sems):
        *_, local, first = plan(ins, outs, sems)
        for cp in local + first:
            cp.start()

    def finish(ins, outs, sems):
        c, me, sibling, chips, copy, local, first = plan(ins, outs, sems)
        passed = []
        for a in range(n):
            for j, chip in enumerate(chips):
                copy(a, 1 + j, (*chip, c), me).wait_recv()
                fwd = copy(a, 4 + j, (*chip, c), sibling)
                fwd.start()
                passed.append(fwd)
        for a in range(n):
            copy(a, 0, sibling, me).wait_recv()
            for j, chip in enumerate(chips):
                copy(a, 4 + j, (*chip, 1 - c), me).wait_recv()
        for cp in first + passed:
            cp.wait_send()
        for mine in local:
            mine.wait()

    return _Rider(arrs, [jax.ShapeDtypeStruct((N_DEV * a.shape[0], a.shape[1]), a.dtype) for a in arrs],
                  [pltpu.SemaphoreType.DMA((7 * n,)), pltpu.SemaphoreType.DMA((7 * n,)),
                   pltpu.SemaphoreType.DMA((n,))], start, finish)


def _piece(ref, col_sharded, k, h):
    m, n = ref.shape
    if col_sharded:
        mh, nc = m // 2, n // N_CHIP
        return ref.at[pl.ds(pl.multiple_of(h * mh, 16), mh), pl.ds(pl.multiple_of(k * nc, 128), nc)]
    rp = m // (2 * N_CHIP)
    return ref.at[pl.ds(pl.multiple_of((2 * k + h) * rp, 16), rp), :]


def _piece_shape(shape, col_sharded):
    m, n = shape
    return (m // 2, n // N_CHIP) if col_sharded else (m // (2 * N_CHIP), n)


def _cast_bf16(arrs, col_flags, k_idx):
    n = len(arrs)

    def body(k_ref, *refs):
        for a in range(n):
            val = refs[a][...].astype(BF16)
            refs[n + a][...] = val
            refs[2 * n + a][...] = val

    whole = [pl.BlockSpec(a.shape, lambda i, k_ref: (0, 0)) for a in arrs]
    window = [pl.BlockSpec(a.shape, (lambda i, k_ref: (0, k_ref[0])) if col else (lambda i, k_ref: (k_ref[0], 0)))
              for a, col in zip(arrs, col_flags)]
    res = pl.pallas_call(
        body, name="cast_weights",
        grid_spec=pltpu.PrefetchScalarGridSpec(num_scalar_prefetch=1, grid=(1,), in_specs=whole,
                                               out_specs=whole + window),
        out_shape=[jax.ShapeDtypeStruct(a.shape, BF16) for a in arrs]
        + [jax.ShapeDtypeStruct(fs, BF16) for fs in _full_shapes(arrs, col_flags)],
        compiler_params=pltpu.CompilerParams(vmem_limit_bytes=VMEM_LIMIT_BYTES))(k_idx, *arrs)
    return list(res[:n]), list(res[n:])


def _full_shapes(shards, col_flags):
    return [(s.shape[0], s.shape[1] * N_CHIP) if col else (s.shape[0] * N_CHIP, s.shape[1])
            for s, col in zip(shards, col_flags)]


def _ici_copies(shard_refs, full_refs, send_sems, recv_sems, col_flags):
    x, y, c = _mesh_pos()
    k_me = 2 * x + y
    copies = []
    for a, (s_ref, f_ref) in enumerate(zip(shard_refs, full_refs)):
        rows = s_ref.shape[0] // 2
        src = s_ref.at[pl.ds(pl.multiple_of(c * rows, 16), rows), :]
        for j, chip in enumerate([(1 - x, y), (x, 1 - y), (1 - x, 1 - y)]):
            copies.append(pltpu.make_async_remote_copy(
                src_ref=src, dst_ref=_piece(f_ref, col_flags[a], k_me, c),
                send_sem=send_sems.at[a * 3 + j], recv_sem=recv_sems.at[a * 3 + j],
                device_id=(*chip, c), device_id_type=MESH))
    return copies


def _split_begin(bufs, n_sems, make_copies, after, name):
    n = len(bufs)
    hbm = pl.BlockSpec(memory_space=pltpu.HBM)
    sem = pl.BlockSpec(memory_space=pltpu.SEMAPHORE)

    def body(*refs):
        for cp in make_copies(refs[:n], refs[n + 1], refs[n + 2]):
            cp.start()
        refs[-1][...] = jnp.zeros_like(refs[-1])

    args = [pltpu.with_memory_space_constraint(t, pltpu.HBM) for t in bufs]
    res = pl.pallas_call(
        body, name=name,
        out_shape=[pltpu.SemaphoreType.DMA((n_sems,)), pltpu.SemaphoreType.DMA((n_sems,))]
        + [pltpu.HBM(t.shape, t.dtype) for t in args] + [jax.ShapeDtypeStruct((8, HEAD), F32)],
        in_specs=[hbm] * n + [pl.BlockSpec(memory_space=pl.ANY)],
        out_specs=[sem, sem] + [hbm] * n + [pl.BlockSpec(memory_space=pltpu.VMEM)],
        input_output_aliases={i: 2 + i for i in range(n)},
        compiler_params=pltpu.CompilerParams(has_side_effects=pltpu.SideEffectType.DATAFLOW_SIDE_EFFECTING),
    )(*args, after)
    return res[0], res[1], list(res[2:2 + n]), res[-1]


def _split_end(handle, make_copies, after, name):
    send_sems, recv_sems, bufs, _ = handle
    n = len(bufs)
    hbm = pl.BlockSpec(memory_space=pltpu.HBM)
    sem = pl.BlockSpec(memory_space=pltpu.SEMAPHORE)

    def body(*refs):
        for cp in make_copies(refs[:n], refs[n], refs[n + 1]):
            cp.wait_send()
            cp.wait_recv()

    res = pl.pallas_call(
        body, name=name,
        out_shape=[pltpu.HBM(t.shape, t.dtype) for t in bufs],
        in_specs=[hbm] * n + [sem, sem, pl.BlockSpec(memory_space=pl.ANY)],
        out_specs=[hbm] * n,
        input_output_aliases={i: i for i in range(n)},
        compiler_params=pltpu.CompilerParams(has_side_effects=pltpu.SideEffectType.DATAFLOW_SIDE_EFFECTING),
    )(*bufs, send_sems, recv_sems, after)
    return list(res)


def _gather_copies(n, col_flags):
    return lambda refs, send_sems, recv_sems: _ici_copies(refs[:n], refs[n:], send_sems, recv_sems, col_flags)


def _gather_begin(shards, fulls, col_flags, after, name):
    n = len(shards)
    return _split_begin(list(shards) + list(fulls), 3 * n, _gather_copies(n, col_flags), after, name)


def _gather_end(handle, col_flags, after, name):
    n = len(handle[2]) // 2
    return _split_end(handle, _gather_copies(n, col_flags), after, name)[n:]


def _sibling_copies(col_flag):
    def make(refs, send_sems, recv_sems):
        grad_ref, land_ref = refs
        x, y, c = _mesh_pos()
        return [pltpu.make_async_remote_copy(
            src_ref=_piece(grad_ref, col_flag, k, 1 - c), dst_ref=land_ref.at[k],
            send_sem=send_sems.at[k], recv_sem=recv_sems.at[k],
            device_id=(x, y, 1 - c), device_id_type=MESH) for k in range(N_CHIP)]
    return make


def _sibling_begin(grad, col_flag, after, name):
    land = lax.empty((N_CHIP,) + _piece_shape(grad.shape, col_flag), grad.dtype)
    return _split_begin([grad, land], N_CHIP, _sibling_copies(col_flag), after, name)


def _sibling_end(handle, col_flag, after, name):
    return _split_end(handle, _sibling_copies(col_flag), after, name)


def _gather_finish(fulls, col_flags, name):
    n = len(fulls)

    def body(*refs):
        full_refs = refs[n:2 * n]
        send_sems, recv_sems = refs[2 * n:]
        x, y, c = _mesh_pos()
        passed, arriving = [], []
        for a in range(n):
            for j, chip in enumerate([(1 - x, y), (x, 1 - y), (1 - x, 1 - y)]):
                k_from = 2 * chip[0] + chip[1]
                for h, group in ((c, passed), (1 - c, arriving)):
                    win = _piece(full_refs[a], col_flags[a], k_from, h)
                    group.append(pltpu.make_async_remote_copy(
                        src_ref=win, dst_ref=win, send_sem=send_sems.at[a * 3 + j],
                        recv_sem=recv_sems.at[a * 3 + j], device_id=(x, y, 1 - c), device_id_type=MESH))
        for cp in passed:
            cp.start()
        for cp in arriving:
            cp.wait_recv()
        for cp in passed:
            cp.wait_send()

    anyspec = pl.BlockSpec(memory_space=pl.ANY)
    return pl.pallas_call(
        body, name=name,
        out_shape=[jax.ShapeDtypeStruct(f.shape, f.dtype) for f in fulls],
        in_specs=[anyspec] * n, out_specs=[anyspec] * n,
        input_output_aliases={a: a for a in range(n)},
        scratch_shapes=[pltpu.SemaphoreType.DMA((3 * n,)), pltpu.SemaphoreType.DMA((3 * n,))],
    )(*fulls)


def _sibling_rider(grads, col_flags):
    n = len(grads)
    pshapes = [_piece_shape(g.shape, col) for g, col in zip(grads, col_flags)]

    def copies(ins, outs, sems):
        send_sems, recv_sems = sems
        x, y, c = _mesh_pos()
        return [pltpu.make_async_remote_copy(
            src_ref=_piece(ins[a], col_flags[a], k, 1 - c), dst_ref=outs[a].at[k],
            send_sem=send_sems.at[a * N_CHIP + k], recv_sem=recv_sems.at[a * N_CHIP + k],
            device_id=(x, y, 1 - c), device_id_type=MESH) for a in range(n) for k in range(N_CHIP)]

    def start(ins, outs, sems):
        for cp in copies(ins, outs, sems):
            cp.start()

    def finish(ins, outs, sems):
        cps = copies(ins, outs, sems)
        for cp in cps:
            cp.wait_recv()
        for cp in cps:
            cp.wait_send()

    return _Rider(grads, [jax.ShapeDtypeStruct((N_CHIP,) + ps, g.dtype) for ps, g in zip(pshapes, grads)],
                  [pltpu.SemaphoreType.DMA((N_CHIP * n,)), pltpu.SemaphoreType.DMA((N_CHIP * n,))], start, finish)


def _sum_with_sibling(grads, landed, col_flags, c_idx, name):
    n = len(grads)
    pshapes = [_piece_shape(g.shape, col) for g, col in zip(grads, col_flags)]

    def body(c_ref, *refs):
        ins, lands, outs = refs[:n], refs[n:2 * n], refs[2 * n:]
        for a in range(n):
            outs[a][0] = (ins[a][...] + lands[a][0]).astype(BF16)

    in_specs = []
    for ps, col in zip(pshapes, col_flags):
        if col:
            in_specs.append(pl.BlockSpec(ps, lambda k, c_ref: (c_ref[0], k)))
        else:
            in_specs.append(pl.BlockSpec(ps, lambda k, c_ref: (2 * k + c_ref[0], 0)))
    land_specs = [pl.BlockSpec((1,) + ps, lambda k, c_ref: (k, 0, 0)) for ps in pshapes]
    return pl.pallas_call(
        body, name=name,
        grid_spec=pltpu.PrefetchScalarGridSpec(
            num_scalar_prefetch=1, grid=(N_CHIP,),
            in_specs=in_specs + land_specs, out_specs=land_specs),
        out_shape=[jax.ShapeDtypeStruct((N_CHIP,) + ps, BF16) for ps in pshapes],
        compiler_params=pltpu.CompilerParams(dimension_semantics=("arbitrary",), vmem_limit_bytes=VMEM_LIMIT_BYTES),
    )(c_idx, *grads, *landed)


def _chips_rider(parts):
    n = len(parts)

    def plan(ins, outs, sems, arriving):
        send_sems, recv_sems = sems
        x, y, c = _mesh_pos()
        k_me = 2 * x + y
        copies = []
        for a in range(n):
            for j, chip in enumerate([(1 - x, y), (x, 1 - y), (1 - x, 1 - y)]):
                k_peer = 2 * chip[0] + chip[1]
                copies.append(pltpu.make_async_remote_copy(
                    src_ref=ins[a].at[k_peer], dst_ref=outs[a].at[k_peer if arriving else k_me],
                    send_sem=send_sems.at[a * 3 + j], recv_sem=recv_sems.at[a * 3 + j],
                    device_id=(*chip, c), device_id_type=MESH))
        return copies

    def start(ins, outs, sems):
        for cp in plan(ins, outs, sems, False):
            cp.start()

    def finish(ins, outs, sems):
        arrivals = plan(ins, outs, sems, True)
        for cp in arrivals:
            cp.wait_recv()
        for cp in arrivals:
            cp.wait_send()

    return _Rider(parts, [jax.ShapeDtypeStruct(p.shape, p.dtype) for p in parts],
                  [pltpu.SemaphoreType.DMA((3 * n,)), pltpu.SemaphoreType.DMA((3 * n,))], start, finish)


def _sum_chips_and_share(landed, parts):
    n = len(landed)

    def body(*refs):
        ins, own, outs, red = refs[:n], refs[n:2 * n], refs[2 * n:3 * n], refs[3 * n:4 * n]
        send_sems, recv_sems, local_sems = refs[4 * n:]
        x, y, c = _mesh_pos()
        sibling = (x, y, 1 - c)
        k_me = 2 * x + y
        copies, local = [], []
        for a in range(n):
            for k in range(N_CHIP):
                @pl.when(k_me == k)
                def _():
                    term = own[a][k].astype(F32)
                    red[a][...] = term if k == 0 else red[a][...] + term

                @pl.when(k_me != k)
                def _():
                    term = ins[a][k].astype(F32)
                    red[a][...] = term if k == 0 else red[a][...] + term

            mine = pltpu.make_async_copy(red[a], outs[a].at[c], local_sems.at[a])
            mine.start()
            local.append(mine)
            cp = pltpu.make_async_remote_copy(
                src_ref=red[a], dst_ref=outs[a].at[c],
                send_sem=send_sems.at[a], recv_sem=recv_sems.at[a],
                device_id=sibling, device_id_type=MESH)
            cp.start()
            copies.append(cp)
        for a in range(n):
            pltpu.make_async_remote_copy(
                src_ref=red[a], dst_ref=outs[a].at[1 - c],
                send_sem=send_sems.at[a], recv_sem=recv_sems.at[a],
                device_id=sibling, device_id_type=MESH).wait_recv()
        for cp in copies:
            cp.wait_send()
        for mine in local:
            mine.wait()

    return pl.pallas_call(
        body, name="sum_chips_and_share",
        out_shape=[jax.ShapeDtypeStruct((2,) + l.shape[1:], F32) for l in landed],
        in_specs=[pl.BlockSpec(memory_space=pltpu.VMEM)] * (2 * n),
        out_specs=[pl.BlockSpec(memory_space=pl.ANY)] * n,
        scratch_shapes=[pltpu.VMEM(l.shape[1:], F32) for l in landed]
        + [pltpu.SemaphoreType.DMA((n,)), pltpu.SemaphoreType.DMA((n,)), pltpu.SemaphoreType.DMA((n,))],
        compiler_params=pltpu.CompilerParams(vmem_limit_bytes=VMEM_LIMIT_BYTES),
    )(*landed, *parts)


def _mod_shard(c_all, w_ada, b_shard):
    def body(c_ref, w_ref, b_ref, o_ref, sc_ref):
        cc = c_ref[...]
        sc = cc * _sigmoid(cc)
        sc_ref[...] = sc
        o_ref[...] = _dot(sc, w_ref[...]) + b_ref[...]

    nb, d = c_all.shape
    nn = w_ada.shape[1]
    return pl.pallas_call(
        body, name="mod_shard",
        out_shape=[jax.ShapeDtypeStruct((nb, nn), F32), jax.ShapeDtypeStruct((nb, d), F32)],
        compiler_params=pltpu.CompilerParams(vmem_limit_bytes=VMEM_LIMIT_BYTES),
    )(c_all, w_ada, b_shard)


V_SH_M, V_SC_M, V_G_M, V_SH_F, V_SC_F, V_G_F, V_PRE_MIX, V_POST_MIX, V_PRE_FFN, V_POST_FFN = range(10)


def _vrow(vec_ref, r):
    return vec_ref[r:r + 1, :]


def _mix_fwd(x, vecs, w_in_b, w_out_b, sgu_g, wm_b, bsb, wp_b, ps, pmats, ts):
    s_len, d = x.shape
    nt = s_len // ts
    nblk = ts // HEAD
    n_proj = w_in_b.shape[1]

    def body(x_ref, vec_ref, win_ref, wout_ref, sg_ref, wm_ref, bs_ref, wp_ref, ps_ref, pm_ref,
             h1_ref, proj_ref, cat_ref, mixed_ref, x2_ref, h2_ref, carry_ref):
        i = pl.program_id(0)

        @pl.when(i == 0)
        def _():
            carry_ref[...] = jnp.zeros_like(carry_ref)

        x = x_ref[...]
        h1 = (((x * _rms(x)) * _vrow(vec_ref, V_PRE_MIX)) * (1.0 + _vrow(vec_ref, V_SC_M))
              + _vrow(vec_ref, V_SH_M)).astype(BF16)
        h1_ref[...] = h1
        proj = _dot(h1, win_ref[...])
        proj_ref[...] = proj.astype(BF16)
        t_glob = lax.broadcasted_iota(jnp.int32, (ts, HEAD), 0) + i * ts
        for h in range(N_HEADS):
            u = _gelu(proj[:, h * HEAD:(h + 1) * HEAD])
            v = _gelu(proj[:, A_WIDTH + h * HEAD:A_WIDTH + (h + 1) * HEAD])
            vn = ((v * _rms(v)) * sg_ref[h:h + 1, :]).astype(BF16)
            for b in range(nblk):
                rs = slice(b * HEAD, (b + 1) * HEAD)
                z = _dot(wm_ref[h], vn[rs]) + bs_ref[h]
                cat_ref[rs, h * HEAD:(h + 1) * HEAD] = (u[rs] * z).astype(BF16)
        for g in range(len(POOL_WINDOWS)):
            gs = slice(g * HEAD, (g + 1) * HEAD)
            p = proj[:, 2 * A_WIDTH + g * HEAD:2 * A_WIDTH + (g + 1) * HEAD]
            pooled = _pool_fwd(p, carry_ref[:, gs], g, t_glob)
            yb = _dot(pooled.astype(BF16), wp_ref[g]) * ps_ref[0:1, gs]
            cat_ref[:, A_WIDTH + g * HEAD:A_WIDTH + (g + 1) * HEAD] = yb.astype(BF16)
        carry_ref[...] = proj[ts - POOL_HALO:ts, 2 * A_WIDTH:]
        mixed = _dot(cat_ref[...], wout_ref[...])
        mixed_ref[...] = mixed
        x2 = x + _vrow(vec_ref, V_G_M) * ((mixed * _rms(mixed)) * _vrow(vec_ref, V_POST_MIX))
        x2_ref[...] = x2
        h2 = (((x2 * _rms(x2)) * _vrow(vec_ref, V_PRE_FFN)) * (1.0 + _vrow(vec_ref, V_SC_F))
              + _vrow(vec_ref, V_SH_F)).astype(BF16)
        for b in range(ts // FFN_TS):
            rs = slice(b * FFN_TS, (b + 1) * FFN_TS)
            h2_ref[rs, :] = _permute_bf16(pm_ref[0], h2[rs])

    row = lambda i: (i, 0)
    return _call(
        body, name="mix_fwd", grid=(nt,),
        in_specs=[_tiled((ts, d), row), _whole(vecs.shape), _resident(w_in_b.shape), _resident(w_out_b.shape),
                  _whole(sgu_g.shape), _whole(wm_b.shape), _whole(bsb.shape), _whole(wp_b.shape), _whole(ps.shape),
                  _whole(pmats.shape)],
        out_specs=[_tiled((ts, d), row), _tiled((ts, n_proj), row), _tiled((ts, d), row),
                   _tiled((ts, d), row), _tiled((ts, d), row), _tiled((ts, d), row)],
        out_shape=[jax.ShapeDtypeStruct((s_len, d), BF16), jax.ShapeDtypeStruct((s_len, n_proj), BF16),
                   jax.ShapeDtypeStruct((s_len, d), BF16), jax.ShapeDtypeStruct((s_len, d), F32),
                   jax.ShapeDtypeStruct((s_len, d), F32), jax.ShapeDtypeStruct((s_len, d), BF16)],
        scratch_shapes=[pltpu.VMEM((POOL_HALO, A_WIDTH), F32)],
        args=(x, vecs, w_in_b, w_out_b, sgu_g, wm_b, bsb, wp_b, ps, pmats))


def _perm_mats(ts):
    p = jnp.arange(ts)
    pm = (((p % 8) * (ts // 8) + p // 8)[:, None] == p[None, :]).astype(BF16)
    return jnp.stack([pm, pm.T])


def _permute_bf16(pm, xb):
    return _dot(pm, xb).astype(BF16)


def _permute_f32(pm, x):
    hi = x.astype(BF16)
    lo = (x - hi.astype(F32)).astype(BF16)
    return _dot(pm, hi) + _dot(pm, lo)


def _conv_out(u, um2, um1, cv_ref, cols):
    return (cv_ref[3:4, cols] + um2 * cv_ref[0:1, cols] + um1 * cv_ref[1:2, cols] + u * cv_ref[2:3, cols])


F_LOSS, F_DGF, F_DPOSTFFN = 0, 1, 2
B_DSHF, B_DSCF, B_DPREFFN, B_DGM, B_DPOSTMIX = 0, 1, 2, 3, 4
M_DSHM, M_DSCM, M_DPREMIX = 0, 1, 2
C_DCB, C_DCW = 0, 1


def _ffn_fwd(h2p, x2, tgt, w_up_b, w_down_b, cvec, vecs, pmats, ts):
    s_len, d = x2.shape
    ff2 = w_up_b.shape[1]
    ff = ff2 // 2
    nt = s_len // ts
    chunks = _ff_chunks(ff)

    def body(h2_ref, x2_ref, t_ref, wu_ref, wd_ref, cv_ref, vec_ref, pm_ref,
             up_ref, y_ref, act_ref, dy_ref, df_ref, acc_ref, carry_ref):
        @pl.when(pl.program_id(0) == 0)
        def _():
            carry_ref[...] = jnp.zeros_like(carry_ref)
            acc_ref[...] = jnp.zeros_like(acc_ref)

        h2v = h2_ref[...]

        def up_dots(o, w):
            return [_dot(h2v, wu_ref[:, base + o:base + o + w]) for base in (0, ff)]

        f = None
        pending = None
        nxt = up_dots(*chunks[0])
        for ci, (o, w) in enumerate(chunks):
            us = nxt
            if ci + 1 < len(chunks):
                nxt = up_dots(*chunks[ci + 1])
            if pending is not None:
                part = _dot(pending[0], wd_ref[pending[1]:pending[1] + pending[2], :])
                f = part if f is None else f + part
            sub0 = lax.broadcasted_iota(jnp.int32, (8, w), 0) == 0
            ys = []
            for base, u in zip((0, ff), us):
                cols = slice(base + o, base + o + w)
                up_ref[:, cols] = u.astype(BF16)
                b1 = jnp.where(sub0, pltpu.roll(carry_ref[8:16, cols], 1, 0), pltpu.roll(u[ts - 8:ts], 1, 0))
                b2 = jnp.where(sub0, pltpu.roll(carry_ref[0:8, cols], 1, 0), pltpu.roll(u[ts - 16:ts - 8], 1, 0))
                um1 = jnp.concatenate([b1, u[:ts - 8]], axis=0)
                um2 = jnp.concatenate([b2, b1, u[:ts - 16]], axis=0)
                ys.append(_conv_out(u, um2, um1, cv_ref, cols))
                carry_ref[:, cols] = u[ts - 16:ts]
            gate, val = ys
            sg = _sigmoid(gate)
            gs = gate * sg
            act = (gs * val).astype(BF16)
            act_ref[:, o:o + w] = act
            y_ref[:, o:o + w] = (val * (sg + gs * (1.0 - sg))).astype(BF16)
            y_ref[:, ff + o:ff + o + w] = gs.astype(BF16)
            pending = (act, o, w)
        f = f + _dot(pending[0], wd_ref[pending[1]:pending[1] + pending[2], :])
        f = _permute_f32(pm_ref[1], f)
        r3 = _rms(f)
        fhat = f * r3
        post = _vrow(vec_ref, V_POST_FFN)
        g_f = _vrow(vec_ref, V_G_F)
        fn = fhat * post
        e = (x2_ref[...] + g_f * fn) - t_ref[...]
        dy = e * (1.0 / d)
        dy_ref[...] = dy
        dfn = dy * g_f
        acc_ref[F_LOSS:F_LOSS + 1, :] += _colsum(e * e)
        acc_ref[F_DGF:F_DGF + 1, :] += _colsum(dy * fn)
        acc_ref[F_DPOSTFFN:F_DPOSTFFN + 1, :] += _colsum(dfn * fhat)
        dfhat = dfn * post
        df = (r3 * (dfhat - fhat * _rowmean(dfhat * fhat))).astype(BF16)
        df_ref[...] = _permute_bf16(pm_ref[0], df)

    row = lambda i: (i, 0)
    return pl.pallas_call(
        body, name="ffn_fwd", grid=(nt,),
        in_specs=[_tiled((ts, d), row), _tiled((ts, d), row), _tiled((ts, d), row), _resident(w_up_b.shape),
                  _resident(w_down_b.shape), _whole(cvec.shape), _whole(vecs.shape), _whole(pmats.shape)],
        out_specs=[_tiled((ts, ff2), row), _tiled((ts, ff2), row), _tiled((ts, ff), row), _tiled((ts, d), row),
                   _tiled((ts, d), row), _whole((8, d))],
        out_shape=[jax.ShapeDtypeStruct((s_len, ff2), BF16), jax.ShapeDtypeStruct((s_len, ff2), BF16),
                   jax.ShapeDtypeStruct((s_len, ff), BF16), jax.ShapeDtypeStruct((s_len, d), F32),
                   jax.ShapeDtypeStruct((s_len, d), BF16), jax.ShapeDtypeStruct((8, d), F32)],
        scratch_shapes=[pltpu.VMEM((16, ff2), F32)],
        compiler_params=_seq_params(),
    )(h2p, x2, tgt, w_up_b, w_down_b, cvec, vecs, pmats)


def _ffn_bwd(dfp, upp, yp, x2, dy, mixed, w_up_b, w_down_b, cvec, vecs, pmats, ts, rider=None):
    s_len, d = x2.shape
    ff2 = w_up_b.shape[1]
    ff = ff2 // 2
    nt = s_len // ts
    chunks = _ff_chunks(ff, 512)

    def body(df_ref, up_ref, y_ref, x2_ref, dy_ref, mx_ref, wu_ref, wd_ref, cv_ref, vec_ref, pm_ref,
             dup_ref, dx2_ref, dmx_ref, accc_ref, acc_ref, carry_ref):
        @pl.when(pl.program_id(0) == 0)
        def _():
            carry_ref[...] = jnp.zeros_like(carry_ref)
            accc_ref[...] = jnp.zeros_like(accc_ref)
            acc_ref[...] = jnp.zeros_like(acc_ref)

        dfv = df_ref[...]

        def dh2_add(acc, dups, o, w):
            for base, dup in zip((0, ff), dups):
                part = _dot_nt(dup, wu_ref[:, base + o:base + o + w])
                acc = part if acc is None else acc + part
            return acc

        dh2 = None
        pending = None
        nxt = _dot_nt(dfv, wd_ref[chunks[0][0]:chunks[0][0] + chunks[0][1], :])
        for ci, (o, w) in enumerate(chunks):
            dact = nxt
            if ci + 1 < len(chunks):
                o2, w2 = chunks[ci + 1]
                nxt = _dot_nt(dfv, wd_ref[o2:o2 + w2, :])
            if pending is not None:
                dh2 = dh2_add(dh2, *pending)
            sub7 = lax.broadcasted_iota(jnp.int32, (8, w), 0) == 7
            dups = []
            dys = (dact * y_ref[:, o:o + w].astype(F32), dact * y_ref[:, ff + o:ff + o + w].astype(F32))
            for base, dyv in zip((0, ff), dys):
                cols = slice(base + o, base + o + w)
                u = up_ref[:, cols].astype(F32)
                e0 = jnp.where(sub7, pltpu.roll(carry_ref[0:8, cols], 7, 0), pltpu.roll(dyv[0:8], 7, 0))
                e1 = jnp.where(sub7, pltpu.roll(carry_ref[8:16, cols], 7, 0), pltpu.roll(dyv[8:16], 7, 0))
                dyp1 = jnp.concatenate([dyv[8:], e0], axis=0)
                dyp2 = jnp.concatenate([dyv[16:], e0, e1], axis=0)
                accc_ref[C_DCB:C_DCB + 1, cols] += _colsum(dyv)
                accc_ref[C_DCW + 0:C_DCW + 1, cols] += _colsum(dyp2 * u)
                accc_ref[C_DCW + 1:C_DCW + 2, cols] += _colsum(dyp1 * u)
                accc_ref[C_DCW + 2:C_DCW + 3, cols] += _colsum(dyv * u)
                dup = (dyv * cv_ref[2:3, cols] + dyp1 * cv_ref[1:2, cols] + dyp2 * cv_ref[0:1, cols]).astype(BF16)
                dup_ref[:, cols] = dup
                dups.append(dup)
                carry_ref[:, cols] = dyv[0:16]
            pending = (dups, o, w)
        dh2 = dh2_add(dh2, *pending)
        dh2 = _permute_f32(pm_ref[1], dh2)
        x2 = x2_ref[...]
        r2 = _rms(x2)
        xn = x2 * r2
        pre = _vrow(vec_ref, V_PRE_FFN)
        one_sc = 1.0 + _vrow(vec_ref, V_SC_F)
        acc_ref[B_DSHF:B_DSHF + 1, :] += _colsum(dh2)
        acc_ref[B_DSCF:B_DSCF + 1, :] += _colsum(dh2 * (xn * pre))
        acc_ref[B_DPREFFN:B_DPREFFN + 1, :] += _colsum(dh2 * xn * one_sc)
        dxn = dh2 * pre * one_sc
        dx2 = dy_ref[...] + r2 * (dxn - xn * _rowmean(dxn * xn))
        dx2_ref[...] = dx2
        mixed = mx_ref[...]
        rm = _rms(mixed)
        mhat = mixed * rm
        post = _vrow(vec_ref, V_POST_MIX)
        acc_ref[B_DGM:B_DGM + 1, :] += _colsum(dx2 * (mhat * post))
        dmn = dx2 * _vrow(vec_ref, V_G_M)
        acc_ref[B_DPOSTMIX:B_DPOSTMIX + 1, :] += _colsum(dmn * mhat)
        dmhat = dmn * post
        dmx_ref[...] = (rm * (dmhat - mhat * _rowmean(dmhat * mhat))).astype(BF16)

    rev = lambda i: (nt - 1 - i, 0)
    return _call(
        body, name="ffn_bwd", grid=(nt,), rider=rider,
        in_specs=[_tiled((ts, d), rev), _tiled((ts, ff2), rev), _tiled((ts, ff2), rev), _tiled((ts, d), rev),
                  _tiled((ts, d), rev), _tiled((ts, d), rev), _resident(w_up_b.shape), _resident(w_down_b.shape),
                  _whole(cvec.shape), _whole(vecs.shape), _whole(pmats.shape)],
        out_specs=[_tiled((ts, ff2), rev), _tiled((ts, d), rev), _tiled((ts, d), rev), _whole((8, ff2)),
                   _whole((8, d))],
        out_shape=[jax.ShapeDtypeStruct((s_len, ff2), BF16), jax.ShapeDtypeStruct((s_len, d), F32),
                   jax.ShapeDtypeStruct((s_len, d), BF16), jax.ShapeDtypeStruct((8, ff2), F32),
                   jax.ShapeDtypeStruct((8, d), F32)],
        scratch_shapes=[pltpu.VMEM((16, ff2), F32)],
        args=(dfp, upp, yp, x2, dy, mixed, w_up_b, w_down_b, cvec, vecs, pmats))


def _mix_bwd(dmixed, dx2, x, proj, vecs, w_in_b, w_out_b, sgu_g, wm_b, wmt_b, bsb, wp_b, ps, ts, rider=None):
    s_len, d = x.shape
    nt = s_len // ts
    nblk = ts // HEAD
    n_proj = proj.shape[1]
    per = ts // POOL_HALO
    ext_len = ts + POOL_HALO

    def body(dmx_ref, dx2_ref, x_ref, proj_ref, projh_ref, vec_ref, win_ref, wout_ref, sg_ref, wm_ref, wmt_ref,
             bs_ref, wp_ref, ps_ref,
             gx_ref, dproj_ref, acc_ref, dwm_out, dwp_out, db_ref, dg_ref, dps_ref,
             carry_ref, dwm_ref, dwp_ref, dbz_ref):
        i = pl.program_id(0)
        tile = nt - 1 - i

        @pl.when(i == 0)
        def _():
            carry_ref[...] = jnp.zeros_like(carry_ref)
            for r in (acc_ref, dwm_ref, dwp_ref, dbz_ref, dg_ref, dps_ref):
                r[...] = jnp.zeros_like(r)

        dcat = _dot_nt(dmx_ref[...], wout_ref[...])
        t_glob = lax.broadcasted_iota(jnp.int32, (ts, HEAD), 0) + tile * ts
        for h in range(N_HEADS):
            hs = slice(h * HEAD, (h + 1) * HEAD)
            vs = slice(A_WIDTH + h * HEAD, A_WIDTH + (h + 1) * HEAD)
            au = proj_ref[:, hs].astype(F32)
            av = proj_ref[:, vs].astype(F32)
            u = _gelu(au)
            v = _gelu(av)
            rv = _rms(v)
            vhat = v * rv
            gain = sg_ref[h:h + 1, :]
            vn = (vhat * gain).astype(BF16)
            dout = dcat[:, hs]
            du_parts, dvn_parts = [], []
            for b in range(nblk):
                rs = slice(b * HEAD, (b + 1) * HEAD)
                z = _dot(wm_ref[h], vn[rs]) + bs_ref[h]
                du_parts.append(dout[rs] * z)
                dz = dout[rs] * u[rs]
                dbz_ref[h] += dz
                dzb = dz.astype(BF16)
                dwm_ref[h] += _dot_nt(dzb, vn[rs])
                dvn_parts.append(_dot(wmt_ref[h], dzb))
            du = jnp.concatenate(du_parts, axis=0)
            dvn = jnp.concatenate(dvn_parts, axis=0)
            dg_ref[h:h + 1, :] += _colsum(dvn * vhat)
            dvhat = dvn * gain
            dv = rv * (dvhat - vhat * _rowmean(dvhat * vhat))
            dproj_ref[:, hs] = (du * _gelu_grad(au)).astype(BF16)
            dproj_ref[:, vs] = (dv * _gelu_grad(av)).astype(BF16)
        for g in range(len(POOL_WINDOWS)):
            gs = slice(g * HEAD, (g + 1) * HEAD)
            pcols = slice(2 * A_WIDTH + g * HEAD, 2 * A_WIDTH + (g + 1) * HEAD)
            p = proj_ref[:, pcols].astype(F32)
            halo = jnp.where(tile > 0, projh_ref[:, pcols].astype(F32), 0.0)
            pb = _pool_fwd(p, halo, g, t_glob).astype(BF16)
            dyb = dcat[:, A_WIDTH + g * HEAD:A_WIDTH + (g + 1) * HEAD]
            dps_ref[0:1, gs] += _colsum(dyb * _dot(pb, wp_ref[g]))
            dyl = (dyb * ps_ref[0:1, gs]).astype(BF16)
            dwp_ref[g] += _dot_tn(pb, dyl)
            dpooled = _dot_nt(dyl, wp_ref[g])
            cnt = jnp.minimum(t_glob + 1, POOL_WINDOWS[g]).astype(F32)
            q = dpooled / cnt
            s = jnp.concatenate([q, carry_ref[:, gs]], axis=0)
            for step in range(g + 1):
                s = s + pltpu.roll(s, ext_len - (1 << step), 0)
            dproj_ref[:, pcols] = (s[:ts] - dpooled).astype(BF16)
            carry_ref[:, gs] = q[0:POOL_HALO]
        dh1 = _dot_nt(dproj_ref[...], win_ref[...])
        x = x_ref[...]
        r1 = _rms(x)
        xn = x * r1
        pre = _vrow(vec_ref, V_PRE_MIX)
        one_sc = 1.0 + _vrow(vec_ref, V_SC_M)
        acc_ref[M_DSHM:M_DSHM + 1, :] += _colsum(dh1)
        acc_ref[M_DSCM:M_DSCM + 1, :] += _colsum(dh1 * (xn * pre))
        acc_ref[M_DPREMIX:M_DPREMIX + 1, :] += _colsum(dh1 * xn * one_sc)
        dxn = dh1 * pre * one_sc
        gx_ref[...] = dx2_ref[...] + r1 * (dxn - xn * _rowmean(dxn * xn))

        @pl.when(i == nt - 1)
        def _():
            dwm_out[...] = dwm_ref[...].astype(BF16)
            dwp_out[...] = dwp_ref[...].astype(BF16)
            db_ref[...] = jnp.zeros_like(db_ref)
            for h in range(N_HEADS):
                db_ref[h:h + 1, :] = jnp.sum(dbz_ref[h].T, axis=0, keepdims=True)

    rev = lambda i: (nt - 1 - i, 0)
    halo_map = lambda i: (jnp.maximum((nt - 1 - i) * per - 1, 0), 0)
    hshape = (N_HEADS, HEAD, HEAD)
    return _call(
        body, name="mix_bwd", grid=(nt,), rider=rider,
        in_specs=[_tiled((ts, d), rev), _tiled((ts, d), rev), _tiled((ts, d), rev), _tiled((ts, n_proj), rev),
                  _tiled((POOL_HALO, n_proj), halo_map), _whole(vecs.shape), _resident(w_in_b.shape),
                  _resident(w_out_b.shape), _whole(sgu_g.shape), _whole(wm_b.shape), _whole(wmt_b.shape),
                  _whole(bsb.shape), _whole(wp_b.shape), _whole(ps.shape)],
        out_specs=[_tiled((ts, d), rev), _tiled((ts, n_proj), rev), _whole((8, d)), _whole(hshape), _whole(hshape),
                   _whole((8, HEAD)), _whole((8, HEAD)), _whole((8, A_WIDTH))],
        out_shape=[jax.ShapeDtypeStruct((s_len, d), F32), jax.ShapeDtypeStruct((s_len, n_proj), BF16),
                   jax.ShapeDtypeStruct((8, d), F32), jax.ShapeDtypeStruct(hshape, BF16),
                   jax.ShapeDtypeStruct(hshape, BF16), jax.ShapeDtypeStruct((8, HEAD), F32),
                   jax.ShapeDtypeStruct((8, HEAD), F32), jax.ShapeDtypeStruct((8, A_WIDTH), F32)],
        scratch_shapes=[pltpu.VMEM((POOL_HALO, A_WIDTH), F32), pltpu.VMEM(hshape, F32), pltpu.VMEM(hshape, F32),
                        pltpu.VMEM(hshape, F32)],
        args=(dmixed, dx2, x, proj, proj, vecs, w_in_b, w_out_b, sgu_g, wm_b, wmt_b, bsb, wp_b, ps))


def _wgrad(a, b, tn, ts, name, rider=None):
    s_len, m = a.shape
    n = b.shape[1]
    ts = min(ts, s_len)

    def body(a_ref, b_ref, o_ref):
        @pl.when(pl.program_id(1) == 0)
        def _():
            o_ref[...] = jnp.zeros_like(o_ref)

        o_ref[...] += _dot_tn(a_ref[...], b_ref[...])

    (g,), r_out = _call(
        body, name=name, grid=(n // tn, s_len // ts), rider=rider,
        in_specs=[pl.BlockSpec((ts, m), lambda j, s: (s, 0)), pl.BlockSpec((ts, tn), lambda j, s: (s, j))],
        out_specs=[pl.BlockSpec((m, tn), lambda j, s: (0, j))],
        out_shape=[jax.ShapeDtypeStruct((m, n), F32)], scratch_shapes=[], args=(a, b))
    return g, r_out


def _wgrad_pair(a1, b1, a2, b2, ts, name, rider=None):
    s_len = a1.shape[0]
    ts = min(ts, s_len)
    shapes = [(a1.shape[1], b1.shape[1]), (a2.shape[1], b2.shape[1])]

    def body(a1_ref, b1_ref, a2_ref, b2_ref, o1_ref, o2_ref):
        @pl.when(pl.program_id(0) == 0)
        def _():
            o1_ref[...] = jnp.zeros_like(o1_ref)
            o2_ref[...] = jnp.zeros_like(o2_ref)

        o1_ref[...] += _dot_tn(a1_ref[...], b1_ref[...])
        o2_ref[...] += _dot_tn(a2_ref[...], b2_ref[...])

    row = lambda s: (s, 0)
    return _call(
        body, name=name, grid=(s_len // ts,), rider=rider,
        in_specs=[pl.BlockSpec((ts, t.shape[1]), row) for t in (a1, b1, a2, b2)],
        out_specs=[_whole(sh) for sh in shapes],
        out_shape=[jax.ShapeDtypeStruct(sh, F32) for sh in shapes], scratch_shapes=[], args=(a1, b1, a2, b2))


def _adamw_big(g, w, m, v, name):
    r, cdim = g.shape
    tr = r
    while tr * cdim * 4 > (3 << 19) and tr % 16 == 0:
        tr //= 2

    def body(g_ref, w_ref, m_ref, v_ref, d_ref, nm_ref, nv_ref):
        delta, m2, v2 = _adamw_math(w_ref[0], g_ref[...], m_ref[0], v_ref[0])
        d_ref[0] = delta
        nm_ref[0] = m2
        nv_ref[0] = v2

    s3 = pl.BlockSpec((1, tr, cdim), lambda i: (0, i, 0))
    return pl.pallas_call(
        body, name=name, grid=(r // tr,),
        in_specs=[pl.BlockSpec((tr, cdim), lambda i: (i, 0)), s3, s3, s3],
        out_specs=[s3, s3, s3],
        out_shape=[jax.ShapeDtypeStruct(w.shape, F32)] * 3,
        compiler_params=pltpu.CompilerParams(dimension_semantics=("parallel",), vmem_limit_bytes=VMEM_LIMIT_BYTES),
    )(g, w, m, v)


def _wada_update(sct, gm, w, m, v):
    _, r, cdim = w.shape
    tr = 256
    kp = sct.shape[1]

    def body(s_ref, g_ref, w_ref, m_ref, v_ref, gw_ref, d_ref, nm_ref, nv_ref):
        g = _dot(s_ref[...], g_ref[...])
        gw_ref[0] = g
        delta, m2, v2 = _adamw_math(w_ref[0], g, m_ref[0], v_ref[0])
        d_ref[0] = delta
        nm_ref[0] = m2
        nv_ref[0] = v2

    s3 = pl.BlockSpec((1, tr, cdim), lambda i: (0, i, 0))
    return pl.pallas_call(
        body, name="wada_update", grid=(r // tr,),
        in_specs=[pl.BlockSpec((tr, kp), lambda i: (i, 0)), _whole(gm.shape), s3, s3, s3],
        out_specs=[s3, s3, s3, s3],
        out_shape=[jax.ShapeDtypeStruct(w.shape, F32)] * 4,
        compiler_params=pltpu.CompilerParams(dimension_semantics=("parallel",), vmem_limit_bytes=VMEM_LIMIT_BYTES),
    )(sct, gm, w, m, v)


def _small_update(g1, g2, g2s, gwm, gwp, gbz, gsg, gps, params):
    names = ["b_ada", "pre_mix_g", "post_mix_g", "sgu_norm_g", "w_spatial", "b_spatial", "w_pool", "pool_scale",
             "pre_ffn_g", "post_ffn_g", "conv_w", "conv_b"]
    d = g1.shape[2]
    flat_in = [g1, g2, g2s, gwm, gwp, gbz, gsg, gps]
    n_g = len(flat_in)
    for nm in names:
        flat_in += list(params[nm])

    def body(*refs):
        g1_ref, g2_ref, g2s_ref, gwm_ref, gwp_ref, gbz_ref, gsg_ref, gps_ref = refs[:n_g]
        wmv = refs[n_g:n_g + 3 * len(names)]
        loss_ref = refs[n_g + 3 * len(names)]
        outs = refs[n_g + 3 * len(names) + 1:]

        def dsum(ref, idx):
            acc = ref[(0,) + idx].astype(F32)
            for dev in range(1, N_DEV):
                acc = acc + ref[(dev,) + idx].astype(F32)
            return acc

        def apply(pi, g, widx, oidx):
            w_ref, m_ref, v_ref = wmv[3 * pi:3 * pi + 3]
            g_ref, d_ref, nm_ref, nv_ref = outs[4 * pi:4 * pi + 4]
            delta, m2, v2 = _adamw_math(w_ref[widx], g, m_ref[widx], v_ref[widx])
            g_ref[oidx] = g
            d_ref[oidx] = delta
            nm_ref[oidx] = m2
            nv_ref[oidx] = v2

        def row1(base, r):
            return (slice(base + r, base + r + 1), slice(None))

        tot = dsum(g1_ref, row1(0, F_LOSS))
        loss_ref[...] = jnp.zeros(loss_ref.shape, F32) + jnp.sum(tot) * (0.5 / d)
        mod_rows = [row1(16, M_DSHM), row1(16, M_DSCM), row1(8, B_DGM), row1(8, B_DSHF), row1(8, B_DSCF),
                    row1(0, F_DGF)]
        for j, rr in enumerate(mod_rows):
            cs = (slice(None), slice(j * d, (j + 1) * d))
            apply(0, dsum(g1_ref, rr), cs, cs)
        full = (slice(None), slice(None))
        apply(1, dsum(g1_ref, row1(16, M_DPREMIX)), full, full)
        apply(2, dsum(g1_ref, row1(8, B_DPOSTMIX)), full, full)
        apply(3, dsum(gsg_ref, (slice(0, N_HEADS), slice(None))), (0,), (0,))
        pos_i = lax.broadcasted_iota(jnp.int32, (HEAD, HEAD), 0)
        pos_j = lax.broadcasted_iota(jnp.int32, (HEAD, HEAD), 1)
        causal = (pos_j // CHUNK) <= (pos_i // CHUNK)
        for h in range(N_HEADS):
            blk = (slice(h * HEAD, (h + 1) * HEAD), slice(None))
            apply(4, jnp.where(causal, dsum(gwm_ref, blk), 0.0), (0, h), (0, h))
            apply(5, dsum(gbz_ref, (slice(h, h + 1), slice(None))), (0, slice(h, h + 1)), (0, slice(h, h + 1)))
            apply(6, dsum(gwp_ref, blk), (0, h), (0, h))
        apply(7, dsum(gps_ref, (slice(0, 1), slice(None))), full, full)
        apply(8, dsum(g1_ref, row1(8, B_DPREFFN)), full, full)
        apply(9, dsum(g1_ref, row1(0, F_DPOSTFFN)), full, full)
        apply(10, dsum(g2s_ref, (slice(C_DCW, C_DCW + 3), slice(None))), (0,), (0,))
        apply(11, dsum(g2_ref, (slice(C_DCB, C_DCB + 1), slice(None))), full, full)

    out_shape = [jax.ShapeDtypeStruct((8, HEAD), F32)]
    for nm in names:
        out_shape += [jax.ShapeDtypeStruct(params[nm][0].shape, F32)] * 4
    res = pl.pallas_call(
        body, name="small_update", out_shape=out_shape,
        compiler_params=pltpu.CompilerParams(vmem_limit_bytes=VMEM_LIMIT_BYTES),
    )(*flat_in)
    out = {nm: tuple(res[1 + 4 * i:5 + 4 * i]) for i, nm in enumerate(names)}
    return res[0], out


def kernel(x, c, w_ada, b_ada, pre_mix_g, post_mix_g, w_in, sgu_norm_g, w_spatial, b_spatial, w_pool, pool_scale, w_out, pre_ffn_g, post_ffn_g, w_up, conv_w, conv_b, w_down, loss_target, m_w_ada, m_b_ada, m_pre_mix_g, m_post_mix_g, m_w_in, m_sgu_norm_g, m_w_spatial, m_b_spatial, m_w_pool, m_pool_scale, m_w_out, m_pre_ffn_g, m_post_ffn_g, m_w_up, m_conv_w, m_conv_b, m_w_down, v_w_ada, v_b_ada, v_pre_mix_g, v_post_mix_g, v_w_in, v_sgu_norm_g, v_w_spatial, v_b_spatial, v_w_pool, v_pool_scale, v_w_out, v_pre_ffn_g, v_post_ffn_g, v_w_up, v_conv_w, v_conv_b, v_w_down):
    xi, yi, ci = _mesh_pos()
    k_me = 2 * xi + yi
    dev = 2 * k_me + ci
    s_len, d = x.shape[1], x.shape[2]
    x2d = x[0]
    tgt = loss_target[0]
    ff2 = conv_b.shape[1]
    n_ada = w_ada.shape[2]
    n_cw = conv_w.shape[2]

    k_idx = k_me.reshape(1).astype(jnp.int32)
    flags4 = (True, False, True, False)
    (w_in_s, w_out_s, w_up_s, w_down_s), (w_in_f, w_out_f, w_up_f, w_down_f) = _cast_bf16(
        [w_in[0], w_out[0], w_up[0], w_down[0]], flags4, k_idx)
    mix_flags = (True, False)
    fly_mix = _gather_begin([w_in_s, w_out_s], [w_in_f, w_out_f], mix_flags, w_in_s, "gather_begin_mix")

    cw_blk = jnp.concatenate([conv_w[0], jnp.zeros((5, n_cw), F32)], axis=0)
    c_blk = c.reshape(8, d // 8) + fly_mix[3][0:1, 0:1]
    c_all, cw_all = _run_rider(_allgather_rider([c_blk, cw_blk]), "gather_c_convw")
    c_all = c_all.reshape(N_DEV, 8, d // 8).reshape(N_DEV, d)
    cw_full = jnp.concatenate([cw_all[16 * k:16 * k + 8] for k in range(N_CHIP)], axis=1)
    cvec = jnp.concatenate([cw_full[0:3], conv_b, jnp.zeros((4, ff2), F32)], axis=0)
    b_shard = lax.dynamic_slice_in_dim(b_ada, k_me * n_ada, n_ada, axis=1)
    mod_k, sc_all = _mod_shard(c_all, w_ada[0], b_shard)
    (mod_g,) = _run_rider(_allgather_rider([mod_k]), "gather_mod")
    mod_all = jnp.concatenate([mod_g[16 * k:16 * k + 8] for k in range(N_CHIP)], axis=1)
    mod_me = lax.dynamic_slice_in_dim(mod_all, dev, 1, axis=0).reshape(6, d)
    vecs = jnp.concatenate([mod_me, pre_mix_g, post_mix_g, pre_ffn_g, post_ffn_g, jnp.zeros((6, d), F32)], axis=0)

    fly_ffn = _gather_begin([w_up_s, w_down_s], [w_up_f, w_down_f], mix_flags, mod_g, "gather_begin_ffn")
    w_in_b, w_out_b = _gather_finish(_gather_end(fly_mix, mix_flags, fly_ffn[3], "gather_end_mix"), mix_flags,
                                     "gather_finish_mix")

    pos = jnp.arange(HEAD)
    causal = (pos[None, :] // CHUNK) <= (pos[:, None] // CHUNK)
    wm = jnp.where(causal[None], w_spatial[0], 0.0)
    wm_b = wm.astype(BF16)
    wmt_b = jnp.swapaxes(wm, 1, 2).astype(BF16)
    bsb = jnp.broadcast_to(b_spatial[0][:, :, None], (N_HEADS, HEAD, HEAD))
    wp_b = w_pool[0].astype(BF16)
    sgu_g = jnp.concatenate([sgu_norm_g[0], jnp.zeros((4, HEAD), F32)], axis=0)
    ps = jnp.concatenate([pool_scale, jnp.zeros((7, A_WIDTH), F32)], axis=0)

    pmats = _perm_mats(FFN_TS)
    h1, proj, cat, mixed, x2, h2p = _mix_fwd(x2d, vecs, w_in_b, w_out_b, sgu_g, wm_b, bsb, wp_b, ps, pmats, ts=512)[0]
    w_up_b, w_down_b = _gather_finish(_gather_end(fly_ffn, mix_flags, h2p, "gather_end_ffn"), mix_flags,
                                      "gather_finish_ffn")
    up, yv, act, dy, dfp, acc_f = _ffn_fwd(h2p, x2, tgt, w_up_b, w_down_b, cvec, vecs, pmats, ts=FFN_TS)

    c_idx = ci.reshape(1).astype(jnp.int32)
    g_w_down, _ = _wgrad(act, dfp, d, 1024, "wgrad_down")
    (dup, dx2, dmixed, acc_c, acc_b), (land_down,) = _ffn_bwd(
        dfp, up, yv, x2, dy, mixed, w_up_b, w_down_b, cvec, vecs, pmats, ts=FFN_TS,
        rider=_sibling_rider([g_w_down], (False,)))
    (part_down,) = _sum_with_sibling([g_w_down], [land_down], (False,), c_idx, "pair_sum_down")
    g_w_up, (chips_down,) = _wgrad(h2p, dup, ff2 // 2, 2048, "wgrad_up", rider=_chips_rider([part_down]))
    fly_up = _sibling_begin(g_w_up, True, g_w_up, "sibling_begin_up")
    gx, dproj, acc_m, dwm, dwp, dbz, dsg, dps = _mix_bwd(
        dmixed, dx2, x2d, proj, vecs + fly_up[3][0:1, 0:1], w_in_b, w_out_b, sgu_g, wm_b, wmt_b, bsb, wp_b, ps,
        ts=256)[0]
    g_w_up, land_up = _sibling_end(fly_up, True, dproj, "sibling_end_up")
    (part_up,) = _sum_with_sibling([g_w_up], [land_up], (True,), c_idx, "pair_sum_up")
    (g_w_out, g_w_in), (chips_up,) = _wgrad_pair(cat, dmixed, h1, dproj, 1024, "wgrad_mix",
                                                 rider=_chips_rider([part_up]))
    land_mix = _run_rider(_sibling_rider([g_w_in, g_w_out], (True, False)), "reduce_to_sibling")
    parts_mix = _sum_with_sibling([g_w_in, g_w_out], land_mix, (True, False), c_idx, "pair_sum_mix")
    chips_in, chips_out = _run_rider(_chips_rider(parts_mix), "reduce_over_chips")
    reduced = _sum_chips_and_share([chips_in, chips_out, chips_up, chips_down],
                                   [parts_mix[0], parts_mix[1], part_up, part_down])
    big = {}
    for nm, red, (w, m, v) in zip(("w_in", "w_out", "w_up", "w_down"), reduced,
                                  ((w_in, m_w_in, v_w_in), (w_out, m_w_out, v_w_out),
                                   (w_up, m_w_up, v_w_up), (w_down, m_w_down, v_w_down))):
        g = red.reshape(w.shape[1], w.shape[2])
        big[nm] = (g.reshape(w.shape),) + tuple(_adamw_big(g, w, m, v, "adamw_" + nm))

    g1 = jnp.concatenate([acc_f, acc_b, acc_m], axis=0)
    hflat = (N_HEADS * HEAD, HEAD)
    gathered = _run_rider(_allgather_rider(
        [g1, acc_c, dwm.reshape(hflat), dwp.reshape(hflat), dbz, dsg, dps]), "gather_small_grads")
    g1a, g2a, gwm, gwp, gbz, gsg, gps = [t.reshape((N_DEV, t.shape[0] // N_DEV, t.shape[1])) for t in gathered]
    g2s = lax.dynamic_slice_in_dim(g2a, k_me * n_cw, n_cw, axis=2)
    params = {
        "b_ada": (b_ada, m_b_ada, v_b_ada), "pre_mix_g": (pre_mix_g, m_pre_mix_g, v_pre_mix_g),
        "post_mix_g": (post_mix_g, m_post_mix_g, v_post_mix_g),
        "sgu_norm_g": (sgu_norm_g, m_sgu_norm_g, v_sgu_norm_g), "w_spatial": (w_spatial, m_w_spatial, v_w_spatial),
        "b_spatial": (b_spatial, m_b_spatial, v_b_spatial), "w_pool": (w_pool, m_w_pool, v_w_pool),
        "pool_scale": (pool_scale, m_pool_scale, v_pool_scale), "pre_ffn_g": (pre_ffn_g, m_pre_ffn_g, v_pre_ffn_g),
        "post_ffn_g": (post_ffn_g, m_post_ffn_g, v_post_ffn_g), "conv_w": (conv_w, m_conv_w, v_conv_w),
        "conv_b": (conv_b, m_conv_b, v_conv_b),
    }
    loss_slab, small = _small_update(g1a, g2a, g2s, gwm, gwp, gbz, gsg, gps, params)

    gmod_all = jnp.concatenate(
        [g1a[:, 16 + M_DSHM], g1a[:, 16 + M_DSCM], g1a[:, 8 + B_DGM], g1a[:, 8 + B_DSHF], g1a[:, 8 + B_DSCF],
         g1a[:, F_DGF]], axis=1)
    gm = lax.dynamic_slice_in_dim(gmod_all, k_me * n_ada, n_ada, axis=1)
    gm = jnp.concatenate([gm, jnp.zeros((HEAD - N_DEV, n_ada), F32)], axis=0)
    sct = jnp.concatenate([sc_all.T, jnp.zeros((d, HEAD - N_DEV), F32)], axis=1)
    ada = tuple(_wada_update(sct, gm, w_ada, m_w_ada, v_w_ada))

    everything = dict(small)
    everything.update(big)
    everything["w_ada"] = ada
    order = ["w_ada", "b_ada", "pre_mix_g", "post_mix_g", "w_in", "sgu_norm_g", "w_spatial", "b_spatial", "w_pool",
             "pool_scale", "w_out", "pre_ffn_g", "post_ffn_g", "w_up", "conv_w", "conv_b", "w_down"]
    outs = [loss_slab[0, 0], gx.reshape(x.shape)]
    for j in range(4):
        outs += [everything[nm][j] for nm in order]
    return tuple(outs)
```

```python
import functools

import jax
import jax.numpy as jnp
from jax import lax
from jax.experimental import pallas as pl
from jax.experimental.pallas import tpu as pltpu

F32 = jnp.float32
BF16 = jnp.bfloat16
MESH = pl.DeviceIdType.MESH

EPS = 1e-6
HEAD = 128
N_HEADS = 4
A_WIDTH = N_HEADS * HEAD
CHUNK = 64
POOL_WINDOWS = (2, 4, 8, 16)
POOL_HALO = 16
FFN_TS = 256

ADAM_LR = 0.001
ADAM_B1 = 0.9
ADAM_B2 = 0.999
ADAM_EPS = 1e-08
ADAM_WD = 0.01
ADAM_STEP = 10

VMEM_LIMIT_BYTES = 58 * 1024 * 1024
N_DEV = 8
N_CHIP = 4


def _dot(a, b):
    return jnp.dot(a, b, preferred_element_type=F32)


def _dot_nt(a, b):
    return lax.dot_general(a, b, (((1,), (1,)), ((), ())), preferred_element_type=F32)


def _dot_tn(a, b):
    return lax.dot_general(a, b, (((0,), (0,)), ((), ())), preferred_element_type=F32)


def _gelu(x):
    return x * (0.5 * (1.0 + jnp.tanh(0.7978845608028654 * (x + 0.044715 * (x * x * x)))))


def _gelu_grad(x):
    t = jnp.tanh(0.7978845608028654 * (x + 0.044715 * (x * x * x)))
    return 0.5 * (1.0 + t) + (0.5 * x) * (1.0 - t * t) * (0.7978845608028654 * (1.0 + 0.134145 * (x * x)))


def _sigmoid(x):
    return 1.0 / (1.0 + jnp.exp(-x))


def _rms(x):
    return lax.rsqrt(jnp.mean(x * x, axis=-1, keepdims=True) + EPS)


def _colsum(x):
    return jnp.sum(x, axis=0, keepdims=True)


def _rowmean(x):
    return jnp.mean(x, axis=-1, keepdims=True)


def _tiled(shape, index_map):
    return pl.BlockSpec(shape, index_map)


def _resident(shape):
    nd = len(shape)
    return pl.BlockSpec(shape, lambda *_: (0,) * nd, pipeline_mode=pl.Buffered(1))


def _whole(shape):
    nd = len(shape)
    return pl.BlockSpec(shape, lambda *_: (0,) * nd)


def _seq_params():
    return pltpu.CompilerParams(dimension_semantics=("arbitrary",), vmem_limit_bytes=VMEM_LIMIT_BYTES)


def _ff_chunks(f, width=768):
    out, o = [], 0
    while o < f:
        w = min(width, f - o)
        out.append((o, w))
        o += w
    return out


def _pool_fwd(p, halo, g, t_glob):
    ext = jnp.concatenate([halo, p], axis=0)
    s = ext
    for step in range(g + 1):
        s = s + pltpu.roll(s, 1 << step, 0)
    cnt = jnp.minimum(t_glob + 1, POOL_WINDOWS[g]).astype(F32)
    return s[POOL_HALO:] / cnt - p


def _adamw_math(w, g, m, v):
    m = ADAM_B1 * m + (1.0 - ADAM_B1) * g
    v = ADAM_B2 * v + (1.0 - ADAM_B2) * (g * g)
    m_hat = m / (1.0 - ADAM_B1 ** ADAM_STEP)
    v_hat = v / (1.0 - ADAM_B2 ** ADAM_STEP)
    delta = -ADAM_LR * (m_hat / (jnp.sqrt(v_hat) + ADAM_EPS) + ADAM_WD * w)
    return delta, m, v


def _mesh_pos():
    return lax.axis_index("x"), lax.axis_index("y"), lax.axis_index("c")


class _Rider:
    def __init__(self, inputs, out_shape, sems, start, finish):
        self.inputs, self.out_shape, self.sems = list(inputs), list(out_shape), list(sems)
        self.start, self.finish = start, finish


def _call(body, *, name, grid, in_specs, out_specs, out_shape, scratch_shapes, args, rider=None):
    params = pltpu.CompilerParams(dimension_semantics=("arbitrary",) * len(grid), vmem_limit_bytes=VMEM_LIMIT_BYTES)
    if rider is None:
        res = pl.pallas_call(body, name=name, grid=grid, in_specs=in_specs, out_specs=out_specs, out_shape=out_shape,
                             scratch_shapes=scratch_shapes, compiler_params=params)(*args)
        return tuple(res), ()
    cuts = [len(in_specs), len(rider.inputs), len(out_specs), len(rider.out_shape), len(scratch_shapes),
            len(rider.sems)]

    def hosted(*refs):
        groups, a = [], 0
        for cnt in cuts:
            groups.append(refs[a:a + cnt])
            a += cnt
        ins, r_in, outs, r_out, scr, r_sem = groups
        first = functools.reduce(jnp.logical_and, [pl.program_id(k) == 0 for k in range(len(grid))])
        last = functools.reduce(jnp.logical_and, [pl.program_id(k) == grid[k] - 1 for k in range(len(grid))])

        @pl.when(first)
        def _():
            rider.start(r_in, r_out, r_sem)

        body(*ins, *outs, *scr)

        @pl.when(last)
        def _():
            rider.finish(r_in, r_out, r_sem)

    anyspec = pl.BlockSpec(memory_space=pl.ANY)
    res = pl.pallas_call(
        hosted, name=name, grid=grid,
        in_specs=list(in_specs) + [anyspec] * cuts[1], out_specs=list(out_specs) + [anyspec] * cuts[3],
        out_shape=list(out_shape) + rider.out_shape, scratch_shapes=list(scratch_shapes) + rider.sems,
        compiler_params=params)(*args, *rider.inputs)
    return tuple(res[:cuts[2]]), tuple(res[cuts[2]:])


def _run_rider(rider, name):
    n_in, n_out = len(rider.inputs), len(rider.out_shape)

    def body(*refs):
        r_in, r_out, r_sem = refs[:n_in], refs[n_in:n_in + n_out], refs[n_in + n_out:]
        rider.start(r_in, r_out, r_sem)
        rider.finish(r_in, r_out, r_sem)

    anyspec = pl.BlockSpec(memory_space=pl.ANY)
    return pl.pallas_call(body, name=name, out_shape=rider.out_shape, in_specs=[anyspec] * n_in,
                          out_specs=[anyspec] * n_out, scratch_shapes=rider.sems)(*rider.inputs)


def _allgather_rider(arrs):
    n = len(arrs)

    def plan(ins, outs, sems):
        send_sems, recv_sems, local_sems = sems
        x, y, c = _mesh_pos()
        me, sibling = (x, y, c), (x, y, 1 - c)
        chips = [(1 - x, y), (x, 1 - y), (1 - x, 1 - y)]

        def rows(a, px, py, pc):
            r = ins[a].shape[0]
            return outs[a].at[pl.ds(pl.multiple_of((4 * px + 2 * py + pc) * r, 8), r), :]

        def copy(a, k, block, to, src=None):
            return pltpu.make_async_remote_copy(
                src_ref=rows(a, *block) if src is None else src, dst_ref=rows(a, *block),
                send_sem=send_sems.at[a * 7 + k], recv_sem=recv_sems.at[a * 7 + k],
                device_id=to, device_id_type=MESH)

        local = [pltpu.make_async_copy(ins[a], rows(a, *me), local_sems.at[a]) for a in range(n)]
        first = []
        for a in range(n):
            first.append(copy(a, 0, me, sibling, src=ins[a]))
            first += [copy(a, 1 + j, me, (*chip, c), src=ins[a]) for j, chip in enumerate(chips)]
        return c, me, sibling, chips, copy, local, first

    def start(ins, outs, sems):
        *_, local, first = plan(ins, outs, sems)
        for cp in local + first:
            cp.start()

    def finish(ins, outs, sems):
        c, me, sibling, chips, copy, local, first = plan(ins, outs, sems)
        passed = []
        for a in range(n):
            for j, chip in enumerate(chips):
                copy(a, 1 + j, (*chip, c), me).wait_recv()
                fwd = copy(a, 4 + j, (*chip, c), sibling)
                fwd.start()
                passed.append(fwd)
        for a in range(n):
            copy(a, 0, sibling, me).wait_recv()
            for j, chip in enumerate(chips):
                copy(a, 4 + j, (*chip, 1 - c), me).wait_recv()
        for cp in first + passed:
            cp.wait_send()
        for mine in local:
            mine.wait()

    return _Rider(arrs, [jax.ShapeDtypeStruct((N_DEV * a.shape[0], a.shape[1]), a.dtype) for a in arrs],
                  [pltpu.SemaphoreType.DMA((7 * n,)), pltpu.SemaphoreType.DMA((7 * n,)),
                   pltpu.SemaphoreType.DMA((n,))], start, finish)


def _piece(ref, col_sharded, k, h):
    m, n = ref.shape
    if col_sharded:
        mh, nc = m // 2, n // N_CHIP
        return ref.at[pl.ds(pl.multiple_of(h * mh, 16), mh), pl.ds(pl.multiple_of(k * nc, 128), nc)]
    rp = m // (2 * N_CHIP)
    return ref.at[pl.ds(pl.multiple_of((2 * k + h) * rp, 16), rp), :]


def _piece_shape(shape, col_sharded):
    m, n = shape
    return (m // 2, n // N_CHIP) if col_sharded else (m // (2 * N_CHIP), n)


def _cast_bf16(arrs, col_flags, k_idx):
    n = len(arrs)

    def body(k_ref, *refs):
        for a in range(n):
            val = refs[a][...].astype(BF16)
            refs[n + a][...] = val
            refs[2 * n + a][...] = val

    whole = [pl.BlockSpec(a.shape, lambda i, k_ref: (0, 0)) for a in arrs]
    window = [pl.BlockSpec(a.shape, (lambda i, k_ref: (0, k_ref[0])) if col else (lambda i, k_ref: (k_ref[0], 0)))
              for a, col in zip(arrs, col_flags)]
    res = pl.pallas_call(
        body, name="cast_weights",
        grid_spec=pltpu.PrefetchScalarGridSpec(num_scalar_prefetch=1, grid=(1,), in_specs=whole,
                                               out_specs=whole + window),
        out_shape=[jax.ShapeDtypeStruct(a.shape, BF16) for a in arrs]
        + [jax.ShapeDtypeStruct(fs, BF16) for fs in _full_shapes(arrs, col_flags)],
        compiler_params=pltpu.CompilerParams(vmem_limit_bytes=VMEM_LIMIT_BYTES))(k_idx, *arrs)
    return list(res[:n]), list(res[n:])


def _full_shapes(shards, col_flags):
    return [(s.shape[0], s.shape[1] * N_CHIP) if col else (s.shape[0] * N_CHIP, s.shape[1])
            for s, col in zip(shards, col_flags)]


def _ici_copies(shard_refs, full_refs, send_sems, recv_sems, col_flags):
    x, y, c = _mesh_pos()
    k_me = 2 * x + y
    copies = []
    for a, (s_ref, f_ref) in enumerate(zip(shard_refs, full_refs)):
        rows = s_ref.shape[0] // 2
        src = s_ref.at[pl.ds(pl.multiple_of(c * rows, 16), rows), :]
        for j, chip in enumerate([(1 - x, y), (x, 1 - y), (1 - x, 1 - y)]):
            copies.append(pltpu.make_async_remote_copy(
                src_ref=src, dst_ref=_piece(f_ref, col_flags[a], k_me, c),
                send_sem=send_sems.at[a * 3 + j], recv_sem=recv_sems.at[a * 3 + j],
                device_id=(*chip, c), device_id_type=MESH))
    return copies


def _split_begin(bufs, n_sems, make_copies, after, name):
    n = len(bufs)
    hbm = pl.BlockSpec(memory_space=pltpu.HBM)
    sem = pl.BlockSpec(memory_space=pltpu.SEMAPHORE)

    def body(*refs):
        for cp in make_copies(refs[:n], refs[n + 1], refs[n + 2]):
            cp.start()
        refs[-1][...] = jnp.zeros_like(refs[-1])

    args = [pltpu.with_memory_space_constraint(t, pltpu.HBM) for t in bufs]
    res = pl.pallas_call(
        body, name=name,
        out_shape=[pltpu.SemaphoreType.DMA((n_sems,)), pltpu.SemaphoreType.DMA((n_sems,))]
        + [pltpu.HBM(t.shape, t.dtype) for t in args] + [jax.ShapeDtypeStruct((8, HEAD), F32)],
        in_specs=[hbm] * n + [pl.BlockSpec(memory_space=pl.ANY)],
        out_specs=[sem, sem] + [hbm] * n + [pl.BlockSpec(memory_space=pltpu.VMEM)],
        input_output_aliases={i: 2 + i for i in range(n)},
        compiler_params=pltpu.CompilerParams(has_side_effects=pltpu.SideEffectType.DATAFLOW_SIDE_EFFECTING),
    )(*args, after)
    return res[0], res[1], list(res[2:2 + n]), res[-1]


def _split_end(handle, make_copies, after, name):
    send_sems, recv_sems, bufs, _ = handle
    n = len(bufs)
    hbm = pl.BlockSpec(memory_space=pltpu.HBM)
    sem = pl.BlockSpec(memory_space=pltpu.SEMAPHORE)

    def body(*refs):
        for cp in make_copies(refs[:n], refs[n], refs[n + 1]):
            cp.wait_send()
            cp.wait_recv()

    res = pl.pallas_call(
        body, name=name,
        out_shape=[pltpu.HBM(t.shape, t.dtype) for t in bufs],
        in_specs=[hbm] * n + [sem, sem, pl.BlockSpec(memory_space=pl.ANY)],
        out_specs=[hbm] * n,
        input_output_aliases={i: i for i in range(n)},
        compiler_params=pltpu.CompilerParams(has_side_effects=pltpu.SideEffectType.DATAFLOW_SIDE_EFFECTING),
    )(*bufs, send_sems, recv_sems, after)
    return list(res)


def _gather_copies(n, col_flags):
    return lambda refs, send_sems, recv_sems: _ici_copies(refs[:n], refs[n:], send_sems, recv_sems, col_flags)


def _gather_begin(shards, fulls, col_flags, after, name):
    n = len(shards)
    return _split_begin(list(shards) + list(fulls), 3 * n, _gather_copies(n, col_flags), after, name)


def _gather_end(handle, col_flags, after, name):
    n = len(handle[2]) // 2
    return _split_end(handle, _gather_copies(n, col_flags), after, name)[n:]


def _sibling_copies(col_flag):
    def make(refs, send_sems, recv_sems):
        grad_ref, land_ref = refs
        x, y, c = _mesh_pos()
        return [pltpu.make_async_remote_copy(
            src_ref=_piece(grad_ref, col_flag, k, 1 - c), dst_ref=land_ref.at[k],
            send_sem=send_sems.at[k], recv_sem=recv_sems.at[k],
            device_id=(x, y, 1 - c), device_id_type=MESH) for k in range(N_CHIP)]
    return make


def _chips_copies(n):
    def make(refs, send_sems, recv_sems):
        parts, landed = refs[:n], refs[n:]
        x, y, c = _mesh_pos()
        k_me = 2 * x + y
        copies = []
        for a in range(n):
            for j, chip in enumerate([(1 - x, y), (x, 1 - y), (1 - x, 1 - y)]):
                copies.append(pltpu.make_async_remote_copy(
                    src_ref=parts[a].at[2 * chip[0] + chip[1]], dst_ref=landed[a].at[k_me],
                    send_sem=send_sems.at[a * 3 + j], recv_sem=recv_sems.at[a * 3 + j],
                    device_id=(*chip, c), device_id_type=MESH))
        return copies
    return make


def _chips_begin(parts, after, name):
    landed = [lax.empty(p.shape, p.dtype) for p in parts]
    return _split_begin(list(parts) + landed, 3 * len(parts), _chips_copies(len(parts)), after, name)


def _chips_end(handle, after, name):
    n = len(handle[2]) // 2
    res = _split_end(handle, _chips_copies(n), after, name)
    return res[:n], res[n:]


def _small_rows(buf_ref, px, py, pc):
    r = buf_ref.shape[0] // N_DEV
    return buf_ref.at[pl.ds(pl.multiple_of((4 * px + 2 * py + pc) * r, 8), r), :]


def _small_ici_copies(refs, send_sems, recv_sems):
    x, y, c = _mesh_pos()
    copies = []
    for a, buf in enumerate(refs):
        mine = _small_rows(buf, x, y, c)
        for j, chip in enumerate([(1 - x, y), (x, 1 - y), (1 - x, 1 - y)]):
            copies.append(pltpu.make_async_remote_copy(
                src_ref=mine, dst_ref=mine, send_sem=send_sems.at[a * 3 + j], recv_sem=recv_sems.at[a * 3 + j],
                device_id=(*chip, c), device_id_type=MESH))
    return copies


def _small_finish(bufs, name):
    n = len(bufs)

    def body(*refs):
        buf_refs = refs[n:2 * n]
        send_sems, recv_sems = refs[2 * n:]
        x, y, c = _mesh_pos()
        owners = [(x, y), (1 - x, y), (x, 1 - y), (1 - x, 1 - y)]
        passed, arriving = [], []
        for a in range(n):
            for j, (px, py) in enumerate(owners):
                for pc, group in ((c, passed), (1 - c, arriving)):
                    rows = _small_rows(buf_refs[a], px, py, pc)
                    group.append(pltpu.make_async_remote_copy(
                        src_ref=rows, dst_ref=rows, send_sem=send_sems.at[a * 4 + j],
                        recv_sem=recv_sems.at[a * 4 + j], device_id=(x, y, 1 - c), device_id_type=MESH))
        for cp in passed:
            cp.start()
        for cp in arriving:
            cp.wait_recv()
        for cp in passed:
            cp.wait_send()

    anyspec = pl.BlockSpec(memory_space=pl.ANY)
    return pl.pallas_call(
        body, name=name, out_shape=[jax.ShapeDtypeStruct(b.shape, b.dtype) for b in bufs],
        in_specs=[anyspec] * n, out_specs=[anyspec] * n, input_output_aliases={a: a for a in range(n)},
        scratch_shapes=[pltpu.SemaphoreType.DMA((4 * n,)), pltpu.SemaphoreType.DMA((4 * n,))],
    )(*bufs)


def _sibling_begin(grad, col_flag, after, name):
    land = lax.empty((N_CHIP,) + _piece_shape(grad.shape, col_flag), grad.dtype)
    return _split_begin([grad, land], N_CHIP, _sibling_copies(col_flag), after, name)


def _sibling_end(handle, col_flag, after, name):
    return _split_end(handle, _sibling_copies(col_flag), after, name)


def _gather_finish(fulls, col_flags, name):
    n = len(fulls)

    def body(*refs):
        full_refs = refs[n:2 * n]
        send_sems, recv_sems = refs[2 * n:]
        x, y, c = _mesh_pos()
        passed, arriving = [], []
        for a in range(n):
            for j, chip in enumerate([(1 - x, y), (x, 1 - y), (1 - x, 1 - y)]):
                k_from = 2 * chip[0] + chip[1]
                for h, group in ((c, passed), (1 - c, arriving)):
                    win = _piece(full_refs[a], col_flags[a], k_from, h)
                    group.append(pltpu.make_async_remote_copy(
                        src_ref=win, dst_ref=win, send_sem=send_sems.at[a * 3 + j],
                        recv_sem=recv_sems.at[a * 3 + j], device_id=(x, y, 1 - c), device_id_type=MESH))
        for cp in passed:
            cp.start()
        for cp in arriving:
            cp.wait_recv()
        for cp in passed:
            cp.wait_send()

    anyspec = pl.BlockSpec(memory_space=pl.ANY)
    return pl.pallas_call(
        body, name=name,
        out_shape=[jax.ShapeDtypeStruct(f.shape, f.dtype) for f in fulls],
        in_specs=[anyspec] * n, out_specs=[anyspec] * n,
        input_output_aliases={a: a for a in range(n)},
        scratch_shapes=[pltpu.SemaphoreType.DMA((3 * n,)), pltpu.SemaphoreType.DMA((3 * n,))],
    )(*fulls)


def _sibling_rider(grads, col_flags):
    n = len(grads)
    pshapes = [_piece_shape(g.shape, col) for g, col in zip(grads, col_flags)]

    def copies(ins, outs, sems):
        send_sems, recv_sems = sems
        x, y, c = _mesh_pos()
        return [pltpu.make_async_remote_copy(
            src_ref=_piece(ins[a], col_flags[a], k, 1 - c), dst_ref=outs[a].at[k],
            send_sem=send_sems.at[a * N_CHIP + k], recv_sem=recv_sems.at[a * N_CHIP + k],
            device_id=(x, y, 1 - c), device_id_type=MESH) for a in range(n) for k in range(N_CHIP)]

    def start(ins, outs, sems):
        for cp in copies(ins, outs, sems):
            cp.start()

    def finish(ins, outs, sems):
        cps = copies(ins, outs, sems)
        for cp in cps:
            cp.wait_recv()
        for cp in cps:
            cp.wait_send()

    return _Rider(grads, [jax.ShapeDtypeStruct((N_CHIP,) + ps, g.dtype) for ps, g in zip(pshapes, grads)],
                  [pltpu.SemaphoreType.DMA((N_CHIP * n,)), pltpu.SemaphoreType.DMA((N_CHIP * n,))], start, finish)


def _sum_with_sibling(grads, landed, col_flags, c_idx, name):
    n = len(grads)
    pshapes = [_piece_shape(g.shape, col) for g, col in zip(grads, col_flags)]

    def body(c_ref, *refs):
        ins, lands, outs = refs[:n], refs[n:2 * n], refs[2 * n:]
        for a in range(n):
            outs[a][0] = (ins[a][...] + lands[a][0]).astype(BF16)

    in_specs = []
    for ps, col in zip(pshapes, col_flags):
        if col:
            in_specs.append(pl.BlockSpec(ps, lambda k, c_ref: (c_ref[0], k)))
        else:
            in_specs.append(pl.BlockSpec(ps, lambda k, c_ref: (2 * k + c_ref[0], 0)))
    land_specs = [pl.BlockSpec((1,) + ps, lambda k, c_ref: (k, 0, 0)) for ps in pshapes]
    return pl.pallas_call(
        body, name=name,
        grid_spec=pltpu.PrefetchScalarGridSpec(
            num_scalar_prefetch=1, grid=(N_CHIP,),
            in_specs=in_specs + land_specs, out_specs=land_specs),
        out_shape=[jax.ShapeDtypeStruct((N_CHIP,) + ps, BF16) for ps in pshapes],
        compiler_params=pltpu.CompilerParams(dimension_semantics=("arbitrary",), vmem_limit_bytes=VMEM_LIMIT_BYTES),
    )(c_idx, *grads, *landed)


def _chips_rider(parts):
    n = len(parts)

    def plan(ins, outs, sems, arriving):
        send_sems, recv_sems = sems
        x, y, c = _mesh_pos()
        k_me = 2 * x + y
        copies = []
        for a in range(n):
            for j, chip in enumerate([(1 - x, y), (x, 1 - y), (1 - x, 1 - y)]):
                k_peer = 2 * chip[0] + chip[1]
                copies.append(pltpu.make_async_remote_copy(
                    src_ref=ins[a].at[k_peer], dst_ref=outs[a].at[k_peer if arriving else k_me],
                    send_sem=send_sems.at[a * 3 + j], recv_sem=recv_sems.at[a * 3 + j],
                    device_id=(*chip, c), device_id_type=MESH))
        return copies

    def start(ins, outs, sems):
        for cp in plan(ins, outs, sems, False):
            cp.start()

    def finish(ins, outs, sems):
        arrivals = plan(ins, outs, sems, True)
        for cp in arrivals:
            cp.wait_recv()
        for cp in arrivals:
            cp.wait_send()

    return _Rider(parts, [jax.ShapeDtypeStruct(p.shape, p.dtype) for p in parts],
                  [pltpu.SemaphoreType.DMA((3 * n,)), pltpu.SemaphoreType.DMA((3 * n,))], start, finish)


def _sum_chips_and_share(landed, parts, after, name):
    n = len(landed)

    def body(*refs):
        ins, own, outs, red = refs[:n], refs[n:2 * n], refs[2 * n + 1:3 * n + 1], refs[3 * n + 1:4 * n + 1]
        send_sems, recv_sems, local_sems = refs[4 * n + 1:]
        x, y, c = _mesh_pos()
        sibling = (x, y, 1 - c)
        k_me = 2 * x + y
        copies, local = [], []
        for a in range(n):
            for k in range(N_CHIP):
                @pl.when(k_me == k)
                def _():
                    term = own[a][k].astype(F32)
                    red[a][...] = term if k == 0 else red[a][...] + term

                @pl.when(k_me != k)
                def _():
                    term = ins[a][k].astype(F32)
                    red[a][...] = term if k == 0 else red[a][...] + term

            mine = pltpu.make_async_copy(red[a], outs[a].at[c], local_sems.at[a])
            mine.start()
            local.append(mine)
            cp = pltpu.make_async_remote_copy(
                src_ref=red[a], dst_ref=outs[a].at[c],
                send_sem=send_sems.at[a], recv_sem=recv_sems.at[a],
                device_id=sibling, device_id_type=MESH)
            cp.start()
            copies.append(cp)
        for a in range(n):
            pltpu.make_async_remote_copy(
                src_ref=red[a], dst_ref=outs[a].at[1 - c],
                send_sem=send_sems.at[a], recv_sem=recv_sems.at[a],
                device_id=sibling, device_id_type=MESH).wait_recv()
        for cp in copies:
            cp.wait_send()
        for mine in local:
            mine.wait()

    return pl.pallas_call(
        body, name=name,
        out_shape=[jax.ShapeDtypeStruct((2,) + l.shape[1:], F32) for l in landed],
        in_specs=[pl.BlockSpec(memory_space=pltpu.VMEM)] * (2 * n + 1),
        out_specs=[pl.BlockSpec(memory_space=pl.ANY)] * n,
        scratch_shapes=[pltpu.VMEM(l.shape[1:], F32) for l in landed]
        + [pltpu.SemaphoreType.DMA((n,)), pltpu.SemaphoreType.DMA((n,)), pltpu.SemaphoreType.DMA((n,))],
        compiler_params=pltpu.CompilerParams(vmem_limit_bytes=VMEM_LIMIT_BYTES),
    )(*landed, *parts, after)


def _mod_shard(c_all, w_ada, b_shard):
    def body(c_ref, w_ref, b_ref, o_ref, sc_ref):
        cc = c_ref[...]
        sc = cc * _sigmoid(cc)
        sc_ref[...] = sc
        o_ref[...] = _dot(sc, w_ref[...]) + b_ref[...]

    nb, d = c_all.shape
    nn = w_ada.shape[1]
    return pl.pallas_call(
        body, name="mod_shard",
        out_shape=[jax.ShapeDtypeStruct((nb, nn), F32), jax.ShapeDtypeStruct((nb, d), F32)],
        compiler_params=pltpu.CompilerParams(vmem_limit_bytes=VMEM_LIMIT_BYTES),
    )(c_all, w_ada, b_shard)


V_SH_M, V_SC_M, V_G_M, V_SH_F, V_SC_F, V_G_F, V_PRE_MIX, V_POST_MIX, V_PRE_FFN, V_POST_FFN = range(10)


def _vrow(vec_ref, r):
    return vec_ref[r:r + 1, :]


def _mix_fwd(x, vecs, w_in_b, w_out_b, sgu_g, wm_b, bsb, wp_b, ps, pmats, ts):
    s_len, d = x.shape
    nt = s_len // ts
    nblk = ts // HEAD
    n_proj = w_in_b.shape[1]

    def body(x_ref, vec_ref, win_ref, wout_ref, sg_ref, wm_ref, bs_ref, wp_ref, ps_ref, pm_ref,
             h1_ref, proj_ref, cat_ref, mixed_ref, x2_ref, h2_ref, carry_ref):
        i = pl.program_id(0)

        @pl.when(i == 0)
        def _():
            carry_ref[...] = jnp.zeros_like(carry_ref)

        x = x_ref[...]
        h1 = (((x * _rms(x)) * _vrow(vec_ref, V_PRE_MIX)) * (1.0 + _vrow(vec_ref, V_SC_M))
              + _vrow(vec_ref, V_SH_M)).astype(BF16)
        h1_ref[...] = h1
        proj = _dot(h1, win_ref[...])
        proj_ref[...] = proj.astype(BF16)
        t_glob = lax.broadcasted_iota(jnp.int32, (ts, HEAD), 0) + i * ts
        for h in range(N_HEADS):
            u = _gelu(proj[:, h * HEAD:(h + 1) * HEAD])
            v = _gelu(proj[:, A_WIDTH + h * HEAD:A_WIDTH + (h + 1) * HEAD])
            vn = ((v * _rms(v)) * sg_ref[h:h + 1, :]).astype(BF16)
            for b in range(nblk):
                rs = slice(b * HEAD, (b + 1) * HEAD)
                z = _dot(wm_ref[h], vn[rs]) + bs_ref[h]
                cat_ref[rs, h * HEAD:(h + 1) * HEAD] = (u[rs] * z).astype(BF16)
        for g in range(len(POOL_WINDOWS)):
            gs = slice(g * HEAD, (g + 1) * HEAD)
            p = proj[:, 2 * A_WIDTH + g * HEAD:2 * A_WIDTH + (g + 1) * HEAD]
            pooled = _pool_fwd(p, carry_ref[:, gs], g, t_glob)
            yb = _dot(pooled.astype(BF16), wp_ref[g]) * ps_ref[0:1, gs]
            cat_ref[:, A_WIDTH + g * HEAD:A_WIDTH + (g + 1) * HEAD] = yb.astype(BF16)
        carry_ref[...] = proj[ts - POOL_HALO:ts, 2 * A_WIDTH:]
        mixed = _dot(cat_ref[...], wout_ref[...])
        mixed_ref[...] = mixed
        x2 = x + _vrow(vec_ref, V_G_M) * ((mixed * _rms(mixed)) * _vrow(vec_ref, V_POST_MIX))
        x2_ref[...] = x2
        h2 = (((x2 * _rms(x2)) * _vrow(vec_ref, V_PRE_FFN)) * (1.0 + _vrow(vec_ref, V_SC_F))
              + _vrow(vec_ref, V_SH_F)).astype(BF16)
        for b in range(ts // FFN_TS):
            rs = slice(b * FFN_TS, (b + 1) * FFN_TS)
            h2_ref[rs, :] = _permute_bf16(pm_ref[0], h2[rs])

    row = lambda i: (i, 0)
    return _call(
        body, name="mix_fwd", grid=(nt,),
        in_specs=[_tiled((ts, d), row), _whole(vecs.shape), _resident(w_in_b.shape), _resident(w_out_b.shape),
                  _whole(sgu_g.shape), _whole(wm_b.shape), _whole(bsb.shape), _whole(wp_b.shape), _whole(ps.shape),
                  _whole(pmats.shape)],
        out_specs=[_tiled((ts, d), row), _tiled((ts, n_proj), row), _tiled((ts, d), row),
                   _tiled((ts, d), row), _tiled((ts, d), row), _tiled((ts, d), row)],
        out_shape=[jax.ShapeDtypeStruct((s_len, d), BF16), jax.ShapeDtypeStruct((s_len, n_proj), BF16),
                   jax.ShapeDtypeStruct((s_len, d), BF16), jax.ShapeDtypeStruct((s_len, d), F32),
                   jax.ShapeDtypeStruct((s_len, d), F32), jax.ShapeDtypeStruct((s_len, d), BF16)],
        scratch_shapes=[pltpu.VMEM((POOL_HALO, A_WIDTH), F32)],
        args=(x, vecs, w_in_b, w_out_b, sgu_g, wm_b, bsb, wp_b, ps, pmats))


def _perm_mats(ts):
    p = jnp.arange(ts)
    pm = (((p % 8) * (ts // 8) + p // 8)[:, None] == p[None, :]).astype(BF16)
    return jnp.stack([pm, pm.T])


def _permute_bf16(pm, xb):
    return _dot(pm, xb).astype(BF16)


def _permute_f32(pm, x):
    hi = x.astype(BF16)
    lo = (x - hi.astype(F32)).astype(BF16)
    return _dot(pm, hi) + _dot(pm, lo)


def _conv_out(u, um2, um1, cv_ref, cols):
    return (cv_ref[3:4, cols] + um2 * cv_ref[0:1, cols] + um1 * cv_ref[1:2, cols] + u * cv_ref[2:3, cols])


F_LOSS, F_DGF, F_DPOSTFFN = 0, 1, 2
B_DSHF, B_DSCF, B_DPREFFN, B_DGM, B_DPOSTMIX = 0, 1, 2, 3, 4
M_DSHM, M_DSCM, M_DPREMIX = 0, 1, 2
C_DCB, C_DCW = 0, 1


def _ffn_fwd(h2p, x2, tgt, w_up_b, w_down_b, cvec, vecs, pmats, ts):
    s_len, d = x2.shape
    ff2 = w_up_b.shape[1]
    ff = ff2 // 2
    nt = s_len // ts
    chunks = _ff_chunks(ff)

    def body(h2_ref, x2_ref, t_ref, wu_ref, wd_ref, cv_ref, vec_ref, pm_ref,
             up_ref, y_ref, act_ref, dy_ref, df_ref, acc_ref, carry_ref):
        @pl.when(pl.program_id(0) == 0)
        def _():
            carry_ref[...] = jnp.zeros_like(carry_ref)
            acc_ref[...] = jnp.zeros_like(acc_ref)

        h2v = h2_ref[...]

        def up_dots(o, w):
            return [_dot(h2v, wu_ref[:, base + o:base + o + w]) for base in (0, ff)]

        f = None
        pending = None
        nxt = up_dots(*chunks[0])
        for ci, (o, w) in enumerate(chunks):
            us = nxt
            if ci + 1 < len(chunks):
                nxt = up_dots(*chunks[ci + 1])
            if pending is not None:
                part = _dot(pending[0], wd_ref[pending[1]:pending[1] + pending[2], :])
                f = part if f is None else f + part
            sub0 = lax.broadcasted_iota(jnp.int32, (8, w), 0) == 0
            ys = []
            for base, u in zip((0, ff), us):
                cols = slice(base + o, base + o + w)
                up_ref[:, cols] = u.astype(BF16)
                b1 = jnp.where(sub0, pltpu.roll(carry_ref[8:16, cols], 1, 0), pltpu.roll(u[ts - 8:ts], 1, 0))
                b2 = jnp.where(sub0, pltpu.roll(carry_ref[0:8, cols], 1, 0), pltpu.roll(u[ts - 16:ts - 8], 1, 0))
                um1 = jnp.concatenate([b1, u[:ts - 8]], axis=0)
                um2 = jnp.concatenate([b2, b1, u[:ts - 16]], axis=0)
                ys.append(_conv_out(u, um2, um1, cv_ref, cols))
                carry_ref[:, cols] = u[ts - 16:ts]
            gate, val = ys
            sg = _sigmoid(gate)
            gs = gate * sg
            act = (gs * val).astype(BF16)
            act_ref[:, o:o + w] = act
            y_ref[:, o:o + w] = (val * (sg + gs * (1.0 - sg))).astype(BF16)
            y_ref[:, ff + o:ff + o + w] = gs.astype(BF16)
            pending = (act, o, w)
        f = f + _dot(pending[0], wd_ref[pending[1]:pending[1] + pending[2], :])
        f = _permute_f32(pm_ref[1], f)
        r3 = _rms(f)
        fhat = f * r3
        post = _vrow(vec_ref, V_POST_FFN)
        g_f = _vrow(vec_ref, V_G_F)
        fn = fhat * post
        e = (x2_ref[...] + g_f * fn) - t_ref[...]
        dy = e * (1.0 / d)
        dy_ref[...] = dy
        dfn = dy * g_f
        acc_ref[F_LOSS:F_LOSS + 1, :] += _colsum(e * e)
        acc_ref[F_DGF:F_DGF + 1, :] += _colsum(dy * fn)
        acc_ref[F_DPOSTFFN:F_DPOSTFFN + 1, :] += _colsum(dfn * fhat)
        dfhat = dfn * post
        df = (r3 * (dfhat - fhat * _rowmean(dfhat * fhat))).astype(BF16)
        df_ref[...] = _permute_bf16(pm_ref[0], df)

    row = lambda i: (i, 0)
    return pl.pallas_call(
        body, name="ffn_fwd", grid=(nt,),
        in_specs=[_tiled((ts, d), row), _tiled((ts, d), row), _tiled((ts, d), row), _resident(w_up_b.shape),
                  _resident(w_down_b.shape), _whole(cvec.shape), _whole(vecs.shape), _whole(pmats.shape)],
        out_specs=[_tiled((ts, ff2), row), _tiled((ts, ff2), row), _tiled((ts, ff), row), _tiled((ts, d), row),
                   _tiled((ts, d), row), _whole((8, d))],
        out_shape=[jax.ShapeDtypeStruct((s_len, ff2), BF16), jax.ShapeDtypeStruct((s_len, ff2), BF16),
                   jax.ShapeDtypeStruct((s_len, ff), BF16), jax.ShapeDtypeStruct((s_len, d), F32),
                   jax.ShapeDtypeStruct((s_len, d), BF16), jax.ShapeDtypeStruct((8, d), F32)],
        scratch_shapes=[pltpu.VMEM((16, ff2), F32)],
        compiler_params=_seq_params(),
    )(h2p, x2, tgt, w_up_b, w_down_b, cvec, vecs, pmats)


def _ffn_bwd(dfp, upp, yp, x2, dy, mixed, w_up_b, w_down_b, cvec, vecs, pmats, ts, rider=None):
    s_len, d = x2.shape
    ff2 = w_up_b.shape[1]
    ff = ff2 // 2
    nt = s_len // ts
    chunks = _ff_chunks(ff, 512)

    def body(df_ref, up_ref, y_ref, x2_ref, dy_ref, mx_ref, wu_ref, wd_ref, cv_ref, vec_ref, pm_ref,
             dup_ref, dx2_ref, dmx_ref, accc_ref, acc_ref, carry_ref):
        @pl.when(pl.program_id(0) == 0)
        def _():
            carry_ref[...] = jnp.zeros_like(carry_ref)
            accc_ref[...] = jnp.zeros_like(accc_ref)
            acc_ref[...] = jnp.zeros_like(acc_ref)

        dfv = df_ref[...]

        def dh2_add(acc, dups, o, w):
            for base, dup in zip((0, ff), dups):
                part = _dot_nt(dup, wu_ref[:, base + o:base + o + w])
                acc = part if acc is None else acc + part
            return acc

        dh2 = None
        pending = None
        nxt = _dot_nt(dfv, wd_ref[chunks[0][0]:chunks[0][0] + chunks[0][1], :])
        for ci, (o, w) in enumerate(chunks):
            dact = nxt
            if ci + 1 < len(chunks):
                o2, w2 = chunks[ci + 1]
                nxt = _dot_nt(dfv, wd_ref[o2:o2 + w2, :])
            if pending is not None:
                dh2 = dh2_add(dh2, *pending)
            sub7 = lax.broadcasted_iota(jnp.int32, (8, w), 0) == 7
            dups = []
            dys = (dact * y_ref[:, o:o + w].astype(F32), dact * y_ref[:, ff + o:ff + o + w].astype(F32))
            for base, dyv in zip((0, ff), dys):
                cols = slice(base + o, base + o + w)
                u = up_ref[:, cols].astype(F32)
                e0 = jnp.where(sub7, pltpu.roll(carry_ref[0:8, cols], 7, 0), pltpu.roll(dyv[0:8], 7, 0))
                e1 = jnp.where(sub7, pltpu.roll(carry_ref[8:16, cols], 7, 0), pltpu.roll(dyv[8:16], 7, 0))
                dyp1 = jnp.concatenate([dyv[8:], e0], axis=0)
                dyp2 = jnp.concatenate([dyv[16:], e0, e1], axis=0)
                accc_ref[C_DCB:C_DCB + 1, cols] += _colsum(dyv)
                accc_ref[C_DCW + 0:C_DCW + 1, cols] += _colsum(dyp2 * u)
                accc_ref[C_DCW + 1:C_DCW + 2, cols] += _colsum(dyp1 * u)
                accc_ref[C_DCW + 2:C_DCW + 3, cols] += _colsum(dyv * u)
                dup = (dyv * cv_ref[2:3, cols] + dyp1 * cv_ref[1:2, cols] + dyp2 * cv_ref[0:1, cols]).astype(BF16)
                dup_ref[:, cols] = dup
                dups.append(dup)
                carry_ref[:, cols] = dyv[0:16]
            pending = (dups, o, w)
        dh2 = dh2_add(dh2, *pending)
        dh2 = _permute_f32(pm_ref[1], dh2)
        x2 = x2_ref[...]
        r2 = _rms(x2)
        xn = x2 * r2
        pre = _vrow(vec_ref, V_PRE_FFN)
        one_sc = 1.0 + _vrow(vec_ref, V_SC_F)
        acc_ref[B_DSHF:B_DSHF + 1, :] += _colsum(dh2)
        acc_ref[B_DSCF:B_DSCF + 1, :] += _colsum(dh2 * (xn * pre))
        acc_ref[B_DPREFFN:B_DPREFFN + 1, :] += _colsum(dh2 * xn * one_sc)
        dxn = dh2 * pre * one_sc
        dx2 = dy_ref[...] + r2 * (dxn - xn * _rowmean(dxn * xn))
        dx2_ref[...] = dx2
        mixed = mx_ref[...]
        rm = _rms(mixed)
        mhat = mixed * rm
        post = _vrow(vec_ref, V_POST_MIX)
        acc_ref[B_DGM:B_DGM + 1, :] += _colsum(dx2 * (mhat * post))
        dmn = dx2 * _vrow(vec_ref, V_G_M)
        acc_ref[B_DPOSTMIX:B_DPOSTMIX + 1, :] += _colsum(dmn * mhat)
        dmhat = dmn * post
        dmx_ref[...] = (rm * (dmhat - mhat * _rowmean(dmhat * mhat))).astype(BF16)

    rev = lambda i: (nt - 1 - i, 0)
    return _call(
        body, name="ffn_bwd", grid=(nt,), rider=rider,
        in_specs=[_tiled((ts, d), rev), _tiled((ts, ff2), rev), _tiled((ts, ff2), rev), _tiled((ts, d), rev),
                  _tiled((ts, d), rev), _tiled((ts, d), rev), _resident(w_up_b.shape), _resident(w_down_b.shape),
                  _whole(cvec.shape), _whole(vecs.shape), _whole(pmats.shape)],
        out_specs=[_tiled((ts, ff2), rev), _tiled((ts, d), rev), _tiled((ts, d), rev), _whole((8, ff2)),
                   _whole((8, d))],
        out_shape=[jax.ShapeDtypeStruct((s_len, ff2), BF16), jax.ShapeDtypeStruct((s_len, d), F32),
                   jax.ShapeDtypeStruct((s_len, d), BF16), jax.ShapeDtypeStruct((8, ff2), F32),
                   jax.ShapeDtypeStruct((8, d), F32)],
        scratch_shapes=[pltpu.VMEM((16, ff2), F32)],
        args=(dfp, upp, yp, x2, dy, mixed, w_up_b, w_down_b, cvec, vecs, pmats))


def _mix_bwd(dmixed, dx2, x, proj, vecs, w_in_b, w_out_b, sgu_g, wm_b, wmt_b, bsb, wp_b, ps, ts, rider=None):
    s_len, d = x.shape
    nt = s_len // ts
    nblk = ts // HEAD
    n_proj = proj.shape[1]
    per = ts // POOL_HALO
    ext_len = ts + POOL_HALO

    def body(dmx_ref, dx2_ref, x_ref, proj_ref, projh_ref, vec_ref, win_ref, wout_ref, sg_ref, wm_ref, wmt_ref,
             bs_ref, wp_ref, ps_ref,
             gx_ref, dproj_ref, acc_ref, dwm_out, dwp_out, db_ref, dg_ref, dps_ref,
             carry_ref, dwm_ref, dwp_ref, dbz_ref):
        i = pl.program_id(0)
        tile = nt - 1 - i

        @pl.when(i == 0)
        def _():
            carry_ref[...] = jnp.zeros_like(carry_ref)
            for r in (acc_ref, dwm_ref, dwp_ref, dbz_ref, dg_ref, dps_ref):
                r[...] = jnp.zeros_like(r)

        dcat = _dot_nt(dmx_ref[...], wout_ref[...])
        t_glob = lax.broadcasted_iota(jnp.int32, (ts, HEAD), 0) + tile * ts
        for h in range(N_HEADS):
            hs = slice(h * HEAD, (h + 1) * HEAD)
            vs = slice(A_WIDTH + h * HEAD, A_WIDTH + (h + 1) * HEAD)
            au = proj_ref[:, hs].astype(F32)
            av = proj_ref[:, vs].astype(F32)
            u = _gelu(au)
            v = _gelu(av)
            rv = _rms(v)
            vhat = v * rv
            gain = sg_ref[h:h + 1, :]
            vn = (vhat * gain).astype(BF16)
            dout = dcat[:, hs]
            du_parts, dvn_parts = [], []
            for b in range(nblk):
                rs = slice(b * HEAD, (b + 1) * HEAD)
                z = _dot(wm_ref[h], vn[rs]) + bs_ref[h]
                du_parts.append(dout[rs] * z)
                dz = dout[rs] * u[rs]
                dbz_ref[h] += dz
                dzb = dz.astype(BF16)
                dwm_ref[h] += _dot_nt(dzb, vn[rs])
                dvn_parts.append(_dot(wmt_ref[h], dzb))
            du = jnp.concatenate(du_parts, axis=0)
            dvn = jnp.concatenate(dvn_parts, axis=0)
            dg_ref[h:h + 1, :] += _colsum(dvn * vhat)
            dvhat = dvn * gain
            dv = rv * (dvhat - vhat * _rowmean(dvhat * vhat))
            dproj_ref[:, hs] = (du * _gelu_grad(au)).astype(BF16)
            dproj_ref[:, vs] = (dv * _gelu_grad(av)).astype(BF16)
        for g in range(len(POOL_WINDOWS)):
            gs = slice(g * HEAD, (g + 1) * HEAD)
            pcols = slice(2 * A_WIDTH + g * HEAD, 2 * A_WIDTH + (g + 1) * HEAD)
            p = proj_ref[:, pcols].astype(F32)
            halo = jnp.where(tile > 0, projh_ref[:, pcols].astype(F32), 0.0)
            pb = _pool_fwd(p, halo, g, t_glob).astype(BF16)
            dyb = dcat[:, A_WIDTH + g * HEAD:A_WIDTH + (g + 1) * HEAD]
            dps_ref[0:1, gs] += _colsum(dyb * _dot(pb, wp_ref[g]))
            dyl = (dyb * ps_ref[0:1, gs]).astype(BF16)
            dwp_ref[g] += _dot_tn(pb, dyl)
            dpooled = _dot_nt(dyl, wp_ref[g])
            cnt = jnp.minimum(t_glob + 1, POOL_WINDOWS[g]).astype(F32)
            q = dpooled / cnt
            s = jnp.concatenate([q, carry_ref[:, gs]], axis=0)
            for step in range(g + 1):
                s = s + pltpu.roll(s, ext_len - (1 << step), 0)
            dproj_ref[:, pcols] = (s[:ts] - dpooled).astype(BF16)
            carry_ref[:, gs] = q[0:POOL_HALO]
        dh1 = _dot_nt(dproj_ref[...], win_ref[...])
        x = x_ref[...]
        r1 = _rms(x)
        xn = x * r1
        pre = _vrow(vec_ref, V_PRE_MIX)
        one_sc = 1.0 + _vrow(vec_ref, V_SC_M)
        acc_ref[M_DSHM:M_DSHM + 1, :] += _colsum(dh1)
        acc_ref[M_DSCM:M_DSCM + 1, :] += _colsum(dh1 * (xn * pre))
        acc_ref[M_DPREMIX:M_DPREMIX + 1, :] += _colsum(dh1 * xn * one_sc)
        dxn = dh1 * pre * one_sc
        gx_ref[...] = dx2_ref[...] + r1 * (dxn - xn * _rowmean(dxn * xn))

        @pl.when(i == nt - 1)
        def _():
            dwm_out[...] = dwm_ref[...].astype(BF16)
            dwp_out[...] = dwp_ref[...].astype(BF16)
            db_ref[...] = jnp.zeros_like(db_ref)
            for h in range(N_HEADS):
                db_ref[h:h + 1, :] = jnp.sum(dbz_ref[h].T, axis=0, keepdims=True)

    rev = lambda i: (nt - 1 - i, 0)
    halo_map = lambda i: (jnp.maximum((nt - 1 - i) * per - 1, 0), 0)
    hshape = (N_HEADS, HEAD, HEAD)
    return _call(
        body, name="mix_bwd", grid=(nt,), rider=rider,
        in_specs=[_tiled((ts, d), rev), _tiled((ts, d), rev), _tiled((ts, d), rev), _tiled((ts, n_proj), rev),
                  _tiled((POOL_HALO, n_proj), halo_map), _whole(vecs.shape), _resident(w_in_b.shape),
                  _resident(w_out_b.shape), _whole(sgu_g.shape), _whole(wm_b.shape), _whole(wmt_b.shape),
                  _whole(bsb.shape), _whole(wp_b.shape), _whole(ps.shape)],
        out_specs=[_tiled((ts, d), rev), _tiled((ts, n_proj), rev), _whole((8, d)), _whole(hshape), _whole(hshape),
                   _whole((8, HEAD)), _whole((8, HEAD)), _whole((8, A_WIDTH))],
        out_shape=[jax.ShapeDtypeStruct((s_len, d), F32), jax.ShapeDtypeStruct((s_len, n_proj), BF16),
                   jax.ShapeDtypeStruct((8, d), F32), jax.ShapeDtypeStruct(hshape, BF16),
                   jax.ShapeDtypeStruct(hshape, BF16), jax.ShapeDtypeStruct((8, HEAD), F32),
                   jax.ShapeDtypeStruct((8, HEAD), F32), jax.ShapeDtypeStruct((8, A_WIDTH), F32)],
        scratch_shapes=[pltpu.VMEM((POOL_HALO, A_WIDTH), F32), pltpu.VMEM(hshape, F32), pltpu.VMEM(hshape, F32),
                        pltpu.VMEM(hshape, F32)],
        args=(dmixed, dx2, x, proj, proj, vecs, w_in_b, w_out_b, sgu_g, wm_b, wmt_b, bsb, wp_b, ps))


def _wgrad(a, b, tn, ts, name, rider=None):
    s_len, m = a.shape
    n = b.shape[1]
    ts = min(ts, s_len)

    def body(a_ref, b_ref, o_ref):
        @pl.when(pl.program_id(1) == 0)
        def _():
            o_ref[...] = jnp.zeros_like(o_ref)

        o_ref[...] += _dot_tn(a_ref[...], b_ref[...])

    (g,), r_out = _call(
        body, name=name, grid=(n // tn, s_len // ts), rider=rider,
        in_specs=[pl.BlockSpec((ts, m), lambda j, s: (s, 0)), pl.BlockSpec((ts, tn), lambda j, s: (s, j))],
        out_specs=[pl.BlockSpec((m, tn), lambda j, s: (0, j))],
        out_shape=[jax.ShapeDtypeStruct((m, n), F32)], scratch_shapes=[], args=(a, b))
    return g, r_out


def _wgrad_pair(a1, b1, a2, b2, after, ts, name):
    s_len = a1.shape[0]
    ts = min(ts, s_len)
    shapes = [(a1.shape[1], b1.shape[1]), (a2.shape[1], b2.shape[1])]

    def body(a1_ref, b1_ref, a2_ref, b2_ref, after_ref, o1_ref, o2_ref):
        @pl.when(pl.program_id(0) == 0)
        def _():
            o1_ref[...] = jnp.zeros_like(o1_ref)
            o2_ref[...] = jnp.zeros_like(o2_ref)

        o1_ref[...] += _dot_tn(a1_ref[...], b1_ref[...])
        o2_ref[...] += _dot_tn(a2_ref[...], b2_ref[...])

    row = lambda s: (s, 0)
    return _call(
        body, name=name, grid=(s_len // ts,),
        in_specs=[pl.BlockSpec((ts, t.shape[1]), row) for t in (a1, b1, a2, b2)] + [_whole(after.shape)],
        out_specs=[_whole(sh) for sh in shapes],
        out_shape=[jax.ShapeDtypeStruct(sh, F32) for sh in shapes], scratch_shapes=[],
        args=(a1, b1, a2, b2, after))[0]


def _adamw_big(g, w, m, v, name):
    r, cdim = g.shape
    tr = r
    while tr * cdim * 4 > (3 << 19) and tr % 16 == 0:
        tr //= 2

    def body(g_ref, w_ref, m_ref, v_ref, d_ref, nm_ref, nv_ref):
        delta, m2, v2 = _adamw_math(w_ref[0], g_ref[...], m_ref[0], v_ref[0])
        d_ref[0] = delta
        nm_ref[0] = m2
        nv_ref[0] = v2

    s3 = pl.BlockSpec((1, tr, cdim), lambda i: (0, i, 0))
    return pl.pallas_call(
        body, name=name, grid=(r // tr,),
        in_specs=[pl.BlockSpec((tr, cdim), lambda i: (i, 0)), s3, s3, s3],
        out_specs=[s3, s3, s3],
        out_shape=[jax.ShapeDtypeStruct(w.shape, F32)] * 3,
        compiler_params=pltpu.CompilerParams(dimension_semantics=("parallel",), vmem_limit_bytes=VMEM_LIMIT_BYTES),
    )(g, w, m, v)


def _wada_update(sct, gm, w, m, v):
    _, r, cdim = w.shape
    tr = 256
    kp = sct.shape[1]

    def body(s_ref, g_ref, w_ref, m_ref, v_ref, gw_ref, d_ref, nm_ref, nv_ref):
        g = _dot(s_ref[...], g_ref[...])
        gw_ref[0] = g
        delta, m2, v2 = _adamw_math(w_ref[0], g, m_ref[0], v_ref[0])
        d_ref[0] = delta
        nm_ref[0] = m2
        nv_ref[0] = v2

    s3 = pl.BlockSpec((1, tr, cdim), lambda i: (0, i, 0))
    return pl.pallas_call(
        body, name="wada_update", grid=(r // tr,),
        in_specs=[pl.BlockSpec((tr, kp), lambda i: (i, 0)), _whole(gm.shape), s3, s3, s3],
        out_specs=[s3, s3, s3, s3],
        out_shape=[jax.ShapeDtypeStruct(w.shape, F32)] * 4,
        compiler_params=pltpu.CompilerParams(dimension_semantics=("parallel",), vmem_limit_bytes=VMEM_LIMIT_BYTES),
    )(sct, gm, w, m, v)


def _small_update(g1, g2, g2s, gwm, gwp, gbz, gsg, gps, params):
    names = ["b_ada", "pre_mix_g", "post_mix_g", "sgu_norm_g", "w_spatial", "b_spatial", "w_pool", "pool_scale",
             "pre_ffn_g", "post_ffn_g", "conv_w", "conv_b"]
    d = g1.shape[2]
    flat_in = [g1, g2, g2s, gwm, gwp, gbz, gsg, gps]
    n_g = len(flat_in)
    for nm in names:
        flat_in += list(params[nm])

    def body(*refs):
        g1_ref, g2_ref, g2s_ref, gwm_ref, gwp_ref, gbz_ref, gsg_ref, gps_ref = refs[:n_g]
        wmv = refs[n_g:n_g + 3 * len(names)]
        loss_ref = refs[n_g + 3 * len(names)]
        outs = refs[n_g + 3 * len(names) + 1:]

        def dsum(ref, idx):
            acc = ref[(0,) + idx].astype(F32)
            for dev in range(1, N_DEV):
                acc = acc + ref[(dev,) + idx].astype(F32)
            return acc

        def apply(pi, g, widx, oidx):
            w_ref, m_ref, v_ref = wmv[3 * pi:3 * pi + 3]
            g_ref, d_ref, nm_ref, nv_ref = outs[4 * pi:4 * pi + 4]
            delta, m2, v2 = _adamw_math(w_ref[widx], g, m_ref[widx], v_ref[widx])
            g_ref[oidx] = g
            d_ref[oidx] = delta
            nm_ref[oidx] = m2
            nv_ref[oidx] = v2

        def row1(base, r):
            return (slice(base + r, base + r + 1), slice(None))

        tot = dsum(g1_ref, row1(0, F_LOSS))
        loss_ref[...] = jnp.zeros(loss_ref.shape, F32) + jnp.sum(tot) * (0.5 / d)
        mod_rows = [row1(16, M_DSHM), row1(16, M_DSCM), row1(8, B_DGM), row1(8, B_DSHF), row1(8, B_DSCF),
                    row1(0, F_DGF)]
        for j, rr in enumerate(mod_rows):
            cs = (slice(None), slice(j * d, (j + 1) * d))
            apply(0, dsum(g1_ref, rr), cs, cs)
        full = (slice(None), slice(None))
        apply(1, dsum(g1_ref, row1(16, M_DPREMIX)), full, full)
        apply(2, dsum(g1_ref, row1(8, B_DPOSTMIX)), full, full)
        apply(3, dsum(gsg_ref, (slice(0, N_HEADS), slice(None))), (0,), (0,))
        pos_i = lax.broadcasted_iota(jnp.int32, (HEAD, HEAD), 0)
        pos_j = lax.broadcasted_iota(jnp.int32, (HEAD, HEAD), 1)
        causal = (pos_j // CHUNK) <= (pos_i // CHUNK)
        for h in range(N_HEADS):
            blk = (slice(h * HEAD, (h + 1) * HEAD), slice(None))
            apply(4, jnp.where(causal, dsum(gwm_ref, blk), 0.0), (0, h), (0, h))
            apply(5, dsum(gbz_ref, (slice(h, h + 1), slice(None))), (0, slice(h, h + 1)), (0, slice(h, h + 1)))
            apply(6, dsum(gwp_ref, blk), (0, h), (0, h))
        apply(7, dsum(gps_ref, (slice(0, 1), slice(None))), full, full)
        apply(8, dsum(g1_ref, row1(8, B_DPREFFN)), full, full)
        apply(9, dsum(g1_ref, row1(0, F_DPOSTFFN)), full, full)
        apply(10, dsum(g2s_ref, (slice(C_DCW, C_DCW + 3), slice(None))), (0,), (0,))
        apply(11, dsum(g2_ref, (slice(C_DCB, C_DCB + 1), slice(None))), full, full)

    out_shape = [jax.ShapeDtypeStruct((8, HEAD), F32)]
    for nm in names:
        out_shape += [jax.ShapeDtypeStruct(params[nm][0].shape, F32)] * 4
    res = pl.pallas_call(
        body, name="small_update", out_shape=out_shape,
        compiler_params=pltpu.CompilerParams(vmem_limit_bytes=VMEM_LIMIT_BYTES),
    )(*flat_in)
    out = {nm: tuple(res[1 + 4 * i:5 + 4 * i]) for i, nm in enumerate(names)}
    return res[0], out


def kernel(x, c, w_ada, b_ada, pre_mix_g, post_mix_g, w_in, sgu_norm_g, w_spatial, b_spatial, w_pool, pool_scale, w_out, pre_ffn_g, post_ffn_g, w_up, conv_w, conv_b, w_down, loss_target, m_w_ada, m_b_ada, m_pre_mix_g, m_post_mix_g, m_w_in, m_sgu_norm_g, m_w_spatial, m_b_spatial, m_w_pool, m_pool_scale, m_w_out, m_pre_ffn_g, m_post_ffn_g, m_w_up, m_conv_w, m_conv_b, m_w_down, v_w_ada, v_b_ada, v_pre_mix_g, v_post_mix_g, v_w_in, v_sgu_norm_g, v_w_spatial, v_b_spatial, v_w_pool, v_pool_scale, v_w_out, v_pre_ffn_g, v_post_ffn_g, v_w_up, v_conv_w, v_conv_b, v_w_down):
    xi, yi, ci = _mesh_pos()
    k_me = 2 * xi + yi
    dev = 2 * k_me + ci
    s_len, d = x.shape[1], x.shape[2]
    x2d = x[0]
    tgt = loss_target[0]
    ff2 = conv_b.shape[1]
    n_ada = w_ada.shape[2]
    n_cw = conv_w.shape[2]

    k_idx = k_me.reshape(1).astype(jnp.int32)
    flags4 = (True, False, True, False)
    (w_in_s, w_out_s, w_up_s, w_down_s), (w_in_f, w_out_f, w_up_f, w_down_f) = _cast_bf16(
        [w_in[0], w_out[0], w_up[0], w_down[0]], flags4, k_idx)
    mix_flags = (True, False)
    fly_mix = _gather_begin([w_in_s, w_out_s], [w_in_f, w_out_f], mix_flags, k_idx, "gather_begin_mix")

    cw_blk = jnp.concatenate([conv_w[0], jnp.zeros((5, n_cw), F32)], axis=0)
    c_blk = c.reshape(8, d // 8) + fly_mix[3][0:1, 0:1]
    c_all, cw_all = _run_rider(_allgather_rider([c_blk, cw_blk]), "gather_c_convw")
    c_all = c_all.reshape(N_DEV, 8, d // 8).reshape(N_DEV, d)
    cw_full = jnp.concatenate([cw_all[16 * k:16 * k + 8] for k in range(N_CHIP)], axis=1)
    cvec = jnp.concatenate([cw_full[0:3], conv_b, jnp.zeros((4, ff2), F32)], axis=0)
    b_shard = lax.dynamic_slice_in_dim(b_ada, k_me * n_ada, n_ada, axis=1)
    mod_k, sc_all = _mod_shard(c_all, w_ada[0], b_shard)
    (mod_g,) = _run_rider(_allgather_rider([mod_k]), "gather_mod")
    mod_all = jnp.concatenate([mod_g[16 * k:16 * k + 8] for k in range(N_CHIP)], axis=1)
    mod_me = lax.dynamic_slice_in_dim(mod_all, dev, 1, axis=0).reshape(6, d)
    vecs = jnp.concatenate([mod_me, pre_mix_g, post_mix_g, pre_ffn_g, post_ffn_g, jnp.zeros((6, d), F32)], axis=0)

    fly_ffn = _gather_begin([w_up_s, w_down_s], [w_up_f, w_down_f], mix_flags, mod_g, "gather_begin_ffn")
    w_in_b, w_out_b = _gather_finish(_gather_end(fly_mix, mix_flags, fly_ffn[3], "gather_end_mix"), mix_flags,
                                     "gather_finish_mix")

    pos = jnp.arange(HEAD)
    causal = (pos[None, :] // CHUNK) <= (pos[:, None] // CHUNK)
    wm = jnp.where(causal[None], w_spatial[0], 0.0)
    wm_b = wm.astype(BF16)
    wmt_b = jnp.swapaxes(wm, 1, 2).astype(BF16)
    bsb = jnp.broadcast_to(b_spatial[0][:, :, None], (N_HEADS, HEAD, HEAD))
    wp_b = w_pool[0].astype(BF16)
    sgu_g = jnp.concatenate([sgu_norm_g[0], jnp.zeros((4, HEAD), F32)], axis=0)
    ps = jnp.concatenate([pool_scale, jnp.zeros((7, A_WIDTH), F32)], axis=0)

    pmats = _perm_mats(FFN_TS)
    h1, proj, cat, mixed, x2, h2p = _mix_fwd(x2d, vecs, w_in_b, w_out_b, sgu_g, wm_b, bsb, wp_b, ps, pmats, ts=512)[0]
    w_up_b, w_down_b = _gather_finish(_gather_end(fly_ffn, mix_flags, h2p, "gather_end_ffn"), mix_flags,
                                      "gather_finish_ffn")
    up, yv, act, dy, dfp, acc_f = _ffn_fwd(h2p, x2, tgt, w_up_b, w_down_b, cvec, vecs, pmats, ts=FFN_TS)

    c_idx = ci.reshape(1).astype(jnp.int32)
    g_w_down, _ = _wgrad(act, dfp, d, 1024, "wgrad_down")
    (dup, dx2, dmixed, acc_c, acc_b), (land_down,) = _ffn_bwd(
        dfp, up, yv, x2, dy, mixed, w_up_b, w_down_b, cvec, vecs, pmats, ts=FFN_TS,
        rider=_sibling_rider([g_w_down], (False,)))
    (part_down,) = _sum_with_sibling([g_w_down], [land_down], (False,), c_idx, "pair_sum_down")
    g_w_up, (chips_down,) = _wgrad(h2p, dup, ff2 // 2, 2048, "wgrad_up", rider=_chips_rider([part_down]))
    fly_up = _sibling_begin(g_w_up, True, chips_down, "sibling_begin_up")
    gx, dproj, acc_m, dwm, dwp, dbz, dsg, dps = _mix_bwd(
        dmixed, dx2, x2d, proj, vecs + fly_up[3][0:1, 0:1], w_in_b, w_out_b, sgu_g, wm_b, wmt_b, bsb, wp_b, ps,
        ts=256)[0]
    g_w_up, land_up = _sibling_end(fly_up, True, dproj, "sibling_end_up")
    (part_up,) = _sum_with_sibling([g_w_up], [land_up], (True,), c_idx, "pair_sum_up")
    fly_chips_up = _chips_begin([part_up], c_idx, "chips_begin_up")
    g1 = jnp.concatenate([acc_f, acc_b, acc_m], axis=0)
    hflat = (N_HEADS * HEAD, HEAD)
    small_bufs = [lax.dynamic_update_slice(jnp.zeros((N_DEV * t.shape[0], t.shape[1]), t.dtype), t,
                                           (dev * t.shape[0], 0))
                  for t in (g1, acc_c, dwm.reshape(hflat), dwp.reshape(hflat), dbz, dsg, dps)]
    fly_small = _split_begin(small_bufs, 3 * len(small_bufs), _small_ici_copies, fly_chips_up[3],
                             "small_gather_begin")
    g_w_out, g_w_in = _wgrad_pair(cat, dmixed, h1, dproj, fly_small[3], 1024, "wgrad_mix")
    land_mix = _run_rider(_sibling_rider([g_w_in, g_w_out], (True, False)), "reduce_to_sibling")
    parts_mix = _sum_with_sibling([g_w_in, g_w_out], land_mix, (True, False), c_idx, "pair_sum_mix")
    (part_up,), (chips_up,) = _chips_end(fly_chips_up, parts_mix[0], "chips_end_up")
    fly_chips_mix = _chips_begin(parts_mix, chips_up, "chips_begin_mix")

    def adamw_of(names, reduced):
        res = {}
        for nm, red in zip(names, reduced):
            w, m, v = big_wmv[nm]
            g = red.reshape(w.shape[1], w.shape[2])
            res[nm] = (g.reshape(w.shape),) + tuple(_adamw_big(g, w, m, v, "adamw_" + nm))
        return res

    big_wmv = {"w_in": (w_in, m_w_in, v_w_in), "w_out": (w_out, m_w_out, v_w_out),
               "w_up": (w_up, m_w_up, v_w_up), "w_down": (w_down, m_w_down, v_w_down)}
    big = adamw_of(("w_up", "w_down"), _sum_chips_and_share([chips_up, chips_down], [part_up, part_down],
                                                           fly_chips_mix[3], "sum_share_ffn"))

    gathered = _small_finish(_split_end(fly_small, _small_ici_copies, big["w_down"][1], "small_gather_end"),
                             "small_gather_finish")
    g1a, g2a, gwm, gwp, gbz, gsg, gps = [t.reshape((N_DEV, t.shape[0] // N_DEV, t.shape[1])) for t in gathered]
    g2s = lax.dynamic_slice_in_dim(g2a, k_me * n_cw, n_cw, axis=2)
    params = {
        "b_ada": (b_ada, m_b_ada, v_b_ada), "pre_mix_g": (pre_mix_g, m_pre_mix_g, v_pre_mix_g),
        "post_mix_g": (post_mix_g, m_post_mix_g, v_post_mix_g),
        "sgu_norm_g": (sgu_norm_g, m_sgu_norm_g, v_sgu_norm_g), "w_spatial": (w_spatial, m_w_spatial, v_w_spatial),
        "b_spatial": (b_spatial, m_b_spatial, v_b_spatial), "w_pool": (w_pool, m_w_pool, v_w_pool),
        "pool_scale": (pool_scale, m_pool_scale, v_pool_scale), "pre_ffn_g": (pre_ffn_g, m_pre_ffn_g, v_pre_ffn_g),
        "post_ffn_g": (post_ffn_g, m_post_ffn_g, v_post_ffn_g), "conv_w": (conv_w, m_conv_w, v_conv_w),
        "conv_b": (conv_b, m_conv_b, v_conv_b),
    }
    loss_slab, small = _small_update(g1a, g2a, g2s, gwm, gwp, gbz, gsg, gps, params)

    gmod_all = jnp.concatenate(
        [g1a[:, 16 + M_DSHM], g1a[:, 16 + M_DSCM], g1a[:, 8 + B_DGM], g1a[:, 8 + B_DSHF], g1a[:, 8 + B_DSCF],
         g1a[:, F_DGF]], axis=1)
    gm = lax.dynamic_slice_in_dim(gmod_all, k_me * n_ada, n_ada, axis=1)
    gm = jnp.concatenate([gm, jnp.zeros((HEAD - N_DEV, n_ada), F32)], axis=0)
    sct = jnp.concatenate([sc_all.T, jnp.zeros((d, HEAD - N_DEV), F32)], axis=1)
    ada = tuple(_wada_update(sct, gm, w_ada, m_w_ada, v_w_ada))

    parts_mix, chips_mix = _chips_end(fly_chips_mix, ada[1], "chips_end_mix")
    big.update(adamw_of(("w_in", "w_out"), _sum_chips_and_share(chips_mix, parts_mix, loss_slab, "sum_share_mix")))

    everything = dict(small)
    everything.update(big)
    everything["w_ada"] = ada
    order = ["w_ada", "b_ada", "pre_mix_g", "post_mix_g", "w_in", "sgu_norm_g", "w_spatial", "b_spatial", "w_pool",
             "pool_scale", "w_out", "pre_ffn_g", "post_ffn_g", "w_up", "conv_w", "conv_b", "w_down"]
    outs = [loss_slab[0, 0], gx.reshape(x.shape)]
    for j in range(4):
        outs += [everything[nm][j] for nm in order]
    return tuple(outs)
```

```python
import functools

import jax
import jax.numpy as jnp
from jax import lax
from jax.experimental import pallas as pl
from jax.experimental.pallas import tpu as pltpu

F32 = jnp.float32
BF16 = jnp.bfloat16
MESH = pl.DeviceIdType.MESH

EPS = 1e-6
HEAD = 128
N_HEADS = 4
A_WIDTH = N_HEADS * HEAD
CHUNK = 64
POOL_WINDOWS = (2, 4, 8, 16)
POOL_HALO = 16
FFN_TS = 256

ADAM_LR = 0.001
ADAM_B1 = 0.9
ADAM_B2 = 0.999
ADAM_EPS = 1e-08
ADAM_WD = 0.01
ADAM_STEP = 10

VMEM_LIMIT_BYTES = 58 * 1024 * 1024
N_DEV = 8
N_CHIP = 4


def _dot(a, b):
    return jnp.dot(a, b, preferred_element_type=F32)


def _dot_nt(a, b):
    return lax.dot_general(a, b, (((1,), (1,)), ((), ())), preferred_element_type=F32)


def _dot_tn(a, b):
    return lax.dot_general(a, b, (((0,), (0,)), ((), ())), preferred_element_type=F32)


def _gelu(x):
    return x * (0.5 * (1.0 + jnp.tanh(0.7978845608028654 * (x + 0.044715 * (x * x * x)))))


def _gelu_grad(x):
    t = jnp.tanh(0.7978845608028654 * (x + 0.044715 * (x * x * x)))
    return 0.5 * (1.0 + t) + (0.5 * x) * (1.0 - t * t) * (0.7978845608028654 * (1.0 + 0.134145 * (x * x)))


def _sigmoid(x):
    return 1.0 / (1.0 + jnp.exp(-x))


def _rms(x):
    return lax.rsqrt(jnp.mean(x * x, axis=-1, keepdims=True) + EPS)


def _colsum(x):
    return jnp.sum(x, axis=0, keepdims=True)


def _rowmean(x):
    return jnp.mean(x, axis=-1, keepdims=True)


def _tiled(shape, index_map):
    return pl.BlockSpec(shape, index_map)


def _resident(shape):
    nd = len(shape)
    return pl.BlockSpec(shape, lambda *_: (0,) * nd, pipeline_mode=pl.Buffered(1))


def _whole(shape):
    nd = len(shape)
    return pl.BlockSpec(shape, lambda *_: (0,) * nd)


def _seq_params():
    return pltpu.CompilerParams(dimension_semantics=("arbitrary",), vmem_limit_bytes=VMEM_LIMIT_BYTES)


def _ff_chunks(f, width=768):
    out, o = [], 0
    while o < f:
        w = min(width, f - o)
        out.append((o, w))
        o += w
    return out


def _pool_fwd(p, halo, g, t_glob):
    ext = jnp.concatenate([halo, p], axis=0)
    s = ext
    for step in range(g + 1):
        s = s + pltpu.roll(s, 1 << step, 0)
    cnt = jnp.minimum(t_glob + 1, POOL_WINDOWS[g]).astype(F32)
    return s[POOL_HALO:] / cnt - p


def _adamw_math(w, g, m, v):
    m = ADAM_B1 * m + (1.0 - ADAM_B1) * g
    v = ADAM_B2 * v + (1.0 - ADAM_B2) * (g * g)
    m_hat = m / (1.0 - ADAM_B1 ** ADAM_STEP)
    v_hat = v / (1.0 - ADAM_B2 ** ADAM_STEP)
    delta = -ADAM_LR * (m_hat / (jnp.sqrt(v_hat) + ADAM_EPS) + ADAM_WD * w)
    return delta, m, v


def _mesh_pos():
    return lax.axis_index("x"), lax.axis_index("y"), lax.axis_index("c")


class _Rider:
    def __init__(self, inputs, out_shape, sems, start, finish):
        self.inputs, self.out_shape, self.sems = list(inputs), list(out_shape), list(sems)
        self.start, self.finish = start, finish


def _call(body, *, name, grid, in_specs, out_specs, out_shape, scratch_shapes, args, rider=None):
    params = pltpu.CompilerParams(dimension_semantics=("arbitrary",) * len(grid), vmem_limit_bytes=VMEM_LIMIT_BYTES)
    if rider is None:
        res = pl.pallas_call(body, name=name, grid=grid, in_specs=in_specs, out_specs=out_specs, out_shape=out_shape,
                             scratch_shapes=scratch_shapes, compiler_params=params)(*args)
        return tuple(res), ()
    cuts = [len(in_specs), len(rider.inputs), len(out_specs), len(rider.out_shape), len(scratch_shapes),
            len(rider.sems)]

    def hosted(*refs):
        groups, a = [], 0
        for cnt in cuts:
            groups.append(refs[a:a + cnt])
            a += cnt
        ins, r_in, outs, r_out, scr, r_sem = groups
        first = functools.reduce(jnp.logical_and, [pl.program_id(k) == 0 for k in range(len(grid))])
        last = functools.reduce(jnp.logical_and, [pl.program_id(k) == grid[k] - 1 for k in range(len(grid))])

        @pl.when(first)
        def _():
            rider.start(r_in, r_out, r_sem)

        body(*ins, *outs, *scr)

        @pl.when(last)
        def _():
            rider.finish(r_in, r_out, r_sem)

    anyspec = pl.BlockSpec(memory_space=pl.ANY)
    res = pl.pallas_call(
        hosted, name=name, grid=grid,
        in_specs=list(in_specs) + [anyspec] * cuts[1], out_specs=list(out_specs) + [anyspec] * cuts[3],
        out_shape=list(out_shape) + rider.out_shape, scratch_shapes=list(scratch_shapes) + rider.sems,
        compiler_params=params)(*args, *rider.inputs)
    return tuple(res[:cuts[2]]), tuple(res[cuts[2]:])


def _run_rider(rider, name):
    n_in, n_out = len(rider.inputs), len(rider.out_shape)

    def body(*refs):
        r_in, r_out, r_sem = refs[:n_in], refs[n_in:n_in + n_out], refs[n_in + n_out:]
        rider.start(r_in, r_out, r_sem)
        rider.finish(r_in, r_out, r_sem)

    anyspec = pl.BlockSpec(memory_space=pl.ANY)
    return pl.pallas_call(body, name=name, out_shape=rider.out_shape, in_specs=[anyspec] * n_in,
                          out_specs=[anyspec] * n_out, scratch_shapes=rider.sems)(*rider.inputs)


def _allgather_rider(arrs):
    n = len(arrs)

    def plan(ins, outs, sems):
        send_sems, recv_sems, local_sems = sems
        x, y, c = _mesh_pos()
        me, sibling = (x, y, c), (x, y, 1 - c)
        chips = [(1 - x, y), (x, 1 - y), (1 - x, 1 - y)]

        def rows(a, px, py, pc):
            r = ins[a].shape[0]
            return outs[a].at[pl.ds(pl.multiple_of((4 * px + 2 * py + pc) * r, 8), r), :]

        def copy(a, k, block, to, src=None):
            return pltpu.make_async_remote_copy(
                src_ref=rows(a, *block) if src is None else src, dst_ref=rows(a, *block),
                send_sem=send_sems.at[a * 7 + k], recv_sem=recv_sems.at[a * 7 + k],
                device_id=to, device_id_type=MESH)

        local = [pltpu.make_async_copy(ins[a], rows(a, *me), local_sems.at[a]) for a in range(n)]
        first = []
        for a in range(n):
            first.append(copy(a, 0, me, sibling, src=ins[a]))
            first += [copy(a, 1 + j, me, (*chip, c), src=ins[a]) for j, chip in enumerate(chips)]
        return c, me, sibling, chips, copy, local, first

    def start(ins, outs, sems):
        *_, local, first = plan(ins, outs, sems)
        for cp in local + first:
            cp.start()

    def finish(ins, outs, sems):
        c, me, sibling, chips, copy, local, first = plan(ins, outs, sems)
        passed = []
        for a in range(n):
            for j, chip in enumerate(chips):
                copy(a, 1 + j, (*chip, c), me).wait_recv()
                fwd = copy(a, 4 + j, (*chip, c), sibling)
                fwd.start()
                passed.append(fwd)
        for a in range(n):
            copy(a, 0, sibling, me).wait_recv()
            for j, chip in enumerate(chips):
                copy(a, 4 + j, (*chip, 1 - c), me).wait_recv()
        for cp in first + passed:
            cp.wait_send()
        for mine in local:
            mine.wait()

    return _Rider(arrs, [jax.ShapeDtypeStruct((N_DEV * a.shape[0], a.shape[1]), a.dtype) for a in arrs],
                  [pltpu.SemaphoreType.DMA((7 * n,)), pltpu.SemaphoreType.DMA((7 * n,)),
                   pltpu.SemaphoreType.DMA((n,))], start, finish)


def _piece(ref, col_sharded, k, h):
    m, n = ref.shape
    if col_sharded:
        mh, nc = m // 2, n // N_CHIP
        return ref.at[pl.ds(pl.multiple_of(h * mh, 16), mh), pl.ds(pl.multiple_of(k * nc, 128), nc)]
    rp = m // (2 * N_CHIP)
    return ref.at[pl.ds(pl.multiple_of((2 * k + h) * rp, 16), rp), :]


def _piece_shape(shape, col_sharded):
    m, n = shape
    return (m // 2, n // N_CHIP) if col_sharded else (m // (2 * N_CHIP), n)


def _cast_bf16(arrs, col_flags, k_idx):
    n = len(arrs)

    def body(k_ref, *refs):
        for a in range(n):
            val = refs[a][...].astype(BF16)
            refs[n + a][...] = val
            refs[2 * n + a][...] = val

    whole = [pl.BlockSpec(a.shape, lambda i, k_ref: (0, 0)) for a in arrs]
    window = [pl.BlockSpec(a.shape, (lambda i, k_ref: (0, k_ref[0])) if col else (lambda i, k_ref: (k_ref[0], 0)))
              for a, col in zip(arrs, col_flags)]
    res = pl.pallas_call(
        body, name="cast_weights",
        grid_spec=pltpu.PrefetchScalarGridSpec(num_scalar_prefetch=1, grid=(1,), in_specs=whole,
                                               out_specs=whole + window),
        out_shape=[jax.ShapeDtypeStruct(a.shape, BF16) for a in arrs]
        + [jax.ShapeDtypeStruct(fs, BF16) for fs in _full_shapes(arrs, col_flags)],
        compiler_params=pltpu.CompilerParams(vmem_limit_bytes=VMEM_LIMIT_BYTES))(k_idx, *arrs)
    return list(res[:n]), list(res[n:])


def _full_shapes(shards, col_flags):
    return [(s.shape[0], s.shape[1] * N_CHIP) if col else (s.shape[0] * N_CHIP, s.shape[1])
            for s, col in zip(shards, col_flags)]


def _ici_copies(shard_refs, full_refs, send_sems, recv_sems, col_flags):
    x, y, c = _mesh_pos()
    k_me = 2 * x + y
    copies = []
    for a, (s_ref, f_ref) in enumerate(zip(shard_refs, full_refs)):
        rows = s_ref.shape[0] // 2
        src = s_ref.at[pl.ds(pl.multiple_of(c * rows, 16), rows), :]
        for j, chip in enumerate([(1 - x, y), (x, 1 - y), (1 - x, 1 - y)]):
            copies.append(pltpu.make_async_remote_copy(
                src_ref=src, dst_ref=_piece(f_ref, col_flags[a], k_me, c),
                send_sem=send_sems.at[a * 3 + j], recv_sem=recv_sems.at[a * 3 + j],
                device_id=(*chip, c), device_id_type=MESH))
    return copies


def _split_begin(bufs, n_sems, make_copies, after, name):
    n = len(bufs)
    hbm = pl.BlockSpec(memory_space=pltpu.HBM)
    sem = pl.BlockSpec(memory_space=pltpu.SEMAPHORE)

    def body(*refs):
        for cp in make_copies(refs[:n], refs[n + 1], refs[n + 2]):
            cp.start()
        refs[-1][...] = jnp.zeros_like(refs[-1])

    args = [pltpu.with_memory_space_constraint(t, pltpu.HBM) for t in bufs]
    res = pl.pallas_call(
        body, name=name,
        out_shape=[pltpu.SemaphoreType.DMA((n_sems,)), pltpu.SemaphoreType.DMA((n_sems,))]
        + [pltpu.HBM(t.shape, t.dtype) for t in args] + [jax.ShapeDtypeStruct((8, HEAD), F32)],
        in_specs=[hbm] * n + [pl.BlockSpec(memory_space=pl.ANY)],
        out_specs=[sem, sem] + [hbm] * n + [pl.BlockSpec(memory_space=pltpu.VMEM)],
        input_output_aliases={i: 2 + i for i in range(n)},
        compiler_params=pltpu.CompilerParams(has_side_effects=pltpu.SideEffectType.DATAFLOW_SIDE_EFFECTING),
    )(*args, after)
    return res[0], res[1], list(res[2:2 + n]), res[-1]


def _split_end(handle, make_copies, after, name):
    send_sems, recv_sems, bufs, _ = handle
    n = len(bufs)
    hbm = pl.BlockSpec(memory_space=pltpu.HBM)
    sem = pl.BlockSpec(memory_space=pltpu.SEMAPHORE)

    def body(*refs):
        for cp in make_copies(refs[:n], refs[n], refs[n + 1]):
            cp.wait_send()
            cp.wait_recv()

    res = pl.pallas_call(
        body, name=name,
        out_shape=[pltpu.HBM(t.shape, t.dtype) for t in bufs],
        in_specs=[hbm] * n + [sem, sem, pl.BlockSpec(memory_space=pl.ANY)],
        out_specs=[hbm] * n,
        input_output_aliases={i: i for i in range(n)},
        compiler_params=pltpu.CompilerParams(has_side_effects=pltpu.SideEffectType.DATAFLOW_SIDE_EFFECTING),
    )(*bufs, send_sems, recv_sems, after)
    return list(res)


def _gather_copies(n, col_flags):
    return lambda refs, send_sems, recv_sems: _ici_copies(refs[:n], refs[n:], send_sems, recv_sems, col_flags)


def _gather_begin(shards, fulls, col_flags, after, name):
    n = len(shards)
    return _split_begin(list(shards) + list(fulls), 3 * n, _gather_copies(n, col_flags), after, name)


def _gather_end(handle, col_flags, after, name):
    n = len(handle[2]) // 2
    return _split_end(handle, _gather_copies(n, col_flags), after, name)[n:]


def _sibling_copies(col_flag):
    def make(refs, send_sems, recv_sems):
        grad_ref, land_ref = refs
        x, y, c = _mesh_pos()
        return [pltpu.make_async_remote_copy(
            src_ref=_piece(grad_ref, col_flag, k, 1 - c), dst_ref=land_ref.at[k],
            send_sem=send_sems.at[k], recv_sem=recv_sems.at[k],
            device_id=(x, y, 1 - c), device_id_type=MESH) for k in range(N_CHIP)]
    return make


def _chips_copies(n):
    def make(refs, send_sems, recv_sems):
        parts, landed = refs[:n], refs[n:]
        x, y, c = _mesh_pos()
        k_me = 2 * x + y
        copies = []
        for a in range(n):
            for j, chip in enumerate([(1 - x, y), (x, 1 - y), (1 - x, 1 - y)]):
                copies.append(pltpu.make_async_remote_copy(
                    src_ref=parts[a].at[2 * chip[0] + chip[1]], dst_ref=landed[a].at[k_me],
                    send_sem=send_sems.at[a * 3 + j], recv_sem=recv_sems.at[a * 3 + j],
                    device_id=(*chip, c), device_id_type=MESH))
        return copies
    return make


def _chips_begin(parts, after, name):
    landed = [lax.empty(p.shape, p.dtype) for p in parts]
    return _split_begin(list(parts) + landed, 3 * len(parts), _chips_copies(len(parts)), after, name)


def _chips_end(handle, after, name):
    n = len(handle[2]) // 2
    res = _split_end(handle, _chips_copies(n), after, name)
    return res[:n], res[n:]


def _small_rows(buf_ref, px, py, pc):
    r = buf_ref.shape[0] // N_DEV
    return buf_ref.at[pl.ds(pl.multiple_of((4 * px + 2 * py + pc) * r, 8), r), :]


def _small_ici_copies(refs, send_sems, recv_sems):
    x, y, c = _mesh_pos()
    copies = []
    for a, buf in enumerate(refs):
        mine = _small_rows(buf, x, y, c)
        for j, chip in enumerate([(1 - x, y), (x, 1 - y), (1 - x, 1 - y)]):
            copies.append(pltpu.make_async_remote_copy(
                src_ref=mine, dst_ref=mine, send_sem=send_sems.at[a * 3 + j], recv_sem=recv_sems.at[a * 3 + j],
                device_id=(*chip, c), device_id_type=MESH))
    return copies


def _small_finish(bufs, name):
    n = len(bufs)

    def body(*refs):
        buf_refs = refs[n:2 * n]
        send_sems, recv_sems = refs[2 * n:]
        x, y, c = _mesh_pos()
        owners = [(x, y), (1 - x, y), (x, 1 - y), (1 - x, 1 - y)]
        passed, arriving = [], []
        for a in range(n):
            for j, (px, py) in enumerate(owners):
                for pc, group in ((c, passed), (1 - c, arriving)):
                    rows = _small_rows(buf_refs[a], px, py, pc)
                    group.append(pltpu.make_async_remote_copy(
                        src_ref=rows, dst_ref=rows, send_sem=send_sems.at[a * 4 + j],
                        recv_sem=recv_sems.at[a * 4 + j], device_id=(x, y, 1 - c), device_id_type=MESH))
        for cp in passed:
            cp.start()
        for cp in arriving:
            cp.wait_recv()
        for cp in passed:
            cp.wait_send()

    anyspec = pl.BlockSpec(memory_space=pl.ANY)
    return pl.pallas_call(
        body, name=name, out_shape=[jax.ShapeDtypeStruct(b.shape, b.dtype) for b in bufs],
        in_specs=[anyspec] * n, out_specs=[anyspec] * n, input_output_aliases={a: a for a in range(n)},
        scratch_shapes=[pltpu.SemaphoreType.DMA((4 * n,)), pltpu.SemaphoreType.DMA((4 * n,))],
    )(*bufs)


def _sibling_begin(grad, col_flag, after, name):
    land = lax.empty((N_CHIP,) + _piece_shape(grad.shape, col_flag), grad.dtype)
    return _split_begin([grad, land], N_CHIP, _sibling_copies(col_flag), after, name)


def _sibling_end(handle, col_flag, after, name):
    return _split_end(handle, _sibling_copies(col_flag), after, name)


def _gather_finish(fulls, col_flags, name):
    n = len(fulls)

    def body(*refs):
        full_refs = refs[n:2 * n]
        send_sems, recv_sems = refs[2 * n:]
        x, y, c = _mesh_pos()
        passed, arriving = [], []
        for a in range(n):
            for j, chip in enumerate([(1 - x, y), (x, 1 - y), (1 - x, 1 - y)]):
                k_from = 2 * chip[0] + chip[1]
                for h, group in ((c, passed), (1 - c, arriving)):
                    win = _piece(full_refs[a], col_flags[a], k_from, h)
                    group.append(pltpu.make_async_remote_copy(
                        src_ref=win, dst_ref=win, send_sem=send_sems.at[a * 3 + j],
                        recv_sem=recv_sems.at[a * 3 + j], device_id=(x, y, 1 - c), device_id_type=MESH))
        for cp in passed:
            cp.start()
        for cp in arriving:
            cp.wait_recv()
        for cp in passed:
            cp.wait_send()

    anyspec = pl.BlockSpec(memory_space=pl.ANY)
    return pl.pallas_call(
        body, name=name,
        out_shape=[jax.ShapeDtypeStruct(f.shape, f.dtype) for f in fulls],
        in_specs=[anyspec] * n, out_specs=[anyspec] * n,
        input_output_aliases={a: a for a in range(n)},
        scratch_shapes=[pltpu.SemaphoreType.DMA((3 * n,)), pltpu.SemaphoreType.DMA((3 * n,))],
    )(*fulls)


def _sibling_rider(grads, col_flags):
    n = len(grads)
    pshapes = [_piece_shape(g.shape, col) for g, col in zip(grads, col_flags)]

    def copies(ins, outs, sems):
        send_sems, recv_sems = sems
        x, y, c = _mesh_pos()
        return [pltpu.make_async_remote_copy(
            src_ref=_piece(ins[a], col_flags[a], k, 1 - c), dst_ref=outs[a].at[k],
            send_sem=send_sems.at[a * N_CHIP + k], recv_sem=recv_sems.at[a * N_CHIP + k],
            device_id=(x, y, 1 - c), device_id_type=MESH) for a in range(n) for k in range(N_CHIP)]

    def start(ins, outs, sems):
        for cp in copies(ins, outs, sems):
            cp.start()

    def finish(ins, outs, sems):
        cps = copies(ins, outs, sems)
        for cp in cps:
            cp.wait_recv()
        for cp in cps:
            cp.wait_send()

    return _Rider(grads, [jax.ShapeDtypeStruct((N_CHIP,) + ps, g.dtype) for ps, g in zip(pshapes, grads)],
                  [pltpu.SemaphoreType.DMA((N_CHIP * n,)), pltpu.SemaphoreType.DMA((N_CHIP * n,))], start, finish)


def _sum_with_sibling(grads, landed, col_flags, c_idx, name):
    n = len(grads)
    pshapes = [_piece_shape(g.shape, col) for g, col in zip(grads, col_flags)]

    def body(c_ref, *refs):
        ins, lands, outs = refs[:n], refs[n:2 * n], refs[2 * n:]
        for a in range(n):
            outs[a][0] = (ins[a][...] + lands[a][0]).astype(BF16)

    in_specs = []
    for ps, col in zip(pshapes, col_flags):
        if col:
            in_specs.append(pl.BlockSpec(ps, lambda k, c_ref: (c_ref[0], k)))
        else:
            in_specs.append(pl.BlockSpec(ps, lambda k, c_ref: (2 * k + c_ref[0], 0)))
    land_specs = [pl.BlockSpec((1,) + ps, lambda k, c_ref: (k, 0, 0)) for ps in pshapes]
    return pl.pallas_call(
        body, name=name,
        grid_spec=pltpu.PrefetchScalarGridSpec(
            num_scalar_prefetch=1, grid=(N_CHIP,),
            in_specs=in_specs + land_specs, out_specs=land_specs),
        out_shape=[jax.ShapeDtypeStruct((N_CHIP,) + ps, BF16) for ps in pshapes],
        compiler_params=pltpu.CompilerParams(dimension_semantics=("arbitrary",), vmem_limit_bytes=VMEM_LIMIT_BYTES),
    )(c_idx, *grads, *landed)


def _chips_rider(parts):
    n = len(parts)

    def plan(ins, outs, sems, arriving):
        send_sems, recv_sems = sems
        x, y, c = _mesh_pos()
        k_me = 2 * x + y
        copies = []
        for a in range(n):
            for j, chip in enumerate([(1 - x, y), (x, 1 - y), (1 - x, 1 - y)]):
                k_peer = 2 * chip[0] + chip[1]
                copies.append(pltpu.make_async_remote_copy(
                    src_ref=ins[a].at[k_peer], dst_ref=outs[a].at[k_peer if arriving else k_me],
                    send_sem=send_sems.at[a * 3 + j], recv_sem=recv_sems.at[a * 3 + j],
                    device_id=(*chip, c), device_id_type=MESH))
        return copies

    def start(ins, outs, sems):
        for cp in plan(ins, outs, sems, False):
            cp.start()

    def finish(ins, outs, sems):
        arrivals = plan(ins, outs, sems, True)
        for cp in arrivals:
            cp.wait_recv()
        for cp in arrivals:
            cp.wait_send()

    return _Rider(parts, [jax.ShapeDtypeStruct(p.shape, p.dtype) for p in parts],
                  [pltpu.SemaphoreType.DMA((3 * n,)), pltpu.SemaphoreType.DMA((3 * n,))], start, finish)


def _sum_chips_and_share(landed, parts, after, name):
    n = len(landed)

    def body(*refs):
        ins, own, outs, red = refs[:n], refs[n:2 * n], refs[2 * n + 1:3 * n + 1], refs[3 * n + 1:4 * n + 1]
        send_sems, recv_sems, local_sems = refs[4 * n + 1:]
        x, y, c = _mesh_pos()
        sibling = (x, y, 1 - c)
        k_me = 2 * x + y
        copies, local = [], []
        for a in range(n):
            for k in range(N_CHIP):
                @pl.when(k_me == k)
                def _():
                    term = own[a][k].astype(F32)
                    red[a][...] = term if k == 0 else red[a][...] + term

                @pl.when(k_me != k)
                def _():
                    term = ins[a][k].astype(F32)
                    red[a][...] = term if k == 0 else red[a][...] + term

            mine = pltpu.make_async_copy(red[a], outs[a].at[c], local_sems.at[a])
            mine.start()
            local.append(mine)
            cp = pltpu.make_async_remote_copy(
                src_ref=red[a], dst_ref=outs[a].at[c],
                send_sem=send_sems.at[a], recv_sem=recv_sems.at[a],
                device_id=sibling, device_id_type=MESH)
            cp.start()
            copies.append(cp)
        for a in range(n):
            pltpu.make_async_remote_copy(
                src_ref=red[a], dst_ref=outs[a].at[1 - c],
                send_sem=send_sems.at[a], recv_sem=recv_sems.at[a],
                device_id=sibling, device_id_type=MESH).wait_recv()
        for cp in copies:
            cp.wait_send()
        for mine in local:
            mine.wait()

    return pl.pallas_call(
        body, name=name,
        out_shape=[jax.ShapeDtypeStruct((2,) + l.shape[1:], F32) for l in landed],
        in_specs=[pl.BlockSpec(memory_space=pltpu.VMEM)] * (2 * n + 1),
        out_specs=[pl.BlockSpec(memory_space=pl.ANY)] * n,
        scratch_shapes=[pltpu.VMEM(l.shape[1:], F32) for l in landed]
        + [pltpu.SemaphoreType.DMA((n,)), pltpu.SemaphoreType.DMA((n,)), pltpu.SemaphoreType.DMA((n,))],
        compiler_params=pltpu.CompilerParams(vmem_limit_bytes=VMEM_LIMIT_BYTES),
    )(*landed, *parts, after)


def _mod_shard(c_all, w_ada, b_shard):
    def body(c_ref, w_ref, b_ref, o_ref, sc_ref):
        cc = c_ref[...]
        sc = cc * _sigmoid(cc)
        sc_ref[...] = sc
        o_ref[...] = _dot(sc, w_ref[...]) + b_ref[...]

    nb, d = c_all.shape
    nn = w_ada.shape[1]
    return pl.pallas_call(
        body, name="mod_shard",
        out_shape=[jax.ShapeDtypeStruct((nb, nn), F32), jax.ShapeDtypeStruct((nb, d), F32)],
        compiler_params=pltpu.CompilerParams(vmem_limit_bytes=VMEM_LIMIT_BYTES),
    )(c_all, w_ada, b_shard)


V_SH_M, V_SC_M, V_G_M, V_SH_F, V_SC_F, V_G_F, V_PRE_MIX, V_POST_MIX, V_PRE_FFN, V_POST_FFN = range(10)


def _vrow(vec_ref, r):
    return vec_ref[r:r + 1, :]


def _mix_fwd(x, vecs, w_in_b, w_out_b, sgu_g, wm_b, bsb, wp_b, ps, pmats, ts):
    s_len, d = x.shape
    nt = s_len // ts
    nblk = ts // HEAD
    n_proj = w_in_b.shape[1]

    def body(x_ref, vec_ref, win_ref, wout_ref, sg_ref, wm_ref, bs_ref, wp_ref, ps_ref, pm_ref,
             h1_ref, proj_ref, cat_ref, mixed_ref, x2_ref, h2_ref, carry_ref):
        i = pl.program_id(0)

        @pl.when(i == 0)
        def _():
            carry_ref[...] = jnp.zeros_like(carry_ref)

        x = x_ref[...]
        h1 = (((x * _rms(x)) * _vrow(vec_ref, V_PRE_MIX)) * (1.0 + _vrow(vec_ref, V_SC_M))
              + _vrow(vec_ref, V_SH_M)).astype(BF16)
        h1_ref[...] = h1
        proj = _dot(h1, win_ref[...])
        proj_ref[...] = proj.astype(BF16)
        t_glob = lax.broadcasted_iota(jnp.int32, (ts, HEAD), 0) + i * ts
        for h in range(N_HEADS):
            u = _gelu(proj[:, h * HEAD:(h + 1) * HEAD])
            v = _gelu(proj[:, A_WIDTH + h * HEAD:A_WIDTH + (h + 1) * HEAD])
            vn = ((v * _rms(v)) * sg_ref[h:h + 1, :]).astype(BF16)
            for b in range(nblk):
                rs = slice(b * HEAD, (b + 1) * HEAD)
                z = _dot(wm_ref[h], vn[rs]) + bs_ref[h]
                cat_ref[rs, h * HEAD:(h + 1) * HEAD] = (u[rs] * z).astype(BF16)
        for g in range(len(POOL_WINDOWS)):
            gs = slice(g * HEAD, (g + 1) * HEAD)
            p = proj[:, 2 * A_WIDTH + g * HEAD:2 * A_WIDTH + (g + 1) * HEAD]
            pooled = _pool_fwd(p, carry_ref[:, gs], g, t_glob)
            yb = _dot(pooled.astype(BF16), wp_ref[g]) * ps_ref[0:1, gs]
            cat_ref[:, A_WIDTH + g * HEAD:A_WIDTH + (g + 1) * HEAD] = yb.astype(BF16)
        carry_ref[...] = proj[ts - POOL_HALO:ts, 2 * A_WIDTH:]
        mixed = _dot(cat_ref[...], wout_ref[...])
        mixed_ref[...] = mixed
        x2 = x + _vrow(vec_ref, V_G_M) * ((mixed * _rms(mixed)) * _vrow(vec_ref, V_POST_MIX))
        x2_ref[...] = x2
        h2 = (((x2 * _rms(x2)) * _vrow(vec_ref, V_PRE_FFN)) * (1.0 + _vrow(vec_ref, V_SC_F))
              + _vrow(vec_ref, V_SH_F)).astype(BF16)
        for b in range(ts // FFN_TS):
            rs = slice(b * FFN_TS, (b + 1) * FFN_TS)
            h2_ref[rs, :] = _permute_bf16(pm_ref[0], h2[rs])

    row = lambda i: (i, 0)
    return _call(
        body, name="mix_fwd", grid=(nt,),
        in_specs=[_tiled((ts, d), row), _whole(vecs.shape), _resident(w_in_b.shape), _resident(w_out_b.shape),
                  _whole(sgu_g.shape), _whole(wm_b.shape), _whole(bsb.shape), _whole(wp_b.shape), _whole(ps.shape),
                  _whole(pmats.shape)],
        out_specs=[_tiled((ts, d), row), _tiled((ts, n_proj), row), _tiled((ts, d), row),
                   _tiled((ts, d), row), _tiled((ts, d), row), _tiled((ts, d), row)],
        out_shape=[jax.ShapeDtypeStruct((s_len, d), BF16), jax.ShapeDtypeStruct((s_len, n_proj), BF16),
                   jax.ShapeDtypeStruct((s_len, d), BF16), jax.ShapeDtypeStruct((s_len, d), F32),
                   jax.ShapeDtypeStruct((s_len, d), F32), jax.ShapeDtypeStruct((s_len, d), BF16)],
        scratch_shapes=[pltpu.VMEM((POOL_HALO, A_WIDTH), F32)],
        args=(x, vecs, w_in_b, w_out_b, sgu_g, wm_b, bsb, wp_b, ps, pmats))


def _perm_mats(ts):
    p = jnp.arange(ts)
    pm = (((p % 8) * (ts // 8) + p // 8)[:, None] == p[None, :]).astype(BF16)
    return jnp.stack([pm, pm.T])


def _permute_bf16(pm, xb):
    return _dot(pm, xb).astype(BF16)


def _permute_f32(pm, x):
    hi = x.astype(BF16)
    lo = (x - hi.astype(F32)).astype(BF16)
    return _dot(pm, hi) + _dot(pm, lo)


def _conv_out(u, um2, um1, cv_ref, cols):
    return (cv_ref[3:4, cols] + um2 * cv_ref[0:1, cols] + um1 * cv_ref[1:2, cols] + u * cv_ref[2:3, cols])


F_LOSS, F_DGF, F_DPOSTFFN = 0, 1, 2
B_DSHF, B_DSCF, B_DPREFFN, B_DGM, B_DPOSTMIX = 0, 1, 2, 3, 4
M_DSHM, M_DSCM, M_DPREMIX = 0, 1, 2
C_DCB, C_DCW = 0, 1


def _ffn_fwd(h2p, x2, tgt, w_up_b, w_down_b, cvec, vecs, pmats, ts):
    s_len, d = x2.shape
    ff2 = w_up_b.shape[1]
    ff = ff2 // 2
    nt = s_len // ts
    chunks = _ff_chunks(ff)

    def body(h2_ref, x2_ref, t_ref, wu_ref, wd_ref, cv_ref, vec_ref, pm_ref,
             up_ref, y_ref, act_ref, dy_ref, df_ref, acc_ref, carry_ref):
        @pl.when(pl.program_id(0) == 0)
        def _():
            carry_ref[...] = jnp.zeros_like(carry_ref)
            acc_ref[...] = jnp.zeros_like(acc_ref)

        h2v = h2_ref[...]

        def up_dots(o, w):
            return [_dot(h2v, wu_ref[:, base + o:base + o + w]) for base in (0, ff)]

        f = None
        pending = None
        nxt = up_dots(*chunks[0])
        for ci, (o, w) in enumerate(chunks):
            us = nxt
            if ci + 1 < len(chunks):
                nxt = up_dots(*chunks[ci + 1])
            if pending is not None:
                part = _dot(pending[0], wd_ref[pending[1]:pending[1] + pending[2], :])
                f = part if f is None else f + part
            sub0 = lax.broadcasted_iota(jnp.int32, (8, w), 0) == 0
            ys = []
            for base, u in zip((0, ff), us):
                cols = slice(base + o, base + o + w)
                up_ref[:, cols] = u.astype(BF16)
                b1 = jnp.where(sub0, pltpu.roll(carry_ref[8:16, cols], 1, 0), pltpu.roll(u[ts - 8:ts], 1, 0))
                b2 = jnp.where(sub0, pltpu.roll(carry_ref[0:8, cols], 1, 0), pltpu.roll(u[ts - 16:ts - 8], 1, 0))
                um1 = jnp.concatenate([b1, u[:ts - 8]], axis=0)
                um2 = jnp.concatenate([b2, b1, u[:ts - 16]], axis=0)
                ys.append(_conv_out(u, um2, um1, cv_ref, cols))
                carry_ref[:, cols] = u[ts - 16:ts]
            gate, val = ys
            sg = _sigmoid(gate)
            gs = gate * sg
            act = (gs * val).astype(BF16)
            act_ref[:, o:o + w] = act
            y_ref[:, o:o + w] = (val * (sg + gs * (1.0 - sg))).astype(BF16)
            y_ref[:, ff + o:ff + o + w] = gs.astype(BF16)
            pending = (act, o, w)
        f = f + _dot(pending[0], wd_ref[pending[1]:pending[1] + pending[2], :])
        f = _permute_f32(pm_ref[1], f)
        r3 = _rms(f)
        fhat = f * r3
        post = _vrow(vec_ref, V_POST_FFN)
        g_f = _vrow(vec_ref, V_G_F)
        fn = fhat * post
        e = (x2_ref[...] + g_f * fn) - t_ref[...]
        dy = e * (1.0 / d)
        dy_ref[...] = dy
        dfn = dy * g_f
        acc_ref[F_LOSS:F_LOSS + 1, :] += _colsum(e * e)
        acc_ref[F_DGF:F_DGF + 1, :] += _colsum(dy * fn)
        acc_ref[F_DPOSTFFN:F_DPOSTFFN + 1, :] += _colsum(dfn * fhat)
        dfhat = dfn * post
        df = (r3 * (dfhat - fhat * _rowmean(dfhat * fhat))).astype(BF16)
        df_ref[...] = _permute_bf16(pm_ref[0], df)

    row = lambda i: (i, 0)
    return pl.pallas_call(
        body, name="ffn_fwd", grid=(nt,),
        in_specs=[_tiled((ts, d), row), _tiled((ts, d), row), _tiled((ts, d), row), _resident(w_up_b.shape),
                  _resident(w_down_b.shape), _whole(cvec.shape), _whole(vecs.shape), _whole(pmats.shape)],
        out_specs=[_tiled((ts, ff2), row), _tiled((ts, ff2), row), _tiled((ts, ff), row), _tiled((ts, d), row),
                   _tiled((ts, d), row), _whole((8, d))],
        out_shape=[jax.ShapeDtypeStruct((s_len, ff2), BF16), jax.ShapeDtypeStruct((s_len, ff2), BF16),
                   jax.ShapeDtypeStruct((s_len, ff), BF16), jax.ShapeDtypeStruct((s_len, d), F32),
                   jax.ShapeDtypeStruct((s_len, d), BF16), jax.ShapeDtypeStruct((8, d), F32)],
        scratch_shapes=[pltpu.VMEM((16, ff2), F32)],
        compiler_params=_seq_params(),
    )(h2p, x2, tgt, w_up_b, w_down_b, cvec, vecs, pmats)


def _ffn_bwd(dfp, upp, yp, x2, dy, mixed, w_up_b, w_down_b, cvec, vecs, pmats, ts, rider=None):
    s_len, d = x2.shape
    ff2 = w_up_b.shape[1]
    ff = ff2 // 2
    nt = s_len // ts
    chunks = _ff_chunks(ff, 512)

    def body(df_ref, up_ref, y_ref, x2_ref, dy_ref, mx_ref, wu_ref, wd_ref, cv_ref, vec_ref, pm_ref,
             dup_ref, dx2_ref, dmx_ref, accc_ref, acc_ref, carry_ref):
        @pl.when(pl.program_id(0) == 0)
        def _():
            carry_ref[...] = jnp.zeros_like(carry_ref)
            accc_ref[...] = jnp.zeros_like(accc_ref)
            acc_ref[...] = jnp.zeros_like(acc_ref)

        dfv = df_ref[...]

        def dh2_add(acc, dups, o, w):
            for base, dup in zip((0, ff), dups):
                part = _dot_nt(dup, wu_ref[:, base + o:base + o + w])
                acc = part if acc is None else acc + part
            return acc

        dh2 = None
        pending = None
        nxt = _dot_nt(dfv, wd_ref[chunks[0][0]:chunks[0][0] + chunks[0][1], :])
        for ci, (o, w) in enumerate(chunks):
            dact = nxt
            if ci + 1 < len(chunks):
                o2, w2 = chunks[ci + 1]
                nxt = _dot_nt(dfv, wd_ref[o2:o2 + w2, :])
            if pending is not None:
                dh2 = dh2_add(dh2, *pending)
            sub7 = lax.broadcasted_iota(jnp.int32, (8, w), 0) == 7
            dups = []
            dys = (dact * y_ref[:, o:o + w].astype(F32), dact * y_ref[:, ff + o:ff + o + w].astype(F32))
            for base, dyv in zip((0, ff), dys):
                cols = slice(base + o, base + o + w)
                u = up_ref[:, cols].astype(F32)
                e0 = jnp.where(sub7, pltpu.roll(carry_ref[0:8, cols], 7, 0), pltpu.roll(dyv[0:8], 7, 0))
                e1 = jnp.where(sub7, pltpu.roll(carry_ref[8:16, cols], 7, 0), pltpu.roll(dyv[8:16], 7, 0))
                dyp1 = jnp.concatenate([dyv[8:], e0], axis=0)
                dyp2 = jnp.concatenate([dyv[16:], e0, e1], axis=0)
                accc_ref[C_DCB:C_DCB + 1, cols] += _colsum(dyv)
                accc_ref[C_DCW + 0:C_DCW + 1, cols] += _colsum(dyp2 * u)
                accc_ref[C_DCW + 1:C_DCW + 2, cols] += _colsum(dyp1 * u)
                accc_ref[C_DCW + 2:C_DCW + 3, cols] += _colsum(dyv * u)
                dup = (dyv * cv_ref[2:3, cols] + dyp1 * cv_ref[1:2, cols] + dyp2 * cv_ref[0:1, cols]).astype(BF16)
                dup_ref[:, cols] = dup
                dups.append(dup)
                carry_ref[:, cols] = dyv[0:16]
            pending = (dups, o, w)
        dh2 = dh2_add(dh2, *pending)
        dh2 = _permute_f32(pm_ref[1], dh2)
        x2 = x2_ref[...]
        r2 = _rms(x2)
        xn = x2 * r2
        pre = _vrow(vec_ref, V_PRE_FFN)
        one_sc = 1.0 + _vrow(vec_ref, V_SC_F)
        acc_ref[B_DSHF:B_DSHF + 1, :] += _colsum(dh2)
        acc_ref[B_DSCF:B_DSCF + 1, :] += _colsum(dh2 * (xn * pre))
        acc_ref[B_DPREFFN:B_DPREFFN + 1, :] += _colsum(dh2 * xn * one_sc)
        dxn = dh2 * pre * one_sc
        dx2 = dy_ref[...] + r2 * (dxn - xn * _rowmean(dxn * xn))
        dx2_ref[...] = dx2
        mixed = mx_ref[...]
        rm = _rms(mixed)
        mhat = mixed * rm
        post = _vrow(vec_ref, V_POST_MIX)
        acc_ref[B_DGM:B_DGM + 1, :] += _colsum(dx2 * (mhat * post))
        dmn = dx2 * _vrow(vec_ref, V_G_M)
        acc_ref[B_DPOSTMIX:B_DPOSTMIX + 1, :] += _colsum(dmn * mhat)
        dmhat = dmn * post
        dmx_ref[...] = (rm * (dmhat - mhat * _rowmean(dmhat * mhat))).astype(BF16)

    rev = lambda i: (nt - 1 - i, 0)
    return _call(
        body, name="ffn_bwd", grid=(nt,), rider=rider,
        in_specs=[_tiled((ts, d), rev), _tiled((ts, ff2), rev), _tiled((ts, ff2), rev), _tiled((ts, d), rev),
                  _tiled((ts, d), rev), _tiled((ts, d), rev), _resident(w_up_b.shape), _resident(w_down_b.shape),
                  _whole(cvec.shape), _whole(vecs.shape), _whole(pmats.shape)],
        out_specs=[_tiled((ts, ff2), rev), _tiled((ts, d), rev), _tiled((ts, d), rev), _whole((8, ff2)),
                   _whole((8, d))],
        out_shape=[jax.ShapeDtypeStruct((s_len, ff2), BF16), jax.ShapeDtypeStruct((s_len, d), F32),
                   jax.ShapeDtypeStruct((s_len, d), BF16), jax.ShapeDtypeStruct((8, ff2), F32),
                   jax.ShapeDtypeStruct((8, d), F32)],
        scratch_shapes=[pltpu.VMEM((16, ff2), F32)],
        args=(dfp, upp, yp, x2, dy, mixed, w_up_b, w_down_b, cvec, vecs, pmats))


def _mix_bwd(dmixed, dx2, x, proj, vecs, w_in_b, w_out_b, sgu_g, wm_b, wmt_b, bsb, wp_b, ps, ts, rider=None):
    s_len, d = x.shape
    nt = s_len // ts
    nblk = ts // HEAD
    n_proj = proj.shape[1]
    per = ts // POOL_HALO
    ext_len = ts + POOL_HALO

    def body(dmx_ref, dx2_ref, x_ref, proj_ref, projh_ref, vec_ref, win_ref, wout_ref, sg_ref, wm_ref, wmt_ref,
             bs_ref, wp_ref, ps_ref,
             gx_ref, dproj_ref, acc_ref, dwm_out, dwp_out, db_ref, dg_ref, dps_ref,
             carry_ref, dwm_ref, dwp_ref, dbz_ref):
        i = pl.program_id(0)
        tile = nt - 1 - i

        @pl.when(i == 0)
        def _():
            carry_ref[...] = jnp.zeros_like(carry_ref)
            for r in (acc_ref, dwm_ref, dwp_ref, dbz_ref, dg_ref, dps_ref):
                r[...] = jnp.zeros_like(r)

        dcat = _dot_nt(dmx_ref[...], wout_ref[...])
        t_glob = lax.broadcasted_iota(jnp.int32, (ts, HEAD), 0) + tile * ts
        for h in range(N_HEADS):
            hs = slice(h * HEAD, (h + 1) * HEAD)
            vs = slice(A_WIDTH + h * HEAD, A_WIDTH + (h + 1) * HEAD)
            au = proj_ref[:, hs].astype(F32)
            av = proj_ref[:, vs].astype(F32)
            u = _gelu(au)
            v = _gelu(av)
            rv = _rms(v)
            vhat = v * rv
            gain = sg_ref[h:h + 1, :]
            vn = (vhat * gain).astype(BF16)
            dout = dcat[:, hs]
            du_parts, dvn_parts = [], []
            for b in range(nblk):
                rs = slice(b * HEAD, (b + 1) * HEAD)
                z = _dot(wm_ref[h], vn[rs]) + bs_ref[h]
                du_parts.append(dout[rs] * z)
                dz = dout[rs] * u[rs]
                dbz_ref[h] += dz
                dzb = dz.astype(BF16)
                dwm_ref[h] += _dot_nt(dzb, vn[rs])
                dvn_parts.append(_dot(wmt_ref[h], dzb))
            du = jnp.concatenate(du_parts, axis=0)
            dvn = jnp.concatenate(dvn_parts, axis=0)
            dg_ref[h:h + 1, :] += _colsum(dvn * vhat)
            dvhat = dvn * gain
            dv = rv * (dvhat - vhat * _rowmean(dvhat * vhat))
            dproj_ref[:, hs] = (du * _gelu_grad(au)).astype(BF16)
            dproj_ref[:, vs] = (dv * _gelu_grad(av)).astype(BF16)
        for g in range(len(POOL_WINDOWS)):
            gs = slice(g * HEAD, (g + 1) * HEAD)
            pcols = slice(2 * A_WIDTH + g * HEAD, 2 * A_WIDTH + (g + 1) * HEAD)
            p = proj_ref[:, pcols].astype(F32)
            halo = jnp.where(tile > 0, projh_ref[:, pcols].astype(F32), 0.0)
            pb = _pool_fwd(p, halo, g, t_glob).astype(BF16)
            dyb = dcat[:, A_WIDTH + g * HEAD:A_WIDTH + (g + 1) * HEAD]
            dps_ref[0:1, gs] += _colsum(dyb * _dot(pb, wp_ref[g]))
            dyl = (dyb * ps_ref[0:1, gs]).astype(BF16)
            dwp_ref[g] += _dot_tn(pb, dyl)
            dpooled = _dot_nt(dyl, wp_ref[g])
            cnt = jnp.minimum(t_glob + 1, POOL_WINDOWS[g]).astype(F32)
            q = dpooled / cnt
            s = jnp.concatenate([q, carry_ref[:, gs]], axis=0)
            for step in range(g + 1):
                s = s + pltpu.roll(s, ext_len - (1 << step), 0)
            dproj_ref[:, pcols] = (s[:ts] - dpooled).astype(BF16)
            carry_ref[:, gs] = q[0:POOL_HALO]
        dh1 = _dot_nt(dproj_ref[...], win_ref[...])
        x = x_ref[...]
        r1 = _rms(x)
        xn = x * r1
        pre = _vrow(vec_ref, V_PRE_MIX)
        one_sc = 1.0 + _vrow(vec_ref, V_SC_M)
        acc_ref[M_DSHM:M_DSHM + 1, :] += _colsum(dh1)
        acc_ref[M_DSCM:M_DSCM + 1, :] += _colsum(dh1 * (xn * pre))
        acc_ref[M_DPREMIX:M_DPREMIX + 1, :] += _colsum(dh1 * xn * one_sc)
        dxn = dh1 * pre * one_sc
        gx_ref[...] = dx2_ref[...] + r1 * (dxn - xn * _rowmean(dxn * xn))

        @pl.when(i == nt - 1)
        def _():
            dwm_out[...] = dwm_ref[...].astype(BF16)
            dwp_out[...] = dwp_ref[...].astype(BF16)
            db_ref[...] = jnp.zeros_like(db_ref)
            for h in range(N_HEADS):
                db_ref[h:h + 1, :] = jnp.sum(dbz_ref[h].T, axis=0, keepdims=True)

    rev = lambda i: (nt - 1 - i, 0)
    halo_map = lambda i: (jnp.maximum((nt - 1 - i) * per - 1, 0), 0)
    hshape = (N_HEADS, HEAD, HEAD)
    return _call(
        body, name="mix_bwd", grid=(nt,), rider=rider,
        in_specs=[_tiled((ts, d), rev), _tiled((ts, d), rev), _tiled((ts, d), rev), _tiled((ts, n_proj), rev),
                  _tiled((POOL_HALO, n_proj), halo_map), _whole(vecs.shape), _resident(w_in_b.shape),
                  _resident(w_out_b.shape), _whole(sgu_g.shape), _whole(wm_b.shape), _whole(wmt_b.shape),
                  _whole(bsb.shape), _whole(wp_b.shape), _whole(ps.shape)],
        out_specs=[_tiled((ts, d), rev), _tiled((ts, n_proj), rev), _whole((8, d)), _whole(hshape), _whole(hshape),
                   _whole((8, HEAD)), _whole((8, HEAD)), _whole((8, A_WIDTH))],
        out_shape=[jax.ShapeDtypeStruct((s_len, d), F32), jax.ShapeDtypeStruct((s_len, n_proj), BF16),
                   jax.ShapeDtypeStruct((8, d), F32), jax.ShapeDtypeStruct(hshape, BF16),
                   jax.ShapeDtypeStruct(hshape, BF16), jax.ShapeDtypeStruct((8, HEAD), F32),
                   jax.ShapeDtypeStruct((8, HEAD), F32), jax.ShapeDtypeStruct((8, A_WIDTH), F32)],
        scratch_shapes=[pltpu.VMEM((POOL_HALO, A_WIDTH), F32), pltpu.VMEM(hshape, F32), pltpu.VMEM(hshape, F32),
                        pltpu.VMEM(hshape, F32)],
        args=(dmixed, dx2, x, proj, proj, vecs, w_in_b, w_out_b, sgu_g, wm_b, wmt_b, bsb, wp_b, ps))


def _wgrad(a, b, tn, ts, name, rider=None):
    s_len, m = a.shape
    n = b.shape[1]
    ts = min(ts, s_len)

    def body(a_ref, b_ref, o_ref):
        @pl.when(pl.program_id(1) == 0)
        def _():
            o_ref[...] = jnp.zeros_like(o_ref)

        o_ref[...] += _dot_tn(a_ref[...], b_ref[...])

    (g,), r_out = _call(
        body, name=name, grid=(n // tn, s_len // ts), rider=rider,
        in_specs=[pl.BlockSpec((ts, m), lambda j, s: (s, 0)), pl.BlockSpec((ts, tn), lambda j, s: (s, j))],
        out_specs=[pl.BlockSpec((m, tn), lambda j, s: (0, j))],
        out_shape=[jax.ShapeDtypeStruct((m, n), F32)], scratch_shapes=[], args=(a, b))
    return g, r_out


def _wgrad_pair(a1, b1, a2, b2, after, ts, name):
    s_len = a1.shape[0]
    ts = min(ts, s_len)
    shapes = [(a1.shape[1], b1.shape[1]), (a2.shape[1], b2.shape[1])]

    def body(a1_ref, b1_ref, a2_ref, b2_ref, after_ref, o1_ref, o2_ref):
        @pl.when(pl.program_id(0) == 0)
        def _():
            o1_ref[...] = jnp.zeros_like(o1_ref)
            o2_ref[...] = jnp.zeros_like(o2_ref)

        o1_ref[...] += _dot_tn(a1_ref[...], b1_ref[...])
        o2_ref[...] += _dot_tn(a2_ref[...], b2_ref[...])

    row = lambda s: (s, 0)
    return _call(
        body, name=name, grid=(s_len // ts,),
        in_specs=[pl.BlockSpec((ts, t.shape[1]), row) for t in (a1, b1, a2, b2)] + [_whole(after.shape)],
        out_specs=[_whole(sh) for sh in shapes],
        out_shape=[jax.ShapeDtypeStruct(sh, F32) for sh in shapes], scratch_shapes=[],
        args=(a1, b1, a2, b2, after))[0]


def _adamw_big(g, w, m, v, name):
    r, cdim = g.shape
    tr = r
    while tr * cdim * 4 > (3 << 19) and tr % 16 == 0:
        tr //= 2

    def body(g_ref, w_ref, m_ref, v_ref, d_ref, nm_ref, nv_ref):
        delta, m2, v2 = _adamw_math(w_ref[0], g_ref[...], m_ref[0], v_ref[0])
        d_ref[0] = delta
        nm_ref[0] = m2
        nv_ref[0] = v2

    s3 = pl.BlockSpec((1, tr, cdim), lambda i: (0, i, 0))
    return pl.pallas_call(
        body, name=name, grid=(r // tr,),
        in_specs=[pl.BlockSpec((tr, cdim), lambda i: (i, 0)), s3, s3, s3],
        out_specs=[s3, s3, s3],
        out_shape=[jax.ShapeDtypeStruct(w.shape, F32)] * 3,
        compiler_params=pltpu.CompilerParams(dimension_semantics=("parallel",), vmem_limit_bytes=VMEM_LIMIT_BYTES),
    )(g, w, m, v)


def _wada_update(sct, gm, w, m, v):
    _, r, cdim = w.shape
    tr = 256
    kp = sct.shape[1]

    def body(s_ref, g_ref, w_ref, m_ref, v_ref, gw_ref, d_ref, nm_ref, nv_ref):
        g = _dot(s_ref[...], g_ref[...])
        gw_ref[0] = g
        delta, m2, v2 = _adamw_math(w_ref[0], g, m_ref[0], v_ref[0])
        d_ref[0] = delta
        nm_ref[0] = m2
        nv_ref[0] = v2

    s3 = pl.BlockSpec((1, tr, cdim), lambda i: (0, i, 0))
    return pl.pallas_call(
        body, name="wada_update", grid=(r // tr,),
        in_specs=[pl.BlockSpec((tr, kp), lambda i: (i, 0)), _whole(gm.shape), s3, s3, s3],
        out_specs=[s3, s3, s3, s3],
        out_shape=[jax.ShapeDtypeStruct(w.shape, F32)] * 4,
        compiler_params=pltpu.CompilerParams(dimension_semantics=("parallel",), vmem_limit_bytes=VMEM_LIMIT_BYTES),
    )(sct, gm, w, m, v)


def _small_update(g1, g2, g2s, gwm, gwp, gbz, gsg, gps, params):
    names = ["b_ada", "pre_mix_g", "post_mix_g", "sgu_norm_g", "w_spatial", "b_spatial", "w_pool", "pool_scale",
             "pre_ffn_g", "post_ffn_g", "conv_w", "conv_b"]
    d = g1.shape[2]
    flat_in = [g1, g2, g2s, gwm, gwp, gbz, gsg, gps]
    n_g = len(flat_in)
    for nm in names:
        flat_in += list(params[nm])

    def body(*refs):
        g1_ref, g2_ref, g2s_ref, gwm_ref, gwp_ref, gbz_ref, gsg_ref, gps_ref = refs[:n_g]
        wmv = refs[n_g:n_g + 3 * len(names)]
        loss_ref = refs[n_g + 3 * len(names)]
        outs = refs[n_g + 3 * len(names) + 1:]

        def dsum(ref, idx):
            acc = ref[(0,) + idx].astype(F32)
            for dev in range(1, N_DEV):
                acc = acc + ref[(dev,) + idx].astype(F32)
            return acc

        def apply(pi, g, widx, oidx):
            w_ref, m_ref, v_ref = wmv[3 * pi:3 * pi + 3]
            g_ref, d_ref, nm_ref, nv_ref = outs[4 * pi:4 * pi + 4]
            delta, m2, v2 = _adamw_math(w_ref[widx], g, m_ref[widx], v_ref[widx])
            g_ref[oidx] = g
            d_ref[oidx] = delta
            nm_ref[oidx] = m2
            nv_ref[oidx] = v2

        def row1(base, r):
            return (slice(base + r, base + r + 1), slice(None))

        tot = dsum(g1_ref, row1(0, F_LOSS))
        loss_ref[...] = jnp.zeros(loss_ref.shape, F32) + jnp.sum(tot) * (0.5 / d)
        mod_rows = [row1(16, M_DSHM), row1(16, M_DSCM), row1(8, B_DGM), row1(8, B_DSHF), row1(8, B_DSCF),
                    row1(0, F_DGF)]
        for j, rr in enumerate(mod_rows):
            cs = (slice(None), slice(j * d, (j + 1) * d))
            apply(0, dsum(g1_ref, rr), cs, cs)
        full = (slice(None), slice(None))
        apply(1, dsum(g1_ref, row1(16, M_DPREMIX)), full, full)
        apply(2, dsum(g1_ref, row1(8, B_DPOSTMIX)), full, full)
        apply(3, dsum(gsg_ref, (slice(0, N_HEADS), slice(None))), (0,), (0,))
        pos_i = lax.broadcasted_iota(jnp.int32, (HEAD, HEAD), 0)
        pos_j = lax.broadcasted_iota(jnp.int32, (HEAD, HEAD), 1)
        causal = (pos_j // CHUNK) <= (pos_i // CHUNK)
        for h in range(N_HEADS):
            blk = (slice(h * HEAD, (h + 1) * HEAD), slice(None))
            apply(4, jnp.where(causal, dsum(gwm_ref, blk), 0.0), (0, h), (0, h))
            apply(5, dsum(gbz_ref, (slice(h, h + 1), slice(None))), (0, slice(h, h + 1)), (0, slice(h, h + 1)))
            apply(6, dsum(gwp_ref, blk), (0, h), (0, h))
        apply(7, dsum(gps_ref, (slice(0, 1), slice(None))), full, full)
        apply(8, dsum(g1_ref, row1(8, B_DPREFFN)), full, full)
        apply(9, dsum(g1_ref, row1(0, F_DPOSTFFN)), full, full)
        apply(10, dsum(g2s_ref, (slice(C_DCW, C_DCW + 3), slice(None))), (0,), (0,))
        apply(11, dsum(g2_ref, (slice(C_DCB, C_DCB + 1), slice(None))), full, full)

    out_shape = [jax.ShapeDtypeStruct((8, HEAD), F32)]
    for nm in names:
        out_shape += [jax.ShapeDtypeStruct(params[nm][0].shape, F32)] * 4
    res = pl.pallas_call(
        body, name="small_update", out_shape=out_shape,
        compiler_params=pltpu.CompilerParams(vmem_limit_bytes=VMEM_LIMIT_BYTES),
    )(*flat_in)
    out = {nm: tuple(res[1 + 4 * i:5 + 4 * i]) for i, nm in enumerate(names)}
    return res[0], out


def kernel(x, c, w_ada, b_ada, pre_mix_g, post_mix_g, w_in, sgu_norm_g, w_spatial, b_spatial, w_pool, pool_scale, w_out, pre_ffn_g, post_ffn_g, w_up, conv_w, conv_b, w_down, loss_target, m_w_ada, m_b_ada, m_pre_mix_g, m_post_mix_g, m_w_in, m_sgu_norm_g, m_w_spatial, m_b_spatial, m_w_pool, m_pool_scale, m_w_out, m_pre_ffn_g, m_post_ffn_g, m_w_up, m_conv_w, m_conv_b, m_w_down, v_w_ada, v_b_ada, v_pre_mix_g, v_post_mix_g, v_w_in, v_sgu_norm_g, v_w_spatial, v_b_spatial, v_w_pool, v_pool_scale, v_w_out, v_pre_ffn_g, v_post_ffn_g, v_w_up, v_conv_w, v_conv_b, v_w_down):
    xi, yi, ci = _mesh_pos()
    k_me = 2 * xi + yi
    dev = 2 * k_me + ci
    s_len, d = x.shape[1], x.shape[2]
    x2d = x[0]
    tgt = loss_target[0]
    ff2 = conv_b.shape[1]
    n_ada = w_ada.shape[2]
    n_cw = conv_w.shape[2]

    k_idx = k_me.reshape(1).astype(jnp.int32)
    flags4 = (True, False, True, False)
    (w_in_s, w_out_s, w_up_s, w_down_s), (w_in_f, w_out_f, w_up_f, w_down_f) = _cast_bf16(
        [w_in[0], w_out[0], w_up[0], w_down[0]], flags4, k_idx)
    mix_flags = (True, False)
    fly_mix = _gather_begin([w_in_s, w_out_s], [w_in_f, w_out_f], mix_flags, k_idx, "gather_begin_mix")

    cw_blk = jnp.concatenate([conv_w[0], jnp.zeros((5, n_cw), F32)], axis=0)
    c_blk = c.reshape(8, d // 8) + fly_mix[3][0:1, 0:1]
    c_all, cw_all = _run_rider(_allgather_rider([c_blk, cw_blk]), "gather_c_convw")
    c_all = c_all.reshape(N_DEV, 8, d // 8).reshape(N_DEV, d)
    cw_full = jnp.concatenate([cw_all[16 * k:16 * k + 8] for k in range(N_CHIP)], axis=1)
    cvec = jnp.concatenate([cw_full[0:3], conv_b, jnp.zeros((4, ff2), F32)], axis=0)
    b_shard = lax.dynamic_slice_in_dim(b_ada, k_me * n_ada, n_ada, axis=1)
    mod_k, sc_all = _mod_shard(c_all, w_ada[0], b_shard)
    (mod_g,) = _run_rider(_allgather_rider([mod_k]), "gather_mod")
    mod_all = jnp.concatenate([mod_g[16 * k:16 * k + 8] for k in range(N_CHIP)], axis=1)
    mod_me = lax.dynamic_slice_in_dim(mod_all, dev, 1, axis=0).reshape(6, d)
    vecs = jnp.concatenate([mod_me, pre_mix_g, post_mix_g, pre_ffn_g, post_ffn_g, jnp.zeros((6, d), F32)], axis=0)

    fly_ffn = _gather_begin([w_up_s, w_down_s], [w_up_f, w_down_f], mix_flags, mod_g, "gather_begin_ffn")
    w_in_b, w_out_b = _gather_finish(_gather_end(fly_mix, mix_flags, fly_ffn[3], "gather_end_mix"), mix_flags,
                                     "gather_finish_mix")

    pos = jnp.arange(HEAD)
    causal = (pos[None, :] // CHUNK) <= (pos[:, None] // CHUNK)
    wm = jnp.where(causal[None], w_spatial[0], 0.0)
    wm_b = wm.astype(BF16)
    wmt_b = jnp.swapaxes(wm, 1, 2).astype(BF16)
    bsb = jnp.broadcast_to(b_spatial[0][:, :, None], (N_HEADS, HEAD, HEAD))
    wp_b = w_pool[0].astype(BF16)
    sgu_g = jnp.concatenate([sgu_norm_g[0], jnp.zeros((4, HEAD), F32)], axis=0)
    ps = jnp.concatenate([pool_scale, jnp.zeros((7, A_WIDTH), F32)], axis=0)

    pmats = _perm_mats(FFN_TS)
    h1, proj, cat, mixed, x2, h2p = _mix_fwd(x2d, vecs, w_in_b, w_out_b, sgu_g, wm_b, bsb, wp_b, ps, pmats, ts=512)[0]
    w_up_b, w_down_b = _gather_finish(_gather_end(fly_ffn, mix_flags, h2p, "gather_end_ffn"), mix_flags,
                                      "gather_finish_ffn")
    up, yv, act, dy, dfp, acc_f = _ffn_fwd(h2p, x2, tgt, w_up_b, w_down_b, cvec, vecs, pmats, ts=FFN_TS)

    c_idx = ci.reshape(1).astype(jnp.int32)
    g_w_down, _ = _wgrad(act, dfp, d, 1024, "wgrad_down")
    (dup, dx2, dmixed, acc_c, acc_b), (land_down,) = _ffn_bwd(
        dfp, up, yv, x2, dy, mixed, w_up_b, w_down_b, cvec, vecs, pmats, ts=FFN_TS,
        rider=_sibling_rider([g_w_down], (False,)))
    (part_down,) = _sum_with_sibling([g_w_down], [land_down], (False,), c_idx, "pair_sum_down")
    g_w_up, (chips_down,) = _wgrad(h2p, dup, ff2 // 2, 2048, "wgrad_up", rider=_chips_rider([part_down]))
    fly_up = _sibling_begin(g_w_up, True, chips_down, "sibling_begin_up")
    gx, dproj, acc_m, dwm, dwp, dbz, dsg, dps = _mix_bwd(
        dmixed, dx2, x2d, proj, vecs + fly_up[3][0:1, 0:1], w_in_b, w_out_b, sgu_g, wm_b, wmt_b, bsb, wp_b, ps,
        ts=512)[0]
    g_w_up, land_up = _sibling_end(fly_up, True, dproj, "sibling_end_up")
    (part_up,) = _sum_with_sibling([g_w_up], [land_up], (True,), c_idx, "pair_sum_up")
    fly_chips_up = _chips_begin([part_up], c_idx, "chips_begin_up")
    g1 = jnp.concatenate([acc_f, acc_b, acc_m], axis=0)
    hflat = (N_HEADS * HEAD, HEAD)
    small_bufs = [lax.dynamic_update_slice(jnp.zeros((N_DEV * t.shape[0], t.shape[1]), t.dtype), t,
                                           (dev * t.shape[0], 0))
                  for t in (g1, acc_c, dwm.reshape(hflat), dwp.reshape(hflat), dbz, dsg, dps)]
    fly_small = _split_begin(small_bufs, 3 * len(small_bufs), _small_ici_copies, fly_chips_up[3],
                             "small_gather_begin")
    g_w_out, g_w_in = _wgrad_pair(cat, dmixed, h1, dproj, fly_small[3], 1024, "wgrad_mix")
    land_mix = _run_rider(_sibling_rider([g_w_in, g_w_out], (True, False)), "reduce_to_sibling")
    parts_mix = _sum_with_sibling([g_w_in, g_w_out], land_mix, (True, False), c_idx, "pair_sum_mix")
    (part_up,), (chips_up,) = _chips_end(fly_chips_up, parts_mix[0], "chips_end_up")
    fly_chips_mix = _chips_begin(parts_mix, chips_up, "chips_begin_mix")

    def adamw_of(names, reduced):
        res = {}
        for nm, red in zip(names, reduced):
            w, m, v = big_wmv[nm]
            g = red.reshape(w.shape[1], w.shape[2])
            res[nm] = (g.reshape(w.shape),) + tuple(_adamw_big(g, w, m, v, "adamw_" + nm))
        return res

    big_wmv = {"w_in": (w_in, m_w_in, v_w_in), "w_out": (w_out, m_w_out, v_w_out),
               "w_up": (w_up, m_w_up, v_w_up), "w_down": (w_down, m_w_down, v_w_down)}
    big = adamw_of(("w_up", "w_down"), _sum_chips_and_share([chips_up, chips_down], [part_up, part_down],
                                                           fly_chips_mix[3], "sum_share_ffn"))

    gathered = _small_finish(_split_end(fly_small, _small_ici_copies, big["w_down"][1], "small_gather_end"),
                             "small_gather_finish")
    g1a, g2a, gwm, gwp, gbz, gsg, gps = [t.reshape((N_DEV, t.shape[0] // N_DEV, t.shape[1])) for t in gathered]
    g2s = lax.dynamic_slice_in_dim(g2a, k_me * n_cw, n_cw, axis=2)
    params = {
        "b_ada": (b_ada, m_b_ada, v_b_ada), "pre_mix_g": (pre_mix_g, m_pre_mix_g, v_pre_mix_g),
        "post_mix_g": (post_mix_g, m_post_mix_g, v_post_mix_g),
        "sgu_norm_g": (sgu_norm_g, m_sgu_norm_g, v_sgu_norm_g), "w_spatial": (w_spatial, m_w_spatial, v_w_spatial),
        "b_spatial": (b_spatial, m_b_spatial, v_b_spatial), "w_pool": (w_pool, m_w_pool, v_w_pool),
        "pool_scale": (pool_scale, m_pool_scale, v_pool_scale), "pre_ffn_g": (pre_ffn_g, m_pre_ffn_g, v_pre_ffn_g),
        "post_ffn_g": (post_ffn_g, m_post_ffn_g, v_post_ffn_g), "conv_w": (conv_w, m_conv_w, v_conv_w),
        "conv_b": (conv_b, m_conv_b, v_conv_b),
    }
    loss_slab, small = _small_update(g1a, g2a, g2s, gwm, gwp, gbz, gsg, gps, params)

    gmod_all = jnp.concatenate(
        [g1a[:, 16 + M_DSHM], g1a[:, 16 + M_DSCM], g1a[:, 8 + B_DGM], g1a[:, 8 + B_DSHF], g1a[:, 8 + B_DSCF],
         g1a[:, F_DGF]], axis=1)
    gm = lax.dynamic_slice_in_dim(gmod_all, k_me * n_ada, n_ada, axis=1)
    gm = jnp.concatenate([gm, jnp.zeros((HEAD - N_DEV, n_ada), F32)], axis=0)
    sct = jnp.concatenate([sc_all.T, jnp.zeros((d, HEAD - N_DEV), F32)], axis=1)
    ada = tuple(_wada_update(sct, gm, w_ada, m_w_ada, v_w_ada))

    parts_mix, chips_mix = _chips_end(fly_chips_mix, ada[1], "chips_end_mix")
    big.update(adamw_of(("w_in", "w_out"), _sum_chips_and_share(chips_mix, parts_mix, loss_slab, "sum_share_mix")))

    everything = dict(small)
    everything.update(big)
    everything["w_ada"] = ada
    order = ["w_ada", "b_ada", "pre_mix_g", "post_mix_g", "w_in", "sgu_norm_g", "w_spatial", "b_spatial", "w_pool",
             "pool_scale", "w_out", "pre_ffn_g", "post_ffn_g", "w_up", "conv_w", "conv_b", "w_down"]
    outs = [loss_slab[0, 0], gx.reshape(x.shape)]
    for j in range(4):
        outs += [everything[nm][j] for nm in order]
    return tuple(outs)
```

```python
import functools

import jax
import jax.numpy as jnp
from jax import lax
from jax.experimental import pallas as pl
from jax.experimental.pallas import tpu as pltpu

F32 = jnp.float32
BF16 = jnp.bfloat16
MESH = pl.DeviceIdType.MESH

EPS = 1e-6
HEAD = 128
N_HEADS = 4
A_WIDTH = N_HEADS * HEAD
CHUNK = 64
POOL_WINDOWS = (2, 4, 8, 16)
POOL_HALO = 16
FFN_TS = 256

ADAM_LR = 0.001
ADAM_B1 = 0.9
ADAM_B2 = 0.999
ADAM_EPS = 1e-08
ADAM_WD = 0.01
ADAM_STEP = 10

VMEM_LIMIT_BYTES = 58 * 1024 * 1024
N_DEV = 8
N_CHIP = 4


def _dot(a, b):
    return jnp.dot(a, b, preferred_element_type=F32)


def _dot_nt(a, b):
    return lax.dot_general(a, b, (((1,), (1,)), ((), ())), preferred_element_type=F32)


def _dot_tn(a, b):
    return lax.dot_general(a, b, (((0,), (0,)), ((), ())), preferred_element_type=F32)


def _gelu(x):
    return x * (0.5 * (1.0 + jnp.tanh(0.7978845608028654 * (x + 0.044715 * (x * x * x)))))


def _gelu_grad(x):
    t = jnp.tanh(0.7978845608028654 * (x + 0.044715 * (x * x * x)))
    return 0.5 * (1.0 + t) + (0.5 * x) * (1.0 - t * t) * (0.7978845608028654 * (1.0 + 0.134145 * (x * x)))


def _sigmoid(x):
    return 1.0 / (1.0 + jnp.exp(-x))


def _rms(x):
    return lax.rsqrt(jnp.mean(x * x, axis=-1, keepdims=True) + EPS)


def _colsum(x):
    return jnp.sum(x, axis=0, keepdims=True)


def _rowmean(x):
    return jnp.mean(x, axis=-1, keepdims=True)


def _tiled(shape, index_map):
    return pl.BlockSpec(shape, index_map)


def _resident(shape):
    nd = len(shape)
    return pl.BlockSpec(shape, lambda *_: (0,) * nd, pipeline_mode=pl.Buffered(1))


def _whole(shape):
    nd = len(shape)
    return pl.BlockSpec(shape, lambda *_: (0,) * nd)


def _seq_params():
    return pltpu.CompilerParams(dimension_semantics=("arbitrary",), vmem_limit_bytes=VMEM_LIMIT_BYTES)


def _ff_chunks(f, width=768):
    out, o = [], 0
    while o < f:
        w = min(width, f - o)
        out.append((o, w))
        o += w
    return out


def _pool_fwd(p, halo, g, t_glob):
    ext = jnp.concatenate([halo, p], axis=0)
    s = ext
    for step in range(g + 1):
        s = s + pltpu.roll(s, 1 << step, 0)
    cnt = jnp.minimum(t_glob + 1, POOL_WINDOWS[g]).astype(F32)
    return s[POOL_HALO:] / cnt - p


def _adamw_math(w, g, m, v):
    m = ADAM_B1 * m + (1.0 - ADAM_B1) * g
    v = ADAM_B2 * v + (1.0 - ADAM_B2) * (g * g)
    m_hat = m / (1.0 - ADAM_B1 ** ADAM_STEP)
    v_hat = v / (1.0 - ADAM_B2 ** ADAM_STEP)
    delta = -ADAM_LR * (m_hat / (jnp.sqrt(v_hat) + ADAM_EPS) + ADAM_WD * w)
    return delta, m, v


def _mesh_pos():
    return lax.axis_index("x"), lax.axis_index("y"), lax.axis_index("c")


class _Rider:
    def __init__(self, inputs, out_shape, sems, start, finish):
        self.inputs, self.out_shape, self.sems = list(inputs), list(out_shape), list(sems)
        self.start, self.finish = start, finish


def _call(body, *, name, grid, in_specs, out_specs, out_shape, scratch_shapes, args, rider=None):
    params = pltpu.CompilerParams(dimension_semantics=("arbitrary",) * len(grid), vmem_limit_bytes=VMEM_LIMIT_BYTES)
    if rider is None:
        res = pl.pallas_call(body, name=name, grid=grid, in_specs=in_specs, out_specs=out_specs, out_shape=out_shape,
                             scratch_shapes=scratch_shapes, compiler_params=params)(*args)
        return tuple(res), ()
    cuts = [len(in_specs), len(rider.inputs), len(out_specs), len(rider.out_shape), len(scratch_shapes),
            len(rider.sems)]

    def hosted(*refs):
        groups, a = [], 0
        for cnt in cuts:
            groups.append(refs[a:a + cnt])
            a += cnt
        ins, r_in, outs, r_out, scr, r_sem = groups
        first = functools.reduce(jnp.logical_and, [pl.program_id(k) == 0 for k in range(len(grid))])
        last = functools.reduce(jnp.logical_and, [pl.program_id(k) == grid[k] - 1 for k in range(len(grid))])

        @pl.when(first)
        def _():
            rider.start(r_in, r_out, r_sem)

        body(*ins, *outs, *scr)

        @pl.when(last)
        def _():
            rider.finish(r_in, r_out, r_sem)

    anyspec = pl.BlockSpec(memory_space=pl.ANY)
    res = pl.pallas_call(
        hosted, name=name, grid=grid,
        in_specs=list(in_specs) + [anyspec] * cuts[1], out_specs=list(out_specs) + [anyspec] * cuts[3],
        out_shape=list(out_shape) + rider.out_shape, scratch_shapes=list(scratch_shapes) + rider.sems,
        compiler_params=params)(*args, *rider.inputs)
    return tuple(res[:cuts[2]]), tuple(res[cuts[2]:])


def _run_rider(rider, name):
    n_in, n_out = len(rider.inputs), len(rider.out_shape)

    def body(*refs):
        r_in, r_out, r_sem = refs[:n_in], refs[n_in:n_in + n_out], refs[n_in + n_out:]
        rider.start(r_in, r_out, r_sem)
        rider.finish(r_in, r_out, r_sem)

    anyspec = pl.BlockSpec(memory_space=pl.ANY)
    return pl.pallas_call(body, name=name, out_shape=rider.out_shape, in_specs=[anyspec] * n_in,
                          out_specs=[anyspec] * n_out, scratch_shapes=rider.sems)(*rider.inputs)


def _allgather_rider(arrs):
    n = len(arrs)

    def plan(ins, outs, sems):
        send_sems, recv_sems, local_sems = sems
        x, y, c = _mesh_pos()
        me, sibling = (x, y, c), (x, y, 1 - c)
        chips = [(1 - x, y), (x, 1 - y), (1 - x, 1 - y)]

        def rows(a, px, py, pc):
            r = ins[a].shape[0]
            return outs[a].at[pl.ds(pl.multiple_of((4 * px + 2 * py + pc) * r, 8), r), :]

        def copy(a, k, block, to, src=None):
            return pltpu.make_async_remote_copy(
                src_ref=rows(a, *block) if src is None else src, dst_ref=rows(a, *block),
                send_sem=send_sems.at[a * 7 + k], recv_sem=recv_sems.at[a * 7 + k],
                device_id=to, device_id_type=MESH)

        local = [pltpu.make_async_copy(ins[a], rows(a, *me), local_sems.at[a]) for a in range(n)]
        first = []
        for a in range(n):
            first.append(copy(a, 0, me, sibling, src=ins[a]))
            first += [copy(a, 1 + j, me, (*chip, c), src=ins[a]) for j, chip in enumerate(chips)]
        return c, me, sibling, chips, copy, local, first

    def start(ins, outs, sems):
        *_, local, first = plan(ins, outs, sems)
        for cp in local + first:
            cp.start()

    def finish(ins, outs, sems):
        c, me, sibling, chips, copy, local, first = plan(ins, outs, sems)
        passed = []
        for a in range(n):
            for j, chip in enumerate(chips):
                copy(a, 1 + j, (*chip, c), me).wait_recv()
                fwd = copy(a, 4 + j, (*chip, c), sibling)
                fwd.start()
                passed.append(fwd)
        for a in range(n):
            copy(a, 0, sibling, me).wait_recv()
            for j, chip in enumerate(chips):
                copy(a, 4 + j, (*chip, 1 - c), me).wait_recv()
        for cp in first + passed:
            cp.wait_send()
        for mine in local:
            mine.wait()

    return _Rider(arrs, [jax.ShapeDtypeStruct((N_DEV * a.shape[0], a.shape[1]), a.dtype) for a in arrs],
                  [pltpu.SemaphoreType.DMA((7 * n,)), pltpu.SemaphoreType.DMA((7 * n,)),
                   pltpu.SemaphoreType.DMA((n,))], start, finish)


def _piece(ref, col_sharded, k, h):
    m, n = ref.shape
    if col_sharded:
        mh, nc = m // 2, n // N_CHIP
        return ref.at[pl.ds(pl.multiple_of(h * mh, 16), mh), pl.ds(pl.multiple_of(k * nc, 128), nc)]
    rp = m // (2 * N_CHIP)
    return ref.at[pl.ds(pl.multiple_of((2 * k + h) * rp, 16), rp), :]


def _piece_shape(shape, col_sharded):
    m, n = shape
    return (m // 2, n // N_CHIP) if col_sharded else (m // (2 * N_CHIP), n)


def _cast_bf16(arrs, col_flags, k_idx):
    n = len(arrs)

    def body(k_ref, *refs):
        for a in range(n):
            val = refs[a][...].astype(BF16)
            refs[n + a][...] = val
            refs[2 * n + a][...] = val

    whole = [pl.BlockSpec(a.shape, lambda i, k_ref: (0, 0)) for a in arrs]
    window = [pl.BlockSpec(a.shape, (lambda i, k_ref: (0, k_ref[0])) if col else (lambda i, k_ref: (k_ref[0], 0)))
              for a, col in zip(arrs, col_flags)]
    res = pl.pallas_call(
        body, name="cast_weights",
        grid_spec=pltpu.PrefetchScalarGridSpec(num_scalar_prefetch=1, grid=(1,), in_specs=whole,
                                               out_specs=whole + window),
        out_shape=[jax.ShapeDtypeStruct(a.shape, BF16) for a in arrs]
        + [jax.ShapeDtypeStruct(fs, BF16) for fs in _full_shapes(arrs, col_flags)],
        compiler_params=pltpu.CompilerParams(vmem_limit_bytes=VMEM_LIMIT_BYTES))(k_idx, *arrs)
    return list(res[:n]), list(res[n:])


def _full_shapes(shards, col_flags):
    return [(s.shape[0], s.shape[1] * N_CHIP) if col else (s.shape[0] * N_CHIP, s.shape[1])
            for s, col in zip(shards, col_flags)]


def _ici_copies(shard_refs, full_refs, send_sems, recv_sems, col_flags):
    x, y, c = _mesh_pos()
    k_me = 2 * x + y
    copies = []
    for a, (s_ref, f_ref) in enumerate(zip(shard_refs, full_refs)):
        rows = s_ref.shape[0] // 2
        src = s_ref.at[pl.ds(pl.multiple_of(c * rows, 16), rows), :]
        for j, chip in enumerate([(1 - x, y), (x, 1 - y), (1 - x, 1 - y)]):
            copies.append(pltpu.make_async_remote_copy(
                src_ref=src, dst_ref=_piece(f_ref, col_flags[a], k_me, c),
                send_sem=send_sems.at[a * 3 + j], recv_sem=recv_sems.at[a * 3 + j],
                device_id=(*chip, c), device_id_type=MESH))
    return copies


def _split_begin(bufs, n_sems, make_copies, after, name):
    n = len(bufs)
    hbm = pl.BlockSpec(memory_space=pltpu.HBM)
    sem = pl.BlockSpec(memory_space=pltpu.SEMAPHORE)

    def body(*refs):
        for cp in make_copies(refs[:n], refs[n + 1], refs[n + 2]):
            cp.start()
        refs[-1][...] = jnp.zeros_like(refs[-1])

    args = [pltpu.with_memory_space_constraint(t, pltpu.HBM) for t in bufs]
    res = pl.pallas_call(
        body, name=name,
        out_shape=[pltpu.SemaphoreType.DMA((n_sems,)), pltpu.SemaphoreType.DMA((n_sems,))]
        + [pltpu.HBM(t.shape, t.dtype) for t in args] + [jax.ShapeDtypeStruct((8, HEAD), F32)],
        in_specs=[hbm] * n + [pl.BlockSpec(memory_space=pl.ANY)],
        out_specs=[sem, sem] + [hbm] * n + [pl.BlockSpec(memory_space=pltpu.VMEM)],
        input_output_aliases={i: 2 + i for i in range(n)},
        compiler_params=pltpu.CompilerParams(has_side_effects=pltpu.SideEffectType.DATAFLOW_SIDE_EFFECTING),
    )(*args, after)
    return res[0], res[1], list(res[2:2 + n]), res[-1]


def _split_end(handle, make_copies, after, name):
    send_sems, recv_sems, bufs, _ = handle
    n = len(bufs)
    hbm = pl.BlockSpec(memory_space=pltpu.HBM)
    sem = pl.BlockSpec(memory_space=pltpu.SEMAPHORE)

    def body(*refs):
        for cp in make_copies(refs[:n], refs[n], refs[n + 1]):
            cp.wait_send()
            cp.wait_recv()

    res = pl.pallas_call(
        body, name=name,
        out_shape=[pltpu.HBM(t.shape, t.dtype) for t in bufs],
        in_specs=[hbm] * n + [sem, sem, pl.BlockSpec(memory_space=pl.ANY)],
        out_specs=[hbm] * n,
        input_output_aliases={i: i for i in range(n)},
        compiler_params=pltpu.CompilerParams(has_side_effects=pltpu.SideEffectType.DATAFLOW_SIDE_EFFECTING),
    )(*bufs, send_sems, recv_sems, after)
    return list(res)


def _gather_copies(n, col_flags):
    return lambda refs, send_sems, recv_sems: _ici_copies(refs[:n], refs[n:], send_sems, recv_sems, col_flags)


def _gather_begin(shards, fulls, col_flags, after, name):
    n = len(shards)
    return _split_begin(list(shards) + list(fulls), 3 * n, _gather_copies(n, col_flags), after, name)


def _gather_end(handle, col_flags, after, name):
    n = len(handle[2]) // 2
    return _split_end(handle, _gather_copies(n, col_flags), after, name)[n:]


def _sibling_copies(col_flag):
    def make(refs, send_sems, recv_sems):
        grad_ref, land_ref = refs
        x, y, c = _mesh_pos()
        return [pltpu.make_async_remote_copy(
            src_ref=_piece(grad_ref, col_flag, k, 1 - c), dst_ref=land_ref.at[k],
            send_sem=send_sems.at[k], recv_sem=recv_sems.at[k],
            device_id=(x, y, 1 - c), device_id_type=MESH) for k in range(N_CHIP)]
    return make


def _chips_copies(n):
    def make(refs, send_sems, recv_sems):
        parts, landed = refs[:n], refs[n:]
        x, y, c = _mesh_pos()
        k_me = 2 * x + y
        copies = []
        for a in range(n):
            for j, chip in enumerate([(1 - x, y), (x, 1 - y), (1 - x, 1 - y)]):
                copies.append(pltpu.make_async_remote_copy(
                    src_ref=parts[a].at[2 * chip[0] + chip[1]], dst_ref=landed[a].at[k_me],
                    send_sem=send_sems.at[a * 3 + j], recv_sem=recv_sems.at[a * 3 + j],
                    device_id=(*chip, c), device_id_type=MESH))
        return copies
    return make


def _chips_begin(parts, after, name):
    landed = [lax.empty(p.shape, p.dtype) for p in parts]
    return _split_begin(list(parts) + landed, 3 * len(parts), _chips_copies(len(parts)), after, name)


def _chips_end(handle, after, name):
    n = len(handle[2]) // 2
    res = _split_end(handle, _chips_copies(n), after, name)
    return res[:n], res[n:]


def _small_rows(buf_ref, px, py, pc):
    r = buf_ref.shape[0] // N_DEV
    return buf_ref.at[pl.ds(pl.multiple_of((4 * px + 2 * py + pc) * r, 8), r), :]


def _small_ici_copies(refs, send_sems, recv_sems):
    x, y, c = _mesh_pos()
    copies = []
    for a, buf in enumerate(refs):
        mine = _small_rows(buf, x, y, c)
        for j, chip in enumerate([(1 - x, y), (x, 1 - y), (1 - x, 1 - y)]):
            copies.append(pltpu.make_async_remote_copy(
                src_ref=mine, dst_ref=mine, send_sem=send_sems.at[a * 3 + j], recv_sem=recv_sems.at[a * 3 + j],
                device_id=(*chip, c), device_id_type=MESH))
    return copies


def _small_finish(bufs, name):
    n = len(bufs)

    def body(*refs):
        buf_refs = refs[n:2 * n]
        send_sems, recv_sems = refs[2 * n:]
        x, y, c = _mesh_pos()
        owners = [(x, y), (1 - x, y), (x, 1 - y), (1 - x, 1 - y)]
        passed, arriving = [], []
        for a in range(n):
            for j, (px, py) in enumerate(owners):
                for pc, group in ((c, passed), (1 - c, arriving)):
                    rows = _small_rows(buf_refs[a], px, py, pc)
                    group.append(pltpu.make_async_remote_copy(
                        src_ref=rows, dst_ref=rows, send_sem=send_sems.at[a * 4 + j],
                        recv_sem=recv_sems.at[a * 4 + j], device_id=(x, y, 1 - c), device_id_type=MESH))
        for cp in passed:
            cp.start()
        for cp in arriving:
            cp.wait_recv()
        for cp in passed:
            cp.wait_send()

    anyspec = pl.BlockSpec(memory_space=pl.ANY)
    return pl.pallas_call(
        body, name=name, out_shape=[jax.ShapeDtypeStruct(b.shape, b.dtype) for b in bufs],
        in_specs=[anyspec] * n, out_specs=[anyspec] * n, input_output_aliases={a: a for a in range(n)},
        scratch_shapes=[pltpu.SemaphoreType.DMA((4 * n,)), pltpu.SemaphoreType.DMA((4 * n,))],
    )(*bufs)


def _sibling_begin(grad, col_flag, after, name):
    land = lax.empty((N_CHIP,) + _piece_shape(grad.shape, col_flag), grad.dtype)
    return _split_begin([grad, land], N_CHIP, _sibling_copies(col_flag), after, name)


def _sibling_end(handle, col_flag, after, name):
    return _split_end(handle, _sibling_copies(col_flag), after, name)


def _gather_finish(fulls, col_flags, name):
    n = len(fulls)

    def body(*refs):
        full_refs = refs[n:2 * n]
        send_sems, recv_sems = refs[2 * n:]
        x, y, c = _mesh_pos()
        passed, arriving = [], []
        for a in range(n):
            for j, chip in enumerate([(1 - x, y), (x, 1 - y), (1 - x, 1 - y)]):
                k_from = 2 * chip[0] + chip[1]
                for h, group in ((c, passed), (1 - c, arriving)):
                    win = _piece(full_refs[a], col_flags[a], k_from, h)
                    group.append(pltpu.make_async_remote_copy(
                        src_ref=win, dst_ref=win, send_sem=send_sems.at[a * 3 + j],
                        recv_sem=recv_sems.at[a * 3 + j], device_id=(x, y, 1 - c), device_id_type=MESH))
        for cp in passed:
            cp.start()
        for cp in arriving:
            cp.wait_recv()
        for cp in passed:
            cp.wait_send()

    anyspec = pl.BlockSpec(memory_space=pl.ANY)
    return pl.pallas_call(
        body, name=name,
        out_shape=[jax.ShapeDtypeStruct(f.shape, f.dtype) for f in fulls],
        in_specs=[anyspec] * n, out_specs=[anyspec] * n,
        input_output_aliases={a: a for a in range(n)},
        scratch_shapes=[pltpu.SemaphoreType.DMA((3 * n,)), pltpu.SemaphoreType.DMA((3 * n,))],
    )(*fulls)


def _sibling_rider(grads, col_flags):
    n = len(grads)
    pshapes = [_piece_shape(g.shape, col) for g, col in zip(grads, col_flags)]

    def copies(ins, outs, sems):
        send_sems, recv_sems = sems
        x, y, c = _mesh_pos()
        return [pltpu.make_async_remote_copy(
            src_ref=_piece(ins[a], col_flags[a], k, 1 - c), dst_ref=outs[a].at[k],
            send_sem=send_sems.at[a * N_CHIP + k], recv_sem=recv_sems.at[a * N_CHIP + k],
            device_id=(x, y, 1 - c), device_id_type=MESH) for a in range(n) for k in range(N_CHIP)]

    def start(ins, outs, sems):
        for cp in copies(ins, outs, sems):
            cp.start()

    def finish(ins, outs, sems):
        cps = copies(ins, outs, sems)
        for cp in cps:
            cp.wait_recv()
        for cp in cps:
            cp.wait_send()

    return _Rider(grads, [jax.ShapeDtypeStruct((N_CHIP,) + ps, g.dtype) for ps, g in zip(pshapes, grads)],
                  [pltpu.SemaphoreType.DMA((N_CHIP * n,)), pltpu.SemaphoreType.DMA((N_CHIP * n,))], start, finish)


def _sum_with_sibling(grads, landed, col_flags, c_idx, name):
    n = len(grads)
    pshapes = [_piece_shape(g.shape, col) for g, col in zip(grads, col_flags)]

    def body(c_ref, *refs):
        ins, lands, outs = refs[:n], refs[n:2 * n], refs[2 * n:]
        for a in range(n):
            outs[a][0] = (ins[a][...] + lands[a][0]).astype(BF16)

    in_specs = []
    for ps, col in zip(pshapes, col_flags):
        if col:
            in_specs.append(pl.BlockSpec(ps, lambda k, c_ref: (c_ref[0], k)))
        else:
            in_specs.append(pl.BlockSpec(ps, lambda k, c_ref: (2 * k + c_ref[0], 0)))
    land_specs = [pl.BlockSpec((1,) + ps, lambda k, c_ref: (k, 0, 0)) for ps in pshapes]
    return pl.pallas_call(
        body, name=name,
        grid_spec=pltpu.PrefetchScalarGridSpec(
            num_scalar_prefetch=1, grid=(N_CHIP,),
            in_specs=in_specs + land_specs, out_specs=land_specs),
        out_shape=[jax.ShapeDtypeStruct((N_CHIP,) + ps, BF16) for ps in pshapes],
        compiler_params=pltpu.CompilerParams(dimension_semantics=("arbitrary",), vmem_limit_bytes=VMEM_LIMIT_BYTES),
    )(c_idx, *grads, *landed)


def _chips_rider(parts):
    n = len(parts)

    def plan(ins, outs, sems, arriving):
        send_sems, recv_sems = sems
        x, y, c = _mesh_pos()
        k_me = 2 * x + y
        copies = []
        for a in range(n):
            for j, chip in enumerate([(1 - x, y), (x, 1 - y), (1 - x, 1 - y)]):
                k_peer = 2 * chip[0] + chip[1]
                copies.append(pltpu.make_async_remote_copy(
                    src_ref=ins[a].at[k_peer], dst_ref=outs[a].at[k_peer if arriving else k_me],
                    send_sem=send_sems.at[a * 3 + j], recv_sem=recv_sems.at[a * 3 + j],
                    device_id=(*chip, c), device_id_type=MESH))
        return copies

    def start(ins, outs, sems):
        for cp in plan(ins, outs, sems, False):
            cp.start()

    def finish(ins, outs, sems):
        arrivals = plan(ins, outs, sems, True)
        for cp in arrivals:
            cp.wait_recv()
        for cp in arrivals:
            cp.wait_send()

    return _Rider(parts, [jax.ShapeDtypeStruct(p.shape, p.dtype) for p in parts],
                  [pltpu.SemaphoreType.DMA((3 * n,)), pltpu.SemaphoreType.DMA((3 * n,))], start, finish)


def _sum_chips_and_share(landed, parts, after, name):
    n = len(landed)

    def body(*refs):
        ins, own, outs, red = refs[:n], refs[n:2 * n], refs[2 * n + 1:3 * n + 1], refs[3 * n + 1:4 * n + 1]
        send_sems, recv_sems, local_sems = refs[4 * n + 1:]
        x, y, c = _mesh_pos()
        sibling = (x, y, 1 - c)
        k_me = 2 * x + y
        copies, local = [], []
        for a in range(n):
            for k in range(N_CHIP):
                @pl.when(k_me == k)
                def _():
                    term = own[a][k].astype(F32)
                    red[a][...] = term if k == 0 else red[a][...] + term

                @pl.when(k_me != k)
                def _():
                    term = ins[a][k].astype(F32)
                    red[a][...] = term if k == 0 else red[a][...] + term

            mine = pltpu.make_async_copy(red[a], outs[a].at[c], local_sems.at[a])
            mine.start()
            local.append(mine)
            cp = pltpu.make_async_remote_copy(
                src_ref=red[a], dst_ref=outs[a].at[c],
                send_sem=send_sems.at[a], recv_sem=recv_sems.at[a],
                device_id=sibling, device_id_type=MESH)
            cp.start()
            copies.append(cp)
        for a in range(n):
            pltpu.make_async_remote_copy(
                src_ref=red[a], dst_ref=outs[a].at[1 - c],
                send_sem=send_sems.at[a], recv_sem=recv_sems.at[a],
                device_id=sibling, device_id_type=MESH).wait_recv()
        for cp in copies:
            cp.wait_send()
        for mine in local:
            mine.wait()

    return pl.pallas_call(
        body, name=name,
        out_shape=[jax.ShapeDtypeStruct((2,) + l.shape[1:], F32) for l in landed],
        in_specs=[pl.BlockSpec(memory_space=pltpu.VMEM)] * (2 * n + 1),
        out_specs=[pl.BlockSpec(memory_space=pl.ANY)] * n,
        scratch_shapes=[pltpu.VMEM(l.shape[1:], F32) for l in landed]
        + [pltpu.SemaphoreType.DMA((n,)), pltpu.SemaphoreType.DMA((n,)), pltpu.SemaphoreType.DMA((n,))],
        compiler_params=pltpu.CompilerParams(vmem_limit_bytes=VMEM_LIMIT_BYTES),
    )(*landed, *parts, after)


def _mod_shard(c_all, w_ada, b_shard):
    def body(c_ref, w_ref, b_ref, o_ref, sc_ref):
        cc = c_ref[...]
        sc = cc * _sigmoid(cc)
        sc_ref[...] = sc
        o_ref[...] = _dot(sc, w_ref[...]) + b_ref[...]

    nb, d = c_all.shape
    nn = w_ada.shape[1]
    return pl.pallas_call(
        body, name="mod_shard",
        out_shape=[jax.ShapeDtypeStruct((nb, nn), F32), jax.ShapeDtypeStruct((nb, d), F32)],
        compiler_params=pltpu.CompilerParams(vmem_limit_bytes=VMEM_LIMIT_BYTES),
    )(c_all, w_ada, b_shard)


V_SH_M, V_SC_M, V_G_M, V_SH_F, V_SC_F, V_G_F, V_PRE_MIX, V_POST_MIX, V_PRE_FFN, V_POST_FFN = range(10)


def _vrow(vec_ref, r):
    return vec_ref[r:r + 1, :]


def _mix_fwd(x, vecs, w_in_b, w_out_b, sgu_g, wm_b, bsb, wp_b, ps, pmats, ts):
    s_len, d = x.shape
    nt = s_len // ts
    nblk = ts // HEAD
    n_proj = w_in_b.shape[1]

    def body(x_ref, vec_ref, win_ref, wout_ref, sg_ref, wm_ref, bs_ref, wp_ref, ps_ref, pm_ref,
             h1_ref, proj_ref, cat_ref, mixed_ref, x2_ref, h2_ref, carry_ref):
        i = pl.program_id(0)

        @pl.when(i == 0)
        def _():
            carry_ref[...] = jnp.zeros_like(carry_ref)

        sub = FFN_TS
        nsub = ts // sub

        def project(s):
            rs = slice(s * sub, (s + 1) * sub)
            x = x_ref[rs, :]
            h1 = (((x * _rms(x)) * _vrow(vec_ref, V_PRE_MIX)) * (1.0 + _vrow(vec_ref, V_SC_M))
                  + _vrow(vec_ref, V_SH_M)).astype(BF16)
            h1_ref[rs, :] = h1
            proj = _dot(h1, win_ref[...])
            proj_ref[rs, :] = proj.astype(BF16)
            return proj

        def mix(s, proj, halo):
            r0 = s * sub
            t_glob = lax.broadcasted_iota(jnp.int32, (sub, HEAD), 0) + (i * ts + r0)
            for h in range(N_HEADS):
                u = _gelu(proj[:, h * HEAD:(h + 1) * HEAD])
                v = _gelu(proj[:, A_WIDTH + h * HEAD:A_WIDTH + (h + 1) * HEAD])
                vn = ((v * _rms(v)) * sg_ref[h:h + 1, :]).astype(BF16)
                for b in range(sub // HEAD):
                    rs = slice(b * HEAD, (b + 1) * HEAD)
                    z = _dot(wm_ref[h], vn[rs]) + bs_ref[h]
                    cat_ref[r0 + b * HEAD:r0 + (b + 1) * HEAD, h * HEAD:(h + 1) * HEAD] = (u[rs] * z).astype(BF16)
            for g in range(len(POOL_WINDOWS)):
                gs = slice(g * HEAD, (g + 1) * HEAD)
                p = proj[:, 2 * A_WIDTH + g * HEAD:2 * A_WIDTH + (g + 1) * HEAD]
                pooled = _pool_fwd(p, halo[:, gs], g, t_glob)
                yb = _dot(pooled.astype(BF16), wp_ref[g]) * ps_ref[0:1, gs]
                cat_ref[r0:r0 + sub, A_WIDTH + g * HEAD:A_WIDTH + (g + 1) * HEAD] = yb.astype(BF16)

        def finish(s, mixed):
            rs = slice(s * sub, (s + 1) * sub)
            mixed_ref[rs, :] = mixed
            x2 = x_ref[rs, :] + _vrow(vec_ref, V_G_M) * ((mixed * _rms(mixed)) * _vrow(vec_ref, V_POST_MIX))
            x2_ref[rs, :] = x2
            h2 = (((x2 * _rms(x2)) * _vrow(vec_ref, V_PRE_FFN)) * (1.0 + _vrow(vec_ref, V_SC_F))
                  + _vrow(vec_ref, V_SH_F)).astype(BF16)
            h2_ref[rs, :] = _permute_bf16(pm_ref[0], h2)

        projs = [project(0)]
        halo = carry_ref[...]
        mixed_prev = None
        for s in range(nsub):
            if s + 1 < nsub:
                projs.append(project(s + 1))
            mix(s, projs[s], halo)
            halo = projs[s][sub - POOL_HALO:sub, 2 * A_WIDTH:]
            mixed = _dot(cat_ref[s * sub:(s + 1) * sub, :], wout_ref[...])
            if mixed_prev is not None:
                finish(s - 1, mixed_prev)
            mixed_prev = mixed
        carry_ref[...] = halo
        finish(nsub - 1, mixed_prev)

    row = lambda i: (i, 0)
    return _call(
        body, name="mix_fwd", grid=(nt,),
        in_specs=[_tiled((ts, d), row), _whole(vecs.shape), _resident(w_in_b.shape), _resident(w_out_b.shape),
                  _whole(sgu_g.shape), _whole(wm_b.shape), _whole(bsb.shape), _whole(wp_b.shape), _whole(ps.shape),
                  _whole(pmats.shape)],
        out_specs=[_tiled((ts, d), row), _tiled((ts, n_proj), row), _tiled((ts, d), row),
                   _tiled((ts, d), row), _tiled((ts, d), row), _tiled((ts, d), row)],
        out_shape=[jax.ShapeDtypeStruct((s_len, d), BF16), jax.ShapeDtypeStruct((s_len, n_proj), BF16),
                   jax.ShapeDtypeStruct((s_len, d), BF16), jax.ShapeDtypeStruct((s_len, d), F32),
                   jax.ShapeDtypeStruct((s_len, d), F32), jax.ShapeDtypeStruct((s_len, d), BF16)],
        scratch_shapes=[pltpu.VMEM((POOL_HALO, A_WIDTH), F32)],
        args=(x, vecs, w_in_b, w_out_b, sgu_g, wm_b, bsb, wp_b, ps, pmats))


def _perm_mats(ts):
    p = jnp.arange(ts)
    pm = (((p % 8) * (ts // 8) + p // 8)[:, None] == p[None, :]).astype(BF16)
    return jnp.stack([pm, pm.T])


def _permute_bf16(pm, xb):
    return _dot(pm, xb).astype(BF16)


def _permute_f32(pm, x):
    hi = x.astype(BF16)
    lo = (x - hi.astype(F32)).astype(BF16)
    return _dot(pm, hi) + _dot(pm, lo)


def _conv_out(u, um2, um1, cv_ref, cols):
    return (cv_ref[3:4, cols] + um2 * cv_ref[0:1, cols] + um1 * cv_ref[1:2, cols] + u * cv_ref[2:3, cols])


F_LOSS, F_DGF, F_DPOSTFFN = 0, 1, 2
B_DSHF, B_DSCF, B_DPREFFN, B_DGM, B_DPOSTMIX = 0, 1, 2, 3, 4
M_DSHM, M_DSCM, M_DPREMIX = 0, 1, 2
C_DCB, C_DCW = 0, 1


def _ffn_fwd(h2p, x2, tgt, w_up_b, w_down_b, cvec, vecs, pmats, ts):
    s_len, d = x2.shape
    ff2 = w_up_b.shape[1]
    ff = ff2 // 2
    nt = s_len // ts
    chunks = _ff_chunks(ff)

    def body(h2_ref, x2_ref, t_ref, wu_ref, wd_ref, cv_ref, vec_ref, pm_ref,
             up_ref, y_ref, act_ref, dy_ref, df_ref, acc_ref, carry_ref):
        @pl.when(pl.program_id(0) == 0)
        def _():
            carry_ref[...] = jnp.zeros_like(carry_ref)
            acc_ref[...] = jnp.zeros_like(acc_ref)

        h2v = h2_ref[...]

        def up_dots(o, w):
            return [_dot(h2v, wu_ref[:, base + o:base + o + w]) for base in (0, ff)]

        f = None
        pending = None
        nxt = up_dots(*chunks[0])
        for ci, (o, w) in enumerate(chunks):
            us = nxt
            if ci + 1 < len(chunks):
                nxt = up_dots(*chunks[ci + 1])
            if pending is not None:
                part = _dot(pending[0], wd_ref[pending[1]:pending[1] + pending[2], :])
                f = part if f is None else f + part
            sub0 = lax.broadcasted_iota(jnp.int32, (8, w), 0) == 0
            ys = []
            for base, u in zip((0, ff), us):
                cols = slice(base + o, base + o + w)
                up_ref[:, cols] = u.astype(BF16)
                b1 = jnp.where(sub0, pltpu.roll(carry_ref[8:16, cols], 1, 0), pltpu.roll(u[ts - 8:ts], 1, 0))
                b2 = jnp.where(sub0, pltpu.roll(carry_ref[0:8, cols], 1, 0), pltpu.roll(u[ts - 16:ts - 8], 1, 0))
                um1 = jnp.concatenate([b1, u[:ts - 8]], axis=0)
                um2 = jnp.concatenate([b2, b1, u[:ts - 16]], axis=0)
                ys.append(_conv_out(u, um2, um1, cv_ref, cols))
                carry_ref[:, cols] = u[ts - 16:ts]
            gate, val = ys
            sg = _sigmoid(gate)
            gs = gate * sg
            act = (gs * val).astype(BF16)
            act_ref[:, o:o + w] = act
            y_ref[:, o:o + w] = (val * (sg + gs * (1.0 - sg))).astype(BF16)
            y_ref[:, ff + o:ff + o + w] = gs.astype(BF16)
            pending = (act, o, w)
        f = f + _dot(pending[0], wd_ref[pending[1]:pending[1] + pending[2], :])
        f = _permute_f32(pm_ref[1], f)
        r3 = _rms(f)
        fhat = f * r3
        post = _vrow(vec_ref, V_POST_FFN)
        g_f = _vrow(vec_ref, V_G_F)
        fn = fhat * post
        e = (x2_ref[...] + g_f * fn) - t_ref[...]
        dy = e * (1.0 / d)
        dy_ref[...] = dy
        dfn = dy * g_f
        acc_ref[F_LOSS:F_LOSS + 1, :] += _colsum(e * e)
        acc_ref[F_DGF:F_DGF + 1, :] += _colsum(dy * fn)
        acc_ref[F_DPOSTFFN:F_DPOSTFFN + 1, :] += _colsum(dfn * fhat)
        dfhat = dfn * post
        df = (r3 * (dfhat - fhat * _rowmean(dfhat * fhat))).astype(BF16)
        df_ref[...] = _permute_bf16(pm_ref[0], df)

    row = lambda i: (i, 0)
    return pl.pallas_call(
        body, name="ffn_fwd", grid=(nt,),
        in_specs=[_tiled((ts, d), row), _tiled((ts, d), row), _tiled((ts, d), row), _resident(w_up_b.shape),
                  _resident(w_down_b.shape), _whole(cvec.shape), _whole(vecs.shape), _whole(pmats.shape)],
        out_specs=[_tiled((ts, ff2), row), _tiled((ts, ff2), row), _tiled((ts, ff), row), _tiled((ts, d), row),
                   _tiled((ts, d), row), _whole((8, d))],
        out_shape=[jax.ShapeDtypeStruct((s_len, ff2), BF16), jax.ShapeDtypeStruct((s_len, ff2), BF16),
                   jax.ShapeDtypeStruct((s_len, ff), BF16), jax.ShapeDtypeStruct((s_len, d), F32),
                   jax.ShapeDtypeStruct((s_len, d), BF16), jax.ShapeDtypeStruct((8, d), F32)],
        scratch_shapes=[pltpu.VMEM((16, ff2), F32)],
        compiler_params=_seq_params(),
    )(h2p, x2, tgt, w_up_b, w_down_b, cvec, vecs, pmats)


def _ffn_bwd(dfp, upp, yp, x2, dy, mixed, w_up_b, w_down_b, cvec, vecs, pmats, ts, rider=None):
    s_len, d = x2.shape
    ff2 = w_up_b.shape[1]
    ff = ff2 // 2
    nt = s_len // ts
    chunks = _ff_chunks(ff, 512)

    def body(df_ref, up_ref, y_ref, x2_ref, dy_ref, mx_ref, wu_ref, wd_ref, cv_ref, vec_ref, pm_ref,
             dup_ref, dx2_ref, dmx_ref, accc_ref, acc_ref, carry_ref):
        @pl.when(pl.program_id(0) == 0)
        def _():
            carry_ref[...] = jnp.zeros_like(carry_ref)
            accc_ref[...] = jnp.zeros_like(accc_ref)
            acc_ref[...] = jnp.zeros_like(acc_ref)

        dfv = df_ref[...]

        def dh2_add(acc, dups, o, w):
            for base, dup in zip((0, ff), dups):
                part = _dot_nt(dup, wu_ref[:, base + o:base + o + w])
                acc = part if acc is None else acc + part
            return acc

        dh2 = None
        pending = None
        nxt = _dot_nt(dfv, wd_ref[chunks[0][0]:chunks[0][0] + chunks[0][1], :])
        for ci, (o, w) in enumerate(chunks):
            dact = nxt
            if ci + 1 < len(chunks):
                o2, w2 = chunks[ci + 1]
                nxt = _dot_nt(dfv, wd_ref[o2:o2 + w2, :])
            if pending is not None:
                dh2 = dh2_add(dh2, *pending)
            sub7 = lax.broadcasted_iota(jnp.int32, (8, w), 0) == 7
            dups = []
            dys = (dact * y_ref[:, o:o + w].astype(F32), dact * y_ref[:, ff + o:ff + o + w].astype(F32))
            for base, dyv in zip((0, ff), dys):
                cols = slice(base + o, base + o + w)
                u = up_ref[:, cols].astype(F32)
                e0 = jnp.where(sub7, pltpu.roll(carry_ref[0:8, cols], 7, 0), pltpu.roll(dyv[0:8], 7, 0))
                e1 = jnp.where(sub7, pltpu.roll(carry_ref[8:16, cols], 7, 0), pltpu.roll(dyv[8:16], 7, 0))
                dyp1 = jnp.concatenate([dyv[8:], e0], axis=0)
                dyp2 = jnp.concatenate([dyv[16:], e0, e1], axis=0)
                accc_ref[C_DCB:C_DCB + 1, cols] += _colsum(dyv)
                accc_ref[C_DCW + 0:C_DCW + 1, cols] += _colsum(dyp2 * u)
                accc_ref[C_DCW + 1:C_DCW + 2, cols] += _colsum(dyp1 * u)
                accc_ref[C_DCW + 2:C_DCW + 3, cols] += _colsum(dyv * u)
                dup = (dyv * cv_ref[2:3, cols] + dyp1 * cv_ref[1:2, cols] + dyp2 * cv_ref[0:1, cols]).astype(BF16)
                dup_ref[:, cols] = dup
                dups.append(dup)
                carry_ref[:, cols] = dyv[0:16]
            pending = (dups, o, w)
        dh2 = dh2_add(dh2, *pending)
        dh2 = _permute_f32(pm_ref[1], dh2)
        x2 = x2_ref[...]
        r2 = _rms(x2)
        xn = x2 * r2
        pre = _vrow(vec_ref, V_PRE_FFN)
        one_sc = 1.0 + _vrow(vec_ref, V_SC_F)
        acc_ref[B_DSHF:B_DSHF + 1, :] += _colsum(dh2)
        acc_ref[B_DSCF:B_DSCF + 1, :] += _colsum(dh2 * (xn * pre))
        acc_ref[B_DPREFFN:B_DPREFFN + 1, :] += _colsum(dh2 * xn * one_sc)
        dxn = dh2 * pre * one_sc
        dx2 = dy_ref[...] + r2 * (dxn - xn * _rowmean(dxn * xn))
        dx2_ref[...] = dx2
        mixed = mx_ref[...]
        rm = _rms(mixed)
        mhat = mixed * rm
        post = _vrow(vec_ref, V_POST_MIX)
        acc_ref[B_DGM:B_DGM + 1, :] += _colsum(dx2 * (mhat * post))
        dmn = dx2 * _vrow(vec_ref, V_G_M)
        acc_ref[B_DPOSTMIX:B_DPOSTMIX + 1, :] += _colsum(dmn * mhat)
        dmhat = dmn * post
        dmx_ref[...] = (rm * (dmhat - mhat * _rowmean(dmhat * mhat))).astype(BF16)

    rev = lambda i: (nt - 1 - i, 0)
    return _call(
        body, name="ffn_bwd", grid=(nt,), rider=rider,
        in_specs=[_tiled((ts, d), rev), _tiled((ts, ff2), rev), _tiled((ts, ff2), rev), _tiled((ts, d), rev),
                  _tiled((ts, d), rev), _tiled((ts, d), rev), _resident(w_up_b.shape), _resident(w_down_b.shape),
                  _whole(cvec.shape), _whole(vecs.shape), _whole(pmats.shape)],
        out_specs=[_tiled((ts, ff2), rev), _tiled((ts, d), rev), _tiled((ts, d), rev), _whole((8, ff2)),
                   _whole((8, d))],
        out_shape=[jax.ShapeDtypeStruct((s_len, ff2), BF16), jax.ShapeDtypeStruct((s_len, d), F32),
                   jax.ShapeDtypeStruct((s_len, d), BF16), jax.ShapeDtypeStruct((8, ff2), F32),
                   jax.ShapeDtypeStruct((8, d), F32)],
        scratch_shapes=[pltpu.VMEM((16, ff2), F32)],
        args=(dfp, upp, yp, x2, dy, mixed, w_up_b, w_down_b, cvec, vecs, pmats))


def _mix_bwd(dmixed, dx2, x, proj, vecs, w_in_b, w_out_b, sgu_g, wm_b, wmt_b, bsb, wp_b, ps, ts, rider=None):
    s_len, d = x.shape
    nt = s_len // ts
    nblk = ts // HEAD
    n_proj = proj.shape[1]
    per = ts // POOL_HALO
    ext_len = ts + POOL_HALO

    def body(dmx_ref, dx2_ref, x_ref, proj_ref, projh_ref, vec_ref, win_ref, wout_ref, sg_ref, wm_ref, wmt_ref,
             bs_ref, wp_ref, ps_ref,
             gx_ref, dproj_ref, acc_ref, dwm_out, dwp_out, db_ref, dg_ref, dps_ref,
             carry_ref, dwm_ref, dwp_ref, dbz_ref):
        i = pl.program_id(0)
        tile = nt - 1 - i

        @pl.when(i == 0)
        def _():
            carry_ref[...] = jnp.zeros_like(carry_ref)
            for r in (acc_ref, dwm_ref, dwp_ref, dbz_ref, dg_ref, dps_ref):
                r[...] = jnp.zeros_like(r)

        sub = FFN_TS
        nsub = ts // sub
        ext = sub + POOL_HALO

        def cotangent(s):
            return _dot_nt(dmx_ref[s * sub:(s + 1) * sub, :], wout_ref[...])

        def back(s, dcat, halo, later):
            r0 = s * sub
            rows = slice(r0, r0 + sub)
            t_glob = lax.broadcasted_iota(jnp.int32, (sub, HEAD), 0) + (tile * ts + r0)
            for h in range(N_HEADS):
                hs = slice(h * HEAD, (h + 1) * HEAD)
                vs = slice(A_WIDTH + h * HEAD, A_WIDTH + (h + 1) * HEAD)
                au = proj_ref[rows, hs].astype(F32)
                av = proj_ref[rows, vs].astype(F32)
                u = _gelu(au)
                v = _gelu(av)
                rv = _rms(v)
                vhat = v * rv
                gain = sg_ref[h:h + 1, :]
                vn = (vhat * gain).astype(BF16)
                dout = dcat[:, hs]
                du_parts, dvn_parts = [], []
                for b in range(sub // HEAD):
                    rs = slice(b * HEAD, (b + 1) * HEAD)
                    z = _dot(wm_ref[h], vn[rs]) + bs_ref[h]
                    du_parts.append(dout[rs] * z)
                    dz = dout[rs] * u[rs]
                    dbz_ref[h] += dz
                    dzb = dz.astype(BF16)
                    dwm_ref[h] += _dot_nt(dzb, vn[rs])
                    dvn_parts.append(_dot(wmt_ref[h], dzb))
                du = jnp.concatenate(du_parts, axis=0)
                dvn = jnp.concatenate(dvn_parts, axis=0)
                dg_ref[h:h + 1, :] += _colsum(dvn * vhat)
                dvhat = dvn * gain
                dv = rv * (dvhat - vhat * _rowmean(dvhat * vhat))
                dproj_ref[rows, hs] = (du * _gelu_grad(au)).astype(BF16)
                dproj_ref[rows, vs] = (dv * _gelu_grad(av)).astype(BF16)
            firsts = []
            for g in range(len(POOL_WINDOWS)):
                gs = slice(g * HEAD, (g + 1) * HEAD)
                pcols = slice(2 * A_WIDTH + g * HEAD, 2 * A_WIDTH + (g + 1) * HEAD)
                p = proj_ref[rows, pcols].astype(F32)
                pb = _pool_fwd(p, halo[:, gs], g, t_glob).astype(BF16)
                dyb = dcat[:, A_WIDTH + g * HEAD:A_WIDTH + (g + 1) * HEAD]
                dps_ref[0:1, gs] += _colsum(dyb * _dot(pb, wp_ref[g]))
                dyl = (dyb * ps_ref[0:1, gs]).astype(BF16)
                dwp_ref[g] += _dot_tn(pb, dyl)
                dpooled = _dot_nt(dyl, wp_ref[g])
                cnt = jnp.minimum(t_glob + 1, POOL_WINDOWS[g]).astype(F32)
                q = dpooled / cnt
                acc = jnp.concatenate([q, later[:, gs]], axis=0)
                for step in range(g + 1):
                    acc = acc + pltpu.roll(acc, ext - (1 << step), 0)
                dproj_ref[rows, pcols] = (acc[:sub] - dpooled).astype(BF16)
                firsts.append(q[0:POOL_HALO])
            return jnp.concatenate(firsts, axis=1)

        def finish(s, dh1):
            rows = slice(s * sub, (s + 1) * sub)
            x = x_ref[rows, :]
            r1 = _rms(x)
            xn = x * r1
            pre = _vrow(vec_ref, V_PRE_MIX)
            one_sc = 1.0 + _vrow(vec_ref, V_SC_M)
            acc_ref[M_DSHM:M_DSHM + 1, :] += _colsum(dh1)
            acc_ref[M_DSCM:M_DSCM + 1, :] += _colsum(dh1 * (xn * pre))
            acc_ref[M_DPREMIX:M_DPREMIX + 1, :] += _colsum(dh1 * xn * one_sc)
            dxn = dh1 * pre * one_sc
            gx_ref[rows, :] = dx2_ref[rows, :] + r1 * (dxn - xn * _rowmean(dxn * xn))

        nxt = cotangent(nsub - 1)
        later = carry_ref[...]
        dh1_prev = None
        for s in reversed(range(nsub)):
            dcat = nxt
            if s > 0:
                nxt = cotangent(s - 1)
                halo = proj_ref[s * sub - POOL_HALO:s * sub, 2 * A_WIDTH:].astype(F32)
            else:
                halo = jnp.where(tile > 0, projh_ref[:, 2 * A_WIDTH:].astype(F32), 0.0)
            later = back(s, dcat, halo, later)
            dh1 = _dot_nt(dproj_ref[s * sub:(s + 1) * sub, :], win_ref[...])
            if dh1_prev is not None:
                finish(s + 1, dh1_prev)
            dh1_prev = dh1
        carry_ref[...] = later
        finish(0, dh1_prev)

        @pl.when(i == nt - 1)
        def _():
            dwm_out[...] = dwm_ref[...].astype(BF16)
            dwp_out[...] = dwp_ref[...].astype(BF16)
            db_ref[...] = jnp.zeros_like(db_ref)
            for h in range(N_HEADS):
                db_ref[h:h + 1, :] = jnp.sum(dbz_ref[h].T, axis=0, keepdims=True)

    rev = lambda i: (nt - 1 - i, 0)
    halo_map = lambda i: (jnp.maximum((nt - 1 - i) * per - 1, 0), 0)
    hshape = (N_HEADS, HEAD, HEAD)
    return _call(
        body, name="mix_bwd", grid=(nt,), rider=rider,
        in_specs=[_tiled((ts, d), rev), _tiled((ts, d), rev), _tiled((ts, d), rev), _tiled((ts, n_proj), rev),
                  _tiled((POOL_HALO, n_proj), halo_map), _whole(vecs.shape), _resident(w_in_b.shape),
                  _resident(w_out_b.shape), _whole(sgu_g.shape), _whole(wm_b.shape), _whole(wmt_b.shape),
                  _whole(bsb.shape), _whole(wp_b.shape), _whole(ps.shape)],
        out_specs=[_tiled((ts, d), rev), _tiled((ts, n_proj), rev), _whole((8, d)), _whole(hshape), _whole(hshape),
                   _whole((8, HEAD)), _whole((8, HEAD)), _whole((8, A_WIDTH))],
        out_shape=[jax.ShapeDtypeStruct((s_len, d), F32), jax.ShapeDtypeStruct((s_len, n_proj), BF16),
                   jax.ShapeDtypeStruct((8, d), F32), jax.ShapeDtypeStruct(hshape, BF16),
                   jax.ShapeDtypeStruct(hshape, BF16), jax.ShapeDtypeStruct((8, HEAD), F32),
                   jax.ShapeDtypeStruct((8, HEAD), F32), jax.ShapeDtypeStruct((8, A_WIDTH), F32)],
        scratch_shapes=[pltpu.VMEM((POOL_HALO, A_WIDTH), F32), pltpu.VMEM(hshape, F32), pltpu.VMEM(hshape, F32),
                        pltpu.VMEM(hshape, F32)],
        args=(dmixed, dx2, x, proj, proj, vecs, w_in_b, w_out_b, sgu_g, wm_b, wmt_b, bsb, wp_b, ps))


def _wgrad(a, b, tn, ts, name, rider=None):
    s_len, m = a.shape
    n = b.shape[1]
    ts = min(ts, s_len)

    def body(a_ref, b_ref, o_ref):
        @pl.when(pl.program_id(1) == 0)
        def _():
            o_ref[...] = jnp.zeros_like(o_ref)

        o_ref[...] += _dot_tn(a_ref[...], b_ref[...])

    (g,), r_out = _call(
        body, name=name, grid=(n // tn, s_len // ts), rider=rider,
        in_specs=[pl.BlockSpec((ts, m), lambda j, s: (s, 0)), pl.BlockSpec((ts, tn), lambda j, s: (s, j))],
        out_specs=[pl.BlockSpec((m, tn), lambda j, s: (0, j))],
        out_shape=[jax.ShapeDtypeStruct((m, n), F32)], scratch_shapes=[], args=(a, b))
    return g, r_out


def _wgrad_pair(a1, b1, a2, b2, after, ts, name):
    s_len = a1.shape[0]
    ts = min(ts, s_len)
    shapes = [(a1.shape[1], b1.shape[1]), (a2.shape[1], b2.shape[1])]

    def body(a1_ref, b1_ref, a2_ref, b2_ref, after_ref, o1_ref, o2_ref):
        @pl.when(pl.program_id(0) == 0)
        def _():
            o1_ref[...] = jnp.zeros_like(o1_ref)
            o2_ref[...] = jnp.zeros_like(o2_ref)

        o1_ref[...] += _dot_tn(a1_ref[...], b1_ref[...])
        o2_ref[...] += _dot_tn(a2_ref[...], b2_ref[...])

    row = lambda s: (s, 0)
    return _call(
        body, name=name, grid=(s_len // ts,),
        in_specs=[pl.BlockSpec((ts, t.shape[1]), row) for t in (a1, b1, a2, b2)] + [_whole(after.shape)],
        out_specs=[_whole(sh) for sh in shapes],
        out_shape=[jax.ShapeDtypeStruct(sh, F32) for sh in shapes], scratch_shapes=[],
        args=(a1, b1, a2, b2, after))[0]


def _adamw_big(g, w, m, v, name):
    r, cdim = g.shape
    tr = r
    while tr * cdim * 4 > (3 << 19) and tr % 16 == 0:
        tr //= 2

    def body(g_ref, w_ref, m_ref, v_ref, d_ref, nm_ref, nv_ref):
        delta, m2, v2 = _adamw_math(w_ref[0], g_ref[...], m_ref[0], v_ref[0])
        d_ref[0] = delta
        nm_ref[0] = m2
        nv_ref[0] = v2

    s3 = pl.BlockSpec((1, tr, cdim), lambda i: (0, i, 0))
    return pl.pallas_call(
        body, name=name, grid=(r // tr,),
        in_specs=[pl.BlockSpec((tr, cdim), lambda i: (i, 0)), s3, s3, s3],
        out_specs=[s3, s3, s3],
        out_shape=[jax.ShapeDtypeStruct(w.shape, F32)] * 3,
        compiler_params=pltpu.CompilerParams(dimension_semantics=("parallel",), vmem_limit_bytes=VMEM_LIMIT_BYTES),
    )(g, w, m, v)


def _wada_update(sct, gm, w, m, v):
    _, r, cdim = w.shape
    tr = 256
    kp = sct.shape[1]

    def body(s_ref, g_ref, w_ref, m_ref, v_ref, gw_ref, d_ref, nm_ref, nv_ref):
        g = _dot(s_ref[...], g_ref[...])
        gw_ref[0] = g
        delta, m2, v2 = _adamw_math(w_ref[0], g, m_ref[0], v_ref[0])
        d_ref[0] = delta
        nm_ref[0] = m2
        nv_ref[0] = v2

    s3 = pl.BlockSpec((1, tr, cdim), lambda i: (0, i, 0))
    return pl.pallas_call(
        body, name="wada_update", grid=(r // tr,),
        in_specs=[pl.BlockSpec((tr, kp), lambda i: (i, 0)), _whole(gm.shape), s3, s3, s3],
        out_specs=[s3, s3, s3, s3],
        out_shape=[jax.ShapeDtypeStruct(w.shape, F32)] * 4,
        compiler_params=pltpu.CompilerParams(dimension_semantics=("parallel",), vmem_limit_bytes=VMEM_LIMIT_BYTES),
    )(sct, gm, w, m, v)


def _small_update(g1, g2, g2s, gwm, gwp, gbz, gsg, gps, params):
    names = ["b_ada", "pre_mix_g", "post_mix_g", "sgu_norm_g", "w_spatial", "b_spatial", "w_pool", "pool_scale",
             "pre_ffn_g", "post_ffn_g", "conv_w", "conv_b"]
    d = g1.shape[2]
    flat_in = [g1, g2, g2s, gwm, gwp, gbz, gsg, gps]
    n_g = len(flat_in)
    for nm in names:
        flat_in += list(params[nm])

    def body(*refs):
        g1_ref, g2_ref, g2s_ref, gwm_ref, gwp_ref, gbz_ref, gsg_ref, gps_ref = refs[:n_g]
        wmv = refs[n_g:n_g + 3 * len(names)]
        loss_ref = refs[n_g + 3 * len(names)]
        outs = refs[n_g + 3 * len(names) + 1:]

        def dsum(ref, idx):
            acc = ref[(0,) + idx].astype(F32)
            for dev in range(1, N_DEV):
                acc = acc + ref[(dev,) + idx].astype(F32)
            return acc

        def apply(pi, g, widx, oidx):
            w_ref, m_ref, v_ref = wmv[3 * pi:3 * pi + 3]
            g_ref, d_ref, nm_ref, nv_ref = outs[4 * pi:4 * pi + 4]
            delta, m2, v2 = _adamw_math(w_ref[widx], g, m_ref[widx], v_ref[widx])
            g_ref[oidx] = g
            d_ref[oidx] = delta
            nm_ref[oidx] = m2
            nv_ref[oidx] = v2

        def row1(base, r):
            return (slice(base + r, base + r + 1), slice(None))

        tot = dsum(g1_ref, row1(0, F_LOSS))
        loss_ref[...] = jnp.zeros(loss_ref.shape, F32) + jnp.sum(tot) * (0.5 / d)
        mod_rows = [row1(16, M_DSHM), row1(16, M_DSCM), row1(8, B_DGM), row1(8, B_DSHF), row1(8, B_DSCF),
                    row1(0, F_DGF)]
        for j, rr in enumerate(mod_rows):
            cs = (slice(None), slice(j * d, (j + 1) * d))
            apply(0, dsum(g1_ref, rr), cs, cs)
        full = (slice(None), slice(None))
        apply(1, dsum(g1_ref, row1(16, M_DPREMIX)), full, full)
        apply(2, dsum(g1_ref, row1(8, B_DPOSTMIX)), full, full)
        apply(3, dsum(gsg_ref, (slice(0, N_HEADS), slice(None))), (0,), (0,))
        pos_i = lax.broadcasted_iota(jnp.int32, (HEAD, HEAD), 0)
        pos_j = lax.broadcasted_iota(jnp.int32, (HEAD, HEAD), 1)
        causal = (pos_j // CHUNK) <= (pos_i // CHUNK)
        for h in range(N_HEADS):
            blk = (slice(h * HEAD, (h + 1) * HEAD), slice(None))
            apply(4, jnp.where(causal, dsum(gwm_ref, blk), 0.0), (0, h), (0, h))
            apply(5, dsum(gbz_ref, (slice(h, h + 1), slice(None))), (0, slice(h, h + 1)), (0, slice(h, h + 1)))
            apply(6, dsum(gwp_ref, blk), (0, h), (0, h))
        apply(7, dsum(gps_ref, (slice(0, 1), slice(None))), full, full)
        apply(8, dsum(g1_ref, row1(8, B_DPREFFN)), full, full)
        apply(9, dsum(g1_ref, row1(0, F_DPOSTFFN)), full, full)
        apply(10, dsum(g2s_ref, (slice(C_DCW, C_DCW + 3), slice(None))), (0,), (0,))
        apply(11, dsum(g2_ref, (slice(C_DCB, C_DCB + 1), slice(None))), full, full)

    out_shape = [jax.ShapeDtypeStruct((8, HEAD), F32)]
    for nm in names:
        out_shape += [jax.ShapeDtypeStruct(params[nm][0].shape, F32)] * 4
    res = pl.pallas_call(
        body, name="small_update", out_shape=out_shape,
        compiler_params=pltpu.CompilerParams(vmem_limit_bytes=VMEM_LIMIT_BYTES),
    )(*flat_in)
    out = {nm: tuple(res[1 + 4 * i:5 + 4 * i]) for i, nm in enumerate(names)}
    return res[0], out


def kernel(x, c, w_ada, b_ada, pre_mix_g, post_mix_g, w_in, sgu_norm_g, w_spatial, b_spatial, w_pool, pool_scale, w_out, pre_ffn_g, post_ffn_g, w_up, conv_w, conv_b, w_down, loss_target, m_w_ada, m_b_ada, m_pre_mix_g, m_post_mix_g, m_w_in, m_sgu_norm_g, m_w_spatial, m_b_spatial, m_w_pool, m_pool_scale, m_w_out, m_pre_ffn_g, m_post_ffn_g, m_w_up, m_conv_w, m_conv_b, m_w_down, v_w_ada, v_b_ada, v_pre_mix_g, v_post_mix_g, v_w_in, v_sgu_norm_g, v_w_spatial, v_b_spatial, v_w_pool, v_pool_scale, v_w_out, v_pre_ffn_g, v_post_ffn_g, v_w_up, v_conv_w, v_conv_b, v_w_down):
    xi, yi, ci = _mesh_pos()
    k_me = 2 * xi + yi
    dev = 2 * k_me + ci
    s_len, d = x.shape[1], x.shape[2]
    x2d = x[0]
    tgt = loss_target[0]
    ff2 = conv_b.shape[1]
    n_ada = w_ada.shape[2]
    n_cw = conv_w.shape[2]

    k_idx = k_me.reshape(1).astype(jnp.int32)
    flags4 = (True, False, True, False)
    (w_in_s, w_out_s, w_up_s, w_down_s), (w_in_f, w_out_f, w_up_f, w_down_f) = _cast_bf16(
        [w_in[0], w_out[0], w_up[0], w_down[0]], flags4, k_idx)
    mix_flags = (True, False)
    fly_mix = _gather_begin([w_in_s, w_out_s], [w_in_f, w_out_f], mix_flags, k_idx, "gather_begin_mix")

    cw_blk = jnp.concatenate([conv_w[0], jnp.zeros((5, n_cw), F32)], axis=0)
    c_blk = c.reshape(8, d // 8) + fly_mix[3][0:1, 0:1]
    c_all, cw_all = _run_rider(_allgather_rider([c_blk, cw_blk]), "gather_c_convw")
    c_all = c_all.reshape(N_DEV, 8, d // 8).reshape(N_DEV, d)
    cw_full = jnp.concatenate([cw_all[16 * k:16 * k + 8] for k in range(N_CHIP)], axis=1)
    cvec = jnp.concatenate([cw_full[0:3], conv_b, jnp.zeros((4, ff2), F32)], axis=0)
    b_shard = lax.dynamic_slice_in_dim(b_ada, k_me * n_ada, n_ada, axis=1)
    mod_k, sc_all = _mod_shard(c_all, w_ada[0], b_shard)
    (mod_g,) = _run_rider(_allgather_rider([mod_k]), "gather_mod")
    mod_all = jnp.concatenate([mod_g[16 * k:16 * k + 8] for k in range(N_CHIP)], axis=1)
    mod_me = lax.dynamic_slice_in_dim(mod_all, dev, 1, axis=0).reshape(6, d)
    vecs = jnp.concatenate([mod_me, pre_mix_g, post_mix_g, pre_ffn_g, post_ffn_g, jnp.zeros((6, d), F32)], axis=0)

    fly_ffn = _gather_begin([w_up_s, w_down_s], [w_up_f, w_down_f], mix_flags, mod_g, "gather_begin_ffn")
    w_in_b, w_out_b = _gather_finish(_gather_end(fly_mix, mix_flags, fly_ffn[3], "gather_end_mix"), mix_flags,
                                     "gather_finish_mix")

    pos = jnp.arange(HEAD)
    causal = (pos[None, :] // CHUNK) <= (pos[:, None] // CHUNK)
    wm = jnp.where(causal[None], w_spatial[0], 0.0)
    wm_b = wm.astype(BF16)
    wmt_b = jnp.swapaxes(wm, 1, 2).astype(BF16)
    bsb = jnp.broadcast_to(b_spatial[0][:, :, None], (N_HEADS, HEAD, HEAD))
    wp_b = w_pool[0].astype(BF16)
    sgu_g = jnp.concatenate([sgu_norm_g[0], jnp.zeros((4, HEAD), F32)], axis=0)
    ps = jnp.concatenate([pool_scale, jnp.zeros((7, A_WIDTH), F32)], axis=0)

    pmats = _perm_mats(FFN_TS)
    h1, proj, cat, mixed, x2, h2p = _mix_fwd(x2d, vecs, w_in_b, w_out_b, sgu_g, wm_b, bsb, wp_b, ps, pmats, ts=512)[0]
    w_up_b, w_down_b = _gather_finish(_gather_end(fly_ffn, mix_flags, h2p, "gather_end_ffn"), mix_flags,
                                      "gather_finish_ffn")
    up, yv, act, dy, dfp, acc_f = _ffn_fwd(h2p, x2, tgt, w_up_b, w_down_b, cvec, vecs, pmats, ts=FFN_TS)

    c_idx = ci.reshape(1).astype(jnp.int32)
    g_w_down, _ = _wgrad(act, dfp, d, 1024, "wgrad_down")
    (dup, dx2, dmixed, acc_c, acc_b), (land_down,) = _ffn_bwd(
        dfp, up, yv, x2, dy, mixed, w_up_b, w_down_b, cvec, vecs, pmats, ts=FFN_TS,
        rider=_sibling_rider([g_w_down], (False,)))
    (part_down,) = _sum_with_sibling([g_w_down], [land_down], (False,), c_idx, "pair_sum_down")
    g_w_up, (chips_down,) = _wgrad(h2p, dup, ff2 // 2, 2048, "wgrad_up", rider=_chips_rider([part_down]))
    fly_up = _sibling_begin(g_w_up, True, chips_down, "sibling_begin_up")
    gx, dproj, acc_m, dwm, dwp, dbz, dsg, dps = _mix_bwd(
        dmixed, dx2, x2d, proj, vecs + fly_up[3][0:1, 0:1], w_in_b, w_out_b, sgu_g, wm_b, wmt_b, bsb, wp_b, ps,
        ts=512)[0]
    g_w_up, land_up = _sibling_end(fly_up, True, dproj, "sibling_end_up")
    (part_up,) = _sum_with_sibling([g_w_up], [land_up], (True,), c_idx, "pair_sum_up")
    fly_chips_up = _chips_begin([part_up], c_idx, "chips_begin_up")
    g1 = jnp.concatenate([acc_f, acc_b, acc_m], axis=0)
    hflat = (N_HEADS * HEAD, HEAD)
    small_bufs = [lax.dynamic_update_slice(jnp.zeros((N_DEV * t.shape[0], t.shape[1]), t.dtype), t,
                                           (dev * t.shape[0], 0))
                  for t in (g1, acc_c, dwm.reshape(hflat), dwp.reshape(hflat), dbz, dsg, dps)]
    fly_small = _split_begin(small_bufs, 3 * len(small_bufs), _small_ici_copies, fly_chips_up[3],
                             "small_gather_begin")
    g_w_out, g_w_in = _wgrad_pair(cat, dmixed, h1, dproj, fly_small[3], 1024, "wgrad_mix")
    land_mix = _run_rider(_sibling_rider([g_w_in, g_w_out], (True, False)), "reduce_to_sibling")
    parts_mix = _sum_with_sibling([g_w_in, g_w_out], land_mix, (True, False), c_idx, "pair_sum_mix")
    (part_up,), (chips_up,) = _chips_end(fly_chips_up, parts_mix[0], "chips_end_up")
    fly_chips_mix = _chips_begin(parts_mix, chips_up, "chips_begin_mix")

    def adamw_of(names, reduced):
        res = {}
        for nm, red in zip(names, reduced):
            w, m, v = big_wmv[nm]
            g = red.reshape(w.shape[1], w.shape[2])
            res[nm] = (g.reshape(w.shape),) + tuple(_adamw_big(g, w, m, v, "adamw_" + nm))
        return res

    big_wmv = {"w_in": (w_in, m_w_in, v_w_in), "w_out": (w_out, m_w_out, v_w_out),
               "w_up": (w_up, m_w_up, v_w_up), "w_down": (w_down, m_w_down, v_w_down)}
    big = adamw_of(("w_up", "w_down"), _sum_chips_and_share([chips_up, chips_down], [part_up, part_down],
                                                           fly_chips_mix[3], "sum_share_ffn"))

    gathered = _small_finish(_split_end(fly_small, _small_ici_copies, big["w_down"][1], "small_gather_end"),
                             "small_gather_finish")
    g1a, g2a, gwm, gwp, gbz, gsg, gps = [t.reshape((N_DEV, t.shape[0] // N_DEV, t.shape[1])) for t in gathered]
    g2s = lax.dynamic_slice_in_dim(g2a, k_me * n_cw, n_cw, axis=2)
    params = {
        "b_ada": (b_ada, m_b_ada, v_b_ada), "pre_mix_g": (pre_mix_g, m_pre_mix_g, v_pre_mix_g),
        "post_mix_g": (post_mix_g, m_post_mix_g, v_post_mix_g),
        "sgu_norm_g": (sgu_norm_g, m_sgu_norm_g, v_sgu_norm_g), "w_spatial": (w_spatial, m_w_spatial, v_w_spatial),
        "b_spatial": (b_spatial, m_b_spatial, v_b_spatial), "w_pool": (w_pool, m_w_pool, v_w_pool),
        "pool_scale": (pool_scale, m_pool_scale, v_pool_scale), "pre_ffn_g": (pre_ffn_g, m_pre_ffn_g, v_pre_ffn_g),
        "post_ffn_g": (post_ffn_g, m_post_ffn_g, v_post_ffn_g), "conv_w": (conv_w, m_conv_w, v_conv_w),
        "conv_b": (conv_b, m_conv_b, v_conv_b),
    }
    loss_slab, small = _small_update(g1a, g2a, g2s, gwm, gwp, gbz, gsg, gps, params)

    gmod_all = jnp.concatenate(
        [g1a[:, 16 + M_DSHM], g1a[:, 16 + M_DSCM], g1a[:, 8 + B_DGM], g1a[:, 8 + B_DSHF], g1a[:, 8 + B_DSCF],
         g1a[:, F_DGF]], axis=1)
    gm = lax.dynamic_slice_in_dim(gmod_all, k_me * n_ada, n_ada, axis=1)
    gm = jnp.concatenate([gm, jnp.zeros((HEAD - N_DEV, n_ada), F32)], axis=0)
    sct = jnp.concatenate([sc_all.T, jnp.zeros((d, HEAD - N_DEV), F32)], axis=1)
    ada = tuple(_wada_update(sct, gm, w_ada, m_w_ada, v_w_ada))

    parts_mix, chips_mix = _chips_end(fly_chips_mix, ada[1], "chips_end_mix")
    big.update(adamw_of(("w_in", "w_out"), _sum_chips_and_share(chips_mix, parts_mix, loss_slab, "sum_share_mix")))

    everything = dict(small)
    everything.update(big)
    everything["w_ada"] = ada
    order = ["w_ada", "b_ada", "pre_mix_g", "post_mix_g", "w_in", "sgu_norm_g", "w_spatial", "b_spatial", "w_pool",
             "pool_scale", "w_out", "pre_ffn_g", "post_ffn_g", "w_up", "conv_w", "conv_b", "w_down"]
    outs = [loss_slab[0, 0], gx.reshape(x.shape)]
    for j in range(4):
        outs += [everything[nm][j] for nm in order]
    return tuple(outs)
```

```python
import functools

import jax
import jax.numpy as jnp
from jax import lax
from jax.experimental import pallas as pl
from jax.experimental.pallas import tpu as pltpu

F32 = jnp.float32
BF16 = jnp.bfloat16
MESH = pl.DeviceIdType.MESH

EPS = 1e-6
HEAD = 128
N_HEADS = 4
A_WIDTH = N_HEADS * HEAD
CHUNK = 64
POOL_WINDOWS = (2, 4, 8, 16)
POOL_HALO = 16
FFN_TS = 256
MIX_TS = 512
WGRAD_TS = 2048
WGRAD_TS_WIDE = 1024

ADAM_LR = 0.001
ADAM_B1 = 0.9
ADAM_B2 = 0.999
ADAM_EPS = 1e-08
ADAM_WD = 0.01
ADAM_STEP = 10

VMEM_LIMIT_BYTES = 58 * 1024 * 1024
N_DEV = 8
N_CHIP = 4


def _dot(a, b):
    return jnp.dot(a, b, preferred_element_type=F32)


def _dot_nt(a, b):
    return lax.dot_general(a, b, (((1,), (1,)), ((), ())), preferred_element_type=F32)


def _dot_tn(a, b):
    return lax.dot_general(a, b, (((0,), (0,)), ((), ())), preferred_element_type=F32)


GELU_C0 = 0.7978845608028654
GELU_C1 = GELU_C0 * 0.044715


def _gelu(x):
    return x * (0.5 + 0.5 * jnp.tanh(x * (GELU_C0 + GELU_C1 * (x * x))))


def _gelu_and_grad(x):
    x2 = x * x
    t = jnp.tanh(x * (GELU_C0 + GELU_C1 * x2))
    half = 0.5 + 0.5 * t
    grad = half + (x * (0.5 - 0.5 * (t * t))) * (GELU_C0 + (3.0 * GELU_C1) * x2)
    return x * half, grad


def _sigmoid(x):
    return 1.0 / (1.0 + jnp.exp(-x))


def _rms(x):
    return lax.rsqrt(jnp.mean(x * x, axis=-1, keepdims=True) + EPS)


def _colsum(x):
    return jnp.sum(x, axis=0, keepdims=True)


def _rowmean(x):
    return jnp.mean(x, axis=-1, keepdims=True)


def _tiled(shape, index_map):
    return pl.BlockSpec(shape, index_map)


def _resident(shape):
    nd = len(shape)
    return pl.BlockSpec(shape, lambda *_: (0,) * nd, pipeline_mode=pl.Buffered(1))


def _whole(shape):
    nd = len(shape)
    return pl.BlockSpec(shape, lambda *_: (0,) * nd)


def _seq_params():
    return pltpu.CompilerParams(dimension_semantics=("arbitrary",), vmem_limit_bytes=VMEM_LIMIT_BYTES)


def _ff_chunks(f, width=768):
    out, o = [], 0
    while o < f:
        w = min(width, f - o)
        out.append((o, w))
        o += w
    return out


def _pool_fwd(p, halo, g, t_glob):
    ext = jnp.concatenate([halo, p], axis=0)
    s = ext
    for step in range(g + 1):
        s = s + pltpu.roll(s, 1 << step, 0)
    cnt = jnp.minimum(t_glob + 1, POOL_WINDOWS[g]).astype(F32)
    return s[POOL_HALO:] / cnt - p


def _adamw_math(w, g, m, v):
    m = ADAM_B1 * m + (1.0 - ADAM_B1) * g
    v = ADAM_B2 * v + (1.0 - ADAM_B2) * (g * g)
    m_hat = m / (1.0 - ADAM_B1 ** ADAM_STEP)
    v_hat = v / (1.0 - ADAM_B2 ** ADAM_STEP)
    delta = -ADAM_LR * (m_hat / (jnp.sqrt(v_hat) + ADAM_EPS) + ADAM_WD * w)
    return delta, m, v


def _mesh_pos():
    return lax.axis_index("x"), lax.axis_index("y"), lax.axis_index("c")


class _Rider:
    def __init__(self, inputs, out_shape, sems, start, finish):
        self.inputs, self.out_shape, self.sems = list(inputs), list(out_shape), list(sems)
        self.start, self.finish = start, finish


def _call(body, *, name, grid, in_specs, out_specs, out_shape, scratch_shapes, args, rider=None):
    params = pltpu.CompilerParams(dimension_semantics=("arbitrary",) * len(grid), vmem_limit_bytes=VMEM_LIMIT_BYTES)
    if rider is None:
        res = pl.pallas_call(body, name=name, grid=grid, in_specs=in_specs, out_specs=out_specs, out_shape=out_shape,
                             scratch_shapes=scratch_shapes, compiler_params=params)(*args)
        return tuple(res), ()
    cuts = [len(in_specs), len(rider.inputs), len(out_specs), len(rider.out_shape), len(scratch_shapes),
            len(rider.sems)]

    def hosted(*refs):
        groups, a = [], 0
        for cnt in cuts:
            groups.append(refs[a:a + cnt])
            a += cnt
        ins, r_in, outs, r_out, scr, r_sem = groups
        first = functools.reduce(jnp.logical_and, [pl.program_id(k) == 0 for k in range(len(grid))])
        last = functools.reduce(jnp.logical_and, [pl.program_id(k) == grid[k] - 1 for k in range(len(grid))])

        @pl.when(first)
        def _():
            rider.start(r_in, r_out, r_sem)

        body(*ins, *outs, *scr)

        @pl.when(last)
        def _():
            rider.finish(r_in, r_out, r_sem)

    anyspec = pl.BlockSpec(memory_space=pl.ANY)
    res = pl.pallas_call(
        hosted, name=name, grid=grid,
        in_specs=list(in_specs) + [anyspec] * cuts[1], out_specs=list(out_specs) + [anyspec] * cuts[3],
        out_shape=list(out_shape) + rider.out_shape, scratch_shapes=list(scratch_shapes) + rider.sems,
        compiler_params=params)(*args, *rider.inputs)
    return tuple(res[:cuts[2]]), tuple(res[cuts[2]:])


def _run_rider(rider, name):
    n_in, n_out = len(rider.inputs), len(rider.out_shape)

    def body(*refs):
        r_in, r_out, r_sem = refs[:n_in], refs[n_in:n_in + n_out], refs[n_in + n_out:]
        rider.start(r_in, r_out, r_sem)
        rider.finish(r_in, r_out, r_sem)

    anyspec = pl.BlockSpec(memory_space=pl.ANY)
    return pl.pallas_call(body, name=name, out_shape=rider.out_shape, in_specs=[anyspec] * n_in,
                          out_specs=[anyspec] * n_out, scratch_shapes=rider.sems)(*rider.inputs)


def _allgather_rider(arrs):
    n = len(arrs)

    def plan(ins, outs, sems):
        send_sems, recv_sems, local_sems = sems
        x, y, c = _mesh_pos()
        me, sibling = (x, y, c), (x, y, 1 - c)
        chips = [(1 - x, y), (x, 1 - y), (1 - x, 1 - y)]

        def rows(a, px, py, pc):
            r = ins[a].shape[0]
            return outs[a].at[pl.ds(pl.multiple_of((4 * px + 2 * py + pc) * r, 8), r), :]

        def copy(a, k, block, to, src=None):
            return pltpu.make_async_remote_copy(
                src_ref=rows(a, *block) if src is None else src, dst_ref=rows(a, *block),
                send_sem=send_sems.at[a * 7 + k], recv_sem=recv_sems.at[a * 7 + k],
                device_id=to, device_id_type=MESH)

        local = [pltpu.make_async_copy(ins[a], rows(a, *me), local_sems.at[a]) for a in range(n)]
        first = []
        for a in range(n):
            first.append(copy(a, 0, me, sibling, src=ins[a]))
            first += [copy(a, 1 + j, me, (*chip, c), src=ins[a]) for j, chip in enumerate(chips)]
        return c, me, sibling, chips, copy, local, first

    def start(ins, outs, sems):
        *_, local, first = plan(ins, outs, sems)
        for cp in local + first:
            cp.start()

    def finish(ins, outs, sems):
        c, me, sibling, chips, copy, local, first = plan(ins, outs, sems)
        passed = []
        for a in range(n):
            for j, chip in enumerate(chips):
                copy(a, 1 + j, (*chip, c), me).wait_recv()
                fwd = copy(a, 4 + j, (*chip, c), sibling)
                fwd.start()
                passed.append(fwd)
        for a in range(n):
            copy(a, 0, sibling, me).wait_recv()
            for j, chip in enumerate(chips):
                copy(a, 4 + j, (*chip, 1 - c), me).wait_recv()
        for cp in first + passed:
            cp.wait_send()
        for mine in local:
            mine.wait()

    return _Rider(arrs, [jax.ShapeDtypeStruct((N_DEV * a.shape[0], a.shape[1]), a.dtype) for a in arrs],
                  [pltpu.SemaphoreType.DMA((7 * n,)), pltpu.SemaphoreType.DMA((7 * n,)),
                   pltpu.SemaphoreType.DMA((n,))], start, finish)


def _piece(ref, col_sharded, k, h):
    m, n = ref.shape
    if col_sharded:
        mh, nc = m // 2, n // N_CHIP
        return ref.at[pl.ds(pl.multiple_of(h * mh, 16), mh), pl.ds(pl.multiple_of(k * nc, 128), nc)]
    rp = m // (2 * N_CHIP)
    return ref.at[pl.ds(pl.multiple_of((2 * k + h) * rp, 16), rp), :]


def _piece_shape(shape, col_sharded):
    m, n = shape
    return (m // 2, n // N_CHIP) if col_sharded else (m // (2 * N_CHIP), n)


def _cast_bf16(arrs, col_flags, k_idx):
    n = len(arrs)

    def body(k_ref, *refs):
        for a in range(n):
            val = refs[a][...].astype(BF16)
            refs[n + a][...] = val
            refs[2 * n + a][...] = val

    whole = [pl.BlockSpec(a.shape, lambda i, k_ref: (0, 0)) for a in arrs]
    window = [pl.BlockSpec(a.shape, (lambda i, k_ref: (0, k_ref[0])) if col else (lambda i, k_ref: (k_ref[0], 0)))
              for a, col in zip(arrs, col_flags)]
    res = pl.pallas_call(
        body, name="cast_weights",
        grid_spec=pltpu.PrefetchScalarGridSpec(num_scalar_prefetch=1, grid=(1,), in_specs=whole,
                                               out_specs=whole + window),
        out_shape=[jax.ShapeDtypeStruct(a.shape, BF16) for a in arrs]
        + [jax.ShapeDtypeStruct(fs, BF16) for fs in _full_shapes(arrs, col_flags)],
        compiler_params=pltpu.CompilerParams(vmem_limit_bytes=VMEM_LIMIT_BYTES))(k_idx, *arrs)
    return list(res[:n]), list(res[n:])


def _full_shapes(shards, col_flags):
    return [(s.shape[0], s.shape[1] * N_CHIP) if col else (s.shape[0] * N_CHIP, s.shape[1])
            for s, col in zip(shards, col_flags)]


def _ici_copies(shard_refs, full_refs, send_sems, recv_sems, col_flags):
    x, y, c = _mesh_pos()
    k_me = 2 * x + y
    copies = []
    for a, (s_ref, f_ref) in enumerate(zip(shard_refs, full_refs)):
        rows = s_ref.shape[0] // 2
        src = s_ref.at[pl.ds(pl.multiple_of(c * rows, 16), rows), :]
        for j, chip in enumerate([(1 - x, y), (x, 1 - y), (1 - x, 1 - y)]):
            copies.append(pltpu.make_async_remote_copy(
                src_ref=src, dst_ref=_piece(f_ref, col_flags[a], k_me, c),
                send_sem=send_sems.at[a * 3 + j], recv_sem=recv_sems.at[a * 3 + j],
                device_id=(*chip, c), device_id_type=MESH))
    return copies


def _split_begin(bufs, n_sems, make_copies, after, name):
    n = len(bufs)
    hbm = pl.BlockSpec(memory_space=pltpu.HBM)
    sem = pl.BlockSpec(memory_space=pltpu.SEMAPHORE)

    def body(*refs):
        for cp in make_copies(refs[:n], refs[n + 1], refs[n + 2]):
            cp.start()
        refs[-1][...] = jnp.zeros_like(refs[-1])

    args = [pltpu.with_memory_space_constraint(t, pltpu.HBM) for t in bufs]
    res = pl.pallas_call(
        body, name=name,
        out_shape=[pltpu.SemaphoreType.DMA((n_sems,)), pltpu.SemaphoreType.DMA((n_sems,))]
        + [pltpu.HBM(t.shape, t.dtype) for t in args] + [jax.ShapeDtypeStruct((8, HEAD), F32)],
        in_specs=[hbm] * n + [pl.BlockSpec(memory_space=pl.ANY)],
        out_specs=[sem, sem] + [hbm] * n + [pl.BlockSpec(memory_space=pltpu.VMEM)],
        input_output_aliases={i: 2 + i for i in range(n)},
        compiler_params=pltpu.CompilerParams(has_side_effects=pltpu.SideEffectType.DATAFLOW_SIDE_EFFECTING),
    )(*args, after)
    return res[0], res[1], list(res[2:2 + n]), res[-1]


def _split_end(handle, make_copies, after, name):
    send_sems, recv_sems, bufs, _ = handle
    n = len(bufs)
    hbm = pl.BlockSpec(memory_space=pltpu.HBM)
    sem = pl.BlockSpec(memory_space=pltpu.SEMAPHORE)

    def body(*refs):
        for cp in make_copies(refs[:n], refs[n], refs[n + 1]):
            cp.wait_send()
            cp.wait_recv()

    res = pl.pallas_call(
        body, name=name,
        out_shape=[pltpu.HBM(t.shape, t.dtype) for t in bufs],
        in_specs=[hbm] * n + [sem, sem, pl.BlockSpec(memory_space=pl.ANY)],
        out_specs=[hbm] * n,
        input_output_aliases={i: i for i in range(n)},
        compiler_params=pltpu.CompilerParams(has_side_effects=pltpu.SideEffectType.DATAFLOW_SIDE_EFFECTING),
    )(*bufs, send_sems, recv_sems, after)
    return list(res)


def _gather_copies(n, col_flags):
    return lambda refs, send_sems, recv_sems: _ici_copies(refs[:n], refs[n:], send_sems, recv_sems, col_flags)


def _gather_begin(shards, fulls, col_flags, after, name):
    n = len(shards)
    return _split_begin(list(shards) + list(fulls), 3 * n, _gather_copies(n, col_flags), after, name)


def _gather_end(handle, col_flags, after, name):
    n = len(handle[2]) // 2
    return _split_end(handle, _gather_copies(n, col_flags), after, name)[n:]


def _sibling_copies(col_flag):
    def make(refs, send_sems, recv_sems):
        grad_ref, land_ref = refs
        x, y, c = _mesh_pos()
        return [pltpu.make_async_remote_copy(
            src_ref=_piece(grad_ref, col_flag, k, 1 - c), dst_ref=land_ref.at[k],
            send_sem=send_sems.at[k], recv_sem=recv_sems.at[k],
            device_id=(x, y, 1 - c), device_id_type=MESH) for k in range(N_CHIP)]
    return make


def _chips_copies(n):
    def make(refs, send_sems, recv_sems):
        parts, landed = refs[:n], refs[n:]
        x, y, c = _mesh_pos()
        k_me = 2 * x + y
        copies = []
        for a in range(n):
            for j, chip in enumerate([(1 - x, y), (x, 1 - y), (1 - x, 1 - y)]):
                copies.append(pltpu.make_async_remote_copy(
                    src_ref=parts[a].at[2 * chip[0] + chip[1]], dst_ref=landed[a].at[k_me],
                    send_sem=send_sems.at[a * 3 + j], recv_sem=recv_sems.at[a * 3 + j],
                    device_id=(*chip, c), device_id_type=MESH))
        return copies
    return make


def _chips_begin(parts, after, name):
    landed = [lax.empty(p.shape, p.dtype) for p in parts]
    return _split_begin(list(parts) + landed, 3 * len(parts), _chips_copies(len(parts)), after, name)


def _chips_end(handle, after, name):
    n = len(handle[2]) // 2
    res = _split_end(handle, _chips_copies(n), after, name)
    return res[:n], res[n:]


def _small_rows(buf_ref, px, py, pc):
    r = buf_ref.shape[0] // N_DEV
    return buf_ref.at[pl.ds(pl.multiple_of((4 * px + 2 * py + pc) * r, 8), r), :]


def _small_ici_copies(refs, send_sems, recv_sems):
    x, y, c = _mesh_pos()
    copies = []
    for a, buf in enumerate(refs):
        mine = _small_rows(buf, x, y, c)
        for j, chip in enumerate([(1 - x, y), (x, 1 - y), (1 - x, 1 - y)]):
            copies.append(pltpu.make_async_remote_copy(
                src_ref=mine, dst_ref=mine, send_sem=send_sems.at[a * 3 + j], recv_sem=recv_sems.at[a * 3 + j],
                device_id=(*chip, c), device_id_type=MESH))
    return copies


def _small_finish(bufs, name):
    n = len(bufs)

    def body(*refs):
        buf_refs = refs[n:2 * n]
        send_sems, recv_sems = refs[2 * n:]
        x, y, c = _mesh_pos()
        owners = [(x, y), (1 - x, y), (x, 1 - y), (1 - x, 1 - y)]
        passed, arriving = [], []
        for a in range(n):
            for j, (px, py) in enumerate(owners):
                for pc, group in ((c, passed), (1 - c, arriving)):
                    rows = _small_rows(buf_refs[a], px, py, pc)
                    group.append(pltpu.make_async_remote_copy(
                        src_ref=rows, dst_ref=rows, send_sem=send_sems.at[a * 4 + j],
                        recv_sem=recv_sems.at[a * 4 + j], device_id=(x, y, 1 - c), device_id_type=MESH))
        for cp in passed:
            cp.start()
        for cp in arriving:
            cp.wait_recv()
        for cp in passed:
            cp.wait_send()

    anyspec = pl.BlockSpec(memory_space=pl.ANY)
    return pl.pallas_call(
        body, name=name, out_shape=[jax.ShapeDtypeStruct(b.shape, b.dtype) for b in bufs],
        in_specs=[anyspec] * n, out_specs=[anyspec] * n, input_output_aliases={a: a for a in range(n)},
        scratch_shapes=[pltpu.SemaphoreType.DMA((4 * n,)), pltpu.SemaphoreType.DMA((4 * n,))],
    )(*bufs)


def _sibling_begin(grad, col_flag, after, name):
    land = lax.empty((N_CHIP,) + _piece_shape(grad.shape, col_flag), grad.dtype)
    return _split_begin([grad, land], N_CHIP, _sibling_copies(col_flag), after, name)


def _sibling_end(handle, col_flag, after, name):
    return _split_end(handle, _sibling_copies(col_flag), after, name)


def _gather_finish(fulls, col_flags, name):
    n = len(fulls)

    def body(*refs):
        full_refs = refs[n:2 * n]
        send_sems, recv_sems = refs[2 * n:]
        x, y, c = _mesh_pos()
        passed, arriving = [], []
        for a in range(n):
            for j, chip in enumerate([(1 - x, y), (x, 1 - y), (1 - x, 1 - y)]):
                k_from = 2 * chip[0] + chip[1]
                for h, group in ((c, passed), (1 - c, arriving)):
                    win = _piece(full_refs[a], col_flags[a], k_from, h)
                    group.append(pltpu.make_async_remote_copy(
                        src_ref=win, dst_ref=win, send_sem=send_sems.at[a * 3 + j],
                        recv_sem=recv_sems.at[a * 3 + j], device_id=(x, y, 1 - c), device_id_type=MESH))
        for cp in passed:
            cp.start()
        for cp in arriving:
            cp.wait_recv()
        for cp in passed:
            cp.wait_send()

    anyspec = pl.BlockSpec(memory_space=pl.ANY)
    return pl.pallas_call(
        body, name=name,
        out_shape=[jax.ShapeDtypeStruct(f.shape, f.dtype) for f in fulls],
        in_specs=[anyspec] * n, out_specs=[anyspec] * n,
        input_output_aliases={a: a for a in range(n)},
        scratch_shapes=[pltpu.SemaphoreType.DMA((3 * n,)), pltpu.SemaphoreType.DMA((3 * n,))],
    )(*fulls)


def _sibling_rider(grads, col_flags):
    n = len(grads)
    pshapes = [_piece_shape(g.shape, col) for g, col in zip(grads, col_flags)]

    def copies(ins, outs, sems):
        send_sems, recv_sems = sems
        x, y, c = _mesh_pos()
        return [pltpu.make_async_remote_copy(
            src_ref=_piece(ins[a], col_flags[a], k, 1 - c), dst_ref=outs[a].at[k],
            send_sem=send_sems.at[a * N_CHIP + k], recv_sem=recv_sems.at[a * N_CHIP + k],
            device_id=(x, y, 1 - c), device_id_type=MESH) for a in range(n) for k in range(N_CHIP)]

    def start(ins, outs, sems):
        for cp in copies(ins, outs, sems):
            cp.start()

    def finish(ins, outs, sems):
        cps = copies(ins, outs, sems)
        for cp in cps:
            cp.wait_recv()
        for cp in cps:
            cp.wait_send()

    return _Rider(grads, [jax.ShapeDtypeStruct((N_CHIP,) + ps, g.dtype) for ps, g in zip(pshapes, grads)],
                  [pltpu.SemaphoreType.DMA((N_CHIP * n,)), pltpu.SemaphoreType.DMA((N_CHIP * n,))], start, finish)


def _sum_with_sibling(grads, landed, col_flags, c_idx, name):
    n = len(grads)
    pshapes = [_piece_shape(g.shape, col) for g, col in zip(grads, col_flags)]

    def body(c_ref, *refs):
        ins, lands, outs = refs[:n], refs[n:2 * n], refs[2 * n:]
        for a in range(n):
            outs[a][0] = (ins[a][...] + lands[a][0]).astype(BF16)

    in_specs = []
    for ps, col in zip(pshapes, col_flags):
        if col:
            in_specs.append(pl.BlockSpec(ps, lambda k, c_ref: (c_ref[0], k)))
        else:
            in_specs.append(pl.BlockSpec(ps, lambda k, c_ref: (2 * k + c_ref[0], 0)))
    land_specs = [pl.BlockSpec((1,) + ps, lambda k, c_ref: (k, 0, 0)) for ps in pshapes]
    return pl.pallas_call(
        body, name=name,
        grid_spec=pltpu.PrefetchScalarGridSpec(
            num_scalar_prefetch=1, grid=(N_CHIP,),
            in_specs=in_specs + land_specs, out_specs=land_specs),
        out_shape=[jax.ShapeDtypeStruct((N_CHIP,) + ps, BF16) for ps in pshapes],
        compiler_params=pltpu.CompilerParams(dimension_semantics=("arbitrary",), vmem_limit_bytes=VMEM_LIMIT_BYTES),
    )(c_idx, *grads, *landed)


def _chips_rider(parts):
    n = len(parts)

    def plan(ins, outs, sems, arriving):
        send_sems, recv_sems = sems
        x, y, c = _mesh_pos()
        k_me = 2 * x + y
        copies = []
        for a in range(n):
            for j, chip in enumerate([(1 - x, y), (x, 1 - y), (1 - x, 1 - y)]):
                k_peer = 2 * chip[0] + chip[1]
                copies.append(pltpu.make_async_remote_copy(
                    src_ref=ins[a].at[k_peer], dst_ref=outs[a].at[k_peer if arriving else k_me],
                    send_sem=send_sems.at[a * 3 + j], recv_sem=recv_sems.at[a * 3 + j],
                    device_id=(*chip, c), device_id_type=MESH))
        return copies

    def start(ins, outs, sems):
        for cp in plan(ins, outs, sems, False):
            cp.start()

    def finish(ins, outs, sems):
        arrivals = plan(ins, outs, sems, True)
        for cp in arrivals:
            cp.wait_recv()
        for cp in arrivals:
            cp.wait_send()

    return _Rider(parts, [jax.ShapeDtypeStruct(p.shape, p.dtype) for p in parts],
                  [pltpu.SemaphoreType.DMA((3 * n,)), pltpu.SemaphoreType.DMA((3 * n,))], start, finish)


def _sum_chips_and_share(landed, parts, after, name):
    n = len(landed)

    def body(*refs):
        ins, own, outs, red = refs[:n], refs[n:2 * n], refs[2 * n + 1:3 * n + 1], refs[3 * n + 1:4 * n + 1]
        send_sems, recv_sems, local_sems = refs[4 * n + 1:]
        x, y, c = _mesh_pos()
        sibling = (x, y, 1 - c)
        k_me = 2 * x + y
        copies, local = [], []
        for a in range(n):
            for k in range(N_CHIP):
                @pl.when(k_me == k)
                def _():
                    term = own[a][k].astype(F32)
                    red[a][...] = term if k == 0 else red[a][...] + term

                @pl.when(k_me != k)
                def _():
                    term = ins[a][k].astype(F32)
                    red[a][...] = term if k == 0 else red[a][...] + term

            mine = pltpu.make_async_copy(red[a], outs[a].at[c], local_sems.at[a])
            mine.start()
            local.append(mine)
            cp = pltpu.make_async_remote_copy(
                src_ref=red[a], dst_ref=outs[a].at[c],
                send_sem=send_sems.at[a], recv_sem=recv_sems.at[a],
                device_id=sibling, device_id_type=MESH)
            cp.start()
            copies.append(cp)
        for a in range(n):
            pltpu.make_async_remote_copy(
                src_ref=red[a], dst_ref=outs[a].at[1 - c],
                send_sem=send_sems.at[a], recv_sem=recv_sems.at[a],
                device_id=sibling, device_id_type=MESH).wait_recv()
        for cp in copies:
            cp.wait_send()
        for mine in local:
            mine.wait()

    return pl.pallas_call(
        body, name=name,
        out_shape=[jax.ShapeDtypeStruct((2,) + l.shape[1:], F32) for l in landed],
        in_specs=[pl.BlockSpec(memory_space=pltpu.VMEM)] * (2 * n + 1),
        out_specs=[pl.BlockSpec(memory_space=pl.ANY)] * n,
        scratch_shapes=[pltpu.VMEM(l.shape[1:], F32) for l in landed]
        + [pltpu.SemaphoreType.DMA((n,)), pltpu.SemaphoreType.DMA((n,)), pltpu.SemaphoreType.DMA((n,))],
        compiler_params=pltpu.CompilerParams(vmem_limit_bytes=VMEM_LIMIT_BYTES),
    )(*landed, *parts, after)


def _mod_shard(c_all, w_ada, b_shard):
    def body(c_ref, w_ref, b_ref, o_ref, sc_ref):
        cc = c_ref[...]
        sc = cc * _sigmoid(cc)
        sc_ref[...] = sc
        o_ref[...] = _dot(sc, w_ref[...]) + b_ref[...]

    nb, d = c_all.shape
    nn = w_ada.shape[1]
    return pl.pallas_call(
        body, name="mod_shard",
        out_shape=[jax.ShapeDtypeStruct((nb, nn), F32), jax.ShapeDtypeStruct((nb, d), F32)],
        compiler_params=pltpu.CompilerParams(vmem_limit_bytes=VMEM_LIMIT_BYTES),
    )(c_all, w_ada, b_shard)


V_SH_M, V_SC_M, V_G_M, V_SH_F, V_SC_F, V_G_F, V_PRE_MIX, V_POST_MIX, V_PRE_FFN, V_POST_FFN = range(10)


def _vrow(vec_ref, r):
    return vec_ref[r:r + 1, :]


def _mix_fwd(x, vecs, w_in_b, w_out_b, sgu_g, wm_b, bsb, wp_b, ps, pmats, ts):
    s_len, d = x.shape
    nt = s_len // ts
    n_proj = w_in_b.shape[1]

    def body(x_ref, vec_ref, win_ref, wout_ref, sg_ref, wm_ref, bs_ref, wp_ref, ps_ref, pm_ref,
             h1_ref, proj_ref, cat_ref, mixed_ref, x2_ref, h2_ref, carry_ref):
        i = pl.program_id(0)

        @pl.when(i == 0)
        def _():
            carry_ref[...] = jnp.zeros_like(carry_ref)

        sub = FFN_TS
        nsub = ts // sub

        def project(s):
            rs = slice(s * sub, (s + 1) * sub)
            x = x_ref[rs, :]
            h1 = (((x * _rms(x)) * _vrow(vec_ref, V_PRE_MIX)) * (1.0 + _vrow(vec_ref, V_SC_M))
                  + _vrow(vec_ref, V_SH_M)).astype(BF16)
            h1_ref[rs, :] = h1
            proj = _dot(h1, win_ref[...])
            proj_ref[rs, :] = proj.astype(BF16)
            return proj

        def mix(s, proj, halo):
            r0 = s * sub
            t_glob = lax.broadcasted_iota(jnp.int32, (sub, HEAD), 0) + (i * ts + r0)
            for h in range(N_HEADS):
                u = _gelu(proj[:, h * HEAD:(h + 1) * HEAD])
                v = _gelu(proj[:, A_WIDTH + h * HEAD:A_WIDTH + (h + 1) * HEAD])
                vn = ((v * _rms(v)) * sg_ref[h:h + 1, :]).astype(BF16)
                for b in range(sub // HEAD):
                    rs = slice(b * HEAD, (b + 1) * HEAD)
                    z = _dot(wm_ref[h], vn[rs]) + bs_ref[h]
                    cat_ref[r0 + b * HEAD:r0 + (b + 1) * HEAD, h * HEAD:(h + 1) * HEAD] = (u[rs] * z).astype(BF16)
            for g in range(len(POOL_WINDOWS)):
                gs = slice(g * HEAD, (g + 1) * HEAD)
                p = proj[:, 2 * A_WIDTH + g * HEAD:2 * A_WIDTH + (g + 1) * HEAD]
                pooled = _pool_fwd(p, halo[:, gs], g, t_glob)
                yb = _dot(pooled.astype(BF16), wp_ref[g]) * ps_ref[0:1, gs]
                cat_ref[r0:r0 + sub, A_WIDTH + g * HEAD:A_WIDTH + (g + 1) * HEAD] = yb.astype(BF16)

        def finish(s, mixed):
            rs = slice(s * sub, (s + 1) * sub)
            mixed_ref[rs, :] = mixed
            x2 = x_ref[rs, :] + _vrow(vec_ref, V_G_M) * ((mixed * _rms(mixed)) * _vrow(vec_ref, V_POST_MIX))
            x2_ref[rs, :] = x2
            h2 = (((x2 * _rms(x2)) * _vrow(vec_ref, V_PRE_FFN)) * (1.0 + _vrow(vec_ref, V_SC_F))
                  + _vrow(vec_ref, V_SH_F)).astype(BF16)
            h2_ref[rs, :] = _permute_bf16(pm_ref[0], h2)

        projs = [project(0)]
        halo = carry_ref[...]
        mixed_prev = None
        for s in range(nsub):
            if s + 1 < nsub:
                projs.append(project(s + 1))
            mix(s, projs[s], halo)
            halo = projs[s][sub - POOL_HALO:sub, 2 * A_WIDTH:]
            mixed = _dot(cat_ref[s * sub:(s + 1) * sub, :], wout_ref[...])
            if mixed_prev is not None:
                finish(s - 1, mixed_prev)
            mixed_prev = mixed
        carry_ref[...] = halo
        finish(nsub - 1, mixed_prev)

    row = lambda i: (i, 0)
    return _call(
        body, name="mix_fwd", grid=(nt,),
        in_specs=[_tiled((ts, d), row), _whole(vecs.shape), _resident(w_in_b.shape), _resident(w_out_b.shape),
                  _whole(sgu_g.shape), _whole(wm_b.shape), _whole(bsb.shape), _whole(wp_b.shape), _whole(ps.shape),
                  _whole(pmats.shape)],
        out_specs=[_tiled((ts, d), row), _tiled((ts, n_proj), row), _tiled((ts, d), row),
                   _tiled((ts, d), row), _tiled((ts, d), row), _tiled((ts, d), row)],
        out_shape=[jax.ShapeDtypeStruct((s_len, d), BF16), jax.ShapeDtypeStruct((s_len, n_proj), BF16),
                   jax.ShapeDtypeStruct((s_len, d), BF16), jax.ShapeDtypeStruct((s_len, d), F32),
                   jax.ShapeDtypeStruct((s_len, d), F32), jax.ShapeDtypeStruct((s_len, d), BF16)],
        scratch_shapes=[pltpu.VMEM((POOL_HALO, A_WIDTH), F32)],
        args=(x, vecs, w_in_b, w_out_b, sgu_g, wm_b, bsb, wp_b, ps, pmats))


def _perm_mats(ts):
    p = jnp.arange(ts)
    pm = (((p % 8) * (ts // 8) + p // 8)[:, None] == p[None, :]).astype(BF16)
    return jnp.stack([pm, pm.T])


def _permute_bf16(pm, xb):
    return _dot(pm, xb).astype(BF16)


def _permute_f32(pm, x):
    hi = x.astype(BF16)
    lo = (x - hi.astype(F32)).astype(BF16)
    return _dot(pm, hi) + _dot(pm, lo)


def _conv_out(u, um2, um1, cv_ref, cols):
    return (cv_ref[3:4, cols] + um2 * cv_ref[0:1, cols] + um1 * cv_ref[1:2, cols] + u * cv_ref[2:3, cols])


F_LOSS, F_DGF, F_DPOSTFFN = 0, 1, 2
B_DSHF, B_DSCF, B_DPREFFN, B_DGM, B_DPOSTMIX = 0, 1, 2, 3, 4
M_DSHM, M_DSCM, M_DPREMIX = 0, 1, 2
C_DCB, C_DCW = 0, 1


def _ffn_fwd(h2p, x2, tgt, w_up_b, w_down_b, cvec, vecs, pmats, ts):
    s_len, d = x2.shape
    ff2 = w_up_b.shape[1]
    ff = ff2 // 2
    nt = s_len // ts
    chunks = _ff_chunks(ff)

    def body(h2_ref, x2_ref, t_ref, wu_ref, wd_ref, cv_ref, vec_ref, pm_ref,
             up_ref, y_ref, act_ref, dy_ref, df_ref, acc_ref, carry_ref):
        @pl.when(pl.program_id(0) == 0)
        def _():
            carry_ref[...] = jnp.zeros_like(carry_ref)
            acc_ref[...] = jnp.zeros_like(acc_ref)

        h2v = h2_ref[...]

        def up_dots(o, w):
            return [_dot(h2v, wu_ref[:, base + o:base + o + w]) for base in (0, ff)]

        f = None
        pending = None
        nxt = up_dots(*chunks[0])
        for ci, (o, w) in enumerate(chunks):
            us = nxt
            if ci + 1 < len(chunks):
                nxt = up_dots(*chunks[ci + 1])
            if pending is not None:
                part = _dot(pending[0], wd_ref[pending[1]:pending[1] + pending[2], :])
                f = part if f is None else f + part
            sub0 = lax.broadcasted_iota(jnp.int32, (8, w), 0) == 0
            ys = []
            for base, u in zip((0, ff), us):
                cols = slice(base + o, base + o + w)
                up_ref[:, cols] = u.astype(BF16)
                b1 = jnp.where(sub0, pltpu.roll(carry_ref[8:16, cols], 1, 0), pltpu.roll(u[ts - 8:ts], 1, 0))
                b2 = jnp.where(sub0, pltpu.roll(carry_ref[0:8, cols], 1, 0), pltpu.roll(u[ts - 16:ts - 8], 1, 0))
                um1 = jnp.concatenate([b1, u[:ts - 8]], axis=0)
                um2 = jnp.concatenate([b2, b1, u[:ts - 16]], axis=0)
                ys.append(_conv_out(u, um2, um1, cv_ref, cols))
                carry_ref[:, cols] = u[ts - 16:ts]
            gate, val = ys
            sg = _sigmoid(gate)
            gs = gate * sg
            act = (gs * val).astype(BF16)
            act_ref[:, o:o + w] = act
            y_ref[:, o:o + w] = (val * (sg + gs * (1.0 - sg))).astype(BF16)
            y_ref[:, ff + o:ff + o + w] = gs.astype(BF16)
            pending = (act, o, w)
        f = f + _dot(pending[0], wd_ref[pending[1]:pending[1] + pending[2], :])
        f = _permute_f32(pm_ref[1], f)
        r3 = _rms(f)
        fhat = f * r3
        post = _vrow(vec_ref, V_POST_FFN)
        g_f = _vrow(vec_ref, V_G_F)
        fn = fhat * post
        e = (x2_ref[...] + g_f * fn) - t_ref[...]
        dy = e * (1.0 / d)
        dy_ref[...] = dy
        dfn = dy * g_f
        acc_ref[F_LOSS:F_LOSS + 1, :] += _colsum(e * e)
        acc_ref[F_DGF:F_DGF + 1, :] += _colsum(dy * fn)
        acc_ref[F_DPOSTFFN:F_DPOSTFFN + 1, :] += _colsum(dfn * fhat)
        dfhat = dfn * post
        df = (r3 * (dfhat - fhat * _rowmean(dfhat * fhat))).astype(BF16)
        df_ref[...] = _permute_bf16(pm_ref[0], df)

    row = lambda i: (i, 0)
    return pl.pallas_call(
        body, name="ffn_fwd", grid=(nt,),
        in_specs=[_tiled((ts, d), row), _tiled((ts, d), row), _tiled((ts, d), row), _resident(w_up_b.shape),
                  _resident(w_down_b.shape), _whole(cvec.shape), _whole(vecs.shape), _whole(pmats.shape)],
        out_specs=[_tiled((ts, ff2), row), _tiled((ts, ff2), row), _tiled((ts, ff), row), _tiled((ts, d), row),
                   _tiled((ts, d), row), _whole((8, d))],
        out_shape=[jax.ShapeDtypeStruct((s_len, ff2), BF16), jax.ShapeDtypeStruct((s_len, ff2), BF16),
                   jax.ShapeDtypeStruct((s_len, ff), BF16), jax.ShapeDtypeStruct((s_len, d), F32),
                   jax.ShapeDtypeStruct((s_len, d), BF16), jax.ShapeDtypeStruct((8, d), F32)],
        scratch_shapes=[pltpu.VMEM((16, ff2), F32)],
        compiler_params=_seq_params(),
    )(h2p, x2, tgt, w_up_b, w_down_b, cvec, vecs, pmats)


def _ffn_bwd(dfp, upp, yp, x2, dy, mixed, w_up_b, w_down_b, cvec, vecs, pmats, ts, rider=None):
    s_len, d = x2.shape
    ff2 = w_up_b.shape[1]
    ff = ff2 // 2
    nt = s_len // ts
    chunks = _ff_chunks(ff, 512)

    def body(df_ref, up_ref, y_ref, x2_ref, dy_ref, mx_ref, wu_ref, wd_ref, cv_ref, vec_ref, pm_ref,
             dup_ref, dx2_ref, dmx_ref, accc_ref, acc_ref, carry_ref):
        @pl.when(pl.program_id(0) == 0)
        def _():
            carry_ref[...] = jnp.zeros_like(carry_ref)
            accc_ref[...] = jnp.zeros_like(accc_ref)
            acc_ref[...] = jnp.zeros_like(acc_ref)

        dfv = df_ref[...]

        def dh2_add(acc, dups, o, w):
            for base, dup in zip((0, ff), dups):
                part = _dot_nt(dup, wu_ref[:, base + o:base + o + w])
                acc = part if acc is None else acc + part
            return acc

        dh2 = None
        pending = None
        nxt = _dot_nt(dfv, wd_ref[chunks[0][0]:chunks[0][0] + chunks[0][1], :])
        for ci, (o, w) in enumerate(chunks):
            dact = nxt
            if ci + 1 < len(chunks):
                o2, w2 = chunks[ci + 1]
                nxt = _dot_nt(dfv, wd_ref[o2:o2 + w2, :])
            if pending is not None:
                dh2 = dh2_add(dh2, *pending)
            sub7 = lax.broadcasted_iota(jnp.int32, (8, w), 0) == 7
            dups = []
            dys = (dact * y_ref[:, o:o + w].astype(F32), dact * y_ref[:, ff + o:ff + o + w].astype(F32))
            for base, dyv in zip((0, ff), dys):
                cols = slice(base + o, base + o + w)
                u = up_ref[:, cols].astype(F32)
                e0 = jnp.where(sub7, pltpu.roll(carry_ref[0:8, cols], 7, 0), pltpu.roll(dyv[0:8], 7, 0))
                e1 = jnp.where(sub7, pltpu.roll(carry_ref[8:16, cols], 7, 0), pltpu.roll(dyv[8:16], 7, 0))
                dyp1 = jnp.concatenate([dyv[8:], e0], axis=0)
                dyp2 = jnp.concatenate([dyv[16:], e0, e1], axis=0)
                accc_ref[C_DCB:C_DCB + 1, cols] += _colsum(dyv)
                accc_ref[C_DCW + 0:C_DCW + 1, cols] += _colsum(dyp2 * u)
                accc_ref[C_DCW + 1:C_DCW + 2, cols] += _colsum(dyp1 * u)
                accc_ref[C_DCW + 2:C_DCW + 3, cols] += _colsum(dyv * u)
                dup = (dyv * cv_ref[2:3, cols] + dyp1 * cv_ref[1:2, cols] + dyp2 * cv_ref[0:1, cols]).astype(BF16)
                dup_ref[:, cols] = dup
                dups.append(dup)
                carry_ref[:, cols] = dyv[0:16]
            pending = (dups, o, w)
        dh2 = dh2_add(dh2, *pending)
        dh2 = _permute_f32(pm_ref[1], dh2)
        x2 = x2_ref[...]
        r2 = _rms(x2)
        xn = x2 * r2
        pre = _vrow(vec_ref, V_PRE_FFN)
        one_sc = 1.0 + _vrow(vec_ref, V_SC_F)
        acc_ref[B_DSHF:B_DSHF + 1, :] += _colsum(dh2)
        acc_ref[B_DSCF:B_DSCF + 1, :] += _colsum(dh2 * (xn * pre))
        acc_ref[B_DPREFFN:B_DPREFFN + 1, :] += _colsum(dh2 * xn * one_sc)
        dxn = dh2 * pre * one_sc
        dx2 = dy_ref[...] + r2 * (dxn - xn * _rowmean(dxn * xn))
        dx2_ref[...] = dx2
        mixed = mx_ref[...]
        rm = _rms(mixed)
        mhat = mixed * rm
        post = _vrow(vec_ref, V_POST_MIX)
        acc_ref[B_DGM:B_DGM + 1, :] += _colsum(dx2 * (mhat * post))
        dmn = dx2 * _vrow(vec_ref, V_G_M)
        acc_ref[B_DPOSTMIX:B_DPOSTMIX + 1, :] += _colsum(dmn * mhat)
        dmhat = dmn * post
        dmx_ref[...] = (rm * (dmhat - mhat * _rowmean(dmhat * mhat))).astype(BF16)

    rev = lambda i: (nt - 1 - i, 0)
    return _call(
        body, name="ffn_bwd", grid=(nt,), rider=rider,
        in_specs=[_tiled((ts, d), rev), _tiled((ts, ff2), rev), _tiled((ts, ff2), rev), _tiled((ts, d), rev),
                  _tiled((ts, d), rev), _tiled((ts, d), rev), _resident(w_up_b.shape), _resident(w_down_b.shape),
                  _whole(cvec.shape), _whole(vecs.shape), _whole(pmats.shape)],
        out_specs=[_tiled((ts, ff2), rev), _tiled((ts, d), rev), _tiled((ts, d), rev), _whole((8, ff2)),
                   _whole((8, d))],
        out_shape=[jax.ShapeDtypeStruct((s_len, ff2), BF16), jax.ShapeDtypeStruct((s_len, d), F32),
                   jax.ShapeDtypeStruct((s_len, d), BF16), jax.ShapeDtypeStruct((8, ff2), F32),
                   jax.ShapeDtypeStruct((8, d), F32)],
        scratch_shapes=[pltpu.VMEM((16, ff2), F32)],
        args=(dfp, upp, yp, x2, dy, mixed, w_up_b, w_down_b, cvec, vecs, pmats))


def _mix_bwd(dmixed, dx2, x, proj, vecs, w_in_b, w_out_b, sgu_g, wm_b, wmt_b, bsb, wp_b, ps, ts):
    s_len, d = x.shape
    nt = s_len // ts
    n_proj = proj.shape[1]
    per = ts // POOL_HALO

    def body(dmx_ref, dx2_ref, x_ref, proj_ref, projh_ref, vec_ref, win_ref, wout_ref, sg_ref, wm_ref, wmt_ref,
             bs_ref, wp_ref, ps_ref,
             gx_ref, dproj_ref, acc_ref, dwm_out, dwp_out, db_ref, dg_ref, dps_ref,
             carry_ref, dwm_ref, dwp_ref, dbz_ref):
        i = pl.program_id(0)
        tile = nt - 1 - i

        @pl.when(i == 0)
        def _():
            carry_ref[...] = jnp.zeros_like(carry_ref)
            for r in (acc_ref, dwm_ref, dwp_ref, dbz_ref, dg_ref, dps_ref):
                r[...] = jnp.zeros_like(r)

        sub = FFN_TS
        nsub = ts // sub
        ext = sub + POOL_HALO

        def cotangent(s):
            return _dot_nt(dmx_ref[s * sub:(s + 1) * sub, :], wout_ref[...])

        def back(s, dcat, halo, later):
            r0 = s * sub
            rows = slice(r0, r0 + sub)
            t_glob = lax.broadcasted_iota(jnp.int32, (sub, HEAD), 0) + (tile * ts + r0)
            for h in range(N_HEADS):
                hs = slice(h * HEAD, (h + 1) * HEAD)
                vs = slice(A_WIDTH + h * HEAD, A_WIDTH + (h + 1) * HEAD)
                gain = sg_ref[h:h + 1, :]
                for b in range(sub // HEAD):
                    blk = slice(r0 + b * HEAD, r0 + (b + 1) * HEAD)
                    au = proj_ref[blk, hs].astype(F32)
                    av = proj_ref[blk, vs].astype(F32)
                    u, u_grad = _gelu_and_grad(au)
                    v, v_grad = _gelu_and_grad(av)
                    rv = _rms(v)
                    vhat = v * rv
                    vn = (vhat * gain).astype(BF16)
                    dout = dcat[b * HEAD:(b + 1) * HEAD, hs]
                    z = _dot(wm_ref[h], vn) + bs_ref[h]
                    dz = dout * u
                    dbz_ref[h] += dz
                    dzb = dz.astype(BF16)
                    dwm_ref[h] += _dot_nt(dzb, vn)
                    dvn = _dot(wmt_ref[h], dzb)
                    dg_ref[h:h + 1, :] += _colsum(dvn * vhat)
                    dvhat = dvn * gain
                    dv = rv * (dvhat - vhat * _rowmean(dvhat * vhat))
                    dproj_ref[blk, hs] = ((dout * z) * u_grad).astype(BF16)
                    dproj_ref[blk, vs] = (dv * v_grad).astype(BF16)
            firsts = []
            for g in range(len(POOL_WINDOWS)):
                gs = slice(g * HEAD, (g + 1) * HEAD)
                pcols = slice(2 * A_WIDTH + g * HEAD, 2 * A_WIDTH + (g + 1) * HEAD)
                p = proj_ref[rows, pcols].astype(F32)
                pb = _pool_fwd(p, halo[:, gs], g, t_glob).astype(BF16)
                dyb = dcat[:, A_WIDTH + g * HEAD:A_WIDTH + (g + 1) * HEAD]
                dps_ref[0:1, gs] += _colsum(dyb * _dot(pb, wp_ref[g]))
                dyl = (dyb * ps_ref[0:1, gs]).astype(BF16)
                dwp_ref[g] += _dot_tn(pb, dyl)
                dpooled = _dot_nt(dyl, wp_ref[g])
                cnt = jnp.minimum(t_glob + 1, POOL_WINDOWS[g]).astype(F32)
                q = dpooled / cnt
                acc = jnp.concatenate([q, later[:, gs]], axis=0)
                for step in range(g + 1):
                    acc = acc + pltpu.roll(acc, ext - (1 << step), 0)
                dproj_ref[rows, pcols] = (acc[:sub] - dpooled).astype(BF16)
                firsts.append(q[0:POOL_HALO])
            return jnp.concatenate(firsts, axis=1)

        def finish(s, dh1):
            rows = slice(s * sub, (s + 1) * sub)
            x = x_ref[rows, :]
            r1 = _rms(x)
            xn = x * r1
            pre = _vrow(vec_ref, V_PRE_MIX)
            one_sc = 1.0 + _vrow(vec_ref, V_SC_M)
            acc_ref[M_DSHM:M_DSHM + 1, :] += _colsum(dh1)
            acc_ref[M_DSCM:M_DSCM + 1, :] += _colsum(dh1 * (xn * pre))
            acc_ref[M_DPREMIX:M_DPREMIX + 1, :] += _colsum(dh1 * xn * one_sc)
            dxn = dh1 * pre * one_sc
            gx_ref[rows, :] = dx2_ref[rows, :] + r1 * (dxn - xn * _rowmean(dxn * xn))

        nxt = cotangent(nsub - 1)
        later = carry_ref[...]
        dh1_prev = None
        for s in reversed(range(nsub)):
            dcat = nxt
            if s > 0:
                nxt = cotangent(s - 1)
                halo = proj_ref[s * sub - POOL_HALO:s * sub, 2 * A_WIDTH:].astype(F32)
            else:
                halo = jnp.where(tile > 0, projh_ref[:, 2 * A_WIDTH:].astype(F32), 0.0)
            later = back(s, dcat, halo, later)
            dh1 = _dot_nt(dproj_ref[s * sub:(s + 1) * sub, :], win_ref[...])
            if dh1_prev is not None:
                finish(s + 1, dh1_prev)
            dh1_prev = dh1
        carry_ref[...] = later
        finish(0, dh1_prev)

        @pl.when(i == nt - 1)
        def _():
            dwm_out[...] = dwm_ref[...].astype(BF16)
            dwp_out[...] = dwp_ref[...].astype(BF16)
            db_ref[...] = jnp.zeros_like(db_ref)
            for h in range(N_HEADS):
                db_ref[h:h + 1, :] = jnp.sum(dbz_ref[h].T, axis=0, keepdims=True)

    rev = lambda i: (nt - 1 - i, 0)
    halo_map = lambda i: (jnp.maximum((nt - 1 - i) * per - 1, 0), 0)
    hshape = (N_HEADS, HEAD, HEAD)
    return _call(
        body, name="mix_bwd", grid=(nt,),
        in_specs=[_tiled((ts, d), rev), _tiled((ts, d), rev), _tiled((ts, d), rev), _tiled((ts, n_proj), rev),
                  _tiled((POOL_HALO, n_proj), halo_map), _whole(vecs.shape), _resident(w_in_b.shape),
                  _resident(w_out_b.shape), _whole(sgu_g.shape), _whole(wm_b.shape), _whole(wmt_b.shape),
                  _whole(bsb.shape), _whole(wp_b.shape), _whole(ps.shape)],
        out_specs=[_tiled((ts, d), rev), _tiled((ts, n_proj), rev), _whole((8, d)), _whole(hshape), _whole(hshape),
                   _whole((8, HEAD)), _whole((8, HEAD)), _whole((8, A_WIDTH))],
        out_shape=[jax.ShapeDtypeStruct((s_len, d), F32), jax.ShapeDtypeStruct((s_len, n_proj), BF16),
                   jax.ShapeDtypeStruct((8, d), F32), jax.ShapeDtypeStruct(hshape, BF16),
                   jax.ShapeDtypeStruct(hshape, BF16), jax.ShapeDtypeStruct((8, HEAD), F32),
                   jax.ShapeDtypeStruct((8, HEAD), F32), jax.ShapeDtypeStruct((8, A_WIDTH), F32)],
        scratch_shapes=[pltpu.VMEM((POOL_HALO, A_WIDTH), F32), pltpu.VMEM(hshape, F32), pltpu.VMEM(hshape, F32),
                        pltpu.VMEM(hshape, F32)],
        args=(dmixed, dx2, x, proj, proj, vecs, w_in_b, w_out_b, sgu_g, wm_b, wmt_b, bsb, wp_b, ps))


def _wgrad(a, b, tn, ts, name, rider=None):
    s_len, m = a.shape
    n = b.shape[1]
    ts = min(ts, s_len)

    def body(a_ref, b_ref, o_ref):
        @pl.when(pl.program_id(1) == 0)
        def _():
            o_ref[...] = jnp.zeros_like(o_ref)

        o_ref[...] += _dot_tn(a_ref[...], b_ref[...])

    (g,), r_out = _call(
        body, name=name, grid=(n // tn, s_len // ts), rider=rider,
        in_specs=[pl.BlockSpec((ts, m), lambda j, s: (s, 0)), pl.BlockSpec((ts, tn), lambda j, s: (s, j))],
        out_specs=[pl.BlockSpec((m, tn), lambda j, s: (0, j))],
        out_shape=[jax.ShapeDtypeStruct((m, n), F32)], scratch_shapes=[], args=(a, b))
    return g, r_out


def _wgrad_pair(a1, b1, a2, b2, after, ts, name):
    s_len = a1.shape[0]
    ts = min(ts, s_len)
    shapes = [(a1.shape[1], b1.shape[1]), (a2.shape[1], b2.shape[1])]

    def body(a1_ref, b1_ref, a2_ref, b2_ref, after_ref, o1_ref, o2_ref):
        @pl.when(pl.program_id(0) == 0)
        def _():
            o1_ref[...] = jnp.zeros_like(o1_ref)
            o2_ref[...] = jnp.zeros_like(o2_ref)

        o1_ref[...] += _dot_tn(a1_ref[...], b1_ref[...])
        o2_ref[...] += _dot_tn(a2_ref[...], b2_ref[...])

    row = lambda s: (s, 0)
    return _call(
        body, name=name, grid=(s_len // ts,),
        in_specs=[pl.BlockSpec((ts, t.shape[1]), row) for t in (a1, b1, a2, b2)] + [_whole(after.shape)],
        out_specs=[_whole(sh) for sh in shapes],
        out_shape=[jax.ShapeDtypeStruct(sh, F32) for sh in shapes], scratch_shapes=[],
        args=(a1, b1, a2, b2, after))[0]


def _adamw_big(g, w, m, v, name):
    r, cdim = g.shape
    tr = r
    while tr * cdim * 4 > (3 << 19) and tr % 16 == 0:
        tr //= 2

    def body(g_ref, w_ref, m_ref, v_ref, d_ref, nm_ref, nv_ref):
        delta, m2, v2 = _adamw_math(w_ref[0], g_ref[...], m_ref[0], v_ref[0])
        d_ref[0] = delta
        nm_ref[0] = m2
        nv_ref[0] = v2

    s3 = pl.BlockSpec((1, tr, cdim), lambda i: (0, i, 0))
    return pl.pallas_call(
        body, name=name, grid=(r // tr,),
        in_specs=[pl.BlockSpec((tr, cdim), lambda i: (i, 0)), s3, s3, s3],
        out_specs=[s3, s3, s3],
        out_shape=[jax.ShapeDtypeStruct(w.shape, F32)] * 3,
        compiler_params=pltpu.CompilerParams(dimension_semantics=("parallel",), vmem_limit_bytes=VMEM_LIMIT_BYTES),
    )(g, w, m, v)


def _wada_update(sct, gm, w, m, v):
    _, r, cdim = w.shape
    tr = 256
    kp = sct.shape[1]

    def body(s_ref, g_ref, w_ref, m_ref, v_ref, gw_ref, d_ref, nm_ref, nv_ref):
        g = _dot(s_ref[...], g_ref[...])
        gw_ref[0] = g
        delta, m2, v2 = _adamw_math(w_ref[0], g, m_ref[0], v_ref[0])
        d_ref[0] = delta
        nm_ref[0] = m2
        nv_ref[0] = v2

    s3 = pl.BlockSpec((1, tr, cdim), lambda i: (0, i, 0))
    return pl.pallas_call(
        body, name="wada_update", grid=(r // tr,),
        in_specs=[pl.BlockSpec((tr, kp), lambda i: (i, 0)), _whole(gm.shape), s3, s3, s3],
        out_specs=[s3, s3, s3, s3],
        out_shape=[jax.ShapeDtypeStruct(w.shape, F32)] * 4,
        compiler_params=pltpu.CompilerParams(dimension_semantics=("parallel",), vmem_limit_bytes=VMEM_LIMIT_BYTES),
    )(sct, gm, w, m, v)


def _small_update(g1, g2, g2s, gwm, gwp, gbz, gsg, gps, params):
    names = ["b_ada", "pre_mix_g", "post_mix_g", "sgu_norm_g", "w_spatial", "b_spatial", "w_pool", "pool_scale",
             "pre_ffn_g", "post_ffn_g", "conv_w", "conv_b"]
    d = g1.shape[2]
    flat_in = [g1, g2, g2s, gwm, gwp, gbz, gsg, gps]
    n_g = len(flat_in)
    for nm in names:
        flat_in += list(params[nm])

    def body(*refs):
        g1_ref, g2_ref, g2s_ref, gwm_ref, gwp_ref, gbz_ref, gsg_ref, gps_ref = refs[:n_g]
        wmv = refs[n_g:n_g + 3 * len(names)]
        loss_ref = refs[n_g + 3 * len(names)]
        outs = refs[n_g + 3 * len(names) + 1:]

        def dsum(ref, idx):
            acc = ref[(0,) + idx].astype(F32)
            for dev in range(1, N_DEV):
                acc = acc + ref[(dev,) + idx].astype(F32)
            return acc

        def apply(pi, g, widx, oidx):
            w_ref, m_ref, v_ref = wmv[3 * pi:3 * pi + 3]
            g_ref, d_ref, nm_ref, nv_ref = outs[4 * pi:4 * pi + 4]
            delta, m2, v2 = _adamw_math(w_ref[widx], g, m_ref[widx], v_ref[widx])
            g_ref[oidx] = g
            d_ref[oidx] = delta
            nm_ref[oidx] = m2
            nv_ref[oidx] = v2

        def row1(base, r):
            return (slice(base + r, base + r + 1), slice(None))

        tot = dsum(g1_ref, row1(0, F_LOSS))
        loss_ref[...] = jnp.zeros(loss_ref.shape, F32) + jnp.sum(tot) * (0.5 / d)
        mod_rows = [row1(16, M_DSHM), row1(16, M_DSCM), row1(8, B_DGM), row1(8, B_DSHF), row1(8, B_DSCF),
                    row1(0, F_DGF)]
        for j, rr in enumerate(mod_rows):
            cs = (slice(None), slice(j * d, (j + 1) * d))
            apply(0, dsum(g1_ref, rr), cs, cs)
        full = (slice(None), slice(None))
        apply(1, dsum(g1_ref, row1(16, M_DPREMIX)), full, full)
        apply(2, dsum(g1_ref, row1(8, B_DPOSTMIX)), full, full)
        apply(3, dsum(gsg_ref, (slice(0, N_HEADS), slice(None))), (0,), (0,))
        pos_i = lax.broadcasted_iota(jnp.int32, (HEAD, HEAD), 0)
        pos_j = lax.broadcasted_iota(jnp.int32, (HEAD, HEAD), 1)
        causal = (pos_j // CHUNK) <= (pos_i // CHUNK)
        for h in range(N_HEADS):
            blk = (slice(h * HEAD, (h + 1) * HEAD), slice(None))
            apply(4, jnp.where(causal, dsum(gwm_ref, blk), 0.0), (0, h), (0, h))
            apply(5, dsum(gbz_ref, (slice(h, h + 1), slice(None))), (0, slice(h, h + 1)), (0, slice(h, h + 1)))
            apply(6, dsum(gwp_ref, blk), (0, h), (0, h))
        apply(7, dsum(gps_ref, (slice(0, 1), slice(None))), full, full)
        apply(8, dsum(g1_ref, row1(8, B_DPREFFN)), full, full)
        apply(9, dsum(g1_ref, row1(0, F_DPOSTFFN)), full, full)
        apply(10, dsum(g2s_ref, (slice(C_DCW, C_DCW + 3), slice(None))), (0,), (0,))
        apply(11, dsum(g2_ref, (slice(C_DCB, C_DCB + 1), slice(None))), full, full)

    out_shape = [jax.ShapeDtypeStruct((8, HEAD), F32)]
    for nm in names:
        out_shape += [jax.ShapeDtypeStruct(params[nm][0].shape, F32)] * 4
    res = pl.pallas_call(
        body, name="small_update", out_shape=out_shape,
        compiler_params=pltpu.CompilerParams(vmem_limit_bytes=VMEM_LIMIT_BYTES),
    )(*flat_in)
    out = {nm: tuple(res[1 + 4 * i:5 + 4 * i]) for i, nm in enumerate(names)}
    return res[0], out


def kernel(x, c, w_ada, b_ada, pre_mix_g, post_mix_g, w_in, sgu_norm_g, w_spatial, b_spatial, w_pool, pool_scale, w_out, pre_ffn_g, post_ffn_g, w_up, conv_w, conv_b, w_down, loss_target, m_w_ada, m_b_ada, m_pre_mix_g, m_post_mix_g, m_w_in, m_sgu_norm_g, m_w_spatial, m_b_spatial, m_w_pool, m_pool_scale, m_w_out, m_pre_ffn_g, m_post_ffn_g, m_w_up, m_conv_w, m_conv_b, m_w_down, v_w_ada, v_b_ada, v_pre_mix_g, v_post_mix_g, v_w_in, v_sgu_norm_g, v_w_spatial, v_b_spatial, v_w_pool, v_pool_scale, v_w_out, v_pre_ffn_g, v_post_ffn_g, v_w_up, v_conv_w, v_conv_b, v_w_down):
    xi, yi, ci = _mesh_pos()
    k_me = 2 * xi + yi
    dev = 2 * k_me + ci
    s_len, d = x.shape[1], x.shape[2]
    x2d = x[0]
    tgt = loss_target[0]
    ff2 = conv_b.shape[1]
    n_ada = w_ada.shape[2]
    n_cw = conv_w.shape[2]

    k_idx = k_me.reshape(1).astype(jnp.int32)
    flags4 = (True, False, True, False)
    (w_in_s, w_out_s, w_up_s, w_down_s), (w_in_f, w_out_f, w_up_f, w_down_f) = _cast_bf16(
        [w_in[0], w_out[0], w_up[0], w_down[0]], flags4, k_idx)
    mix_flags = (True, False)
    fly_mix = _gather_begin([w_in_s, w_out_s], [w_in_f, w_out_f], mix_flags, k_idx, "gather_begin_mix")

    cw_blk = jnp.concatenate([conv_w[0], jnp.zeros((5, n_cw), F32)], axis=0)
    c_blk = c.reshape(8, d // 8) + fly_mix[3][0:1, 0:1]
    c_all, cw_all = _run_rider(_allgather_rider([c_blk, cw_blk]), "gather_c_convw")
    c_all = c_all.reshape(N_DEV, 8, d // 8).reshape(N_DEV, d)
    cw_full = jnp.concatenate([cw_all[16 * k:16 * k + 8] for k in range(N_CHIP)], axis=1)
    cvec = jnp.concatenate([cw_full[0:3], conv_b, jnp.zeros((4, ff2), F32)], axis=0)
    b_shard = lax.dynamic_slice_in_dim(b_ada, k_me * n_ada, n_ada, axis=1)
    mod_k, sc_all = _mod_shard(c_all, w_ada[0], b_shard)
    (mod_g,) = _run_rider(_allgather_rider([mod_k]), "gather_mod")
    mod_all = jnp.concatenate([mod_g[16 * k:16 * k + 8] for k in range(N_CHIP)], axis=1)
    mod_me = lax.dynamic_slice_in_dim(mod_all, dev, 1, axis=0).reshape(6, d)
    vecs = jnp.concatenate([mod_me, pre_mix_g, post_mix_g, pre_ffn_g, post_ffn_g, jnp.zeros((6, d), F32)], axis=0)

    fly_ffn = _gather_begin([w_up_s, w_down_s], [w_up_f, w_down_f], mix_flags, mod_g, "gather_begin_ffn")
    w_in_b, w_out_b = _gather_finish(_gather_end(fly_mix, mix_flags, fly_ffn[3], "gather_end_mix"), mix_flags,
                                     "gather_finish_mix")

    pos = jnp.arange(HEAD)
    causal = (pos[None, :] // CHUNK) <= (pos[:, None] // CHUNK)
    wm = jnp.where(causal[None], w_spatial[0], 0.0)
    wm_b = wm.astype(BF16)
    wmt_b = jnp.swapaxes(wm, 1, 2).astype(BF16)
    bsb = jnp.broadcast_to(b_spatial[0][:, :, None], (N_HEADS, HEAD, HEAD))
    wp_b = w_pool[0].astype(BF16)
    sgu_g = jnp.concatenate([sgu_norm_g[0], jnp.zeros((4, HEAD), F32)], axis=0)
    ps = jnp.concatenate([pool_scale, jnp.zeros((7, A_WIDTH), F32)], axis=0)

    pmats = _perm_mats(FFN_TS)
    h1, proj, cat, mixed, x2, h2p = _mix_fwd(x2d, vecs, w_in_b, w_out_b, sgu_g, wm_b, bsb, wp_b, ps, pmats,
                                             ts=MIX_TS)[0]
    w_up_b, w_down_b = _gather_finish(_gather_end(fly_ffn, mix_flags, h2p, "gather_end_ffn"), mix_flags,
                                      "gather_finish_ffn")
    up, yv, act, dy, dfp, acc_f = _ffn_fwd(h2p, x2, tgt, w_up_b, w_down_b, cvec, vecs, pmats, ts=FFN_TS)

    c_idx = ci.reshape(1).astype(jnp.int32)
    g_w_down, _ = _wgrad(act, dfp, d, WGRAD_TS_WIDE, "wgrad_down")
    (dup, dx2, dmixed, acc_c, acc_b), (land_down,) = _ffn_bwd(
        dfp, up, yv, x2, dy, mixed, w_up_b, w_down_b, cvec, vecs, pmats, ts=FFN_TS,
        rider=_sibling_rider([g_w_down], (False,)))
    (part_down,) = _sum_with_sibling([g_w_down], [land_down], (False,), c_idx, "pair_sum_down")
    g_w_up, (chips_down,) = _wgrad(h2p, dup, ff2 // 2, WGRAD_TS, "wgrad_up", rider=_chips_rider([part_down]))
    fly_up = _sibling_begin(g_w_up, True, chips_down, "sibling_begin_up")
    gx, dproj, acc_m, dwm, dwp, dbz, dsg, dps = _mix_bwd(
        dmixed, dx2, x2d, proj, vecs + fly_up[3][0:1, 0:1], w_in_b, w_out_b, sgu_g, wm_b, wmt_b, bsb, wp_b, ps,
        ts=MIX_TS)[0]
    g_w_up, land_up = _sibling_end(fly_up, True, dproj, "sibling_end_up")
    (part_up,) = _sum_with_sibling([g_w_up], [land_up], (True,), c_idx, "pair_sum_up")
    fly_chips_up = _chips_begin([part_up], c_idx, "chips_begin_up")
    g1 = jnp.concatenate([acc_f, acc_b, acc_m], axis=0)
    hflat = (N_HEADS * HEAD, HEAD)
    small_bufs = [lax.dynamic_update_slice(jnp.zeros((N_DEV * t.shape[0], t.shape[1]), t.dtype), t,
                                           (dev * t.shape[0], 0))
                  for t in (g1, acc_c, dwm.reshape(hflat), dwp.reshape(hflat), dbz, dsg, dps)]
    fly_small = _split_begin(small_bufs, 3 * len(small_bufs), _small_ici_copies, fly_chips_up[3],
                             "small_gather_begin")
    g_w_out, g_w_in = _wgrad_pair(cat, dmixed, h1, dproj, fly_small[3], WGRAD_TS_WIDE, "wgrad_mix")
    land_mix = _run_rider(_sibling_rider([g_w_in, g_w_out], (True, False)), "reduce_to_sibling")
    parts_mix = _sum_with_sibling([g_w_in, g_w_out], land_mix, (True, False), c_idx, "pair_sum_mix")
    (part_up,), (chips_up,) = _chips_end(fly_chips_up, parts_mix[0], "chips_end_up")
    fly_chips_mix = _chips_begin(parts_mix, chips_up, "chips_begin_mix")

    def adamw_of(names, reduced):
        res = {}
        for nm, red in zip(names, reduced):
            w, m, v = big_wmv[nm]
            g = red.reshape(w.shape[1], w.shape[2])
            res[nm] = (g.reshape(w.shape),) + tuple(_adamw_big(g, w, m, v, "adamw_" + nm))
        return res

    big_wmv = {"w_in": (w_in, m_w_in, v_w_in), "w_out": (w_out, m_w_out, v_w_out),
               "w_up": (w_up, m_w_up, v_w_up), "w_down": (w_down, m_w_down, v_w_down)}
    big = adamw_of(("w_up", "w_down"), _sum_chips_and_share([chips_up, chips_down], [part_up, part_down],
                                                           fly_chips_mix[3], "sum_share_ffn"))

    gathered = _small_finish(_split_end(fly_small, _small_ici_copies, big["w_down"][1], "small_gather_end"),
                             "small_gather_finish")
    g1a, g2a, gwm, gwp, gbz, gsg, gps = [t.reshape((N_DEV, t.shape[0] // N_DEV, t.shape[1])) for t in gathered]
    g2s = lax.dynamic_slice_in_dim(g2a, k_me * n_cw, n_cw, axis=2)
    params = {
        "b_ada": (b_ada, m_b_ada, v_b_ada), "pre_mix_g": (pre_mix_g, m_pre_mix_g, v_pre_mix_g),
        "post_mix_g": (post_mix_g, m_post_mix_g, v_post_mix_g),
        "sgu_norm_g": (sgu_norm_g, m_sgu_norm_g, v_sgu_norm_g), "w_spatial": (w_spatial, m_w_spatial, v_w_spatial),
        "b_spatial": (b_spatial, m_b_spatial, v_b_spatial), "w_pool": (w_pool, m_w_pool, v_w_pool),
        "pool_scale": (pool_scale, m_pool_scale, v_pool_scale), "pre_ffn_g": (pre_ffn_g, m_pre_ffn_g, v_pre_ffn_g),
        "post_ffn_g": (post_ffn_g, m_post_ffn_g, v_post_ffn_g), "conv_w": (conv_w, m_conv_w, v_conv_w),
        "conv_b": (conv_b, m_conv_b, v_conv_b),
    }
    loss_slab, small = _small_update(g1a, g2a, g2s, gwm, gwp, gbz, gsg, gps, params)

    gmod_all = jnp.concatenate(
        [g1a[:, 16 + M_DSHM], g1a[:, 16 + M_DSCM], g1a[:, 8 + B_DGM], g1a[:, 8 + B_DSHF], g1a[:, 8 + B_DSCF],
         g1a[:, F_DGF]], axis=1)
    gm = lax.dynamic_slice_in_dim(gmod_all, k_me * n_ada, n_ada, axis=1)
    gm = jnp.concatenate([gm, jnp.zeros((HEAD - N_DEV, n_ada), F32)], axis=0)
    sct = jnp.concatenate([sc_all.T, jnp.zeros((d, HEAD - N_DEV), F32)], axis=1)
    ada = tuple(_wada_update(sct, gm, w_ada, m_w_ada, v_w_ada))

    parts_mix, chips_mix = _chips_end(fly_chips_mix, ada[1], "chips_end_mix")
    big.update(adamw_of(("w_in", "w_out"), _sum_chips_and_share(chips_mix, parts_mix, loss_slab, "sum_share_mix")))

    everything = dict(small)
    everything.update(big)
    everything["w_ada"] = ada
    order = ["w_ada", "b_ada", "pre_mix_g", "post_mix_g", "w_in", "sgu_norm_g", "w_spatial", "b_spatial", "w_pool",
             "pool_scale", "w_out", "pre_ffn_g", "post_ffn_g", "w_up", "conv_w", "conv_b", "w_down"]
    outs = [loss_slab[0, 0], gx.reshape(x.shape)]
    for j in range(4):
        outs += [everything[nm][j] for nm in order]
    return tuple(outs)
```

```python
import functools

import jax
import jax.numpy as jnp
from jax import lax
from jax.experimental import pallas as pl
from jax.experimental.pallas import tpu as pltpu

F32 = jnp.float32
BF16 = jnp.bfloat16
MESH = pl.DeviceIdType.MESH

EPS = 1e-6
HEAD = 128
N_HEADS = 4
A_WIDTH = N_HEADS * HEAD
CHUNK = 64
POOL_WINDOWS = (2, 4, 8, 16)
POOL_HALO = 16
FFN_TS = 256
MIX_TS = 512
WGRAD_TS = 2048
WGRAD_TS_WIDE = 1024

ADAM_LR = 0.001
ADAM_B1 = 0.9
ADAM_B2 = 0.999
ADAM_EPS = 1e-08
ADAM_WD = 0.01
ADAM_STEP = 10

VMEM_LIMIT_BYTES = 58 * 1024 * 1024
N_DEV = 8
N_CHIP = 4


def _dot(a, b):
    return jnp.dot(a, b, preferred_element_type=F32)


def _dot_nt(a, b):
    return lax.dot_general(a, b, (((1,), (1,)), ((), ())), preferred_element_type=F32)


def _dot_tn(a, b):
    return lax.dot_general(a, b, (((0,), (0,)), ((), ())), preferred_element_type=F32)


GELU_C0 = 0.7978845608028654
GELU_C1 = GELU_C0 * 0.044715


def _gelu(x):
    return x * (0.5 + 0.5 * jnp.tanh(x * (GELU_C0 + GELU_C1 * (x * x))))


def _gelu_and_grad(x):
    x2 = x * x
    t = jnp.tanh(x * (GELU_C0 + GELU_C1 * x2))
    half = 0.5 + 0.5 * t
    grad = half + (x * (0.5 - 0.5 * (t * t))) * (GELU_C0 + (3.0 * GELU_C1) * x2)
    return x * half, grad


def _sigmoid(x):
    return 1.0 / (1.0 + jnp.exp(-x))


def _rms(x):
    return lax.rsqrt(jnp.mean(x * x, axis=-1, keepdims=True) + EPS)


def _colsum(x):
    return jnp.sum(x, axis=0, keepdims=True)


def _rowmean(x):
    return jnp.mean(x, axis=-1, keepdims=True)


def _tiled(shape, index_map):
    return pl.BlockSpec(shape, index_map)


def _resident(shape):
    nd = len(shape)
    return pl.BlockSpec(shape, lambda *_: (0,) * nd, pipeline_mode=pl.Buffered(1))


def _whole(shape):
    nd = len(shape)
    return pl.BlockSpec(shape, lambda *_: (0,) * nd)


def _seq_params():
    return pltpu.CompilerParams(dimension_semantics=("arbitrary",), vmem_limit_bytes=VMEM_LIMIT_BYTES)


def _ff_chunks(f, width=768):
    out, o = [], 0
    while o < f:
        w = min(width, f - o)
        out.append((o, w))
        o += w
    return out


def _pool_fwd(p, halo, g, t_glob):
    ext = jnp.concatenate([halo, p], axis=0)
    s = ext
    for step in range(g + 1):
        s = s + pltpu.roll(s, 1 << step, 0)
    cnt = jnp.minimum(t_glob + 1, POOL_WINDOWS[g]).astype(F32)
    return s[POOL_HALO:] / cnt - p


def _adamw_math(w, g, m, v):
    m = ADAM_B1 * m + (1.0 - ADAM_B1) * g
    v = ADAM_B2 * v + (1.0 - ADAM_B2) * (g * g)
    m_hat = m / (1.0 - ADAM_B1 ** ADAM_STEP)
    v_hat = v / (1.0 - ADAM_B2 ** ADAM_STEP)
    delta = -ADAM_LR * (m_hat / (jnp.sqrt(v_hat) + ADAM_EPS) + ADAM_WD * w)
    return delta, m, v


def _mesh_pos():
    return lax.axis_index("x"), lax.axis_index("y"), lax.axis_index("c")


class _Rider:
    def __init__(self, inputs, out_shape, sems, start, finish):
        self.inputs, self.out_shape, self.sems = list(inputs), list(out_shape), list(sems)
        self.start, self.finish = start, finish


def _call(body, *, name, grid, in_specs, out_specs, out_shape, scratch_shapes, args, rider=None):
    params = pltpu.CompilerParams(dimension_semantics=("arbitrary",) * len(grid), vmem_limit_bytes=VMEM_LIMIT_BYTES)
    if rider is None:
        res = pl.pallas_call(body, name=name, grid=grid, in_specs=in_specs, out_specs=out_specs, out_shape=out_shape,
                             scratch_shapes=scratch_shapes, compiler_params=params)(*args)
        return tuple(res), ()
    cuts = [len(in_specs), len(rider.inputs), len(out_specs), len(rider.out_shape), len(scratch_shapes),
            len(rider.sems)]

    def hosted(*refs):
        groups, a = [], 0
        for cnt in cuts:
            groups.append(refs[a:a + cnt])
            a += cnt
        ins, r_in, outs, r_out, scr, r_sem = groups
        first = functools.reduce(jnp.logical_and, [pl.program_id(k) == 0 for k in range(len(grid))])
        last = functools.reduce(jnp.logical_and, [pl.program_id(k) == grid[k] - 1 for k in range(len(grid))])

        @pl.when(first)
        def _():
            rider.start(r_in, r_out, r_sem)

        body(*ins, *outs, *scr)

        @pl.when(last)
        def _():
            rider.finish(r_in, r_out, r_sem)

    anyspec = pl.BlockSpec(memory_space=pl.ANY)
    res = pl.pallas_call(
        hosted, name=name, grid=grid,
        in_specs=list(in_specs) + [anyspec] * cuts[1], out_specs=list(out_specs) + [anyspec] * cuts[3],
        out_shape=list(out_shape) + rider.out_shape, scratch_shapes=list(scratch_shapes) + rider.sems,
        compiler_params=params)(*args, *rider.inputs)
    return tuple(res[:cuts[2]]), tuple(res[cuts[2]:])


def _run_rider(rider, name):
    n_in, n_out = len(rider.inputs), len(rider.out_shape)

    def body(*refs):
        r_in, r_out, r_sem = refs[:n_in], refs[n_in:n_in + n_out], refs[n_in + n_out:]
        rider.start(r_in, r_out, r_sem)
        rider.finish(r_in, r_out, r_sem)

    anyspec = pl.BlockSpec(memory_space=pl.ANY)
    return pl.pallas_call(body, name=name, out_shape=rider.out_shape, in_specs=[anyspec] * n_in,
                          out_specs=[anyspec] * n_out, scratch_shapes=rider.sems)(*rider.inputs)


def _allgather_rider(arrs):
    n = len(arrs)

    def plan(ins, outs, sems):
        send_sems, recv_sems, local_sems = sems
        x, y, c = _mesh_pos()
        me, sibling = (x, y, c), (x, y, 1 - c)
        chips = [(1 - x, y), (x, 1 - y), (1 - x, 1 - y)]

        def rows(a, px, py, pc):
            r = ins[a].shape[0]
            return outs[a].at[pl.ds(pl.multiple_of((4 * px + 2 * py + pc) * r, 8), r), :]

        def copy(a, k, block, to, src=None):
            return pltpu.make_async_remote_copy(
                src_ref=rows(a, *block) if src is None else src, dst_ref=rows(a, *block),
                send_sem=send_sems.at[a * 7 + k], recv_sem=recv_sems.at[a * 7 + k],
                device_id=to, device_id_type=MESH)

        local = [pltpu.make_async_copy(ins[a], rows(a, *me), local_sems.at[a]) for a in range(n)]
        first = []
        for a in range(n):
            first.append(copy(a, 0, me, sibling, src=ins[a]))
            first += [copy(a, 1 + j, me, (*chip, c), src=ins[a]) for j, chip in enumerate(chips)]
        return c, me, sibling, chips, copy, local, first

    def start(ins, outs, sems):
        *_, local, first = plan(ins, outs, sems)
        for cp in local + first:
            cp.start()

    def finish(ins, outs, sems):
        c, me, sibling, chips, copy, local, first = plan(ins, outs, sems)
        passed = []
        for a in range(n):
            for j, chip in enumerate(chips):
                copy(a, 1 + j, (*chip, c), me).wait_recv()
                fwd = copy(a, 4 + j, (*chip, c), sibling)
                fwd.start()
                passed.append(fwd)
        for a in range(n):
            copy(a, 0, sibling, me).wait_recv()
            for j, chip in enumerate(chips):
                copy(a, 4 + j, (*chip, 1 - c), me).wait_recv()
        for cp in first + passed:
            cp.wait_send()
        for mine in local:
            mine.wait()

    return _Rider(arrs, [jax.ShapeDtypeStruct((N_DEV * a.shape[0], a.shape[1]), a.dtype) for a in arrs],
                  [pltpu.SemaphoreType.DMA((7 * n,)), pltpu.SemaphoreType.DMA((7 * n,)),
                   pltpu.SemaphoreType.DMA((n,))], start, finish)


def _piece(ref, col_sharded, k, h):
    m, n = ref.shape
    if col_sharded:
        mh, nc = m // 2, n // N_CHIP
        return ref.at[pl.ds(pl.multiple_of(h * mh, 16), mh), pl.ds(pl.multiple_of(k * nc, 128), nc)]
    rp = m // (2 * N_CHIP)
    return ref.at[pl.ds(pl.multiple_of((2 * k + h) * rp, 16), rp), :]


def _piece_shape(shape, col_sharded):
    m, n = shape
    return (m // 2, n // N_CHIP) if col_sharded else (m // (2 * N_CHIP), n)


def _cast_bf16(arrs, col_flags, k_idx):
    n = len(arrs)

    def body(k_ref, *refs):
        for a in range(n):
            val = refs[a][...].astype(BF16)
            refs[n + a][...] = val
            refs[2 * n + a][...] = val

    whole = [pl.BlockSpec(a.shape, lambda i, k_ref: (0, 0)) for a in arrs]
    window = [pl.BlockSpec(a.shape, (lambda i, k_ref: (0, k_ref[0])) if col else (lambda i, k_ref: (k_ref[0], 0)))
              for a, col in zip(arrs, col_flags)]
    res = pl.pallas_call(
        body, name="cast_weights",
        grid_spec=pltpu.PrefetchScalarGridSpec(num_scalar_prefetch=1, grid=(1,), in_specs=whole,
                                               out_specs=whole + window),
        out_shape=[jax.ShapeDtypeStruct(a.shape, BF16) for a in arrs]
        + [jax.ShapeDtypeStruct(fs, BF16) for fs in _full_shapes(arrs, col_flags)],
        compiler_params=pltpu.CompilerParams(vmem_limit_bytes=VMEM_LIMIT_BYTES))(k_idx, *arrs)
    return list(res[:n]), list(res[n:])


def _full_shapes(shards, col_flags):
    return [(s.shape[0], s.shape[1] * N_CHIP) if col else (s.shape[0] * N_CHIP, s.shape[1])
            for s, col in zip(shards, col_flags)]


def _ici_copies(shard_refs, full_refs, send_sems, recv_sems, col_flags):
    x, y, c = _mesh_pos()
    k_me = 2 * x + y
    copies = []
    for a, (s_ref, f_ref) in enumerate(zip(shard_refs, full_refs)):
        rows = s_ref.shape[0] // 2
        src = s_ref.at[pl.ds(pl.multiple_of(c * rows, 16), rows), :]
        for j, chip in enumerate([(1 - x, y), (x, 1 - y), (1 - x, 1 - y)]):
            copies.append(pltpu.make_async_remote_copy(
                src_ref=src, dst_ref=_piece(f_ref, col_flags[a], k_me, c),
                send_sem=send_sems.at[a * 3 + j], recv_sem=recv_sems.at[a * 3 + j],
                device_id=(*chip, c), device_id_type=MESH))
    return copies


def _split_begin(bufs, n_sems, make_copies, after, name):
    n = len(bufs)
    hbm = pl.BlockSpec(memory_space=pltpu.HBM)
    sem = pl.BlockSpec(memory_space=pltpu.SEMAPHORE)

    def body(*refs):
        for cp in make_copies(refs[:n], refs[n + 1], refs[n + 2]):
            cp.start()
        refs[-1][...] = jnp.zeros_like(refs[-1])

    args = [pltpu.with_memory_space_constraint(t, pltpu.HBM) for t in bufs]
    res = pl.pallas_call(
        body, name=name,
        out_shape=[pltpu.SemaphoreType.DMA((n_sems,)), pltpu.SemaphoreType.DMA((n_sems,))]
        + [pltpu.HBM(t.shape, t.dtype) for t in args] + [jax.ShapeDtypeStruct((8, HEAD), F32)],
        in_specs=[hbm] * n + [pl.BlockSpec(memory_space=pl.ANY)],
        out_specs=[sem, sem] + [hbm] * n + [pl.BlockSpec(memory_space=pltpu.VMEM)],
        input_output_aliases={i: 2 + i for i in range(n)},
        compiler_params=pltpu.CompilerParams(has_side_effects=pltpu.SideEffectType.DATAFLOW_SIDE_EFFECTING),
    )(*args, after)
    return res[0], res[1], list(res[2:2 + n]), res[-1]


def _split_end(handle, make_copies, after, name):
    send_sems, recv_sems, bufs, _ = handle
    n = len(bufs)
    hbm = pl.BlockSpec(memory_space=pltpu.HBM)
    sem = pl.BlockSpec(memory_space=pltpu.SEMAPHORE)

    def body(*refs):
        for cp in make_copies(refs[:n], refs[n], refs[n + 1]):
            cp.wait_send()
            cp.wait_recv()

    res = pl.pallas_call(
        body, name=name,
        out_shape=[pltpu.HBM(t.shape, t.dtype) for t in bufs],
        in_specs=[hbm] * n + [sem, sem, pl.BlockSpec(memory_space=pl.ANY)],
        out_specs=[hbm] * n,
        input_output_aliases={i: i for i in range(n)},
        compiler_params=pltpu.CompilerParams(has_side_effects=pltpu.SideEffectType.DATAFLOW_SIDE_EFFECTING),
    )(*bufs, send_sems, recv_sems, after)
    return list(res)


def _gather_copies(n, col_flags):
    return lambda refs, send_sems, recv_sems: _ici_copies(refs[:n], refs[n:], send_sems, recv_sems, col_flags)


def _gather_begin(shards, fulls, col_flags, after, name):
    n = len(shards)
    return _split_begin(list(shards) + list(fulls), 3 * n, _gather_copies(n, col_flags), after, name)


def _gather_end(handle, col_flags, after, name):
    n = len(handle[2]) // 2
    return _split_end(handle, _gather_copies(n, col_flags), after, name)[n:]


def _sibling_copies(col_flag):
    def make(refs, send_sems, recv_sems):
        grad_ref, land_ref = refs
        x, y, c = _mesh_pos()
        return [pltpu.make_async_remote_copy(
            src_ref=_piece(grad_ref, col_flag, k, 1 - c), dst_ref=land_ref.at[k],
            send_sem=send_sems.at[k], recv_sem=recv_sems.at[k],
            device_id=(x, y, 1 - c), device_id_type=MESH) for k in range(N_CHIP)]
    return make


def _chips_copies(n):
    def make(refs, send_sems, recv_sems):
        parts, landed = refs[:n], refs[n:]
        x, y, c = _mesh_pos()
        k_me = 2 * x + y
        copies = []
        for a in range(n):
            for j, chip in enumerate([(1 - x, y), (x, 1 - y), (1 - x, 1 - y)]):
                copies.append(pltpu.make_async_remote_copy(
                    src_ref=parts[a].at[2 * chip[0] + chip[1]], dst_ref=landed[a].at[k_me],
                    send_sem=send_sems.at[a * 3 + j], recv_sem=recv_sems.at[a * 3 + j],
                    device_id=(*chip, c), device_id_type=MESH))
        return copies
    return make


def _chips_begin(parts, after, name):
    landed = [lax.empty(p.shape, p.dtype) for p in parts]
    return _split_begin(list(parts) + landed, 3 * len(parts), _chips_copies(len(parts)), after, name)


def _chips_end(handle, after, name):
    n = len(handle[2]) // 2
    res = _split_end(handle, _chips_copies(n), after, name)
    return res[:n], res[n:]


def _small_rows(buf_ref, px, py, pc):
    r = buf_ref.shape[0] // N_DEV
    return buf_ref.at[pl.ds(pl.multiple_of((4 * px + 2 * py + pc) * r, 8), r), :]


def _small_ici_copies(refs, send_sems, recv_sems):
    x, y, c = _mesh_pos()
    copies = []
    for a, buf in enumerate(refs):
        mine = _small_rows(buf, x, y, c)
        for j, chip in enumerate([(1 - x, y), (x, 1 - y), (1 - x, 1 - y)]):
            copies.append(pltpu.make_async_remote_copy(
                src_ref=mine, dst_ref=mine, send_sem=send_sems.at[a * 3 + j], recv_sem=recv_sems.at[a * 3 + j],
                device_id=(*chip, c), device_id_type=MESH))
    return copies


def _small_finish(bufs, name):
    n = len(bufs)

    def body(*refs):
        buf_refs = refs[n:2 * n]
        send_sems, recv_sems = refs[2 * n:]
        x, y, c = _mesh_pos()
        owners = [(x, y), (1 - x, y), (x, 1 - y), (1 - x, 1 - y)]
        passed, arriving = [], []
        for a in range(n):
            for j, (px, py) in enumerate(owners):
                for pc, group in ((c, passed), (1 - c, arriving)):
                    rows = _small_rows(buf_refs[a], px, py, pc)
                    group.append(pltpu.make_async_remote_copy(
                        src_ref=rows, dst_ref=rows, send_sem=send_sems.at[a * 4 + j],
                        recv_sem=recv_sems.at[a * 4 + j], device_id=(x, y, 1 - c), device_id_type=MESH))
        for cp in passed:
            cp.start()
        for cp in arriving:
            cp.wait_recv()
        for cp in passed:
            cp.wait_send()

    anyspec = pl.BlockSpec(memory_space=pl.ANY)
    return pl.pallas_call(
        body, name=name, out_shape=[jax.ShapeDtypeStruct(b.shape, b.dtype) for b in bufs],
        in_specs=[anyspec] * n, out_specs=[anyspec] * n, input_output_aliases={a: a for a in range(n)},
        scratch_shapes=[pltpu.SemaphoreType.DMA((4 * n,)), pltpu.SemaphoreType.DMA((4 * n,))],
    )(*bufs)


def _sibling_begin(grad, col_flag, after, name):
    land = lax.empty((N_CHIP,) + _piece_shape(grad.shape, col_flag), grad.dtype)
    return _split_begin([grad, land], N_CHIP, _sibling_copies(col_flag), after, name)


def _sibling_end(handle, col_flag, after, name):
    return _split_end(handle, _sibling_copies(col_flag), after, name)


def _gather_finish(fulls, col_flags, name):
    n = len(fulls)

    def body(*refs):
        full_refs = refs[n:2 * n]
        send_sems, recv_sems = refs[2 * n:]
        x, y, c = _mesh_pos()
        passed, arriving = [], []
        for a in range(n):
            for j, chip in enumerate([(1 - x, y), (x, 1 - y), (1 - x, 1 - y)]):
                k_from = 2 * chip[0] + chip[1]
                for h, group in ((c, passed), (1 - c, arriving)):
                    win = _piece(full_refs[a], col_flags[a], k_from, h)
                    group.append(pltpu.make_async_remote_copy(
                        src_ref=win, dst_ref=win, send_sem=send_sems.at[a * 3 + j],
                        recv_sem=recv_sems.at[a * 3 + j], device_id=(x, y, 1 - c), device_id_type=MESH))
        for cp in passed:
            cp.start()
        for cp in arriving:
            cp.wait_recv()
        for cp in passed:
            cp.wait_send()

    anyspec = pl.BlockSpec(memory_space=pl.ANY)
    return pl.pallas_call(
        body, name=name,
        out_shape=[jax.ShapeDtypeStruct(f.shape, f.dtype) for f in fulls],
        in_specs=[anyspec] * n, out_specs=[anyspec] * n,
        input_output_aliases={a: a for a in range(n)},
        scratch_shapes=[pltpu.SemaphoreType.DMA((3 * n,)), pltpu.SemaphoreType.DMA((3 * n,))],
    )(*fulls)


def _sibling_rider(grads, col_flags):
    n = len(grads)
    pshapes = [_piece_shape(g.shape, col) for g, col in zip(grads, col_flags)]

    def copies(ins, outs, sems):
        send_sems, recv_sems = sems
        x, y, c = _mesh_pos()
        return [pltpu.make_async_remote_copy(
            src_ref=_piece(ins[a], col_flags[a], k, 1 - c), dst_ref=outs[a].at[k],
            send_sem=send_sems.at[a * N_CHIP + k], recv_sem=recv_sems.at[a * N_CHIP + k],
            device_id=(x, y, 1 - c), device_id_type=MESH) for a in range(n) for k in range(N_CHIP)]

    def start(ins, outs, sems):
        for cp in copies(ins, outs, sems):
            cp.start()

    def finish(ins, outs, sems):
        cps = copies(ins, outs, sems)
        for cp in cps:
            cp.wait_recv()
        for cp in cps:
            cp.wait_send()

    return _Rider(grads, [jax.ShapeDtypeStruct((N_CHIP,) + ps, g.dtype) for ps, g in zip(pshapes, grads)],
                  [pltpu.SemaphoreType.DMA((N_CHIP * n,)), pltpu.SemaphoreType.DMA((N_CHIP * n,))], start, finish)


def _sum_with_sibling(grads, landed, col_flags, c_idx, name):
    n = len(grads)
    pshapes = [_piece_shape(g.shape, col) for g, col in zip(grads, col_flags)]

    def body(c_ref, *refs):
        ins, lands, outs = refs[:n], refs[n:2 * n], refs[2 * n:]
        for a in range(n):
            outs[a][0] = (ins[a][...] + lands[a][0]).astype(BF16)

    in_specs = []
    for ps, col in zip(pshapes, col_flags):
        if col:
            in_specs.append(pl.BlockSpec(ps, lambda k, c_ref: (c_ref[0], k)))
        else:
            in_specs.append(pl.BlockSpec(ps, lambda k, c_ref: (2 * k + c_ref[0], 0)))
    land_specs = [pl.BlockSpec((1,) + ps, lambda k, c_ref: (k, 0, 0)) for ps in pshapes]
    return pl.pallas_call(
        body, name=name,
        grid_spec=pltpu.PrefetchScalarGridSpec(
            num_scalar_prefetch=1, grid=(N_CHIP,),
            in_specs=in_specs + land_specs, out_specs=land_specs),
        out_shape=[jax.ShapeDtypeStruct((N_CHIP,) + ps, BF16) for ps in pshapes],
        compiler_params=pltpu.CompilerParams(dimension_semantics=("arbitrary",), vmem_limit_bytes=VMEM_LIMIT_BYTES),
    )(c_idx, *grads, *landed)


def _chips_rider(parts):
    n = len(parts)

    def plan(ins, outs, sems, arriving):
        send_sems, recv_sems = sems
        x, y, c = _mesh_pos()
        k_me = 2 * x + y
        copies = []
        for a in range(n):
            for j, chip in enumerate([(1 - x, y), (x, 1 - y), (1 - x, 1 - y)]):
                k_peer = 2 * chip[0] + chip[1]
                copies.append(pltpu.make_async_remote_copy(
                    src_ref=ins[a].at[k_peer], dst_ref=outs[a].at[k_peer if arriving else k_me],
                    send_sem=send_sems.at[a * 3 + j], recv_sem=recv_sems.at[a * 3 + j],
                    device_id=(*chip, c), device_id_type=MESH))
        return copies

    def start(ins, outs, sems):
        for cp in plan(ins, outs, sems, False):
            cp.start()

    def finish(ins, outs, sems):
        arrivals = plan(ins, outs, sems, True)
        for cp in arrivals:
            cp.wait_recv()
        for cp in arrivals:
            cp.wait_send()

    return _Rider(parts, [jax.ShapeDtypeStruct(p.shape, p.dtype) for p in parts],
                  [pltpu.SemaphoreType.DMA((3 * n,)), pltpu.SemaphoreType.DMA((3 * n,))], start, finish)


def _sum_chips_and_share(landed, parts, after, name):
    n = len(landed)

    def body(*refs):
        ins, own, outs, red = refs[:n], refs[n:2 * n], refs[2 * n + 1:3 * n + 1], refs[3 * n + 1:4 * n + 1]
        send_sems, recv_sems, local_sems = refs[4 * n + 1:]
        x, y, c = _mesh_pos()
        sibling = (x, y, 1 - c)
        k_me = 2 * x + y
        copies, local = [], []
        for a in range(n):
            for k in range(N_CHIP):
                @pl.when(k_me == k)
                def _():
                    term = own[a][k].astype(F32)
                    red[a][...] = term if k == 0 else red[a][...] + term

                @pl.when(k_me != k)
                def _():
                    term = ins[a][k].astype(F32)
                    red[a][...] = term if k == 0 else red[a][...] + term

            mine = pltpu.make_async_copy(red[a], outs[a].at[c], local_sems.at[a])
            mine.start()
            local.append(mine)
            cp = pltpu.make_async_remote_copy(
                src_ref=red[a], dst_ref=outs[a].at[c],
                send_sem=send_sems.at[a], recv_sem=recv_sems.at[a],
                device_id=sibling, device_id_type=MESH)
            cp.start()
            copies.append(cp)
        for a in range(n):
            pltpu.make_async_remote_copy(
                src_ref=red[a], dst_ref=outs[a].at[1 - c],
                send_sem=send_sems.at[a], recv_sem=recv_sems.at[a],
                device_id=sibling, device_id_type=MESH).wait_recv()
        for cp in copies:
            cp.wait_send()
        for mine in local:
            mine.wait()

    return pl.pallas_call(
        body, name=name,
        out_shape=[jax.ShapeDtypeStruct((2,) + l.shape[1:], F32) for l in landed],
        in_specs=[pl.BlockSpec(memory_space=pltpu.VMEM)] * (2 * n + 1),
        out_specs=[pl.BlockSpec(memory_space=pl.ANY)] * n,
        scratch_shapes=[pltpu.VMEM(l.shape[1:], F32) for l in landed]
        + [pltpu.SemaphoreType.DMA((n,)), pltpu.SemaphoreType.DMA((n,)), pltpu.SemaphoreType.DMA((n,))],
        compiler_params=pltpu.CompilerParams(vmem_limit_bytes=VMEM_LIMIT_BYTES),
    )(*landed, *parts, after)


def _mod_shard(c_all, w_ada, b_shard):
    def body(c_ref, w_ref, b_ref, o_ref, sc_ref):
        cc = c_ref[...]
        sc = cc * _sigmoid(cc)
        sc_ref[...] = sc
        o_ref[...] = _dot(sc, w_ref[...]) + b_ref[...]

    nb, d = c_all.shape
    nn = w_ada.shape[1]
    return pl.pallas_call(
        body, name="mod_shard",
        out_shape=[jax.ShapeDtypeStruct((nb, nn), F32), jax.ShapeDtypeStruct((nb, d), F32)],
        compiler_params=pltpu.CompilerParams(vmem_limit_bytes=VMEM_LIMIT_BYTES),
    )(c_all, w_ada, b_shard)


V_SH_M, V_SC_M, V_G_M, V_SH_F, V_SC_F, V_G_F, V_PRE_MIX, V_POST_MIX, V_PRE_FFN, V_POST_FFN = range(10)


def _vrow(vec_ref, r):
    return vec_ref[r:r + 1, :]


def _mix_fwd(x, vecs, w_in_b, w_out_b, sgu_g, wm_b, bsb, wp_b, ps, pmats, ts):
    s_len, d = x.shape
    nt = s_len // ts
    n_proj = w_in_b.shape[1]

    def body(x_ref, vec_ref, win_ref, wout_ref, sg_ref, wm_ref, bs_ref, wp_ref, ps_ref, pm_ref,
             h1_ref, proj_ref, cat_ref, mixed_ref, x2_ref, h2_ref, carry_ref):
        i = pl.program_id(0)

        @pl.when(i == 0)
        def _():
            carry_ref[...] = jnp.zeros_like(carry_ref)

        sub = FFN_TS
        nsub = ts // sub

        def project(s):
            rs = slice(s * sub, (s + 1) * sub)
            x = x_ref[rs, :]
            h1 = (((x * _rms(x)) * _vrow(vec_ref, V_PRE_MIX)) * (1.0 + _vrow(vec_ref, V_SC_M))
                  + _vrow(vec_ref, V_SH_M)).astype(BF16)
            h1_ref[rs, :] = h1
            proj = _dot(h1, win_ref[...])
            proj_ref[rs, :] = proj.astype(BF16)
            return proj

        def mix(s, proj, halo):
            r0 = s * sub
            t_glob = lax.broadcasted_iota(jnp.int32, (sub, HEAD), 0) + (i * ts + r0)
            for h in range(N_HEADS):
                u = _gelu(proj[:, h * HEAD:(h + 1) * HEAD])
                v = _gelu(proj[:, A_WIDTH + h * HEAD:A_WIDTH + (h + 1) * HEAD])
                vn = ((v * _rms(v)) * sg_ref[h:h + 1, :]).astype(BF16)
                for b in range(sub // HEAD):
                    rs = slice(b * HEAD, (b + 1) * HEAD)
                    z = _dot(wm_ref[h], vn[rs]) + bs_ref[h]
                    cat_ref[r0 + b * HEAD:r0 + (b + 1) * HEAD, h * HEAD:(h + 1) * HEAD] = (u[rs] * z).astype(BF16)
            for g in range(len(POOL_WINDOWS)):
                gs = slice(g * HEAD, (g + 1) * HEAD)
                p = proj[:, 2 * A_WIDTH + g * HEAD:2 * A_WIDTH + (g + 1) * HEAD]
                pooled = _pool_fwd(p, halo[:, gs], g, t_glob)
                yb = _dot(pooled.astype(BF16), wp_ref[g]) * ps_ref[0:1, gs]
                cat_ref[r0:r0 + sub, A_WIDTH + g * HEAD:A_WIDTH + (g + 1) * HEAD] = yb.astype(BF16)

        def finish(s, mixed):
            rs = slice(s * sub, (s + 1) * sub)
            mixed_ref[rs, :] = mixed
            x2 = x_ref[rs, :] + _vrow(vec_ref, V_G_M) * ((mixed * _rms(mixed)) * _vrow(vec_ref, V_POST_MIX))
            x2_ref[rs, :] = x2
            h2 = (((x2 * _rms(x2)) * _vrow(vec_ref, V_PRE_FFN)) * (1.0 + _vrow(vec_ref, V_SC_F))
                  + _vrow(vec_ref, V_SH_F)).astype(BF16)
            h2_ref[rs, :] = _permute_bf16(pm_ref[0], h2)

        projs = [project(0)]
        halo = carry_ref[...]
        mixed_prev = None
        for s in range(nsub):
            if s + 1 < nsub:
                projs.append(project(s + 1))
            mix(s, projs[s], halo)
            halo = projs[s][sub - POOL_HALO:sub, 2 * A_WIDTH:]
            mixed = _dot(cat_ref[s * sub:(s + 1) * sub, :], wout_ref[...])
            if mixed_prev is not None:
                finish(s - 1, mixed_prev)
            mixed_prev = mixed
        carry_ref[...] = halo
        finish(nsub - 1, mixed_prev)

    row = lambda i: (i, 0)
    return _call(
        body, name="mix_fwd", grid=(nt,),
        in_specs=[_tiled((ts, d), row), _whole(vecs.shape), _resident(w_in_b.shape), _resident(w_out_b.shape),
                  _whole(sgu_g.shape), _whole(wm_b.shape), _whole(bsb.shape), _whole(wp_b.shape), _whole(ps.shape),
                  _whole(pmats.shape)],
        out_specs=[_tiled((ts, d), row), _tiled((ts, n_proj), row), _tiled((ts, d), row),
                   _tiled((ts, d), row), _tiled((ts, d), row), _tiled((ts, d), row)],
        out_shape=[jax.ShapeDtypeStruct((s_len, d), BF16), jax.ShapeDtypeStruct((s_len, n_proj), BF16),
                   jax.ShapeDtypeStruct((s_len, d), BF16), jax.ShapeDtypeStruct((s_len, d), F32),
                   jax.ShapeDtypeStruct((s_len, d), F32), jax.ShapeDtypeStruct((s_len, d), BF16)],
        scratch_shapes=[pltpu.VMEM((POOL_HALO, A_WIDTH), F32)],
        args=(x, vecs, w_in_b, w_out_b, sgu_g, wm_b, bsb, wp_b, ps, pmats))


def _perm_mats(ts):
    p = jnp.arange(ts)
    pm = (((p % 8) * (ts // 8) + p // 8)[:, None] == p[None, :]).astype(BF16)
    return jnp.stack([pm, pm.T])


def _permute_bf16(pm, xb):
    return _dot(pm, xb).astype(BF16)


def _permute_f32(pm, x):
    hi = x.astype(BF16)
    lo = (x - hi.astype(F32)).astype(BF16)
    return _dot(pm, hi) + _dot(pm, lo)


def _conv_out(u, um2, um1, cv_ref, cols):
    return (cv_ref[3:4, cols] + um2 * cv_ref[0:1, cols] + um1 * cv_ref[1:2, cols] + u * cv_ref[2:3, cols])


F_LOSS, F_DGF, F_DPOSTFFN = 0, 1, 2
B_DSHF, B_DSCF, B_DPREFFN, B_DGM, B_DPOSTMIX = 0, 1, 2, 3, 4
M_DSHM, M_DSCM, M_DPREMIX = 0, 1, 2
C_DCB, C_DCW = 0, 1


def _ffn_fwd(h2p, x2, tgt, w_up_b, w_down_b, cvec, vecs, pmats, ts):
    s_len, d = x2.shape
    ff2 = w_up_b.shape[1]
    ff = ff2 // 2
    nt = s_len // ts
    chunks = _ff_chunks(ff)

    def body(h2_ref, x2_ref, t_ref, wu_ref, wd_ref, cv_ref, vec_ref, pm_ref,
             up_ref, y_ref, act_ref, dy_ref, df_ref, acc_ref, carry_ref):
        @pl.when(pl.program_id(0) == 0)
        def _():
            carry_ref[...] = jnp.zeros_like(carry_ref)
            acc_ref[...] = jnp.zeros_like(acc_ref)

        h2v = h2_ref[...]

        def up_dots(o, w):
            return [_dot(h2v, wu_ref[:, base + o:base + o + w]) for base in (0, ff)]

        f = None
        pending = None
        nxt = up_dots(*chunks[0])
        for ci, (o, w) in enumerate(chunks):
            us = nxt
            if ci + 1 < len(chunks):
                nxt = up_dots(*chunks[ci + 1])
            if pending is not None:
                part = _dot(pending[0], wd_ref[pending[1]:pending[1] + pending[2], :])
                f = part if f is None else f + part
            sub0 = lax.broadcasted_iota(jnp.int32, (8, w), 0) == 0
            ys = []
            for base, u in zip((0, ff), us):
                cols = slice(base + o, base + o + w)
                up_ref[:, cols] = u.astype(BF16)
                b1 = jnp.where(sub0, pltpu.roll(carry_ref[8:16, cols], 1, 0), pltpu.roll(u[ts - 8:ts], 1, 0))
                b2 = jnp.where(sub0, pltpu.roll(carry_ref[0:8, cols], 1, 0), pltpu.roll(u[ts - 16:ts - 8], 1, 0))
                um1 = jnp.concatenate([b1, u[:ts - 8]], axis=0)
                um2 = jnp.concatenate([b2, b1, u[:ts - 16]], axis=0)
                ys.append(_conv_out(u, um2, um1, cv_ref, cols))
                carry_ref[:, cols] = u[ts - 16:ts]
            gate, val = ys
            sg = _sigmoid(gate)
            gs = gate * sg
            act = (gs * val).astype(BF16)
            act_ref[:, o:o + w] = act
            y_ref[:, o:o + w] = (val * (sg + gs * (1.0 - sg))).astype(BF16)
            y_ref[:, ff + o:ff + o + w] = gs.astype(BF16)
            pending = (act, o, w)
        f = f + _dot(pending[0], wd_ref[pending[1]:pending[1] + pending[2], :])
        f = _permute_f32(pm_ref[1], f)
        r3 = _rms(f)
        fhat = f * r3
        post = _vrow(vec_ref, V_POST_FFN)
        g_f = _vrow(vec_ref, V_G_F)
        fn = fhat * post
        e = (x2_ref[...] + g_f * fn) - t_ref[...]
        dy = e * (1.0 / d)
        dy_ref[...] = dy
        dfn = dy * g_f
        acc_ref[F_LOSS:F_LOSS + 1, :] += _colsum(e * e)
        acc_ref[F_DGF:F_DGF + 1, :] += _colsum(dy * fn)
        acc_ref[F_DPOSTFFN:F_DPOSTFFN + 1, :] += _colsum(dfn * fhat)
        dfhat = dfn * post
        df = (r3 * (dfhat - fhat * _rowmean(dfhat * fhat))).astype(BF16)
        df_ref[...] = _permute_bf16(pm_ref[0], df)

    row = lambda i: (i, 0)
    return pl.pallas_call(
        body, name="ffn_fwd", grid=(nt,),
        in_specs=[_tiled((ts, d), row), _tiled((ts, d), row), _tiled((ts, d), row), _resident(w_up_b.shape),
                  _resident(w_down_b.shape), _whole(cvec.shape), _whole(vecs.shape), _whole(pmats.shape)],
        out_specs=[_tiled((ts, ff2), row), _tiled((ts, ff2), row), _tiled((ts, ff), row), _tiled((ts, d), row),
                   _tiled((ts, d), row), _whole((8, d))],
        out_shape=[jax.ShapeDtypeStruct((s_len, ff2), BF16), jax.ShapeDtypeStruct((s_len, ff2), BF16),
                   jax.ShapeDtypeStruct((s_len, ff), BF16), jax.ShapeDtypeStruct((s_len, d), F32),
                   jax.ShapeDtypeStruct((s_len, d), BF16), jax.ShapeDtypeStruct((8, d), F32)],
        scratch_shapes=[pltpu.VMEM((16, ff2), F32)],
        compiler_params=_seq_params(),
    )(h2p, x2, tgt, w_up_b, w_down_b, cvec, vecs, pmats)


def _ffn_bwd(dfp, upp, yp, x2, dy, mixed, w_up_b, w_down_b, cvec, vecs, pmats, ts, rider=None):
    s_len, d = x2.shape
    ff2 = w_up_b.shape[1]
    ff = ff2 // 2
    nt = s_len // ts
    chunks = _ff_chunks(ff, 512)

    def body(df_ref, up_ref, y_ref, x2_ref, dy_ref, mx_ref, wu_ref, wd_ref, cv_ref, vec_ref, pm_ref,
             dup_ref, dx2_ref, dmx_ref, accc_ref, acc_ref, carry_ref):
        @pl.when(pl.program_id(0) == 0)
        def _():
            carry_ref[...] = jnp.zeros_like(carry_ref)
            accc_ref[...] = jnp.zeros_like(accc_ref)
            acc_ref[...] = jnp.zeros_like(acc_ref)

        dfv = df_ref[...]

        def dh2_add(acc, dups, o, w):
            for base, dup in zip((0, ff), dups):
                part = _dot_nt(dup, wu_ref[:, base + o:base + o + w])
                acc = part if acc is None else acc + part
            return acc

        dh2 = None
        pending = None
        nxt = _dot_nt(dfv, wd_ref[chunks[0][0]:chunks[0][0] + chunks[0][1], :])
        for ci, (o, w) in enumerate(chunks):
            dact = nxt
            if ci + 1 < len(chunks):
                o2, w2 = chunks[ci + 1]
                nxt = _dot_nt(dfv, wd_ref[o2:o2 + w2, :])
            if pending is not None:
                dh2 = dh2_add(dh2, *pending)
            sub7 = lax.broadcasted_iota(jnp.int32, (8, w), 0) == 7
            dups = []
            dys = (dact * y_ref[:, o:o + w].astype(F32), dact * y_ref[:, ff + o:ff + o + w].astype(F32))
            for base, dyv in zip((0, ff), dys):
                cols = slice(base + o, base + o + w)
                u = up_ref[:, cols].astype(F32)
                e0 = jnp.where(sub7, pltpu.roll(carry_ref[0:8, cols], 7, 0), pltpu.roll(dyv[0:8], 7, 0))
                e1 = jnp.where(sub7, pltpu.roll(carry_ref[8:16, cols], 7, 0), pltpu.roll(dyv[8:16], 7, 0))
                dyp1 = jnp.concatenate([dyv[8:], e0], axis=0)
                dyp2 = jnp.concatenate([dyv[16:], e0, e1], axis=0)
                accc_ref[C_DCB:C_DCB + 1, cols] += _colsum(dyv)
                accc_ref[C_DCW + 0:C_DCW + 1, cols] += _colsum(dyp2 * u)
                accc_ref[C_DCW + 1:C_DCW + 2, cols] += _colsum(dyp1 * u)
                accc_ref[C_DCW + 2:C_DCW + 3, cols] += _colsum(dyv * u)
                dup = (dyv * cv_ref[2:3, cols] + dyp1 * cv_ref[1:2, cols] + dyp2 * cv_ref[0:1, cols]).astype(BF16)
                dup_ref[:, cols] = dup
                dups.append(dup)
                carry_ref[:, cols] = dyv[0:16]
            pending = (dups, o, w)
        dh2 = dh2_add(dh2, *pending)
        dh2 = _permute_f32(pm_ref[1], dh2)
        x2 = x2_ref[...]
        r2 = _rms(x2)
        xn = x2 * r2
        pre = _vrow(vec_ref, V_PRE_FFN)
        one_sc = 1.0 + _vrow(vec_ref, V_SC_F)
        acc_ref[B_DSHF:B_DSHF + 1, :] += _colsum(dh2)
        acc_ref[B_DSCF:B_DSCF + 1, :] += _colsum(dh2 * (xn * pre))
        acc_ref[B_DPREFFN:B_DPREFFN + 1, :] += _colsum(dh2 * xn * one_sc)
        dxn = dh2 * pre * one_sc
        dx2 = dy_ref[...] + r2 * (dxn - xn * _rowmean(dxn * xn))
        dx2_ref[...] = dx2
        mixed = mx_ref[...]
        rm = _rms(mixed)
        mhat = mixed * rm
        post = _vrow(vec_ref, V_POST_MIX)
        acc_ref[B_DGM:B_DGM + 1, :] += _colsum(dx2 * (mhat * post))
        dmn = dx2 * _vrow(vec_ref, V_G_M)
        acc_ref[B_DPOSTMIX:B_DPOSTMIX + 1, :] += _colsum(dmn * mhat)
        dmhat = dmn * post
        dmx_ref[...] = (rm * (dmhat - mhat * _rowmean(dmhat * mhat))).astype(BF16)

    rev = lambda i: (nt - 1 - i, 0)
    return _call(
        body, name="ffn_bwd", grid=(nt,), rider=rider,
        in_specs=[_tiled((ts, d), rev), _tiled((ts, ff2), rev), _tiled((ts, ff2), rev), _tiled((ts, d), rev),
                  _tiled((ts, d), rev), _tiled((ts, d), rev), _resident(w_up_b.shape), _resident(w_down_b.shape),
                  _whole(cvec.shape), _whole(vecs.shape), _whole(pmats.shape)],
        out_specs=[_tiled((ts, ff2), rev), _tiled((ts, d), rev), _tiled((ts, d), rev), _whole((8, ff2)),
                   _whole((8, d))],
        out_shape=[jax.ShapeDtypeStruct((s_len, ff2), BF16), jax.ShapeDtypeStruct((s_len, d), F32),
                   jax.ShapeDtypeStruct((s_len, d), BF16), jax.ShapeDtypeStruct((8, ff2), F32),
                   jax.ShapeDtypeStruct((8, d), F32)],
        scratch_shapes=[pltpu.VMEM((16, ff2), F32)],
        args=(dfp, upp, yp, x2, dy, mixed, w_up_b, w_down_b, cvec, vecs, pmats))


def _mix_bwd(dmixed, dx2, x, proj, vecs, w_in_b, w_out_b, sgu_g, wm_b, wmt_b, bsb, wp_b, ps, ts):
    s_len, d = x.shape
    nt = s_len // ts
    n_proj = proj.shape[1]
    per = ts // POOL_HALO

    def body(dmx_ref, dx2_ref, x_ref, proj_ref, projh_ref, vec_ref, win_ref, wout_ref, sg_ref, wm_ref, wmt_ref,
             bs_ref, wp_ref, ps_ref,
             gx_ref, dproj_ref, acc_ref, dwm_out, dwp_out, db_ref, dg_ref, dps_ref,
             carry_ref, dwm_ref, dwp_ref, dbz_ref):
        i = pl.program_id(0)
        tile = nt - 1 - i

        @pl.when(i == 0)
        def _():
            carry_ref[...] = jnp.zeros_like(carry_ref)
            for r in (acc_ref, dwm_ref, dwp_ref, dbz_ref, dg_ref, dps_ref):
                r[...] = jnp.zeros_like(r)

        sub = FFN_TS
        nsub = ts // sub
        ext = sub + POOL_HALO

        def cotangent(s):
            return _dot_nt(dmx_ref[s * sub:(s + 1) * sub, :], wout_ref[...])

        def back(s, dcat, halo, later):
            r0 = s * sub
            rows = slice(r0, r0 + sub)
            t_glob = lax.broadcasted_iota(jnp.int32, (sub, HEAD), 0) + (tile * ts + r0)
            for h in range(N_HEADS):
                hs = slice(h * HEAD, (h + 1) * HEAD)
                vs = slice(A_WIDTH + h * HEAD, A_WIDTH + (h + 1) * HEAD)
                gain = sg_ref[h:h + 1, :]
                for b in range(sub // HEAD):
                    blk = slice(r0 + b * HEAD, r0 + (b + 1) * HEAD)
                    au = proj_ref[blk, hs].astype(F32)
                    av = proj_ref[blk, vs].astype(F32)
                    u, u_grad = _gelu_and_grad(au)
                    v, v_grad = _gelu_and_grad(av)
                    rv = _rms(v)
                    vhat = v * rv
                    vn = (vhat * gain).astype(BF16)
                    dout = dcat[b * HEAD:(b + 1) * HEAD, hs]
                    z = _dot(wm_ref[h], vn) + bs_ref[h]
                    dz = dout * u
                    dbz_ref[h] += dz
                    dzb = dz.astype(BF16)
                    dwm_ref[h] += _dot_nt(dzb, vn)
                    dvn = _dot(wmt_ref[h], dzb)
                    dg_ref[h:h + 1, :] += _colsum(dvn * vhat)
                    dvhat = dvn * gain
                    dv = rv * (dvhat - vhat * _rowmean(dvhat * vhat))
                    dproj_ref[blk, hs] = ((dout * z) * u_grad).astype(BF16)
                    dproj_ref[blk, vs] = (dv * v_grad).astype(BF16)
            firsts = []
            for g in range(len(POOL_WINDOWS)):
                gs = slice(g * HEAD, (g + 1) * HEAD)
                pcols = slice(2 * A_WIDTH + g * HEAD, 2 * A_WIDTH + (g + 1) * HEAD)
                p = proj_ref[rows, pcols].astype(F32)
                pb = _pool_fwd(p, halo[:, gs], g, t_glob).astype(BF16)
                dyb = dcat[:, A_WIDTH + g * HEAD:A_WIDTH + (g + 1) * HEAD]
                dps_ref[0:1, gs] += _colsum(dyb * _dot(pb, wp_ref[g]))
                dyl = (dyb * ps_ref[0:1, gs]).astype(BF16)
                dwp_ref[g] += _dot_tn(pb, dyl)
                dpooled = _dot_nt(dyl, wp_ref[g])
                cnt = jnp.minimum(t_glob + 1, POOL_WINDOWS[g]).astype(F32)
                q = dpooled / cnt
                acc = jnp.concatenate([q, later[:, gs]], axis=0)
                for step in range(g + 1):
                    acc = acc + pltpu.roll(acc, ext - (1 << step), 0)
                dproj_ref[rows, pcols] = (acc[:sub] - dpooled).astype(BF16)
                firsts.append(q[0:POOL_HALO])
            return jnp.concatenate(firsts, axis=1)

        def finish(s, dh1):
            rows = slice(s * sub, (s + 1) * sub)
            x = x_ref[rows, :]
            r1 = _rms(x)
            xn = x * r1
            pre = _vrow(vec_ref, V_PRE_MIX)
            one_sc = 1.0 + _vrow(vec_ref, V_SC_M)
            acc_ref[M_DSHM:M_DSHM + 1, :] += _colsum(dh1)
            acc_ref[M_DSCM:M_DSCM + 1, :] += _colsum(dh1 * (xn * pre))
            acc_ref[M_DPREMIX:M_DPREMIX + 1, :] += _colsum(dh1 * xn * one_sc)
            dxn = dh1 * pre * one_sc
            gx_ref[rows, :] = dx2_ref[rows, :] + r1 * (dxn - xn * _rowmean(dxn * xn))

        nxt = cotangent(nsub - 1)
        later = carry_ref[...]
        dh1_prev = None
        for s in reversed(range(nsub)):
            dcat = nxt
            if s > 0:
                nxt = cotangent(s - 1)
                halo = proj_ref[s * sub - POOL_HALO:s * sub, 2 * A_WIDTH:].astype(F32)
            else:
                halo = jnp.where(tile > 0, projh_ref[:, 2 * A_WIDTH:].astype(F32), 0.0)
            later = back(s, dcat, halo, later)
            dh1 = _dot_nt(dproj_ref[s * sub:(s + 1) * sub, :], win_ref[...])
            if dh1_prev is not None:
                finish(s + 1, dh1_prev)
            dh1_prev = dh1
        carry_ref[...] = later
        finish(0, dh1_prev)

        @pl.when(i == nt - 1)
        def _():
            dwm_out[...] = dwm_ref[...].astype(BF16)
            dwp_out[...] = dwp_ref[...].astype(BF16)
            db_ref[...] = jnp.zeros_like(db_ref)
            for h in range(N_HEADS):
                db_ref[h:h + 1, :] = jnp.sum(dbz_ref[h].T, axis=0, keepdims=True)

    rev = lambda i: (nt - 1 - i, 0)
    halo_map = lambda i: (jnp.maximum((nt - 1 - i) * per - 1, 0), 0)
    hshape = (N_HEADS, HEAD, HEAD)
    return _call(
        body, name="mix_bwd", grid=(nt,),
        in_specs=[_tiled((ts, d), rev), _tiled((ts, d), rev), _tiled((ts, d), rev), _tiled((ts, n_proj), rev),
                  _tiled((POOL_HALO, n_proj), halo_map), _whole(vecs.shape), _resident(w_in_b.shape),
                  _resident(w_out_b.shape), _whole(sgu_g.shape), _whole(wm_b.shape), _whole(wmt_b.shape),
                  _whole(bsb.shape), _whole(wp_b.shape), _whole(ps.shape)],
        out_specs=[_tiled((ts, d), rev), _tiled((ts, n_proj), rev), _whole((8, d)), _whole(hshape), _whole(hshape),
                   _whole((8, HEAD)), _whole((8, HEAD)), _whole((8, A_WIDTH))],
        out_shape=[jax.ShapeDtypeStruct((s_len, d), F32), jax.ShapeDtypeStruct((s_len, n_proj), BF16),
                   jax.ShapeDtypeStruct((8, d), F32), jax.ShapeDtypeStruct(hshape, BF16),
                   jax.ShapeDtypeStruct(hshape, BF16), jax.ShapeDtypeStruct((8, HEAD), F32),
                   jax.ShapeDtypeStruct((8, HEAD), F32), jax.ShapeDtypeStruct((8, A_WIDTH), F32)],
        scratch_shapes=[pltpu.VMEM((POOL_HALO, A_WIDTH), F32), pltpu.VMEM(hshape, F32), pltpu.VMEM(hshape, F32),
                        pltpu.VMEM(hshape, F32)],
        args=(dmixed, dx2, x, proj, proj, vecs, w_in_b, w_out_b, sgu_g, wm_b, wmt_b, bsb, wp_b, ps))


def _wgrad(a, b, tn, ts, name, rider=None):
    s_len, m = a.shape
    n = b.shape[1]
    ts = min(ts, s_len)

    def body(a_ref, b_ref, o_ref):
        @pl.when(pl.program_id(1) == 0)
        def _():
            o_ref[...] = jnp.zeros_like(o_ref)

        o_ref[...] += _dot_tn(a_ref[...], b_ref[...])

    (g,), r_out = _call(
        body, name=name, grid=(n // tn, s_len // ts), rider=rider,
        in_specs=[pl.BlockSpec((ts, m), lambda j, s: (s, 0)), pl.BlockSpec((ts, tn), lambda j, s: (s, j))],
        out_specs=[pl.BlockSpec((m, tn), lambda j, s: (0, j))],
        out_shape=[jax.ShapeDtypeStruct((m, n), F32)], scratch_shapes=[], args=(a, b))
    return g, r_out


def _wgrad_pair(a1, b1, a2, b2, after, ts, name):
    s_len = a1.shape[0]
    ts = min(ts, s_len)
    shapes = [(a1.shape[1], b1.shape[1]), (a2.shape[1], b2.shape[1])]

    def body(a1_ref, b1_ref, a2_ref, b2_ref, after_ref, o1_ref, o2_ref):
        @pl.when(pl.program_id(0) == 0)
        def _():
            o1_ref[...] = jnp.zeros_like(o1_ref)
            o2_ref[...] = jnp.zeros_like(o2_ref)

        o1_ref[...] += _dot_tn(a1_ref[...], b1_ref[...])
        o2_ref[...] += _dot_tn(a2_ref[...], b2_ref[...])

    row = lambda s: (s, 0)
    return _call(
        body, name=name, grid=(s_len // ts,),
        in_specs=[pl.BlockSpec((ts, t.shape[1]), row) for t in (a1, b1, a2, b2)] + [_whole(after.shape)],
        out_specs=[_whole(sh) for sh in shapes],
        out_shape=[jax.ShapeDtypeStruct(sh, F32) for sh in shapes], scratch_shapes=[],
        args=(a1, b1, a2, b2, after))[0]


def _adamw_big(g, w, m, v, name):
    r, cdim = g.shape
    tr = r
    while tr * cdim * 4 > (3 << 19) and tr % 16 == 0:
        tr //= 2

    def body(g_ref, w_ref, m_ref, v_ref, d_ref, nm_ref, nv_ref):
        delta, m2, v2 = _adamw_math(w_ref[0], g_ref[...], m_ref[0], v_ref[0])
        d_ref[0] = delta
        nm_ref[0] = m2
        nv_ref[0] = v2

    s3 = pl.BlockSpec((1, tr, cdim), lambda i: (0, i, 0))
    return pl.pallas_call(
        body, name=name, grid=(r // tr,),
        in_specs=[pl.BlockSpec((tr, cdim), lambda i: (i, 0)), s3, s3, s3],
        out_specs=[s3, s3, s3],
        out_shape=[jax.ShapeDtypeStruct(w.shape, F32)] * 3,
        compiler_params=pltpu.CompilerParams(dimension_semantics=("parallel",), vmem_limit_bytes=VMEM_LIMIT_BYTES),
    )(g, w, m, v)


def _wada_update(sct, gm, w, m, v):
    _, r, cdim = w.shape
    tr = 256
    kp = sct.shape[1]

    def body(s_ref, g_ref, w_ref, m_ref, v_ref, gw_ref, d_ref, nm_ref, nv_ref):
        g = _dot(s_ref[...], g_ref[...])
        gw_ref[0] = g
        delta, m2, v2 = _adamw_math(w_ref[0], g, m_ref[0], v_ref[0])
        d_ref[0] = delta
        nm_ref[0] = m2
        nv_ref[0] = v2

    s3 = pl.BlockSpec((1, tr, cdim), lambda i: (0, i, 0))
    return pl.pallas_call(
        body, name="wada_update", grid=(r // tr,),
        in_specs=[pl.BlockSpec((tr, kp), lambda i: (i, 0)), _whole(gm.shape), s3, s3, s3],
        out_specs=[s3, s3, s3, s3],
        out_shape=[jax.ShapeDtypeStruct(w.shape, F32)] * 4,
        compiler_params=pltpu.CompilerParams(dimension_semantics=("parallel",), vmem_limit_bytes=VMEM_LIMIT_BYTES),
    )(sct, gm, w, m, v)


def _small_update(g1, g2, g2s, gwm, gwp, gbz, gsg, gps, params):
    names = ["b_ada", "pre_mix_g", "post_mix_g", "sgu_norm_g", "w_spatial", "b_spatial", "w_pool", "pool_scale",
             "pre_ffn_g", "post_ffn_g", "conv_w", "conv_b"]
    d = g1.shape[2]
    flat_in = [g1, g2, g2s, gwm, gwp, gbz, gsg, gps]
    n_g = len(flat_in)
    for nm in names:
        flat_in += list(params[nm])

    def body(*refs):
        g1_ref, g2_ref, g2s_ref, gwm_ref, gwp_ref, gbz_ref, gsg_ref, gps_ref = refs[:n_g]
        wmv = refs[n_g:n_g + 3 * len(names)]
        loss_ref = refs[n_g + 3 * len(names)]
        outs = refs[n_g + 3 * len(names) + 1:]

        def dsum(ref, idx):
            acc = ref[(0,) + idx].astype(F32)
            for dev in range(1, N_DEV):
                acc = acc + ref[(dev,) + idx].astype(F32)
            return acc

        def apply(pi, g, widx, oidx):
            w_ref, m_ref, v_ref = wmv[3 * pi:3 * pi + 3]
            g_ref, d_ref, nm_ref, nv_ref = outs[4 * pi:4 * pi + 4]
            delta, m2, v2 = _adamw_math(w_ref[widx], g, m_ref[widx], v_ref[widx])
            g_ref[oidx] = g
            d_ref[oidx] = delta
            nm_ref[oidx] = m2
            nv_ref[oidx] = v2

        def row1(base, r):
            return (slice(base + r, base + r + 1), slice(None))

        tot = dsum(g1_ref, row1(0, F_LOSS))
        loss_ref[...] = jnp.zeros(loss_ref.shape, F32) + jnp.sum(tot) * (0.5 / d)
        mod_rows = [row1(16, M_DSHM), row1(16, M_DSCM), row1(8, B_DGM), row1(8, B_DSHF), row1(8, B_DSCF),
                    row1(0, F_DGF)]
        for j, rr in enumerate(mod_rows):
            cs = (slice(None), slice(j * d, (j + 1) * d))
            apply(0, dsum(g1_ref, rr), cs, cs)
        full = (slice(None), slice(None))
        apply(1, dsum(g1_ref, row1(16, M_DPREMIX)), full, full)
        apply(2, dsum(g1_ref, row1(8, B_DPOSTMIX)), full, full)
        apply(3, dsum(gsg_ref, (slice(0, N_HEADS), slice(None))), (0,), (0,))
        pos_i = lax.broadcasted_iota(jnp.int32, (HEAD, HEAD), 0)
        pos_j = lax.broadcasted_iota(jnp.int32, (HEAD, HEAD), 1)
        causal = (pos_j // CHUNK) <= (pos_i // CHUNK)
        for h in range(N_HEADS):
            blk = (slice(h * HEAD, (h + 1) * HEAD), slice(None))
            apply(4, jnp.where(causal, dsum(gwm_ref, blk), 0.0), (0, h), (0, h))
            apply(5, dsum(gbz_ref, (slice(h, h + 1), slice(None))), (0, slice(h, h + 1)), (0, slice(h, h + 1)))
            apply(6, dsum(gwp_ref, blk), (0, h), (0, h))
        apply(7, dsum(gps_ref, (slice(0, 1), slice(None))), full, full)
        apply(8, dsum(g1_ref, row1(8, B_DPREFFN)), full, full)
        apply(9, dsum(g1_ref, row1(0, F_DPOSTFFN)), full, full)
        apply(10, dsum(g2s_ref, (slice(C_DCW, C_DCW + 3), slice(None))), (0,), (0,))
        apply(11, dsum(g2_ref, (slice(C_DCB, C_DCB + 1), slice(None))), full, full)

    out_shape = [jax.ShapeDtypeStruct((8, HEAD), F32)]
    for nm in names:
        out_shape += [jax.ShapeDtypeStruct(params[nm][0].shape, F32)] * 4
    res = pl.pallas_call(
        body, name="small_update", out_shape=out_shape,
        compiler_params=pltpu.CompilerParams(vmem_limit_bytes=VMEM_LIMIT_BYTES),
    )(*flat_in)
    out = {nm: tuple(res[1 + 4 * i:5 + 4 * i]) for i, nm in enumerate(names)}
    return res[0], out


def kernel(x, c, w_ada, b_ada, pre_mix_g, post_mix_g, w_in, sgu_norm_g, w_spatial, b_spatial, w_pool, pool_scale, w_out, pre_ffn_g, post_ffn_g, w_up, conv_w, conv_b, w_down, loss_target, m_w_ada, m_b_ada, m_pre_mix_g, m_post_mix_g, m_w_in, m_sgu_norm_g, m_w_spatial, m_b_spatial, m_w_pool, m_pool_scale, m_w_out, m_pre_ffn_g, m_post_ffn_g, m_w_up, m_conv_w, m_conv_b, m_w_down, v_w_ada, v_b_ada, v_pre_mix_g, v_post_mix_g, v_w_in, v_sgu_norm_g, v_w_spatial, v_b_spatial, v_w_pool, v_pool_scale, v_w_out, v_pre_ffn_g, v_post_ffn_g, v_w_up, v_conv_w, v_conv_b, v_w_down):
    xi, yi, ci = _mesh_pos()
    k_me = 2 * xi + yi
    dev = 2 * k_me + ci
    s_len, d = x.shape[1], x.shape[2]
    x2d = x[0]
    tgt = loss_target[0]
    ff2 = conv_b.shape[1]
    n_ada = w_ada.shape[2]
    n_cw = conv_w.shape[2]

    k_idx = k_me.reshape(1).astype(jnp.int32)
    flags4 = (True, False, True, False)
    (w_in_s, w_out_s, w_up_s, w_down_s), (w_in_f, w_out_f, w_up_f, w_down_f) = _cast_bf16(
        [w_in[0], w_out[0], w_up[0], w_down[0]], flags4, k_idx)
    mix_flags = (True, False)

    def place(t):
        return lax.dynamic_update_slice(jnp.zeros((N_DEV * t.shape[0], t.shape[1]), t.dtype), t,
                                        (dev * t.shape[0], 0))

    cw_blk = jnp.concatenate([conv_w[0], jnp.zeros((5, n_cw), F32)], axis=0)
    fly_c = _split_begin([place(c.reshape(8, d // 8)), place(cw_blk)], 6, _small_ici_copies, k_idx,
                         "c_gather_begin")
    fly_mix = _gather_begin([w_in_s, w_out_s], [w_in_f, w_out_f], mix_flags, fly_c[3], "gather_begin_mix")
    c_all, cw_all = _small_finish(_split_end(fly_c, _small_ici_copies, fly_mix[3], "c_gather_end"),
                                  "c_gather_finish")
    c_all = c_all.reshape(N_DEV, 8, d // 8).reshape(N_DEV, d)
    cw_full = jnp.concatenate([cw_all[16 * k:16 * k + 8] for k in range(N_CHIP)], axis=1)
    cvec = jnp.concatenate([cw_full[0:3], conv_b, jnp.zeros((4, ff2), F32)], axis=0)
    b_shard = lax.dynamic_slice_in_dim(b_ada, k_me * n_ada, n_ada, axis=1)
    mod_k, sc_all = _mod_shard(c_all, w_ada[0], b_shard)
    (mod_g,) = _run_rider(_allgather_rider([mod_k]), "gather_mod")
    mod_all = jnp.concatenate([mod_g[16 * k:16 * k + 8] for k in range(N_CHIP)], axis=1)
    mod_me = lax.dynamic_slice_in_dim(mod_all, dev, 1, axis=0).reshape(6, d)
    vecs = jnp.concatenate([mod_me, pre_mix_g, post_mix_g, pre_ffn_g, post_ffn_g, jnp.zeros((6, d), F32)], axis=0)

    fly_ffn = _gather_begin([w_up_s, w_down_s], [w_up_f, w_down_f], mix_flags, mod_g, "gather_begin_ffn")
    w_in_b, w_out_b = _gather_finish(_gather_end(fly_mix, mix_flags, fly_ffn[3], "gather_end_mix"), mix_flags,
                                     "gather_finish_mix")

    pos = jnp.arange(HEAD)
    causal = (pos[None, :] // CHUNK) <= (pos[:, None] // CHUNK)
    wm = jnp.where(causal[None], w_spatial[0], 0.0)
    wm_b = wm.astype(BF16)
    wmt_b = jnp.swapaxes(wm, 1, 2).astype(BF16)
    bsb = jnp.broadcast_to(b_spatial[0][:, :, None], (N_HEADS, HEAD, HEAD))
    wp_b = w_pool[0].astype(BF16)
    sgu_g = jnp.concatenate([sgu_norm_g[0], jnp.zeros((4, HEAD), F32)], axis=0)
    ps = jnp.concatenate([pool_scale, jnp.zeros((7, A_WIDTH), F32)], axis=0)

    pmats = _perm_mats(FFN_TS)
    h1, proj, cat, mixed, x2, h2p = _mix_fwd(x2d, vecs, w_in_b, w_out_b, sgu_g, wm_b, bsb, wp_b, ps, pmats,
                                             ts=MIX_TS)[0]
    w_up_b, w_down_b = _gather_finish(_gather_end(fly_ffn, mix_flags, h2p, "gather_end_ffn"), mix_flags,
                                      "gather_finish_ffn")
    up, yv, act, dy, dfp, acc_f = _ffn_fwd(h2p, x2, tgt, w_up_b, w_down_b, cvec, vecs, pmats, ts=FFN_TS)

    c_idx = ci.reshape(1).astype(jnp.int32)
    g_w_down, _ = _wgrad(act, dfp, d, WGRAD_TS_WIDE, "wgrad_down")
    (dup, dx2, dmixed, acc_c, acc_b), (land_down,) = _ffn_bwd(
        dfp, up, yv, x2, dy, mixed, w_up_b, w_down_b, cvec, vecs, pmats, ts=FFN_TS,
        rider=_sibling_rider([g_w_down], (False,)))
    (part_down,) = _sum_with_sibling([g_w_down], [land_down], (False,), c_idx, "pair_sum_down")
    g_w_up, (chips_down,) = _wgrad(h2p, dup, ff2 // 2, WGRAD_TS, "wgrad_up", rider=_chips_rider([part_down]))
    fly_up = _sibling_begin(g_w_up, True, chips_down, "sibling_begin_up")
    gx, dproj, acc_m, dwm, dwp, dbz, dsg, dps = _mix_bwd(
        dmixed, dx2, x2d, proj, vecs + fly_up[3][0:1, 0:1], w_in_b, w_out_b, sgu_g, wm_b, wmt_b, bsb, wp_b, ps,
        ts=MIX_TS)[0]
    g_w_up, land_up = _sibling_end(fly_up, True, dproj, "sibling_end_up")
    (part_up,) = _sum_with_sibling([g_w_up], [land_up], (True,), c_idx, "pair_sum_up")
    fly_chips_up = _chips_begin([part_up], c_idx, "chips_begin_up")
    g1 = jnp.concatenate([acc_f, acc_b, acc_m], axis=0)
    hflat = (N_HEADS * HEAD, HEAD)
    small_bufs = [place(t) for t in (g1, acc_c, dwm.reshape(hflat), dwp.reshape(hflat), dbz, dsg, dps)]
    fly_small = _split_begin(small_bufs, 3 * len(small_bufs), _small_ici_copies, fly_chips_up[3],
                             "small_gather_begin")
    g_w_out, g_w_in = _wgrad_pair(cat, dmixed, h1, dproj, fly_small[3], WGRAD_TS_WIDE, "wgrad_mix")
    land_mix = _run_rider(_sibling_rider([g_w_in, g_w_out], (True, False)), "reduce_to_sibling")
    parts_mix = _sum_with_sibling([g_w_in, g_w_out], land_mix, (True, False), c_idx, "pair_sum_mix")
    (part_up,), (chips_up,) = _chips_end(fly_chips_up, parts_mix[0], "chips_end_up")
    fly_chips_mix = _chips_begin(parts_mix, chips_up, "chips_begin_mix")

    def adamw_of(names, reduced):
        res = {}
        for nm, red in zip(names, reduced):
            w, m, v = big_wmv[nm]
            g = red.reshape(w.shape[1], w.shape[2])
            res[nm] = (g.reshape(w.shape),) + tuple(_adamw_big(g, w, m, v, "adamw_" + nm))
        return res

    big_wmv = {"w_in": (w_in, m_w_in, v_w_in), "w_out": (w_out, m_w_out, v_w_out),
               "w_up": (w_up, m_w_up, v_w_up), "w_down": (w_down, m_w_down, v_w_down)}
    big = adamw_of(("w_up", "w_down"), _sum_chips_and_share([chips_up, chips_down], [part_up, part_down],
                                                           fly_chips_mix[3], "sum_share_ffn"))

    gathered = _small_finish(_split_end(fly_small, _small_ici_copies, big["w_down"][1], "small_gather_end"),
                             "small_gather_finish")
    g1a, g2a, gwm, gwp, gbz, gsg, gps = [t.reshape((N_DEV, t.shape[0] // N_DEV, t.shape[1])) for t in gathered]
    g2s = lax.dynamic_slice_in_dim(g2a, k_me * n_cw, n_cw, axis=2)
    params = {
        "b_ada": (b_ada, m_b_ada, v_b_ada), "pre_mix_g": (pre_mix_g, m_pre_mix_g, v_pre_mix_g),
        "post_mix_g": (post_mix_g, m_post_mix_g, v_post_mix_g),
        "sgu_norm_g": (sgu_norm_g, m_sgu_norm_g, v_sgu_norm_g), "w_spatial": (w_spatial, m_w_spatial, v_w_spatial),
        "b_spatial": (b_spatial, m_b_spatial, v_b_spatial), "w_pool": (w_pool, m_w_pool, v_w_pool),
        "pool_scale": (pool_scale, m_pool_scale, v_pool_scale), "pre_ffn_g": (pre_ffn_g, m_pre_ffn_g, v_pre_ffn_g),
        "post_ffn_g": (post_ffn_g, m_post_ffn_g, v_post_ffn_g), "conv_w": (conv_w, m_conv_w, v_conv_w),
        "conv_b": (conv_b, m_conv_b, v_conv_b),
    }
    loss_slab, small = _small_update(g1a, g2a, g2s, gwm, gwp, gbz, gsg, gps, params)

    gmod_all = jnp.concatenate(
        [g1a[:, 16 + M_DSHM], g1a[:, 16 + M_DSCM], g1a[:, 8 + B_DGM], g1a[:, 8 + B_DSHF], g1a[:, 8 + B_DSCF],
         g1a[:, F_DGF]], axis=1)
    gm = lax.dynamic_slice_in_dim(gmod_all, k_me * n_ada, n_ada, axis=1)
    gm = jnp.concatenate([gm, jnp.zeros((HEAD - N_DEV, n_ada), F32)], axis=0)
    sct = jnp.concatenate([sc_all.T, jnp.zeros((d, HEAD - N_DEV), F32)], axis=1)
    ada = tuple(_wada_update(sct, gm, w_ada, m_w_ada, v_w_ada))

    parts_mix, chips_mix = _chips_end(fly_chips_mix, ada[1], "chips_end_mix")
    big.update(adamw_of(("w_in", "w_out"), _sum_chips_and_share(chips_mix, parts_mix, loss_slab, "sum_share_mix")))

    everything = dict(small)
    everything.update(big)
    everything["w_ada"] = ada
    order = ["w_ada", "b_ada", "pre_mix_g", "post_mix_g", "w_in", "sgu_norm_g", "w_spatial", "b_spatial", "w_pool",
             "pool_scale", "w_out", "pre_ffn_g", "post_ffn_g", "w_up", "conv_w", "conv_b", "w_down"]
    outs = [loss_slab[0, 0], gx.reshape(x.shape)]
    for j in range(4):
        outs += [everything[nm][j] for nm in order]
    return tuple(outs)
```

```python
import functools

import jax
import jax.numpy as jnp
from jax import lax
from jax.experimental import pallas as pl
from jax.experimental.pallas import tpu as pltpu

F32 = jnp.float32
BF16 = jnp.bfloat16
MESH = pl.DeviceIdType.MESH

EPS = 1e-6
HEAD = 128
N_HEADS = 4
A_WIDTH = N_HEADS * HEAD
CHUNK = 64
POOL_WINDOWS = (2, 4, 8, 16)
POOL_HALO = 16
SUBLANES = 8
CONV_KEEP = 2 * SUBLANES
FFN_TS = 256
MIX_TS = 512
WGRAD_TS = 2048
WGRAD_TS_WIDE = 1024

ADAM_LR = 0.001
ADAM_B1 = 0.9
ADAM_B2 = 0.999
ADAM_EPS = 1e-08
ADAM_WD = 0.01
ADAM_STEP = 10

VMEM_LIMIT_BYTES = 58 * 1024 * 1024
N_DEV = 8
N_CHIP = 4


def _dot(a, b):
    return jnp.dot(a, b, preferred_element_type=F32)


def _dot_nt(a, b):
    return lax.dot_general(a, b, (((1,), (1,)), ((), ())), preferred_element_type=F32)


def _dot_tn(a, b):
    return lax.dot_general(a, b, (((0,), (0,)), ((), ())), preferred_element_type=F32)


GELU_C0 = 0.7978845608028654
GELU_C1 = GELU_C0 * 0.044715


def _gelu(x):
    return x * (0.5 + 0.5 * jnp.tanh(x * (GELU_C0 + GELU_C1 * (x * x))))


def _gelu_and_grad(x):
    x2 = x * x
    t = jnp.tanh(x * (GELU_C0 + GELU_C1 * x2))
    half = 0.5 + 0.5 * t
    grad = half + (x * (0.5 - 0.5 * (t * t))) * (GELU_C0 + (3.0 * GELU_C1) * x2)
    return x * half, grad


def _sigmoid(x):
    return 1.0 / (1.0 + jnp.exp(-x))


def _rms(x):
    return lax.rsqrt(jnp.mean(x * x, axis=-1, keepdims=True) + EPS)


def _colsum(x):
    return jnp.sum(x, axis=0, keepdims=True)


def _rowmean(x):
    return jnp.mean(x, axis=-1, keepdims=True)


def _tiled(shape, index_map):
    return pl.BlockSpec(shape, index_map)


def _resident(shape):
    nd = len(shape)
    return pl.BlockSpec(shape, lambda *_: (0,) * nd, pipeline_mode=pl.Buffered(1))


def _whole(shape):
    nd = len(shape)
    return pl.BlockSpec(shape, lambda *_: (0,) * nd)


def _seq_params():
    return pltpu.CompilerParams(dimension_semantics=("arbitrary",), vmem_limit_bytes=VMEM_LIMIT_BYTES)


def _ff_chunks(f, width=768):
    out, o = [], 0
    while o < f:
        w = min(width, f - o)
        out.append((o, w))
        o += w
    return out


def _pool_fwd(p, halo, g, t_glob):
    ext = jnp.concatenate([halo, p], axis=0)
    s = ext
    for step in range(g + 1):
        s = s + pltpu.roll(s, 1 << step, 0)
    cnt = jnp.minimum(t_glob + 1, POOL_WINDOWS[g]).astype(F32)
    return s[POOL_HALO:] / cnt - p


def _adamw_math(w, g, m, v):
    m = ADAM_B1 * m + (1.0 - ADAM_B1) * g
    v = ADAM_B2 * v + (1.0 - ADAM_B2) * (g * g)
    m_hat = m / (1.0 - ADAM_B1 ** ADAM_STEP)
    v_hat = v / (1.0 - ADAM_B2 ** ADAM_STEP)
    delta = -ADAM_LR * (m_hat / (jnp.sqrt(v_hat) + ADAM_EPS) + ADAM_WD * w)
    return delta, m, v


def _mesh_pos():
    return lax.axis_index("x"), lax.axis_index("y"), lax.axis_index("c")


class _Rider:
    def __init__(self, inputs, out_shape, sems, start, finish):
        self.inputs, self.out_shape, self.sems = list(inputs), list(out_shape), list(sems)
        self.start, self.finish = start, finish


def _call(body, *, name, grid, in_specs, out_specs, out_shape, scratch_shapes, args, rider=None):
    params = pltpu.CompilerParams(dimension_semantics=("arbitrary",) * len(grid), vmem_limit_bytes=VMEM_LIMIT_BYTES)
    if rider is None:
        res = pl.pallas_call(body, name=name, grid=grid, in_specs=in_specs, out_specs=out_specs, out_shape=out_shape,
                             scratch_shapes=scratch_shapes, compiler_params=params)(*args)
        return tuple(res), ()
    cuts = [len(in_specs), len(rider.inputs), len(out_specs), len(rider.out_shape), len(scratch_shapes),
            len(rider.sems)]

    def hosted(*refs):
        groups, a = [], 0
        for cnt in cuts:
            groups.append(refs[a:a + cnt])
            a += cnt
        ins, r_in, outs, r_out, scr, r_sem = groups
        first = functools.reduce(jnp.logical_and, [pl.program_id(k) == 0 for k in range(len(grid))])
        last = functools.reduce(jnp.logical_and, [pl.program_id(k) == grid[k] - 1 for k in range(len(grid))])

        @pl.when(first)
        def _():
            rider.start(r_in, r_out, r_sem)

        body(*ins, *outs, *scr)

        @pl.when(last)
        def _():
            rider.finish(r_in, r_out, r_sem)

    anyspec = pl.BlockSpec(memory_space=pl.ANY)
    res = pl.pallas_call(
        hosted, name=name, grid=grid,
        in_specs=list(in_specs) + [anyspec] * cuts[1], out_specs=list(out_specs) + [anyspec] * cuts[3],
        out_shape=list(out_shape) + rider.out_shape, scratch_shapes=list(scratch_shapes) + rider.sems,
        compiler_params=params)(*args, *rider.inputs)
    return tuple(res[:cuts[2]]), tuple(res[cuts[2]:])


def _run_rider(rider, name):
    n_in, n_out = len(rider.inputs), len(rider.out_shape)

    def body(*refs):
        r_in, r_out, r_sem = refs[:n_in], refs[n_in:n_in + n_out], refs[n_in + n_out:]
        rider.start(r_in, r_out, r_sem)
        rider.finish(r_in, r_out, r_sem)

    anyspec = pl.BlockSpec(memory_space=pl.ANY)
    return pl.pallas_call(body, name=name, out_shape=rider.out_shape, in_specs=[anyspec] * n_in,
                          out_specs=[anyspec] * n_out, scratch_shapes=rider.sems)(*rider.inputs)


def _allgather_rider(arrs):
    n = len(arrs)

    def plan(ins, outs, sems):
        send_sems, recv_sems, local_sems = sems
        x, y, c = _mesh_pos()
        me, sibling = (x, y, c), (x, y, 1 - c)
        chips = [(1 - x, y), (x, 1 - y), (1 - x, 1 - y)]

        def rows(a, px, py, pc):
            r = ins[a].shape[0]
            return outs[a].at[pl.ds(pl.multiple_of((4 * px + 2 * py + pc) * r, 8), r), :]

        def copy(a, k, block, to, src=None):
            return pltpu.make_async_remote_copy(
                src_ref=rows(a, *block) if src is None else src, dst_ref=rows(a, *block),
                send_sem=send_sems.at[a * 7 + k], recv_sem=recv_sems.at[a * 7 + k],
                device_id=to, device_id_type=MESH)

        local = [pltpu.make_async_copy(ins[a], rows(a, *me), local_sems.at[a]) for a in range(n)]
        first = []
        for a in range(n):
            first.append(copy(a, 0, me, sibling, src=ins[a]))
            first += [copy(a, 1 + j, me, (*chip, c), src=ins[a]) for j, chip in enumerate(chips)]
        return c, me, sibling, chips, copy, local, first

    def start(ins, outs, sems):
        *_, local, first = plan(ins, outs, sems)
        for cp in local + first:
            cp.start()

    def finish(ins, outs, sems):
        c, me, sibling, chips, copy, local, first = plan(ins, outs, sems)
        passed = []
        for a in range(n):
            for j, chip in enumerate(chips):
                copy(a, 1 + j, (*chip, c), me).wait_recv()
                fwd = copy(a, 4 + j, (*chip, c), sibling)
                fwd.start()
                passed.append(fwd)
        for a in range(n):
            copy(a, 0, sibling, me).wait_recv()
            for j, chip in enumerate(chips):
                copy(a, 4 + j, (*chip, 1 - c), me).wait_recv()
        for cp in first + passed:
            cp.wait_send()
        for mine in local:
            mine.wait()

    return _Rider(arrs, [jax.ShapeDtypeStruct((N_DEV * a.shape[0], a.shape[1]), a.dtype) for a in arrs],
                  [pltpu.SemaphoreType.DMA((7 * n,)), pltpu.SemaphoreType.DMA((7 * n,)),
                   pltpu.SemaphoreType.DMA((n,))], start, finish)


def _piece(ref, col_sharded, k, h):
    m, n = ref.shape
    if col_sharded:
        mh, nc = m // 2, n // N_CHIP
        return ref.at[pl.ds(pl.multiple_of(h * mh, 16), mh), pl.ds(pl.multiple_of(k * nc, 128), nc)]
    rp = m // (2 * N_CHIP)
    return ref.at[pl.ds(pl.multiple_of((2 * k + h) * rp, 16), rp), :]


def _piece_shape(shape, col_sharded):
    m, n = shape
    return (m // 2, n // N_CHIP) if col_sharded else (m // (2 * N_CHIP), n)


def _cast_bf16(arrs, col_flags, k_idx, name, after=None):
    n = len(arrs)
    extra = [] if after is None else [after]

    def body(k_ref, *refs):
        outs = refs[n + len(extra):]
        for a in range(n):
            val = refs[a][...].astype(BF16)
            outs[a][...] = val
            outs[n + a][...] = val

    whole = [pl.BlockSpec(a.shape, lambda i, k_ref: (0, 0)) for a in arrs]
    window = [pl.BlockSpec(a.shape, (lambda i, k_ref: (0, k_ref[0])) if col else (lambda i, k_ref: (k_ref[0], 0)))
              for a, col in zip(arrs, col_flags)]
    res = pl.pallas_call(
        body, name=name,
        grid_spec=pltpu.PrefetchScalarGridSpec(
            num_scalar_prefetch=1, grid=(1,),
            in_specs=whole + [pl.BlockSpec(t.shape, lambda i, k_ref: (0, 0)) for t in extra],
            out_specs=whole + window),
        out_shape=[jax.ShapeDtypeStruct(a.shape, BF16) for a in arrs]
        + [jax.ShapeDtypeStruct(fs, BF16) for fs in _full_shapes(arrs, col_flags)],
        compiler_params=pltpu.CompilerParams(vmem_limit_bytes=VMEM_LIMIT_BYTES))(k_idx, *arrs, *extra)
    return list(res[:n]), list(res[n:])


def _full_shapes(shards, col_flags):
    return [(s.shape[0], s.shape[1] * N_CHIP) if col else (s.shape[0] * N_CHIP, s.shape[1])
            for s, col in zip(shards, col_flags)]


def _ici_copies(shard_refs, full_refs, send_sems, recv_sems, col_flags):
    x, y, c = _mesh_pos()
    k_me = 2 * x + y
    copies = []
    for a, (s_ref, f_ref) in enumerate(zip(shard_refs, full_refs)):
        rows = s_ref.shape[0] // 2
        src = s_ref.at[pl.ds(pl.multiple_of(c * rows, 16), rows), :]
        for j, chip in enumerate([(1 - x, y), (x, 1 - y), (1 - x, 1 - y)]):
            copies.append(pltpu.make_async_remote_copy(
                src_ref=src, dst_ref=_piece(f_ref, col_flags[a], k_me, c),
                send_sem=send_sems.at[a * 3 + j], recv_sem=recv_sems.at[a * 3 + j],
                device_id=(*chip, c), device_id_type=MESH))
    return copies


def _split_begin(bufs, n_sems, make_copies, after, name):
    n = len(bufs)
    hbm = pl.BlockSpec(memory_space=pltpu.HBM)
    sem = pl.BlockSpec(memory_space=pltpu.SEMAPHORE)

    def body(*refs):
        for cp in make_copies(refs[:n], refs[n + 1], refs[n + 2]):
            cp.start()
        refs[-1][...] = jnp.zeros_like(refs[-1])

    args = [pltpu.with_memory_space_constraint(t, pltpu.HBM) for t in bufs]
    res = pl.pallas_call(
        body, name=name,
        out_shape=[pltpu.SemaphoreType.DMA((n_sems,)), pltpu.SemaphoreType.DMA((n_sems,))]
        + [pltpu.HBM(t.shape, t.dtype) for t in args] + [jax.ShapeDtypeStruct((8, HEAD), F32)],
        in_specs=[hbm] * n + [pl.BlockSpec(memory_space=pl.ANY)],
        out_specs=[sem, sem] + [hbm] * n + [pl.BlockSpec(memory_space=pltpu.VMEM)],
        input_output_aliases={i: 2 + i for i in range(n)},
        compiler_params=pltpu.CompilerParams(has_side_effects=pltpu.SideEffectType.DATAFLOW_SIDE_EFFECTING),
    )(*args, after)
    return res[0], res[1], list(res[2:2 + n]), res[-1]


def _split_end(handle, make_copies, after, name):
    send_sems, recv_sems, bufs, _ = handle
    n = len(bufs)
    hbm = pl.BlockSpec(memory_space=pltpu.HBM)
    sem = pl.BlockSpec(memory_space=pltpu.SEMAPHORE)

    def body(*refs):
        for cp in make_copies(refs[:n], refs[n], refs[n + 1]):
            cp.wait_send()
            cp.wait_recv()

    res = pl.pallas_call(
        body, name=name,
        out_shape=[pltpu.HBM(t.shape, t.dtype) for t in bufs],
        in_specs=[hbm] * n + [sem, sem, pl.BlockSpec(memory_space=pl.ANY)],
        out_specs=[hbm] * n,
        input_output_aliases={i: i for i in range(n)},
        compiler_params=pltpu.CompilerParams(has_side_effects=pltpu.SideEffectType.DATAFLOW_SIDE_EFFECTING),
    )(*bufs, send_sems, recv_sems, after)
    return list(res)


def _gather_copies(n, col_flags):
    return lambda refs, send_sems, recv_sems: _ici_copies(refs[:n], refs[n:], send_sems, recv_sems, col_flags)


def _gather_begin(shards, fulls, col_flags, after, name):
    n = len(shards)
    return _split_begin(list(shards) + list(fulls), 3 * n, _gather_copies(n, col_flags), after, name)


def _gather_end(handle, col_flags, after, name):
    n = len(handle[2]) // 2
    return _split_end(handle, _gather_copies(n, col_flags), after, name)[n:]


def _sibling_copies(col_flag):
    def make(refs, send_sems, recv_sems):
        grad_ref, land_ref = refs
        x, y, c = _mesh_pos()
        return [pltpu.make_async_remote_copy(
            src_ref=_piece(grad_ref, col_flag, k, 1 - c), dst_ref=land_ref.at[k],
            send_sem=send_sems.at[k], recv_sem=recv_sems.at[k],
            device_id=(x, y, 1 - c), device_id_type=MESH) for k in range(N_CHIP)]
    return make


def _chips_copies(n):
    def make(refs, send_sems, recv_sems):
        parts, landed = refs[:n], refs[n:]
        x, y, c = _mesh_pos()
        k_me = 2 * x + y
        copies = []
        for a in range(n):
            for j, chip in enumerate([(1 - x, y), (x, 1 - y), (1 - x, 1 - y)]):
                copies.append(pltpu.make_async_remote_copy(
                    src_ref=parts[a].at[2 * chip[0] + chip[1]], dst_ref=landed[a].at[k_me],
                    send_sem=send_sems.at[a * 3 + j], recv_sem=recv_sems.at[a * 3 + j],
                    device_id=(*chip, c), device_id_type=MESH))
        return copies
    return make


def _chips_begin(parts, after, name):
    landed = [lax.empty(p.shape, p.dtype) for p in parts]
    return _split_begin(list(parts) + landed, 3 * len(parts), _chips_copies(len(parts)), after, name)


def _chips_end(handle, after, name):
    n = len(handle[2]) // 2
    res = _split_end(handle, _chips_copies(n), after, name)
    return res[:n], res[n:]


def _small_rows(buf_ref, px, py, pc):
    r = buf_ref.shape[0] // N_DEV
    return buf_ref.at[pl.ds(pl.multiple_of((4 * px + 2 * py + pc) * r, 8), r), :]


def _small_ici_copies(refs, send_sems, recv_sems):
    x, y, c = _mesh_pos()
    copies = []
    for a, buf in enumerate(refs):
        mine = _small_rows(buf, x, y, c)
        for j, chip in enumerate([(1 - x, y), (x, 1 - y), (1 - x, 1 - y)]):
            copies.append(pltpu.make_async_remote_copy(
                src_ref=mine, dst_ref=mine, send_sem=send_sems.at[a * 3 + j], recv_sem=recv_sems.at[a * 3 + j],
                device_id=(*chip, c), device_id_type=MESH))
    return copies


def _small_finish(bufs, name):
    n = len(bufs)

    def body(*refs):
        buf_refs = refs[n:2 * n]
        send_sems, recv_sems = refs[2 * n:]
        x, y, c = _mesh_pos()
        owners = [(x, y), (1 - x, y), (x, 1 - y), (1 - x, 1 - y)]
        passed, arriving = [], []
        for a in range(n):
            for j, (px, py) in enumerate(owners):
                for pc, group in ((c, passed), (1 - c, arriving)):
                    rows = _small_rows(buf_refs[a], px, py, pc)
                    group.append(pltpu.make_async_remote_copy(
                        src_ref=rows, dst_ref=rows, send_sem=send_sems.at[a * 4 + j],
                        recv_sem=recv_sems.at[a * 4 + j], device_id=(x, y, 1 - c), device_id_type=MESH))
        for cp in passed:
            cp.start()
        for cp in arriving:
            cp.wait_recv()
        for cp in passed:
            cp.wait_send()

    anyspec = pl.BlockSpec(memory_space=pl.ANY)
    return pl.pallas_call(
        body, name=name, out_shape=[jax.ShapeDtypeStruct(b.shape, b.dtype) for b in bufs],
        in_specs=[anyspec] * n, out_specs=[anyspec] * n, input_output_aliases={a: a for a in range(n)},
        scratch_shapes=[pltpu.SemaphoreType.DMA((4 * n,)), pltpu.SemaphoreType.DMA((4 * n,))],
    )(*bufs)


def _sibling_begin(grad, col_flag, after, name):
    land = lax.empty((N_CHIP,) + _piece_shape(grad.shape, col_flag), grad.dtype)
    return _split_begin([grad, land], N_CHIP, _sibling_copies(col_flag), after, name)


def _sibling_end(handle, col_flag, after, name):
    return _split_end(handle, _sibling_copies(col_flag), after, name)


def _gather_finish(fulls, col_flags, name):
    n = len(fulls)

    def body(*refs):
        full_refs = refs[n:2 * n]
        send_sems, recv_sems = refs[2 * n:]
        x, y, c = _mesh_pos()
        passed, arriving = [], []
        for a in range(n):
            for j, chip in enumerate([(1 - x, y), (x, 1 - y), (1 - x, 1 - y)]):
                k_from = 2 * chip[0] + chip[1]
                for h, group in ((c, passed), (1 - c, arriving)):
                    win = _piece(full_refs[a], col_flags[a], k_from, h)
                    group.append(pltpu.make_async_remote_copy(
                        src_ref=win, dst_ref=win, send_sem=send_sems.at[a * 3 + j],
                        recv_sem=recv_sems.at[a * 3 + j], device_id=(x, y, 1 - c), device_id_type=MESH))
        for cp in passed:
            cp.start()
        for cp in arriving:
            cp.wait_recv()
        for cp in passed:
            cp.wait_send()

    anyspec = pl.BlockSpec(memory_space=pl.ANY)
    return pl.pallas_call(
        body, name=name,
        out_shape=[jax.ShapeDtypeStruct(f.shape, f.dtype) for f in fulls],
        in_specs=[anyspec] * n, out_specs=[anyspec] * n,
        input_output_aliases={a: a for a in range(n)},
        scratch_shapes=[pltpu.SemaphoreType.DMA((3 * n,)), pltpu.SemaphoreType.DMA((3 * n,))],
    )(*fulls)


def _sibling_rider(grads, col_flags):
    n = len(grads)
    pshapes = [_piece_shape(g.shape, col) for g, col in zip(grads, col_flags)]

    def copies(ins, outs, sems):
        send_sems, recv_sems = sems
        x, y, c = _mesh_pos()
        return [pltpu.make_async_remote_copy(
            src_ref=_piece(ins[a], col_flags[a], k, 1 - c), dst_ref=outs[a].at[k],
            send_sem=send_sems.at[a * N_CHIP + k], recv_sem=recv_sems.at[a * N_CHIP + k],
            device_id=(x, y, 1 - c), device_id_type=MESH) for a in range(n) for k in range(N_CHIP)]

    def start(ins, outs, sems):
        for cp in copies(ins, outs, sems):
            cp.start()

    def finish(ins, outs, sems):
        cps = copies(ins, outs, sems)
        for cp in cps:
            cp.wait_recv()
        for cp in cps:
            cp.wait_send()

    return _Rider(grads, [jax.ShapeDtypeStruct((N_CHIP,) + ps, g.dtype) for ps, g in zip(pshapes, grads)],
                  [pltpu.SemaphoreType.DMA((N_CHIP * n,)), pltpu.SemaphoreType.DMA((N_CHIP * n,))], start, finish)


def _sum_with_sibling(grads, landed, col_flags, c_idx, name):
    n = len(grads)
    pshapes = [_piece_shape(g.shape, col) for g, col in zip(grads, col_flags)]

    def body(c_ref, *refs):
        ins, lands, outs = refs[:n], refs[n:2 * n], refs[2 * n:]
        for a in range(n):
            outs[a][0] = (ins[a][...] + lands[a][0]).astype(BF16)

    in_specs = []
    for ps, col in zip(pshapes, col_flags):
        if col:
            in_specs.append(pl.BlockSpec(ps, lambda k, c_ref: (c_ref[0], k)))
        else:
            in_specs.append(pl.BlockSpec(ps, lambda k, c_ref: (2 * k + c_ref[0], 0)))
    land_specs = [pl.BlockSpec((1,) + ps, lambda k, c_ref: (k, 0, 0)) for ps in pshapes]
    return pl.pallas_call(
        body, name=name,
        grid_spec=pltpu.PrefetchScalarGridSpec(
            num_scalar_prefetch=1, grid=(N_CHIP,),
            in_specs=in_specs + land_specs, out_specs=land_specs),
        out_shape=[jax.ShapeDtypeStruct((N_CHIP,) + ps, BF16) for ps in pshapes],
        compiler_params=pltpu.CompilerParams(dimension_semantics=("arbitrary",), vmem_limit_bytes=VMEM_LIMIT_BYTES),
    )(c_idx, *grads, *landed)


def _chips_rider(parts):
    n = len(parts)

    def plan(ins, outs, sems, arriving):
        send_sems, recv_sems = sems
        x, y, c = _mesh_pos()
        k_me = 2 * x + y
        copies = []
        for a in range(n):
            for j, chip in enumerate([(1 - x, y), (x, 1 - y), (1 - x, 1 - y)]):
                k_peer = 2 * chip[0] + chip[1]
                copies.append(pltpu.make_async_remote_copy(
                    src_ref=ins[a].at[k_peer], dst_ref=outs[a].at[k_peer if arriving else k_me],
                    send_sem=send_sems.at[a * 3 + j], recv_sem=recv_sems.at[a * 3 + j],
                    device_id=(*chip, c), device_id_type=MESH))
        return copies

    def start(ins, outs, sems):
        for cp in plan(ins, outs, sems, False):
            cp.start()

    def finish(ins, outs, sems):
        arrivals = plan(ins, outs, sems, True)
        for cp in arrivals:
            cp.wait_recv()
        for cp in arrivals:
            cp.wait_send()

    return _Rider(parts, [jax.ShapeDtypeStruct(p.shape, p.dtype) for p in parts],
                  [pltpu.SemaphoreType.DMA((3 * n,)), pltpu.SemaphoreType.DMA((3 * n,))], start, finish)


def _sum_chips_and_share(landed, parts, after, name):
    n = len(landed)

    def body(*refs):
        ins, own, outs, red = refs[:n], refs[n:2 * n], refs[2 * n + 1:3 * n + 1], refs[3 * n + 1:4 * n + 1]
        send_sems, recv_sems, local_sems = refs[4 * n + 1:]
        x, y, c = _mesh_pos()
        sibling = (x, y, 1 - c)
        k_me = 2 * x + y
        copies, local = [], []
        for a in range(n):
            for k in range(N_CHIP):
                @pl.when(k_me == k)
                def _():
                    term = own[a][k].astype(F32)
                    red[a][...] = term if k == 0 else red[a][...] + term

                @pl.when(k_me != k)
                def _():
                    term = ins[a][k].astype(F32)
                    red[a][...] = term if k == 0 else red[a][...] + term

            mine = pltpu.make_async_copy(red[a], outs[a].at[c], local_sems.at[a])
            mine.start()
            local.append(mine)
            cp = pltpu.make_async_remote_copy(
                src_ref=red[a], dst_ref=outs[a].at[c],
                send_sem=send_sems.at[a], recv_sem=recv_sems.at[a],
                device_id=sibling, device_id_type=MESH)
            cp.start()
            copies.append(cp)
        for a in range(n):
            pltpu.make_async_remote_copy(
                src_ref=red[a], dst_ref=outs[a].at[1 - c],
                send_sem=send_sems.at[a], recv_sem=recv_sems.at[a],
                device_id=sibling, device_id_type=MESH).wait_recv()
        for cp in copies:
            cp.wait_send()
        for mine in local:
            mine.wait()

    return pl.pallas_call(
        body, name=name,
        out_shape=[jax.ShapeDtypeStruct((2,) + l.shape[1:], F32) for l in landed],
        in_specs=[pl.BlockSpec(memory_space=pltpu.VMEM)] * (2 * n + 1),
        out_specs=[pl.BlockSpec(memory_space=pl.ANY)] * n,
        scratch_shapes=[pltpu.VMEM(l.shape[1:], F32) for l in landed]
        + [pltpu.SemaphoreType.DMA((n,)), pltpu.SemaphoreType.DMA((n,)), pltpu.SemaphoreType.DMA((n,))],
        compiler_params=pltpu.CompilerParams(vmem_limit_bytes=VMEM_LIMIT_BYTES),
    )(*landed, *parts, after)


def _mod_shard(c_all, w_ada, b_shard):
    def body(c_ref, w_ref, b_ref, o_ref, sc_ref):
        cc = c_ref[...]
        sc = cc * _sigmoid(cc)
        sc_ref[...] = sc
        o_ref[...] = _dot(sc, w_ref[...]) + b_ref[...]

    nb, d = c_all.shape
    nn = w_ada.shape[1]
    return pl.pallas_call(
        body, name="mod_shard",
        out_shape=[jax.ShapeDtypeStruct((nb, nn), F32), jax.ShapeDtypeStruct((nb, d), F32)],
        compiler_params=pltpu.CompilerParams(vmem_limit_bytes=VMEM_LIMIT_BYTES),
    )(c_all, w_ada, b_shard)


V_SH_M, V_SC_M, V_G_M, V_SH_F, V_SC_F, V_G_F, V_PRE_MIX, V_POST_MIX, V_PRE_FFN, V_POST_FFN = range(10)


def _vrow(vec_ref, r):
    return vec_ref[r:r + 1, :]


def _mix_fwd(x, vecs, w_in_b, w_out_b, sgu_g, wm_b, bsb, wp_b, ps, pmats, ts):
    s_len, d = x.shape
    nt = s_len // ts
    n_proj = w_in_b.shape[1]

    def body(x_ref, vec_ref, win_ref, wout_ref, sg_ref, wm_ref, bs_ref, wp_ref, ps_ref, pm_ref,
             h1_ref, proj_ref, cat_ref, mixed_ref, x2_ref, h2_ref, carry_ref):
        i = pl.program_id(0)

        @pl.when(i == 0)
        def _():
            carry_ref[...] = jnp.zeros_like(carry_ref)

        sub = FFN_TS
        nsub = ts // sub

        def project(s):
            rs = slice(s * sub, (s + 1) * sub)
            x = x_ref[rs, :]
            h1 = (((x * _rms(x)) * _vrow(vec_ref, V_PRE_MIX)) * (1.0 + _vrow(vec_ref, V_SC_M))
                  + _vrow(vec_ref, V_SH_M)).astype(BF16)
            h1_ref[rs, :] = h1
            proj = _dot(h1, win_ref[...])
            proj_ref[rs, :] = proj.astype(BF16)
            return proj

        def mix(s, proj, halo):
            r0 = s * sub
            t_glob = lax.broadcasted_iota(jnp.int32, (sub, HEAD), 0) + (i * ts + r0)
            for h in range(N_HEADS):
                u = _gelu(proj[:, h * HEAD:(h + 1) * HEAD])
                v = _gelu(proj[:, A_WIDTH + h * HEAD:A_WIDTH + (h + 1) * HEAD])
                vn = ((v * _rms(v)) * sg_ref[h:h + 1, :]).astype(BF16)
                for b in range(sub // HEAD):
                    rs = slice(b * HEAD, (b + 1) * HEAD)
                    z = _dot(wm_ref[h], vn[rs]) + bs_ref[h]
                    cat_ref[r0 + b * HEAD:r0 + (b + 1) * HEAD, h * HEAD:(h + 1) * HEAD] = (u[rs] * z).astype(BF16)
            for g in range(len(POOL_WINDOWS)):
                gs = slice(g * HEAD, (g + 1) * HEAD)
                p = proj[:, 2 * A_WIDTH + g * HEAD:2 * A_WIDTH + (g + 1) * HEAD]
                pooled = _pool_fwd(p, halo[:, gs], g, t_glob)
                yb = _dot(pooled.astype(BF16), wp_ref[g]) * ps_ref[0:1, gs]
                cat_ref[r0:r0 + sub, A_WIDTH + g * HEAD:A_WIDTH + (g + 1) * HEAD] = yb.astype(BF16)

        def finish(s, mixed):
            rs = slice(s * sub, (s + 1) * sub)
            mixed_ref[rs, :] = mixed
            x2 = x_ref[rs, :] + _vrow(vec_ref, V_G_M) * ((mixed * _rms(mixed)) * _vrow(vec_ref, V_POST_MIX))
            x2_ref[rs, :] = x2
            h2 = (((x2 * _rms(x2)) * _vrow(vec_ref, V_PRE_FFN)) * (1.0 + _vrow(vec_ref, V_SC_F))
                  + _vrow(vec_ref, V_SH_F)).astype(BF16)
            h2_ref[rs, :] = _permute_bf16(pm_ref[0], h2)

        projs = [project(0)]
        halo = carry_ref[...]
        mixed_prev = None
        for s in range(nsub):
            if s + 1 < nsub:
                projs.append(project(s + 1))
            mix(s, projs[s], halo)
            halo = projs[s][sub - POOL_HALO:sub, 2 * A_WIDTH:]
            mixed = _dot(cat_ref[s * sub:(s + 1) * sub, :], wout_ref[...])
            if mixed_prev is not None:
                finish(s - 1, mixed_prev)
            mixed_prev = mixed
        carry_ref[...] = halo
        finish(nsub - 1, mixed_prev)

    row = lambda i: (i, 0)
    return _call(
        body, name="mix_fwd", grid=(nt,),
        in_specs=[_tiled((ts, d), row), _whole(vecs.shape), _resident(w_in_b.shape), _resident(w_out_b.shape),
                  _whole(sgu_g.shape), _whole(wm_b.shape), _whole(bsb.shape), _whole(wp_b.shape), _whole(ps.shape),
                  _whole(pmats.shape)],
        out_specs=[_tiled((ts, d), row), _tiled((ts, n_proj), row), _tiled((ts, d), row),
                   _tiled((ts, d), row), _tiled((ts, d), row), _tiled((ts, d), row)],
        out_shape=[jax.ShapeDtypeStruct((s_len, d), BF16), jax.ShapeDtypeStruct((s_len, n_proj), BF16),
                   jax.ShapeDtypeStruct((s_len, d), BF16), jax.ShapeDtypeStruct((s_len, d), F32),
                   jax.ShapeDtypeStruct((s_len, d), F32), jax.ShapeDtypeStruct((s_len, d), BF16)],
        scratch_shapes=[pltpu.VMEM((POOL_HALO, A_WIDTH), F32)],
        args=(x, vecs, w_in_b, w_out_b, sgu_g, wm_b, bsb, wp_b, ps, pmats))


def _perm_mats(ts):
    p = jnp.arange(ts)
    pm = (((p % SUBLANES) * (ts // SUBLANES) + p // SUBLANES)[:, None] == p[None, :]).astype(BF16)
    return jnp.stack([pm, pm.T])


def _permute_bf16(pm, xb):
    return _dot(pm, xb).astype(BF16)


def _permute_f32(pm, x):
    hi = x.astype(BF16)
    lo = (x - hi.astype(F32)).astype(BF16)
    return _dot(pm, hi) + _dot(pm, lo)


def _conv_out(u, um2, um1, cv_ref, cols):
    return (cv_ref[3:4, cols] + um2 * cv_ref[0:1, cols] + um1 * cv_ref[1:2, cols] + u * cv_ref[2:3, cols])


F_LOSS, F_DGF, F_DPOSTFFN = 0, 1, 2
B_DSHF, B_DSCF, B_DPREFFN, B_DGM, B_DPOSTMIX = 0, 1, 2, 3, 4
M_DSHM, M_DSCM, M_DPREMIX = 0, 1, 2
C_DCB, C_DCW = 0, 1
G1_F, G1_B, G1_M = 0, 8, 16


def _ffn_fwd(h2p, x2, tgt, w_up_b, w_down_b, cvec, vecs, pmats, ts):
    s_len, d = x2.shape
    ff2 = w_up_b.shape[1]
    ff = ff2 // 2
    nt = s_len // ts
    chunks = _ff_chunks(ff)

    def body(h2_ref, x2_ref, t_ref, wu_ref, wd_ref, cv_ref, vec_ref, pm_ref,
             up_ref, y_ref, act_ref, dy_ref, df_ref, acc_ref, carry_ref):
        @pl.when(pl.program_id(0) == 0)
        def _():
            carry_ref[...] = jnp.zeros_like(carry_ref)
            acc_ref[...] = jnp.zeros_like(acc_ref)

        h2v = h2_ref[...]

        def up_dots(o, w):
            return [_dot(h2v, wu_ref[:, base + o:base + o + w]) for base in (0, ff)]

        f = None
        pending = None
        nxt = up_dots(*chunks[0])
        for ci, (o, w) in enumerate(chunks):
            us = nxt
            if ci + 1 < len(chunks):
                nxt = up_dots(*chunks[ci + 1])
            if pending is not None:
                part = _dot(pending[0], wd_ref[pending[1]:pending[1] + pending[2], :])
                f = part if f is None else f + part
            sub0 = lax.broadcasted_iota(jnp.int32, (SUBLANES, w), 0) == 0
            ys = []
            for base, u in zip((0, ff), us):
                cols = slice(base + o, base + o + w)
                up_ref[:, cols] = u.astype(BF16)
                last1, last2 = u[ts - SUBLANES:ts], u[ts - CONV_KEEP:ts - SUBLANES]
                b1 = jnp.where(sub0, pltpu.roll(carry_ref[SUBLANES:CONV_KEEP, cols], 1, 0), pltpu.roll(last1, 1, 0))
                b2 = jnp.where(sub0, pltpu.roll(carry_ref[0:SUBLANES, cols], 1, 0), pltpu.roll(last2, 1, 0))
                um1 = jnp.concatenate([b1, u[:ts - SUBLANES]], axis=0)
                um2 = jnp.concatenate([b2, b1, u[:ts - CONV_KEEP]], axis=0)
                ys.append(_conv_out(u, um2, um1, cv_ref, cols))
                carry_ref[:, cols] = u[ts - CONV_KEEP:ts]
            gate, val = ys
            sg = _sigmoid(gate)
            gs = gate * sg
            act = (gs * val).astype(BF16)
            act_ref[:, o:o + w] = act
            y_ref[:, o:o + w] = (val * (sg + gs * (1.0 - sg))).astype(BF16)
            y_ref[:, ff + o:ff + o + w] = gs.astype(BF16)
            pending = (act, o, w)
        f = f + _dot(pending[0], wd_ref[pending[1]:pending[1] + pending[2], :])
        f = _permute_f32(pm_ref[1], f)
        r3 = _rms(f)
        fhat = f * r3
        post = _vrow(vec_ref, V_POST_FFN)
        g_f = _vrow(vec_ref, V_G_F)
        fn = fhat * post
        e = (x2_ref[...] + g_f * fn) - t_ref[...]
        dy = e * (1.0 / d)
        dy_ref[...] = dy
        dfn = dy * g_f
        acc_ref[F_LOSS:F_LOSS + 1, :] += _colsum(e * e)
        acc_ref[F_DGF:F_DGF + 1, :] += _colsum(dy * fn)
        acc_ref[F_DPOSTFFN:F_DPOSTFFN + 1, :] += _colsum(dfn * fhat)
        dfhat = dfn * post
        df = (r3 * (dfhat - fhat * _rowmean(dfhat * fhat))).astype(BF16)
        df_ref[...] = _permute_bf16(pm_ref[0], df)

    row = lambda i: (i, 0)
    return pl.pallas_call(
        body, name="ffn_fwd", grid=(nt,),
        in_specs=[_tiled((ts, d), row), _tiled((ts, d), row), _tiled((ts, d), row), _resident(w_up_b.shape),
                  _resident(w_down_b.shape), _whole(cvec.shape), _whole(vecs.shape), _whole(pmats.shape)],
        out_specs=[_tiled((ts, ff2), row), _tiled((ts, ff2), row), _tiled((ts, ff), row), _tiled((ts, d), row),
                   _tiled((ts, d), row), _whole((8, d))],
        out_shape=[jax.ShapeDtypeStruct((s_len, ff2), BF16), jax.ShapeDtypeStruct((s_len, ff2), BF16),
                   jax.ShapeDtypeStruct((s_len, ff), BF16), jax.ShapeDtypeStruct((s_len, d), F32),
                   jax.ShapeDtypeStruct((s_len, d), BF16), jax.ShapeDtypeStruct((8, d), F32)],
        scratch_shapes=[pltpu.VMEM((CONV_KEEP, ff2), F32)],
        compiler_params=_seq_params(),
    )(h2p, x2, tgt, w_up_b, w_down_b, cvec, vecs, pmats)


def _ffn_bwd(dfp, upp, yp, x2, dy, mixed, w_up_b, w_down_b, cvec, vecs, pmats, ts, rider=None):
    s_len, d = x2.shape
    ff2 = w_up_b.shape[1]
    ff = ff2 // 2
    nt = s_len // ts
    chunks = _ff_chunks(ff, 512)

    def body(df_ref, up_ref, y_ref, x2_ref, dy_ref, mx_ref, wu_ref, wd_ref, cv_ref, vec_ref, pm_ref,
             dup_ref, dx2_ref, dmx_ref, accc_ref, acc_ref, carry_ref):
        @pl.when(pl.program_id(0) == 0)
        def _():
            carry_ref[...] = jnp.zeros_like(carry_ref)
            accc_ref[...] = jnp.zeros_like(accc_ref)
            acc_ref[...] = jnp.zeros_like(acc_ref)

        dfv = df_ref[...]

        def dh2_add(acc, dups, o, w):
            for base, dup in zip((0, ff), dups):
                part = _dot_nt(dup, wu_ref[:, base + o:base + o + w])
                acc = part if acc is None else acc + part
            return acc

        dh2 = None
        pending = None
        nxt = _dot_nt(dfv, wd_ref[chunks[0][0]:chunks[0][0] + chunks[0][1], :])
        for ci, (o, w) in enumerate(chunks):
            dact = nxt
            if ci + 1 < len(chunks):
                o2, w2 = chunks[ci + 1]
                nxt = _dot_nt(dfv, wd_ref[o2:o2 + w2, :])
            if pending is not None:
                dh2 = dh2_add(dh2, *pending)
            sub7 = lax.broadcasted_iota(jnp.int32, (SUBLANES, w), 0) == SUBLANES - 1
            dups = []
            dys = (dact * y_ref[:, o:o + w].astype(F32), dact * y_ref[:, ff + o:ff + o + w].astype(F32))
            for base, dyv in zip((0, ff), dys):
                cols = slice(base + o, base + o + w)
                u = up_ref[:, cols].astype(F32)
                up1 = SUBLANES - 1
                e0 = jnp.where(sub7, pltpu.roll(carry_ref[0:SUBLANES, cols], up1, 0),
                               pltpu.roll(dyv[0:SUBLANES], up1, 0))
                e1 = jnp.where(sub7, pltpu.roll(carry_ref[SUBLANES:CONV_KEEP, cols], up1, 0),
                               pltpu.roll(dyv[SUBLANES:CONV_KEEP], up1, 0))
                dyp1 = jnp.concatenate([dyv[SUBLANES:], e0], axis=0)
                dyp2 = jnp.concatenate([dyv[CONV_KEEP:], e0, e1], axis=0)
                accc_ref[C_DCB:C_DCB + 1, cols] += _colsum(dyv)
                accc_ref[C_DCW + 0:C_DCW + 1, cols] += _colsum(dyp2 * u)
                accc_ref[C_DCW + 1:C_DCW + 2, cols] += _colsum(dyp1 * u)
                accc_ref[C_DCW + 2:C_DCW + 3, cols] += _colsum(dyv * u)
                dup = (dyv * cv_ref[2:3, cols] + dyp1 * cv_ref[1:2, cols] + dyp2 * cv_ref[0:1, cols]).astype(BF16)
                dup_ref[:, cols] = dup
                dups.append(dup)
                carry_ref[:, cols] = dyv[0:CONV_KEEP]
            pending = (dups, o, w)
        dh2 = dh2_add(dh2, *pending)
        dh2 = _permute_f32(pm_ref[1], dh2)
        x2 = x2_ref[...]
        r2 = _rms(x2)
        xn = x2 * r2
        pre = _vrow(vec_ref, V_PRE_FFN)
        one_sc = 1.0 + _vrow(vec_ref, V_SC_F)
        acc_ref[B_DSHF:B_DSHF + 1, :] += _colsum(dh2)
        acc_ref[B_DSCF:B_DSCF + 1, :] += _colsum(dh2 * (xn * pre))
        acc_ref[B_DPREFFN:B_DPREFFN + 1, :] += _colsum(dh2 * xn * one_sc)
        dxn = dh2 * pre * one_sc
        dx2 = dy_ref[...] + r2 * (dxn - xn * _rowmean(dxn * xn))
        dx2_ref[...] = dx2
        mixed = mx_ref[...]
        rm = _rms(mixed)
        mhat = mixed * rm
        post = _vrow(vec_ref, V_POST_MIX)
        acc_ref[B_DGM:B_DGM + 1, :] += _colsum(dx2 * (mhat * post))
        dmn = dx2 * _vrow(vec_ref, V_G_M)
        acc_ref[B_DPOSTMIX:B_DPOSTMIX + 1, :] += _colsum(dmn * mhat)
        dmhat = dmn * post
        dmx_ref[...] = (rm * (dmhat - mhat * _rowmean(dmhat * mhat))).astype(BF16)

    rev = lambda i: (nt - 1 - i, 0)
    return _call(
        body, name="ffn_bwd", grid=(nt,), rider=rider,
        in_specs=[_tiled((ts, d), rev), _tiled((ts, ff2), rev), _tiled((ts, ff2), rev), _tiled((ts, d), rev),
                  _tiled((ts, d), rev), _tiled((ts, d), rev), _resident(w_up_b.shape), _resident(w_down_b.shape),
                  _whole(cvec.shape), _whole(vecs.shape), _whole(pmats.shape)],
        out_specs=[_tiled((ts, ff2), rev), _tiled((ts, d), rev), _tiled((ts, d), rev), _whole((8, ff2)),
                   _whole((8, d))],
        out_shape=[jax.ShapeDtypeStruct((s_len, ff2), BF16), jax.ShapeDtypeStruct((s_len, d), F32),
                   jax.ShapeDtypeStruct((s_len, d), BF16), jax.ShapeDtypeStruct((8, ff2), F32),
                   jax.ShapeDtypeStruct((8, d), F32)],
        scratch_shapes=[pltpu.VMEM((CONV_KEEP, ff2), F32)],
        args=(dfp, upp, yp, x2, dy, mixed, w_up_b, w_down_b, cvec, vecs, pmats))


def _mix_bwd(dmixed, dx2, x, proj, vecs, w_in_b, w_out_b, sgu_g, wm_b, wmt_b, bsb, wp_b, ps, ts):
    s_len, d = x.shape
    nt = s_len // ts
    n_proj = proj.shape[1]
    per = ts // POOL_HALO

    def body(dmx_ref, dx2_ref, x_ref, proj_ref, projh_ref, vec_ref, win_ref, wout_ref, sg_ref, wm_ref, wmt_ref,
             bs_ref, wp_ref, ps_ref,
             gx_ref, dproj_ref, acc_ref, dwm_out, dwp_out, db_ref, dg_ref, dps_ref,
             carry_ref, dwm_ref, dwp_ref, dbz_ref):
        i = pl.program_id(0)
        tile = nt - 1 - i

        @pl.when(i == 0)
        def _():
            carry_ref[...] = jnp.zeros_like(carry_ref)
            for r in (acc_ref, dwm_ref, dwp_ref, dbz_ref, dg_ref, dps_ref):
                r[...] = jnp.zeros_like(r)

        sub = FFN_TS
        nsub = ts // sub
        ext = sub + POOL_HALO

        def cotangent(s):
            return _dot_nt(dmx_ref[s * sub:(s + 1) * sub, :], wout_ref[...])

        def back(s, dcat, halo, later):
            r0 = s * sub
            rows = slice(r0, r0 + sub)
            t_glob = lax.broadcasted_iota(jnp.int32, (sub, HEAD), 0) + (tile * ts + r0)
            for h in range(N_HEADS):
                hs = slice(h * HEAD, (h + 1) * HEAD)
                vs = slice(A_WIDTH + h * HEAD, A_WIDTH + (h + 1) * HEAD)
                gain = sg_ref[h:h + 1, :]
                for b in range(sub // HEAD):
                    blk = slice(r0 + b * HEAD, r0 + (b + 1) * HEAD)
                    au = proj_ref[blk, hs].astype(F32)
                    av = proj_ref[blk, vs].astype(F32)
                    u, u_grad = _gelu_and_grad(au)
                    v, v_grad = _gelu_and_grad(av)
                    rv = _rms(v)
                    vhat = v * rv
                    vn = (vhat * gain).astype(BF16)
                    dout = dcat[b * HEAD:(b + 1) * HEAD, hs]
                    z = _dot(wm_ref[h], vn) + bs_ref[h]
                    dz = dout * u
                    dbz_ref[h] += dz
                    dzb = dz.astype(BF16)
                    dwm_ref[h] += _dot_nt(dzb, vn)
                    dvn = _dot(wmt_ref[h], dzb)
                    dg_ref[h:h + 1, :] += _colsum(dvn * vhat)
                    dvhat = dvn * gain
                    dv = rv * (dvhat - vhat * _rowmean(dvhat * vhat))
                    dproj_ref[blk, hs] = ((dout * z) * u_grad).astype(BF16)
                    dproj_ref[blk, vs] = (dv * v_grad).astype(BF16)
            firsts = []
            for g in range(len(POOL_WINDOWS)):
                gs = slice(g * HEAD, (g + 1) * HEAD)
                pcols = slice(2 * A_WIDTH + g * HEAD, 2 * A_WIDTH + (g + 1) * HEAD)
                p = proj_ref[rows, pcols].astype(F32)
                pb = _pool_fwd(p, halo[:, gs], g, t_glob).astype(BF16)
                dyb = dcat[:, A_WIDTH + g * HEAD:A_WIDTH + (g + 1) * HEAD]
                dps_ref[0:1, gs] += _colsum(dyb * _dot(pb, wp_ref[g]))
                dyl = (dyb * ps_ref[0:1, gs]).astype(BF16)
                dwp_ref[g] += _dot_tn(pb, dyl)
                dpooled = _dot_nt(dyl, wp_ref[g])
                cnt = jnp.minimum(t_glob + 1, POOL_WINDOWS[g]).astype(F32)
                q = dpooled / cnt
                acc = jnp.concatenate([q, later[:, gs]], axis=0)
                for step in range(g + 1):
                    acc = acc + pltpu.roll(acc, ext - (1 << step), 0)
                dproj_ref[rows, pcols] = (acc[:sub] - dpooled).astype(BF16)
                firsts.append(q[0:POOL_HALO])
            return jnp.concatenate(firsts, axis=1)

        def finish(s, dh1):
            rows = slice(s * sub, (s + 1) * sub)
            x = x_ref[rows, :]
            r1 = _rms(x)
            xn = x * r1
            pre = _vrow(vec_ref, V_PRE_MIX)
            one_sc = 1.0 + _vrow(vec_ref, V_SC_M)
            acc_ref[M_DSHM:M_DSHM + 1, :] += _colsum(dh1)
            acc_ref[M_DSCM:M_DSCM + 1, :] += _colsum(dh1 * (xn * pre))
            acc_ref[M_DPREMIX:M_DPREMIX + 1, :] += _colsum(dh1 * xn * one_sc)
            dxn = dh1 * pre * one_sc
            gx_ref[rows, :] = dx2_ref[rows, :] + r1 * (dxn - xn * _rowmean(dxn * xn))

        nxt = cotangent(nsub - 1)
        later = carry_ref[...]
        dh1_prev = None
        for s in reversed(range(nsub)):
            dcat = nxt
            if s > 0:
                nxt = cotangent(s - 1)
                halo = proj_ref[s * sub - POOL_HALO:s * sub, 2 * A_WIDTH:].astype(F32)
            else:
                halo = jnp.where(tile > 0, projh_ref[:, 2 * A_WIDTH:].astype(F32), 0.0)
            later = back(s, dcat, halo, later)
            dh1 = _dot_nt(dproj_ref[s * sub:(s + 1) * sub, :], win_ref[...])
            if dh1_prev is not None:
                finish(s + 1, dh1_prev)
            dh1_prev = dh1
        carry_ref[...] = later
        finish(0, dh1_prev)

        @pl.when(i == nt - 1)
        def _():
            dwm_out[...] = dwm_ref[...].astype(BF16)
            dwp_out[...] = dwp_ref[...].astype(BF16)
            db_ref[...] = jnp.zeros_like(db_ref)
            for h in range(N_HEADS):
                db_ref[h:h + 1, :] = jnp.sum(dbz_ref[h].T, axis=0, keepdims=True)

    rev = lambda i: (nt - 1 - i, 0)
    halo_map = lambda i: (jnp.maximum((nt - 1 - i) * per - 1, 0), 0)
    hshape = (N_HEADS, HEAD, HEAD)
    return _call(
        body, name="mix_bwd", grid=(nt,),
        in_specs=[_tiled((ts, d), rev), _tiled((ts, d), rev), _tiled((ts, d), rev), _tiled((ts, n_proj), rev),
                  _tiled((POOL_HALO, n_proj), halo_map), _whole(vecs.shape), _resident(w_in_b.shape),
                  _resident(w_out_b.shape), _whole(sgu_g.shape), _whole(wm_b.shape), _whole(wmt_b.shape),
                  _whole(bsb.shape), _whole(wp_b.shape), _whole(ps.shape)],
        out_specs=[_tiled((ts, d), rev), _tiled((ts, n_proj), rev), _whole((8, d)), _whole(hshape), _whole(hshape),
                   _whole((8, HEAD)), _whole((8, HEAD)), _whole((8, A_WIDTH))],
        out_shape=[jax.ShapeDtypeStruct((s_len, d), F32), jax.ShapeDtypeStruct((s_len, n_proj), BF16),
                   jax.ShapeDtypeStruct((8, d), F32), jax.ShapeDtypeStruct(hshape, BF16),
                   jax.ShapeDtypeStruct(hshape, BF16), jax.ShapeDtypeStruct((8, HEAD), F32),
                   jax.ShapeDtypeStruct((8, HEAD), F32), jax.ShapeDtypeStruct((8, A_WIDTH), F32)],
        scratch_shapes=[pltpu.VMEM((POOL_HALO, A_WIDTH), F32), pltpu.VMEM(hshape, F32), pltpu.VMEM(hshape, F32),
                        pltpu.VMEM(hshape, F32)],
        args=(dmixed, dx2, x, proj, proj, vecs, w_in_b, w_out_b, sgu_g, wm_b, wmt_b, bsb, wp_b, ps))


def _wgrad(a, b, tn, ts, name, rider=None):
    s_len, m = a.shape
    n = b.shape[1]
    ts = min(ts, s_len)

    def body(a_ref, b_ref, o_ref):
        @pl.when(pl.program_id(1) == 0)
        def _():
            o_ref[...] = jnp.zeros_like(o_ref)

        o_ref[...] += _dot_tn(a_ref[...], b_ref[...])

    (g,), r_out = _call(
        body, name=name, grid=(n // tn, s_len // ts), rider=rider,
        in_specs=[pl.BlockSpec((ts, m), lambda j, s: (s, 0)), pl.BlockSpec((ts, tn), lambda j, s: (s, j))],
        out_specs=[pl.BlockSpec((m, tn), lambda j, s: (0, j))],
        out_shape=[jax.ShapeDtypeStruct((m, n), F32)], scratch_shapes=[], args=(a, b))
    return g, r_out


def _wgrad_pair(a1, b1, a2, b2, after, ts, name):
    s_len = a1.shape[0]
    ts = min(ts, s_len)
    shapes = [(a1.shape[1], b1.shape[1]), (a2.shape[1], b2.shape[1])]

    def body(a1_ref, b1_ref, a2_ref, b2_ref, after_ref, o1_ref, o2_ref):
        @pl.when(pl.program_id(0) == 0)
        def _():
            o1_ref[...] = jnp.zeros_like(o1_ref)
            o2_ref[...] = jnp.zeros_like(o2_ref)

        o1_ref[...] += _dot_tn(a1_ref[...], b1_ref[...])
        o2_ref[...] += _dot_tn(a2_ref[...], b2_ref[...])

    row = lambda s: (s, 0)
    return _call(
        body, name=name, grid=(s_len // ts,),
        in_specs=[pl.BlockSpec((ts, t.shape[1]), row) for t in (a1, b1, a2, b2)] + [_whole(after.shape)],
        out_specs=[_whole(sh) for sh in shapes],
        out_shape=[jax.ShapeDtypeStruct(sh, F32) for sh in shapes], scratch_shapes=[],
        args=(a1, b1, a2, b2, after))[0]


def _adamw_big(g, w, m, v, name):
    r, cdim = g.shape
    tr = r
    while tr * cdim * 4 > (3 << 19) and tr % 16 == 0:
        tr //= 2

    def body(g_ref, w_ref, m_ref, v_ref, d_ref, nm_ref, nv_ref):
        delta, m2, v2 = _adamw_math(w_ref[0], g_ref[...], m_ref[0], v_ref[0])
        d_ref[0] = delta
        nm_ref[0] = m2
        nv_ref[0] = v2

    s3 = pl.BlockSpec((1, tr, cdim), lambda i: (0, i, 0))
    return pl.pallas_call(
        body, name=name, grid=(r // tr,),
        in_specs=[pl.BlockSpec((tr, cdim), lambda i: (i, 0)), s3, s3, s3],
        out_specs=[s3, s3, s3],
        out_shape=[jax.ShapeDtypeStruct(w.shape, F32)] * 3,
        compiler_params=pltpu.CompilerParams(dimension_semantics=("parallel",), vmem_limit_bytes=VMEM_LIMIT_BYTES),
    )(g, w, m, v)


def _wada_update(sct, gm, w, m, v):
    _, r, cdim = w.shape
    tr = 256
    kp = sct.shape[1]

    def body(s_ref, g_ref, w_ref, m_ref, v_ref, gw_ref, d_ref, nm_ref, nv_ref):
        g = _dot(s_ref[...], g_ref[...])
        gw_ref[0] = g
        delta, m2, v2 = _adamw_math(w_ref[0], g, m_ref[0], v_ref[0])
        d_ref[0] = delta
        nm_ref[0] = m2
        nv_ref[0] = v2

    s3 = pl.BlockSpec((1, tr, cdim), lambda i: (0, i, 0))
    return pl.pallas_call(
        body, name="wada_update", grid=(r // tr,),
        in_specs=[pl.BlockSpec((tr, kp), lambda i: (i, 0)), _whole(gm.shape), s3, s3, s3],
        out_specs=[s3, s3, s3, s3],
        out_shape=[jax.ShapeDtypeStruct(w.shape, F32)] * 4,
        compiler_params=pltpu.CompilerParams(dimension_semantics=("parallel",), vmem_limit_bytes=VMEM_LIMIT_BYTES),
    )(sct, gm, w, m, v)


def _small_update(g1, g2, g2s, gwm, gwp, gbz, gsg, gps, params):
    names = ["b_ada", "pre_mix_g", "post_mix_g", "sgu_norm_g", "w_spatial", "b_spatial", "w_pool", "pool_scale",
             "pre_ffn_g", "post_ffn_g", "conv_w", "conv_b"]
    d = g1.shape[2]
    flat_in = [g1, g2, g2s, gwm, gwp, gbz, gsg, gps]
    n_g = len(flat_in)
    for nm in names:
        flat_in += list(params[nm])

    def body(*refs):
        g1_ref, g2_ref, g2s_ref, gwm_ref, gwp_ref, gbz_ref, gsg_ref, gps_ref = refs[:n_g]
        wmv = refs[n_g:n_g + 3 * len(names)]
        loss_ref = refs[n_g + 3 * len(names)]
        outs = refs[n_g + 3 * len(names) + 1:]

        def dsum(ref, idx):
            acc = ref[(0,) + idx].astype(F32)
            for dev in range(1, N_DEV):
                acc = acc + ref[(dev,) + idx].astype(F32)
            return acc

        def apply(pi, g, widx, oidx):
            w_ref, m_ref, v_ref = wmv[3 * pi:3 * pi + 3]
            g_ref, d_ref, nm_ref, nv_ref = outs[4 * pi:4 * pi + 4]
            delta, m2, v2 = _adamw_math(w_ref[widx], g, m_ref[widx], v_ref[widx])
            g_ref[oidx] = g
            d_ref[oidx] = delta
            nm_ref[oidx] = m2
            nv_ref[oidx] = v2

        def row1(base, r):
            return (slice(base + r, base + r + 1), slice(None))

        tot = dsum(g1_ref, row1(G1_F, F_LOSS))
        loss_ref[...] = jnp.zeros(loss_ref.shape, F32) + jnp.sum(tot) * (0.5 / d)
        mod_rows = [row1(G1_M, M_DSHM), row1(G1_M, M_DSCM), row1(G1_B, B_DGM), row1(G1_B, B_DSHF),
                    row1(G1_B, B_DSCF), row1(G1_F, F_DGF)]
        for j, rr in enumerate(mod_rows):
            cs = (slice(None), slice(j * d, (j + 1) * d))
            apply(0, dsum(g1_ref, rr), cs, cs)
        full = (slice(None), slice(None))
        apply(1, dsum(g1_ref, row1(G1_M, M_DPREMIX)), full, full)
        apply(2, dsum(g1_ref, row1(G1_B, B_DPOSTMIX)), full, full)
        apply(3, dsum(gsg_ref, (slice(0, N_HEADS), slice(None))), (0,), (0,))
        pos_i = lax.broadcasted_iota(jnp.int32, (HEAD, HEAD), 0)
        pos_j = lax.broadcasted_iota(jnp.int32, (HEAD, HEAD), 1)
        causal = (pos_j // CHUNK) <= (pos_i // CHUNK)
        for h in range(N_HEADS):
            blk = (slice(h * HEAD, (h + 1) * HEAD), slice(None))
            apply(4, jnp.where(causal, dsum(gwm_ref, blk), 0.0), (0, h), (0, h))
            apply(5, dsum(gbz_ref, (slice(h, h + 1), slice(None))), (0, slice(h, h + 1)), (0, slice(h, h + 1)))
            apply(6, dsum(gwp_ref, blk), (0, h), (0, h))
        apply(7, dsum(gps_ref, (slice(0, 1), slice(None))), full, full)
        apply(8, dsum(g1_ref, row1(G1_B, B_DPREFFN)), full, full)
        apply(9, dsum(g1_ref, row1(G1_F, F_DPOSTFFN)), full, full)
        apply(10, dsum(g2s_ref, (slice(C_DCW, C_DCW + 3), slice(None))), (0,), (0,))
        apply(11, dsum(g2_ref, (slice(C_DCB, C_DCB + 1), slice(None))), full, full)

    out_shape = [jax.ShapeDtypeStruct((8, HEAD), F32)]
    for nm in names:
        out_shape += [jax.ShapeDtypeStruct(params[nm][0].shape, F32)] * 4
    res = pl.pallas_call(
        body, name="small_update", out_shape=out_shape,
        compiler_params=pltpu.CompilerParams(vmem_limit_bytes=VMEM_LIMIT_BYTES),
    )(*flat_in)
    out = {nm: tuple(res[1 + 4 * i:5 + 4 * i]) for i, nm in enumerate(names)}
    return res[0], out


def kernel(x, c, w_ada, b_ada, pre_mix_g, post_mix_g, w_in, sgu_norm_g, w_spatial, b_spatial, w_pool, pool_scale, w_out, pre_ffn_g, post_ffn_g, w_up, conv_w, conv_b, w_down, loss_target, m_w_ada, m_b_ada, m_pre_mix_g, m_post_mix_g, m_w_in, m_sgu_norm_g, m_w_spatial, m_b_spatial, m_w_pool, m_pool_scale, m_w_out, m_pre_ffn_g, m_post_ffn_g, m_w_up, m_conv_w, m_conv_b, m_w_down, v_w_ada, v_b_ada, v_pre_mix_g, v_post_mix_g, v_w_in, v_sgu_norm_g, v_w_spatial, v_b_spatial, v_w_pool, v_pool_scale, v_w_out, v_pre_ffn_g, v_post_ffn_g, v_w_up, v_conv_w, v_conv_b, v_w_down):
    xi, yi, ci = _mesh_pos()
    k_me = 2 * xi + yi
    dev = 2 * k_me + ci
    s_len, d = x.shape[1], x.shape[2]
    x2d = x[0]
    tgt = loss_target[0]
    ff2 = conv_b.shape[1]
    n_ada = w_ada.shape[2]
    n_cw = conv_w.shape[2]

    k_idx = k_me.reshape(1).astype(jnp.int32)
    mix_flags = (True, False)
    (w_in_s, w_out_s), (w_in_f, w_out_f) = _cast_bf16([w_in[0], w_out[0]], mix_flags, k_idx, "cast_weights_mix")

    def place(t):
        return lax.dynamic_update_slice(jnp.zeros((N_DEV * t.shape[0], t.shape[1]), t.dtype), t,
                                        (dev * t.shape[0], 0))

    cw_blk = jnp.concatenate([conv_w[0], jnp.zeros((5, n_cw), F32)], axis=0)
    fly_c = _split_begin([place(c.reshape(8, d // 8)), place(cw_blk)], 6, _small_ici_copies, k_idx,
                         "c_gather_begin")
    fly_mix = _gather_begin([w_in_s, w_out_s], [w_in_f, w_out_f], mix_flags, fly_c[3], "gather_begin_mix")
    c_all, cw_all = _small_finish(_split_end(fly_c, _small_ici_copies, fly_mix[3], "c_gather_end"),
                                  "c_gather_finish")
    (w_up_s, w_down_s), (w_up_f, w_down_f) = _cast_bf16([w_up[0], w_down[0]], mix_flags, k_idx, "cast_weights_ffn",
                                                        after=fly_mix[3])
    c_all = c_all.reshape(N_DEV, 8, d // 8).reshape(N_DEV, d)
    cw_full = jnp.concatenate([cw_all[16 * k:16 * k + 8] for k in range(N_CHIP)], axis=1)
    cvec = jnp.concatenate([cw_full[0:3], conv_b, jnp.zeros((4, ff2), F32)], axis=0)
    b_shard = lax.dynamic_slice_in_dim(b_ada, k_me * n_ada, n_ada, axis=1)
    mod_k, sc_all = _mod_shard(c_all, w_ada[0], b_shard)
    (mod_g,) = _run_rider(_allgather_rider([mod_k]), "gather_mod")
    mod_all = jnp.concatenate([mod_g[16 * k:16 * k + 8] for k in range(N_CHIP)], axis=1)
    mod_me = lax.dynamic_slice_in_dim(mod_all, dev, 1, axis=0).reshape(6, d)
    vecs = jnp.concatenate([mod_me, pre_mix_g, post_mix_g, pre_ffn_g, post_ffn_g, jnp.zeros((6, d), F32)], axis=0)

    fly_ffn = _gather_begin([w_up_s, w_down_s], [w_up_f, w_down_f], mix_flags, mod_g, "gather_begin_ffn")
    w_in_b, w_out_b = _gather_finish(_gather_end(fly_mix, mix_flags, fly_ffn[3], "gather_end_mix"), mix_flags,
                                     "gather_finish_mix")

    pos = jnp.arange(HEAD)
    causal = (pos[None, :] // CHUNK) <= (pos[:, None] // CHUNK)
    wm = jnp.where(causal[None], w_spatial[0], 0.0)
    wm_b = wm.astype(BF16)
    wmt_b = jnp.swapaxes(wm, 1, 2).astype(BF16)
    bsb = jnp.broadcast_to(b_spatial[0][:, :, None], (N_HEADS, HEAD, HEAD))
    wp_b = w_pool[0].astype(BF16)
    sgu_g = jnp.concatenate([sgu_norm_g[0], jnp.zeros((4, HEAD), F32)], axis=0)
    ps = jnp.concatenate([pool_scale, jnp.zeros((7, A_WIDTH), F32)], axis=0)

    pmats = _perm_mats(FFN_TS)
    h1, proj, cat, mixed, x2, h2p = _mix_fwd(x2d, vecs, w_in_b, w_out_b, sgu_g, wm_b, bsb, wp_b, ps, pmats,
                                             ts=MIX_TS)[0]
    w_up_b, w_down_b = _gather_finish(_gather_end(fly_ffn, mix_flags, h2p, "gather_end_ffn"), mix_flags,
                                      "gather_finish_ffn")
    up, yv, act, dy, dfp, acc_f = _ffn_fwd(h2p, x2, tgt, w_up_b, w_down_b, cvec, vecs, pmats, ts=FFN_TS)

    c_idx = ci.reshape(1).astype(jnp.int32)
    g_w_down, _ = _wgrad(act, dfp, d, WGRAD_TS_WIDE, "wgrad_down")
    (dup, dx2, dmixed, acc_c, acc_b), (land_down,) = _ffn_bwd(
        dfp, up, yv, x2, dy, mixed, w_up_b, w_down_b, cvec, vecs, pmats, ts=FFN_TS,
        rider=_sibling_rider([g_w_down], (False,)))
    (part_down,) = _sum_with_sibling([g_w_down], [land_down], (False,), c_idx, "pair_sum_down")
    g_w_up, (chips_down,) = _wgrad(h2p, dup, ff2 // 2, WGRAD_TS, "wgrad_up", rider=_chips_rider([part_down]))
    fly_up = _sibling_begin(g_w_up, True, chips_down, "sibling_begin_up")
    gx, dproj, acc_m, dwm, dwp, dbz, dsg, dps = _mix_bwd(
        dmixed, dx2, x2d, proj, vecs + fly_up[3][0:1, 0:1], w_in_b, w_out_b, sgu_g, wm_b, wmt_b, bsb, wp_b, ps,
        ts=MIX_TS)[0]
    g_w_up, land_up = _sibling_end(fly_up, True, dproj, "sibling_end_up")
    (part_up,) = _sum_with_sibling([g_w_up], [land_up], (True,), c_idx, "pair_sum_up")
    fly_chips_up = _chips_begin([part_up], c_idx, "chips_begin_up")
    g1 = jnp.concatenate([acc_f, acc_b, acc_m], axis=0)
    hflat = (N_HEADS * HEAD, HEAD)
    small_bufs = [place(t) for t in (g1, acc_c, dwm.reshape(hflat), dwp.reshape(hflat), dbz, dsg, dps)]
    fly_small = _split_begin(small_bufs, 3 * len(small_bufs), _small_ici_copies, fly_chips_up[3],
                             "small_gather_begin")
    g_w_out, g_w_in = _wgrad_pair(cat, dmixed, h1, dproj, fly_small[3], WGRAD_TS_WIDE, "wgrad_mix")
    land_mix = _run_rider(_sibling_rider([g_w_in, g_w_out], (True, False)), "reduce_to_sibling")
    parts_mix = _sum_with_sibling([g_w_in, g_w_out], land_mix, (True, False), c_idx, "pair_sum_mix")
    (part_up,), (chips_up,) = _chips_end(fly_chips_up, parts_mix[0], "chips_end_up")
    fly_chips_mix = _chips_begin(parts_mix, chips_up, "chips_begin_mix")

    def adamw_of(names, reduced):
        res = {}
        for nm, red in zip(names, reduced):
            w, m, v = big_wmv[nm]
            g = red.reshape(w.shape[1], w.shape[2])
            res[nm] = (g.reshape(w.shape),) + tuple(_adamw_big(g, w, m, v, "adamw_" + nm))
        return res

    big_wmv = {"w_in": (w_in, m_w_in, v_w_in), "w_out": (w_out, m_w_out, v_w_out),
               "w_up": (w_up, m_w_up, v_w_up), "w_down": (w_down, m_w_down, v_w_down)}
    big = adamw_of(("w_up", "w_down"), _sum_chips_and_share([chips_up, chips_down], [part_up, part_down],
                                                           fly_chips_mix[3], "sum_share_ffn"))

    gathered = _small_finish(_split_end(fly_small, _small_ici_copies, big["w_down"][1], "small_gather_end"),
                             "small_gather_finish")
    g1a, g2a, gwm, gwp, gbz, gsg, gps = [t.reshape((N_DEV, t.shape[0] // N_DEV, t.shape[1])) for t in gathered]
    g2s = lax.dynamic_slice_in_dim(g2a, k_me * n_cw, n_cw, axis=2)
    params = {
        "b_ada": (b_ada, m_b_ada, v_b_ada), "pre_mix_g": (pre_mix_g, m_pre_mix_g, v_pre_mix_g),
        "post_mix_g": (post_mix_g, m_post_mix_g, v_post_mix_g),
        "sgu_norm_g": (sgu_norm_g, m_sgu_norm_g, v_sgu_norm_g), "w_spatial": (w_spatial, m_w_spatial, v_w_spatial),
        "b_spatial": (b_spatial, m_b_spatial, v_b_spatial), "w_pool": (w_pool, m_w_pool, v_w_pool),
        "pool_scale": (pool_scale, m_pool_scale, v_pool_scale), "pre_ffn_g": (pre_ffn_g, m_pre_ffn_g, v_pre_ffn_g),
        "post_ffn_g": (post_ffn_g, m_post_ffn_g, v_post_ffn_g), "conv_w": (conv_w, m_conv_w, v_conv_w),
        "conv_b": (conv_b, m_conv_b, v_conv_b),
    }
    loss_slab, small = _small_update(g1a, g2a, g2s, gwm, gwp, gbz, gsg, gps, params)

    gmod_all = jnp.concatenate(
        [g1a[:, G1_M + M_DSHM], g1a[:, G1_M + M_DSCM], g1a[:, G1_B + B_DGM], g1a[:, G1_B + B_DSHF],
         g1a[:, G1_B + B_DSCF], g1a[:, G1_F + F_DGF]], axis=1)
    gm = lax.dynamic_slice_in_dim(gmod_all, k_me * n_ada, n_ada, axis=1)
    gm = jnp.concatenate([gm, jnp.zeros((HEAD - N_DEV, n_ada), F32)], axis=0)
    sct = jnp.concatenate([sc_all.T, jnp.zeros((d, HEAD - N_DEV), F32)], axis=1)
    ada = tuple(_wada_update(sct, gm, w_ada, m_w_ada, v_w_ada))

    parts_mix, chips_mix = _chips_end(fly_chips_mix, ada[1], "chips_end_mix")
    big.update(adamw_of(("w_in", "w_out"), _sum_chips_and_share(chips_mix, parts_mix, loss_slab, "sum_share_mix")))

    everything = dict(small)
    everything.update(big)
    everything["w_ada"] = ada
    order = ["w_ada", "b_ada", "pre_mix_g", "post_mix_g", "w_in", "sgu_norm_g", "w_spatial", "b_spatial", "w_pool",
             "pool_scale", "w_out", "pre_ffn_g", "post_ffn_g", "w_up", "conv_w", "conv_b", "w_down"]
    outs = [loss_slab[0, 0], gx.reshape(x.shape)]
    for j in range(4):
        outs += [everything[nm][j] for nm in order]
    return tuple(outs)
```

```python
import functools

import jax
import jax.numpy as jnp
from jax import lax
from jax.experimental import pallas as pl
from jax.experimental.pallas import tpu as pltpu

F32 = jnp.float32
BF16 = jnp.bfloat16
MESH = pl.DeviceIdType.MESH

EPS = 1e-6
HEAD = 128
N_HEADS = 4
A_WIDTH = N_HEADS * HEAD
CHUNK = 64
POOL_WINDOWS = (2, 4, 8, 16)
POOL_HALO = 16
SUBLANES = 8
CONV_KEEP = 2 * SUBLANES
FFN_TS = 256
MIX_TS = 1024
WGRAD_TS = 2048
WGRAD_TS_WIDE = 1024

ADAM_LR = 0.001
ADAM_B1 = 0.9
ADAM_B2 = 0.999
ADAM_EPS = 1e-08
ADAM_WD = 0.01
ADAM_STEP = 10

VMEM_LIMIT_BYTES = 58 * 1024 * 1024
N_DEV = 8
N_CHIP = 4


def _dot(a, b):
    return jnp.dot(a, b, preferred_element_type=F32)


def _dot_nt(a, b):
    return lax.dot_general(a, b, (((1,), (1,)), ((), ())), preferred_element_type=F32)


def _dot_tn(a, b):
    return lax.dot_general(a, b, (((0,), (0,)), ((), ())), preferred_element_type=F32)


GELU_C0 = 0.7978845608028654
GELU_C1 = GELU_C0 * 0.044715


def _gelu(x):
    return x * (0.5 + 0.5 * jnp.tanh(x * (GELU_C0 + GELU_C1 * (x * x))))


def _gelu_and_grad(x):
    x2 = x * x
    t = jnp.tanh(x * (GELU_C0 + GELU_C1 * x2))
    half = 0.5 + 0.5 * t
    grad = half + (x * (0.5 - 0.5 * (t * t))) * (GELU_C0 + (3.0 * GELU_C1) * x2)
    return x * half, grad


def _sigmoid(x):
    return 1.0 / (1.0 + jnp.exp(-x))


def _rms(x):
    return lax.rsqrt(jnp.mean(x * x, axis=-1, keepdims=True) + EPS)


def _colsum(x):
    return jnp.sum(x, axis=0, keepdims=True)


def _rowmean(x):
    return jnp.mean(x, axis=-1, keepdims=True)


def _tiled(shape, index_map):
    return pl.BlockSpec(shape, index_map)


def _resident(shape):
    nd = len(shape)
    return pl.BlockSpec(shape, lambda *_: (0,) * nd, pipeline_mode=pl.Buffered(1))


def _whole(shape):
    nd = len(shape)
    return pl.BlockSpec(shape, lambda *_: (0,) * nd)


def _seq_params():
    return pltpu.CompilerParams(dimension_semantics=("arbitrary",), vmem_limit_bytes=VMEM_LIMIT_BYTES)


def _ff_chunks(f, width=768):
    out, o = [], 0
    while o < f:
        w = min(width, f - o)
        out.append((o, w))
        o += w
    return out


def _pool_fwd(p, halo, g, t_glob):
    ext = jnp.concatenate([halo, p], axis=0)
    s = ext
    for step in range(g + 1):
        s = s + pltpu.roll(s, 1 << step, 0)
    cnt = jnp.minimum(t_glob + 1, POOL_WINDOWS[g]).astype(F32)
    return s[POOL_HALO:] / cnt - p


def _adamw_math(w, g, m, v):
    m = ADAM_B1 * m + (1.0 - ADAM_B1) * g
    v = ADAM_B2 * v + (1.0 - ADAM_B2) * (g * g)
    m_hat = m / (1.0 - ADAM_B1 ** ADAM_STEP)
    v_hat = v / (1.0 - ADAM_B2 ** ADAM_STEP)
    delta = -ADAM_LR * (m_hat / (jnp.sqrt(v_hat) + ADAM_EPS) + ADAM_WD * w)
    return delta, m, v


def _mesh_pos():
    return lax.axis_index("x"), lax.axis_index("y"), lax.axis_index("c")


class _Rider:
    def __init__(self, inputs, out_shape, sems, start, finish):
        self.inputs, self.out_shape, self.sems = list(inputs), list(out_shape), list(sems)
        self.start, self.finish = start, finish


def _call(body, *, name, grid, in_specs, out_specs, out_shape, scratch_shapes, args, rider=None):
    params = pltpu.CompilerParams(dimension_semantics=("arbitrary",) * len(grid), vmem_limit_bytes=VMEM_LIMIT_BYTES)
    if rider is None:
        res = pl.pallas_call(body, name=name, grid=grid, in_specs=in_specs, out_specs=out_specs, out_shape=out_shape,
                             scratch_shapes=scratch_shapes, compiler_params=params)(*args)
        return tuple(res), ()
    cuts = [len(in_specs), len(rider.inputs), len(out_specs), len(rider.out_shape), len(scratch_shapes),
            len(rider.sems)]

    def hosted(*refs):
        groups, a = [], 0
        for cnt in cuts:
            groups.append(refs[a:a + cnt])
            a += cnt
        ins, r_in, outs, r_out, scr, r_sem = groups
        first = functools.reduce(jnp.logical_and, [pl.program_id(k) == 0 for k in range(len(grid))])
        last = functools.reduce(jnp.logical_and, [pl.program_id(k) == grid[k] - 1 for k in range(len(grid))])

        @pl.when(first)
        def _():
            rider.start(r_in, r_out, r_sem)

        body(*ins, *outs, *scr)

        @pl.when(last)
        def _():
            rider.finish(r_in, r_out, r_sem)

    anyspec = pl.BlockSpec(memory_space=pl.ANY)
    res = pl.pallas_call(
        hosted, name=name, grid=grid,
        in_specs=list(in_specs) + [anyspec] * cuts[1], out_specs=list(out_specs) + [anyspec] * cuts[3],
        out_shape=list(out_shape) + rider.out_shape, scratch_shapes=list(scratch_shapes) + rider.sems,
        compiler_params=params)(*args, *rider.inputs)
    return tuple(res[:cuts[2]]), tuple(res[cuts[2]:])


def _run_rider(rider, name):
    n_in, n_out = len(rider.inputs), len(rider.out_shape)

    def body(*refs):
        r_in, r_out, r_sem = refs[:n_in], refs[n_in:n_in + n_out], refs[n_in + n_out:]
        rider.start(r_in, r_out, r_sem)
        rider.finish(r_in, r_out, r_sem)

    anyspec = pl.BlockSpec(memory_space=pl.ANY)
    return pl.pallas_call(body, name=name, out_shape=rider.out_shape, in_specs=[anyspec] * n_in,
                          out_specs=[anyspec] * n_out, scratch_shapes=rider.sems)(*rider.inputs)


def _allgather_rider(arrs):
    n = len(arrs)

    def plan(ins, outs, sems):
        send_sems, recv_sems, local_sems = sems
        x, y, c = _mesh_pos()
        me, sibling = (x, y, c), (x, y, 1 - c)
        chips = [(1 - x, y), (x, 1 - y), (1 - x, 1 - y)]

        def rows(a, px, py, pc):
            r = ins[a].shape[0]
            return outs[a].at[pl.ds(pl.multiple_of((4 * px + 2 * py + pc) * r, 8), r), :]

        def copy(a, k, block, to, src=None):
            return pltpu.make_async_remote_copy(
                src_ref=rows(a, *block) if src is None else src, dst_ref=rows(a, *block),
                send_sem=send_sems.at[a * 7 + k], recv_sem=recv_sems.at[a * 7 + k],
                device_id=to, device_id_type=MESH)

        local = [pltpu.make_async_copy(ins[a], rows(a, *me), local_sems.at[a]) for a in range(n)]
        first = []
        for a in range(n):
            first.append(copy(a, 0, me, sibling, src=ins[a]))
            first += [copy(a, 1 + j, me, (*chip, c), src=ins[a]) for j, chip in enumerate(chips)]
        return c, me, sibling, chips, copy, local, first

    def start(ins, outs, sems):
        *_, local, first = plan(ins, outs, sems)
        for cp in local + first:
            cp.start()

    def finish(ins, outs, sems):
        c, me, sibling, chips, copy, local, first = plan(ins, outs, sems)
        passed = []
        for a in range(n):
            for j, chip in enumerate(chips):
                copy(a, 1 + j, (*chip, c), me).wait_recv()
                fwd = copy(a, 4 + j, (*chip, c), sibling)
                fwd.start()
                passed.append(fwd)
        for a in range(n):
            copy(a, 0, sibling, me).wait_recv()
            for j, chip in enumerate(chips):
                copy(a, 4 + j, (*chip, 1 - c), me).wait_recv()
        for cp in first + passed:
            cp.wait_send()
        for mine in local:
            mine.wait()

    return _Rider(arrs, [jax.ShapeDtypeStruct((N_DEV * a.shape[0], a.shape[1]), a.dtype) for a in arrs],
                  [pltpu.SemaphoreType.DMA((7 * n,)), pltpu.SemaphoreType.DMA((7 * n,)),
                   pltpu.SemaphoreType.DMA((n,))], start, finish)


def _piece(ref, col_sharded, k, h):
    m, n = ref.shape
    if col_sharded:
        mh, nc = m // 2, n // N_CHIP
        return ref.at[pl.ds(pl.multiple_of(h * mh, 16), mh), pl.ds(pl.multiple_of(k * nc, 128), nc)]
    rp = m // (2 * N_CHIP)
    return ref.at[pl.ds(pl.multiple_of((2 * k + h) * rp, 16), rp), :]


def _piece_shape(shape, col_sharded):
    m, n = shape
    return (m // 2, n // N_CHIP) if col_sharded else (m // (2 * N_CHIP), n)


def _cast_bf16(arrs, col_flags, k_idx, name, after=None):
    n = len(arrs)
    extra = [] if after is None else [after]

    def body(k_ref, *refs):
        outs = refs[n + len(extra):]
        for a in range(n):
            val = refs[a][...].astype(BF16)
            outs[a][...] = val
            outs[n + a][...] = val

    whole = [pl.BlockSpec(a.shape, lambda i, k_ref: (0, 0)) for a in arrs]
    window = [pl.BlockSpec(a.shape, (lambda i, k_ref: (0, k_ref[0])) if col else (lambda i, k_ref: (k_ref[0], 0)))
              for a, col in zip(arrs, col_flags)]
    res = pl.pallas_call(
        body, name=name,
        grid_spec=pltpu.PrefetchScalarGridSpec(
            num_scalar_prefetch=1, grid=(1,),
            in_specs=whole + [pl.BlockSpec(t.shape, lambda i, k_ref: (0, 0)) for t in extra],
            out_specs=whole + window),
        out_shape=[jax.ShapeDtypeStruct(a.shape, BF16) for a in arrs]
        + [jax.ShapeDtypeStruct(fs, BF16) for fs in _full_shapes(arrs, col_flags)],
        compiler_params=pltpu.CompilerParams(vmem_limit_bytes=VMEM_LIMIT_BYTES))(k_idx, *arrs, *extra)
    return list(res[:n]), list(res[n:])


def _full_shapes(shards, col_flags):
    return [(s.shape[0], s.shape[1] * N_CHIP) if col else (s.shape[0] * N_CHIP, s.shape[1])
            for s, col in zip(shards, col_flags)]


def _ici_copies(shard_refs, full_refs, send_sems, recv_sems, col_flags):
    x, y, c = _mesh_pos()
    k_me = 2 * x + y
    copies = []
    for a, (s_ref, f_ref) in enumerate(zip(shard_refs, full_refs)):
        rows = s_ref.shape[0] // 2
        src = s_ref.at[pl.ds(pl.multiple_of(c * rows, 16), rows), :]
        for j, chip in enumerate([(1 - x, y), (x, 1 - y), (1 - x, 1 - y)]):
            copies.append(pltpu.make_async_remote_copy(
                src_ref=src, dst_ref=_piece(f_ref, col_flags[a], k_me, c),
                send_sem=send_sems.at[a * 3 + j], recv_sem=recv_sems.at[a * 3 + j],
                device_id=(*chip, c), device_id_type=MESH))
    return copies


def _split_begin(bufs, n_sems, make_copies, after, name):
    n = len(bufs)
    hbm = pl.BlockSpec(memory_space=pltpu.HBM)
    sem = pl.BlockSpec(memory_space=pltpu.SEMAPHORE)

    def body(*refs):
        for cp in make_copies(refs[:n], refs[n + 1], refs[n + 2]):
            cp.start()
        refs[-1][...] = jnp.zeros_like(refs[-1])

    args = [pltpu.with_memory_space_constraint(t, pltpu.HBM) for t in bufs]
    res = pl.pallas_call(
        body, name=name,
        out_shape=[pltpu.SemaphoreType.DMA((n_sems,)), pltpu.SemaphoreType.DMA((n_sems,))]
        + [pltpu.HBM(t.shape, t.dtype) for t in args] + [jax.ShapeDtypeStruct((8, HEAD), F32)],
        in_specs=[hbm] * n + [pl.BlockSpec(memory_space=pl.ANY)],
        out_specs=[sem, sem] + [hbm] * n + [pl.BlockSpec(memory_space=pltpu.VMEM)],
        input_output_aliases={i: 2 + i for i in range(n)},
        compiler_params=pltpu.CompilerParams(has_side_effects=pltpu.SideEffectType.DATAFLOW_SIDE_EFFECTING),
    )(*args, after)
    return res[0], res[1], list(res[2:2 + n]), res[-1]


def _split_end(handle, make_copies, after, name):
    send_sems, recv_sems, bufs, _ = handle
    n = len(bufs)
    hbm = pl.BlockSpec(memory_space=pltpu.HBM)
    sem = pl.BlockSpec(memory_space=pltpu.SEMAPHORE)

    def body(*refs):
        for cp in make_copies(refs[:n], refs[n], refs[n + 1]):
            cp.wait_send()
            cp.wait_recv()

    res = pl.pallas_call(
        body, name=name,
        out_shape=[pltpu.HBM(t.shape, t.dtype) for t in bufs],
        in_specs=[hbm] * n + [sem, sem, pl.BlockSpec(memory_space=pl.ANY)],
        out_specs=[hbm] * n,
        input_output_aliases={i: i for i in range(n)},
        compiler_params=pltpu.CompilerParams(has_side_effects=pltpu.SideEffectType.DATAFLOW_SIDE_EFFECTING),
    )(*bufs, send_sems, recv_sems, after)
    return list(res)


def _gather_copies(n, col_flags):
    return lambda refs, send_sems, recv_sems: _ici_copies(refs[:n], refs[n:], send_sems, recv_sems, col_flags)


def _gather_begin(shards, fulls, col_flags, after, name):
    n = len(shards)
    return _split_begin(list(shards) + list(fulls), 3 * n, _gather_copies(n, col_flags), after, name)


def _gather_end(handle, col_flags, after, name):
    n = len(handle[2]) // 2
    return _split_end(handle, _gather_copies(n, col_flags), after, name)[n:]


def _sibling_copies(col_flag):
    def make(refs, send_sems, recv_sems):
        grad_ref, land_ref = refs
        x, y, c = _mesh_pos()
        return [pltpu.make_async_remote_copy(
            src_ref=_piece(grad_ref, col_flag, k, 1 - c), dst_ref=land_ref.at[k],
            send_sem=send_sems.at[k], recv_sem=recv_sems.at[k],
            device_id=(x, y, 1 - c), device_id_type=MESH) for k in range(N_CHIP)]
    return make


def _chips_copies(n):
    def make(refs, send_sems, recv_sems):
        parts, landed = refs[:n], refs[n:]
        x, y, c = _mesh_pos()
        k_me = 2 * x + y
        copies = []
        for a in range(n):
            for j, chip in enumerate([(1 - x, y), (x, 1 - y), (1 - x, 1 - y)]):
                copies.append(pltpu.make_async_remote_copy(
                    src_ref=parts[a].at[2 * chip[0] + chip[1]], dst_ref=landed[a].at[k_me],
                    send_sem=send_sems.at[a * 3 + j], recv_sem=recv_sems.at[a * 3 + j],
                    device_id=(*chip, c), device_id_type=MESH))
        return copies
    return make


def _chips_begin(parts, after, name):
    landed = [lax.empty(p.shape, p.dtype) for p in parts]
    return _split_begin(list(parts) + landed, 3 * len(parts), _chips_copies(len(parts)), after, name)


def _chips_end(handle, after, name):
    n = len(handle[2]) // 2
    res = _split_end(handle, _chips_copies(n), after, name)
    return res[:n], res[n:]


def _small_rows(buf_ref, px, py, pc):
    r = buf_ref.shape[0] // N_DEV
    return buf_ref.at[pl.ds(pl.multiple_of((4 * px + 2 * py + pc) * r, 8), r), :]


def _small_ici_copies(refs, send_sems, recv_sems):
    x, y, c = _mesh_pos()
    copies = []
    for a, buf in enumerate(refs):
        mine = _small_rows(buf, x, y, c)
        for j, chip in enumerate([(1 - x, y), (x, 1 - y), (1 - x, 1 - y)]):
            copies.append(pltpu.make_async_remote_copy(
                src_ref=mine, dst_ref=mine, send_sem=send_sems.at[a * 3 + j], recv_sem=recv_sems.at[a * 3 + j],
                device_id=(*chip, c), device_id_type=MESH))
    return copies


def _small_finish(bufs, name):
    n = len(bufs)

    def body(*refs):
        buf_refs = refs[n:2 * n]
        send_sems, recv_sems = refs[2 * n:]
        x, y, c = _mesh_pos()
        owners = [(x, y), (1 - x, y), (x, 1 - y), (1 - x, 1 - y)]
        passed, arriving = [], []
        for a in range(n):
            for j, (px, py) in enumerate(owners):
                for pc, group in ((c, passed), (1 - c, arriving)):
                    rows = _small_rows(buf_refs[a], px, py, pc)
                    group.append(pltpu.make_async_remote_copy(
                        src_ref=rows, dst_ref=rows, send_sem=send_sems.at[a * 4 + j],
                        recv_sem=recv_sems.at[a * 4 + j], device_id=(x, y, 1 - c), device_id_type=MESH))
        for cp in passed:
            cp.start()
        for cp in arriving:
            cp.wait_recv()
        for cp in passed:
            cp.wait_send()

    anyspec = pl.BlockSpec(memory_space=pl.ANY)
    return pl.pallas_call(
        body, name=name, out_shape=[jax.ShapeDtypeStruct(b.shape, b.dtype) for b in bufs],
        in_specs=[anyspec] * n, out_specs=[anyspec] * n, input_output_aliases={a: a for a in range(n)},
        scratch_shapes=[pltpu.SemaphoreType.DMA((4 * n,)), pltpu.SemaphoreType.DMA((4 * n,))],
    )(*bufs)


def _sibling_begin(grad, col_flag, after, name):
    land = lax.empty((N_CHIP,) + _piece_shape(grad.shape, col_flag), grad.dtype)
    return _split_begin([grad, land], N_CHIP, _sibling_copies(col_flag), after, name)


def _sibling_end(handle, col_flag, after, name):
    return _split_end(handle, _sibling_copies(col_flag), after, name)


def _gather_finish(fulls, col_flags, name):
    n = len(fulls)

    def body(*refs):
        full_refs = refs[n:2 * n]
        send_sems, recv_sems = refs[2 * n:]
        x, y, c = _mesh_pos()
        passed, arriving = [], []
        for a in range(n):
            for j, chip in enumerate([(1 - x, y), (x, 1 - y), (1 - x, 1 - y)]):
                k_from = 2 * chip[0] + chip[1]
                for h, group in ((c, passed), (1 - c, arriving)):
                    win = _piece(full_refs[a], col_flags[a], k_from, h)
                    group.append(pltpu.make_async_remote_copy(
                        src_ref=win, dst_ref=win, send_sem=send_sems.at[a * 3 + j],
                        recv_sem=recv_sems.at[a * 3 + j], device_id=(x, y, 1 - c), device_id_type=MESH))
        for cp in passed:
            cp.start()
        for cp in arriving:
            cp.wait_recv()
        for cp in passed:
            cp.wait_send()

    anyspec = pl.BlockSpec(memory_space=pl.ANY)
    return pl.pallas_call(
        body, name=name,
        out_shape=[jax.ShapeDtypeStruct(f.shape, f.dtype) for f in fulls],
        in_specs=[anyspec] * n, out_specs=[anyspec] * n,
        input_output_aliases={a: a for a in range(n)},
        scratch_shapes=[pltpu.SemaphoreType.DMA((3 * n,)), pltpu.SemaphoreType.DMA((3 * n,))],
    )(*fulls)


def _sibling_rider(grads, col_flags):
    n = len(grads)
    pshapes = [_piece_shape(g.shape, col) for g, col in zip(grads, col_flags)]

    def copies(ins, outs, sems):
        send_sems, recv_sems = sems
        x, y, c = _mesh_pos()
        return [pltpu.make_async_remote_copy(
            src_ref=_piece(ins[a], col_flags[a], k, 1 - c), dst_ref=outs[a].at[k],
            send_sem=send_sems.at[a * N_CHIP + k], recv_sem=recv_sems.at[a * N_CHIP + k],
            device_id=(x, y, 1 - c), device_id_type=MESH) for a in range(n) for k in range(N_CHIP)]

    def start(ins, outs, sems):
        for cp in copies(ins, outs, sems):
            cp.start()

    def finish(ins, outs, sems):
        cps = copies(ins, outs, sems)
        for cp in cps:
            cp.wait_recv()
        for cp in cps:
            cp.wait_send()

    return _Rider(grads, [jax.ShapeDtypeStruct((N_CHIP,) + ps, g.dtype) for ps, g in zip(pshapes, grads)],
                  [pltpu.SemaphoreType.DMA((N_CHIP * n,)), pltpu.SemaphoreType.DMA((N_CHIP * n,))], start, finish)


def _sum_with_sibling(grads, landed, col_flags, c_idx, name):
    n = len(grads)
    pshapes = [_piece_shape(g.shape, col) for g, col in zip(grads, col_flags)]

    def body(c_ref, *refs):
        ins, lands, outs = refs[:n], refs[n:2 * n], refs[2 * n:]
        for a in range(n):
            outs[a][0] = (ins[a][...] + lands[a][0]).astype(BF16)

    in_specs = []
    for ps, col in zip(pshapes, col_flags):
        if col:
            in_specs.append(pl.BlockSpec(ps, lambda k, c_ref: (c_ref[0], k)))
        else:
            in_specs.append(pl.BlockSpec(ps, lambda k, c_ref: (2 * k + c_ref[0], 0)))
    land_specs = [pl.BlockSpec((1,) + ps, lambda k, c_ref: (k, 0, 0)) for ps in pshapes]
    return pl.pallas_call(
        body, name=name,
        grid_spec=pltpu.PrefetchScalarGridSpec(
            num_scalar_prefetch=1, grid=(N_CHIP,),
            in_specs=in_specs + land_specs, out_specs=land_specs),
        out_shape=[jax.ShapeDtypeStruct((N_CHIP,) + ps, BF16) for ps in pshapes],
        compiler_params=pltpu.CompilerParams(dimension_semantics=("arbitrary",), vmem_limit_bytes=VMEM_LIMIT_BYTES),
    )(c_idx, *grads, *landed)


def _chips_rider(parts):
    n = len(parts)

    def plan(ins, outs, sems, arriving):
        send_sems, recv_sems = sems
        x, y, c = _mesh_pos()
        k_me = 2 * x + y
        copies = []
        for a in range(n):
            for j, chip in enumerate([(1 - x, y), (x, 1 - y), (1 - x, 1 - y)]):
                k_peer = 2 * chip[0] + chip[1]
                copies.append(pltpu.make_async_remote_copy(
                    src_ref=ins[a].at[k_peer], dst_ref=outs[a].at[k_peer if arriving else k_me],
                    send_sem=send_sems.at[a * 3 + j], recv_sem=recv_sems.at[a * 3 + j],
                    device_id=(*chip, c), device_id_type=MESH))
        return copies

    def start(ins, outs, sems):
        for cp in plan(ins, outs, sems, False):
            cp.start()

    def finish(ins, outs, sems):
        arrivals = plan(ins, outs, sems, True)
        for cp in arrivals:
            cp.wait_recv()
        for cp in arrivals:
            cp.wait_send()

    return _Rider(parts, [jax.ShapeDtypeStruct(p.shape, p.dtype) for p in parts],
                  [pltpu.SemaphoreType.DMA((3 * n,)), pltpu.SemaphoreType.DMA((3 * n,))], start, finish)


def _sum_chips_and_share(landed, parts, after, name):
    n = len(landed)

    def body(*refs):
        ins, own, outs, red = refs[:n], refs[n:2 * n], refs[2 * n + 1:3 * n + 1], refs[3 * n + 1:4 * n + 1]
        send_sems, recv_sems, local_sems = refs[4 * n + 1:]
        x, y, c = _mesh_pos()
        sibling = (x, y, 1 - c)
        k_me = 2 * x + y
        copies, local = [], []
        for a in range(n):
            for k in range(N_CHIP):
                @pl.when(k_me == k)
                def _():
                    term = own[a][k].astype(F32)
                    red[a][...] = term if k == 0 else red[a][...] + term

                @pl.when(k_me != k)
                def _():
                    term = ins[a][k].astype(F32)
                    red[a][...] = term if k == 0 else red[a][...] + term

            mine = pltpu.make_async_copy(red[a], outs[a].at[c], local_sems.at[a])
            mine.start()
            local.append(mine)
            cp = pltpu.make_async_remote_copy(
                src_ref=red[a], dst_ref=outs[a].at[c],
                send_sem=send_sems.at[a], recv_sem=recv_sems.at[a],
                device_id=sibling, device_id_type=MESH)
            cp.start()
            copies.append(cp)
        for a in range(n):
            pltpu.make_async_remote_copy(
                src_ref=red[a], dst_ref=outs[a].at[1 - c],
                send_sem=send_sems.at[a], recv_sem=recv_sems.at[a],
                device_id=sibling, device_id_type=MESH).wait_recv()
        for cp in copies:
            cp.wait_send()
        for mine in local:
            mine.wait()

    return pl.pallas_call(
        body, name=name,
        out_shape=[jax.ShapeDtypeStruct((2,) + l.shape[1:], F32) for l in landed],
        in_specs=[pl.BlockSpec(memory_space=pltpu.VMEM)] * (2 * n + 1),
        out_specs=[pl.BlockSpec(memory_space=pl.ANY)] * n,
        scratch_shapes=[pltpu.VMEM(l.shape[1:], F32) for l in landed]
        + [pltpu.SemaphoreType.DMA((n,)), pltpu.SemaphoreType.DMA((n,)), pltpu.SemaphoreType.DMA((n,))],
        compiler_params=pltpu.CompilerParams(vmem_limit_bytes=VMEM_LIMIT_BYTES),
    )(*landed, *parts, after)


def _mod_shard(c_all, w_ada, b_shard):
    def body(c_ref, w_ref, b_ref, o_ref, sc_ref):
        cc = c_ref[...]
        sc = cc * _sigmoid(cc)
        sc_ref[...] = sc
        o_ref[...] = _dot(sc, w_ref[...]) + b_ref[...]

    nb, d = c_all.shape
    nn = w_ada.shape[1]
    return pl.pallas_call(
        body, name="mod_shard",
        out_shape=[jax.ShapeDtypeStruct((nb, nn), F32), jax.ShapeDtypeStruct((nb, d), F32)],
        compiler_params=pltpu.CompilerParams(vmem_limit_bytes=VMEM_LIMIT_BYTES),
    )(c_all, w_ada, b_shard)


V_SH_M, V_SC_M, V_G_M, V_SH_F, V_SC_F, V_G_F, V_PRE_MIX, V_POST_MIX, V_PRE_FFN, V_POST_FFN = range(10)


def _vrow(vec_ref, r):
    return vec_ref[r:r + 1, :]


def _mix_fwd(x, vecs, w_in_b, w_out_b, sgu_g, wm_b, bsb, wp_b, ps, pmats, ts):
    s_len, d = x.shape
    nt = s_len // ts
    n_proj = w_in_b.shape[1]

    def body(x_ref, vec_ref, win_ref, wout_ref, sg_ref, wm_ref, bs_ref, wp_ref, ps_ref, pm_ref,
             h1_ref, proj_ref, cat_ref, mixed_ref, x2_ref, h2_ref, carry_ref):
        i = pl.program_id(0)

        @pl.when(i == 0)
        def _():
            carry_ref[...] = jnp.zeros_like(carry_ref)

        sub = FFN_TS
        nsub = ts // sub

        def project(s):
            rs = slice(s * sub, (s + 1) * sub)
            x = x_ref[rs, :]
            h1 = (((x * _rms(x)) * _vrow(vec_ref, V_PRE_MIX)) * (1.0 + _vrow(vec_ref, V_SC_M))
                  + _vrow(vec_ref, V_SH_M)).astype(BF16)
            h1_ref[rs, :] = h1
            proj = _dot(h1, win_ref[...])
            proj_ref[rs, :] = proj.astype(BF16)
            return proj

        def mix(s, proj, halo):
            r0 = s * sub
            t_glob = lax.broadcasted_iota(jnp.int32, (sub, HEAD), 0) + (i * ts + r0)
            for h in range(N_HEADS):
                u = _gelu(proj[:, h * HEAD:(h + 1) * HEAD])
                v = _gelu(proj[:, A_WIDTH + h * HEAD:A_WIDTH + (h + 1) * HEAD])
                vn = ((v * _rms(v)) * sg_ref[h:h + 1, :]).astype(BF16)
                for b in range(sub // HEAD):
                    rs = slice(b * HEAD, (b + 1) * HEAD)
                    z = _dot(wm_ref[h], vn[rs]) + bs_ref[h]
                    cat_ref[r0 + b * HEAD:r0 + (b + 1) * HEAD, h * HEAD:(h + 1) * HEAD] = (u[rs] * z).astype(BF16)
            for g in range(len(POOL_WINDOWS)):
                gs = slice(g * HEAD, (g + 1) * HEAD)
                p = proj[:, 2 * A_WIDTH + g * HEAD:2 * A_WIDTH + (g + 1) * HEAD]
                pooled = _pool_fwd(p, halo[:, gs], g, t_glob)
                yb = _dot(pooled.astype(BF16), wp_ref[g]) * ps_ref[0:1, gs]
                cat_ref[r0:r0 + sub, A_WIDTH + g * HEAD:A_WIDTH + (g + 1) * HEAD] = yb.astype(BF16)

        def finish(s, mixed):
            rs = slice(s * sub, (s + 1) * sub)
            mixed_ref[rs, :] = mixed
            x2 = x_ref[rs, :] + _vrow(vec_ref, V_G_M) * ((mixed * _rms(mixed)) * _vrow(vec_ref, V_POST_MIX))
            x2_ref[rs, :] = x2
            h2 = (((x2 * _rms(x2)) * _vrow(vec_ref, V_PRE_FFN)) * (1.0 + _vrow(vec_ref, V_SC_F))
                  + _vrow(vec_ref, V_SH_F)).astype(BF16)
            h2_ref[rs, :] = _permute_bf16(pm_ref[0], h2)

        projs = [project(0)]
        halo = carry_ref[...]
        mixed_prev = None
        for s in range(nsub):
            if s + 1 < nsub:
                projs.append(project(s + 1))
            mix(s, projs[s], halo)
            halo = projs[s][sub - POOL_HALO:sub, 2 * A_WIDTH:]
            mixed = _dot(cat_ref[s * sub:(s + 1) * sub, :], wout_ref[...])
            if mixed_prev is not None:
                finish(s - 1, mixed_prev)
            mixed_prev = mixed
        carry_ref[...] = halo
        finish(nsub - 1, mixed_prev)

    row = lambda i: (i, 0)
    return _call(
        body, name="mix_fwd", grid=(nt,),
        in_specs=[_tiled((ts, d), row), _whole(vecs.shape), _resident(w_in_b.shape), _resident(w_out_b.shape),
                  _whole(sgu_g.shape), _whole(wm_b.shape), _whole(bsb.shape), _whole(wp_b.shape), _whole(ps.shape),
                  _whole(pmats.shape)],
        out_specs=[_tiled((ts, d), row), _tiled((ts, n_proj), row), _tiled((ts, d), row),
                   _tiled((ts, d), row), _tiled((ts, d), row), _tiled((ts, d), row)],
        out_shape=[jax.ShapeDtypeStruct((s_len, d), BF16), jax.ShapeDtypeStruct((s_len, n_proj), BF16),
                   jax.ShapeDtypeStruct((s_len, d), BF16), jax.ShapeDtypeStruct((s_len, d), F32),
                   jax.ShapeDtypeStruct((s_len, d), F32), jax.ShapeDtypeStruct((s_len, d), BF16)],
        scratch_shapes=[pltpu.VMEM((POOL_HALO, A_WIDTH), F32)],
        args=(x, vecs, w_in_b, w_out_b, sgu_g, wm_b, bsb, wp_b, ps, pmats))


def _perm_mats(ts):
    p = jnp.arange(ts)
    pm = (((p % SUBLANES) * (ts // SUBLANES) + p // SUBLANES)[:, None] == p[None, :]).astype(BF16)
    return jnp.stack([pm, pm.T])


def _permute_bf16(pm, xb):
    return _dot(pm, xb).astype(BF16)


def _permute_f32(pm, x):
    hi = x.astype(BF16)
    lo = (x - hi.astype(F32)).astype(BF16)
    return _dot(pm, hi) + _dot(pm, lo)


def _conv_out(u, um2, um1, cv_ref, cols):
    return (cv_ref[3:4, cols] + um2 * cv_ref[0:1, cols] + um1 * cv_ref[1:2, cols] + u * cv_ref[2:3, cols])


F_LOSS, F_DGF, F_DPOSTFFN = 0, 1, 2
B_DSHF, B_DSCF, B_DPREFFN, B_DGM, B_DPOSTMIX = 0, 1, 2, 3, 4
M_DSHM, M_DSCM, M_DPREMIX = 0, 1, 2
C_DCB, C_DCW = 0, 1
G1_F, G1_B, G1_M = 0, 8, 16


def _ffn_fwd(h2p, x2, tgt, w_up_b, w_down_b, cvec, vecs, pmats, ts):
    s_len, d = x2.shape
    ff2 = w_up_b.shape[1]
    ff = ff2 // 2
    nt = s_len // ts
    chunks = _ff_chunks(ff)

    def body(h2_ref, x2_ref, t_ref, wu_ref, wd_ref, cv_ref, vec_ref, pm_ref,
             up_ref, y_ref, act_ref, dy_ref, df_ref, acc_ref, carry_ref):
        @pl.when(pl.program_id(0) == 0)
        def _():
            carry_ref[...] = jnp.zeros_like(carry_ref)
            acc_ref[...] = jnp.zeros_like(acc_ref)

        h2v = h2_ref[...]

        def up_dots(o, w):
            return [_dot(h2v, wu_ref[:, base + o:base + o + w]) for base in (0, ff)]

        f = None
        pending = None
        nxt = up_dots(*chunks[0])
        for ci, (o, w) in enumerate(chunks):
            us = nxt
            if ci + 1 < len(chunks):
                nxt = up_dots(*chunks[ci + 1])
            if pending is not None:
                part = _dot(pending[0], wd_ref[pending[1]:pending[1] + pending[2], :])
                f = part if f is None else f + part
            sub0 = lax.broadcasted_iota(jnp.int32, (SUBLANES, w), 0) == 0
            ys = []
            for base, u in zip((0, ff), us):
                cols = slice(base + o, base + o + w)
                up_ref[:, cols] = u.astype(BF16)
                last1, last2 = u[ts - SUBLANES:ts], u[ts - CONV_KEEP:ts - SUBLANES]
                b1 = jnp.where(sub0, pltpu.roll(carry_ref[SUBLANES:CONV_KEEP, cols], 1, 0), pltpu.roll(last1, 1, 0))
                b2 = jnp.where(sub0, pltpu.roll(carry_ref[0:SUBLANES, cols], 1, 0), pltpu.roll(last2, 1, 0))
                um1 = jnp.concatenate([b1, u[:ts - SUBLANES]], axis=0)
                um2 = jnp.concatenate([b2, b1, u[:ts - CONV_KEEP]], axis=0)
                ys.append(_conv_out(u, um2, um1, cv_ref, cols))
                carry_ref[:, cols] = u[ts - CONV_KEEP:ts]
            gate, val = ys
            sg = _sigmoid(gate)
            gs = gate * sg
            act = (gs * val).astype(BF16)
            act_ref[:, o:o + w] = act
            y_ref[:, o:o + w] = (val * (sg + gs * (1.0 - sg))).astype(BF16)
            y_ref[:, ff + o:ff + o + w] = gs.astype(BF16)
            pending = (act, o, w)
        f = f + _dot(pending[0], wd_ref[pending[1]:pending[1] + pending[2], :])
        f = _permute_f32(pm_ref[1], f)
        r3 = _rms(f)
        fhat = f * r3
        post = _vrow(vec_ref, V_POST_FFN)
        g_f = _vrow(vec_ref, V_G_F)
        fn = fhat * post
        e = (x2_ref[...] + g_f * fn) - t_ref[...]
        dy = e * (1.0 / d)
        dy_ref[...] = dy
        dfn = dy * g_f
        acc_ref[F_LOSS:F_LOSS + 1, :] += _colsum(e * e)
        acc_ref[F_DGF:F_DGF + 1, :] += _colsum(dy * fn)
        acc_ref[F_DPOSTFFN:F_DPOSTFFN + 1, :] += _colsum(dfn * fhat)
        dfhat = dfn * post
        df = (r3 * (dfhat - fhat * _rowmean(dfhat * fhat))).astype(BF16)
        df_ref[...] = _permute_bf16(pm_ref[0], df)

    row = lambda i: (i, 0)
    return pl.pallas_call(
        body, name="ffn_fwd", grid=(nt,),
        in_specs=[_tiled((ts, d), row), _tiled((ts, d), row), _tiled((ts, d), row), _resident(w_up_b.shape),
                  _resident(w_down_b.shape), _whole(cvec.shape), _whole(vecs.shape), _whole(pmats.shape)],
        out_specs=[_tiled((ts, ff2), row), _tiled((ts, ff2), row), _tiled((ts, ff), row), _tiled((ts, d), row),
                   _tiled((ts, d), row), _whole((8, d))],
        out_shape=[jax.ShapeDtypeStruct((s_len, ff2), BF16), jax.ShapeDtypeStruct((s_len, ff2), BF16),
                   jax.ShapeDtypeStruct((s_len, ff), BF16), jax.ShapeDtypeStruct((s_len, d), F32),
                   jax.ShapeDtypeStruct((s_len, d), BF16), jax.ShapeDtypeStruct((8, d), F32)],
        scratch_shapes=[pltpu.VMEM((CONV_KEEP, ff2), F32)],
        compiler_params=_seq_params(),
    )(h2p, x2, tgt, w_up_b, w_down_b, cvec, vecs, pmats)


def _ffn_bwd(dfp, upp, yp, x2, dy, mixed, w_up_b, w_down_b, cvec, vecs, pmats, ts, rider=None):
    s_len, d = x2.shape
    ff2 = w_up_b.shape[1]
    ff = ff2 // 2
    nt = s_len // ts
    chunks = _ff_chunks(ff, 512)

    def body(df_ref, up_ref, y_ref, x2_ref, dy_ref, mx_ref, wu_ref, wd_ref, cv_ref, vec_ref, pm_ref,
             dup_ref, dx2_ref, dmx_ref, accc_ref, acc_ref, carry_ref):
        @pl.when(pl.program_id(0) == 0)
        def _():
            carry_ref[...] = jnp.zeros_like(carry_ref)
            accc_ref[...] = jnp.zeros_like(accc_ref)
            acc_ref[...] = jnp.zeros_like(acc_ref)

        dfv = df_ref[...]

        def dh2_add(acc, dups, o, w):
            for base, dup in zip((0, ff), dups):
                part = _dot_nt(dup, wu_ref[:, base + o:base + o + w])
                acc = part if acc is None else acc + part
            return acc

        dh2 = None
        pending = None
        nxt = _dot_nt(dfv, wd_ref[chunks[0][0]:chunks[0][0] + chunks[0][1], :])
        for ci, (o, w) in enumerate(chunks):
            dact = nxt
            if ci + 1 < len(chunks):
                o2, w2 = chunks[ci + 1]
                nxt = _dot_nt(dfv, wd_ref[o2:o2 + w2, :])
            if pending is not None:
                dh2 = dh2_add(dh2, *pending)
            sub7 = lax.broadcasted_iota(jnp.int32, (SUBLANES, w), 0) == SUBLANES - 1
            dups = []
            dys = (dact * y_ref[:, o:o + w].astype(F32), dact * y_ref[:, ff + o:ff + o + w].astype(F32))
            for base, dyv in zip((0, ff), dys):
                cols = slice(base + o, base + o + w)
                u = up_ref[:, cols].astype(F32)
                up1 = SUBLANES - 1
                e0 = jnp.where(sub7, pltpu.roll(carry_ref[0:SUBLANES, cols], up1, 0),
                               pltpu.roll(dyv[0:SUBLANES], up1, 0))
                e1 = jnp.where(sub7, pltpu.roll(carry_ref[SUBLANES:CONV_KEEP, cols], up1, 0),
                               pltpu.roll(dyv[SUBLANES:CONV_KEEP], up1, 0))
                dyp1 = jnp.concatenate([dyv[SUBLANES:], e0], axis=0)
                dyp2 = jnp.concatenate([dyv[CONV_KEEP:], e0, e1], axis=0)
                accc_ref[C_DCB:C_DCB + 1, cols] += _colsum(dyv)
                accc_ref[C_DCW + 0:C_DCW + 1, cols] += _colsum(dyp2 * u)
                accc_ref[C_DCW + 1:C_DCW + 2, cols] += _colsum(dyp1 * u)
                accc_ref[C_DCW + 2:C_DCW + 3, cols] += _colsum(dyv * u)
                dup = (dyv * cv_ref[2:3, cols] + dyp1 * cv_ref[1:2, cols] + dyp2 * cv_ref[0:1, cols]).astype(BF16)
                dup_ref[:, cols] = dup
                dups.append(dup)
                carry_ref[:, cols] = dyv[0:CONV_KEEP]
            pending = (dups, o, w)
        dh2 = dh2_add(dh2, *pending)
        dh2 = _permute_f32(pm_ref[1], dh2)
        x2 = x2_ref[...]
        r2 = _rms(x2)
        xn = x2 * r2
        pre = _vrow(vec_ref, V_PRE_FFN)
        one_sc = 1.0 + _vrow(vec_ref, V_SC_F)
        acc_ref[B_DSHF:B_DSHF + 1, :] += _colsum(dh2)
        acc_ref[B_DSCF:B_DSCF + 1, :] += _colsum(dh2 * (xn * pre))
        acc_ref[B_DPREFFN:B_DPREFFN + 1, :] += _colsum(dh2 * xn * one_sc)
        dxn = dh2 * pre * one_sc
        dx2 = dy_ref[...] + r2 * (dxn - xn * _rowmean(dxn * xn))
        dx2_ref[...] = dx2
        mixed = mx_ref[...]
        rm = _rms(mixed)
        mhat = mixed * rm
        post = _vrow(vec_ref, V_POST_MIX)
        acc_ref[B_DGM:B_DGM + 1, :] += _colsum(dx2 * (mhat * post))
        dmn = dx2 * _vrow(vec_ref, V_G_M)
        acc_ref[B_DPOSTMIX:B_DPOSTMIX + 1, :] += _colsum(dmn * mhat)
        dmhat = dmn * post
        dmx_ref[...] = (rm * (dmhat - mhat * _rowmean(dmhat * mhat))).astype(BF16)

    rev = lambda i: (nt - 1 - i, 0)
    return _call(
        body, name="ffn_bwd", grid=(nt,), rider=rider,
        in_specs=[_tiled((ts, d), rev), _tiled((ts, ff2), rev), _tiled((ts, ff2), rev), _tiled((ts, d), rev),
                  _tiled((ts, d), rev), _tiled((ts, d), rev), _resident(w_up_b.shape), _resident(w_down_b.shape),
                  _whole(cvec.shape), _whole(vecs.shape), _whole(pmats.shape)],
        out_specs=[_tiled((ts, ff2), rev), _tiled((ts, d), rev), _tiled((ts, d), rev), _whole((8, ff2)),
                   _whole((8, d))],
        out_shape=[jax.ShapeDtypeStruct((s_len, ff2), BF16), jax.ShapeDtypeStruct((s_len, d), F32),
                   jax.ShapeDtypeStruct((s_len, d), BF16), jax.ShapeDtypeStruct((8, ff2), F32),
                   jax.ShapeDtypeStruct((8, d), F32)],
        scratch_shapes=[pltpu.VMEM((CONV_KEEP, ff2), F32)],
        args=(dfp, upp, yp, x2, dy, mixed, w_up_b, w_down_b, cvec, vecs, pmats))


def _mix_bwd(dmixed, dx2, x, proj, vecs, w_in_b, w_out_b, sgu_g, wm_b, wmt_b, bsb, wp_b, ps, ts):
    s_len, d = x.shape
    nt = s_len // ts
    n_proj = proj.shape[1]
    per = ts // POOL_HALO

    def body(dmx_ref, dx2_ref, x_ref, proj_ref, projh_ref, vec_ref, win_ref, wout_ref, sg_ref, wm_ref, wmt_ref,
             bs_ref, wp_ref, ps_ref,
             gx_ref, dproj_ref, acc_ref, dwm_out, dwp_out, db_ref, dg_ref, dps_ref,
             carry_ref, dwm_ref, dwp_ref, dbz_ref):
        i = pl.program_id(0)
        tile = nt - 1 - i

        @pl.when(i == 0)
        def _():
            carry_ref[...] = jnp.zeros_like(carry_ref)
            for r in (acc_ref, dwm_ref, dwp_ref, dbz_ref, dg_ref, dps_ref):
                r[...] = jnp.zeros_like(r)

        sub = FFN_TS
        nsub = ts // sub
        ext = sub + POOL_HALO

        def cotangent(s):
            return _dot_nt(dmx_ref[s * sub:(s + 1) * sub, :], wout_ref[...])

        def back(s, dcat, halo, later):
            r0 = s * sub
            rows = slice(r0, r0 + sub)
            t_glob = lax.broadcasted_iota(jnp.int32, (sub, HEAD), 0) + (tile * ts + r0)
            for h in range(N_HEADS):
                hs = slice(h * HEAD, (h + 1) * HEAD)
                vs = slice(A_WIDTH + h * HEAD, A_WIDTH + (h + 1) * HEAD)
                gain = sg_ref[h:h + 1, :]
                for b in range(sub // HEAD):
                    blk = slice(r0 + b * HEAD, r0 + (b + 1) * HEAD)
                    au = proj_ref[blk, hs].astype(F32)
                    av = proj_ref[blk, vs].astype(F32)
                    u, u_grad = _gelu_and_grad(au)
                    v, v_grad = _gelu_and_grad(av)
                    rv = _rms(v)
                    vhat = v * rv
                    vn = (vhat * gain).astype(BF16)
                    dout = dcat[b * HEAD:(b + 1) * HEAD, hs]
                    z = _dot(wm_ref[h], vn) + bs_ref[h]
                    dz = dout * u
                    dbz_ref[h] += dz
                    dzb = dz.astype(BF16)
                    dwm_ref[h] += _dot_nt(dzb, vn)
                    dvn = _dot(wmt_ref[h], dzb)
                    dg_ref[h:h + 1, :] += _colsum(dvn * vhat)
                    dvhat = dvn * gain
                    dv = rv * (dvhat - vhat * _rowmean(dvhat * vhat))
                    dproj_ref[blk, hs] = ((dout * z) * u_grad).astype(BF16)
                    dproj_ref[blk, vs] = (dv * v_grad).astype(BF16)
            firsts = []
            for g in range(len(POOL_WINDOWS)):
                gs = slice(g * HEAD, (g + 1) * HEAD)
                pcols = slice(2 * A_WIDTH + g * HEAD, 2 * A_WIDTH + (g + 1) * HEAD)
                p = proj_ref[rows, pcols].astype(F32)
                pb = _pool_fwd(p, halo[:, gs], g, t_glob).astype(BF16)
                dyb = dcat[:, A_WIDTH + g * HEAD:A_WIDTH + (g + 1) * HEAD]
                dps_ref[0:1, gs] += _colsum(dyb * _dot(pb, wp_ref[g]))
                dyl = (dyb * ps_ref[0:1, gs]).astype(BF16)
                dwp_ref[g] += _dot_tn(pb, dyl)
                dpooled = _dot_nt(dyl, wp_ref[g])
                cnt = jnp.minimum(t_glob + 1, POOL_WINDOWS[g]).astype(F32)
                q = dpooled / cnt
                acc = jnp.concatenate([q, later[:, gs]], axis=0)
                for step in range(g + 1):
                    acc = acc + pltpu.roll(acc, ext - (1 << step), 0)
                dproj_ref[rows, pcols] = (acc[:sub] - dpooled).astype(BF16)
                firsts.append(q[0:POOL_HALO])
            return jnp.concatenate(firsts, axis=1)

        def finish(s, dh1):
            rows = slice(s * sub, (s + 1) * sub)
            x = x_ref[rows, :]
            r1 = _rms(x)
            xn = x * r1
            pre = _vrow(vec_ref, V_PRE_MIX)
            one_sc = 1.0 + _vrow(vec_ref, V_SC_M)
            acc_ref[M_DSHM:M_DSHM + 1, :] += _colsum(dh1)
            acc_ref[M_DSCM:M_DSCM + 1, :] += _colsum(dh1 * (xn * pre))
            acc_ref[M_DPREMIX:M_DPREMIX + 1, :] += _colsum(dh1 * xn * one_sc)
            dxn = dh1 * pre * one_sc
            gx_ref[rows, :] = dx2_ref[rows, :] + r1 * (dxn - xn * _rowmean(dxn * xn))

        nxt = cotangent(nsub - 1)
        later = carry_ref[...]
        dh1_prev = None
        for s in reversed(range(nsub)):
            dcat = nxt
            if s > 0:
                nxt = cotangent(s - 1)
                halo = proj_ref[s * sub - POOL_HALO:s * sub, 2 * A_WIDTH:].astype(F32)
            else:
                halo = jnp.where(tile > 0, projh_ref[:, 2 * A_WIDTH:].astype(F32), 0.0)
            later = back(s, dcat, halo, later)
            dh1 = _dot_nt(dproj_ref[s * sub:(s + 1) * sub, :], win_ref[...])
            if dh1_prev is not None:
                finish(s + 1, dh1_prev)
            dh1_prev = dh1
        carry_ref[...] = later
        finish(0, dh1_prev)

        @pl.when(i == nt - 1)
        def _():
            dwm_out[...] = dwm_ref[...].astype(BF16)
            dwp_out[...] = dwp_ref[...].astype(BF16)
            db_ref[...] = jnp.zeros_like(db_ref)
            for h in range(N_HEADS):
                db_ref[h:h + 1, :] = jnp.sum(dbz_ref[h].T, axis=0, keepdims=True)

    rev = lambda i: (nt - 1 - i, 0)
    halo_map = lambda i: (jnp.maximum((nt - 1 - i) * per - 1, 0), 0)
    hshape = (N_HEADS, HEAD, HEAD)
    return _call(
        body, name="mix_bwd", grid=(nt,),
        in_specs=[_tiled((ts, d), rev), _tiled((ts, d), rev), _tiled((ts, d), rev), _tiled((ts, n_proj), rev),
                  _tiled((POOL_HALO, n_proj), halo_map), _whole(vecs.shape), _resident(w_in_b.shape),
                  _resident(w_out_b.shape), _whole(sgu_g.shape), _whole(wm_b.shape), _whole(wmt_b.shape),
                  _whole(bsb.shape), _whole(wp_b.shape), _whole(ps.shape)],
        out_specs=[_tiled((ts, d), rev), _tiled((ts, n_proj), rev), _whole((8, d)), _whole(hshape), _whole(hshape),
                   _whole((8, HEAD)), _whole((8, HEAD)), _whole((8, A_WIDTH))],
        out_shape=[jax.ShapeDtypeStruct((s_len, d), F32), jax.ShapeDtypeStruct((s_len, n_proj), BF16),
                   jax.ShapeDtypeStruct((8, d), F32), jax.ShapeDtypeStruct(hshape, BF16),
                   jax.ShapeDtypeStruct(hshape, BF16), jax.ShapeDtypeStruct((8, HEAD), F32),
                   jax.ShapeDtypeStruct((8, HEAD), F32), jax.ShapeDtypeStruct((8, A_WIDTH), F32)],
        scratch_shapes=[pltpu.VMEM((POOL_HALO, A_WIDTH), F32), pltpu.VMEM(hshape, F32), pltpu.VMEM(hshape, F32),
                        pltpu.VMEM(hshape, F32)],
        args=(dmixed, dx2, x, proj, proj, vecs, w_in_b, w_out_b, sgu_g, wm_b, wmt_b, bsb, wp_b, ps))


def _wgrad(a, b, tn, ts, name, rider=None):
    s_len, m = a.shape
    n = b.shape[1]
    ts = min(ts, s_len)

    def body(a_ref, b_ref, o_ref):
        @pl.when(pl.program_id(1) == 0)
        def _():
            o_ref[...] = jnp.zeros_like(o_ref)

        o_ref[...] += _dot_tn(a_ref[...], b_ref[...])

    (g,), r_out = _call(
        body, name=name, grid=(n // tn, s_len // ts), rider=rider,
        in_specs=[pl.BlockSpec((ts, m), lambda j, s: (s, 0)), pl.BlockSpec((ts, tn), lambda j, s: (s, j))],
        out_specs=[pl.BlockSpec((m, tn), lambda j, s: (0, j))],
        out_shape=[jax.ShapeDtypeStruct((m, n), F32)], scratch_shapes=[], args=(a, b))
    return g, r_out


def _wgrad_pair(a1, b1, a2, b2, after, ts, name):
    s_len = a1.shape[0]
    ts = min(ts, s_len)
    shapes = [(a1.shape[1], b1.shape[1]), (a2.shape[1], b2.shape[1])]

    def body(a1_ref, b1_ref, a2_ref, b2_ref, after_ref, o1_ref, o2_ref):
        @pl.when(pl.program_id(0) == 0)
        def _():
            o1_ref[...] = jnp.zeros_like(o1_ref)
            o2_ref[...] = jnp.zeros_like(o2_ref)

        o1_ref[...] += _dot_tn(a1_ref[...], b1_ref[...])
        o2_ref[...] += _dot_tn(a2_ref[...], b2_ref[...])

    row = lambda s: (s, 0)
    return _call(
        body, name=name, grid=(s_len // ts,),
        in_specs=[pl.BlockSpec((ts, t.shape[1]), row) for t in (a1, b1, a2, b2)] + [_whole(after.shape)],
        out_specs=[_whole(sh) for sh in shapes],
        out_shape=[jax.ShapeDtypeStruct(sh, F32) for sh in shapes], scratch_shapes=[],
        args=(a1, b1, a2, b2, after))[0]


def _adamw_big(g, w, m, v, name):
    r, cdim = g.shape
    tr = r
    while tr * cdim * 4 > (3 << 19) and tr % 16 == 0:
        tr //= 2

    def body(g_ref, w_ref, m_ref, v_ref, d_ref, nm_ref, nv_ref):
        delta, m2, v2 = _adamw_math(w_ref[0], g_ref[...], m_ref[0], v_ref[0])
        d_ref[0] = delta
        nm_ref[0] = m2
        nv_ref[0] = v2

    s3 = pl.BlockSpec((1, tr, cdim), lambda i: (0, i, 0))
    return pl.pallas_call(
        body, name=name, grid=(r // tr,),
        in_specs=[pl.BlockSpec((tr, cdim), lambda i: (i, 0)), s3, s3, s3],
        out_specs=[s3, s3, s3],
        out_shape=[jax.ShapeDtypeStruct(w.shape, F32)] * 3,
        compiler_params=pltpu.CompilerParams(dimension_semantics=("parallel",), vmem_limit_bytes=VMEM_LIMIT_BYTES),
    )(g, w, m, v)


def _wada_update(sct, gm, w, m, v):
    _, r, cdim = w.shape
    tr = 256
    kp = sct.shape[1]

    def body(s_ref, g_ref, w_ref, m_ref, v_ref, gw_ref, d_ref, nm_ref, nv_ref):
        g = _dot(s_ref[...], g_ref[...])
        gw_ref[0] = g
        delta, m2, v2 = _adamw_math(w_ref[0], g, m_ref[0], v_ref[0])
        d_ref[0] = delta
        nm_ref[0] = m2
        nv_ref[0] = v2

    s3 = pl.BlockSpec((1, tr, cdim), lambda i: (0, i, 0))
    return pl.pallas_call(
        body, name="wada_update", grid=(r // tr,),
        in_specs=[pl.BlockSpec((tr, kp), lambda i: (i, 0)), _whole(gm.shape), s3, s3, s3],
        out_specs=[s3, s3, s3, s3],
        out_shape=[jax.ShapeDtypeStruct(w.shape, F32)] * 4,
        compiler_params=pltpu.CompilerParams(dimension_semantics=("parallel",), vmem_limit_bytes=VMEM_LIMIT_BYTES),
    )(sct, gm, w, m, v)


def _small_update(g1, g2, g2s, gwm, gwp, gbz, gsg, gps, params):
    names = ["b_ada", "pre_mix_g", "post_mix_g", "sgu_norm_g", "w_spatial", "b_spatial", "w_pool", "pool_scale",
             "pre_ffn_g", "post_ffn_g", "conv_w", "conv_b"]
    d = g1.shape[2]
    flat_in = [g1, g2, g2s, gwm, gwp, gbz, gsg, gps]
    n_g = len(flat_in)
    for nm in names:
        flat_in += list(params[nm])

    def body(*refs):
        g1_ref, g2_ref, g2s_ref, gwm_ref, gwp_ref, gbz_ref, gsg_ref, gps_ref = refs[:n_g]
        wmv = refs[n_g:n_g + 3 * len(names)]
        loss_ref = refs[n_g + 3 * len(names)]
        outs = refs[n_g + 3 * len(names) + 1:]

        def dsum(ref, idx):
            acc = ref[(0,) + idx].astype(F32)
            for dev in range(1, N_DEV):
                acc = acc + ref[(dev,) + idx].astype(F32)
            return acc

        def apply(pi, g, widx, oidx):
            w_ref, m_ref, v_ref = wmv[3 * pi:3 * pi + 3]
            g_ref, d_ref, nm_ref, nv_ref = outs[4 * pi:4 * pi + 4]
            delta, m2, v2 = _adamw_math(w_ref[widx], g, m_ref[widx], v_ref[widx])
            g_ref[oidx] = g
            d_ref[oidx] = delta
            nm_ref[oidx] = m2
            nv_ref[oidx] = v2

        def row1(base, r):
            return (slice(base + r, base + r + 1), slice(None))

        tot = dsum(g1_ref, row1(G1_F, F_LOSS))
        loss_ref[...] = jnp.zeros(loss_ref.shape, F32) + jnp.sum(tot) * (0.5 / d)
        mod_rows = [row1(G1_M, M_DSHM), row1(G1_M, M_DSCM), row1(G1_B, B_DGM), row1(G1_B, B_DSHF),
                    row1(G1_B, B_DSCF), row1(G1_F, F_DGF)]
        for j, rr in enumerate(mod_rows):
            cs = (slice(None), slice(j * d, (j + 1) * d))
            apply(0, dsum(g1_ref, rr), cs, cs)
        full = (slice(None), slice(None))
        apply(1, dsum(g1_ref, row1(G1_M, M_DPREMIX)), full, full)
        apply(2, dsum(g1_ref, row1(G1_B, B_DPOSTMIX)), full, full)
        apply(3, dsum(gsg_ref, (slice(0, N_HEADS), slice(None))), (0,), (0,))
        pos_i = lax.broadcasted_iota(jnp.int32, (HEAD, HEAD), 0)
        pos_j = lax.broadcasted_iota(jnp.int32, (HEAD, HEAD), 1)
        causal = (pos_j // CHUNK) <= (pos_i // CHUNK)
        for h in range(N_HEADS):
            blk = (slice(h * HEAD, (h + 1) * HEAD), slice(None))
            apply(4, jnp.where(causal, dsum(gwm_ref, blk), 0.0), (0, h), (0, h))
            apply(5, dsum(gbz_ref, (slice(h, h + 1), slice(None))), (0, slice(h, h + 1)), (0, slice(h, h + 1)))
            apply(6, dsum(gwp_ref, blk), (0, h), (0, h))
        apply(7, dsum(gps_ref, (slice(0, 1), slice(None))), full, full)
        apply(8, dsum(g1_ref, row1(G1_B, B_DPREFFN)), full, full)
        apply(9, dsum(g1_ref, row1(G1_F, F_DPOSTFFN)), full, full)
        apply(10, dsum(g2s_ref, (slice(C_DCW, C_DCW + 3), slice(None))), (0,), (0,))
        apply(11, dsum(g2_ref, (slice(C_DCB, C_DCB + 1), slice(None))), full, full)

    out_shape = [jax.ShapeDtypeStruct((8, HEAD), F32)]
    for nm in names:
        out_shape += [jax.ShapeDtypeStruct(params[nm][0].shape, F32)] * 4
    res = pl.pallas_call(
        body, name="small_update", out_shape=out_shape,
        compiler_params=pltpu.CompilerParams(vmem_limit_bytes=VMEM_LIMIT_BYTES),
    )(*flat_in)
    out = {nm: tuple(res[1 + 4 * i:5 + 4 * i]) for i, nm in enumerate(names)}
    return res[0], out


def kernel(x, c, w_ada, b_ada, pre_mix_g, post_mix_g, w_in, sgu_norm_g, w_spatial, b_spatial, w_pool, pool_scale, w_out, pre_ffn_g, post_ffn_g, w_up, conv_w, conv_b, w_down, loss_target, m_w_ada, m_b_ada, m_pre_mix_g, m_post_mix_g, m_w_in, m_sgu_norm_g, m_w_spatial, m_b_spatial, m_w_pool, m_pool_scale, m_w_out, m_pre_ffn_g, m_post_ffn_g, m_w_up, m_conv_w, m_conv_b, m_w_down, v_w_ada, v_b_ada, v_pre_mix_g, v_post_mix_g, v_w_in, v_sgu_norm_g, v_w_spatial, v_b_spatial, v_w_pool, v_pool_scale, v_w_out, v_pre_ffn_g, v_post_ffn_g, v_w_up, v_conv_w, v_conv_b, v_w_down):
    xi, yi, ci = _mesh_pos()
    k_me = 2 * xi + yi
    dev = 2 * k_me + ci
    s_len, d = x.shape[1], x.shape[2]
    x2d = x[0]
    tgt = loss_target[0]
    ff2 = conv_b.shape[1]
    n_ada = w_ada.shape[2]
    n_cw = conv_w.shape[2]

    k_idx = k_me.reshape(1).astype(jnp.int32)
    mix_flags = (True, False)
    (w_in_s, w_out_s), (w_in_f, w_out_f) = _cast_bf16([w_in[0], w_out[0]], mix_flags, k_idx, "cast_weights_mix")

    def place(t):
        return lax.dynamic_update_slice(jnp.zeros((N_DEV * t.shape[0], t.shape[1]), t.dtype), t,
                                        (dev * t.shape[0], 0))

    cw_blk = jnp.concatenate([conv_w[0], jnp.zeros((5, n_cw), F32)], axis=0)
    fly_c = _split_begin([place(c.reshape(8, d // 8)), place(cw_blk)], 6, _small_ici_copies, k_idx,
                         "c_gather_begin")
    fly_mix = _gather_begin([w_in_s, w_out_s], [w_in_f, w_out_f], mix_flags, fly_c[3], "gather_begin_mix")
    c_all, cw_all = _small_finish(_split_end(fly_c, _small_ici_copies, fly_mix[3], "c_gather_end"),
                                  "c_gather_finish")
    (w_up_s, w_down_s), (w_up_f, w_down_f) = _cast_bf16([w_up[0], w_down[0]], mix_flags, k_idx, "cast_weights_ffn",
                                                        after=fly_mix[3])
    c_all = c_all.reshape(N_DEV, 8, d // 8).reshape(N_DEV, d)
    cw_full = jnp.concatenate([cw_all[16 * k:16 * k + 8] for k in range(N_CHIP)], axis=1)
    cvec = jnp.concatenate([cw_full[0:3], conv_b, jnp.zeros((4, ff2), F32)], axis=0)
    b_shard = lax.dynamic_slice_in_dim(b_ada, k_me * n_ada, n_ada, axis=1)
    mod_k, sc_all = _mod_shard(c_all, w_ada[0], b_shard)
    (mod_g,) = _run_rider(_allgather_rider([mod_k]), "gather_mod")
    mod_all = jnp.concatenate([mod_g[16 * k:16 * k + 8] for k in range(N_CHIP)], axis=1)
    mod_me = lax.dynamic_slice_in_dim(mod_all, dev, 1, axis=0).reshape(6, d)
    vecs = jnp.concatenate([mod_me, pre_mix_g, post_mix_g, pre_ffn_g, post_ffn_g, jnp.zeros((6, d), F32)], axis=0)

    fly_ffn = _gather_begin([w_up_s, w_down_s], [w_up_f, w_down_f], mix_flags, mod_g, "gather_begin_ffn")
    w_in_b, w_out_b = _gather_finish(_gather_end(fly_mix, mix_flags, fly_ffn[3], "gather_end_mix"), mix_flags,
                                     "gather_finish_mix")

    pos = jnp.arange(HEAD)
    causal = (pos[None, :] // CHUNK) <= (pos[:, None] // CHUNK)
    wm = jnp.where(causal[None], w_spatial[0], 0.0)
    wm_b = wm.astype(BF16)
    wmt_b = jnp.swapaxes(wm, 1, 2).astype(BF16)
    bsb = jnp.broadcast_to(b_spatial[0][:, :, None], (N_HEADS, HEAD, HEAD))
    wp_b = w_pool[0].astype(BF16)
    sgu_g = jnp.concatenate([sgu_norm_g[0], jnp.zeros((4, HEAD), F32)], axis=0)
    ps = jnp.concatenate([pool_scale, jnp.zeros((7, A_WIDTH), F32)], axis=0)

    pmats = _perm_mats(FFN_TS)
    h1, proj, cat, mixed, x2, h2p = _mix_fwd(x2d, vecs, w_in_b, w_out_b, sgu_g, wm_b, bsb, wp_b, ps, pmats,
                                             ts=MIX_TS)[0]
    w_up_b, w_down_b = _gather_finish(_gather_end(fly_ffn, mix_flags, h2p, "gather_end_ffn"), mix_flags,
                                      "gather_finish_ffn")
    up, yv, act, dy, dfp, acc_f = _ffn_fwd(h2p, x2, tgt, w_up_b, w_down_b, cvec, vecs, pmats, ts=FFN_TS)

    c_idx = ci.reshape(1).astype(jnp.int32)
    g_w_down, _ = _wgrad(act, dfp, d, WGRAD_TS_WIDE, "wgrad_down")
    (dup, dx2, dmixed, acc_c, acc_b), (land_down,) = _ffn_bwd(
        dfp, up, yv, x2, dy, mixed, w_up_b, w_down_b, cvec, vecs, pmats, ts=FFN_TS,
        rider=_sibling_rider([g_w_down], (False,)))
    (part_down,) = _sum_with_sibling([g_w_down], [land_down], (False,), c_idx, "pair_sum_down")
    g_w_up, (chips_down,) = _wgrad(h2p, dup, ff2 // 2, WGRAD_TS, "wgrad_up", rider=_chips_rider([part_down]))
    fly_up = _sibling_begin(g_w_up, True, chips_down, "sibling_begin_up")
    gx, dproj, acc_m, dwm, dwp, dbz, dsg, dps = _mix_bwd(
        dmixed, dx2, x2d, proj, vecs + fly_up[3][0:1, 0:1], w_in_b, w_out_b, sgu_g, wm_b, wmt_b, bsb, wp_b, ps,
        ts=MIX_TS)[0]
    g_w_up, land_up = _sibling_end(fly_up, True, dproj, "sibling_end_up")
    (part_up,) = _sum_with_sibling([g_w_up], [land_up], (True,), c_idx, "pair_sum_up")
    fly_chips_up = _chips_begin([part_up], c_idx, "chips_begin_up")
    g1 = jnp.concatenate([acc_f, acc_b, acc_m], axis=0)
    hflat = (N_HEADS * HEAD, HEAD)
    small_bufs = [place(t) for t in (g1, acc_c, dwm.reshape(hflat), dwp.reshape(hflat), dbz, dsg, dps)]
    fly_small = _split_begin(small_bufs, 3 * len(small_bufs), _small_ici_copies, fly_chips_up[3],
                             "small_gather_begin")
    g_w_out, g_w_in = _wgrad_pair(cat, dmixed, h1, dproj, fly_small[3], WGRAD_TS_WIDE, "wgrad_mix")
    land_mix = _run_rider(_sibling_rider([g_w_in, g_w_out], (True, False)), "reduce_to_sibling")
    parts_mix = _sum_with_sibling([g_w_in, g_w_out], land_mix, (True, False), c_idx, "pair_sum_mix")
    (part_up,), (chips_up,) = _chips_end(fly_chips_up, parts_mix[0], "chips_end_up")
    fly_chips_mix = _chips_begin(parts_mix, chips_up, "chips_begin_mix")

    def adamw_of(names, reduced):
        res = {}
        for nm, red in zip(names, reduced):
            w, m, v = big_wmv[nm]
            g = red.reshape(w.shape[1], w.shape[2])
            res[nm] = (g.reshape(w.shape),) + tuple(_adamw_big(g, w, m, v, "adamw_" + nm))
        return res

    big_wmv = {"w_in": (w_in, m_w_in, v_w_in), "w_out": (w_out, m_w_out, v_w_out),
               "w_up": (w_up, m_w_up, v_w_up), "w_down": (w_down, m_w_down, v_w_down)}
    big = adamw_of(("w_up", "w_down"), _sum_chips_and_share([chips_up, chips_down], [part_up, part_down],
                                                           fly_chips_mix[3], "sum_share_ffn"))

    gathered = _small_finish(_split_end(fly_small, _small_ici_copies, big["w_down"][1], "small_gather_end"),
                             "small_gather_finish")
    g1a, g2a, gwm, gwp, gbz, gsg, gps = [t.reshape((N_DEV, t.shape[0] // N_DEV, t.shape[1])) for t in gathered]
    g2s = lax.dynamic_slice_in_dim(g2a, k_me * n_cw, n_cw, axis=2)
    params = {
        "b_ada": (b_ada, m_b_ada, v_b_ada), "pre_mix_g": (pre_mix_g, m_pre_mix_g, v_pre_mix_g),
        "post_mix_g": (post_mix_g, m_post_mix_g, v_post_mix_g),
        "sgu_norm_g": (sgu_norm_g, m_sgu_norm_g, v_sgu_norm_g), "w_spatial": (w_spatial, m_w_spatial, v_w_spatial),
        "b_spatial": (b_spatial, m_b_spatial, v_b_spatial), "w_pool": (w_pool, m_w_pool, v_w_pool),
        "pool_scale": (pool_scale, m_pool_scale, v_pool_scale), "pre_ffn_g": (pre_ffn_g, m_pre_ffn_g, v_pre_ffn_g),
        "post_ffn_g": (post_ffn_g, m_post_ffn_g, v_post_ffn_g), "conv_w": (conv_w, m_conv_w, v_conv_w),
        "conv_b": (conv_b, m_conv_b, v_conv_b),
    }
    loss_slab, small = _small_update(g1a, g2a, g2s, gwm, gwp, gbz, gsg, gps, params)

    gmod_all = jnp.concatenate(
        [g1a[:, G1_M + M_DSHM], g1a[:, G1_M + M_DSCM], g1a[:, G1_B + B_DGM], g1a[:, G1_B + B_DSHF],
         g1a[:, G1_B + B_DSCF], g1a[:, G1_F + F_DGF]], axis=1)
    gm = lax.dynamic_slice_in_dim(gmod_all, k_me * n_ada, n_ada, axis=1)
    gm = jnp.concatenate([gm, jnp.zeros((HEAD - N_DEV, n_ada), F32)], axis=0)
    sct = jnp.concatenate([sc_all.T, jnp.zeros((d, HEAD - N_DEV), F32)], axis=1)
    ada = tuple(_wada_update(sct, gm, w_ada, m_w_ada, v_w_ada))

    parts_mix, chips_mix = _chips_end(fly_chips_mix, ada[1], "chips_end_mix")
    big.update(adamw_of(("w_in", "w_out"), _sum_chips_and_share(chips_mix, parts_mix, loss_slab, "sum_share_mix")))

    everything = dict(small)
    everything.update(big)
    everything["w_ada"] = ada
    order = ["w_ada", "b_ada", "pre_mix_g", "post_mix_g", "w_in", "sgu_norm_g", "w_spatial", "b_spatial", "w_pool",
             "pool_scale", "w_out", "pre_ffn_g", "post_ffn_g", "w_up", "conv_w", "conv_b", "w_down"]
    outs = [loss_slab[0, 0], gx.reshape(x.shape)]
    for j in range(4):
        outs += [everything[nm][j] for nm in order]
    return tuple(outs)
```

```python
import functools

import jax
import jax.numpy as jnp
from jax import lax
from jax.experimental import pallas as pl
from jax.experimental.pallas import tpu as pltpu

F32 = jnp.float32
BF16 = jnp.bfloat16
MESH = pl.DeviceIdType.MESH

EPS = 1e-6
HEAD = 128
N_HEADS = 4
A_WIDTH = N_HEADS * HEAD
CHUNK = 64
POOL_WINDOWS = (2, 4, 8, 16)
POOL_HALO = 16
SUBLANES = 8
CONV_KEEP = 2 * SUBLANES
FFN_TS = 256
MIX_TS = 512
WGRAD_TS = 2048
WGRAD_TS_WIDE = 1024

ADAM_LR = 0.001
ADAM_B1 = 0.9
ADAM_B2 = 0.999
ADAM_EPS = 1e-08
ADAM_WD = 0.01
ADAM_STEP = 10

VMEM_LIMIT_BYTES = 58 * 1024 * 1024
N_DEV = 8
N_CHIP = 4


def _dot(a, b):
    return jnp.dot(a, b, preferred_element_type=F32)


def _dot_nt(a, b):
    return lax.dot_general(a, b, (((1,), (1,)), ((), ())), preferred_element_type=F32)


def _dot_tn(a, b):
    return lax.dot_general(a, b, (((0,), (0,)), ((), ())), preferred_element_type=F32)


GELU_C0 = 0.7978845608028654
GELU_C1 = GELU_C0 * 0.044715


def _gelu(x):
    return x * (0.5 + 0.5 * jnp.tanh(x * (GELU_C0 + GELU_C1 * (x * x))))


def _gelu_and_grad(x):
    x2 = x * x
    t = jnp.tanh(x * (GELU_C0 + GELU_C1 * x2))
    half = 0.5 + 0.5 * t
    grad = half + (x * (0.5 - 0.5 * (t * t))) * (GELU_C0 + (3.0 * GELU_C1) * x2)
    return x * half, grad


def _sigmoid(x):
    return 1.0 / (1.0 + jnp.exp(-x))


def _rms(x):
    return lax.rsqrt(jnp.mean(x * x, axis=-1, keepdims=True) + EPS)


def _colsum(x):
    return jnp.sum(x, axis=0, keepdims=True)


def _rowmean(x):
    return jnp.mean(x, axis=-1, keepdims=True)


def _tiled(shape, index_map):
    return pl.BlockSpec(shape, index_map)


def _resident(shape):
    nd = len(shape)
    return pl.BlockSpec(shape, lambda *_: (0,) * nd, pipeline_mode=pl.Buffered(1))


def _whole(shape):
    nd = len(shape)
    return pl.BlockSpec(shape, lambda *_: (0,) * nd)


def _seq_params():
    return pltpu.CompilerParams(dimension_semantics=("arbitrary",), vmem_limit_bytes=VMEM_LIMIT_BYTES)


def _ff_chunks(f, width=768):
    out, o = [], 0
    while o < f:
        w = min(width, f - o)
        out.append((o, w))
        o += w
    return out


def _pool_fwd(p, halo, g, t_glob):
    ext = jnp.concatenate([halo, p], axis=0)
    s = ext
    for step in range(g + 1):
        s = s + pltpu.roll(s, 1 << step, 0)
    cnt = jnp.minimum(t_glob + 1, POOL_WINDOWS[g]).astype(F32)
    return s[POOL_HALO:] / cnt - p


def _adamw_math(w, g, m, v):
    m = ADAM_B1 * m + (1.0 - ADAM_B1) * g
    v = ADAM_B2 * v + (1.0 - ADAM_B2) * (g * g)
    m_hat = m / (1.0 - ADAM_B1 ** ADAM_STEP)
    v_hat = v / (1.0 - ADAM_B2 ** ADAM_STEP)
    delta = -ADAM_LR * (m_hat / (jnp.sqrt(v_hat) + ADAM_EPS) + ADAM_WD * w)
    return delta, m, v


def _mesh_pos():
    return lax.axis_index("x"), lax.axis_index("y"), lax.axis_index("c")


class _Rider:
    def __init__(self, inputs, out_shape, sems, start, finish):
        self.inputs, self.out_shape, self.sems = list(inputs), list(out_shape), list(sems)
        self.start, self.finish = start, finish


def _call(body, *, name, grid, in_specs, out_specs, out_shape, scratch_shapes, args, rider=None):
    params = pltpu.CompilerParams(dimension_semantics=("arbitrary",) * len(grid), vmem_limit_bytes=VMEM_LIMIT_BYTES)
    if rider is None:
        res = pl.pallas_call(body, name=name, grid=grid, in_specs=in_specs, out_specs=out_specs, out_shape=out_shape,
                             scratch_shapes=scratch_shapes, compiler_params=params)(*args)
        return tuple(res), ()
    cuts = [len(in_specs), len(rider.inputs), len(out_specs), len(rider.out_shape), len(scratch_shapes),
            len(rider.sems)]

    def hosted(*refs):
        groups, a = [], 0
        for cnt in cuts:
            groups.append(refs[a:a + cnt])
            a += cnt
        ins, r_in, outs, r_out, scr, r_sem = groups
        first = functools.reduce(jnp.logical_and, [pl.program_id(k) == 0 for k in range(len(grid))])
        last = functools.reduce(jnp.logical_and, [pl.program_id(k) == grid[k] - 1 for k in range(len(grid))])

        @pl.when(first)
        def _():
            rider.start(r_in, r_out, r_sem)

        body(*ins, *outs, *scr)

        @pl.when(last)
        def _():
            rider.finish(r_in, r_out, r_sem)

    anyspec = pl.BlockSpec(memory_space=pl.ANY)
    res = pl.pallas_call(
        hosted, name=name, grid=grid,
        in_specs=list(in_specs) + [anyspec] * cuts[1], out_specs=list(out_specs) + [anyspec] * cuts[3],
        out_shape=list(out_shape) + rider.out_shape, scratch_shapes=list(scratch_shapes) + rider.sems,
        compiler_params=params)(*args, *rider.inputs)
    return tuple(res[:cuts[2]]), tuple(res[cuts[2]:])


def _run_rider(rider, name):
    n_in, n_out = len(rider.inputs), len(rider.out_shape)

    def body(*refs):
        r_in, r_out, r_sem = refs[:n_in], refs[n_in:n_in + n_out], refs[n_in + n_out:]
        rider.start(r_in, r_out, r_sem)
        rider.finish(r_in, r_out, r_sem)

    anyspec = pl.BlockSpec(memory_space=pl.ANY)
    return pl.pallas_call(body, name=name, out_shape=rider.out_shape, in_specs=[anyspec] * n_in,
                          out_specs=[anyspec] * n_out, scratch_shapes=rider.sems)(*rider.inputs)


def _allgather_rider(arrs):
    n = len(arrs)

    def plan(ins, outs, sems):
        send_sems, recv_sems, local_sems = sems
        x, y, c = _mesh_pos()
        me, sibling = (x, y, c), (x, y, 1 - c)
        chips = [(1 - x, y), (x, 1 - y), (1 - x, 1 - y)]

        def rows(a, px, py, pc):
            r = ins[a].shape[0]
            return outs[a].at[pl.ds(pl.multiple_of((4 * px + 2 * py + pc) * r, 8), r), :]

        def copy(a, k, block, to, src=None):
            return pltpu.make_async_remote_copy(
                src_ref=rows(a, *block) if src is None else src, dst_ref=rows(a, *block),
                send_sem=send_sems.at[a * 7 + k], recv_sem=recv_sems.at[a * 7 + k],
                device_id=to, device_id_type=MESH)

        local = [pltpu.make_async_copy(ins[a], rows(a, *me), local_sems.at[a]) for a in range(n)]
        first = []
        for a in range(n):
            first.append(copy(a, 0, me, sibling, src=ins[a]))
            first += [copy(a, 1 + j, me, (*chip, c), src=ins[a]) for j, chip in enumerate(chips)]
        return c, me, sibling, chips, copy, local, first

    def start(ins, outs, sems):
        *_, local, first = plan(ins, outs, sems)
        for cp in local + first:
            cp.start()

    def finish(ins, outs, sems):
        c, me, sibling, chips, copy, local, first = plan(ins, outs, sems)
        passed = []
        for a in range(n):
            for j, chip in enumerate(chips):
                copy(a, 1 + j, (*chip, c), me).wait_recv()
                fwd = copy(a, 4 + j, (*chip, c), sibling)
                fwd.start()
                passed.append(fwd)
        for a in range(n):
            copy(a, 0, sibling, me).wait_recv()
            for j, chip in enumerate(chips):
                copy(a, 4 + j, (*chip, 1 - c), me).wait_recv()
        for cp in first + passed:
            cp.wait_send()
        for mine in local:
            mine.wait()

    return _Rider(arrs, [jax.ShapeDtypeStruct((N_DEV * a.shape[0], a.shape[1]), a.dtype) for a in arrs],
                  [pltpu.SemaphoreType.DMA((7 * n,)), pltpu.SemaphoreType.DMA((7 * n,)),
                   pltpu.SemaphoreType.DMA((n,))], start, finish)


def _piece(ref, col_sharded, k, h):
    m, n = ref.shape
    if col_sharded:
        mh, nc = m // 2, n // N_CHIP
        return ref.at[pl.ds(pl.multiple_of(h * mh, 16), mh), pl.ds(pl.multiple_of(k * nc, 128), nc)]
    rp = m // (2 * N_CHIP)
    return ref.at[pl.ds(pl.multiple_of((2 * k + h) * rp, 16), rp), :]


def _piece_shape(shape, col_sharded):
    m, n = shape
    return (m // 2, n // N_CHIP) if col_sharded else (m // (2 * N_CHIP), n)


def _cast_bf16(arrs, col_flags, k_idx, name, after=None):
    n = len(arrs)
    extra = [] if after is None else [after]

    def body(k_ref, *refs):
        outs = refs[n + len(extra):]
        for a in range(n):
            val = refs[a][...].astype(BF16)
            outs[a][...] = val
            outs[n + a][...] = val

    whole = [pl.BlockSpec(a.shape, lambda i, k_ref: (0, 0)) for a in arrs]
    window = [pl.BlockSpec(a.shape, (lambda i, k_ref: (0, k_ref[0])) if col else (lambda i, k_ref: (k_ref[0], 0)))
              for a, col in zip(arrs, col_flags)]
    res = pl.pallas_call(
        body, name=name,
        grid_spec=pltpu.PrefetchScalarGridSpec(
            num_scalar_prefetch=1, grid=(1,),
            in_specs=whole + [pl.BlockSpec(t.shape, lambda i, k_ref: (0, 0)) for t in extra],
            out_specs=whole + window),
        out_shape=[jax.ShapeDtypeStruct(a.shape, BF16) for a in arrs]
        + [jax.ShapeDtypeStruct(fs, BF16) for fs in _full_shapes(arrs, col_flags)],
        compiler_params=pltpu.CompilerParams(vmem_limit_bytes=VMEM_LIMIT_BYTES))(k_idx, *arrs, *extra)
    return list(res[:n]), list(res[n:])


def _full_shapes(shards, col_flags):
    return [(s.shape[0], s.shape[1] * N_CHIP) if col else (s.shape[0] * N_CHIP, s.shape[1])
            for s, col in zip(shards, col_flags)]


def _ici_copies(shard_refs, full_refs, send_sems, recv_sems, col_flags):
    x, y, c = _mesh_pos()
    k_me = 2 * x + y
    copies = []
    for a, (s_ref, f_ref) in enumerate(zip(shard_refs, full_refs)):
        rows = s_ref.shape[0] // 2
        src = s_ref.at[pl.ds(pl.multiple_of(c * rows, 16), rows), :]
        for j, chip in enumerate([(1 - x, y), (x, 1 - y), (1 - x, 1 - y)]):
            copies.append(pltpu.make_async_remote_copy(
                src_ref=src, dst_ref=_piece(f_ref, col_flags[a], k_me, c),
                send_sem=send_sems.at[a * 3 + j], recv_sem=recv_sems.at[a * 3 + j],
                device_id=(*chip, c), device_id_type=MESH))
    return copies


def _split_begin(bufs, n_sems, make_copies, after, name):
    n = len(bufs)
    hbm = pl.BlockSpec(memory_space=pltpu.HBM)
    sem = pl.BlockSpec(memory_space=pltpu.SEMAPHORE)

    def body(*refs):
        for cp in make_copies(refs[:n], refs[n + 1], refs[n + 2]):
            cp.start()
        refs[-1][...] = jnp.zeros_like(refs[-1])

    args = [pltpu.with_memory_space_constraint(t, pltpu.HBM) for t in bufs]
    res = pl.pallas_call(
        body, name=name,
        out_shape=[pltpu.SemaphoreType.DMA((n_sems,)), pltpu.SemaphoreType.DMA((n_sems,))]
        + [pltpu.HBM(t.shape, t.dtype) for t in args] + [jax.ShapeDtypeStruct((8, HEAD), F32)],
        in_specs=[hbm] * n + [pl.BlockSpec(memory_space=pl.ANY)],
        out_specs=[sem, sem] + [hbm] * n + [pl.BlockSpec(memory_space=pltpu.VMEM)],
        input_output_aliases={i: 2 + i for i in range(n)},
        compiler_params=pltpu.CompilerParams(has_side_effects=pltpu.SideEffectType.DATAFLOW_SIDE_EFFECTING),
    )(*args, after)
    return res[0], res[1], list(res[2:2 + n]), res[-1]


def _split_end(handle, make_copies, after, name):
    send_sems, recv_sems, bufs, _ = handle
    n = len(bufs)
    hbm = pl.BlockSpec(memory_space=pltpu.HBM)
    sem = pl.BlockSpec(memory_space=pltpu.SEMAPHORE)

    def body(*refs):
        for cp in make_copies(refs[:n], refs[n], refs[n + 1]):
            cp.wait_send()
            cp.wait_recv()

    res = pl.pallas_call(
        body, name=name,
        out_shape=[pltpu.HBM(t.shape, t.dtype) for t in bufs],
        in_specs=[hbm] * n + [sem, sem, pl.BlockSpec(memory_space=pl.ANY)],
        out_specs=[hbm] * n,
        input_output_aliases={i: i for i in range(n)},
        compiler_params=pltpu.CompilerParams(has_side_effects=pltpu.SideEffectType.DATAFLOW_SIDE_EFFECTING),
    )(*bufs, send_sems, recv_sems, after)
    return list(res)


def _gather_copies(n, col_flags):
    return lambda refs, send_sems, recv_sems: _ici_copies(refs[:n], refs[n:], send_sems, recv_sems, col_flags)


def _gather_begin(shards, fulls, col_flags, after, name):
    n = len(shards)
    return _split_begin(list(shards) + list(fulls), 3 * n, _gather_copies(n, col_flags), after, name)


def _gather_end(handle, col_flags, after, name):
    n = len(handle[2]) // 2
    return _split_end(handle, _gather_copies(n, col_flags), after, name)[n:]


def _sibling_copies(col_flag):
    def make(refs, send_sems, recv_sems):
        grad_ref, land_ref = refs
        x, y, c = _mesh_pos()
        return [pltpu.make_async_remote_copy(
            src_ref=_piece(grad_ref, col_flag, k, 1 - c), dst_ref=land_ref.at[k],
            send_sem=send_sems.at[k], recv_sem=recv_sems.at[k],
            device_id=(x, y, 1 - c), device_id_type=MESH) for k in range(N_CHIP)]
    return make


def _chips_copies(n):
    def make(refs, send_sems, recv_sems):
        parts, landed = refs[:n], refs[n:]
        x, y, c = _mesh_pos()
        k_me = 2 * x + y
        copies = []
        for a in range(n):
            for j, chip in enumerate([(1 - x, y), (x, 1 - y), (1 - x, 1 - y)]):
                copies.append(pltpu.make_async_remote_copy(
                    src_ref=parts[a].at[2 * chip[0] + chip[1]], dst_ref=landed[a].at[k_me],
                    send_sem=send_sems.at[a * 3 + j], recv_sem=recv_sems.at[a * 3 + j],
                    device_id=(*chip, c), device_id_type=MESH))
        return copies
    return make


def _chips_begin(parts, after, name):
    landed = [lax.empty(p.shape, p.dtype) for p in parts]
    return _split_begin(list(parts) + landed, 3 * len(parts), _chips_copies(len(parts)), after, name)


def _chips_end(handle, after, name):
    n = len(handle[2]) // 2
    res = _split_end(handle, _chips_copies(n), after, name)
    return res[:n], res[n:]


def _small_rows(buf_ref, px, py, pc):
    r = buf_ref.shape[0] // N_DEV
    return buf_ref.at[pl.ds(pl.multiple_of((4 * px + 2 * py + pc) * r, 8), r), :]


def _small_ici_copies(refs, send_sems, recv_sems):
    x, y, c = _mesh_pos()
    copies = []
    for a, buf in enumerate(refs):
        mine = _small_rows(buf, x, y, c)
        for j, chip in enumerate([(1 - x, y), (x, 1 - y), (1 - x, 1 - y)]):
            copies.append(pltpu.make_async_remote_copy(
                src_ref=mine, dst_ref=mine, send_sem=send_sems.at[a * 3 + j], recv_sem=recv_sems.at[a * 3 + j],
                device_id=(*chip, c), device_id_type=MESH))
    return copies


def _small_finish(bufs, name):
    n = len(bufs)

    def body(*refs):
        buf_refs = refs[n:2 * n]
        send_sems, recv_sems = refs[2 * n:]
        x, y, c = _mesh_pos()
        owners = [(x, y), (1 - x, y), (x, 1 - y), (1 - x, 1 - y)]
        passed, arriving = [], []
        for a in range(n):
            for j, (px, py) in enumerate(owners):
                for pc, group in ((c, passed), (1 - c, arriving)):
                    rows = _small_rows(buf_refs[a], px, py, pc)
                    group.append(pltpu.make_async_remote_copy(
                        src_ref=rows, dst_ref=rows, send_sem=send_sems.at[a * 4 + j],
                        recv_sem=recv_sems.at[a * 4 + j], device_id=(x, y, 1 - c), device_id_type=MESH))
        for cp in passed:
            cp.start()
        for cp in arriving:
            cp.wait_recv()
        for cp in passed:
            cp.wait_send()

    anyspec = pl.BlockSpec(memory_space=pl.ANY)
    return pl.pallas_call(
        body, name=name, out_shape=[jax.ShapeDtypeStruct(b.shape, b.dtype) for b in bufs],
        in_specs=[anyspec] * n, out_specs=[anyspec] * n, input_output_aliases={a: a for a in range(n)},
        scratch_shapes=[pltpu.SemaphoreType.DMA((4 * n,)), pltpu.SemaphoreType.DMA((4 * n,))],
    )(*bufs)


def _sibling_begin(grad, col_flag, after, name):
    land = lax.empty((N_CHIP,) + _piece_shape(grad.shape, col_flag), grad.dtype)
    return _split_begin([grad, land], N_CHIP, _sibling_copies(col_flag), after, name)


def _sibling_end(handle, col_flag, after, name):
    return _split_end(handle, _sibling_copies(col_flag), after, name)


def _gather_finish(fulls, col_flags, name):
    n = len(fulls)

    def body(*refs):
        full_refs = refs[n:2 * n]
        send_sems, recv_sems = refs[2 * n:]
        x, y, c = _mesh_pos()
        passed, arriving = [], []
        for a in range(n):
            for j, chip in enumerate([(1 - x, y), (x, 1 - y), (1 - x, 1 - y)]):
                k_from = 2 * chip[0] + chip[1]
                for h, group in ((c, passed), (1 - c, arriving)):
                    win = _piece(full_refs[a], col_flags[a], k_from, h)
                    group.append(pltpu.make_async_remote_copy(
                        src_ref=win, dst_ref=win, send_sem=send_sems.at[a * 3 + j],
                        recv_sem=recv_sems.at[a * 3 + j], device_id=(x, y, 1 - c), device_id_type=MESH))
        for cp in passed:
            cp.start()
        for cp in arriving:
            cp.wait_recv()
        for cp in passed:
            cp.wait_send()

    anyspec = pl.BlockSpec(memory_space=pl.ANY)
    return pl.pallas_call(
        body, name=name,
        out_shape=[jax.ShapeDtypeStruct(f.shape, f.dtype) for f in fulls],
        in_specs=[anyspec] * n, out_specs=[anyspec] * n,
        input_output_aliases={a: a for a in range(n)},
        scratch_shapes=[pltpu.SemaphoreType.DMA((3 * n,)), pltpu.SemaphoreType.DMA((3 * n,))],
    )(*fulls)


def _sibling_rider(grads, col_flags):
    n = len(grads)
    pshapes = [_piece_shape(g.shape, col) for g, col in zip(grads, col_flags)]

    def copies(ins, outs, sems):
        send_sems, recv_sems = sems
        x, y, c = _mesh_pos()
        return [pltpu.make_async_remote_copy(
            src_ref=_piece(ins[a], col_flags[a], k, 1 - c), dst_ref=outs[a].at[k],
            send_sem=send_sems.at[a * N_CHIP + k], recv_sem=recv_sems.at[a * N_CHIP + k],
            device_id=(x, y, 1 - c), device_id_type=MESH) for a in range(n) for k in range(N_CHIP)]

    def start(ins, outs, sems):
        for cp in copies(ins, outs, sems):
            cp.start()

    def finish(ins, outs, sems):
        cps = copies(ins, outs, sems)
        for cp in cps:
            cp.wait_recv()
        for cp in cps:
            cp.wait_send()

    return _Rider(grads, [jax.ShapeDtypeStruct((N_CHIP,) + ps, g.dtype) for ps, g in zip(pshapes, grads)],
                  [pltpu.SemaphoreType.DMA((N_CHIP * n,)), pltpu.SemaphoreType.DMA((N_CHIP * n,))], start, finish)


def _sum_with_sibling(grads, landed, col_flags, c_idx, name):
    n = len(grads)
    pshapes = [_piece_shape(g.shape, col) for g, col in zip(grads, col_flags)]

    def body(c_ref, *refs):
        ins, lands, outs = refs[:n], refs[n:2 * n], refs[2 * n:]
        for a in range(n):
            outs[a][0] = (ins[a][...] + lands[a][0]).astype(BF16)

    in_specs = []
    for ps, col in zip(pshapes, col_flags):
        if col:
            in_specs.append(pl.BlockSpec(ps, lambda k, c_ref: (c_ref[0], k)))
        else:
            in_specs.append(pl.BlockSpec(ps, lambda k, c_ref: (2 * k + c_ref[0], 0)))
    land_specs = [pl.BlockSpec((1,) + ps, lambda k, c_ref: (k, 0, 0)) for ps in pshapes]
    return pl.pallas_call(
        body, name=name,
        grid_spec=pltpu.PrefetchScalarGridSpec(
            num_scalar_prefetch=1, grid=(N_CHIP,),
            in_specs=in_specs + land_specs, out_specs=land_specs),
        out_shape=[jax.ShapeDtypeStruct((N_CHIP,) + ps, BF16) for ps in pshapes],
        compiler_params=pltpu.CompilerParams(dimension_semantics=("arbitrary",), vmem_limit_bytes=VMEM_LIMIT_BYTES),
    )(c_idx, *grads, *landed)


def _chips_rider(parts):
    n = len(parts)

    def plan(ins, outs, sems, arriving):
        send_sems, recv_sems = sems
        x, y, c = _mesh_pos()
        k_me = 2 * x + y
        copies = []
        for a in range(n):
            for j, chip in enumerate([(1 - x, y), (x, 1 - y), (1 - x, 1 - y)]):
                k_peer = 2 * chip[0] + chip[1]
                copies.append(pltpu.make_async_remote_copy(
                    src_ref=ins[a].at[k_peer], dst_ref=outs[a].at[k_peer if arriving else k_me],
                    send_sem=send_sems.at[a * 3 + j], recv_sem=recv_sems.at[a * 3 + j],
                    device_id=(*chip, c), device_id_type=MESH))
        return copies

    def start(ins, outs, sems):
        for cp in plan(ins, outs, sems, False):
            cp.start()

    def finish(ins, outs, sems):
        arrivals = plan(ins, outs, sems, True)
        for cp in arrivals:
            cp.wait_recv()
        for cp in arrivals:
            cp.wait_send()

    return _Rider(parts, [jax.ShapeDtypeStruct(p.shape, p.dtype) for p in parts],
                  [pltpu.SemaphoreType.DMA((3 * n,)), pltpu.SemaphoreType.DMA((3 * n,))], start, finish)


def _sum_chips_and_share(landed, parts, after, name):
    n = len(landed)

    def body(*refs):
        ins, own, outs, red = refs[:n], refs[n:2 * n], refs[2 * n + 1:3 * n + 1], refs[3 * n + 1:4 * n + 1]
        send_sems, recv_sems, local_sems = refs[4 * n + 1:]
        x, y, c = _mesh_pos()
        sibling = (x, y, 1 - c)
        k_me = 2 * x + y
        copies, local = [], []
        for a in range(n):
            for k in range(N_CHIP):
                @pl.when(k_me == k)
                def _():
                    term = own[a][k].astype(F32)
                    red[a][...] = term if k == 0 else red[a][...] + term

                @pl.when(k_me != k)
                def _():
                    term = ins[a][k].astype(F32)
                    red[a][...] = term if k == 0 else red[a][...] + term

            mine = pltpu.make_async_copy(red[a], outs[a].at[c], local_sems.at[a])
            mine.start()
            local.append(mine)
            cp = pltpu.make_async_remote_copy(
                src_ref=red[a], dst_ref=outs[a].at[c],
                send_sem=send_sems.at[a], recv_sem=recv_sems.at[a],
                device_id=sibling, device_id_type=MESH)
            cp.start()
            copies.append(cp)
        for a in range(n):
            pltpu.make_async_remote_copy(
                src_ref=red[a], dst_ref=outs[a].at[1 - c],
                send_sem=send_sems.at[a], recv_sem=recv_sems.at[a],
                device_id=sibling, device_id_type=MESH).wait_recv()
        for cp in copies:
            cp.wait_send()
        for mine in local:
            mine.wait()

    return pl.pallas_call(
        body, name=name,
        out_shape=[jax.ShapeDtypeStruct((2,) + l.shape[1:], F32) for l in landed],
        in_specs=[pl.BlockSpec(memory_space=pltpu.VMEM)] * (2 * n + 1),
        out_specs=[pl.BlockSpec(memory_space=pl.ANY)] * n,
        scratch_shapes=[pltpu.VMEM(l.shape[1:], F32) for l in landed]
        + [pltpu.SemaphoreType.DMA((n,)), pltpu.SemaphoreType.DMA((n,)), pltpu.SemaphoreType.DMA((n,))],
        compiler_params=pltpu.CompilerParams(vmem_limit_bytes=VMEM_LIMIT_BYTES),
    )(*landed, *parts, after)


def _mod_shard(c_all, w_ada, b_shard):
    def body(c_ref, w_ref, b_ref, o_ref, sc_ref):
        cc = c_ref[...]
        sc = cc * _sigmoid(cc)
        sc_ref[...] = sc
        o_ref[...] = _dot(sc, w_ref[...]) + b_ref[...]

    nb, d = c_all.shape
    nn = w_ada.shape[1]
    return pl.pallas_call(
        body, name="mod_shard",
        out_shape=[jax.ShapeDtypeStruct((nb, nn), F32), jax.ShapeDtypeStruct((nb, d), F32)],
        compiler_params=pltpu.CompilerParams(vmem_limit_bytes=VMEM_LIMIT_BYTES),
    )(c_all, w_ada, b_shard)


V_SH_M, V_SC_M, V_G_M, V_SH_F, V_SC_F, V_G_F, V_PRE_MIX, V_POST_MIX, V_PRE_FFN, V_POST_FFN = range(10)


def _vrow(vec_ref, r):
    return vec_ref[r:r + 1, :]


def _mix_fwd(x, vecs, w_in_b, w_out_b, sgu_g, wm_b, bsb, wp_b, ps, pmats, ts):
    s_len, d = x.shape
    nt = s_len // ts
    n_proj = w_in_b.shape[1]

    def body(x_ref, vec_ref, win_ref, wout_ref, sg_ref, wm_ref, bs_ref, wp_ref, ps_ref, pm_ref,
             h1_ref, proj_ref, cat_ref, mixed_ref, x2_ref, h2_ref, carry_ref):
        i = pl.program_id(0)

        @pl.when(i == 0)
        def _():
            carry_ref[...] = jnp.zeros_like(carry_ref)

        sub = FFN_TS
        nsub = ts // sub

        def project(s):
            rs = slice(s * sub, (s + 1) * sub)
            x = x_ref[rs, :]
            h1 = (((x * _rms(x)) * _vrow(vec_ref, V_PRE_MIX)) * (1.0 + _vrow(vec_ref, V_SC_M))
                  + _vrow(vec_ref, V_SH_M)).astype(BF16)
            h1_ref[rs, :] = h1
            proj = _dot(h1, win_ref[...])
            proj_ref[rs, :] = proj.astype(BF16)
            return proj

        def mix(s, proj, halo):
            r0 = s * sub
            t_glob = lax.broadcasted_iota(jnp.int32, (sub, HEAD), 0) + (i * ts + r0)
            for h in range(N_HEADS):
                u = _gelu(proj[:, h * HEAD:(h + 1) * HEAD])
                v = _gelu(proj[:, A_WIDTH + h * HEAD:A_WIDTH + (h + 1) * HEAD])
                vn = ((v * _rms(v)) * sg_ref[h:h + 1, :]).astype(BF16)
                for b in range(sub // HEAD):
                    rs = slice(b * HEAD, (b + 1) * HEAD)
                    z = _dot(wm_ref[h], vn[rs]) + bs_ref[h]
                    cat_ref[r0 + b * HEAD:r0 + (b + 1) * HEAD, h * HEAD:(h + 1) * HEAD] = (u[rs] * z).astype(BF16)
            for g in range(len(POOL_WINDOWS)):
                gs = slice(g * HEAD, (g + 1) * HEAD)
                p = proj[:, 2 * A_WIDTH + g * HEAD:2 * A_WIDTH + (g + 1) * HEAD]
                pooled = _pool_fwd(p, halo[:, gs], g, t_glob)
                yb = _dot(pooled.astype(BF16), wp_ref[g]) * ps_ref[0:1, gs]
                cat_ref[r0:r0 + sub, A_WIDTH + g * HEAD:A_WIDTH + (g + 1) * HEAD] = yb.astype(BF16)

        def finish(s, mixed):
            rs = slice(s * sub, (s + 1) * sub)
            mixed_ref[rs, :] = mixed
            x2 = x_ref[rs, :] + _vrow(vec_ref, V_G_M) * ((mixed * _rms(mixed)) * _vrow(vec_ref, V_POST_MIX))
            x2_ref[rs, :] = x2
            h2 = (((x2 * _rms(x2)) * _vrow(vec_ref, V_PRE_FFN)) * (1.0 + _vrow(vec_ref, V_SC_F))
                  + _vrow(vec_ref, V_SH_F)).astype(BF16)
            h2_ref[rs, :] = _permute_bf16(pm_ref[0], h2)

        projs = [project(0)]
        halo = carry_ref[...]
        mixed_prev = None
        for s in range(nsub):
            if s + 1 < nsub:
                projs.append(project(s + 1))
            mix(s, projs[s], halo)
            halo = projs[s][sub - POOL_HALO:sub, 2 * A_WIDTH:]
            mixed = _dot(cat_ref[s * sub:(s + 1) * sub, :], wout_ref[...])
            if mixed_prev is not None:
                finish(s - 1, mixed_prev)
            mixed_prev = mixed
        carry_ref[...] = halo
        finish(nsub - 1, mixed_prev)

    row = lambda i: (i, 0)
    return _call(
        body, name="mix_fwd", grid=(nt,),
        in_specs=[_tiled((ts, d), row), _whole(vecs.shape), _resident(w_in_b.shape), _resident(w_out_b.shape),
                  _whole(sgu_g.shape), _whole(wm_b.shape), _whole(bsb.shape), _whole(wp_b.shape), _whole(ps.shape),
                  _whole(pmats.shape)],
        out_specs=[_tiled((ts, d), row), _tiled((ts, n_proj), row), _tiled((ts, d), row),
                   _tiled((ts, d), row), _tiled((ts, d), row), _tiled((ts, d), row)],
        out_shape=[jax.ShapeDtypeStruct((s_len, d), BF16), jax.ShapeDtypeStruct((s_len, n_proj), BF16),
                   jax.ShapeDtypeStruct((s_len, d), BF16), jax.ShapeDtypeStruct((s_len, d), F32),
                   jax.ShapeDtypeStruct((s_len, d), F32), jax.ShapeDtypeStruct((s_len, d), BF16)],
        scratch_shapes=[pltpu.VMEM((POOL_HALO, A_WIDTH), F32)],
        args=(x, vecs, w_in_b, w_out_b, sgu_g, wm_b, bsb, wp_b, ps, pmats))


def _perm_mats(ts):
    p = jnp.arange(ts)
    pm = (((p % SUBLANES) * (ts // SUBLANES) + p // SUBLANES)[:, None] == p[None, :]).astype(BF16)
    return jnp.stack([pm, pm.T])


def _permute_bf16(pm, xb):
    return _dot(pm, xb).astype(BF16)


def _permute_f32(pm, x):
    hi = x.astype(BF16)
    lo = (x - hi.astype(F32)).astype(BF16)
    return _dot(pm, hi) + _dot(pm, lo)


def _conv_out(u, um2, um1, cv_ref, cols):
    return (cv_ref[3:4, cols] + um2 * cv_ref[0:1, cols] + um1 * cv_ref[1:2, cols] + u * cv_ref[2:3, cols])


F_LOSS, F_DGF, F_DPOSTFFN = 0, 1, 2
B_DSHF, B_DSCF, B_DPREFFN, B_DGM, B_DPOSTMIX = 0, 1, 2, 3, 4
M_DSHM, M_DSCM, M_DPREMIX = 0, 1, 2
C_DCB, C_DCW = 0, 1
G1_F, G1_B, G1_M = 0, 8, 16


def _ffn_fwd(h2p, x2, tgt, w_up_b, w_down_b, cvec, vecs, pmats, ts):
    s_len, d = x2.shape
    ff2 = w_up_b.shape[1]
    ff = ff2 // 2
    nt = s_len // ts
    chunks = _ff_chunks(ff)

    def body(h2_ref, x2_ref, t_ref, wu_ref, wd_ref, cv_ref, vec_ref, pm_ref,
             up_ref, y_ref, act_ref, dy_ref, df_ref, acc_ref, carry_ref):
        @pl.when(pl.program_id(0) == 0)
        def _():
            carry_ref[...] = jnp.zeros_like(carry_ref)
            acc_ref[...] = jnp.zeros_like(acc_ref)

        h2v = h2_ref[...]

        def up_dots(o, w):
            return [_dot(h2v, wu_ref[:, base + o:base + o + w]) for base in (0, ff)]

        f = None
        pending = None
        nxt = up_dots(*chunks[0])
        for ci, (o, w) in enumerate(chunks):
            us = nxt
            if ci + 1 < len(chunks):
                nxt = up_dots(*chunks[ci + 1])
            if pending is not None:
                part = _dot(pending[0], wd_ref[pending[1]:pending[1] + pending[2], :])
                f = part if f is None else f + part
            sub0 = lax.broadcasted_iota(jnp.int32, (SUBLANES, w), 0) == 0
            ys = []
            for base, u in zip((0, ff), us):
                cols = slice(base + o, base + o + w)
                up_ref[:, cols] = u.astype(BF16)
                last1, last2 = u[ts - SUBLANES:ts], u[ts - CONV_KEEP:ts - SUBLANES]
                b1 = jnp.where(sub0, pltpu.roll(carry_ref[SUBLANES:CONV_KEEP, cols], 1, 0), pltpu.roll(last1, 1, 0))
                b2 = jnp.where(sub0, pltpu.roll(carry_ref[0:SUBLANES, cols], 1, 0), pltpu.roll(last2, 1, 0))
                um1 = jnp.concatenate([b1, u[:ts - SUBLANES]], axis=0)
                um2 = jnp.concatenate([b2, b1, u[:ts - CONV_KEEP]], axis=0)
                ys.append(_conv_out(u, um2, um1, cv_ref, cols))
                carry_ref[:, cols] = u[ts - CONV_KEEP:ts]
            gate, val = ys
            sg = _sigmoid(gate)
            gs = gate * sg
            act = (gs * val).astype(BF16)
            act_ref[:, o:o + w] = act
            y_ref[:, o:o + w] = (val * (sg + gs * (1.0 - sg))).astype(BF16)
            y_ref[:, ff + o:ff + o + w] = gs.astype(BF16)
            pending = (act, o, w)
        f = f + _dot(pending[0], wd_ref[pending[1]:pending[1] + pending[2], :])
        f = _permute_f32(pm_ref[1], f)
        r3 = _rms(f)
        fhat = f * r3
        post = _vrow(vec_ref, V_POST_FFN)
        g_f = _vrow(vec_ref, V_G_F)
        fn = fhat * post
        e = (x2_ref[...] + g_f * fn) - t_ref[...]
        dy = e * (1.0 / d)
        dy_ref[...] = dy
        dfn = dy * g_f
        acc_ref[F_LOSS:F_LOSS + 1, :] += _colsum(e * e)
        acc_ref[F_DGF:F_DGF + 1, :] += _colsum(dy * fn)
        acc_ref[F_DPOSTFFN:F_DPOSTFFN + 1, :] += _colsum(dfn * fhat)
        dfhat = dfn * post
        df = (r3 * (dfhat - fhat * _rowmean(dfhat * fhat))).astype(BF16)
        df_ref[...] = _permute_bf16(pm_ref[0], df)

    row = lambda i: (i, 0)
    return pl.pallas_call(
        body, name="ffn_fwd", grid=(nt,),
        in_specs=[_tiled((ts, d), row), _tiled((ts, d), row), _tiled((ts, d), row), _resident(w_up_b.shape),
                  _resident(w_down_b.shape), _whole(cvec.shape), _whole(vecs.shape), _whole(pmats.shape)],
        out_specs=[_tiled((ts, ff2), row), _tiled((ts, ff2), row), _tiled((ts, ff), row), _tiled((ts, d), row),
                   _tiled((ts, d), row), _whole((8, d))],
        out_shape=[jax.ShapeDtypeStruct((s_len, ff2), BF16), jax.ShapeDtypeStruct((s_len, ff2), BF16),
                   jax.ShapeDtypeStruct((s_len, ff), BF16), jax.ShapeDtypeStruct((s_len, d), F32),
                   jax.ShapeDtypeStruct((s_len, d), BF16), jax.ShapeDtypeStruct((8, d), F32)],
        scratch_shapes=[pltpu.VMEM((CONV_KEEP, ff2), F32)],
        compiler_params=_seq_params(),
    )(h2p, x2, tgt, w_up_b, w_down_b, cvec, vecs, pmats)


def _ffn_bwd(dfp, upp, yp, x2, dy, mixed, w_up_b, w_down_b, cvec, vecs, pmats, ts, rider=None):
    s_len, d = x2.shape
    ff2 = w_up_b.shape[1]
    ff = ff2 // 2
    nt = s_len // ts
    chunks = _ff_chunks(ff, 512)

    def body(df_ref, up_ref, y_ref, x2_ref, dy_ref, mx_ref, wu_ref, wd_ref, cv_ref, vec_ref, pm_ref,
             dup_ref, dx2_ref, dmx_ref, accc_ref, acc_ref, carry_ref):
        @pl.when(pl.program_id(0) == 0)
        def _():
            carry_ref[...] = jnp.zeros_like(carry_ref)
            accc_ref[...] = jnp.zeros_like(accc_ref)
            acc_ref[...] = jnp.zeros_like(acc_ref)

        dfv = df_ref[...]

        def dh2_add(acc, dups, o, w):
            for base, dup in zip((0, ff), dups):
                part = _dot_nt(dup, wu_ref[:, base + o:base + o + w])
                acc = part if acc is None else acc + part
            return acc

        dh2 = None
        pending = None
        nxt = _dot_nt(dfv, wd_ref[chunks[0][0]:chunks[0][0] + chunks[0][1], :])
        for ci, (o, w) in enumerate(chunks):
            dact = nxt
            if ci + 1 < len(chunks):
                o2, w2 = chunks[ci + 1]
                nxt = _dot_nt(dfv, wd_ref[o2:o2 + w2, :])
            if pending is not None:
                dh2 = dh2_add(dh2, *pending)
            sub7 = lax.broadcasted_iota(jnp.int32, (SUBLANES, w), 0) == SUBLANES - 1
            dups = []
            dys = (dact * y_ref[:, o:o + w].astype(F32), dact * y_ref[:, ff + o:ff + o + w].astype(F32))
            for base, dyv in zip((0, ff), dys):
                cols = slice(base + o, base + o + w)
                u = up_ref[:, cols].astype(F32)
                up1 = SUBLANES - 1
                e0 = jnp.where(sub7, pltpu.roll(carry_ref[0:SUBLANES, cols], up1, 0),
                               pltpu.roll(dyv[0:SUBLANES], up1, 0))
                e1 = jnp.where(sub7, pltpu.roll(carry_ref[SUBLANES:CONV_KEEP, cols], up1, 0),
                               pltpu.roll(dyv[SUBLANES:CONV_KEEP], up1, 0))
                dyp1 = jnp.concatenate([dyv[SUBLANES:], e0], axis=0)
                dyp2 = jnp.concatenate([dyv[CONV_KEEP:], e0, e1], axis=0)
                accc_ref[C_DCB:C_DCB + 1, cols] += _colsum(dyv)
                accc_ref[C_DCW + 0:C_DCW + 1, cols] += _colsum(dyp2 * u)
                accc_ref[C_DCW + 1:C_DCW + 2, cols] += _colsum(dyp1 * u)
                accc_ref[C_DCW + 2:C_DCW + 3, cols] += _colsum(dyv * u)
                dup = (dyv * cv_ref[2:3, cols] + dyp1 * cv_ref[1:2, cols] + dyp2 * cv_ref[0:1, cols]).astype(BF16)
                dup_ref[:, cols] = dup
                dups.append(dup)
                carry_ref[:, cols] = dyv[0:CONV_KEEP]
            pending = (dups, o, w)
        dh2 = dh2_add(dh2, *pending)
        dh2 = _permute_f32(pm_ref[1], dh2)
        x2 = x2_ref[...]
        r2 = _rms(x2)
        xn = x2 * r2
        pre = _vrow(vec_ref, V_PRE_FFN)
        one_sc = 1.0 + _vrow(vec_ref, V_SC_F)
        acc_ref[B_DSHF:B_DSHF + 1, :] += _colsum(dh2)
        acc_ref[B_DSCF:B_DSCF + 1, :] += _colsum(dh2 * (xn * pre))
        acc_ref[B_DPREFFN:B_DPREFFN + 1, :] += _colsum(dh2 * xn * one_sc)
        dxn = dh2 * pre * one_sc
        dx2 = dy_ref[...] + r2 * (dxn - xn * _rowmean(dxn * xn))
        dx2_ref[...] = dx2
        mixed = mx_ref[...]
        rm = _rms(mixed)
        mhat = mixed * rm
        post = _vrow(vec_ref, V_POST_MIX)
        acc_ref[B_DGM:B_DGM + 1, :] += _colsum(dx2 * (mhat * post))
        dmn = dx2 * _vrow(vec_ref, V_G_M)
        acc_ref[B_DPOSTMIX:B_DPOSTMIX + 1, :] += _colsum(dmn * mhat)
        dmhat = dmn * post
        dmx_ref[...] = (rm * (dmhat - mhat * _rowmean(dmhat * mhat))).astype(BF16)

    rev = lambda i: (nt - 1 - i, 0)
    return _call(
        body, name="ffn_bwd", grid=(nt,), rider=rider,
        in_specs=[_tiled((ts, d), rev), _tiled((ts, ff2), rev), _tiled((ts, ff2), rev), _tiled((ts, d), rev),
                  _tiled((ts, d), rev), _tiled((ts, d), rev), _resident(w_up_b.shape), _resident(w_down_b.shape),
                  _whole(cvec.shape), _whole(vecs.shape), _whole(pmats.shape)],
        out_specs=[_tiled((ts, ff2), rev), _tiled((ts, d), rev), _tiled((ts, d), rev), _whole((8, ff2)),
                   _whole((8, d))],
        out_shape=[jax.ShapeDtypeStruct((s_len, ff2), BF16), jax.ShapeDtypeStruct((s_len, d), F32),
                   jax.ShapeDtypeStruct((s_len, d), BF16), jax.ShapeDtypeStruct((8, ff2), F32),
                   jax.ShapeDtypeStruct((8, d), F32)],
        scratch_shapes=[pltpu.VMEM((CONV_KEEP, ff2), F32)],
        args=(dfp, upp, yp, x2, dy, mixed, w_up_b, w_down_b, cvec, vecs, pmats))


def _mix_bwd(dmixed, dx2, x, proj, vecs, w_in_b, w_out_b, sgu_g, wm_b, wmt_b, bsb, wp_b, ps, ts):
    s_len, d = x.shape
    nt = s_len // ts
    n_proj = proj.shape[1]
    per = ts // POOL_HALO

    def body(dmx_ref, dx2_ref, x_ref, proj_ref, projh_ref, vec_ref, win_ref, wout_ref, sg_ref, wm_ref, wmt_ref,
             bs_ref, wp_ref, ps_ref,
             gx_ref, dproj_ref, acc_ref, dwm_out, dwp_out, db_ref, dg_ref, dps_ref,
             carry_ref, dwm_ref, dwp_ref, dbz_ref):
        i = pl.program_id(0)
        tile = nt - 1 - i

        @pl.when(i == 0)
        def _():
            carry_ref[...] = jnp.zeros_like(carry_ref)
            for r in (acc_ref, dwm_ref, dwp_ref, dbz_ref, dg_ref, dps_ref):
                r[...] = jnp.zeros_like(r)

        sub = FFN_TS
        nsub = ts // sub
        ext = sub + POOL_HALO

        def cotangent(s):
            return _dot_nt(dmx_ref[s * sub:(s + 1) * sub, :], wout_ref[...])

        def back(s, dcat, halo, later):
            r0 = s * sub
            rows = slice(r0, r0 + sub)
            t_glob = lax.broadcasted_iota(jnp.int32, (sub, HEAD), 0) + (tile * ts + r0)
            for h in range(N_HEADS):
                hs = slice(h * HEAD, (h + 1) * HEAD)
                vs = slice(A_WIDTH + h * HEAD, A_WIDTH + (h + 1) * HEAD)
                gain = sg_ref[h:h + 1, :]
                for b in range(sub // HEAD):
                    blk = slice(r0 + b * HEAD, r0 + (b + 1) * HEAD)
                    au = proj_ref[blk, hs].astype(F32)
                    av = proj_ref[blk, vs].astype(F32)
                    u, u_grad = _gelu_and_grad(au)
                    v, v_grad = _gelu_and_grad(av)
                    rv = _rms(v)
                    vhat = v * rv
                    vn = (vhat * gain).astype(BF16)
                    dout = dcat[b * HEAD:(b + 1) * HEAD, hs]
                    z = _dot(wm_ref[h], vn) + bs_ref[h]
                    dz = dout * u
                    dbz_ref[h] += dz
                    dzb = dz.astype(BF16)
                    dwm_ref[h] += _dot_nt(dzb, vn)
                    dvn = _dot(wmt_ref[h], dzb)
                    dg_ref[h:h + 1, :] += _colsum(dvn * vhat)
                    dvhat = dvn * gain
                    dv = rv * (dvhat - vhat * _rowmean(dvhat * vhat))
                    dproj_ref[blk, hs] = ((dout * z) * u_grad).astype(BF16)
                    dproj_ref[blk, vs] = (dv * v_grad).astype(BF16)
            firsts = []
            for g in range(len(POOL_WINDOWS)):
                gs = slice(g * HEAD, (g + 1) * HEAD)
                pcols = slice(2 * A_WIDTH + g * HEAD, 2 * A_WIDTH + (g + 1) * HEAD)
                p = proj_ref[rows, pcols].astype(F32)
                pb = _pool_fwd(p, halo[:, gs], g, t_glob).astype(BF16)
                dyb = dcat[:, A_WIDTH + g * HEAD:A_WIDTH + (g + 1) * HEAD]
                dps_ref[0:1, gs] += _colsum(dyb * _dot(pb, wp_ref[g]))
                dyl = (dyb * ps_ref[0:1, gs]).astype(BF16)
                dwp_ref[g] += _dot_tn(pb, dyl)
                dpooled = _dot_nt(dyl, wp_ref[g])
                cnt = jnp.minimum(t_glob + 1, POOL_WINDOWS[g]).astype(F32)
                q = dpooled / cnt
                acc = jnp.concatenate([q, later[:, gs]], axis=0)
                for step in range(g + 1):
                    acc = acc + pltpu.roll(acc, ext - (1 << step), 0)
                dproj_ref[rows, pcols] = (acc[:sub] - dpooled).astype(BF16)
                firsts.append(q[0:POOL_HALO])
            return jnp.concatenate(firsts, axis=1)

        def finish(s, dh1):
            rows = slice(s * sub, (s + 1) * sub)
            x = x_ref[rows, :]
            r1 = _rms(x)
            xn = x * r1
            pre = _vrow(vec_ref, V_PRE_MIX)
            one_sc = 1.0 + _vrow(vec_ref, V_SC_M)
            acc_ref[M_DSHM:M_DSHM + 1, :] += _colsum(dh1)
            acc_ref[M_DSCM:M_DSCM + 1, :] += _colsum(dh1 * (xn * pre))
            acc_ref[M_DPREMIX:M_DPREMIX + 1, :] += _colsum(dh1 * xn * one_sc)
            dxn = dh1 * pre * one_sc
            gx_ref[rows, :] = dx2_ref[rows, :] + r1 * (dxn - xn * _rowmean(dxn * xn))

        nxt = cotangent(nsub - 1)
        later = carry_ref[...]
        dh1_prev = None
        for s in reversed(range(nsub)):
            dcat = nxt
            if s > 0:
                nxt = cotangent(s - 1)
                halo = proj_ref[s * sub - POOL_HALO:s * sub, 2 * A_WIDTH:].astype(F32)
            else:
                halo = jnp.where(tile > 0, projh_ref[:, 2 * A_WIDTH:].astype(F32), 0.0)
            later = back(s, dcat, halo, later)
            dh1 = _dot_nt(dproj_ref[s * sub:(s + 1) * sub, :], win_ref[...])
            if dh1_prev is not None:
                finish(s + 1, dh1_prev)
            dh1_prev = dh1
        carry_ref[...] = later
        finish(0, dh1_prev)

        @pl.when(i == nt - 1)
        def _():
            dwm_out[...] = dwm_ref[...].astype(BF16)
            dwp_out[...] = dwp_ref[...].astype(BF16)
            db_ref[...] = jnp.zeros_like(db_ref)
            for h in range(N_HEADS):
                db_ref[h:h + 1, :] = jnp.sum(dbz_ref[h].T, axis=0, keepdims=True)

    rev = lambda i: (nt - 1 - i, 0)
    halo_map = lambda i: (jnp.maximum((nt - 1 - i) * per - 1, 0), 0)
    hshape = (N_HEADS, HEAD, HEAD)
    return _call(
        body, name="mix_bwd", grid=(nt,),
        in_specs=[_tiled((ts, d), rev), _tiled((ts, d), rev), _tiled((ts, d), rev), _tiled((ts, n_proj), rev),
                  _tiled((POOL_HALO, n_proj), halo_map), _whole(vecs.shape), _resident(w_in_b.shape),
                  _resident(w_out_b.shape), _whole(sgu_g.shape), _whole(wm_b.shape), _whole(wmt_b.shape),
                  _whole(bsb.shape), _whole(wp_b.shape), _whole(ps.shape)],
        out_specs=[_tiled((ts, d), rev), _tiled((ts, n_proj), rev), _whole((8, d)), _whole(hshape), _whole(hshape),
                   _whole((8, HEAD)), _whole((8, HEAD)), _whole((8, A_WIDTH))],
        out_shape=[jax.ShapeDtypeStruct((s_len, d), F32), jax.ShapeDtypeStruct((s_len, n_proj), BF16),
                   jax.ShapeDtypeStruct((8, d), F32), jax.ShapeDtypeStruct(hshape, BF16),
                   jax.ShapeDtypeStruct(hshape, BF16), jax.ShapeDtypeStruct((8, HEAD), F32),
                   jax.ShapeDtypeStruct((8, HEAD), F32), jax.ShapeDtypeStruct((8, A_WIDTH), F32)],
        scratch_shapes=[pltpu.VMEM((POOL_HALO, A_WIDTH), F32), pltpu.VMEM(hshape, F32), pltpu.VMEM(hshape, F32),
                        pltpu.VMEM(hshape, F32)],
        args=(dmixed, dx2, x, proj, proj, vecs, w_in_b, w_out_b, sgu_g, wm_b, wmt_b, bsb, wp_b, ps))


def _wgrad(a, b, tn, ts, name, rider=None):
    s_len, m = a.shape
    n = b.shape[1]
    ts = min(ts, s_len)

    def body(a_ref, b_ref, o_ref):
        @pl.when(pl.program_id(1) == 0)
        def _():
            o_ref[...] = jnp.zeros_like(o_ref)

        o_ref[...] += _dot_tn(a_ref[...], b_ref[...])

    (g,), r_out = _call(
        body, name=name, grid=(n // tn, s_len // ts), rider=rider,
        in_specs=[pl.BlockSpec((ts, m), lambda j, s: (s, 0)), pl.BlockSpec((ts, tn), lambda j, s: (s, j))],
        out_specs=[pl.BlockSpec((m, tn), lambda j, s: (0, j))],
        out_shape=[jax.ShapeDtypeStruct((m, n), F32)], scratch_shapes=[], args=(a, b))
    return g, r_out


def _wgrad_pair(a1, b1, a2, b2, after, ts, name):
    s_len = a1.shape[0]
    ts = min(ts, s_len)
    shapes = [(a1.shape[1], b1.shape[1]), (a2.shape[1], b2.shape[1])]

    def body(a1_ref, b1_ref, a2_ref, b2_ref, after_ref, o1_ref, o2_ref):
        @pl.when(pl.program_id(0) == 0)
        def _():
            o1_ref[...] = jnp.zeros_like(o1_ref)
            o2_ref[...] = jnp.zeros_like(o2_ref)

        o1_ref[...] += _dot_tn(a1_ref[...], b1_ref[...])
        o2_ref[...] += _dot_tn(a2_ref[...], b2_ref[...])

    row = lambda s: (s, 0)
    return _call(
        body, name=name, grid=(s_len // ts,),
        in_specs=[pl.BlockSpec((ts, t.shape[1]), row) for t in (a1, b1, a2, b2)] + [_whole(after.shape)],
        out_specs=[_whole(sh) for sh in shapes],
        out_shape=[jax.ShapeDtypeStruct(sh, F32) for sh in shapes], scratch_shapes=[],
        args=(a1, b1, a2, b2, after))[0]


def _adamw_big(g, w, m, v, name):
    r, cdim = g.shape
    tr = r
    while tr * cdim * 4 > (3 << 19) and tr % 16 == 0:
        tr //= 2

    def body(g_ref, w_ref, m_ref, v_ref, go_ref, d_ref, nm_ref, nv_ref):
        grad = g_ref[...]
        delta, m2, v2 = _adamw_math(w_ref[0], grad, m_ref[0], v_ref[0])
        go_ref[0] = grad
        d_ref[0] = delta
        nm_ref[0] = m2
        nv_ref[0] = v2

    s3 = pl.BlockSpec((1, tr, cdim), lambda i: (0, i, 0))
    return pl.pallas_call(
        body, name=name, grid=(r // tr,),
        in_specs=[pl.BlockSpec((tr, cdim), lambda i: (i, 0)), s3, s3, s3],
        out_specs=[s3, s3, s3, s3],
        out_shape=[jax.ShapeDtypeStruct(w.shape, F32)] * 4,
        compiler_params=pltpu.CompilerParams(dimension_semantics=("parallel",), vmem_limit_bytes=VMEM_LIMIT_BYTES),
    )(g, w, m, v)


def _wada_update(sct, gm, w, m, v):
    _, r, cdim = w.shape
    tr = 256
    kp = sct.shape[1]

    def body(s_ref, g_ref, w_ref, m_ref, v_ref, gw_ref, d_ref, nm_ref, nv_ref):
        g = _dot(s_ref[...], g_ref[...])
        gw_ref[0] = g
        delta, m2, v2 = _adamw_math(w_ref[0], g, m_ref[0], v_ref[0])
        d_ref[0] = delta
        nm_ref[0] = m2
        nv_ref[0] = v2

    s3 = pl.BlockSpec((1, tr, cdim), lambda i: (0, i, 0))
    return pl.pallas_call(
        body, name="wada_update", grid=(r // tr,),
        in_specs=[pl.BlockSpec((tr, kp), lambda i: (i, 0)), _whole(gm.shape), s3, s3, s3],
        out_specs=[s3, s3, s3, s3],
        out_shape=[jax.ShapeDtypeStruct(w.shape, F32)] * 4,
        compiler_params=pltpu.CompilerParams(dimension_semantics=("parallel",), vmem_limit_bytes=VMEM_LIMIT_BYTES),
    )(sct, gm, w, m, v)


def _small_update(g1, g2, g2s, gwm, gwp, gbz, gsg, gps, params):
    names = ["b_ada", "pre_mix_g", "post_mix_g", "sgu_norm_g", "w_spatial", "b_spatial", "w_pool", "pool_scale",
             "pre_ffn_g", "post_ffn_g", "conv_w", "conv_b"]
    d = g1.shape[2]
    flat_in = [g1, g2, g2s, gwm, gwp, gbz, gsg, gps]
    n_g = len(flat_in)
    for nm in names:
        flat_in += list(params[nm])

    def body(*refs):
        g1_ref, g2_ref, g2s_ref, gwm_ref, gwp_ref, gbz_ref, gsg_ref, gps_ref = refs[:n_g]
        wmv = refs[n_g:n_g + 3 * len(names)]
        loss_ref = refs[n_g + 3 * len(names)]
        outs = refs[n_g + 3 * len(names) + 1:]

        def dsum(ref, idx):
            acc = ref[(0,) + idx].astype(F32)
            for dev in range(1, N_DEV):
                acc = acc + ref[(dev,) + idx].astype(F32)
            return acc

        def apply(pi, g, widx, oidx):
            w_ref, m_ref, v_ref = wmv[3 * pi:3 * pi + 3]
            g_ref, d_ref, nm_ref, nv_ref = outs[4 * pi:4 * pi + 4]
            delta, m2, v2 = _adamw_math(w_ref[widx], g, m_ref[widx], v_ref[widx])
            g_ref[oidx] = g
            d_ref[oidx] = delta
            nm_ref[oidx] = m2
            nv_ref[oidx] = v2

        def row1(base, r):
            return (slice(base + r, base + r + 1), slice(None))

        tot = dsum(g1_ref, row1(G1_F, F_LOSS))
        loss_ref[...] = jnp.zeros(loss_ref.shape, F32) + jnp.sum(tot) * (0.5 / d)
        mod_rows = [row1(G1_M, M_DSHM), row1(G1_M, M_DSCM), row1(G1_B, B_DGM), row1(G1_B, B_DSHF),
                    row1(G1_B, B_DSCF), row1(G1_F, F_DGF)]
        for j, rr in enumerate(mod_rows):
            cs = (slice(None), slice(j * d, (j + 1) * d))
            apply(0, dsum(g1_ref, rr), cs, cs)
        full = (slice(None), slice(None))
        apply(1, dsum(g1_ref, row1(G1_M, M_DPREMIX)), full, full)
        apply(2, dsum(g1_ref, row1(G1_B, B_DPOSTMIX)), full, full)
        apply(3, dsum(gsg_ref, (slice(0, N_HEADS), slice(None))), (0,), (0,))
        pos_i = lax.broadcasted_iota(jnp.int32, (HEAD, HEAD), 0)
        pos_j = lax.broadcasted_iota(jnp.int32, (HEAD, HEAD), 1)
        causal = (pos_j // CHUNK) <= (pos_i // CHUNK)
        for h in range(N_HEADS):
            blk = (slice(h * HEAD, (h + 1) * HEAD), slice(None))
            apply(4, jnp.where(causal, dsum(gwm_ref, blk), 0.0), (0, h), (0, h))
            apply(5, dsum(gbz_ref, (slice(h, h + 1), slice(None))), (0, slice(h, h + 1)), (0, slice(h, h + 1)))
            apply(6, dsum(gwp_ref, blk), (0, h), (0, h))
        apply(7, dsum(gps_ref, (slice(0, 1), slice(None))), full, full)
        apply(8, dsum(g1_ref, row1(G1_B, B_DPREFFN)), full, full)
        apply(9, dsum(g1_ref, row1(G1_F, F_DPOSTFFN)), full, full)
        apply(10, dsum(g2s_ref, (slice(C_DCW, C_DCW + 3), slice(None))), (0,), (0,))
        apply(11, dsum(g2_ref, (slice(C_DCB, C_DCB + 1), slice(None))), full, full)

    out_shape = [jax.ShapeDtypeStruct((8, HEAD), F32)]
    for nm in names:
        out_shape += [jax.ShapeDtypeStruct(params[nm][0].shape, F32)] * 4
    res = pl.pallas_call(
        body, name="small_update", out_shape=out_shape,
        compiler_params=pltpu.CompilerParams(vmem_limit_bytes=VMEM_LIMIT_BYTES),
    )(*flat_in)
    out = {nm: tuple(res[1 + 4 * i:5 + 4 * i]) for i, nm in enumerate(names)}
    return res[0], out


def kernel(x, c, w_ada, b_ada, pre_mix_g, post_mix_g, w_in, sgu_norm_g, w_spatial, b_spatial, w_pool, pool_scale, w_out, pre_ffn_g, post_ffn_g, w_up, conv_w, conv_b, w_down, loss_target, m_w_ada, m_b_ada, m_pre_mix_g, m_post_mix_g, m_w_in, m_sgu_norm_g, m_w_spatial, m_b_spatial, m_w_pool, m_pool_scale, m_w_out, m_pre_ffn_g, m_post_ffn_g, m_w_up, m_conv_w, m_conv_b, m_w_down, v_w_ada, v_b_ada, v_pre_mix_g, v_post_mix_g, v_w_in, v_sgu_norm_g, v_w_spatial, v_b_spatial, v_w_pool, v_pool_scale, v_w_out, v_pre_ffn_g, v_post_ffn_g, v_w_up, v_conv_w, v_conv_b, v_w_down):
    xi, yi, ci = _mesh_pos()
    k_me = 2 * xi + yi
    dev = 2 * k_me + ci
    s_len, d = x.shape[1], x.shape[2]
    x2d = x[0]
    tgt = loss_target[0]
    ff2 = conv_b.shape[1]
    n_ada = w_ada.shape[2]
    n_cw = conv_w.shape[2]

    k_idx = k_me.reshape(1).astype(jnp.int32)
    mix_flags = (True, False)
    (w_in_s, w_out_s), (w_in_f, w_out_f) = _cast_bf16([w_in[0], w_out[0]], mix_flags, k_idx, "cast_weights_mix")

    def place(t):
        return lax.dynamic_update_slice(jnp.zeros((N_DEV * t.shape[0], t.shape[1]), t.dtype), t,
                                        (dev * t.shape[0], 0))

    cw_blk = jnp.concatenate([conv_w[0], jnp.zeros((5, n_cw), F32)], axis=0)
    fly_c = _split_begin([place(c.reshape(8, d // 8)), place(cw_blk)], 6, _small_ici_copies, k_idx,
                         "c_gather_begin")
    fly_mix = _gather_begin([w_in_s, w_out_s], [w_in_f, w_out_f], mix_flags, fly_c[3], "gather_begin_mix")
    c_all, cw_all = _small_finish(_split_end(fly_c, _small_ici_copies, fly_mix[3], "c_gather_end"),
                                  "c_gather_finish")
    (w_up_s, w_down_s), (w_up_f, w_down_f) = _cast_bf16([w_up[0], w_down[0]], mix_flags, k_idx, "cast_weights_ffn",
                                                        after=fly_mix[3])
    c_all = c_all.reshape(N_DEV, 8, d // 8).reshape(N_DEV, d)
    cw_full = jnp.concatenate([cw_all[16 * k:16 * k + 8] for k in range(N_CHIP)], axis=1)
    cvec = jnp.concatenate([cw_full[0:3], conv_b, jnp.zeros((4, ff2), F32)], axis=0)
    b_shard = lax.dynamic_slice_in_dim(b_ada, k_me * n_ada, n_ada, axis=1)
    mod_k, sc_all = _mod_shard(c_all, w_ada[0], b_shard)
    (mod_g,) = _run_rider(_allgather_rider([mod_k]), "gather_mod")
    mod_all = jnp.concatenate([mod_g[16 * k:16 * k + 8] for k in range(N_CHIP)], axis=1)
    mod_me = lax.dynamic_slice_in_dim(mod_all, dev, 1, axis=0).reshape(6, d)
    vecs = jnp.concatenate([mod_me, pre_mix_g, post_mix_g, pre_ffn_g, post_ffn_g, jnp.zeros((6, d), F32)], axis=0)

    fly_ffn = _gather_begin([w_up_s, w_down_s], [w_up_f, w_down_f], mix_flags, mod_g, "gather_begin_ffn")
    w_in_b, w_out_b = _gather_finish(_gather_end(fly_mix, mix_flags, fly_ffn[3], "gather_end_mix"), mix_flags,
                                     "gather_finish_mix")

    pos = jnp.arange(HEAD)
    causal = (pos[None, :] // CHUNK) <= (pos[:, None] // CHUNK)
    wm = jnp.where(causal[None], w_spatial[0], 0.0)
    wm_b = wm.astype(BF16)
    wmt_b = jnp.swapaxes(wm, 1, 2).astype(BF16)
    bsb = jnp.broadcast_to(b_spatial[0][:, :, None], (N_HEADS, HEAD, HEAD))
    wp_b = w_pool[0].astype(BF16)
    sgu_g = jnp.concatenate([sgu_norm_g[0], jnp.zeros((4, HEAD), F32)], axis=0)
    ps = jnp.concatenate([pool_scale, jnp.zeros((7, A_WIDTH), F32)], axis=0)

    pmats = _perm_mats(FFN_TS)
    h1, proj, cat, mixed, x2, h2p = _mix_fwd(x2d, vecs, w_in_b, w_out_b, sgu_g, wm_b, bsb, wp_b, ps, pmats,
                                             ts=MIX_TS)[0]
    w_up_b, w_down_b = _gather_finish(_gather_end(fly_ffn, mix_flags, h2p, "gather_end_ffn"), mix_flags,
                                      "gather_finish_ffn")
    up, yv, act, dy, dfp, acc_f = _ffn_fwd(h2p, x2, tgt, w_up_b, w_down_b, cvec, vecs, pmats, ts=FFN_TS)

    c_idx = ci.reshape(1).astype(jnp.int32)
    g_w_down, _ = _wgrad(act, dfp, d, WGRAD_TS_WIDE, "wgrad_down")
    (dup, dx2, dmixed, acc_c, acc_b), (land_down,) = _ffn_bwd(
        dfp, up, yv, x2, dy, mixed, w_up_b, w_down_b, cvec, vecs, pmats, ts=FFN_TS,
        rider=_sibling_rider([g_w_down], (False,)))
    (part_down,) = _sum_with_sibling([g_w_down], [land_down], (False,), c_idx, "pair_sum_down")
    g_w_up, (chips_down,) = _wgrad(h2p, dup, ff2 // 2, WGRAD_TS, "wgrad_up", rider=_chips_rider([part_down]))
    fly_up = _sibling_begin(g_w_up, True, chips_down, "sibling_begin_up")
    gx, dproj, acc_m, dwm, dwp, dbz, dsg, dps = _mix_bwd(
        dmixed, dx2, x2d, proj, vecs + fly_up[3][0:1, 0:1], w_in_b, w_out_b, sgu_g, wm_b, wmt_b, bsb, wp_b, ps,
        ts=MIX_TS)[0]
    g_w_up, land_up = _sibling_end(fly_up, True, dproj, "sibling_end_up")
    (part_up,) = _sum_with_sibling([g_w_up], [land_up], (True,), c_idx, "pair_sum_up")
    fly_chips_up = _chips_begin([part_up], c_idx, "chips_begin_up")
    g1 = jnp.concatenate([acc_f, acc_b, acc_m], axis=0)
    hflat = (N_HEADS * HEAD, HEAD)
    small_bufs = [place(t) for t in (g1, acc_c, dwm.reshape(hflat), dwp.reshape(hflat), dbz, dsg, dps)]
    fly_small = _split_begin(small_bufs, 3 * len(small_bufs), _small_ici_copies, fly_chips_up[3],
                             "small_gather_begin")
    g_w_out, g_w_in = _wgrad_pair(cat, dmixed, h1, dproj, fly_small[3], WGRAD_TS_WIDE, "wgrad_mix")
    land_mix = _run_rider(_sibling_rider([g_w_in, g_w_out], (True, False)), "reduce_to_sibling")
    parts_mix = _sum_with_sibling([g_w_in, g_w_out], land_mix, (True, False), c_idx, "pair_sum_mix")
    (part_up,), (chips_up,) = _chips_end(fly_chips_up, parts_mix[0], "chips_end_up")
    fly_chips_mix = _chips_begin(parts_mix, chips_up, "chips_begin_mix")

    def adamw_of(names, reduced):
        res = {}
        for nm, red in zip(names, reduced):
            w, m, v = big_wmv[nm]
            g = red.reshape(w.shape[1], w.shape[2])
            res[nm] = tuple(_adamw_big(g, w, m, v, "adamw_" + nm))
        return res

    big_wmv = {"w_in": (w_in, m_w_in, v_w_in), "w_out": (w_out, m_w_out, v_w_out),
               "w_up": (w_up, m_w_up, v_w_up), "w_down": (w_down, m_w_down, v_w_down)}
    big = adamw_of(("w_up", "w_down"), _sum_chips_and_share([chips_up, chips_down], [part_up, part_down],
                                                           fly_chips_mix[3], "sum_share_ffn"))

    gathered = _small_finish(_split_end(fly_small, _small_ici_copies, big["w_down"][1], "small_gather_end"),
                             "small_gather_finish")
    g1a, g2a, gwm, gwp, gbz, gsg, gps = [t.reshape((N_DEV, t.shape[0] // N_DEV, t.shape[1])) for t in gathered]
    g2s = lax.dynamic_slice_in_dim(g2a, k_me * n_cw, n_cw, axis=2)
    params = {
        "b_ada": (b_ada, m_b_ada, v_b_ada), "pre_mix_g": (pre_mix_g, m_pre_mix_g, v_pre_mix_g),
        "post_mix_g": (post_mix_g, m_post_mix_g, v_post_mix_g),
        "sgu_norm_g": (sgu_norm_g, m_sgu_norm_g, v_sgu_norm_g), "w_spatial": (w_spatial, m_w_spatial, v_w_spatial),
        "b_spatial": (b_spatial, m_b_spatial, v_b_spatial), "w_pool": (w_pool, m_w_pool, v_w_pool),
        "pool_scale": (pool_scale, m_pool_scale, v_pool_scale), "pre_ffn_g": (pre_ffn_g, m_pre_ffn_g, v_pre_ffn_g),
        "post_ffn_g": (post_ffn_g, m_post_ffn_g, v_post_ffn_g), "conv_w": (conv_w, m_conv_w, v_conv_w),
        "conv_b": (conv_b, m_conv_b, v_conv_b),
    }
    loss_slab, small = _small_update(g1a, g2a, g2s, gwm, gwp, gbz, gsg, gps, params)

    gmod_all = jnp.concatenate(
        [g1a[:, G1_M + M_DSHM], g1a[:, G1_M + M_DSCM], g1a[:, G1_B + B_DGM], g1a[:, G1_B + B_DSHF],
         g1a[:, G1_B + B_DSCF], g1a[:, G1_F + F_DGF]], axis=1)
    gm = lax.dynamic_slice_in_dim(gmod_all, k_me * n_ada, n_ada, axis=1)
    gm = jnp.concatenate([gm, jnp.zeros((HEAD - N_DEV, n_ada), F32)], axis=0)
    sct = jnp.concatenate([sc_all.T, jnp.zeros((d, HEAD - N_DEV), F32)], axis=1)
    ada = tuple(_wada_update(sct, gm, w_ada, m_w_ada, v_w_ada))

    parts_mix, chips_mix = _chips_end(fly_chips_mix, ada[1], "chips_end_mix")
    big.update(adamw_of(("w_in", "w_out"), _sum_chips_and_share(chips_mix, parts_mix, loss_slab, "sum_share_mix")))

    everything = dict(small)
    everything.update(big)
    everything["w_ada"] = ada
    order = ["w_ada", "b_ada", "pre_mix_g", "post_mix_g", "w_in", "sgu_norm_g", "w_spatial", "b_spatial", "w_pool",
             "pool_scale", "w_out", "pre_ffn_g", "post_ffn_g", "w_up", "conv_w", "conv_b", "w_down"]
    outs = [loss_slab[0, 0], gx.reshape(x.shape)]
    for j in range(4):
        outs += [everything[nm][j] for nm in order]
    return tuple(outs)
```

```python
import functools

import jax
import jax.numpy as jnp
from jax import lax
from jax.experimental import pallas as pl
from jax.experimental.pallas import tpu as pltpu

F32 = jnp.float32
BF16 = jnp.bfloat16
MESH = pl.DeviceIdType.MESH

EPS = 1e-6
HEAD = 128
N_HEADS = 4
A_WIDTH = N_HEADS * HEAD
CHUNK = 64
POOL_WINDOWS = (2, 4, 8, 16)
POOL_HALO = 16
SUBLANES = 8
CONV_KEEP = 2 * SUBLANES
FFN_TS = 256
MIX_TS = 512
WGRAD_TS = 2048
WGRAD_TS_WIDE = 1024

ADAM_LR = 0.001
ADAM_B1 = 0.9
ADAM_B2 = 0.999
ADAM_EPS = 1e-08
ADAM_WD = 0.01
ADAM_STEP = 10

VMEM_LIMIT_BYTES = 58 * 1024 * 1024
N_DEV = 8
N_CHIP = 4


def _dot(a, b):
    return jnp.dot(a, b, preferred_element_type=F32)


def _dot_nt(a, b):
    return lax.dot_general(a, b, (((1,), (1,)), ((), ())), preferred_element_type=F32)


def _dot_tn(a, b):
    return lax.dot_general(a, b, (((0,), (0,)), ((), ())), preferred_element_type=F32)


GELU_C0 = 0.7978845608028654
GELU_C1 = GELU_C0 * 0.044715


def _gelu(x):
    return x * (0.5 + 0.5 * jnp.tanh(x * (GELU_C0 + GELU_C1 * (x * x))))


def _gelu_and_grad(x):
    x2 = x * x
    t = jnp.tanh(x * (GELU_C0 + GELU_C1 * x2))
    half = 0.5 + 0.5 * t
    grad = half + (x * (0.5 - 0.5 * (t * t))) * (GELU_C0 + (3.0 * GELU_C1) * x2)
    return x * half, grad


def _sigmoid(x):
    return 1.0 / (1.0 + jnp.exp(-x))


def _rms(x):
    return lax.rsqrt(jnp.mean(x * x, axis=-1, keepdims=True) + EPS)


def _colsum(x):
    return jnp.sum(x, axis=0, keepdims=True)


def _rowmean(x):
    return jnp.mean(x, axis=-1, keepdims=True)


def _tiled(shape, index_map):
    return pl.BlockSpec(shape, index_map)


def _resident(shape):
    nd = len(shape)
    return pl.BlockSpec(shape, lambda *_: (0,) * nd, pipeline_mode=pl.Buffered(1))


def _whole(shape):
    nd = len(shape)
    return pl.BlockSpec(shape, lambda *_: (0,) * nd)


def _seq_params():
    return pltpu.CompilerParams(dimension_semantics=("arbitrary",), vmem_limit_bytes=VMEM_LIMIT_BYTES)


def _ff_chunks(f, width=768):
    out, o = [], 0
    while o < f:
        w = min(width, f - o)
        out.append((o, w))
        o += w
    return out


def _pool_fwd(p, halo, g, t_glob):
    ext = jnp.concatenate([halo, p], axis=0)
    s = ext
    for step in range(g + 1):
        s = s + pltpu.roll(s, 1 << step, 0)
    cnt = jnp.minimum(t_glob + 1, POOL_WINDOWS[g]).astype(F32)
    return s[POOL_HALO:] / cnt - p


def _adamw_math(w, g, m, v):
    m = ADAM_B1 * m + (1.0 - ADAM_B1) * g
    v = ADAM_B2 * v + (1.0 - ADAM_B2) * (g * g)
    m_hat = m / (1.0 - ADAM_B1 ** ADAM_STEP)
    v_hat = v / (1.0 - ADAM_B2 ** ADAM_STEP)
    delta = -ADAM_LR * (m_hat / (jnp.sqrt(v_hat) + ADAM_EPS) + ADAM_WD * w)
    return delta, m, v


def _mesh_pos():
    return lax.axis_index("x"), lax.axis_index("y"), lax.axis_index("c")


class _Rider:
    def __init__(self, inputs, out_shape, sems, start, finish):
        self.inputs, self.out_shape, self.sems = list(inputs), list(out_shape), list(sems)
        self.start, self.finish = start, finish


def _call(body, *, name, grid, in_specs, out_specs, out_shape, scratch_shapes, args, rider=None):
    params = pltpu.CompilerParams(dimension_semantics=("arbitrary",) * len(grid), vmem_limit_bytes=VMEM_LIMIT_BYTES)
    if rider is None:
        res = pl.pallas_call(body, name=name, grid=grid, in_specs=in_specs, out_specs=out_specs, out_shape=out_shape,
                             scratch_shapes=scratch_shapes, compiler_params=params)(*args)
        return tuple(res), ()
    cuts = [len(in_specs), len(rider.inputs), len(out_specs), len(rider.out_shape), len(scratch_shapes),
            len(rider.sems)]

    def hosted(*refs):
        groups, a = [], 0
        for cnt in cuts:
            groups.append(refs[a:a + cnt])
            a += cnt
        ins, r_in, outs, r_out, scr, r_sem = groups
        first = functools.reduce(jnp.logical_and, [pl.program_id(k) == 0 for k in range(len(grid))])
        last = functools.reduce(jnp.logical_and, [pl.program_id(k) == grid[k] - 1 for k in range(len(grid))])

        @pl.when(first)
        def _():
            rider.start(r_in, r_out, r_sem)

        body(*ins, *outs, *scr)

        @pl.when(last)
        def _():
            rider.finish(r_in, r_out, r_sem)

    anyspec = pl.BlockSpec(memory_space=pl.ANY)
    res = pl.pallas_call(
        hosted, name=name, grid=grid,
        in_specs=list(in_specs) + [anyspec] * cuts[1], out_specs=list(out_specs) + [anyspec] * cuts[3],
        out_shape=list(out_shape) + rider.out_shape, scratch_shapes=list(scratch_shapes) + rider.sems,
        compiler_params=params)(*args, *rider.inputs)
    return tuple(res[:cuts[2]]), tuple(res[cuts[2]:])


def _run_rider(rider, name):
    n_in, n_out = len(rider.inputs), len(rider.out_shape)

    def body(*refs):
        r_in, r_out, r_sem = refs[:n_in], refs[n_in:n_in + n_out], refs[n_in + n_out:]
        rider.start(r_in, r_out, r_sem)
        rider.finish(r_in, r_out, r_sem)

    anyspec = pl.BlockSpec(memory_space=pl.ANY)
    return pl.pallas_call(body, name=name, out_shape=rider.out_shape, in_specs=[anyspec] * n_in,
                          out_specs=[anyspec] * n_out, scratch_shapes=rider.sems)(*rider.inputs)


def _allgather_rider(arrs):
    n = len(arrs)

    def plan(ins, outs, sems):
        send_sems, recv_sems, local_sems = sems
        x, y, c = _mesh_pos()
        me, sibling = (x, y, c), (x, y, 1 - c)
        chips = [(1 - x, y), (x, 1 - y), (1 - x, 1 - y)]

        def rows(a, px, py, pc):
            r = ins[a].shape[0]
            return outs[a].at[pl.ds(pl.multiple_of((4 * px + 2 * py + pc) * r, 8), r), :]

        def copy(a, k, block, to, src=None):
            return pltpu.make_async_remote_copy(
                src_ref=rows(a, *block) if src is None else src, dst_ref=rows(a, *block),
                send_sem=send_sems.at[a * 7 + k], recv_sem=recv_sems.at[a * 7 + k],
                device_id=to, device_id_type=MESH)

        local = [pltpu.make_async_copy(ins[a], rows(a, *me), local_sems.at[a]) for a in range(n)]
        first = []
        for a in range(n):
            first.append(copy(a, 0, me, sibling, src=ins[a]))
            first += [copy(a, 1 + j, me, (*chip, c), src=ins[a]) for j, chip in enumerate(chips)]
        return c, me, sibling, chips, copy, local, first

    def start(ins, outs, sems):
        *_, local, first = plan(ins, outs, sems)
        for cp in local + first:
            cp.start()

    def finish(ins, outs, sems):
        c, me, sibling, chips, copy, local, first = plan(ins, outs, sems)
        passed = []
        for a in range(n):
            for j, chip in enumerate(chips):
                copy(a, 1 + j, (*chip, c), me).wait_recv()
                fwd = copy(a, 4 + j, (*chip, c), sibling)
                fwd.start()
                passed.append(fwd)
        for a in range(n):
            copy(a, 0, sibling, me).wait_recv()
            for j, chip in enumerate(chips):
                copy(a, 4 + j, (*chip, 1 - c), me).wait_recv()
        for cp in first + passed:
            cp.wait_send()
        for mine in local:
            mine.wait()

    return _Rider(arrs, [jax.ShapeDtypeStruct((N_DEV * a.shape[0], a.shape[1]), a.dtype) for a in arrs],
                  [pltpu.SemaphoreType.DMA((7 * n,)), pltpu.SemaphoreType.DMA((7 * n,)),
                   pltpu.SemaphoreType.DMA((n,))], start, finish)


def _piece(ref, col_sharded, k, h):
    m, n = ref.shape
    if col_sharded:
        mh, nc = m // 2, n // N_CHIP
        return ref.at[pl.ds(pl.multiple_of(h * mh, 16), mh), pl.ds(pl.multiple_of(k * nc, 128), nc)]
    rp = m // (2 * N_CHIP)
    return ref.at[pl.ds(pl.multiple_of((2 * k + h) * rp, 16), rp), :]


def _piece_shape(shape, col_sharded):
    m, n = shape
    return (m // 2, n // N_CHIP) if col_sharded else (m // (2 * N_CHIP), n)


def _cast_bf16(arrs, col_flags, k_idx, name, after=None):
    n = len(arrs)
    extra = [] if after is None else [after]

    def body(k_ref, *refs):
        outs = refs[n + len(extra):]
        for a in range(n):
            val = refs[a][...].astype(BF16)
            outs[a][...] = val
            outs[n + a][...] = val

    whole = [pl.BlockSpec(a.shape, lambda i, k_ref: (0, 0)) for a in arrs]
    window = [pl.BlockSpec(a.shape, (lambda i, k_ref: (0, k_ref[0])) if col else (lambda i, k_ref: (k_ref[0], 0)))
              for a, col in zip(arrs, col_flags)]
    res = pl.pallas_call(
        body, name=name,
        grid_spec=pltpu.PrefetchScalarGridSpec(
            num_scalar_prefetch=1, grid=(1,),
            in_specs=whole + [pl.BlockSpec(t.shape, lambda i, k_ref: (0, 0)) for t in extra],
            out_specs=whole + window),
        out_shape=[jax.ShapeDtypeStruct(a.shape, BF16) for a in arrs]
        + [jax.ShapeDtypeStruct(fs, BF16) for fs in _full_shapes(arrs, col_flags)],
        compiler_params=pltpu.CompilerParams(vmem_limit_bytes=VMEM_LIMIT_BYTES))(k_idx, *arrs, *extra)
    return list(res[:n]), list(res[n:])


def _full_shapes(shards, col_flags):
    return [(s.shape[0], s.shape[1] * N_CHIP) if col else (s.shape[0] * N_CHIP, s.shape[1])
            for s, col in zip(shards, col_flags)]


def _ici_copies(shard_refs, full_refs, send_sems, recv_sems, col_flags):
    x, y, c = _mesh_pos()
    k_me = 2 * x + y
    copies = []
    for a, (s_ref, f_ref) in enumerate(zip(shard_refs, full_refs)):
        rows = s_ref.shape[0] // 2
        src = s_ref.at[pl.ds(pl.multiple_of(c * rows, 16), rows), :]
        for j, chip in enumerate([(1 - x, y), (x, 1 - y), (1 - x, 1 - y)]):
            copies.append(pltpu.make_async_remote_copy(
                src_ref=src, dst_ref=_piece(f_ref, col_flags[a], k_me, c),
                send_sem=send_sems.at[a * 3 + j], recv_sem=recv_sems.at[a * 3 + j],
                device_id=(*chip, c), device_id_type=MESH))
    return copies


def _split_begin(bufs, n_sems, make_copies, after, name):
    n = len(bufs)
    hbm = pl.BlockSpec(memory_space=pltpu.HBM)
    sem = pl.BlockSpec(memory_space=pltpu.SEMAPHORE)

    def body(*refs):
        for cp in make_copies(refs[:n], refs[n + 1], refs[n + 2]):
            cp.start()
        refs[-1][...] = jnp.zeros_like(refs[-1])

    args = [pltpu.with_memory_space_constraint(t, pltpu.HBM) for t in bufs]
    res = pl.pallas_call(
        body, name=name,
        out_shape=[pltpu.SemaphoreType.DMA((n_sems,)), pltpu.SemaphoreType.DMA((n_sems,))]
        + [pltpu.HBM(t.shape, t.dtype) for t in args] + [jax.ShapeDtypeStruct((8, HEAD), F32)],
        in_specs=[hbm] * n + [pl.BlockSpec(memory_space=pl.ANY)],
        out_specs=[sem, sem] + [hbm] * n + [pl.BlockSpec(memory_space=pltpu.VMEM)],
        input_output_aliases={i: 2 + i for i in range(n)},
        compiler_params=pltpu.CompilerParams(has_side_effects=pltpu.SideEffectType.DATAFLOW_SIDE_EFFECTING),
    )(*args, after)
    return res[0], res[1], list(res[2:2 + n]), res[-1]


def _split_end(handle, make_copies, after, name):
    send_sems, recv_sems, bufs, _ = handle
    n = len(bufs)
    hbm = pl.BlockSpec(memory_space=pltpu.HBM)
    sem = pl.BlockSpec(memory_space=pltpu.SEMAPHORE)

    def body(*refs):
        for cp in make_copies(refs[:n], refs[n], refs[n + 1]):
            cp.wait_send()
            cp.wait_recv()

    res = pl.pallas_call(
        body, name=name,
        out_shape=[pltpu.HBM(t.shape, t.dtype) for t in bufs],
        in_specs=[hbm] * n + [sem, sem, pl.BlockSpec(memory_space=pl.ANY)],
        out_specs=[hbm] * n,
        input_output_aliases={i: i for i in range(n)},
        compiler_params=pltpu.CompilerParams(has_side_effects=pltpu.SideEffectType.DATAFLOW_SIDE_EFFECTING),
    )(*bufs, send_sems, recv_sems, after)
    return list(res)


def _gather_copies(n, col_flags):
    return lambda refs, send_sems, recv_sems: _ici_copies(refs[:n], refs[n:], send_sems, recv_sems, col_flags)


def _gather_begin(shards, fulls, col_flags, after, name):
    n = len(shards)
    return _split_begin(list(shards) + list(fulls), 3 * n, _gather_copies(n, col_flags), after, name)


def _gather_end(handle, col_flags, after, name):
    n = len(handle[2]) // 2
    return _split_end(handle, _gather_copies(n, col_flags), after, name)[n:]


def _sibling_copies(col_flag):
    def make(refs, send_sems, recv_sems):
        grad_ref, land_ref = refs
        x, y, c = _mesh_pos()
        return [pltpu.make_async_remote_copy(
            src_ref=_piece(grad_ref, col_flag, k, 1 - c), dst_ref=land_ref.at[k],
            send_sem=send_sems.at[k], recv_sem=recv_sems.at[k],
            device_id=(x, y, 1 - c), device_id_type=MESH) for k in range(N_CHIP)]
    return make


def _chips_copies(n):
    def make(refs, send_sems, recv_sems):
        parts, landed = refs[:n], refs[n:]
        x, y, c = _mesh_pos()
        k_me = 2 * x + y
        copies = []
        for a in range(n):
            for j, chip in enumerate([(1 - x, y), (x, 1 - y), (1 - x, 1 - y)]):
                copies.append(pltpu.make_async_remote_copy(
                    src_ref=parts[a].at[2 * chip[0] + chip[1]], dst_ref=landed[a].at[k_me],
                    send_sem=send_sems.at[a * 3 + j], recv_sem=recv_sems.at[a * 3 + j],
                    device_id=(*chip, c), device_id_type=MESH))
        return copies
    return make


def _chips_begin(parts, after, name):
    landed = [lax.empty(p.shape, p.dtype) for p in parts]
    return _split_begin(list(parts) + landed, 3 * len(parts), _chips_copies(len(parts)), after, name)


def _chips_end(handle, after, name):
    n = len(handle[2]) // 2
    res = _split_end(handle, _chips_copies(n), after, name)
    return res[:n], res[n:]


def _small_rows(buf_ref, px, py, pc):
    r = buf_ref.shape[0] // N_DEV
    return buf_ref.at[pl.ds(pl.multiple_of((4 * px + 2 * py + pc) * r, 8), r), :]


def _small_ici_copies(refs, send_sems, recv_sems):
    x, y, c = _mesh_pos()
    copies = []
    for a, buf in enumerate(refs):
        mine = _small_rows(buf, x, y, c)
        for j, chip in enumerate([(1 - x, y), (x, 1 - y), (1 - x, 1 - y)]):
            copies.append(pltpu.make_async_remote_copy(
                src_ref=mine, dst_ref=mine, send_sem=send_sems.at[a * 3 + j], recv_sem=recv_sems.at[a * 3 + j],
                device_id=(*chip, c), device_id_type=MESH))
    return copies


def _small_finish(bufs, name):
    n = len(bufs)

    def body(*refs):
        buf_refs = refs[n:2 * n]
        send_sems, recv_sems = refs[2 * n:]
        x, y, c = _mesh_pos()
        owners = [(x, y), (1 - x, y), (x, 1 - y), (1 - x, 1 - y)]
        passed, arriving = [], []
        for a in range(n):
            for j, (px, py) in enumerate(owners):
                for pc, group in ((c, passed), (1 - c, arriving)):
                    rows = _small_rows(buf_refs[a], px, py, pc)
                    group.append(pltpu.make_async_remote_copy(
                        src_ref=rows, dst_ref=rows, send_sem=send_sems.at[a * 4 + j],
                        recv_sem=recv_sems.at[a * 4 + j], device_id=(x, y, 1 - c), device_id_type=MESH))
        for cp in passed:
            cp.start()
        for cp in arriving:
            cp.wait_recv()
        for cp in passed:
            cp.wait_send()

    anyspec = pl.BlockSpec(memory_space=pl.ANY)
    return pl.pallas_call(
        body, name=name, out_shape=[jax.ShapeDtypeStruct(b.shape, b.dtype) for b in bufs],
        in_specs=[anyspec] * n, out_specs=[anyspec] * n, input_output_aliases={a: a for a in range(n)},
        scratch_shapes=[pltpu.SemaphoreType.DMA((4 * n,)), pltpu.SemaphoreType.DMA((4 * n,))],
    )(*bufs)


def _sibling_begin(grad, col_flag, after, name):
    land = lax.empty((N_CHIP,) + _piece_shape(grad.shape, col_flag), grad.dtype)
    return _split_begin([grad, land], N_CHIP, _sibling_copies(col_flag), after, name)


def _sibling_end(handle, col_flag, after, name):
    return _split_end(handle, _sibling_copies(col_flag), after, name)


def _gather_finish(fulls, col_flags, name):
    n = len(fulls)

    def body(*refs):
        full_refs = refs[n:2 * n]
        send_sems, recv_sems = refs[2 * n:]
        x, y, c = _mesh_pos()
        passed, arriving = [], []
        for a in range(n):
            for j, chip in enumerate([(1 - x, y), (x, 1 - y), (1 - x, 1 - y)]):
                k_from = 2 * chip[0] + chip[1]
                for h, group in ((c, passed), (1 - c, arriving)):
                    win = _piece(full_refs[a], col_flags[a], k_from, h)
                    group.append(pltpu.make_async_remote_copy(
                        src_ref=win, dst_ref=win, send_sem=send_sems.at[a * 3 + j],
                        recv_sem=recv_sems.at[a * 3 + j], device_id=(x, y, 1 - c), device_id_type=MESH))
        for cp in passed:
            cp.start()
        for cp in arriving:
            cp.wait_recv()
        for cp in passed:
            cp.wait_send()

    anyspec = pl.BlockSpec(memory_space=pl.ANY)
    return pl.pallas_call(
        body, name=name,
        out_shape=[jax.ShapeDtypeStruct(f.shape, f.dtype) for f in fulls],
        in_specs=[anyspec] * n, out_specs=[anyspec] * n,
        input_output_aliases={a: a for a in range(n)},
        scratch_shapes=[pltpu.SemaphoreType.DMA((3 * n,)), pltpu.SemaphoreType.DMA((3 * n,))],
    )(*fulls)


def _sibling_rider(grads, col_flags):
    n = len(grads)
    pshapes = [_piece_shape(g.shape, col) for g, col in zip(grads, col_flags)]

    def copies(ins, outs, sems):
        send_sems, recv_sems = sems
        x, y, c = _mesh_pos()
        return [pltpu.make_async_remote_copy(
            src_ref=_piece(ins[a], col_flags[a], k, 1 - c), dst_ref=outs[a].at[k],
            send_sem=send_sems.at[a * N_CHIP + k], recv_sem=recv_sems.at[a * N_CHIP + k],
            device_id=(x, y, 1 - c), device_id_type=MESH) for a in range(n) for k in range(N_CHIP)]

    def start(ins, outs, sems):
        for cp in copies(ins, outs, sems):
            cp.start()

    def finish(ins, outs, sems):
        cps = copies(ins, outs, sems)
        for cp in cps:
            cp.wait_recv()
        for cp in cps:
            cp.wait_send()

    return _Rider(grads, [jax.ShapeDtypeStruct((N_CHIP,) + ps, g.dtype) for ps, g in zip(pshapes, grads)],
                  [pltpu.SemaphoreType.DMA((N_CHIP * n,)), pltpu.SemaphoreType.DMA((N_CHIP * n,))], start, finish)


def _sum_with_sibling(grads, landed, col_flags, c_idx, name):
    n = len(grads)
    pshapes = [_piece_shape(g.shape, col) for g, col in zip(grads, col_flags)]

    def body(c_ref, *refs):
        ins, lands, outs = refs[:n], refs[n:2 * n], refs[2 * n:]
        for a in range(n):
            outs[a][0] = (ins[a][...] + lands[a][0]).astype(BF16)

    in_specs = []
    for ps, col in zip(pshapes, col_flags):
        if col:
            in_specs.append(pl.BlockSpec(ps, lambda k, c_ref: (c_ref[0], k)))
        else:
            in_specs.append(pl.BlockSpec(ps, lambda k, c_ref: (2 * k + c_ref[0], 0)))
    land_specs = [pl.BlockSpec((1,) + ps, lambda k, c_ref: (k, 0, 0)) for ps in pshapes]
    return pl.pallas_call(
        body, name=name,
        grid_spec=pltpu.PrefetchScalarGridSpec(
            num_scalar_prefetch=1, grid=(N_CHIP,),
            in_specs=in_specs + land_specs, out_specs=land_specs),
        out_shape=[jax.ShapeDtypeStruct((N_CHIP,) + ps, BF16) for ps in pshapes],
        compiler_params=pltpu.CompilerParams(dimension_semantics=("arbitrary",), vmem_limit_bytes=VMEM_LIMIT_BYTES),
    )(c_idx, *grads, *landed)


def _chips_rider(parts):
    n = len(parts)

    def plan(ins, outs, sems, arriving):
        send_sems, recv_sems = sems
        x, y, c = _mesh_pos()
        k_me = 2 * x + y
        copies = []
        for a in range(n):
            for j, chip in enumerate([(1 - x, y), (x, 1 - y), (1 - x, 1 - y)]):
                k_peer = 2 * chip[0] + chip[1]
                copies.append(pltpu.make_async_remote_copy(
                    src_ref=ins[a].at[k_peer], dst_ref=outs[a].at[k_peer if arriving else k_me],
                    send_sem=send_sems.at[a * 3 + j], recv_sem=recv_sems.at[a * 3 + j],
                    device_id=(*chip, c), device_id_type=MESH))
        return copies

    def start(ins, outs, sems):
        for cp in plan(ins, outs, sems, False):
            cp.start()

    def finish(ins, outs, sems):
        arrivals = plan(ins, outs, sems, True)
        for cp in arrivals:
            cp.wait_recv()
        for cp in arrivals:
            cp.wait_send()

    return _Rider(parts, [jax.ShapeDtypeStruct(p.shape, p.dtype) for p in parts],
                  [pltpu.SemaphoreType.DMA((3 * n,)), pltpu.SemaphoreType.DMA((3 * n,))], start, finish)


def _sum_chips_and_share(landed, parts, after, name):
    n = len(landed)

    def body(*refs):
        ins, own, outs, red = refs[:n], refs[n:2 * n], refs[2 * n + 1:3 * n + 1], refs[3 * n + 1:4 * n + 1]
        send_sems, recv_sems, local_sems = refs[4 * n + 1:]
        x, y, c = _mesh_pos()
        sibling = (x, y, 1 - c)
        k_me = 2 * x + y
        copies, local = [], []
        for a in range(n):
            for k in range(N_CHIP):
                @pl.when(k_me == k)
                def _():
                    term = own[a][k].astype(F32)
                    red[a][...] = term if k == 0 else red[a][...] + term

                @pl.when(k_me != k)
                def _():
                    term = ins[a][k].astype(F32)
                    red[a][...] = term if k == 0 else red[a][...] + term

            mine = pltpu.make_async_copy(red[a], outs[a].at[c], local_sems.at[a])
            mine.start()
            local.append(mine)
            cp = pltpu.make_async_remote_copy(
                src_ref=red[a], dst_ref=outs[a].at[c],
                send_sem=send_sems.at[a], recv_sem=recv_sems.at[a],
                device_id=sibling, device_id_type=MESH)
            cp.start()
            copies.append(cp)
        for a in range(n):
            pltpu.make_async_remote_copy(
                src_ref=red[a], dst_ref=outs[a].at[1 - c],
                send_sem=send_sems.at[a], recv_sem=recv_sems.at[a],
                device_id=sibling, device_id_type=MESH).wait_recv()
        for cp in copies:
            cp.wait_send()
        for mine in local:
            mine.wait()

    return pl.pallas_call(
        body, name=name,
        out_shape=[jax.ShapeDtypeStruct((2,) + l.shape[1:], F32) for l in landed],
        in_specs=[pl.BlockSpec(memory_space=pltpu.VMEM)] * (2 * n + 1),
        out_specs=[pl.BlockSpec(memory_space=pl.ANY)] * n,
        scratch_shapes=[pltpu.VMEM(l.shape[1:], F32) for l in landed]
        + [pltpu.SemaphoreType.DMA((n,)), pltpu.SemaphoreType.DMA((n,)), pltpu.SemaphoreType.DMA((n,))],
        compiler_params=pltpu.CompilerParams(vmem_limit_bytes=VMEM_LIMIT_BYTES),
    )(*landed, *parts, after)


def _mod_shard(c_all, w_ada, b_shard, after):
    def body(c_ref, w_ref, b_ref, after_ref, o_ref, sc_ref):
        cc = c_ref[...]
        sc = cc * _sigmoid(cc)
        sc_ref[...] = sc
        o_ref[...] = _dot(sc, w_ref[...]) + b_ref[...]

    nb, d = c_all.shape
    nn = w_ada.shape[1]
    vm = pl.BlockSpec(memory_space=pltpu.VMEM)
    return pl.pallas_call(
        body, name="mod_shard",
        in_specs=[vm, vm, vm, pl.BlockSpec(memory_space=pl.ANY)], out_specs=[vm, vm],
        out_shape=[jax.ShapeDtypeStruct((nb, nn), F32), jax.ShapeDtypeStruct((nb, d), F32)],
        compiler_params=pltpu.CompilerParams(vmem_limit_bytes=VMEM_LIMIT_BYTES),
    )(c_all, w_ada, b_shard, after)


V_SH_M, V_SC_M, V_G_M, V_SH_F, V_SC_F, V_G_F, V_PRE_MIX, V_POST_MIX, V_PRE_FFN, V_POST_FFN = range(10)


def _vrow(vec_ref, r):
    return vec_ref[r:r + 1, :]


def _mix_fwd(x, vecs, w_in_b, w_out_b, sgu_g, wm_b, bsb, wp_b, ps, pmats, ts):
    s_len, d = x.shape
    nt = s_len // ts
    n_proj = w_in_b.shape[1]

    def body(x_ref, vec_ref, win_ref, wout_ref, sg_ref, wm_ref, bs_ref, wp_ref, ps_ref, pm_ref,
             h1_ref, proj_ref, cat_ref, mixed_ref, x2_ref, h2_ref, carry_ref):
        i = pl.program_id(0)

        @pl.when(i == 0)
        def _():
            carry_ref[...] = jnp.zeros_like(carry_ref)

        sub = FFN_TS
        nsub = ts // sub

        def project(s):
            rs = slice(s * sub, (s + 1) * sub)
            x = x_ref[rs, :]
            h1 = (((x * _rms(x)) * _vrow(vec_ref, V_PRE_MIX)) * (1.0 + _vrow(vec_ref, V_SC_M))
                  + _vrow(vec_ref, V_SH_M)).astype(BF16)
            h1_ref[rs, :] = h1
            proj = _dot(h1, win_ref[...])
            proj_ref[rs, :] = proj.astype(BF16)
            return proj

        def mix(s, proj, halo):
            r0 = s * sub
            t_glob = lax.broadcasted_iota(jnp.int32, (sub, HEAD), 0) + (i * ts + r0)
            for h in range(N_HEADS):
                u = _gelu(proj[:, h * HEAD:(h + 1) * HEAD])
                v = _gelu(proj[:, A_WIDTH + h * HEAD:A_WIDTH + (h + 1) * HEAD])
                vn = ((v * _rms(v)) * sg_ref[h:h + 1, :]).astype(BF16)
                for b in range(sub // HEAD):
                    rs = slice(b * HEAD, (b + 1) * HEAD)
                    z = _dot(wm_ref[h], vn[rs]) + bs_ref[h]
                    cat_ref[r0 + b * HEAD:r0 + (b + 1) * HEAD, h * HEAD:(h + 1) * HEAD] = (u[rs] * z).astype(BF16)
            for g in range(len(POOL_WINDOWS)):
                gs = slice(g * HEAD, (g + 1) * HEAD)
                p = proj[:, 2 * A_WIDTH + g * HEAD:2 * A_WIDTH + (g + 1) * HEAD]
                pooled = _pool_fwd(p, halo[:, gs], g, t_glob)
                yb = _dot(pooled.astype(BF16), wp_ref[g]) * ps_ref[0:1, gs]
                cat_ref[r0:r0 + sub, A_WIDTH + g * HEAD:A_WIDTH + (g + 1) * HEAD] = yb.astype(BF16)

        def finish(s, mixed):
            rs = slice(s * sub, (s + 1) * sub)
            mixed_ref[rs, :] = mixed
            x2 = x_ref[rs, :] + _vrow(vec_ref, V_G_M) * ((mixed * _rms(mixed)) * _vrow(vec_ref, V_POST_MIX))
            x2_ref[rs, :] = x2
            h2 = (((x2 * _rms(x2)) * _vrow(vec_ref, V_PRE_FFN)) * (1.0 + _vrow(vec_ref, V_SC_F))
                  + _vrow(vec_ref, V_SH_F)).astype(BF16)
            h2_ref[rs, :] = _permute_bf16(pm_ref[0], h2)

        projs = [project(0)]
        halo = carry_ref[...]
        mixed_prev = None
        for s in range(nsub):
            if s + 1 < nsub:
                projs.append(project(s + 1))
            mix(s, projs[s], halo)
            halo = projs[s][sub - POOL_HALO:sub, 2 * A_WIDTH:]
            mixed = _dot(cat_ref[s * sub:(s + 1) * sub, :], wout_ref[...])
            if mixed_prev is not None:
                finish(s - 1, mixed_prev)
            mixed_prev = mixed
        carry_ref[...] = halo
        finish(nsub - 1, mixed_prev)

    row = lambda i: (i, 0)
    return _call(
        body, name="mix_fwd", grid=(nt,),
        in_specs=[_tiled((ts, d), row), _whole(vecs.shape), _resident(w_in_b.shape), _resident(w_out_b.shape),
                  _whole(sgu_g.shape), _whole(wm_b.shape), _whole(bsb.shape), _whole(wp_b.shape), _whole(ps.shape),
                  _whole(pmats.shape)],
        out_specs=[_tiled((ts, d), row), _tiled((ts, n_proj), row), _tiled((ts, d), row),
                   _tiled((ts, d), row), _tiled((ts, d), row), _tiled((ts, d), row)],
        out_shape=[jax.ShapeDtypeStruct((s_len, d), BF16), jax.ShapeDtypeStruct((s_len, n_proj), BF16),
                   jax.ShapeDtypeStruct((s_len, d), BF16), jax.ShapeDtypeStruct((s_len, d), F32),
                   jax.ShapeDtypeStruct((s_len, d), F32), jax.ShapeDtypeStruct((s_len, d), BF16)],
        scratch_shapes=[pltpu.VMEM((POOL_HALO, A_WIDTH), F32)],
        args=(x, vecs, w_in_b, w_out_b, sgu_g, wm_b, bsb, wp_b, ps, pmats))


def _perm_mats(ts):
    p = jnp.arange(ts)
    pm = (((p % SUBLANES) * (ts // SUBLANES) + p // SUBLANES)[:, None] == p[None, :]).astype(BF16)
    return jnp.stack([pm, pm.T])


def _permute_bf16(pm, xb):
    return _dot(pm, xb).astype(BF16)


def _permute_f32(pm, x):
    hi = x.astype(BF16)
    lo = (x - hi.astype(F32)).astype(BF16)
    return _dot(pm, hi) + _dot(pm, lo)


def _conv_out(u, um2, um1, cv_ref, cols):
    return (cv_ref[3:4, cols] + um2 * cv_ref[0:1, cols] + um1 * cv_ref[1:2, cols] + u * cv_ref[2:3, cols])


F_LOSS, F_DGF, F_DPOSTFFN = 0, 1, 2
B_DSHF, B_DSCF, B_DPREFFN, B_DGM, B_DPOSTMIX = 0, 1, 2, 3, 4
M_DSHM, M_DSCM, M_DPREMIX = 0, 1, 2
C_DCB, C_DCW = 0, 1
G1_F, G1_B, G1_M = 0, 8, 16


def _ffn_fwd(h2p, x2, tgt, w_up_b, w_down_b, cvec, vecs, pmats, ts):
    s_len, d = x2.shape
    ff2 = w_up_b.shape[1]
    ff = ff2 // 2
    nt = s_len // ts
    chunks = _ff_chunks(ff)

    def body(h2_ref, x2_ref, t_ref, wu_ref, wd_ref, cv_ref, vec_ref, pm_ref,
             up_ref, y_ref, act_ref, dy_ref, df_ref, acc_ref, carry_ref):
        @pl.when(pl.program_id(0) == 0)
        def _():
            carry_ref[...] = jnp.zeros_like(carry_ref)
            acc_ref[...] = jnp.zeros_like(acc_ref)

        h2v = h2_ref[...]

        def up_dots(o, w):
            return [_dot(h2v, wu_ref[:, base + o:base + o + w]) for base in (0, ff)]

        f = None
        pending = None
        nxt = up_dots(*chunks[0])
        for ci, (o, w) in enumerate(chunks):
            us = nxt
            if ci + 1 < len(chunks):
                nxt = up_dots(*chunks[ci + 1])
            if pending is not None:
                part = _dot(pending[0], wd_ref[pending[1]:pending[1] + pending[2], :])
                f = part if f is None else f + part
            sub0 = lax.broadcasted_iota(jnp.int32, (SUBLANES, w), 0) == 0
            ys = []
            for base, u in zip((0, ff), us):
                cols = slice(base + o, base + o + w)
                up_ref[:, cols] = u.astype(BF16)
                last1, last2 = u[ts - SUBLANES:ts], u[ts - CONV_KEEP:ts - SUBLANES]
                b1 = jnp.where(sub0, pltpu.roll(carry_ref[SUBLANES:CONV_KEEP, cols], 1, 0), pltpu.roll(last1, 1, 0))
                b2 = jnp.where(sub0, pltpu.roll(carry_ref[0:SUBLANES, cols], 1, 0), pltpu.roll(last2, 1, 0))
                um1 = jnp.concatenate([b1, u[:ts - SUBLANES]], axis=0)
                um2 = jnp.concatenate([b2, b1, u[:ts - CONV_KEEP]], axis=0)
                ys.append(_conv_out(u, um2, um1, cv_ref, cols))
                carry_ref[:, cols] = u[ts - CONV_KEEP:ts]
            gate, val = ys
            sg = _sigmoid(gate)
            gs = gate * sg
            act = (gs * val).astype(BF16)
            act_ref[:, o:o + w] = act
            y_ref[:, o:o + w] = (val * (sg + gs * (1.0 - sg))).astype(BF16)
            y_ref[:, ff + o:ff + o + w] = gs.astype(BF16)
            pending = (act, o, w)
        f = f + _dot(pending[0], wd_ref[pending[1]:pending[1] + pending[2], :])
        f = _permute_f32(pm_ref[1], f)
        r3 = _rms(f)
        fhat = f * r3
        post = _vrow(vec_ref, V_POST_FFN)
        g_f = _vrow(vec_ref, V_G_F)
        fn = fhat * post
        e = (x2_ref[...] + g_f * fn) - t_ref[...]
        dy = e * (1.0 / d)
        dy_ref[...] = dy
        dfn = dy * g_f
        acc_ref[F_LOSS:F_LOSS + 1, :] += _colsum(e * e)
        acc_ref[F_DGF:F_DGF + 1, :] += _colsum(dy * fn)
        acc_ref[F_DPOSTFFN:F_DPOSTFFN + 1, :] += _colsum(dfn * fhat)
        dfhat = dfn * post
        df = (r3 * (dfhat - fhat * _rowmean(dfhat * fhat))).astype(BF16)
        df_ref[...] = _permute_bf16(pm_ref[0], df)

    row = lambda i: (i, 0)
    return pl.pallas_call(
        body, name="ffn_fwd", grid=(nt,),
        in_specs=[_tiled((ts, d), row), _tiled((ts, d), row), _tiled((ts, d), row), _resident(w_up_b.shape),
                  _resident(w_down_b.shape), _whole(cvec.shape), _whole(vecs.shape), _whole(pmats.shape)],
        out_specs=[_tiled((ts, ff2), row), _tiled((ts, ff2), row), _tiled((ts, ff), row), _tiled((ts, d), row),
                   _tiled((ts, d), row), _whole((8, d))],
        out_shape=[jax.ShapeDtypeStruct((s_len, ff2), BF16), jax.ShapeDtypeStruct((s_len, ff2), BF16),
                   jax.ShapeDtypeStruct((s_len, ff), BF16), jax.ShapeDtypeStruct((s_len, d), F32),
                   jax.ShapeDtypeStruct((s_len, d), BF16), jax.ShapeDtypeStruct((8, d), F32)],
        scratch_shapes=[pltpu.VMEM((CONV_KEEP, ff2), F32)],
        compiler_params=_seq_params(),
    )(h2p, x2, tgt, w_up_b, w_down_b, cvec, vecs, pmats)


def _ffn_bwd(dfp, upp, yp, x2, dy, mixed, w_up_b, w_down_b, cvec, vecs, pmats, ts, rider=None):
    s_len, d = x2.shape
    ff2 = w_up_b.shape[1]
    ff = ff2 // 2
    nt = s_len // ts
    chunks = _ff_chunks(ff, 512)

    def body(df_ref, up_ref, y_ref, x2_ref, dy_ref, mx_ref, wu_ref, wd_ref, cv_ref, vec_ref, pm_ref,
             dup_ref, dx2_ref, dmx_ref, accc_ref, acc_ref, carry_ref):
        @pl.when(pl.program_id(0) == 0)
        def _():
            carry_ref[...] = jnp.zeros_like(carry_ref)
            accc_ref[...] = jnp.zeros_like(accc_ref)
            acc_ref[...] = jnp.zeros_like(acc_ref)

        dfv = df_ref[...]

        def dh2_add(acc, dups, o, w):
            for base, dup in zip((0, ff), dups):
                part = _dot_nt(dup, wu_ref[:, base + o:base + o + w])
                acc = part if acc is None else acc + part
            return acc

        dh2 = None
        pending = None
        nxt = _dot_nt(dfv, wd_ref[chunks[0][0]:chunks[0][0] + chunks[0][1], :])
        for ci, (o, w) in enumerate(chunks):
            dact = nxt
            if ci + 1 < len(chunks):
                o2, w2 = chunks[ci + 1]
                nxt = _dot_nt(dfv, wd_ref[o2:o2 + w2, :])
            if pending is not None:
                dh2 = dh2_add(dh2, *pending)
            sub7 = lax.broadcasted_iota(jnp.int32, (SUBLANES, w), 0) == SUBLANES - 1
            dups = []
            dys = (dact * y_ref[:, o:o + w].astype(F32), dact * y_ref[:, ff + o:ff + o + w].astype(F32))
            for base, dyv in zip((0, ff), dys):
                cols = slice(base + o, base + o + w)
                u = up_ref[:, cols].astype(F32)
                up1 = SUBLANES - 1
                e0 = jnp.where(sub7, pltpu.roll(carry_ref[0:SUBLANES, cols], up1, 0),
                               pltpu.roll(dyv[0:SUBLANES], up1, 0))
                e1 = jnp.where(sub7, pltpu.roll(carry_ref[SUBLANES:CONV_KEEP, cols], up1, 0),
                               pltpu.roll(dyv[SUBLANES:CONV_KEEP], up1, 0))
                dyp1 = jnp.concatenate([dyv[SUBLANES:], e0], axis=0)
                dyp2 = jnp.concatenate([dyv[CONV_KEEP:], e0, e1], axis=0)
                accc_ref[C_DCB:C_DCB + 1, cols] += _colsum(dyv)
                accc_ref[C_DCW + 0:C_DCW + 1, cols] += _colsum(dyp2 * u)
                accc_ref[C_DCW + 1:C_DCW + 2, cols] += _colsum(dyp1 * u)
                accc_ref[C_DCW + 2:C_DCW + 3, cols] += _colsum(dyv * u)
                dup = (dyv * cv_ref[2:3, cols] + dyp1 * cv_ref[1:2, cols] + dyp2 * cv_ref[0:1, cols]).astype(BF16)
                dup_ref[:, cols] = dup
                dups.append(dup)
                carry_ref[:, cols] = dyv[0:CONV_KEEP]
            pending = (dups, o, w)
        dh2 = dh2_add(dh2, *pending)
        dh2 = _permute_f32(pm_ref[1], dh2)
        x2 = x2_ref[...]
        r2 = _rms(x2)
        xn = x2 * r2
        pre = _vrow(vec_ref, V_PRE_FFN)
        one_sc = 1.0 + _vrow(vec_ref, V_SC_F)
        acc_ref[B_DSHF:B_DSHF + 1, :] += _colsum(dh2)
        acc_ref[B_DSCF:B_DSCF + 1, :] += _colsum(dh2 * (xn * pre))
        acc_ref[B_DPREFFN:B_DPREFFN + 1, :] += _colsum(dh2 * xn * one_sc)
        dxn = dh2 * pre * one_sc
        dx2 = dy_ref[...] + r2 * (dxn - xn * _rowmean(dxn * xn))
        dx2_ref[...] = dx2
        mixed = mx_ref[...]
        rm = _rms(mixed)
        mhat = mixed * rm
        post = _vrow(vec_ref, V_POST_MIX)
        acc_ref[B_DGM:B_DGM + 1, :] += _colsum(dx2 * (mhat * post))
        dmn = dx2 * _vrow(vec_ref, V_G_M)
        acc_ref[B_DPOSTMIX:B_DPOSTMIX + 1, :] += _colsum(dmn * mhat)
        dmhat = dmn * post
        dmx_ref[...] = (rm * (dmhat - mhat * _rowmean(dmhat * mhat))).astype(BF16)

    rev = lambda i: (nt - 1 - i, 0)
    return _call(
        body, name="ffn_bwd", grid=(nt,), rider=rider,
        in_specs=[_tiled((ts, d), rev), _tiled((ts, ff2), rev), _tiled((ts, ff2), rev), _tiled((ts, d), rev),
                  _tiled((ts, d), rev), _tiled((ts, d), rev), _resident(w_up_b.shape), _resident(w_down_b.shape),
                  _whole(cvec.shape), _whole(vecs.shape), _whole(pmats.shape)],
        out_specs=[_tiled((ts, ff2), rev), _tiled((ts, d), rev), _tiled((ts, d), rev), _whole((8, ff2)),
                   _whole((8, d))],
        out_shape=[jax.ShapeDtypeStruct((s_len, ff2), BF16), jax.ShapeDtypeStruct((s_len, d), F32),
                   jax.ShapeDtypeStruct((s_len, d), BF16), jax.ShapeDtypeStruct((8, ff2), F32),
                   jax.ShapeDtypeStruct((8, d), F32)],
        scratch_shapes=[pltpu.VMEM((CONV_KEEP, ff2), F32)],
        args=(dfp, upp, yp, x2, dy, mixed, w_up_b, w_down_b, cvec, vecs, pmats))


def _mix_bwd(dmixed, dx2, x, proj, vecs, w_in_b, w_out_b, sgu_g, wm_b, wmt_b, bsb, wp_b, ps, ts):
    s_len, d = x.shape
    nt = s_len // ts
    n_proj = proj.shape[1]
    per = ts // POOL_HALO

    def body(dmx_ref, dx2_ref, x_ref, proj_ref, projh_ref, vec_ref, win_ref, wout_ref, sg_ref, wm_ref, wmt_ref,
             bs_ref, wp_ref, ps_ref,
             gx_ref, dproj_ref, acc_ref, dwm_out, dwp_out, db_ref, dg_ref, dps_ref,
             carry_ref, dwm_ref, dwp_ref, dbz_ref):
        i = pl.program_id(0)
        tile = nt - 1 - i

        @pl.when(i == 0)
        def _():
            carry_ref[...] = jnp.zeros_like(carry_ref)
            for r in (acc_ref, dwm_ref, dwp_ref, dbz_ref, dg_ref, dps_ref):
                r[...] = jnp.zeros_like(r)

        sub = FFN_TS
        nsub = ts // sub
        ext = sub + POOL_HALO

        def cotangent(s):
            return _dot_nt(dmx_ref[s * sub:(s + 1) * sub, :], wout_ref[...])

        def back(s, dcat, halo, later):
            r0 = s * sub
            rows = slice(r0, r0 + sub)
            t_glob = lax.broadcasted_iota(jnp.int32, (sub, HEAD), 0) + (tile * ts + r0)
            for h in range(N_HEADS):
                hs = slice(h * HEAD, (h + 1) * HEAD)
                vs = slice(A_WIDTH + h * HEAD, A_WIDTH + (h + 1) * HEAD)
                gain = sg_ref[h:h + 1, :]
                for b in range(sub // HEAD):
                    blk = slice(r0 + b * HEAD, r0 + (b + 1) * HEAD)
                    au = proj_ref[blk, hs].astype(F32)
                    av = proj_ref[blk, vs].astype(F32)
                    u, u_grad = _gelu_and_grad(au)
                    v, v_grad = _gelu_and_grad(av)
                    rv = _rms(v)
                    vhat = v * rv
                    vn = (vhat * gain).astype(BF16)
                    dout = dcat[b * HEAD:(b + 1) * HEAD, hs]
                    z = _dot(wm_ref[h], vn) + bs_ref[h]
                    dz = dout * u
                    dbz_ref[h] += dz
                    dzb = dz.astype(BF16)
                    dwm_ref[h] += _dot_nt(dzb, vn)
                    dvn = _dot(wmt_ref[h], dzb)
                    dg_ref[h:h + 1, :] += _colsum(dvn * vhat)
                    dvhat = dvn * gain
                    dv = rv * (dvhat - vhat * _rowmean(dvhat * vhat))
                    dproj_ref[blk, hs] = ((dout * z) * u_grad).astype(BF16)
                    dproj_ref[blk, vs] = (dv * v_grad).astype(BF16)
            firsts = []
            for g in range(len(POOL_WINDOWS)):
                gs = slice(g * HEAD, (g + 1) * HEAD)
                pcols = slice(2 * A_WIDTH + g * HEAD, 2 * A_WIDTH + (g + 1) * HEAD)
                p = proj_ref[rows, pcols].astype(F32)
                pb = _pool_fwd(p, halo[:, gs], g, t_glob).astype(BF16)
                dyb = dcat[:, A_WIDTH + g * HEAD:A_WIDTH + (g + 1) * HEAD]
                dps_ref[0:1, gs] += _colsum(dyb * _dot(pb, wp_ref[g]))
                dyl = (dyb * ps_ref[0:1, gs]).astype(BF16)
                dwp_ref[g] += _dot_tn(pb, dyl)
                dpooled = _dot_nt(dyl, wp_ref[g])
                cnt = jnp.minimum(t_glob + 1, POOL_WINDOWS[g]).astype(F32)
                q = dpooled / cnt
                acc = jnp.concatenate([q, later[:, gs]], axis=0)
                for step in range(g + 1):
                    acc = acc + pltpu.roll(acc, ext - (1 << step), 0)
                dproj_ref[rows, pcols] = (acc[:sub] - dpooled).astype(BF16)
                firsts.append(q[0:POOL_HALO])
            return jnp.concatenate(firsts, axis=1)

        def finish(s, dh1):
            rows = slice(s * sub, (s + 1) * sub)
            x = x_ref[rows, :]
            r1 = _rms(x)
            xn = x * r1
            pre = _vrow(vec_ref, V_PRE_MIX)
            one_sc = 1.0 + _vrow(vec_ref, V_SC_M)
            acc_ref[M_DSHM:M_DSHM + 1, :] += _colsum(dh1)
            acc_ref[M_DSCM:M_DSCM + 1, :] += _colsum(dh1 * (xn * pre))
            acc_ref[M_DPREMIX:M_DPREMIX + 1, :] += _colsum(dh1 * xn * one_sc)
            dxn = dh1 * pre * one_sc
            gx_ref[rows, :] = dx2_ref[rows, :] + r1 * (dxn - xn * _rowmean(dxn * xn))

        nxt = cotangent(nsub - 1)
        later = carry_ref[...]
        dh1_prev = None
        for s in reversed(range(nsub)):
            dcat = nxt
            if s > 0:
                nxt = cotangent(s - 1)
                halo = proj_ref[s * sub - POOL_HALO:s * sub, 2 * A_WIDTH:].astype(F32)
            else:
                halo = jnp.where(tile > 0, projh_ref[:, 2 * A_WIDTH:].astype(F32), 0.0)
            later = back(s, dcat, halo, later)
            dh1 = _dot_nt(dproj_ref[s * sub:(s + 1) * sub, :], win_ref[...])
            if dh1_prev is not None:
                finish(s + 1, dh1_prev)
            dh1_prev = dh1
        carry_ref[...] = later
        finish(0, dh1_prev)

        @pl.when(i == nt - 1)
        def _():
            dwm_out[...] = dwm_ref[...].astype(BF16)
            dwp_out[...] = dwp_ref[...].astype(BF16)
            db_ref[...] = jnp.zeros_like(db_ref)
            for h in range(N_HEADS):
                db_ref[h:h + 1, :] = jnp.sum(dbz_ref[h].T, axis=0, keepdims=True)

    rev = lambda i: (nt - 1 - i, 0)
    halo_map = lambda i: (jnp.maximum((nt - 1 - i) * per - 1, 0), 0)
    hshape = (N_HEADS, HEAD, HEAD)
    return _call(
        body, name="mix_bwd", grid=(nt,),
        in_specs=[_tiled((ts, d), rev), _tiled((ts, d), rev), _tiled((ts, d), rev), _tiled((ts, n_proj), rev),
                  _tiled((POOL_HALO, n_proj), halo_map), _whole(vecs.shape), _resident(w_in_b.shape),
                  _resident(w_out_b.shape), _whole(sgu_g.shape), _whole(wm_b.shape), _whole(wmt_b.shape),
                  _whole(bsb.shape), _whole(wp_b.shape), _whole(ps.shape)],
        out_specs=[_tiled((ts, d), rev), _tiled((ts, n_proj), rev), _whole((8, d)), _whole(hshape), _whole(hshape),
                   _whole((8, HEAD)), _whole((8, HEAD)), _whole((8, A_WIDTH))],
        out_shape=[jax.ShapeDtypeStruct((s_len, d), F32), jax.ShapeDtypeStruct((s_len, n_proj), BF16),
                   jax.ShapeDtypeStruct((8, d), F32), jax.ShapeDtypeStruct(hshape, BF16),
                   jax.ShapeDtypeStruct(hshape, BF16), jax.ShapeDtypeStruct((8, HEAD), F32),
                   jax.ShapeDtypeStruct((8, HEAD), F32), jax.ShapeDtypeStruct((8, A_WIDTH), F32)],
        scratch_shapes=[pltpu.VMEM((POOL_HALO, A_WIDTH), F32), pltpu.VMEM(hshape, F32), pltpu.VMEM(hshape, F32),
                        pltpu.VMEM(hshape, F32)],
        args=(dmixed, dx2, x, proj, proj, vecs, w_in_b, w_out_b, sgu_g, wm_b, wmt_b, bsb, wp_b, ps))


def _wgrad(a, b, tn, ts, name, rider=None):
    s_len, m = a.shape
    n = b.shape[1]
    ts = min(ts, s_len)

    def body(a_ref, b_ref, o_ref):
        @pl.when(pl.program_id(1) == 0)
        def _():
            o_ref[...] = jnp.zeros_like(o_ref)

        o_ref[...] += _dot_tn(a_ref[...], b_ref[...])

    (g,), r_out = _call(
        body, name=name, grid=(n // tn, s_len // ts), rider=rider,
        in_specs=[pl.BlockSpec((ts, m), lambda j, s: (s, 0)), pl.BlockSpec((ts, tn), lambda j, s: (s, j))],
        out_specs=[pl.BlockSpec((m, tn), lambda j, s: (0, j))],
        out_shape=[jax.ShapeDtypeStruct((m, n), F32)], scratch_shapes=[], args=(a, b))
    return g, r_out


def _wgrad_pair(a1, b1, a2, b2, after, ts, name):
    s_len = a1.shape[0]
    ts = min(ts, s_len)
    shapes = [(a1.shape[1], b1.shape[1]), (a2.shape[1], b2.shape[1])]

    def body(a1_ref, b1_ref, a2_ref, b2_ref, after_ref, o1_ref, o2_ref):
        @pl.when(pl.program_id(0) == 0)
        def _():
            o1_ref[...] = jnp.zeros_like(o1_ref)
            o2_ref[...] = jnp.zeros_like(o2_ref)

        o1_ref[...] += _dot_tn(a1_ref[...], b1_ref[...])
        o2_ref[...] += _dot_tn(a2_ref[...], b2_ref[...])

    row = lambda s: (s, 0)
    return _call(
        body, name=name, grid=(s_len // ts,),
        in_specs=[pl.BlockSpec((ts, t.shape[1]), row) for t in (a1, b1, a2, b2)] + [_whole(after.shape)],
        out_specs=[_whole(sh) for sh in shapes],
        out_shape=[jax.ShapeDtypeStruct(sh, F32) for sh in shapes], scratch_shapes=[],
        args=(a1, b1, a2, b2, after))[0]


def _adamw_big(g, w, m, v, name):
    r, cdim = g.shape
    tr = r
    while tr * cdim * 4 > (3 << 19) and tr % 16 == 0:
        tr //= 2

    def body(g_ref, w_ref, m_ref, v_ref, go_ref, d_ref, nm_ref, nv_ref):
        grad = g_ref[...]
        delta, m2, v2 = _adamw_math(w_ref[0], grad, m_ref[0], v_ref[0])
        go_ref[0] = grad
        d_ref[0] = delta
        nm_ref[0] = m2
        nv_ref[0] = v2

    s3 = pl.BlockSpec((1, tr, cdim), lambda i: (0, i, 0))
    return pl.pallas_call(
        body, name=name, grid=(r // tr,),
        in_specs=[pl.BlockSpec((tr, cdim), lambda i: (i, 0)), s3, s3, s3],
        out_specs=[s3, s3, s3, s3],
        out_shape=[jax.ShapeDtypeStruct(w.shape, F32)] * 4,
        compiler_params=pltpu.CompilerParams(dimension_semantics=("parallel",), vmem_limit_bytes=VMEM_LIMIT_BYTES),
    )(g, w, m, v)


def _wada_update(sct, gm, w, m, v):
    _, r, cdim = w.shape
    tr = 256
    kp = sct.shape[1]

    def body(s_ref, g_ref, w_ref, m_ref, v_ref, gw_ref, d_ref, nm_ref, nv_ref):
        g = _dot(s_ref[...], g_ref[...])
        gw_ref[0] = g
        delta, m2, v2 = _adamw_math(w_ref[0], g, m_ref[0], v_ref[0])
        d_ref[0] = delta
        nm_ref[0] = m2
        nv_ref[0] = v2

    s3 = pl.BlockSpec((1, tr, cdim), lambda i: (0, i, 0))
    return pl.pallas_call(
        body, name="wada_update", grid=(r // tr,),
        in_specs=[pl.BlockSpec((tr, kp), lambda i: (i, 0)), _whole(gm.shape), s3, s3, s3],
        out_specs=[s3, s3, s3, s3],
        out_shape=[jax.ShapeDtypeStruct(w.shape, F32)] * 4,
        compiler_params=pltpu.CompilerParams(dimension_semantics=("parallel",), vmem_limit_bytes=VMEM_LIMIT_BYTES),
    )(sct, gm, w, m, v)


def _small_update(g1, g2, g2s, gwm, gwp, gbz, gsg, gps, params):
    names = ["b_ada", "pre_mix_g", "post_mix_g", "sgu_norm_g", "w_spatial", "b_spatial", "w_pool", "pool_scale",
             "pre_ffn_g", "post_ffn_g", "conv_w", "conv_b"]
    d = g1.shape[2]
    flat_in = [g1, g2, g2s, gwm, gwp, gbz, gsg, gps]
    n_g = len(flat_in)
    for nm in names:
        flat_in += list(params[nm])

    def body(*refs):
        g1_ref, g2_ref, g2s_ref, gwm_ref, gwp_ref, gbz_ref, gsg_ref, gps_ref = refs[:n_g]
        wmv = refs[n_g:n_g + 3 * len(names)]
        loss_ref = refs[n_g + 3 * len(names)]
        outs = refs[n_g + 3 * len(names) + 1:]

        def dsum(ref, idx):
            acc = ref[(0,) + idx].astype(F32)
            for dev in range(1, N_DEV):
                acc = acc + ref[(dev,) + idx].astype(F32)
            return acc

        def apply(pi, g, widx, oidx):
            w_ref, m_ref, v_ref = wmv[3 * pi:3 * pi + 3]
            g_ref, d_ref, nm_ref, nv_ref = outs[4 * pi:4 * pi + 4]
            delta, m2, v2 = _adamw_math(w_ref[widx], g, m_ref[widx], v_ref[widx])
            g_ref[oidx] = g
            d_ref[oidx] = delta
            nm_ref[oidx] = m2
            nv_ref[oidx] = v2

        def row1(base, r):
            return (slice(base + r, base + r + 1), slice(None))

        tot = dsum(g1_ref, row1(G1_F, F_LOSS))
        loss_ref[...] = jnp.zeros(loss_ref.shape, F32) + jnp.sum(tot) * (0.5 / d)
        mod_rows = [row1(G1_M, M_DSHM), row1(G1_M, M_DSCM), row1(G1_B, B_DGM), row1(G1_B, B_DSHF),
                    row1(G1_B, B_DSCF), row1(G1_F, F_DGF)]
        for j, rr in enumerate(mod_rows):
            cs = (slice(None), slice(j * d, (j + 1) * d))
            apply(0, dsum(g1_ref, rr), cs, cs)
        full = (slice(None), slice(None))
        apply(1, dsum(g1_ref, row1(G1_M, M_DPREMIX)), full, full)
        apply(2, dsum(g1_ref, row1(G1_B, B_DPOSTMIX)), full, full)
        apply(3, dsum(gsg_ref, (slice(0, N_HEADS), slice(None))), (0,), (0,))
        pos_i = lax.broadcasted_iota(jnp.int32, (HEAD, HEAD), 0)
        pos_j = lax.broadcasted_iota(jnp.int32, (HEAD, HEAD), 1)
        causal = (pos_j // CHUNK) <= (pos_i // CHUNK)
        for h in range(N_HEADS):
            blk = (slice(h * HEAD, (h + 1) * HEAD), slice(None))
            apply(4, jnp.where(causal, dsum(gwm_ref, blk), 0.0), (0, h), (0, h))
            apply(5, dsum(gbz_ref, (slice(h, h + 1), slice(None))), (0, slice(h, h + 1)), (0, slice(h, h + 1)))
            apply(6, dsum(gwp_ref, blk), (0, h), (0, h))
        apply(7, dsum(gps_ref, (slice(0, 1), slice(None))), full, full)
        apply(8, dsum(g1_ref, row1(G1_B, B_DPREFFN)), full, full)
        apply(9, dsum(g1_ref, row1(G1_F, F_DPOSTFFN)), full, full)
        apply(10, dsum(g2s_ref, (slice(C_DCW, C_DCW + 3), slice(None))), (0,), (0,))
        apply(11, dsum(g2_ref, (slice(C_DCB, C_DCB + 1), slice(None))), full, full)

    out_shape = [jax.ShapeDtypeStruct((8, HEAD), F32)]
    for nm in names:
        out_shape += [jax.ShapeDtypeStruct(params[nm][0].shape, F32)] * 4
    res = pl.pallas_call(
        body, name="small_update", out_shape=out_shape,
        compiler_params=pltpu.CompilerParams(vmem_limit_bytes=VMEM_LIMIT_BYTES),
    )(*flat_in)
    out = {nm: tuple(res[1 + 4 * i:5 + 4 * i]) for i, nm in enumerate(names)}
    return res[0], out


def kernel(x, c, w_ada, b_ada, pre_mix_g, post_mix_g, w_in, sgu_norm_g, w_spatial, b_spatial, w_pool, pool_scale, w_out, pre_ffn_g, post_ffn_g, w_up, conv_w, conv_b, w_down, loss_target, m_w_ada, m_b_ada, m_pre_mix_g, m_post_mix_g, m_w_in, m_sgu_norm_g, m_w_spatial, m_b_spatial, m_w_pool, m_pool_scale, m_w_out, m_pre_ffn_g, m_post_ffn_g, m_w_up, m_conv_w, m_conv_b, m_w_down, v_w_ada, v_b_ada, v_pre_mix_g, v_post_mix_g, v_w_in, v_sgu_norm_g, v_w_spatial, v_b_spatial, v_w_pool, v_pool_scale, v_w_out, v_pre_ffn_g, v_post_ffn_g, v_w_up, v_conv_w, v_conv_b, v_w_down):
    xi, yi, ci = _mesh_pos()
    k_me = 2 * xi + yi
    dev = 2 * k_me + ci
    s_len, d = x.shape[1], x.shape[2]
    x2d = x[0]
    tgt = loss_target[0]
    ff2 = conv_b.shape[1]
    n_ada = w_ada.shape[2]
    n_cw = conv_w.shape[2]

    k_idx = k_me.reshape(1).astype(jnp.int32)
    mix_flags = (True, False)
    (w_in_s, w_out_s), (w_in_f, w_out_f) = _cast_bf16([w_in[0], w_out[0]], mix_flags, k_idx, "cast_weights_mix")

    def place(t):
        return lax.dynamic_update_slice(jnp.zeros((N_DEV * t.shape[0], t.shape[1]), t.dtype), t,
                                        (dev * t.shape[0], 0))

    cw_blk = jnp.concatenate([conv_w[0], jnp.zeros((5, n_cw), F32)], axis=0)
    fly_c = _split_begin([place(c.reshape(8, d // 8)), place(cw_blk)], 6, _small_ici_copies, k_idx,
                         "c_gather_begin")
    fly_mix = _gather_begin([w_in_s, w_out_s], [w_in_f, w_out_f], mix_flags, fly_c[3], "gather_begin_mix")
    c_all, cw_all = _small_finish(_split_end(fly_c, _small_ici_copies, fly_mix[3], "c_gather_end"),
                                  "c_gather_finish")
    (w_up_s, w_down_s), (w_up_f, w_down_f) = _cast_bf16([w_up[0], w_down[0]], mix_flags, k_idx, "cast_weights_ffn",
                                                        after=fly_mix[3])
    c_all = c_all.reshape(N_DEV, 8, d // 8).reshape(N_DEV, d)
    cw_full = jnp.concatenate([cw_all[16 * k:16 * k + 8] for k in range(N_CHIP)], axis=1)
    cvec = jnp.concatenate([cw_full[0:3], conv_b, jnp.zeros((4, ff2), F32)], axis=0)
    b_shard = lax.dynamic_slice_in_dim(b_ada, k_me * n_ada, n_ada, axis=1)
    mod_k, sc_all = _mod_shard(c_all, w_ada[0], b_shard, w_down_s)
    (mod_g,) = _run_rider(_allgather_rider([mod_k]), "gather_mod")
    mod_all = jnp.concatenate([mod_g[16 * k:16 * k + 8] for k in range(N_CHIP)], axis=1)
    mod_me = lax.dynamic_slice_in_dim(mod_all, dev, 1, axis=0).reshape(6, d)
    vecs = jnp.concatenate([mod_me, pre_mix_g, post_mix_g, pre_ffn_g, post_ffn_g, jnp.zeros((6, d), F32)], axis=0)

    fly_ffn = _gather_begin([w_up_s, w_down_s], [w_up_f, w_down_f], mix_flags, mod_g, "gather_begin_ffn")
    w_in_b, w_out_b = _gather_finish(_gather_end(fly_mix, mix_flags, fly_ffn[3], "gather_end_mix"), mix_flags,
                                     "gather_finish_mix")

    pos = jnp.arange(HEAD)
    causal = (pos[None, :] // CHUNK) <= (pos[:, None] // CHUNK)
    wm = jnp.where(causal[None], w_spatial[0], 0.0)
    wm_b = wm.astype(BF16)
    wmt_b = jnp.swapaxes(wm, 1, 2).astype(BF16)
    bsb = jnp.broadcast_to(b_spatial[0][:, :, None], (N_HEADS, HEAD, HEAD))
    wp_b = w_pool[0].astype(BF16)
    sgu_g = jnp.concatenate([sgu_norm_g[0], jnp.zeros((4, HEAD), F32)], axis=0)
    ps = jnp.concatenate([pool_scale, jnp.zeros((7, A_WIDTH), F32)], axis=0)

    pmats = _perm_mats(FFN_TS)
    h1, proj, cat, mixed, x2, h2p = _mix_fwd(x2d, vecs, w_in_b, w_out_b, sgu_g, wm_b, bsb, wp_b, ps, pmats,
                                             ts=MIX_TS)[0]
    w_up_b, w_down_b = _gather_finish(_gather_end(fly_ffn, mix_flags, h2p, "gather_end_ffn"), mix_flags,
                                      "gather_finish_ffn")
    up, yv, act, dy, dfp, acc_f = _ffn_fwd(h2p, x2, tgt, w_up_b, w_down_b, cvec, vecs, pmats, ts=FFN_TS)

    c_idx = ci.reshape(1).astype(jnp.int32)
    g_w_down, _ = _wgrad(act, dfp, d, WGRAD_TS_WIDE, "wgrad_down")
    (dup, dx2, dmixed, acc_c, acc_b), (land_down,) = _ffn_bwd(
        dfp, up, yv, x2, dy, mixed, w_up_b, w_down_b, cvec, vecs, pmats, ts=FFN_TS,
        rider=_sibling_rider([g_w_down], (False,)))
    (part_down,) = _sum_with_sibling([g_w_down], [land_down], (False,), c_idx, "pair_sum_down")
    g_w_up, (chips_down,) = _wgrad(h2p, dup, ff2 // 2, WGRAD_TS, "wgrad_up", rider=_chips_rider([part_down]))
    fly_up = _sibling_begin(g_w_up, True, chips_down, "sibling_begin_up")
    gx, dproj, acc_m, dwm, dwp, dbz, dsg, dps = _mix_bwd(
        dmixed, dx2, x2d, proj, vecs + fly_up[3][0:1, 0:1], w_in_b, w_out_b, sgu_g, wm_b, wmt_b, bsb, wp_b, ps,
        ts=MIX_TS)[0]
    g_w_up, land_up = _sibling_end(fly_up, True, dproj, "sibling_end_up")
    (part_up,) = _sum_with_sibling([g_w_up], [land_up], (True,), c_idx, "pair_sum_up")
    fly_chips_up = _chips_begin([part_up], c_idx, "chips_begin_up")
    g1 = jnp.concatenate([acc_f, acc_b, acc_m], axis=0)
    hflat = (N_HEADS * HEAD, HEAD)
    small_bufs = [place(t) for t in (g1, acc_c, dwm.reshape(hflat), dwp.reshape(hflat), dbz, dsg, dps)]
    fly_small = _split_begin(small_bufs, 3 * len(small_bufs), _small_ici_copies, fly_chips_up[3],
                             "small_gather_begin")
    g_w_out, g_w_in = _wgrad_pair(cat, dmixed, h1, dproj, fly_small[3], WGRAD_TS_WIDE, "wgrad_mix")
    land_mix = _run_rider(_sibling_rider([g_w_in, g_w_out], (True, False)), "reduce_to_sibling")
    parts_mix = _sum_with_sibling([g_w_in, g_w_out], land_mix, (True, False), c_idx, "pair_sum_mix")
    (part_up,), (chips_up,) = _chips_end(fly_chips_up, parts_mix[0], "chips_end_up")
    fly_chips_mix = _chips_begin(parts_mix, chips_up, "chips_begin_mix")

    def adamw_of(names, reduced):
        res = {}
        for nm, red in zip(names, reduced):
            w, m, v = big_wmv[nm]
            g = red.reshape(w.shape[1], w.shape[2])
            res[nm] = tuple(_adamw_big(g, w, m, v, "adamw_" + nm))
        return res

    big_wmv = {"w_in": (w_in, m_w_in, v_w_in), "w_out": (w_out, m_w_out, v_w_out),
               "w_up": (w_up, m_w_up, v_w_up), "w_down": (w_down, m_w_down, v_w_down)}
    big = adamw_of(("w_up", "w_down"), _sum_chips_and_share([chips_up, chips_down], [part_up, part_down],
                                                           fly_chips_mix[3], "sum_share_ffn"))

    gathered = _small_finish(_split_end(fly_small, _small_ici_copies, big["w_down"][1], "small_gather_end"),
                             "small_gather_finish")
    g1a, g2a, gwm, gwp, gbz, gsg, gps = [t.reshape((N_DEV, t.shape[0] // N_DEV, t.shape[1])) for t in gathered]
    g2s = lax.dynamic_slice_in_dim(g2a, k_me * n_cw, n_cw, axis=2)
    params = {
        "b_ada": (b_ada, m_b_ada, v_b_ada), "pre_mix_g": (pre_mix_g, m_pre_mix_g, v_pre_mix_g),
        "post_mix_g": (post_mix_g, m_post_mix_g, v_post_mix_g),
        "sgu_norm_g": (sgu_norm_g, m_sgu_norm_g, v_sgu_norm_g), "w_spatial": (w_spatial, m_w_spatial, v_w_spatial),
        "b_spatial": (b_spatial, m_b_spatial, v_b_spatial), "w_pool": (w_pool, m_w_pool, v_w_pool),
        "pool_scale": (pool_scale, m_pool_scale, v_pool_scale), "pre_ffn_g": (pre_ffn_g, m_pre_ffn_g, v_pre_ffn_g),
        "post_ffn_g": (post_ffn_g, m_post_ffn_g, v_post_ffn_g), "conv_w": (conv_w, m_conv_w, v_conv_w),
        "conv_b": (conv_b, m_conv_b, v_conv_b),
    }
    loss_slab, small = _small_update(g1a, g2a, g2s, gwm, gwp, gbz, gsg, gps, params)

    gmod_all = jnp.concatenate(
        [g1a[:, G1_M + M_DSHM], g1a[:, G1_M + M_DSCM], g1a[:, G1_B + B_DGM], g1a[:, G1_B + B_DSHF],
         g1a[:, G1_B + B_DSCF], g1a[:, G1_F + F_DGF]], axis=1)
    gm = lax.dynamic_slice_in_dim(gmod_all, k_me * n_ada, n_ada, axis=1)
    gm = jnp.concatenate([gm, jnp.zeros((HEAD - N_DEV, n_ada), F32)], axis=0)
    sct = jnp.concatenate([sc_all.T, jnp.zeros((d, HEAD - N_DEV), F32)], axis=1)
    ada = tuple(_wada_update(sct, gm, w_ada, m_w_ada, v_w_ada))

    parts_mix, chips_mix = _chips_end(fly_chips_mix, ada[1], "chips_end_mix")
    big.update(adamw_of(("w_in", "w_out"), _sum_chips_and_share(chips_mix, parts_mix, loss_slab, "sum_share_mix")))

    everything = dict(small)
    everything.update(big)
    everything["w_ada"] = ada
    order = ["w_ada", "b_ada", "pre_mix_g", "post_mix_g", "w_in", "sgu_norm_g", "w_spatial", "b_spatial", "w_pool",
             "pool_scale", "w_out", "pre_ffn_g", "post_ffn_g", "w_up", "conv_w", "conv_b", "w_down"]
    outs = [loss_slab[0, 0], gx.reshape(x.shape)]
    for j in range(4):
        outs += [everything[nm][j] for nm in order]
    return tuple(outs)
```

```python
import functools

import jax
import jax.numpy as jnp
from jax import lax
from jax.experimental import pallas as pl
from jax.experimental.pallas import tpu as pltpu

F32 = jnp.float32
BF16 = jnp.bfloat16
MESH = pl.DeviceIdType.MESH

EPS = 1e-6
HEAD = 128
N_HEADS = 4
A_WIDTH = N_HEADS * HEAD
CHUNK = 64
POOL_WINDOWS = (2, 4, 8, 16)
POOL_HALO = 16
SUBLANES = 8
CONV_KEEP = 2 * SUBLANES
FFN_TS = 256
MIX_TS = 512
WGRAD_TS = 2048
WGRAD_TS_WIDE = 1024

ADAM_LR = 0.001
ADAM_B1 = 0.9
ADAM_B2 = 0.999
ADAM_EPS = 1e-08
ADAM_WD = 0.01
ADAM_STEP = 10

VMEM_LIMIT_BYTES = 58 * 1024 * 1024
N_DEV = 8
N_CHIP = 4


def _dot(a, b):
    return jnp.dot(a, b, preferred_element_type=F32)


def _dot_nt(a, b):
    return lax.dot_general(a, b, (((1,), (1,)), ((), ())), preferred_element_type=F32)


def _dot_tn(a, b):
    return lax.dot_general(a, b, (((0,), (0,)), ((), ())), preferred_element_type=F32)


GELU_C0 = 0.7978845608028654
GELU_C1 = GELU_C0 * 0.044715


def _gelu(x):
    return x * (0.5 + 0.5 * jnp.tanh(x * (GELU_C0 + GELU_C1 * (x * x))))


def _gelu_and_grad(x):
    x2 = x * x
    t = jnp.tanh(x * (GELU_C0 + GELU_C1 * x2))
    half = 0.5 + 0.5 * t
    grad = half + (x * (0.5 - 0.5 * (t * t))) * (GELU_C0 + (3.0 * GELU_C1) * x2)
    return x * half, grad


def _sigmoid(x):
    return 1.0 / (1.0 + jnp.exp(-x))


def _rms(x):
    return lax.rsqrt(jnp.mean(x * x, axis=-1, keepdims=True) + EPS)


def _colsum(x):
    return jnp.sum(x, axis=0, keepdims=True)


def _rowmean(x):
    return jnp.mean(x, axis=-1, keepdims=True)


def _tiled(shape, index_map):
    return pl.BlockSpec(shape, index_map)


def _resident(shape):
    nd = len(shape)
    return pl.BlockSpec(shape, lambda *_: (0,) * nd, pipeline_mode=pl.Buffered(1))


def _whole(shape):
    nd = len(shape)
    return pl.BlockSpec(shape, lambda *_: (0,) * nd)


def _seq_params():
    return pltpu.CompilerParams(dimension_semantics=("arbitrary",), vmem_limit_bytes=VMEM_LIMIT_BYTES)


def _ff_chunks(f, width=768):
    out, o = [], 0
    while o < f:
        w = min(width, f - o)
        out.append((o, w))
        o += w
    return out


def _pool_fwd(p, halo, g, t_glob):
    ext = jnp.concatenate([halo, p], axis=0)
    s = ext
    for step in range(g + 1):
        s = s + pltpu.roll(s, 1 << step, 0)
    cnt = jnp.minimum(t_glob + 1, POOL_WINDOWS[g]).astype(F32)
    return s[POOL_HALO:] / cnt - p


def _adamw_math(w, g, m, v):
    m = ADAM_B1 * m + (1.0 - ADAM_B1) * g
    v = ADAM_B2 * v + (1.0 - ADAM_B2) * (g * g)
    m_hat = m / (1.0 - ADAM_B1 ** ADAM_STEP)
    v_hat = v / (1.0 - ADAM_B2 ** ADAM_STEP)
    delta = -ADAM_LR * (m_hat / (jnp.sqrt(v_hat) + ADAM_EPS) + ADAM_WD * w)
    return delta, m, v


def _mesh_pos():
    return lax.axis_index("x"), lax.axis_index("y"), lax.axis_index("c")


class _Rider:
    def __init__(self, inputs, out_shape, sems, start, finish):
        self.inputs, self.out_shape, self.sems = list(inputs), list(out_shape), list(sems)
        self.start, self.finish = start, finish


def _call(body, *, name, grid, in_specs, out_specs, out_shape, scratch_shapes, args, rider=None):
    params = pltpu.CompilerParams(dimension_semantics=("arbitrary",) * len(grid), vmem_limit_bytes=VMEM_LIMIT_BYTES)
    if rider is None:
        res = pl.pallas_call(body, name=name, grid=grid, in_specs=in_specs, out_specs=out_specs, out_shape=out_shape,
                             scratch_shapes=scratch_shapes, compiler_params=params)(*args)
        return tuple(res), ()
    cuts = [len(in_specs), len(rider.inputs), len(out_specs), len(rider.out_shape), len(scratch_shapes),
            len(rider.sems)]

    def hosted(*refs):
        groups, a = [], 0
        for cnt in cuts:
            groups.append(refs[a:a + cnt])
            a += cnt
        ins, r_in, outs, r_out, scr, r_sem = groups
        first = functools.reduce(jnp.logical_and, [pl.program_id(k) == 0 for k in range(len(grid))])
        last = functools.reduce(jnp.logical_and, [pl.program_id(k) == grid[k] - 1 for k in range(len(grid))])

        @pl.when(first)
        def _():
            rider.start(r_in, r_out, r_sem)

        body(*ins, *outs, *scr)

        @pl.when(last)
        def _():
            rider.finish(r_in, r_out, r_sem)

    anyspec = pl.BlockSpec(memory_space=pl.ANY)
    res = pl.pallas_call(
        hosted, name=name, grid=grid,
        in_specs=list(in_specs) + [anyspec] * cuts[1], out_specs=list(out_specs) + [anyspec] * cuts[3],
        out_shape=list(out_shape) + rider.out_shape, scratch_shapes=list(scratch_shapes) + rider.sems,
        compiler_params=params)(*args, *rider.inputs)
    return tuple(res[:cuts[2]]), tuple(res[cuts[2]:])


def _run_rider(rider, name):
    n_in, n_out = len(rider.inputs), len(rider.out_shape)

    def body(*refs):
        r_in, r_out, r_sem = refs[:n_in], refs[n_in:n_in + n_out], refs[n_in + n_out:]
        rider.start(r_in, r_out, r_sem)
        rider.finish(r_in, r_out, r_sem)

    anyspec = pl.BlockSpec(memory_space=pl.ANY)
    return pl.pallas_call(body, name=name, out_shape=rider.out_shape, in_specs=[anyspec] * n_in,
                          out_specs=[anyspec] * n_out, scratch_shapes=rider.sems)(*rider.inputs)


def _allgather_rider(arrs):
    n = len(arrs)

    def plan(ins, outs, sems):
        send_sems, recv_sems, local_sems = sems
        x, y, c = _mesh_pos()
        me, sibling = (x, y, c), (x, y, 1 - c)
        chips = [(1 - x, y), (x, 1 - y), (1 - x, 1 - y)]

        def rows(a, px, py, pc):
            r = ins[a].shape[0]
            return outs[a].at[pl.ds(pl.multiple_of((4 * px + 2 * py + pc) * r, 8), r), :]

        def copy(a, k, block, to, src=None):
            return pltpu.make_async_remote_copy(
                src_ref=rows(a, *block) if src is None else src, dst_ref=rows(a, *block),
                send_sem=send_sems.at[a * 7 + k], recv_sem=recv_sems.at[a * 7 + k],
                device_id=to, device_id_type=MESH)

        local = [pltpu.make_async_copy(ins[a], rows(a, *me), local_sems.at[a]) for a in range(n)]
        first = []
        for a in range(n):
            first.append(copy(a, 0, me, sibling, src=ins[a]))
            first += [copy(a, 1 + j, me, (*chip, c), src=ins[a]) for j, chip in enumerate(chips)]
        return c, me, sibling, chips, copy, local, first

    def start(ins, outs, sems):
        *_, local, first = plan(ins, outs, sems)
        for cp in local + first:
            cp.start()

    def finish(ins, outs, sems):
        c, me, sibling, chips, copy, local, first = plan(ins, outs, sems)
        passed = []
        for a in range(n):
            for j, chip in enumerate(chips):
                copy(a, 1 + j, (*chip, c), me).wait_recv()
                fwd = copy(a, 4 + j, (*chip, c), sibling)
                fwd.start()
                passed.append(fwd)
        for a in range(n):
            copy(a, 0, sibling, me).wait_recv()
            for j, chip in enumerate(chips):
                copy(a, 4 + j, (*chip, 1 - c), me).wait_recv()
        for cp in first + passed:
            cp.wait_send()
        for mine in local:
            mine.wait()

    return _Rider(arrs, [jax.ShapeDtypeStruct((N_DEV * a.shape[0], a.shape[1]), a.dtype) for a in arrs],
                  [pltpu.SemaphoreType.DMA((7 * n,)), pltpu.SemaphoreType.DMA((7 * n,)),
                   pltpu.SemaphoreType.DMA((n,))], start, finish)


def _piece(ref, col_sharded, k, h):
    m, n = ref.shape
    if col_sharded:
        mh, nc = m // 2, n // N_CHIP
        return ref.at[pl.ds(pl.multiple_of(h * mh, 16), mh), pl.ds(pl.multiple_of(k * nc, 128), nc)]
    rp = m // (2 * N_CHIP)
    return ref.at[pl.ds(pl.multiple_of((2 * k + h) * rp, 16), rp), :]


def _piece_shape(shape, col_sharded):
    m, n = shape
    return (m // 2, n // N_CHIP) if col_sharded else (m // (2 * N_CHIP), n)


def _cast_bf16(arrs, col_flags, k_idx, name, after=None):
    n = len(arrs)
    extra = [] if after is None else [after]

    def body(k_ref, *refs):
        outs = refs[n + len(extra):]
        for a in range(n):
            val = refs[a][...].astype(BF16)
            outs[a][...] = val
            outs[n + a][...] = val

    whole = [pl.BlockSpec(a.shape, lambda i, k_ref: (0, 0)) for a in arrs]
    window = [pl.BlockSpec(a.shape, (lambda i, k_ref: (0, k_ref[0])) if col else (lambda i, k_ref: (k_ref[0], 0)))
              for a, col in zip(arrs, col_flags)]
    res = pl.pallas_call(
        body, name=name,
        grid_spec=pltpu.PrefetchScalarGridSpec(
            num_scalar_prefetch=1, grid=(1,),
            in_specs=whole + [pl.BlockSpec(t.shape, lambda i, k_ref: (0, 0)) for t in extra],
            out_specs=whole + window),
        out_shape=[jax.ShapeDtypeStruct(a.shape, BF16) for a in arrs]
        + [jax.ShapeDtypeStruct(fs, BF16) for fs in _full_shapes(arrs, col_flags)],
        compiler_params=pltpu.CompilerParams(vmem_limit_bytes=VMEM_LIMIT_BYTES))(k_idx, *arrs, *extra)
    return list(res[:n]), list(res[n:])


def _full_shapes(shards, col_flags):
    return [(s.shape[0], s.shape[1] * N_CHIP) if col else (s.shape[0] * N_CHIP, s.shape[1])
            for s, col in zip(shards, col_flags)]


def _ici_copies(shard_refs, full_refs, send_sems, recv_sems, col_flags):
    x, y, c = _mesh_pos()
    k_me = 2 * x + y
    copies = []
    for a, (s_ref, f_ref) in enumerate(zip(shard_refs, full_refs)):
        rows = s_ref.shape[0] // 2
        src = s_ref.at[pl.ds(pl.multiple_of(c * rows, 16), rows), :]
        for j, chip in enumerate([(1 - x, y), (x, 1 - y), (1 - x, 1 - y)]):
            copies.append(pltpu.make_async_remote_copy(
                src_ref=src, dst_ref=_piece(f_ref, col_flags[a], k_me, c),
                send_sem=send_sems.at[a * 3 + j], recv_sem=recv_sems.at[a * 3 + j],
                device_id=(*chip, c), device_id_type=MESH))
    return copies


def _split_begin(bufs, n_sems, make_copies, after, name):
    n = len(bufs)
    hbm = pl.BlockSpec(memory_space=pltpu.HBM)
    sem = pl.BlockSpec(memory_space=pltpu.SEMAPHORE)

    def body(*refs):
        for cp in make_copies(refs[:n], refs[n + 1], refs[n + 2]):
            cp.start()
        refs[-1][...] = jnp.zeros_like(refs[-1])

    args = [pltpu.with_memory_space_constraint(t, pltpu.HBM) for t in bufs]
    res = pl.pallas_call(
        body, name=name,
        out_shape=[pltpu.SemaphoreType.DMA((n_sems,)), pltpu.SemaphoreType.DMA((n_sems,))]
        + [pltpu.HBM(t.shape, t.dtype) for t in args] + [jax.ShapeDtypeStruct((8, HEAD), F32)],
        in_specs=[hbm] * n + [pl.BlockSpec(memory_space=pl.ANY)],
        out_specs=[sem, sem] + [hbm] * n + [pl.BlockSpec(memory_space=pltpu.VMEM)],
        input_output_aliases={i: 2 + i for i in range(n)},
        compiler_params=pltpu.CompilerParams(has_side_effects=pltpu.SideEffectType.DATAFLOW_SIDE_EFFECTING),
    )(*args, after)
    return res[0], res[1], list(res[2:2 + n]), res[-1]


def _split_end(handle, make_copies, after, name):
    send_sems, recv_sems, bufs, _ = handle
    n = len(bufs)
    hbm = pl.BlockSpec(memory_space=pltpu.HBM)
    sem = pl.BlockSpec(memory_space=pltpu.SEMAPHORE)

    def body(*refs):
        for cp in make_copies(refs[:n], refs[n], refs[n + 1]):
            cp.wait_send()
            cp.wait_recv()

    res = pl.pallas_call(
        body, name=name,
        out_shape=[pltpu.HBM(t.shape, t.dtype) for t in bufs],
        in_specs=[hbm] * n + [sem, sem, pl.BlockSpec(memory_space=pl.ANY)],
        out_specs=[hbm] * n,
        input_output_aliases={i: i for i in range(n)},
        compiler_params=pltpu.CompilerParams(has_side_effects=pltpu.SideEffectType.DATAFLOW_SIDE_EFFECTING),
    )(*bufs, send_sems, recv_sems, after)
    return list(res)


def _gather_copies(n, col_flags):
    return lambda refs, send_sems, recv_sems: _ici_copies(refs[:n], refs[n:], send_sems, recv_sems, col_flags)


def _gather_begin(shards, fulls, col_flags, after, name):
    n = len(shards)
    return _split_begin(list(shards) + list(fulls), 3 * n, _gather_copies(n, col_flags), after, name)


def _gather_end(handle, col_flags, after, name):
    n = len(handle[2]) // 2
    return _split_end(handle, _gather_copies(n, col_flags), after, name)[n:]


def _sibling_copies(col_flag):
    def make(refs, send_sems, recv_sems):
        grad_ref, land_ref = refs
        x, y, c = _mesh_pos()
        return [pltpu.make_async_remote_copy(
            src_ref=_piece(grad_ref, col_flag, k, 1 - c), dst_ref=land_ref.at[k],
            send_sem=send_sems.at[k], recv_sem=recv_sems.at[k],
            device_id=(x, y, 1 - c), device_id_type=MESH) for k in range(N_CHIP)]
    return make


def _chips_copies(n):
    def make(refs, send_sems, recv_sems):
        parts, landed = refs[:n], refs[n:]
        x, y, c = _mesh_pos()
        k_me = 2 * x + y
        copies = []
        for a in range(n):
            for j, chip in enumerate([(1 - x, y), (x, 1 - y), (1 - x, 1 - y)]):
                copies.append(pltpu.make_async_remote_copy(
                    src_ref=parts[a].at[2 * chip[0] + chip[1]], dst_ref=landed[a].at[k_me],
                    send_sem=send_sems.at[a * 3 + j], recv_sem=recv_sems.at[a * 3 + j],
                    device_id=(*chip, c), device_id_type=MESH))
        return copies
    return make


def _chips_begin(parts, after, name):
    landed = [lax.empty(p.shape, p.dtype) for p in parts]
    return _split_begin(list(parts) + landed, 3 * len(parts), _chips_copies(len(parts)), after, name)


def _chips_end(handle, after, name):
    n = len(handle[2]) // 2
    res = _split_end(handle, _chips_copies(n), after, name)
    return res[:n], res[n:]


def _small_rows(buf_ref, px, py, pc):
    r = buf_ref.shape[0] // N_DEV
    return buf_ref.at[pl.ds(pl.multiple_of((4 * px + 2 * py + pc) * r, 8), r), :]


def _small_ici_copies(refs, send_sems, recv_sems):
    x, y, c = _mesh_pos()
    copies = []
    for a, buf in enumerate(refs):
        mine = _small_rows(buf, x, y, c)
        for j, chip in enumerate([(1 - x, y), (x, 1 - y), (1 - x, 1 - y)]):
            copies.append(pltpu.make_async_remote_copy(
                src_ref=mine, dst_ref=mine, send_sem=send_sems.at[a * 3 + j], recv_sem=recv_sems.at[a * 3 + j],
                device_id=(*chip, c), device_id_type=MESH))
    return copies


def _small_finish(bufs, name):
    n = len(bufs)

    def body(*refs):
        buf_refs = refs[n:2 * n]
        send_sems, recv_sems = refs[2 * n:]
        x, y, c = _mesh_pos()
        owners = [(x, y), (1 - x, y), (x, 1 - y), (1 - x, 1 - y)]
        passed, arriving = [], []
        for a in range(n):
            for j, (px, py) in enumerate(owners):
                for pc, group in ((c, passed), (1 - c, arriving)):
                    rows = _small_rows(buf_refs[a], px, py, pc)
                    group.append(pltpu.make_async_remote_copy(
                        src_ref=rows, dst_ref=rows, send_sem=send_sems.at[a * 4 + j],
                        recv_sem=recv_sems.at[a * 4 + j], device_id=(x, y, 1 - c), device_id_type=MESH))
        for cp in passed:
            cp.start()
        for cp in arriving:
            cp.wait_recv()
        for cp in passed:
            cp.wait_send()

    anyspec = pl.BlockSpec(memory_space=pl.ANY)
    return pl.pallas_call(
        body, name=name, out_shape=[jax.ShapeDtypeStruct(b.shape, b.dtype) for b in bufs],
        in_specs=[anyspec] * n, out_specs=[anyspec] * n, input_output_aliases={a: a for a in range(n)},
        scratch_shapes=[pltpu.SemaphoreType.DMA((4 * n,)), pltpu.SemaphoreType.DMA((4 * n,))],
    )(*bufs)


def _sibling_begin(grad, col_flag, after, name):
    land = lax.empty((N_CHIP,) + _piece_shape(grad.shape, col_flag), grad.dtype)
    return _split_begin([grad, land], N_CHIP, _sibling_copies(col_flag), after, name)


def _sibling_end(handle, col_flag, after, name):
    return _split_end(handle, _sibling_copies(col_flag), after, name)


def _gather_finish(fulls, col_flags, name):
    n = len(fulls)

    def body(*refs):
        full_refs = refs[n:2 * n]
        send_sems, recv_sems = refs[2 * n:]
        x, y, c = _mesh_pos()
        passed, arriving = [], []
        for a in range(n):
            for j, chip in enumerate([(1 - x, y), (x, 1 - y), (1 - x, 1 - y)]):
                k_from = 2 * chip[0] + chip[1]
                for h, group in ((c, passed), (1 - c, arriving)):
                    win = _piece(full_refs[a], col_flags[a], k_from, h)
                    group.append(pltpu.make_async_remote_copy(
                        src_ref=win, dst_ref=win, send_sem=send_sems.at[a * 3 + j],
                        recv_sem=recv_sems.at[a * 3 + j], device_id=(x, y, 1 - c), device_id_type=MESH))
        for cp in passed:
            cp.start()
        for cp in arriving:
            cp.wait_recv()
        for cp in passed:
            cp.wait_send()

    anyspec = pl.BlockSpec(memory_space=pl.ANY)
    return pl.pallas_call(
        body, name=name,
        out_shape=[jax.ShapeDtypeStruct(f.shape, f.dtype) for f in fulls],
        in_specs=[anyspec] * n, out_specs=[anyspec] * n,
        input_output_aliases={a: a for a in range(n)},
        scratch_shapes=[pltpu.SemaphoreType.DMA((3 * n,)), pltpu.SemaphoreType.DMA((3 * n,))],
    )(*fulls)


def _sibling_rider(grads, col_flags):
    n = len(grads)
    pshapes = [_piece_shape(g.shape, col) for g, col in zip(grads, col_flags)]

    def copies(ins, outs, sems):
        send_sems, recv_sems = sems
        x, y, c = _mesh_pos()
        return [pltpu.make_async_remote_copy(
            src_ref=_piece(ins[a], col_flags[a], k, 1 - c), dst_ref=outs[a].at[k],
            send_sem=send_sems.at[a * N_CHIP + k], recv_sem=recv_sems.at[a * N_CHIP + k],
            device_id=(x, y, 1 - c), device_id_type=MESH) for a in range(n) for k in range(N_CHIP)]

    def start(ins, outs, sems):
        for cp in copies(ins, outs, sems):
            cp.start()

    def finish(ins, outs, sems):
        cps = copies(ins, outs, sems)
        for cp in cps:
            cp.wait_recv()
        for cp in cps:
            cp.wait_send()

    return _Rider(grads, [jax.ShapeDtypeStruct((N_CHIP,) + ps, g.dtype) for ps, g in zip(pshapes, grads)],
                  [pltpu.SemaphoreType.DMA((N_CHIP * n,)), pltpu.SemaphoreType.DMA((N_CHIP * n,))], start, finish)


def _sum_with_sibling(grads, landed, col_flags, c_idx, name):
    n = len(grads)
    pshapes = [_piece_shape(g.shape, col) for g, col in zip(grads, col_flags)]

    def body(c_ref, *refs):
        ins, lands, outs = refs[:n], refs[n:2 * n], refs[2 * n:]
        for a in range(n):
            outs[a][0] = (ins[a][...] + lands[a][0]).astype(BF16)

    in_specs = []
    for ps, col in zip(pshapes, col_flags):
        if col:
            in_specs.append(pl.BlockSpec(ps, lambda k, c_ref: (c_ref[0], k)))
        else:
            in_specs.append(pl.BlockSpec(ps, lambda k, c_ref: (2 * k + c_ref[0], 0)))
    land_specs = [pl.BlockSpec((1,) + ps, lambda k, c_ref: (k, 0, 0)) for ps in pshapes]
    return pl.pallas_call(
        body, name=name,
        grid_spec=pltpu.PrefetchScalarGridSpec(
            num_scalar_prefetch=1, grid=(N_CHIP,),
            in_specs=in_specs + land_specs, out_specs=land_specs),
        out_shape=[jax.ShapeDtypeStruct((N_CHIP,) + ps, BF16) for ps in pshapes],
        compiler_params=pltpu.CompilerParams(dimension_semantics=("arbitrary",), vmem_limit_bytes=VMEM_LIMIT_BYTES),
    )(c_idx, *grads, *landed)


def _chips_rider(parts):
    n = len(parts)

    def plan(ins, outs, sems, arriving):
        send_sems, recv_sems = sems
        x, y, c = _mesh_pos()
        k_me = 2 * x + y
        copies = []
        for a in range(n):
            for j, chip in enumerate([(1 - x, y), (x, 1 - y), (1 - x, 1 - y)]):
                k_peer = 2 * chip[0] + chip[1]
                copies.append(pltpu.make_async_remote_copy(
                    src_ref=ins[a].at[k_peer], dst_ref=outs[a].at[k_peer if arriving else k_me],
                    send_sem=send_sems.at[a * 3 + j], recv_sem=recv_sems.at[a * 3 + j],
                    device_id=(*chip, c), device_id_type=MESH))
        return copies

    def start(ins, outs, sems):
        for cp in plan(ins, outs, sems, False):
            cp.start()

    def finish(ins, outs, sems):
        arrivals = plan(ins, outs, sems, True)
        for cp in arrivals:
            cp.wait_recv()
        for cp in arrivals:
            cp.wait_send()

    return _Rider(parts, [jax.ShapeDtypeStruct(p.shape, p.dtype) for p in parts],
                  [pltpu.SemaphoreType.DMA((3 * n,)), pltpu.SemaphoreType.DMA((3 * n,))], start, finish)


def _sum_chips_and_share(landed, parts, after, name):
    n = len(landed)

    def body(*refs):
        ins, own, outs, red = refs[:n], refs[n:2 * n], refs[2 * n + 1:3 * n + 1], refs[3 * n + 1:4 * n + 1]
        send_sems, recv_sems, local_sems = refs[4 * n + 1:]
        x, y, c = _mesh_pos()
        sibling = (x, y, 1 - c)
        k_me = 2 * x + y
        copies, local = [], []
        for a in range(n):
            for k in range(N_CHIP):
                @pl.when(k_me == k)
                def _():
                    term = own[a][k].astype(F32)
                    red[a][...] = term if k == 0 else red[a][...] + term

                @pl.when(k_me != k)
                def _():
                    term = ins[a][k].astype(F32)
                    red[a][...] = term if k == 0 else red[a][...] + term

            mine = pltpu.make_async_copy(red[a], outs[a].at[c], local_sems.at[a])
            mine.start()
            local.append(mine)
            cp = pltpu.make_async_remote_copy(
                src_ref=red[a], dst_ref=outs[a].at[c],
                send_sem=send_sems.at[a], recv_sem=recv_sems.at[a],
                device_id=sibling, device_id_type=MESH)
            cp.start()
            copies.append(cp)
        for a in range(n):
            pltpu.make_async_remote_copy(
                src_ref=red[a], dst_ref=outs[a].at[1 - c],
                send_sem=send_sems.at[a], recv_sem=recv_sems.at[a],
                device_id=sibling, device_id_type=MESH).wait_recv()
        for cp in copies:
            cp.wait_send()
        for mine in local:
            mine.wait()

    return pl.pallas_call(
        body, name=name,
        out_shape=[jax.ShapeDtypeStruct((2,) + l.shape[1:], F32) for l in landed],
        in_specs=[pl.BlockSpec(memory_space=pltpu.VMEM)] * (2 * n + 1),
        out_specs=[pl.BlockSpec(memory_space=pl.ANY)] * n,
        scratch_shapes=[pltpu.VMEM(l.shape[1:], F32) for l in landed]
        + [pltpu.SemaphoreType.DMA((n,)), pltpu.SemaphoreType.DMA((n,)), pltpu.SemaphoreType.DMA((n,))],
        compiler_params=pltpu.CompilerParams(vmem_limit_bytes=VMEM_LIMIT_BYTES),
    )(*landed, *parts, after)


def _mod_shard(c_all, w_ada, b_shard, after):
    def body(c_ref, w_ref, b_ref, after_ref, o_ref, sc_ref):
        cc = c_ref[...]
        sc = cc * _sigmoid(cc)
        sc_ref[...] = sc
        o_ref[...] = _dot(sc, w_ref[...]) + b_ref[...]

    nb, d = c_all.shape
    nn = w_ada.shape[1]
    vm = pl.BlockSpec(memory_space=pltpu.VMEM)
    return pl.pallas_call(
        body, name="mod_shard",
        in_specs=[vm, vm, vm, pl.BlockSpec(memory_space=pl.ANY)], out_specs=[vm, vm],
        out_shape=[jax.ShapeDtypeStruct((nb, nn), F32), jax.ShapeDtypeStruct((nb, d), F32)],
        compiler_params=pltpu.CompilerParams(vmem_limit_bytes=VMEM_LIMIT_BYTES),
    )(c_all, w_ada, b_shard, after)


V_SH_M, V_SC_M, V_G_M, V_SH_F, V_SC_F, V_G_F, V_PRE_MIX, V_POST_MIX, V_PRE_FFN, V_POST_FFN = range(10)


def _vrow(vec_ref, r):
    return vec_ref[r:r + 1, :]


def _mix_fwd(x, vecs, w_in_b, w_out_b, sgu_g, wm_b, bsb, wp_b, ps, pmats, ts):
    s_len, d = x.shape
    nt = s_len // ts
    n_proj = w_in_b.shape[1]

    def body(x_ref, vec_ref, win_ref, wout_ref, sg_ref, wm_ref, bs_ref, wp_ref, ps_ref, pm_ref,
             h1_ref, proj_ref, cat_ref, mixed_ref, x2_ref, h2_ref, carry_ref):
        i = pl.program_id(0)

        @pl.when(i == 0)
        def _():
            carry_ref[...] = jnp.zeros_like(carry_ref)

        sub = FFN_TS
        nsub = ts // sub

        def project(s):
            rs = slice(s * sub, (s + 1) * sub)
            x = x_ref[rs, :]
            h1 = (((x * _rms(x)) * _vrow(vec_ref, V_PRE_MIX)) * (1.0 + _vrow(vec_ref, V_SC_M))
                  + _vrow(vec_ref, V_SH_M)).astype(BF16)
            h1_ref[rs, :] = h1
            proj = _dot(h1, win_ref[...])
            proj_ref[rs, :] = proj.astype(BF16)
            return proj

        def mix(s, proj, halo):
            r0 = s * sub
            t_glob = lax.broadcasted_iota(jnp.int32, (sub, HEAD), 0) + (i * ts + r0)
            for h in range(N_HEADS):
                u = _gelu(proj[:, h * HEAD:(h + 1) * HEAD])
                v = _gelu(proj[:, A_WIDTH + h * HEAD:A_WIDTH + (h + 1) * HEAD])
                vn = ((v * _rms(v)) * sg_ref[h:h + 1, :]).astype(BF16)
                for b in range(sub // HEAD):
                    rs = slice(b * HEAD, (b + 1) * HEAD)
                    z = _dot(wm_ref[h], vn[rs]) + bs_ref[h]
                    cat_ref[r0 + b * HEAD:r0 + (b + 1) * HEAD, h * HEAD:(h + 1) * HEAD] = (u[rs] * z).astype(BF16)
            for g in range(len(POOL_WINDOWS)):
                gs = slice(g * HEAD, (g + 1) * HEAD)
                p = proj[:, 2 * A_WIDTH + g * HEAD:2 * A_WIDTH + (g + 1) * HEAD]
                pooled = _pool_fwd(p, halo[:, gs], g, t_glob)
                yb = _dot(pooled.astype(BF16), wp_ref[g]) * ps_ref[0:1, gs]
                cat_ref[r0:r0 + sub, A_WIDTH + g * HEAD:A_WIDTH + (g + 1) * HEAD] = yb.astype(BF16)

        def finish(s, mixed):
            rs = slice(s * sub, (s + 1) * sub)
            mixed_ref[rs, :] = mixed
            x2 = x_ref[rs, :] + _vrow(vec_ref, V_G_M) * ((mixed * _rms(mixed)) * _vrow(vec_ref, V_POST_MIX))
            x2_ref[rs, :] = x2
            h2 = (((x2 * _rms(x2)) * _vrow(vec_ref, V_PRE_FFN)) * (1.0 + _vrow(vec_ref, V_SC_F))
                  + _vrow(vec_ref, V_SH_F)).astype(BF16)
            h2_ref[rs, :] = _permute_bf16(pm_ref[0], h2)

        projs = [project(0)]
        halo = carry_ref[...]
        mixed_prev = None
        for s in range(nsub):
            if s + 1 < nsub:
                projs.append(project(s + 1))
            mix(s, projs[s], halo)
            halo = projs[s][sub - POOL_HALO:sub, 2 * A_WIDTH:]
            mixed = _dot(cat_ref[s * sub:(s + 1) * sub, :], wout_ref[...])
            if mixed_prev is not None:
                finish(s - 1, mixed_prev)
            mixed_prev = mixed
        carry_ref[...] = halo
        finish(nsub - 1, mixed_prev)

    row = lambda i: (i, 0)
    return _call(
        body, name="mix_fwd", grid=(nt,),
        in_specs=[_tiled((ts, d), row), _whole(vecs.shape), _resident(w_in_b.shape), _resident(w_out_b.shape),
                  _whole(sgu_g.shape), _whole(wm_b.shape), _whole(bsb.shape), _whole(wp_b.shape), _whole(ps.shape),
                  _whole(pmats.shape)],
        out_specs=[_tiled((ts, d), row), _tiled((ts, n_proj), row), _tiled((ts, d), row),
                   _tiled((ts, d), row), _tiled((ts, d), row), _tiled((ts, d), row)],
        out_shape=[jax.ShapeDtypeStruct((s_len, d), BF16), jax.ShapeDtypeStruct((s_len, n_proj), BF16),
                   jax.ShapeDtypeStruct((s_len, d), BF16), jax.ShapeDtypeStruct((s_len, d), F32),
                   jax.ShapeDtypeStruct((s_len, d), F32), jax.ShapeDtypeStruct((s_len, d), BF16)],
        scratch_shapes=[pltpu.VMEM((POOL_HALO, A_WIDTH), F32)],
        args=(x, vecs, w_in_b, w_out_b, sgu_g, wm_b, bsb, wp_b, ps, pmats))


def _perm_mats(ts):
    p = jnp.arange(ts)
    pm = (((p % SUBLANES) * (ts // SUBLANES) + p // SUBLANES)[:, None] == p[None, :]).astype(BF16)
    return jnp.stack([pm, pm.T])


def _permute_bf16(pm, xb):
    return _dot(pm, xb).astype(BF16)


def _permute_f32(pm, x):
    hi = x.astype(BF16)
    lo = (x - hi.astype(F32)).astype(BF16)
    return _dot(pm, hi) + _dot(pm, lo)


def _conv_out(u, um2, um1, cv_ref, cols):
    return (cv_ref[3:4, cols] + um2 * cv_ref[0:1, cols] + um1 * cv_ref[1:2, cols] + u * cv_ref[2:3, cols])


F_LOSS, F_DGF, F_DPOSTFFN = 0, 1, 2
B_DSHF, B_DSCF, B_DPREFFN, B_DGM, B_DPOSTMIX = 0, 1, 2, 3, 4
M_DSHM, M_DSCM, M_DPREMIX = 0, 1, 2
C_DCB, C_DCW = 0, 1
G1_F, G1_B, G1_M = 0, 8, 16


def _ffn_fwd(h2p, x2, tgt, w_up_b, w_down_b, cvec, vecs, pmats, ts):
    s_len, d = x2.shape
    ff2 = w_up_b.shape[1]
    ff = ff2 // 2
    nt = s_len // ts
    chunks = _ff_chunks(ff)

    def body(h2_ref, x2_ref, t_ref, wu_ref, wd_ref, cv_ref, vec_ref, pm_ref,
             up_ref, y_ref, act_ref, dy_ref, df_ref, acc_ref, carry_ref):
        @pl.when(pl.program_id(0) == 0)
        def _():
            carry_ref[...] = jnp.zeros_like(carry_ref)
            acc_ref[...] = jnp.zeros_like(acc_ref)

        h2v = h2_ref[...]

        def up_dots(o, w):
            return [_dot(h2v, wu_ref[:, base + o:base + o + w]) for base in (0, ff)]

        f = None
        pending = None
        nxt = up_dots(*chunks[0])
        for ci, (o, w) in enumerate(chunks):
            us = nxt
            if ci + 1 < len(chunks):
                nxt = up_dots(*chunks[ci + 1])
            if pending is not None:
                part = _dot(pending[0], wd_ref[pending[1]:pending[1] + pending[2], :])
                f = part if f is None else f + part
            sub0 = lax.broadcasted_iota(jnp.int32, (SUBLANES, w), 0) == 0
            ys = []
            for base, u in zip((0, ff), us):
                cols = slice(base + o, base + o + w)
                up_ref[:, cols] = u.astype(BF16)
                last1, last2 = u[ts - SUBLANES:ts], u[ts - CONV_KEEP:ts - SUBLANES]
                b1 = jnp.where(sub0, pltpu.roll(carry_ref[SUBLANES:CONV_KEEP, cols], 1, 0), pltpu.roll(last1, 1, 0))
                b2 = jnp.where(sub0, pltpu.roll(carry_ref[0:SUBLANES, cols], 1, 0), pltpu.roll(last2, 1, 0))
                um1 = jnp.concatenate([b1, u[:ts - SUBLANES]], axis=0)
                um2 = jnp.concatenate([b2, b1, u[:ts - CONV_KEEP]], axis=0)
                ys.append(_conv_out(u, um2, um1, cv_ref, cols))
                carry_ref[:, cols] = u[ts - CONV_KEEP:ts]
            gate, val = ys
            sg = _sigmoid(gate)
            gs = gate * sg
            act = (gs * val).astype(BF16)
            act_ref[:, o:o + w] = act
            y_ref[:, o:o + w] = (val * (sg + gs * (1.0 - sg))).astype(BF16)
            y_ref[:, ff + o:ff + o + w] = gs.astype(BF16)
            pending = (act, o, w)
        f = f + _dot(pending[0], wd_ref[pending[1]:pending[1] + pending[2], :])
        f = _permute_f32(pm_ref[1], f)
        r3 = _rms(f)
        fhat = f * r3
        post = _vrow(vec_ref, V_POST_FFN)
        g_f = _vrow(vec_ref, V_G_F)
        fn = fhat * post
        e = (x2_ref[...] + g_f * fn) - t_ref[...]
        dy = e * (1.0 / d)
        dy_ref[...] = dy
        acc_ref[F_LOSS:F_LOSS + 1, :] += _colsum(e * e)
        acc_ref[F_DGF:F_DGF + 1, :] += _colsum(dy * fhat)
        dfhat = dy * (g_f * post)
        df = (r3 * (dfhat - fhat * _rowmean(dfhat * fhat))).astype(BF16)
        df_ref[...] = _permute_bf16(pm_ref[0], df)

        @pl.when(pl.program_id(0) == nt - 1)
        def _():
            through_norm = acc_ref[F_DGF:F_DGF + 1, :]
            acc_ref[F_DGF:F_DGF + 1, :] = through_norm * post
            acc_ref[F_DPOSTFFN:F_DPOSTFFN + 1, :] = through_norm * g_f

    row = lambda i: (i, 0)
    return pl.pallas_call(
        body, name="ffn_fwd", grid=(nt,),
        in_specs=[_tiled((ts, d), row), _tiled((ts, d), row), _tiled((ts, d), row), _resident(w_up_b.shape),
                  _resident(w_down_b.shape), _whole(cvec.shape), _whole(vecs.shape), _whole(pmats.shape)],
        out_specs=[_tiled((ts, ff2), row), _tiled((ts, ff2), row), _tiled((ts, ff), row), _tiled((ts, d), row),
                   _tiled((ts, d), row), _whole((8, d))],
        out_shape=[jax.ShapeDtypeStruct((s_len, ff2), BF16), jax.ShapeDtypeStruct((s_len, ff2), BF16),
                   jax.ShapeDtypeStruct((s_len, ff), BF16), jax.ShapeDtypeStruct((s_len, d), F32),
                   jax.ShapeDtypeStruct((s_len, d), BF16), jax.ShapeDtypeStruct((8, d), F32)],
        scratch_shapes=[pltpu.VMEM((CONV_KEEP, ff2), F32)],
        compiler_params=_seq_params(),
    )(h2p, x2, tgt, w_up_b, w_down_b, cvec, vecs, pmats)


def _ffn_bwd(dfp, upp, yp, x2, dy, mixed, w_up_b, w_down_b, cvec, vecs, pmats, ts, rider=None):
    s_len, d = x2.shape
    ff2 = w_up_b.shape[1]
    ff = ff2 // 2
    nt = s_len // ts
    chunks = _ff_chunks(ff, 512)

    def body(df_ref, up_ref, y_ref, x2_ref, dy_ref, mx_ref, wu_ref, wd_ref, cv_ref, vec_ref, pm_ref,
             dup_ref, dx2_ref, dmx_ref, accc_ref, acc_ref, carry_ref):
        @pl.when(pl.program_id(0) == 0)
        def _():
            carry_ref[...] = jnp.zeros_like(carry_ref)
            accc_ref[...] = jnp.zeros_like(accc_ref)
            acc_ref[...] = jnp.zeros_like(acc_ref)

        dfv = df_ref[...]

        def dh2_add(acc, dups, o, w):
            for base, dup in zip((0, ff), dups):
                part = _dot_nt(dup, wu_ref[:, base + o:base + o + w])
                acc = part if acc is None else acc + part
            return acc

        dh2 = None
        pending = None
        nxt = _dot_nt(dfv, wd_ref[chunks[0][0]:chunks[0][0] + chunks[0][1], :])
        for ci, (o, w) in enumerate(chunks):
            dact = nxt
            if ci + 1 < len(chunks):
                o2, w2 = chunks[ci + 1]
                nxt = _dot_nt(dfv, wd_ref[o2:o2 + w2, :])
            if pending is not None:
                dh2 = dh2_add(dh2, *pending)
            sub7 = lax.broadcasted_iota(jnp.int32, (SUBLANES, w), 0) == SUBLANES - 1
            dups = []
            dys = (dact * y_ref[:, o:o + w].astype(F32), dact * y_ref[:, ff + o:ff + o + w].astype(F32))
            for base, dyv in zip((0, ff), dys):
                cols = slice(base + o, base + o + w)
                u = up_ref[:, cols].astype(F32)
                up1 = SUBLANES - 1
                e0 = jnp.where(sub7, pltpu.roll(carry_ref[0:SUBLANES, cols], up1, 0),
                               pltpu.roll(dyv[0:SUBLANES], up1, 0))
                e1 = jnp.where(sub7, pltpu.roll(carry_ref[SUBLANES:CONV_KEEP, cols], up1, 0),
                               pltpu.roll(dyv[SUBLANES:CONV_KEEP], up1, 0))
                dyp1 = jnp.concatenate([dyv[SUBLANES:], e0], axis=0)
                dyp2 = jnp.concatenate([dyv[CONV_KEEP:], e0, e1], axis=0)
                accc_ref[C_DCB:C_DCB + 1, cols] += _colsum(dyv)
                accc_ref[C_DCW + 0:C_DCW + 1, cols] += _colsum(dyp2 * u)
                accc_ref[C_DCW + 1:C_DCW + 2, cols] += _colsum(dyp1 * u)
                accc_ref[C_DCW + 2:C_DCW + 3, cols] += _colsum(dyv * u)
                dup = (dyv * cv_ref[2:3, cols] + dyp1 * cv_ref[1:2, cols] + dyp2 * cv_ref[0:1, cols]).astype(BF16)
                dup_ref[:, cols] = dup
                dups.append(dup)
                carry_ref[:, cols] = dyv[0:CONV_KEEP]
            pending = (dups, o, w)
        dh2 = dh2_add(dh2, *pending)
        dh2 = _permute_f32(pm_ref[1], dh2)
        x2 = x2_ref[...]
        r2 = _rms(x2)
        xn = x2 * r2
        pre = _vrow(vec_ref, V_PRE_FFN)
        one_sc = 1.0 + _vrow(vec_ref, V_SC_F)
        acc_ref[B_DSHF:B_DSHF + 1, :] += _colsum(dh2)
        acc_ref[B_DSCF:B_DSCF + 1, :] += _colsum(dh2 * xn)
        dxn = dh2 * (pre * one_sc)
        dx2 = dy_ref[...] + r2 * (dxn - xn * _rowmean(dxn * xn))
        dx2_ref[...] = dx2
        mixed = mx_ref[...]
        rm = _rms(mixed)
        mhat = mixed * rm
        post = _vrow(vec_ref, V_POST_MIX)
        g_m = _vrow(vec_ref, V_G_M)
        acc_ref[B_DGM:B_DGM + 1, :] += _colsum(dx2 * mhat)
        dmhat = dx2 * (g_m * post)
        dmx_ref[...] = (rm * (dmhat - mhat * _rowmean(dmhat * mhat))).astype(BF16)

        @pl.when(pl.program_id(0) == nt - 1)
        def _():
            through_norm = acc_ref[B_DSCF:B_DSCF + 1, :]
            acc_ref[B_DSCF:B_DSCF + 1, :] = through_norm * pre
            acc_ref[B_DPREFFN:B_DPREFFN + 1, :] = through_norm * one_sc
            through_mix = acc_ref[B_DGM:B_DGM + 1, :]
            acc_ref[B_DGM:B_DGM + 1, :] = through_mix * post
            acc_ref[B_DPOSTMIX:B_DPOSTMIX + 1, :] = through_mix * g_m

    rev = lambda i: (nt - 1 - i, 0)
    return _call(
        body, name="ffn_bwd", grid=(nt,), rider=rider,
        in_specs=[_tiled((ts, d), rev), _tiled((ts, ff2), rev), _tiled((ts, ff2), rev), _tiled((ts, d), rev),
                  _tiled((ts, d), rev), _tiled((ts, d), rev), _resident(w_up_b.shape), _resident(w_down_b.shape),
                  _whole(cvec.shape), _whole(vecs.shape), _whole(pmats.shape)],
        out_specs=[_tiled((ts, ff2), rev), _tiled((ts, d), rev), _tiled((ts, d), rev), _whole((8, ff2)),
                   _whole((8, d))],
        out_shape=[jax.ShapeDtypeStruct((s_len, ff2), BF16), jax.ShapeDtypeStruct((s_len, d), F32),
                   jax.ShapeDtypeStruct((s_len, d), BF16), jax.ShapeDtypeStruct((8, ff2), F32),
                   jax.ShapeDtypeStruct((8, d), F32)],
        scratch_shapes=[pltpu.VMEM((CONV_KEEP, ff2), F32)],
        args=(dfp, upp, yp, x2, dy, mixed, w_up_b, w_down_b, cvec, vecs, pmats))


def _mix_bwd(dmixed, dx2, x, proj, vecs, w_in_b, w_out_b, sgu_g, wm_b, wmt_b, bsb, wp_b, ps, ts):
    s_len, d = x.shape
    nt = s_len // ts
    n_proj = proj.shape[1]
    per = ts // POOL_HALO

    def body(dmx_ref, dx2_ref, x_ref, proj_ref, projh_ref, vec_ref, win_ref, wout_ref, sg_ref, wm_ref, wmt_ref,
             bs_ref, wp_ref, ps_ref,
             gx_ref, dproj_ref, acc_ref, dwm_out, dwp_out, db_ref, dg_ref, dps_ref,
             carry_ref, dwm_ref, dwp_ref, dbz_ref):
        i = pl.program_id(0)
        tile = nt - 1 - i

        @pl.when(i == 0)
        def _():
            carry_ref[...] = jnp.zeros_like(carry_ref)
            for r in (acc_ref, dwm_ref, dwp_ref, dbz_ref, dg_ref, dps_ref):
                r[...] = jnp.zeros_like(r)

        sub = FFN_TS
        nsub = ts // sub
        ext = sub + POOL_HALO

        def cotangent(s):
            return _dot_nt(dmx_ref[s * sub:(s + 1) * sub, :], wout_ref[...])

        def back(s, dcat, halo, later):
            r0 = s * sub
            rows = slice(r0, r0 + sub)
            t_glob = lax.broadcasted_iota(jnp.int32, (sub, HEAD), 0) + (tile * ts + r0)
            for h in range(N_HEADS):
                hs = slice(h * HEAD, (h + 1) * HEAD)
                vs = slice(A_WIDTH + h * HEAD, A_WIDTH + (h + 1) * HEAD)
                gain = sg_ref[h:h + 1, :]
                for b in range(sub // HEAD):
                    blk = slice(r0 + b * HEAD, r0 + (b + 1) * HEAD)
                    au = proj_ref[blk, hs].astype(F32)
                    av = proj_ref[blk, vs].astype(F32)
                    u, u_grad = _gelu_and_grad(au)
                    v, v_grad = _gelu_and_grad(av)
                    rv = _rms(v)
                    vhat = v * rv
                    vn = (vhat * gain).astype(BF16)
                    dout = dcat[b * HEAD:(b + 1) * HEAD, hs]
                    z = _dot(wm_ref[h], vn) + bs_ref[h]
                    dz = dout * u
                    dbz_ref[h] += dz
                    dzb = dz.astype(BF16)
                    dwm_ref[h] += _dot_nt(dzb, vn)
                    dvn = _dot(wmt_ref[h], dzb)
                    dg_ref[h:h + 1, :] += _colsum(dvn * vhat)
                    dvhat = dvn * gain
                    dv = rv * (dvhat - vhat * _rowmean(dvhat * vhat))
                    dproj_ref[blk, hs] = ((dout * z) * u_grad).astype(BF16)
                    dproj_ref[blk, vs] = (dv * v_grad).astype(BF16)
            firsts = []
            for g in range(len(POOL_WINDOWS)):
                gs = slice(g * HEAD, (g + 1) * HEAD)
                pcols = slice(2 * A_WIDTH + g * HEAD, 2 * A_WIDTH + (g + 1) * HEAD)
                p = proj_ref[rows, pcols].astype(F32)
                pb = _pool_fwd(p, halo[:, gs], g, t_glob).astype(BF16)
                dyb = dcat[:, A_WIDTH + g * HEAD:A_WIDTH + (g + 1) * HEAD]
                dps_ref[0:1, gs] += _colsum(dyb * _dot(pb, wp_ref[g]))
                dyl = (dyb * ps_ref[0:1, gs]).astype(BF16)
                dwp_ref[g] += _dot_tn(pb, dyl)
                dpooled = _dot_nt(dyl, wp_ref[g])
                cnt = jnp.minimum(t_glob + 1, POOL_WINDOWS[g]).astype(F32)
                q = dpooled / cnt
                acc = jnp.concatenate([q, later[:, gs]], axis=0)
                for step in range(g + 1):
                    acc = acc + pltpu.roll(acc, ext - (1 << step), 0)
                dproj_ref[rows, pcols] = (acc[:sub] - dpooled).astype(BF16)
                firsts.append(q[0:POOL_HALO])
            return jnp.concatenate(firsts, axis=1)

        def finish(s, dh1):
            rows = slice(s * sub, (s + 1) * sub)
            x = x_ref[rows, :]
            r1 = _rms(x)
            xn = x * r1
            pre = _vrow(vec_ref, V_PRE_MIX)
            one_sc = 1.0 + _vrow(vec_ref, V_SC_M)
            acc_ref[M_DSHM:M_DSHM + 1, :] += _colsum(dh1)
            acc_ref[M_DSCM:M_DSCM + 1, :] += _colsum(dh1 * xn)
            dxn = dh1 * (pre * one_sc)
            gx_ref[rows, :] = dx2_ref[rows, :] + r1 * (dxn - xn * _rowmean(dxn * xn))

        nxt = cotangent(nsub - 1)
        later = carry_ref[...]
        dh1_prev = None
        for s in reversed(range(nsub)):
            dcat = nxt
            if s > 0:
                nxt = cotangent(s - 1)
                halo = proj_ref[s * sub - POOL_HALO:s * sub, 2 * A_WIDTH:].astype(F32)
            else:
                halo = jnp.where(tile > 0, projh_ref[:, 2 * A_WIDTH:].astype(F32), 0.0)
            later = back(s, dcat, halo, later)
            dh1 = _dot_nt(dproj_ref[s * sub:(s + 1) * sub, :], win_ref[...])
            if dh1_prev is not None:
                finish(s + 1, dh1_prev)
            dh1_prev = dh1
        carry_ref[...] = later
        finish(0, dh1_prev)

        @pl.when(i == nt - 1)
        def _():
            through_norm = acc_ref[M_DSCM:M_DSCM + 1, :]
            acc_ref[M_DSCM:M_DSCM + 1, :] = through_norm * _vrow(vec_ref, V_PRE_MIX)
            acc_ref[M_DPREMIX:M_DPREMIX + 1, :] = through_norm * (1.0 + _vrow(vec_ref, V_SC_M))
            dwm_out[...] = dwm_ref[...].astype(BF16)
            dwp_out[...] = dwp_ref[...].astype(BF16)
            db_ref[...] = jnp.zeros_like(db_ref)
            for h in range(N_HEADS):
                db_ref[h:h + 1, :] = jnp.sum(dbz_ref[h].T, axis=0, keepdims=True)

    rev = lambda i: (nt - 1 - i, 0)
    halo_map = lambda i: (jnp.maximum((nt - 1 - i) * per - 1, 0), 0)
    hshape = (N_HEADS, HEAD, HEAD)
    return _call(
        body, name="mix_bwd", grid=(nt,),
        in_specs=[_tiled((ts, d), rev), _tiled((ts, d), rev), _tiled((ts, d), rev), _tiled((ts, n_proj), rev),
                  _tiled((POOL_HALO, n_proj), halo_map), _whole(vecs.shape), _resident(w_in_b.shape),
                  _resident(w_out_b.shape), _whole(sgu_g.shape), _whole(wm_b.shape), _whole(wmt_b.shape),
                  _whole(bsb.shape), _whole(wp_b.shape), _whole(ps.shape)],
        out_specs=[_tiled((ts, d), rev), _tiled((ts, n_proj), rev), _whole((8, d)), _whole(hshape), _whole(hshape),
                   _whole((8, HEAD)), _whole((8, HEAD)), _whole((8, A_WIDTH))],
        out_shape=[jax.ShapeDtypeStruct((s_len, d), F32), jax.ShapeDtypeStruct((s_len, n_proj), BF16),
                   jax.ShapeDtypeStruct((8, d), F32), jax.ShapeDtypeStruct(hshape, BF16),
                   jax.ShapeDtypeStruct(hshape, BF16), jax.ShapeDtypeStruct((8, HEAD), F32),
                   jax.ShapeDtypeStruct((8, HEAD), F32), jax.ShapeDtypeStruct((8, A_WIDTH), F32)],
        scratch_shapes=[pltpu.VMEM((POOL_HALO, A_WIDTH), F32), pltpu.VMEM(hshape, F32), pltpu.VMEM(hshape, F32),
                        pltpu.VMEM(hshape, F32)],
        args=(dmixed, dx2, x, proj, proj, vecs, w_in_b, w_out_b, sgu_g, wm_b, wmt_b, bsb, wp_b, ps))


def _wgrad(a, b, tn, ts, name, rider=None):
    s_len, m = a.shape
    n = b.shape[1]
    ts = min(ts, s_len)

    def body(a_ref, b_ref, o_ref):
        @pl.when(pl.program_id(1) == 0)
        def _():
            o_ref[...] = jnp.zeros_like(o_ref)

        o_ref[...] += _dot_tn(a_ref[...], b_ref[...])

    (g,), r_out = _call(
        body, name=name, grid=(n // tn, s_len // ts), rider=rider,
        in_specs=[pl.BlockSpec((ts, m), lambda j, s: (s, 0)), pl.BlockSpec((ts, tn), lambda j, s: (s, j))],
        out_specs=[pl.BlockSpec((m, tn), lambda j, s: (0, j))],
        out_shape=[jax.ShapeDtypeStruct((m, n), F32)], scratch_shapes=[], args=(a, b))
    return g, r_out


def _wgrad_pair(a1, b1, a2, b2, after, ts, name):
    s_len = a1.shape[0]
    ts = min(ts, s_len)
    shapes = [(a1.shape[1], b1.shape[1]), (a2.shape[1], b2.shape[1])]

    def body(a1_ref, b1_ref, a2_ref, b2_ref, after_ref, o1_ref, o2_ref):
        @pl.when(pl.program_id(0) == 0)
        def _():
            o1_ref[...] = jnp.zeros_like(o1_ref)
            o2_ref[...] = jnp.zeros_like(o2_ref)

        o1_ref[...] += _dot_tn(a1_ref[...], b1_ref[...])
        o2_ref[...] += _dot_tn(a2_ref[...], b2_ref[...])

    row = lambda s: (s, 0)
    return _call(
        body, name=name, grid=(s_len // ts,),
        in_specs=[pl.BlockSpec((ts, t.shape[1]), row) for t in (a1, b1, a2, b2)] + [_whole(after.shape)],
        out_specs=[_whole(sh) for sh in shapes],
        out_shape=[jax.ShapeDtypeStruct(sh, F32) for sh in shapes], scratch_shapes=[],
        args=(a1, b1, a2, b2, after))[0]


def _adamw_big(g, w, m, v, name):
    r, cdim = g.shape
    tr = r
    while tr * cdim * 4 > (3 << 19) and tr % 16 == 0:
        tr //= 2

    def body(g_ref, w_ref, m_ref, v_ref, go_ref, d_ref, nm_ref, nv_ref):
        grad = g_ref[...]
        delta, m2, v2 = _adamw_math(w_ref[0], grad, m_ref[0], v_ref[0])
        go_ref[0] = grad
        d_ref[0] = delta
        nm_ref[0] = m2
        nv_ref[0] = v2

    s3 = pl.BlockSpec((1, tr, cdim), lambda i: (0, i, 0))
    return pl.pallas_call(
        body, name=name, grid=(r // tr,),
        in_specs=[pl.BlockSpec((tr, cdim), lambda i: (i, 0)), s3, s3, s3],
        out_specs=[s3, s3, s3, s3],
        out_shape=[jax.ShapeDtypeStruct(w.shape, F32)] * 4,
        compiler_params=pltpu.CompilerParams(dimension_semantics=("parallel",), vmem_limit_bytes=VMEM_LIMIT_BYTES),
    )(g, w, m, v)


def _wada_update(sct, gm, w, m, v):
    _, r, cdim = w.shape
    tr = 256
    kp = sct.shape[1]

    def body(s_ref, g_ref, w_ref, m_ref, v_ref, gw_ref, d_ref, nm_ref, nv_ref):
        g = _dot(s_ref[...], g_ref[...])
        gw_ref[0] = g
        delta, m2, v2 = _adamw_math(w_ref[0], g, m_ref[0], v_ref[0])
        d_ref[0] = delta
        nm_ref[0] = m2
        nv_ref[0] = v2

    s3 = pl.BlockSpec((1, tr, cdim), lambda i: (0, i, 0))
    return pl.pallas_call(
        body, name="wada_update", grid=(r // tr,),
        in_specs=[pl.BlockSpec((tr, kp), lambda i: (i, 0)), _whole(gm.shape), s3, s3, s3],
        out_specs=[s3, s3, s3, s3],
        out_shape=[jax.ShapeDtypeStruct(w.shape, F32)] * 4,
        compiler_params=pltpu.CompilerParams(dimension_semantics=("parallel",), vmem_limit_bytes=VMEM_LIMIT_BYTES),
    )(sct, gm, w, m, v)


def _small_update(g1, g2, g2s, gwm, gwp, gbz, gsg, gps, params):
    names = ["b_ada", "pre_mix_g", "post_mix_g", "sgu_norm_g", "w_spatial", "b_spatial", "w_pool", "pool_scale",
             "pre_ffn_g", "post_ffn_g", "conv_w", "conv_b"]
    d = g1.shape[2]
    flat_in = [g1, g2, g2s, gwm, gwp, gbz, gsg, gps]
    n_g = len(flat_in)
    for nm in names:
        flat_in += list(params[nm])

    def body(*refs):
        g1_ref, g2_ref, g2s_ref, gwm_ref, gwp_ref, gbz_ref, gsg_ref, gps_ref = refs[:n_g]
        wmv = refs[n_g:n_g + 3 * len(names)]
        loss_ref = refs[n_g + 3 * len(names)]
        outs = refs[n_g + 3 * len(names) + 1:]

        def dsum(ref, idx):
            acc = ref[(0,) + idx].astype(F32)
            for dev in range(1, N_DEV):
                acc = acc + ref[(dev,) + idx].astype(F32)
            return acc

        def apply(pi, g, widx, oidx):
            w_ref, m_ref, v_ref = wmv[3 * pi:3 * pi + 3]
            g_ref, d_ref, nm_ref, nv_ref = outs[4 * pi:4 * pi + 4]
            delta, m2, v2 = _adamw_math(w_ref[widx], g, m_ref[widx], v_ref[widx])
            g_ref[oidx] = g
            d_ref[oidx] = delta
            nm_ref[oidx] = m2
            nv_ref[oidx] = v2

        def row1(base, r):
            return (slice(base + r, base + r + 1), slice(None))

        tot = dsum(g1_ref, row1(G1_F, F_LOSS))
        loss_ref[...] = jnp.zeros(loss_ref.shape, F32) + jnp.sum(tot) * (0.5 / d)
        mod_rows = [row1(G1_M, M_DSHM), row1(G1_M, M_DSCM), row1(G1_B, B_DGM), row1(G1_B, B_DSHF),
                    row1(G1_B, B_DSCF), row1(G1_F, F_DGF)]
        for j, rr in enumerate(mod_rows):
            cs = (slice(None), slice(j * d, (j + 1) * d))
            apply(0, dsum(g1_ref, rr), cs, cs)
        full = (slice(None), slice(None))
        apply(1, dsum(g1_ref, row1(G1_M, M_DPREMIX)), full, full)
        apply(2, dsum(g1_ref, row1(G1_B, B_DPOSTMIX)), full, full)
        apply(3, dsum(gsg_ref, (slice(0, N_HEADS), slice(None))), (0,), (0,))
        pos_i = lax.broadcasted_iota(jnp.int32, (HEAD, HEAD), 0)
        pos_j = lax.broadcasted_iota(jnp.int32, (HEAD, HEAD), 1)
        causal = (pos_j // CHUNK) <= (pos_i // CHUNK)
        for h in range(N_HEADS):
            blk = (slice(h * HEAD, (h + 1) * HEAD), slice(None))
            apply(4, jnp.where(causal, dsum(gwm_ref, blk), 0.0), (0, h), (0, h))
            apply(5, dsum(gbz_ref, (slice(h, h + 1), slice(None))), (0, slice(h, h + 1)), (0, slice(h, h + 1)))
            apply(6, dsum(gwp_ref, blk), (0, h), (0, h))
        apply(7, dsum(gps_ref, (slice(0, 1), slice(None))), full, full)
        apply(8, dsum(g1_ref, row1(G1_B, B_DPREFFN)), full, full)
        apply(9, dsum(g1_ref, row1(G1_F, F_DPOSTFFN)), full, full)
        apply(10, dsum(g2s_ref, (slice(C_DCW, C_DCW + 3), slice(None))), (0,), (0,))
        apply(11, dsum(g2_ref, (slice(C_DCB, C_DCB + 1), slice(None))), full, full)

    out_shape = [jax.ShapeDtypeStruct((8, HEAD), F32)]
    for nm in names:
        out_shape += [jax.ShapeDtypeStruct(params[nm][0].shape, F32)] * 4
    res = pl.pallas_call(
        body, name="small_update", out_shape=out_shape,
        compiler_params=pltpu.CompilerParams(vmem_limit_bytes=VMEM_LIMIT_BYTES),
    )(*flat_in)
    out = {nm: tuple(res[1 + 4 * i:5 + 4 * i]) for i, nm in enumerate(names)}
    return res[0], out


def kernel(x, c, w_ada, b_ada, pre_mix_g, post_mix_g, w_in, sgu_norm_g, w_spatial, b_spatial, w_pool, pool_scale, w_out, pre_ffn_g, post_ffn_g, w_up, conv_w, conv_b, w_down, loss_target, m_w_ada, m_b_ada, m_pre_mix_g, m_post_mix_g, m_w_in, m_sgu_norm_g, m_w_spatial, m_b_spatial, m_w_pool, m_pool_scale, m_w_out, m_pre_ffn_g, m_post_ffn_g, m_w_up, m_conv_w, m_conv_b, m_w_down, v_w_ada, v_b_ada, v_pre_mix_g, v_post_mix_g, v_w_in, v_sgu_norm_g, v_w_spatial, v_b_spatial, v_w_pool, v_pool_scale, v_w_out, v_pre_ffn_g, v_post_ffn_g, v_w_up, v_conv_w, v_conv_b, v_w_down):
    xi, yi, ci = _mesh_pos()
    k_me = 2 * xi + yi
    dev = 2 * k_me + ci
    s_len, d = x.shape[1], x.shape[2]
    x2d = x[0]
    tgt = loss_target[0]
    ff2 = conv_b.shape[1]
    n_ada = w_ada.shape[2]
    n_cw = conv_w.shape[2]

    k_idx = k_me.reshape(1).astype(jnp.int32)
    mix_flags = (True, False)
    (w_in_s, w_out_s), (w_in_f, w_out_f) = _cast_bf16([w_in[0], w_out[0]], mix_flags, k_idx, "cast_weights_mix")

    def place(t):
        return lax.dynamic_update_slice(jnp.zeros((N_DEV * t.shape[0], t.shape[1]), t.dtype), t,
                                        (dev * t.shape[0], 0))

    cw_blk = jnp.concatenate([conv_w[0], jnp.zeros((5, n_cw), F32)], axis=0)
    fly_c = _split_begin([place(c.reshape(8, d // 8)), place(cw_blk)], 6, _small_ici_copies, k_idx,
                         "c_gather_begin")
    fly_mix = _gather_begin([w_in_s, w_out_s], [w_in_f, w_out_f], mix_flags, fly_c[3], "gather_begin_mix")
    c_all, cw_all = _small_finish(_split_end(fly_c, _small_ici_copies, fly_mix[3], "c_gather_end"),
                                  "c_gather_finish")
    (w_up_s, w_down_s), (w_up_f, w_down_f) = _cast_bf16([w_up[0], w_down[0]], mix_flags, k_idx, "cast_weights_ffn",
                                                        after=fly_mix[3])
    c_all = c_all.reshape(N_DEV, 8, d // 8).reshape(N_DEV, d)
    cw_full = jnp.concatenate([cw_all[16 * k:16 * k + 8] for k in range(N_CHIP)], axis=1)
    cvec = jnp.concatenate([cw_full[0:3], conv_b, jnp.zeros((4, ff2), F32)], axis=0)
    b_shard = lax.dynamic_slice_in_dim(b_ada, k_me * n_ada, n_ada, axis=1)
    mod_k, sc_all = _mod_shard(c_all, w_ada[0], b_shard, w_down_s)
    (mod_g,) = _run_rider(_allgather_rider([mod_k]), "gather_mod")
    mod_all = jnp.concatenate([mod_g[16 * k:16 * k + 8] for k in range(N_CHIP)], axis=1)
    mod_me = lax.dynamic_slice_in_dim(mod_all, dev, 1, axis=0).reshape(6, d)
    vecs = jnp.concatenate([mod_me, pre_mix_g, post_mix_g, pre_ffn_g, post_ffn_g, jnp.zeros((6, d), F32)], axis=0)

    fly_ffn = _gather_begin([w_up_s, w_down_s], [w_up_f, w_down_f], mix_flags, mod_g, "gather_begin_ffn")
    w_in_b, w_out_b = _gather_finish(_gather_end(fly_mix, mix_flags, fly_ffn[3], "gather_end_mix"), mix_flags,
                                     "gather_finish_mix")

    pos = jnp.arange(HEAD)
    causal = (pos[None, :] // CHUNK) <= (pos[:, None] // CHUNK)
    wm = jnp.where(causal[None], w_spatial[0], 0.0)
    wm_b = wm.astype(BF16)
    wmt_b = jnp.swapaxes(wm, 1, 2).astype(BF16)
    bsb = jnp.broadcast_to(b_spatial[0][:, :, None], (N_HEADS, HEAD, HEAD))
    wp_b = w_pool[0].astype(BF16)
    sgu_g = jnp.concatenate([sgu_norm_g[0], jnp.zeros((4, HEAD), F32)], axis=0)
    ps = jnp.concatenate([pool_scale, jnp.zeros((7, A_WIDTH), F32)], axis=0)

    pmats = _perm_mats(FFN_TS)
    h1, proj, cat, mixed, x2, h2p = _mix_fwd(x2d, vecs, w_in_b, w_out_b, sgu_g, wm_b, bsb, wp_b, ps, pmats,
                                             ts=MIX_TS)[0]
    w_up_b, w_down_b = _gather_finish(_gather_end(fly_ffn, mix_flags, h2p, "gather_end_ffn"), mix_flags,
                                      "gather_finish_ffn")
    up, yv, act, dy, dfp, acc_f = _ffn_fwd(h2p, x2, tgt, w_up_b, w_down_b, cvec, vecs, pmats, ts=FFN_TS)

    c_idx = ci.reshape(1).astype(jnp.int32)
    g_w_down, _ = _wgrad(act, dfp, d, WGRAD_TS_WIDE, "wgrad_down")
    (dup, dx2, dmixed, acc_c, acc_b), (land_down,) = _ffn_bwd(
        dfp, up, yv, x2, dy, mixed, w_up_b, w_down_b, cvec, vecs, pmats, ts=FFN_TS,
        rider=_sibling_rider([g_w_down], (False,)))
    (part_down,) = _sum_with_sibling([g_w_down], [land_down], (False,), c_idx, "pair_sum_down")
    g_w_up, (chips_down,) = _wgrad(h2p, dup, ff2 // 2, WGRAD_TS, "wgrad_up", rider=_chips_rider([part_down]))
    fly_up = _sibling_begin(g_w_up, True, chips_down, "sibling_begin_up")
    gx, dproj, acc_m, dwm, dwp, dbz, dsg, dps = _mix_bwd(
        dmixed, dx2, x2d, proj, vecs + fly_up[3][0:1, 0:1], w_in_b, w_out_b, sgu_g, wm_b, wmt_b, bsb, wp_b, ps,
        ts=MIX_TS)[0]
    g_w_up, land_up = _sibling_end(fly_up, True, dproj, "sibling_end_up")
    (part_up,) = _sum_with_sibling([g_w_up], [land_up], (True,), c_idx, "pair_sum_up")
    fly_chips_up = _chips_begin([part_up], c_idx, "chips_begin_up")
    g1 = jnp.concatenate([acc_f, acc_b, acc_m], axis=0)
    hflat = (N_HEADS * HEAD, HEAD)
    small_bufs = [place(t) for t in (g1, acc_c, dwm.reshape(hflat), dwp.reshape(hflat), dbz, dsg, dps)]
    fly_small = _split_begin(small_bufs, 3 * len(small_bufs), _small_ici_copies, fly_chips_up[3],
                             "small_gather_begin")
    g_w_out, g_w_in = _wgrad_pair(cat, dmixed, h1, dproj, fly_small[3], WGRAD_TS_WIDE, "wgrad_mix")
    land_mix = _run_rider(_sibling_rider([g_w_in, g_w_out], (True, False)), "reduce_to_sibling")
    parts_mix = _sum_with_sibling([g_w_in, g_w_out], land_mix, (True, False), c_idx, "pair_sum_mix")
    (part_up,), (chips_up,) = _chips_end(fly_chips_up, parts_mix[0], "chips_end_up")
    fly_chips_mix = _chips_begin(parts_mix, chips_up, "chips_begin_mix")

    def adamw_of(names, reduced):
        res = {}
        for nm, red in zip(names, reduced):
            w, m, v = big_wmv[nm]
            g = red.reshape(w.shape[1], w.shape[2])
            res[nm] = tuple(_adamw_big(g, w, m, v, "adamw_" + nm))
        return res

    big_wmv = {"w_in": (w_in, m_w_in, v_w_in), "w_out": (w_out, m_w_out, v_w_out),
               "w_up": (w_up, m_w_up, v_w_up), "w_down": (w_down, m_w_down, v_w_down)}
    big = adamw_of(("w_up", "w_down"), _sum_chips_and_share([chips_up, chips_down], [part_up, part_down],
                                                           fly_chips_mix[3], "sum_share_ffn"))

    gathered = _small_finish(_split_end(fly_small, _small_ici_copies, big["w_down"][1], "small_gather_end"),
                             "small_gather_finish")
    g1a, g2a, gwm, gwp, gbz, gsg, gps = [t.reshape((N_DEV, t.shape[0] // N_DEV, t.shape[1])) for t in gathered]
    g2s = lax.dynamic_slice_in_dim(g2a, k_me * n_cw, n_cw, axis=2)
    params = {
        "b_ada": (b_ada, m_b_ada, v_b_ada), "pre_mix_g": (pre_mix_g, m_pre_mix_g, v_pre_mix_g),
        "post_mix_g": (post_mix_g, m_post_mix_g, v_post_mix_g),
        "sgu_norm_g": (sgu_norm_g, m_sgu_norm_g, v_sgu_norm_g), "w_spatial": (w_spatial, m_w_spatial, v_w_spatial),
        "b_spatial": (b_spatial, m_b_spatial, v_b_spatial), "w_pool": (w_pool, m_w_pool, v_w_pool),
        "pool_scale": (pool_scale, m_pool_scale, v_pool_scale), "pre_ffn_g": (pre_ffn_g, m_pre_ffn_g, v_pre_ffn_g),
        "post_ffn_g": (post_ffn_g, m_post_ffn_g, v_post_ffn_g), "conv_w": (conv_w, m_conv_w, v_conv_w),
        "conv_b": (conv_b, m_conv_b, v_conv_b),
    }
    loss_slab, small = _small_update(g1a, g2a, g2s, gwm, gwp, gbz, gsg, gps, params)

    gmod_all = jnp.concatenate(
        [g1a[:, G1_M + M_DSHM], g1a[:, G1_M + M_DSCM], g1a[:, G1_B + B_DGM], g1a[:, G1_B + B_DSHF],
         g1a[:, G1_B + B_DSCF], g1a[:, G1_F + F_DGF]], axis=1)
    gm = lax.dynamic_slice_in_dim(gmod_all, k_me * n_ada, n_ada, axis=1)
    gm = jnp.concatenate([gm, jnp.zeros((HEAD - N_DEV, n_ada), F32)], axis=0)
    sct = jnp.concatenate([sc_all.T, jnp.zeros((d, HEAD - N_DEV), F32)], axis=1)
    ada = tuple(_wada_update(sct, gm, w_ada, m_w_ada, v_w_ada))

    parts_mix, chips_mix = _chips_end(fly_chips_mix, ada[1], "chips_end_mix")
    big.update(adamw_of(("w_in", "w_out"), _sum_chips_and_share(chips_mix, parts_mix, loss_slab, "sum_share_mix")))

    everything = dict(small)
    everything.update(big)
    everything["w_ada"] = ada
    order = ["w_ada", "b_ada", "pre_mix_g", "post_mix_g", "w_in", "sgu_norm_g", "w_spatial", "b_spatial", "w_pool",
             "pool_scale", "w_out", "pre_ffn_g", "post_ffn_g", "w_up", "conv_w", "conv_b", "w_down"]
    outs = [loss_slab[0, 0], gx.reshape(x.shape)]
    for j in range(4):
        outs += [everything[nm][j] for nm in order]
    return tuple(outs)
```

```python
import functools

import jax
import jax.numpy as jnp
from jax import lax
from jax.experimental import pallas as pl
from jax.experimental.pallas import tpu as pltpu

F32 = jnp.float32
BF16 = jnp.bfloat16
MESH = pl.DeviceIdType.MESH

EPS = 1e-6
HEAD = 128
N_HEADS = 4
A_WIDTH = N_HEADS * HEAD
CHUNK = 64
POOL_WINDOWS = (2, 4, 8, 16)
POOL_HALO = 16
SUBLANES = 8
CONV_KEEP = 2 * SUBLANES
FFN_TS = 256
MIX_TS = 512
WGRAD_TS = 2048
WGRAD_TS_WIDE = 1024

ADAM_LR = 0.001
ADAM_B1 = 0.9
ADAM_B2 = 0.999
ADAM_EPS = 1e-08
ADAM_WD = 0.01
ADAM_STEP = 10

VMEM_LIMIT_BYTES = 58 * 1024 * 1024
N_DEV = 8
N_CHIP = 4


def _dot(a, b):
    return jnp.dot(a, b, preferred_element_type=F32)


def _dot_nt(a, b):
    return lax.dot_general(a, b, (((1,), (1,)), ((), ())), preferred_element_type=F32)


def _dot_tn(a, b):
    return lax.dot_general(a, b, (((0,), (0,)), ((), ())), preferred_element_type=F32)


GELU_C0 = 0.7978845608028654
GELU_C1 = GELU_C0 * 0.044715


def _gelu(x):
    return x * (0.5 + 0.5 * jnp.tanh(x * (GELU_C0 + GELU_C1 * (x * x))))


def _gelu_and_grad(x):
    x2 = x * x
    t = jnp.tanh(x * (GELU_C0 + GELU_C1 * x2))
    half = 0.5 + 0.5 * t
    grad = half + (x * (half * (1.0 - t))) * (GELU_C0 + (3.0 * GELU_C1) * x2)
    return x * half, grad


def _sigmoid(x):
    return 1.0 / (1.0 + jnp.exp(-x))


def _rms(x):
    return lax.rsqrt(jnp.mean(x * x, axis=-1, keepdims=True) + EPS)


def _colsum(x):
    return jnp.sum(x, axis=0, keepdims=True)


def _rowmean(x):
    return jnp.mean(x, axis=-1, keepdims=True)


def _tiled(shape, index_map):
    return pl.BlockSpec(shape, index_map)


def _resident(shape):
    nd = len(shape)
    return pl.BlockSpec(shape, lambda *_: (0,) * nd, pipeline_mode=pl.Buffered(1))


def _whole(shape):
    nd = len(shape)
    return pl.BlockSpec(shape, lambda *_: (0,) * nd)


def _seq_params():
    return pltpu.CompilerParams(dimension_semantics=("arbitrary",), vmem_limit_bytes=VMEM_LIMIT_BYTES)


def _ff_chunks(f, width=768):
    out, o = [], 0
    while o < f:
        w = min(width, f - o)
        out.append((o, w))
        o += w
    return out


def _pool_fwd(p, halo, g, t_glob):
    ext = jnp.concatenate([halo, p], axis=0)
    s = ext
    for step in range(g + 1):
        s = s + pltpu.roll(s, 1 << step, 0)
    cnt = jnp.minimum(t_glob + 1, POOL_WINDOWS[g]).astype(F32)
    return s[POOL_HALO:] / cnt - p


def _adamw_math(w, g, m, v):
    m = ADAM_B1 * m + (1.0 - ADAM_B1) * g
    v = ADAM_B2 * v + (1.0 - ADAM_B2) * (g * g)
    m_hat = m / (1.0 - ADAM_B1 ** ADAM_STEP)
    v_hat = v / (1.0 - ADAM_B2 ** ADAM_STEP)
    delta = -ADAM_LR * (m_hat / (jnp.sqrt(v_hat) + ADAM_EPS) + ADAM_WD * w)
    return delta, m, v


def _mesh_pos():
    return lax.axis_index("x"), lax.axis_index("y"), lax.axis_index("c")


class _Rider:
    def __init__(self, inputs, out_shape, sems, start, finish):
        self.inputs, self.out_shape, self.sems = list(inputs), list(out_shape), list(sems)
        self.start, self.finish = start, finish


def _call(body, *, name, grid, in_specs, out_specs, out_shape, scratch_shapes, args, rider=None):
    params = pltpu.CompilerParams(dimension_semantics=("arbitrary",) * len(grid), vmem_limit_bytes=VMEM_LIMIT_BYTES)
    if rider is None:
        res = pl.pallas_call(body, name=name, grid=grid, in_specs=in_specs, out_specs=out_specs, out_shape=out_shape,
                             scratch_shapes=scratch_shapes, compiler_params=params)(*args)
        return tuple(res), ()
    cuts = [len(in_specs), len(rider.inputs), len(out_specs), len(rider.out_shape), len(scratch_shapes),
            len(rider.sems)]

    def hosted(*refs):
        groups, a = [], 0
        for cnt in cuts:
            groups.append(refs[a:a + cnt])
            a += cnt
        ins, r_in, outs, r_out, scr, r_sem = groups
        first = functools.reduce(jnp.logical_and, [pl.program_id(k) == 0 for k in range(len(grid))])
        last = functools.reduce(jnp.logical_and, [pl.program_id(k) == grid[k] - 1 for k in range(len(grid))])

        @pl.when(first)
        def _():
            rider.start(r_in, r_out, r_sem)

        body(*ins, *outs, *scr)

        @pl.when(last)
        def _():
            rider.finish(r_in, r_out, r_sem)

    anyspec = pl.BlockSpec(memory_space=pl.ANY)
    res = pl.pallas_call(
        hosted, name=name, grid=grid,
        in_specs=list(in_specs) + [anyspec] * cuts[1], out_specs=list(out_specs) + [anyspec] * cuts[3],
        out_shape=list(out_shape) + rider.out_shape, scratch_shapes=list(scratch_shapes) + rider.sems,
        compiler_params=params)(*args, *rider.inputs)
    return tuple(res[:cuts[2]]), tuple(res[cuts[2]:])


def _run_rider(rider, name):
    n_in, n_out = len(rider.inputs), len(rider.out_shape)

    def body(*refs):
        r_in, r_out, r_sem = refs[:n_in], refs[n_in:n_in + n_out], refs[n_in + n_out:]
        rider.start(r_in, r_out, r_sem)
        rider.finish(r_in, r_out, r_sem)

    anyspec = pl.BlockSpec(memory_space=pl.ANY)
    return pl.pallas_call(body, name=name, out_shape=rider.out_shape, in_specs=[anyspec] * n_in,
                          out_specs=[anyspec] * n_out, scratch_shapes=rider.sems)(*rider.inputs)


def _allgather_rider(arrs):
    n = len(arrs)

    def plan(ins, outs, sems):
        send_sems, recv_sems, local_sems = sems
        x, y, c = _mesh_pos()
        me, sibling = (x, y, c), (x, y, 1 - c)
        chips = [(1 - x, y), (x, 1 - y), (1 - x, 1 - y)]

        def rows(a, px, py, pc):
            r = ins[a].shape[0]
            return outs[a].at[pl.ds(pl.multiple_of((4 * px + 2 * py + pc) * r, 8), r), :]

        def copy(a, k, block, to, src=None):
            return pltpu.make_async_remote_copy(
                src_ref=rows(a, *block) if src is None else src, dst_ref=rows(a, *block),
                send_sem=send_sems.at[a * 7 + k], recv_sem=recv_sems.at[a * 7 + k],
                device_id=to, device_id_type=MESH)

        local = [pltpu.make_async_copy(ins[a], rows(a, *me), local_sems.at[a]) for a in range(n)]
        first = []
        for a in range(n):
            first.append(copy(a, 0, me, sibling, src=ins[a]))
            first += [copy(a, 1 + j, me, (*chip, c), src=ins[a]) for j, chip in enumerate(chips)]
        return c, me, sibling, chips, copy, local, first

    def start(ins, outs, sems):
        *_, local, first = plan(ins, outs, sems)
        for cp in local + first:
            cp.start()

    def finish(ins, outs, sems):
        c, me, sibling, chips, copy, local, first = plan(ins, outs, sems)
        passed = []
        for a in range(n):
            for j, chip in enumerate(chips):
                copy(a, 1 + j, (*chip, c), me).wait_recv()
                fwd = copy(a, 4 + j, (*chip, c), sibling)
                fwd.start()
                passed.append(fwd)
        for a in range(n):
            copy(a, 0, sibling, me).wait_recv()
            for j, chip in enumerate(chips):
                copy(a, 4 + j, (*chip, 1 - c), me).wait_recv()
        for cp in first + passed:
            cp.wait_send()
        for mine in local:
            mine.wait()

    return _Rider(arrs, [jax.ShapeDtypeStruct((N_DEV * a.shape[0], a.shape[1]), a.dtype) for a in arrs],
                  [pltpu.SemaphoreType.DMA((7 * n,)), pltpu.SemaphoreType.DMA((7 * n,)),
                   pltpu.SemaphoreType.DMA((n,))], start, finish)


def _piece(ref, col_sharded, k, h):
    m, n = ref.shape
    if col_sharded:
        mh, nc = m // 2, n // N_CHIP
        return ref.at[pl.ds(pl.multiple_of(h * mh, 16), mh), pl.ds(pl.multiple_of(k * nc, 128), nc)]
    rp = m // (2 * N_CHIP)
    return ref.at[pl.ds(pl.multiple_of((2 * k + h) * rp, 16), rp), :]


def _piece_shape(shape, col_sharded):
    m, n = shape
    return (m // 2, n // N_CHIP) if col_sharded else (m // (2 * N_CHIP), n)


def _cast_bf16(arrs, col_flags, k_idx, name, after=None):
    n = len(arrs)
    extra = [] if after is None else [after]

    def body(k_ref, *refs):
        outs = refs[n + len(extra):]
        for a in range(n):
            val = refs[a][...].astype(BF16)
            outs[a][...] = val
            outs[n + a][...] = val

    whole = [pl.BlockSpec(a.shape, lambda i, k_ref: (0, 0)) for a in arrs]
    window = [pl.BlockSpec(a.shape, (lambda i, k_ref: (0, k_ref[0])) if col else (lambda i, k_ref: (k_ref[0], 0)))
              for a, col in zip(arrs, col_flags)]
    res = pl.pallas_call(
        body, name=name,
        grid_spec=pltpu.PrefetchScalarGridSpec(
            num_scalar_prefetch=1, grid=(1,),
            in_specs=whole + [pl.BlockSpec(t.shape, lambda i, k_ref: (0, 0)) for t in extra],
            out_specs=whole + window),
        out_shape=[jax.ShapeDtypeStruct(a.shape, BF16) for a in arrs]
        + [jax.ShapeDtypeStruct(fs, BF16) for fs in _full_shapes(arrs, col_flags)],
        compiler_params=pltpu.CompilerParams(vmem_limit_bytes=VMEM_LIMIT_BYTES))(k_idx, *arrs, *extra)
    return list(res[:n]), list(res[n:])


def _full_shapes(shards, col_flags):
    return [(s.shape[0], s.shape[1] * N_CHIP) if col else (s.shape[0] * N_CHIP, s.shape[1])
            for s, col in zip(shards, col_flags)]


def _ici_copies(shard_refs, full_refs, send_sems, recv_sems, col_flags):
    x, y, c = _mesh_pos()
    k_me = 2 * x + y
    copies = []
    for a, (s_ref, f_ref) in enumerate(zip(shard_refs, full_refs)):
        rows = s_ref.shape[0] // 2
        src = s_ref.at[pl.ds(pl.multiple_of(c * rows, 16), rows), :]
        for j, chip in enumerate([(1 - x, y), (x, 1 - y), (1 - x, 1 - y)]):
            copies.append(pltpu.make_async_remote_copy(
                src_ref=src, dst_ref=_piece(f_ref, col_flags[a], k_me, c),
                send_sem=send_sems.at[a * 3 + j], recv_sem=recv_sems.at[a * 3 + j],
                device_id=(*chip, c), device_id_type=MESH))
    return copies


def _split_begin(bufs, n_sems, make_copies, after, name):
    n = len(bufs)
    hbm = pl.BlockSpec(memory_space=pltpu.HBM)
    sem = pl.BlockSpec(memory_space=pltpu.SEMAPHORE)

    def body(*refs):
        for cp in make_copies(refs[:n], refs[n + 1], refs[n + 2]):
            cp.start()
        refs[-1][...] = jnp.zeros_like(refs[-1])

    args = [pltpu.with_memory_space_constraint(t, pltpu.HBM) for t in bufs]
    res = pl.pallas_call(
        body, name=name,
        out_shape=[pltpu.SemaphoreType.DMA((n_sems,)), pltpu.SemaphoreType.DMA((n_sems,))]
        + [pltpu.HBM(t.shape, t.dtype) for t in args] + [jax.ShapeDtypeStruct((8, HEAD), F32)],
        in_specs=[hbm] * n + [pl.BlockSpec(memory_space=pl.ANY)],
        out_specs=[sem, sem] + [hbm] * n + [pl.BlockSpec(memory_space=pltpu.VMEM)],
        input_output_aliases={i: 2 + i for i in range(n)},
        compiler_params=pltpu.CompilerParams(has_side_effects=pltpu.SideEffectType.DATAFLOW_SIDE_EFFECTING),
    )(*args, after)
    return res[0], res[1], list(res[2:2 + n]), res[-1]


def _split_end(handle, make_copies, after, name):
    send_sems, recv_sems, bufs, _ = handle
    n = len(bufs)
    hbm = pl.BlockSpec(memory_space=pltpu.HBM)
    sem = pl.BlockSpec(memory_space=pltpu.SEMAPHORE)

    def body(*refs):
        for cp in make_copies(refs[:n], refs[n], refs[n + 1]):
            cp.wait_send()
            cp.wait_recv()

    res = pl.pallas_call(
        body, name=name,
        out_shape=[pltpu.HBM(t.shape, t.dtype) for t in bufs],
        in_specs=[hbm] * n + [sem, sem, pl.BlockSpec(memory_space=pl.ANY)],
        out_specs=[hbm] * n,
        input_output_aliases={i: i for i in range(n)},
        compiler_params=pltpu.CompilerParams(has_side_effects=pltpu.SideEffectType.DATAFLOW_SIDE_EFFECTING),
    )(*bufs, send_sems, recv_sems, after)
    return list(res)


def _gather_copies(n, col_flags):
    return lambda refs, send_sems, recv_sems: _ici_copies(refs[:n], refs[n:], send_sems, recv_sems, col_flags)


def _gather_begin(shards, fulls, col_flags, after, name):
    n = len(shards)
    return _split_begin(list(shards) + list(fulls), 3 * n, _gather_copies(n, col_flags), after, name)


def _gather_end(handle, col_flags, after, name):
    n = len(handle[2]) // 2
    return _split_end(handle, _gather_copies(n, col_flags), after, name)[n:]


def _sibling_copies(col_flag):
    def make(refs, send_sems, recv_sems):
        grad_ref, land_ref = refs
        x, y, c = _mesh_pos()
        return [pltpu.make_async_remote_copy(
            src_ref=_piece(grad_ref, col_flag, k, 1 - c), dst_ref=land_ref.at[k],
            send_sem=send_sems.at[k], recv_sem=recv_sems.at[k],
            device_id=(x, y, 1 - c), device_id_type=MESH) for k in range(N_CHIP)]
    return make


def _chips_copies(n):
    def make(refs, send_sems, recv_sems):
        parts, landed = refs[:n], refs[n:]
        x, y, c = _mesh_pos()
        k_me = 2 * x + y
        copies = []
        for a in range(n):
            for j, chip in enumerate([(1 - x, y), (x, 1 - y), (1 - x, 1 - y)]):
                copies.append(pltpu.make_async_remote_copy(
                    src_ref=parts[a].at[2 * chip[0] + chip[1]], dst_ref=landed[a].at[k_me],
                    send_sem=send_sems.at[a * 3 + j], recv_sem=recv_sems.at[a * 3 + j],
                    device_id=(*chip, c), device_id_type=MESH))
        return copies
    return make


def _chips_begin(parts, after, name):
    landed = [lax.empty(p.shape, p.dtype) for p in parts]
    return _split_begin(list(parts) + landed, 3 * len(parts), _chips_copies(len(parts)), after, name)


def _chips_end(handle, after, name):
    n = len(handle[2]) // 2
    res = _split_end(handle, _chips_copies(n), after, name)
    return res[:n], res[n:]


def _small_rows(buf_ref, px, py, pc):
    r = buf_ref.shape[0] // N_DEV
    return buf_ref.at[pl.ds(pl.multiple_of((4 * px + 2 * py + pc) * r, 8), r), :]


def _small_ici_copies(refs, send_sems, recv_sems):
    x, y, c = _mesh_pos()
    copies = []
    for a, buf in enumerate(refs):
        mine = _small_rows(buf, x, y, c)
        for j, chip in enumerate([(1 - x, y), (x, 1 - y), (1 - x, 1 - y)]):
            copies.append(pltpu.make_async_remote_copy(
                src_ref=mine, dst_ref=mine, send_sem=send_sems.at[a * 3 + j], recv_sem=recv_sems.at[a * 3 + j],
                device_id=(*chip, c), device_id_type=MESH))
    return copies


def _small_finish(bufs, name):
    n = len(bufs)

    def body(*refs):
        buf_refs = refs[n:2 * n]
        send_sems, recv_sems = refs[2 * n:]
        x, y, c = _mesh_pos()
        owners = [(x, y), (1 - x, y), (x, 1 - y), (1 - x, 1 - y)]
        passed, arriving = [], []
        for a in range(n):
            for j, (px, py) in enumerate(owners):
                for pc, group in ((c, passed), (1 - c, arriving)):
                    rows = _small_rows(buf_refs[a], px, py, pc)
                    group.append(pltpu.make_async_remote_copy(
                        src_ref=rows, dst_ref=rows, send_sem=send_sems.at[a * 4 + j],
                        recv_sem=recv_sems.at[a * 4 + j], device_id=(x, y, 1 - c), device_id_type=MESH))
        for cp in passed:
            cp.start()
        for cp in arriving:
            cp.wait_recv()
        for cp in passed:
            cp.wait_send()

    anyspec = pl.BlockSpec(memory_space=pl.ANY)
    return pl.pallas_call(
        body, name=name, out_shape=[jax.ShapeDtypeStruct(b.shape, b.dtype) for b in bufs],
        in_specs=[anyspec] * n, out_specs=[anyspec] * n, input_output_aliases={a: a for a in range(n)},
        scratch_shapes=[pltpu.SemaphoreType.DMA((4 * n,)), pltpu.SemaphoreType.DMA((4 * n,))],
    )(*bufs)


def _sibling_begin(grad, col_flag, after, name):
    land = lax.empty((N_CHIP,) + _piece_shape(grad.shape, col_flag), grad.dtype)
    return _split_begin([grad, land], N_CHIP, _sibling_copies(col_flag), after, name)


def _sibling_end(handle, col_flag, after, name):
    return _split_end(handle, _sibling_copies(col_flag), after, name)


def _gather_finish(fulls, col_flags, name):
    n = len(fulls)

    def body(*refs):
        full_refs = refs[n:2 * n]
        send_sems, recv_sems = refs[2 * n:]
        x, y, c = _mesh_pos()
        passed, arriving = [], []
        for a in range(n):
            for j, chip in enumerate([(1 - x, y), (x, 1 - y), (1 - x, 1 - y)]):
                k_from = 2 * chip[0] + chip[1]
                for h, group in ((c, passed), (1 - c, arriving)):
                    win = _piece(full_refs[a], col_flags[a], k_from, h)
                    group.append(pltpu.make_async_remote_copy(
                        src_ref=win, dst_ref=win, send_sem=send_sems.at[a * 3 + j],
                        recv_sem=recv_sems.at[a * 3 + j], device_id=(x, y, 1 - c), device_id_type=MESH))
        for cp in passed:
            cp.start()
        for cp in arriving:
            cp.wait_recv()
        for cp in passed:
            cp.wait_send()

    anyspec = pl.BlockSpec(memory_space=pl.ANY)
    return pl.pallas_call(
        body, name=name,
        out_shape=[jax.ShapeDtypeStruct(f.shape, f.dtype) for f in fulls],
        in_specs=[anyspec] * n, out_specs=[anyspec] * n,
        input_output_aliases={a: a for a in range(n)},
        scratch_shapes=[pltpu.SemaphoreType.DMA((3 * n,)), pltpu.SemaphoreType.DMA((3 * n,))],
    )(*fulls)


def _sibling_rider(grads, col_flags):
    n = len(grads)
    pshapes = [_piece_shape(g.shape, col) for g, col in zip(grads, col_flags)]

    def copies(ins, outs, sems):
        send_sems, recv_sems = sems
        x, y, c = _mesh_pos()
        return [pltpu.make_async_remote_copy(
            src_ref=_piece(ins[a], col_flags[a], k, 1 - c), dst_ref=outs[a].at[k],
            send_sem=send_sems.at[a * N_CHIP + k], recv_sem=recv_sems.at[a * N_CHIP + k],
            device_id=(x, y, 1 - c), device_id_type=MESH) for a in range(n) for k in range(N_CHIP)]

    def start(ins, outs, sems):
        for cp in copies(ins, outs, sems):
            cp.start()

    def finish(ins, outs, sems):
        cps = copies(ins, outs, sems)
        for cp in cps:
            cp.wait_recv()
        for cp in cps:
            cp.wait_send()

    return _Rider(grads, [jax.ShapeDtypeStruct((N_CHIP,) + ps, g.dtype) for ps, g in zip(pshapes, grads)],
                  [pltpu.SemaphoreType.DMA((N_CHIP * n,)), pltpu.SemaphoreType.DMA((N_CHIP * n,))], start, finish)


def _sum_with_sibling(grads, landed, col_flags, c_idx, name):
    n = len(grads)
    pshapes = [_piece_shape(g.shape, col) for g, col in zip(grads, col_flags)]

    def body(c_ref, *refs):
        ins, lands, outs = refs[:n], refs[n:2 * n], refs[2 * n:]
        for a in range(n):
            outs[a][0] = (ins[a][...] + lands[a][0]).astype(BF16)

    in_specs = []
    for ps, col in zip(pshapes, col_flags):
        if col:
            in_specs.append(pl.BlockSpec(ps, lambda k, c_ref: (c_ref[0], k)))
        else:
            in_specs.append(pl.BlockSpec(ps, lambda k, c_ref: (2 * k + c_ref[0], 0)))
    land_specs = [pl.BlockSpec((1,) + ps, lambda k, c_ref: (k, 0, 0)) for ps in pshapes]
    return pl.pallas_call(
        body, name=name,
        grid_spec=pltpu.PrefetchScalarGridSpec(
            num_scalar_prefetch=1, grid=(N_CHIP,),
            in_specs=in_specs + land_specs, out_specs=land_specs),
        out_shape=[jax.ShapeDtypeStruct((N_CHIP,) + ps, BF16) for ps in pshapes],
        compiler_params=pltpu.CompilerParams(dimension_semantics=("arbitrary",), vmem_limit_bytes=VMEM_LIMIT_BYTES),
    )(c_idx, *grads, *landed)


def _chips_rider(parts):
    n = len(parts)

    def plan(ins, outs, sems, arriving):
        send_sems, recv_sems = sems
        x, y, c = _mesh_pos()
        k_me = 2 * x + y
        copies = []
        for a in range(n):
            for j, chip in enumerate([(1 - x, y), (x, 1 - y), (1 - x, 1 - y)]):
                k_peer = 2 * chip[0] + chip[1]
                copies.append(pltpu.make_async_remote_copy(
                    src_ref=ins[a].at[k_peer], dst_ref=outs[a].at[k_peer if arriving else k_me],
                    send_sem=send_sems.at[a * 3 + j], recv_sem=recv_sems.at[a * 3 + j],
                    device_id=(*chip, c), device_id_type=MESH))
        return copies

    def start(ins, outs, sems):
        for cp in plan(ins, outs, sems, False):
            cp.start()

    def finish(ins, outs, sems):
        arrivals = plan(ins, outs, sems, True)
        for cp in arrivals:
            cp.wait_recv()
        for cp in arrivals:
            cp.wait_send()

    return _Rider(parts, [jax.ShapeDtypeStruct(p.shape, p.dtype) for p in parts],
                  [pltpu.SemaphoreType.DMA((3 * n,)), pltpu.SemaphoreType.DMA((3 * n,))], start, finish)


def _sum_chips_and_share(landed, parts, after, name):
    n = len(landed)

    def body(*refs):
        ins, own, outs, red = refs[:n], refs[n:2 * n], refs[2 * n + 1:3 * n + 1], refs[3 * n + 1:4 * n + 1]
        send_sems, recv_sems, local_sems = refs[4 * n + 1:]
        x, y, c = _mesh_pos()
        sibling = (x, y, 1 - c)
        k_me = 2 * x + y
        copies, local = [], []
        for a in range(n):
            for k in range(N_CHIP):
                @pl.when(k_me == k)
                def _():
                    term = own[a][k].astype(F32)
                    red[a][...] = term if k == 0 else red[a][...] + term

                @pl.when(k_me != k)
                def _():
                    term = ins[a][k].astype(F32)
                    red[a][...] = term if k == 0 else red[a][...] + term

            mine = pltpu.make_async_copy(red[a], outs[a].at[c], local_sems.at[a])
            mine.start()
            local.append(mine)
            cp = pltpu.make_async_remote_copy(
                src_ref=red[a], dst_ref=outs[a].at[c],
                send_sem=send_sems.at[a], recv_sem=recv_sems.at[a],
                device_id=sibling, device_id_type=MESH)
            cp.start()
            copies.append(cp)
        for a in range(n):
            pltpu.make_async_remote_copy(
                src_ref=red[a], dst_ref=outs[a].at[1 - c],
                send_sem=send_sems.at[a], recv_sem=recv_sems.at[a],
                device_id=sibling, device_id_type=MESH).wait_recv()
        for cp in copies:
            cp.wait_send()
        for mine in local:
            mine.wait()

    return pl.pallas_call(
        body, name=name,
        out_shape=[jax.ShapeDtypeStruct((2,) + l.shape[1:], F32) for l in landed],
        in_specs=[pl.BlockSpec(memory_space=pltpu.VMEM)] * (2 * n + 1),
        out_specs=[pl.BlockSpec(memory_space=pl.ANY)] * n,
        scratch_shapes=[pltpu.VMEM(l.shape[1:], F32) for l in landed]
        + [pltpu.SemaphoreType.DMA((n,)), pltpu.SemaphoreType.DMA((n,)), pltpu.SemaphoreType.DMA((n,))],
        compiler_params=pltpu.CompilerParams(vmem_limit_bytes=VMEM_LIMIT_BYTES),
    )(*landed, *parts, after)


def _mod_shard(c_all, w_ada, b_shard, after):
    def body(c_ref, w_ref, b_ref, after_ref, o_ref, sc_ref):
        cc = c_ref[...]
        sc = cc * _sigmoid(cc)
        sc_ref[...] = sc
        o_ref[...] = _dot(sc, w_ref[...]) + b_ref[...]

    nb, d = c_all.shape
    nn = w_ada.shape[1]
    vm = pl.BlockSpec(memory_space=pltpu.VMEM)
    return pl.pallas_call(
        body, name="mod_shard",
        in_specs=[vm, vm, vm, pl.BlockSpec(memory_space=pl.ANY)], out_specs=[vm, vm],
        out_shape=[jax.ShapeDtypeStruct((nb, nn), F32), jax.ShapeDtypeStruct((nb, d), F32)],
        compiler_params=pltpu.CompilerParams(vmem_limit_bytes=VMEM_LIMIT_BYTES),
    )(c_all, w_ada, b_shard, after)


V_SH_M, V_SC_M, V_G_M, V_SH_F, V_SC_F, V_G_F, V_PRE_MIX, V_POST_MIX, V_PRE_FFN, V_POST_FFN = range(10)


def _vrow(vec_ref, r):
    return vec_ref[r:r + 1, :]


def _mix_fwd(x, vecs, w_in_b, w_out_b, sgu_g, wm_b, bsb, wp_b, ps, pmats, ts):
    s_len, d = x.shape
    nt = s_len // ts
    n_proj = w_in_b.shape[1]

    def body(x_ref, vec_ref, win_ref, wout_ref, sg_ref, wm_ref, bs_ref, wp_ref, ps_ref, pm_ref,
             h1_ref, proj_ref, cat_ref, mixed_ref, x2_ref, h2_ref, carry_ref):
        i = pl.program_id(0)

        @pl.when(i == 0)
        def _():
            carry_ref[...] = jnp.zeros_like(carry_ref)

        sub = FFN_TS
        nsub = ts // sub

        def project(s):
            rs = slice(s * sub, (s + 1) * sub)
            x = x_ref[rs, :]
            h1 = ((x * _rms(x)) * (_vrow(vec_ref, V_PRE_MIX) * (1.0 + _vrow(vec_ref, V_SC_M)))
                  + _vrow(vec_ref, V_SH_M)).astype(BF16)
            h1_ref[rs, :] = h1
            proj = _dot(h1, win_ref[...])
            proj_ref[rs, :] = proj.astype(BF16)
            return proj

        def mix(s, proj, halo):
            r0 = s * sub
            t_glob = lax.broadcasted_iota(jnp.int32, (sub, HEAD), 0) + (i * ts + r0)
            for h in range(N_HEADS):
                u = _gelu(proj[:, h * HEAD:(h + 1) * HEAD])
                v = _gelu(proj[:, A_WIDTH + h * HEAD:A_WIDTH + (h + 1) * HEAD])
                vn = ((v * _rms(v)) * sg_ref[h:h + 1, :]).astype(BF16)
                for b in range(sub // HEAD):
                    rs = slice(b * HEAD, (b + 1) * HEAD)
                    z = _dot(wm_ref[h], vn[rs]) + bs_ref[h]
                    cat_ref[r0 + b * HEAD:r0 + (b + 1) * HEAD, h * HEAD:(h + 1) * HEAD] = (u[rs] * z).astype(BF16)
            for g in range(len(POOL_WINDOWS)):
                gs = slice(g * HEAD, (g + 1) * HEAD)
                p = proj[:, 2 * A_WIDTH + g * HEAD:2 * A_WIDTH + (g + 1) * HEAD]
                pooled = _pool_fwd(p, halo[:, gs], g, t_glob)
                yb = _dot(pooled.astype(BF16), wp_ref[g]) * ps_ref[0:1, gs]
                cat_ref[r0:r0 + sub, A_WIDTH + g * HEAD:A_WIDTH + (g + 1) * HEAD] = yb.astype(BF16)

        def finish(s, mixed):
            rs = slice(s * sub, (s + 1) * sub)
            mixed_ref[rs, :] = mixed
            x2 = x_ref[rs, :] + (mixed * _rms(mixed)) * (_vrow(vec_ref, V_G_M) * _vrow(vec_ref, V_POST_MIX))
            x2_ref[rs, :] = x2
            h2 = ((x2 * _rms(x2)) * (_vrow(vec_ref, V_PRE_FFN) * (1.0 + _vrow(vec_ref, V_SC_F)))
                  + _vrow(vec_ref, V_SH_F)).astype(BF16)
            h2_ref[rs, :] = _permute_bf16(pm_ref[0], h2)

        projs = [project(0)]
        halo = carry_ref[...]
        mixed_prev = None
        for s in range(nsub):
            if s + 1 < nsub:
                projs.append(project(s + 1))
            mix(s, projs[s], halo)
            halo = projs[s][sub - POOL_HALO:sub, 2 * A_WIDTH:]
            mixed = _dot(cat_ref[s * sub:(s + 1) * sub, :], wout_ref[...])
            if mixed_prev is not None:
                finish(s - 1, mixed_prev)
            mixed_prev = mixed
        carry_ref[...] = halo
        finish(nsub - 1, mixed_prev)

    row = lambda i: (i, 0)
    return _call(
        body, name="mix_fwd", grid=(nt,),
        in_specs=[_tiled((ts, d), row), _whole(vecs.shape), _resident(w_in_b.shape), _resident(w_out_b.shape),
                  _whole(sgu_g.shape), _whole(wm_b.shape), _whole(bsb.shape), _whole(wp_b.shape), _whole(ps.shape),
                  _whole(pmats.shape)],
        out_specs=[_tiled((ts, d), row), _tiled((ts, n_proj), row), _tiled((ts, d), row),
                   _tiled((ts, d), row), _tiled((ts, d), row), _tiled((ts, d), row)],
        out_shape=[jax.ShapeDtypeStruct((s_len, d), BF16), jax.ShapeDtypeStruct((s_len, n_proj), BF16),
                   jax.ShapeDtypeStruct((s_len, d), BF16), jax.ShapeDtypeStruct((s_len, d), F32),
                   jax.ShapeDtypeStruct((s_len, d), F32), jax.ShapeDtypeStruct((s_len, d), BF16)],
        scratch_shapes=[pltpu.VMEM((POOL_HALO, A_WIDTH), F32)],
        args=(x, vecs, w_in_b, w_out_b, sgu_g, wm_b, bsb, wp_b, ps, pmats))


def _perm_mats(ts):
    p = jnp.arange(ts)
    pm = (((p % SUBLANES) * (ts // SUBLANES) + p // SUBLANES)[:, None] == p[None, :]).astype(BF16)
    return jnp.stack([pm, pm.T])


def _permute_bf16(pm, xb):
    return _dot(pm, xb).astype(BF16)


def _permute_f32(pm, x):
    hi = x.astype(BF16)
    lo = (x - hi.astype(F32)).astype(BF16)
    return _dot(pm, hi) + _dot(pm, lo)


def _conv_out(u, um2, um1, cv_ref, cols):
    return (cv_ref[3:4, cols] + um2 * cv_ref[0:1, cols] + um1 * cv_ref[1:2, cols] + u * cv_ref[2:3, cols])


F_LOSS, F_DGF, F_DPOSTFFN = 0, 1, 2
B_DSHF, B_DSCF, B_DPREFFN, B_DGM, B_DPOSTMIX = 0, 1, 2, 3, 4
M_DSHM, M_DSCM, M_DPREMIX = 0, 1, 2
C_DCB, C_DCW = 0, 1
G1_F, G1_B, G1_M = 0, 8, 16


def _ffn_fwd(h2p, x2, tgt, w_up_b, w_down_b, cvec, vecs, pmats, ts):
    s_len, d = x2.shape
    ff2 = w_up_b.shape[1]
    ff = ff2 // 2
    nt = s_len // ts
    chunks = _ff_chunks(ff)

    def body(h2_ref, x2_ref, t_ref, wu_ref, wd_ref, cv_ref, vec_ref, pm_ref,
             up_ref, y_ref, act_ref, dy_ref, df_ref, acc_ref, carry_ref):
        @pl.when(pl.program_id(0) == 0)
        def _():
            carry_ref[...] = jnp.zeros_like(carry_ref)
            acc_ref[...] = jnp.zeros_like(acc_ref)

        h2v = h2_ref[...]

        def up_dots(o, w):
            return [_dot(h2v, wu_ref[:, base + o:base + o + w]) for base in (0, ff)]

        f = None
        pending = None
        nxt = up_dots(*chunks[0])
        for ci, (o, w) in enumerate(chunks):
            us = nxt
            if ci + 1 < len(chunks):
                nxt = up_dots(*chunks[ci + 1])
            if pending is not None:
                part = _dot(pending[0], wd_ref[pending[1]:pending[1] + pending[2], :])
                f = part if f is None else f + part
            sub0 = lax.broadcasted_iota(jnp.int32, (SUBLANES, w), 0) == 0
            ys = []
            for base, u in zip((0, ff), us):
                cols = slice(base + o, base + o + w)
                up_ref[:, cols] = u.astype(BF16)
                last1, last2 = u[ts - SUBLANES:ts], u[ts - CONV_KEEP:ts - SUBLANES]
                b1 = jnp.where(sub0, pltpu.roll(carry_ref[SUBLANES:CONV_KEEP, cols], 1, 0), pltpu.roll(last1, 1, 0))
                b2 = jnp.where(sub0, pltpu.roll(carry_ref[0:SUBLANES, cols], 1, 0), pltpu.roll(last2, 1, 0))
                um1 = jnp.concatenate([b1, u[:ts - SUBLANES]], axis=0)
                um2 = jnp.concatenate([b2, b1, u[:ts - CONV_KEEP]], axis=0)
                ys.append(_conv_out(u, um2, um1, cv_ref, cols))
                carry_ref[:, cols] = u[ts - CONV_KEEP:ts]
            gate, val = ys
            sg = _sigmoid(gate)
            gs = gate * sg
            act = (gs * val).astype(BF16)
            act_ref[:, o:o + w] = act
            y_ref[:, o:o + w] = (val * (sg + gs * (1.0 - sg))).astype(BF16)
            y_ref[:, ff + o:ff + o + w] = gs.astype(BF16)
            pending = (act, o, w)
        f = f + _dot(pending[0], wd_ref[pending[1]:pending[1] + pending[2], :])
        f = _permute_f32(pm_ref[1], f)
        r3 = _rms(f)
        fhat = f * r3
        post = _vrow(vec_ref, V_POST_FFN)
        g_f = _vrow(vec_ref, V_G_F)
        e = (x2_ref[...] + fhat * (g_f * post)) - t_ref[...]
        dy = e * (1.0 / d)
        dy_ref[...] = dy
        acc_ref[F_LOSS:F_LOSS + 1, :] += _colsum(e * e)
        acc_ref[F_DGF:F_DGF + 1, :] += _colsum(dy * fhat)
        dfhat = dy * (g_f * post)
        df = (r3 * (dfhat - fhat * _rowmean(dfhat * fhat))).astype(BF16)
        df_ref[...] = _permute_bf16(pm_ref[0], df)

        @pl.when(pl.program_id(0) == nt - 1)
        def _():
            through_norm = acc_ref[F_DGF:F_DGF + 1, :]
            acc_ref[F_DGF:F_DGF + 1, :] = through_norm * post
            acc_ref[F_DPOSTFFN:F_DPOSTFFN + 1, :] = through_norm * g_f

    row = lambda i: (i, 0)
    return pl.pallas_call(
        body, name="ffn_fwd", grid=(nt,),
        in_specs=[_tiled((ts, d), row), _tiled((ts, d), row), _tiled((ts, d), row), _resident(w_up_b.shape),
                  _resident(w_down_b.shape), _whole(cvec.shape), _whole(vecs.shape), _whole(pmats.shape)],
        out_specs=[_tiled((ts, ff2), row), _tiled((ts, ff2), row), _tiled((ts, ff), row), _tiled((ts, d), row),
                   _tiled((ts, d), row), _whole((8, d))],
        out_shape=[jax.ShapeDtypeStruct((s_len, ff2), BF16), jax.ShapeDtypeStruct((s_len, ff2), BF16),
                   jax.ShapeDtypeStruct((s_len, ff), BF16), jax.ShapeDtypeStruct((s_len, d), F32),
                   jax.ShapeDtypeStruct((s_len, d), BF16), jax.ShapeDtypeStruct((8, d), F32)],
        scratch_shapes=[pltpu.VMEM((CONV_KEEP, ff2), F32)],
        compiler_params=_seq_params(),
    )(h2p, x2, tgt, w_up_b, w_down_b, cvec, vecs, pmats)


def _ffn_bwd(dfp, upp, yp, x2, dy, mixed, w_up_b, w_down_b, cvec, vecs, pmats, ts, rider=None):
    s_len, d = x2.shape
    ff2 = w_up_b.shape[1]
    ff = ff2 // 2
    nt = s_len // ts
    chunks = _ff_chunks(ff, 512)

    def body(df_ref, up_ref, y_ref, x2_ref, dy_ref, mx_ref, wu_ref, wd_ref, cv_ref, vec_ref, pm_ref,
             dup_ref, dx2_ref, dmx_ref, accc_ref, acc_ref, carry_ref):
        @pl.when(pl.program_id(0) == 0)
        def _():
            carry_ref[...] = jnp.zeros_like(carry_ref)
            accc_ref[...] = jnp.zeros_like(accc_ref)
            acc_ref[...] = jnp.zeros_like(acc_ref)

        dfv = df_ref[...]

        def dh2_add(acc, dups, o, w):
            for base, dup in zip((0, ff), dups):
                part = _dot_nt(dup, wu_ref[:, base + o:base + o + w])
                acc = part if acc is None else acc + part
            return acc

        dh2 = None
        pending = None
        nxt = _dot_nt(dfv, wd_ref[chunks[0][0]:chunks[0][0] + chunks[0][1], :])
        for ci, (o, w) in enumerate(chunks):
            dact = nxt
            if ci + 1 < len(chunks):
                o2, w2 = chunks[ci + 1]
                nxt = _dot_nt(dfv, wd_ref[o2:o2 + w2, :])
            if pending is not None:
                dh2 = dh2_add(dh2, *pending)
            sub7 = lax.broadcasted_iota(jnp.int32, (SUBLANES, w), 0) == SUBLANES - 1
            dups = []
            dys = (dact * y_ref[:, o:o + w].astype(F32), dact * y_ref[:, ff + o:ff + o + w].astype(F32))
            for base, dyv in zip((0, ff), dys):
                cols = slice(base + o, base + o + w)
                u = up_ref[:, cols].astype(F32)
                up1 = SUBLANES - 1
                e0 = jnp.where(sub7, pltpu.roll(carry_ref[0:SUBLANES, cols], up1, 0),
                               pltpu.roll(dyv[0:SUBLANES], up1, 0))
                e1 = jnp.where(sub7, pltpu.roll(carry_ref[SUBLANES:CONV_KEEP, cols], up1, 0),
                               pltpu.roll(dyv[SUBLANES:CONV_KEEP], up1, 0))
                dyp1 = jnp.concatenate([dyv[SUBLANES:], e0], axis=0)
                dyp2 = jnp.concatenate([dyv[CONV_KEEP:], e0, e1], axis=0)
                accc_ref[C_DCB:C_DCB + 1, cols] += _colsum(dyv)
                accc_ref[C_DCW + 0:C_DCW + 1, cols] += _colsum(dyp2 * u)
                accc_ref[C_DCW + 1:C_DCW + 2, cols] += _colsum(dyp1 * u)
                accc_ref[C_DCW + 2:C_DCW + 3, cols] += _colsum(dyv * u)
                dup = (dyv * cv_ref[2:3, cols] + dyp1 * cv_ref[1:2, cols] + dyp2 * cv_ref[0:1, cols]).astype(BF16)
                dup_ref[:, cols] = dup
                dups.append(dup)
                carry_ref[:, cols] = dyv[0:CONV_KEEP]
            pending = (dups, o, w)
        dh2 = dh2_add(dh2, *pending)
        dh2 = _permute_f32(pm_ref[1], dh2)
        x2 = x2_ref[...]
        r2 = _rms(x2)
        xn = x2 * r2
        pre = _vrow(vec_ref, V_PRE_FFN)
        one_sc = 1.0 + _vrow(vec_ref, V_SC_F)
        acc_ref[B_DSHF:B_DSHF + 1, :] += _colsum(dh2)
        acc_ref[B_DSCF:B_DSCF + 1, :] += _colsum(dh2 * xn)
        dxn = dh2 * (pre * one_sc)
        dx2 = dy_ref[...] + r2 * (dxn - xn * _rowmean(dxn * xn))
        dx2_ref[...] = dx2
        mixed = mx_ref[...]
        rm = _rms(mixed)
        mhat = mixed * rm
        post = _vrow(vec_ref, V_POST_MIX)
        g_m = _vrow(vec_ref, V_G_M)
        acc_ref[B_DGM:B_DGM + 1, :] += _colsum(dx2 * mhat)
        dmhat = dx2 * (g_m * post)
        dmx_ref[...] = (rm * (dmhat - mhat * _rowmean(dmhat * mhat))).astype(BF16)

        @pl.when(pl.program_id(0) == nt - 1)
        def _():
            through_norm = acc_ref[B_DSCF:B_DSCF + 1, :]
            acc_ref[B_DSCF:B_DSCF + 1, :] = through_norm * pre
            acc_ref[B_DPREFFN:B_DPREFFN + 1, :] = through_norm * one_sc
            through_mix = acc_ref[B_DGM:B_DGM + 1, :]
            acc_ref[B_DGM:B_DGM + 1, :] = through_mix * post
            acc_ref[B_DPOSTMIX:B_DPOSTMIX + 1, :] = through_mix * g_m

    rev = lambda i: (nt - 1 - i, 0)
    return _call(
        body, name="ffn_bwd", grid=(nt,), rider=rider,
        in_specs=[_tiled((ts, d), rev), _tiled((ts, ff2), rev), _tiled((ts, ff2), rev), _tiled((ts, d), rev),
                  _tiled((ts, d), rev), _tiled((ts, d), rev), _resident(w_up_b.shape), _resident(w_down_b.shape),
                  _whole(cvec.shape), _whole(vecs.shape), _whole(pmats.shape)],
        out_specs=[_tiled((ts, ff2), rev), _tiled((ts, d), rev), _tiled((ts, d), rev), _whole((8, ff2)),
                   _whole((8, d))],
        out_shape=[jax.ShapeDtypeStruct((s_len, ff2), BF16), jax.ShapeDtypeStruct((s_len, d), F32),
                   jax.ShapeDtypeStruct((s_len, d), BF16), jax.ShapeDtypeStruct((8, ff2), F32),
                   jax.ShapeDtypeStruct((8, d), F32)],
        scratch_shapes=[pltpu.VMEM((CONV_KEEP, ff2), F32)],
        args=(dfp, upp, yp, x2, dy, mixed, w_up_b, w_down_b, cvec, vecs, pmats))


def _mix_bwd(dmixed, dx2, x, proj, vecs, w_in_b, w_out_b, sgu_g, wm_b, wmt_b, bsb, wp_b, ps, ts):
    s_len, d = x.shape
    nt = s_len // ts
    n_proj = proj.shape[1]
    per = ts // POOL_HALO

    def body(dmx_ref, dx2_ref, x_ref, proj_ref, projh_ref, vec_ref, win_ref, wout_ref, sg_ref, wm_ref, wmt_ref,
             bs_ref, wp_ref, ps_ref,
             gx_ref, dproj_ref, acc_ref, dwm_out, dwp_out, db_ref, dg_ref, dps_ref,
             carry_ref, dwm_ref, dwp_ref, dbz_ref):
        i = pl.program_id(0)
        tile = nt - 1 - i

        @pl.when(i == 0)
        def _():
            carry_ref[...] = jnp.zeros_like(carry_ref)
            for r in (acc_ref, dwm_ref, dwp_ref, dbz_ref, dg_ref, dps_ref):
                r[...] = jnp.zeros_like(r)

        sub = FFN_TS
        nsub = ts // sub
        ext = sub + POOL_HALO

        def cotangent(s):
            return _dot_nt(dmx_ref[s * sub:(s + 1) * sub, :], wout_ref[...])

        def back(s, dcat, halo, later):
            r0 = s * sub
            rows = slice(r0, r0 + sub)
            t_glob = lax.broadcasted_iota(jnp.int32, (sub, HEAD), 0) + (tile * ts + r0)
            for h in range(N_HEADS):
                hs = slice(h * HEAD, (h + 1) * HEAD)
                vs = slice(A_WIDTH + h * HEAD, A_WIDTH + (h + 1) * HEAD)
                gain = sg_ref[h:h + 1, :]
                for b in range(sub // HEAD):
                    blk = slice(r0 + b * HEAD, r0 + (b + 1) * HEAD)
                    au = proj_ref[blk, hs].astype(F32)
                    av = proj_ref[blk, vs].astype(F32)
                    u, u_grad = _gelu_and_grad(au)
                    v, v_grad = _gelu_and_grad(av)
                    rv = _rms(v)
                    vhat = v * rv
                    vn = (vhat * gain).astype(BF16)
                    dout = dcat[b * HEAD:(b + 1) * HEAD, hs]
                    z = _dot(wm_ref[h], vn) + bs_ref[h]
                    dz = dout * u
                    dbz_ref[h] += dz
                    dzb = dz.astype(BF16)
                    dwm_ref[h] += _dot_nt(dzb, vn)
                    dvn = _dot(wmt_ref[h], dzb)
                    dg_ref[h:h + 1, :] += _colsum(dvn * vhat)
                    dvhat = dvn * gain
                    dv = rv * (dvhat - vhat * _rowmean(dvhat * vhat))
                    dproj_ref[blk, hs] = ((dout * z) * u_grad).astype(BF16)
                    dproj_ref[blk, vs] = (dv * v_grad).astype(BF16)
            firsts = []
            for g in range(len(POOL_WINDOWS)):
                gs = slice(g * HEAD, (g + 1) * HEAD)
                pcols = slice(2 * A_WIDTH + g * HEAD, 2 * A_WIDTH + (g + 1) * HEAD)
                p = proj_ref[rows, pcols].astype(F32)
                pb = _pool_fwd(p, halo[:, gs], g, t_glob).astype(BF16)
                dyb = dcat[:, A_WIDTH + g * HEAD:A_WIDTH + (g + 1) * HEAD]
                dps_ref[0:1, gs] += _colsum(dyb * _dot(pb, wp_ref[g]))
                dyl = (dyb * ps_ref[0:1, gs]).astype(BF16)
                dwp_ref[g] += _dot_tn(pb, dyl)
                dpooled = _dot_nt(dyl, wp_ref[g])
                cnt = jnp.minimum(t_glob + 1, POOL_WINDOWS[g]).astype(F32)
                q = dpooled / cnt
                acc = jnp.concatenate([q, later[:, gs]], axis=0)
                for step in range(g + 1):
                    acc = acc + pltpu.roll(acc, ext - (1 << step), 0)
                dproj_ref[rows, pcols] = (acc[:sub] - dpooled).astype(BF16)
                firsts.append(q[0:POOL_HALO])
            return jnp.concatenate(firsts, axis=1)

        def finish(s, dh1):
            rows = slice(s * sub, (s + 1) * sub)
            x = x_ref[rows, :]
            r1 = _rms(x)
            xn = x * r1
            pre = _vrow(vec_ref, V_PRE_MIX)
            one_sc = 1.0 + _vrow(vec_ref, V_SC_M)
            acc_ref[M_DSHM:M_DSHM + 1, :] += _colsum(dh1)
            acc_ref[M_DSCM:M_DSCM + 1, :] += _colsum(dh1 * xn)
            dxn = dh1 * (pre * one_sc)
            gx_ref[rows, :] = dx2_ref[rows, :] + r1 * (dxn - xn * _rowmean(dxn * xn))

        nxt = cotangent(nsub - 1)
        later = carry_ref[...]
        dh1_prev = None
        for s in reversed(range(nsub)):
            dcat = nxt
            if s > 0:
                nxt = cotangent(s - 1)
                halo = proj_ref[s * sub - POOL_HALO:s * sub, 2 * A_WIDTH:].astype(F32)
            else:
                halo = jnp.where(tile > 0, projh_ref[:, 2 * A_WIDTH:].astype(F32), 0.0)
            later = back(s, dcat, halo, later)
            dh1 = _dot_nt(dproj_ref[s * sub:(s + 1) * sub, :], win_ref[...])
            if dh1_prev is not None:
                finish(s + 1, dh1_prev)
            dh1_prev = dh1
        carry_ref[...] = later
        finish(0, dh1_prev)

        @pl.when(i == nt - 1)
        def _():
            through_norm = acc_ref[M_DSCM:M_DSCM + 1, :]
            acc_ref[M_DSCM:M_DSCM + 1, :] = through_norm * _vrow(vec_ref, V_PRE_MIX)
            acc_ref[M_DPREMIX:M_DPREMIX + 1, :] = through_norm * (1.0 + _vrow(vec_ref, V_SC_M))
            dwm_out[...] = dwm_ref[...].astype(BF16)
            dwp_out[...] = dwp_ref[...].astype(BF16)
            db_ref[...] = jnp.zeros_like(db_ref)
            for h in range(N_HEADS):
                db_ref[h:h + 1, :] = jnp.sum(dbz_ref[h].T, axis=0, keepdims=True)

    rev = lambda i: (nt - 1 - i, 0)
    halo_map = lambda i: (jnp.maximum((nt - 1 - i) * per - 1, 0), 0)
    hshape = (N_HEADS, HEAD, HEAD)
    return _call(
        body, name="mix_bwd", grid=(nt,),
        in_specs=[_tiled((ts, d), rev), _tiled((ts, d), rev), _tiled((ts, d), rev), _tiled((ts, n_proj), rev),
                  _tiled((POOL_HALO, n_proj), halo_map), _whole(vecs.shape), _resident(w_in_b.shape),
                  _resident(w_out_b.shape), _whole(sgu_g.shape), _whole(wm_b.shape), _whole(wmt_b.shape),
                  _whole(bsb.shape), _whole(wp_b.shape), _whole(ps.shape)],
        out_specs=[_tiled((ts, d), rev), _tiled((ts, n_proj), rev), _whole((8, d)), _whole(hshape), _whole(hshape),
                   _whole((8, HEAD)), _whole((8, HEAD)), _whole((8, A_WIDTH))],
        out_shape=[jax.ShapeDtypeStruct((s_len, d), F32), jax.ShapeDtypeStruct((s_len, n_proj), BF16),
                   jax.ShapeDtypeStruct((8, d), F32), jax.ShapeDtypeStruct(hshape, BF16),
                   jax.ShapeDtypeStruct(hshape, BF16), jax.ShapeDtypeStruct((8, HEAD), F32),
                   jax.ShapeDtypeStruct((8, HEAD), F32), jax.ShapeDtypeStruct((8, A_WIDTH), F32)],
        scratch_shapes=[pltpu.VMEM((POOL_HALO, A_WIDTH), F32), pltpu.VMEM(hshape, F32), pltpu.VMEM(hshape, F32),
                        pltpu.VMEM(hshape, F32)],
        args=(dmixed, dx2, x, proj, proj, vecs, w_in_b, w_out_b, sgu_g, wm_b, wmt_b, bsb, wp_b, ps))


def _wgrad(a, b, tn, ts, name, rider=None):
    s_len, m = a.shape
    n = b.shape[1]
    ts = min(ts, s_len)

    def body(a_ref, b_ref, o_ref):
        @pl.when(pl.program_id(1) == 0)
        def _():
            o_ref[...] = jnp.zeros_like(o_ref)

        o_ref[...] += _dot_tn(a_ref[...], b_ref[...])

    (g,), r_out = _call(
        body, name=name, grid=(n // tn, s_len // ts), rider=rider,
        in_specs=[pl.BlockSpec((ts, m), lambda j, s: (s, 0)), pl.BlockSpec((ts, tn), lambda j, s: (s, j))],
        out_specs=[pl.BlockSpec((m, tn), lambda j, s: (0, j))],
        out_shape=[jax.ShapeDtypeStruct((m, n), F32)], scratch_shapes=[], args=(a, b))
    return g, r_out


def _wgrad_pair(a1, b1, a2, b2, after, ts, name):
    s_len = a1.shape[0]
    ts = min(ts, s_len)
    shapes = [(a1.shape[1], b1.shape[1]), (a2.shape[1], b2.shape[1])]

    def body(a1_ref, b1_ref, a2_ref, b2_ref, after_ref, o1_ref, o2_ref):
        @pl.when(pl.program_id(0) == 0)
        def _():
            o1_ref[...] = jnp.zeros_like(o1_ref)
            o2_ref[...] = jnp.zeros_like(o2_ref)

        o1_ref[...] += _dot_tn(a1_ref[...], b1_ref[...])
        o2_ref[...] += _dot_tn(a2_ref[...], b2_ref[...])

    row = lambda s: (s, 0)
    return _call(
        body, name=name, grid=(s_len // ts,),
        in_specs=[pl.BlockSpec((ts, t.shape[1]), row) for t in (a1, b1, a2, b2)] + [_whole(after.shape)],
        out_specs=[_whole(sh) for sh in shapes],
        out_shape=[jax.ShapeDtypeStruct(sh, F32) for sh in shapes], scratch_shapes=[],
        args=(a1, b1, a2, b2, after))[0]


def _adamw_big(g, w, m, v, name):
    r, cdim = g.shape
    tr = r
    while tr * cdim * 4 > (3 << 19) and tr % 16 == 0:
        tr //= 2

    def body(g_ref, w_ref, m_ref, v_ref, go_ref, d_ref, nm_ref, nv_ref):
        grad = g_ref[...]
        delta, m2, v2 = _adamw_math(w_ref[0], grad, m_ref[0], v_ref[0])
        go_ref[0] = grad
        d_ref[0] = delta
        nm_ref[0] = m2
        nv_ref[0] = v2

    s3 = pl.BlockSpec((1, tr, cdim), lambda i: (0, i, 0))
    return pl.pallas_call(
        body, name=name, grid=(r // tr,),
        in_specs=[pl.BlockSpec((tr, cdim), lambda i: (i, 0)), s3, s3, s3],
        out_specs=[s3, s3, s3, s3],
        out_shape=[jax.ShapeDtypeStruct(w.shape, F32)] * 4,
        compiler_params=pltpu.CompilerParams(dimension_semantics=("parallel",), vmem_limit_bytes=VMEM_LIMIT_BYTES),
    )(g, w, m, v)


def _wada_update(sct, gm, w, m, v):
    _, r, cdim = w.shape
    tr = 256
    kp = sct.shape[1]

    def body(s_ref, g_ref, w_ref, m_ref, v_ref, gw_ref, d_ref, nm_ref, nv_ref):
        g = _dot(s_ref[...], g_ref[...])
        gw_ref[0] = g
        delta, m2, v2 = _adamw_math(w_ref[0], g, m_ref[0], v_ref[0])
        d_ref[0] = delta
        nm_ref[0] = m2
        nv_ref[0] = v2

    s3 = pl.BlockSpec((1, tr, cdim), lambda i: (0, i, 0))
    return pl.pallas_call(
        body, name="wada_update", grid=(r // tr,),
        in_specs=[pl.BlockSpec((tr, kp), lambda i: (i, 0)), _whole(gm.shape), s3, s3, s3],
        out_specs=[s3, s3, s3, s3],
        out_shape=[jax.ShapeDtypeStruct(w.shape, F32)] * 4,
        compiler_params=pltpu.CompilerParams(dimension_semantics=("parallel",), vmem_limit_bytes=VMEM_LIMIT_BYTES),
    )(sct, gm, w, m, v)


def _small_update(g1, g2, g2s, gwm, gwp, gbz, gsg, gps, params):
    names = ["b_ada", "pre_mix_g", "post_mix_g", "sgu_norm_g", "w_spatial", "b_spatial", "w_pool", "pool_scale",
             "pre_ffn_g", "post_ffn_g", "conv_w", "conv_b"]
    d = g1.shape[2]
    flat_in = [g1, g2, g2s, gwm, gwp, gbz, gsg, gps]
    n_g = len(flat_in)
    for nm in names:
        flat_in += list(params[nm])

    def body(*refs):
        g1_ref, g2_ref, g2s_ref, gwm_ref, gwp_ref, gbz_ref, gsg_ref, gps_ref = refs[:n_g]
        wmv = refs[n_g:n_g + 3 * len(names)]
        loss_ref = refs[n_g + 3 * len(names)]
        outs = refs[n_g + 3 * len(names) + 1:]

        def dsum(ref, idx):
            acc = ref[(0,) + idx].astype(F32)
            for dev in range(1, N_DEV):
                acc = acc + ref[(dev,) + idx].astype(F32)
            return acc

        def apply(pi, g, widx, oidx):
            w_ref, m_ref, v_ref = wmv[3 * pi:3 * pi + 3]
            g_ref, d_ref, nm_ref, nv_ref = outs[4 * pi:4 * pi + 4]
            delta, m2, v2 = _adamw_math(w_ref[widx], g, m_ref[widx], v_ref[widx])
            g_ref[oidx] = g
            d_ref[oidx] = delta
            nm_ref[oidx] = m2
            nv_ref[oidx] = v2

        def row1(base, r):
            return (slice(base + r, base + r + 1), slice(None))

        tot = dsum(g1_ref, row1(G1_F, F_LOSS))
        loss_ref[...] = jnp.zeros(loss_ref.shape, F32) + jnp.sum(tot) * (0.5 / d)
        mod_rows = [row1(G1_M, M_DSHM), row1(G1_M, M_DSCM), row1(G1_B, B_DGM), row1(G1_B, B_DSHF),
                    row1(G1_B, B_DSCF), row1(G1_F, F_DGF)]
        for j, rr in enumerate(mod_rows):
            cs = (slice(None), slice(j * d, (j + 1) * d))
            apply(0, dsum(g1_ref, rr), cs, cs)
        full = (slice(None), slice(None))
        apply(1, dsum(g1_ref, row1(G1_M, M_DPREMIX)), full, full)
        apply(2, dsum(g1_ref, row1(G1_B, B_DPOSTMIX)), full, full)
        apply(3, dsum(gsg_ref, (slice(0, N_HEADS), slice(None))), (0,), (0,))
        pos_i = lax.broadcasted_iota(jnp.int32, (HEAD, HEAD), 0)
        pos_j = lax.broadcasted_iota(jnp.int32, (HEAD, HEAD), 1)
        causal = (pos_j // CHUNK) <= (pos_i // CHUNK)
        for h in range(N_HEADS):
            blk = (slice(h * HEAD, (h + 1) * HEAD), slice(None))
            apply(4, jnp.where(causal, dsum(gwm_ref, blk), 0.0), (0, h), (0, h))
            apply(5, dsum(gbz_ref, (slice(h, h + 1), slice(None))), (0, slice(h, h + 1)), (0, slice(h, h + 1)))
            apply(6, dsum(gwp_ref, blk), (0, h), (0, h))
        apply(7, dsum(gps_ref, (slice(0, 1), slice(None))), full, full)
        apply(8, dsum(g1_ref, row1(G1_B, B_DPREFFN)), full, full)
        apply(9, dsum(g1_ref, row1(G1_F, F_DPOSTFFN)), full, full)
        apply(10, dsum(g2s_ref, (slice(C_DCW, C_DCW + 3), slice(None))), (0,), (0,))
        apply(11, dsum(g2_ref, (slice(C_DCB, C_DCB + 1), slice(None))), full, full)

    out_shape = [jax.ShapeDtypeStruct((8, HEAD), F32)]
    for nm in names:
        out_shape += [jax.ShapeDtypeStruct(params[nm][0].shape, F32)] * 4
    res = pl.pallas_call(
        body, name="small_update", out_shape=out_shape,
        compiler_params=pltpu.CompilerParams(vmem_limit_bytes=VMEM_LIMIT_BYTES),
    )(*flat_in)
    out = {nm: tuple(res[1 + 4 * i:5 + 4 * i]) for i, nm in enumerate(names)}
    return res[0], out


def kernel(x, c, w_ada, b_ada, pre_mix_g, post_mix_g, w_in, sgu_norm_g, w_spatial, b_spatial, w_pool, pool_scale, w_out, pre_ffn_g, post_ffn_g, w_up, conv_w, conv_b, w_down, loss_target, m_w_ada, m_b_ada, m_pre_mix_g, m_post_mix_g, m_w_in, m_sgu_norm_g, m_w_spatial, m_b_spatial, m_w_pool, m_pool_scale, m_w_out, m_pre_ffn_g, m_post_ffn_g, m_w_up, m_conv_w, m_conv_b, m_w_down, v_w_ada, v_b_ada, v_pre_mix_g, v_post_mix_g, v_w_in, v_sgu_norm_g, v_w_spatial, v_b_spatial, v_w_pool, v_pool_scale, v_w_out, v_pre_ffn_g, v_post_ffn_g, v_w_up, v_conv_w, v_conv_b, v_w_down):
    xi, yi, ci = _mesh_pos()
    k_me = 2 * xi + yi
    dev = 2 * k_me + ci
    s_len, d = x.shape[1], x.shape[2]
    x2d = x[0]
    tgt = loss_target[0]
    ff2 = conv_b.shape[1]
    n_ada = w_ada.shape[2]
    n_cw = conv_w.shape[2]

    k_idx = k_me.reshape(1).astype(jnp.int32)
    mix_flags = (True, False)
    (w_in_s, w_out_s), (w_in_f, w_out_f) = _cast_bf16([w_in[0], w_out[0]], mix_flags, k_idx, "cast_weights_mix")

    def place(t):
        return lax.dynamic_update_slice(jnp.zeros((N_DEV * t.shape[0], t.shape[1]), t.dtype), t,
                                        (dev * t.shape[0], 0))

    cw_blk = jnp.concatenate([conv_w[0], jnp.zeros((5, n_cw), F32)], axis=0)
    fly_c = _split_begin([place(c.reshape(8, d // 8)), place(cw_blk)], 6, _small_ici_copies, k_idx,
                         "c_gather_begin")
    fly_mix = _gather_begin([w_in_s, w_out_s], [w_in_f, w_out_f], mix_flags, fly_c[3], "gather_begin_mix")
    c_all, cw_all = _small_finish(_split_end(fly_c, _small_ici_copies, fly_mix[3], "c_gather_end"),
                                  "c_gather_finish")
    (w_up_s, w_down_s), (w_up_f, w_down_f) = _cast_bf16([w_up[0], w_down[0]], mix_flags, k_idx, "cast_weights_ffn",
                                                        after=fly_mix[3])
    c_all = c_all.reshape(N_DEV, 8, d // 8).reshape(N_DEV, d)
    cw_full = jnp.concatenate([cw_all[16 * k:16 * k + 8] for k in range(N_CHIP)], axis=1)
    cvec = jnp.concatenate([cw_full[0:3], conv_b, jnp.zeros((4, ff2), F32)], axis=0)
    b_shard = lax.dynamic_slice_in_dim(b_ada, k_me * n_ada, n_ada, axis=1)
    mod_k, sc_all = _mod_shard(c_all, w_ada[0], b_shard, w_down_s)
    (mod_g,) = _run_rider(_allgather_rider([mod_k]), "gather_mod")
    mod_all = jnp.concatenate([mod_g[16 * k:16 * k + 8] for k in range(N_CHIP)], axis=1)
    mod_me = lax.dynamic_slice_in_dim(mod_all, dev, 1, axis=0).reshape(6, d)
    vecs = jnp.concatenate([mod_me, pre_mix_g, post_mix_g, pre_ffn_g, post_ffn_g, jnp.zeros((6, d), F32)], axis=0)

    fly_ffn = _gather_begin([w_up_s, w_down_s], [w_up_f, w_down_f], mix_flags, mod_g, "gather_begin_ffn")
    w_in_b, w_out_b = _gather_finish(_gather_end(fly_mix, mix_flags, fly_ffn[3], "gather_end_mix"), mix_flags,
                                     "gather_finish_mix")

    pos = jnp.arange(HEAD)
    causal = (pos[None, :] // CHUNK) <= (pos[:, None] // CHUNK)
    wm = jnp.where(causal[None], w_spatial[0], 0.0)
    wm_b = wm.astype(BF16)
    wmt_b = jnp.swapaxes(wm, 1, 2).astype(BF16)
    bsb = jnp.broadcast_to(b_spatial[0][:, :, None], (N_HEADS, HEAD, HEAD))
    wp_b = w_pool[0].astype(BF16)
    sgu_g = jnp.concatenate([sgu_norm_g[0], jnp.zeros((4, HEAD), F32)], axis=0)
    ps = jnp.concatenate([pool_scale, jnp.zeros((7, A_WIDTH), F32)], axis=0)

    pmats = _perm_mats(FFN_TS)
    h1, proj, cat, mixed, x2, h2p = _mix_fwd(x2d, vecs, w_in_b, w_out_b, sgu_g, wm_b, bsb, wp_b, ps, pmats,
                                             ts=MIX_TS)[0]
    w_up_b, w_down_b = _gather_finish(_gather_end(fly_ffn, mix_flags, h2p, "gather_end_ffn"), mix_flags,
                                      "gather_finish_ffn")
    up, yv, act, dy, dfp, acc_f = _ffn_fwd(h2p, x2, tgt, w_up_b, w_down_b, cvec, vecs, pmats, ts=FFN_TS)

    c_idx = ci.reshape(1).astype(jnp.int32)
    g_w_down, _ = _wgrad(act, dfp, d, WGRAD_TS_WIDE, "wgrad_down")
    (dup, dx2, dmixed, acc_c, acc_b), (land_down,) = _ffn_bwd(
        dfp, up, yv, x2, dy, mixed, w_up_b, w_down_b, cvec, vecs, pmats, ts=FFN_TS,
        rider=_sibling_rider([g_w_down], (False,)))
    (part_down,) = _sum_with_sibling([g_w_down], [land_down], (False,), c_idx, "pair_sum_down")
    g_w_up, (chips_down,) = _wgrad(h2p, dup, ff2 // 2, WGRAD_TS, "wgrad_up", rider=_chips_rider([part_down]))
    fly_up = _sibling_begin(g_w_up, True, chips_down, "sibling_begin_up")
    gx, dproj, acc_m, dwm, dwp, dbz, dsg, dps = _mix_bwd(
        dmixed, dx2, x2d, proj, vecs + fly_up[3][0:1, 0:1], w_in_b, w_out_b, sgu_g, wm_b, wmt_b, bsb, wp_b, ps,
        ts=MIX_TS)[0]
    g_w_up, land_up = _sibling_end(fly_up, True, dproj, "sibling_end_up")
    (part_up,) = _sum_with_sibling([g_w_up], [land_up], (True,), c_idx, "pair_sum_up")
    fly_chips_up = _chips_begin([part_up], c_idx, "chips_begin_up")
    g1 = jnp.concatenate([acc_f, acc_b, acc_m], axis=0)
    hflat = (N_HEADS * HEAD, HEAD)
    small_bufs = [place(t) for t in (g1, acc_c, dwm.reshape(hflat), dwp.reshape(hflat), dbz, dsg, dps)]
    fly_small = _split_begin(small_bufs, 3 * len(small_bufs), _small_ici_copies, fly_chips_up[3],
                             "small_gather_begin")
    g_w_out, g_w_in = _wgrad_pair(cat, dmixed, h1, dproj, fly_small[3], WGRAD_TS_WIDE, "wgrad_mix")
    land_mix = _run_rider(_sibling_rider([g_w_in, g_w_out], (True, False)), "reduce_to_sibling")
    parts_mix = _sum_with_sibling([g_w_in, g_w_out], land_mix, (True, False), c_idx, "pair_sum_mix")
    (part_up,), (chips_up,) = _chips_end(fly_chips_up, parts_mix[0], "chips_end_up")
    fly_chips_mix = _chips_begin(parts_mix, chips_up, "chips_begin_mix")

    def adamw_of(names, reduced):
        res = {}
        for nm, red in zip(names, reduced):
            w, m, v = big_wmv[nm]
            g = red.reshape(w.shape[1], w.shape[2])
            res[nm] = tuple(_adamw_big(g, w, m, v, "adamw_" + nm))
        return res

    big_wmv = {"w_in": (w_in, m_w_in, v_w_in), "w_out": (w_out, m_w_out, v_w_out),
               "w_up": (w_up, m_w_up, v_w_up), "w_down": (w_down, m_w_down, v_w_down)}
    big = adamw_of(("w_up", "w_down"), _sum_chips_and_share([chips_up, chips_down], [part_up, part_down],
                                                           fly_chips_mix[3], "sum_share_ffn"))

    gathered = _small_finish(_split_end(fly_small, _small_ici_copies, big["w_down"][1], "small_gather_end"),
                             "small_gather_finish")
    g1a, g2a, gwm, gwp, gbz, gsg, gps = [t.reshape((N_DEV, t.shape[0] // N_DEV, t.shape[1])) for t in gathered]
    g2s = lax.dynamic_slice_in_dim(g2a, k_me * n_cw, n_cw, axis=2)
    params = {
        "b_ada": (b_ada, m_b_ada, v_b_ada), "pre_mix_g": (pre_mix_g, m_pre_mix_g, v_pre_mix_g),
        "post_mix_g": (post_mix_g, m_post_mix_g, v_post_mix_g),
        "sgu_norm_g": (sgu_norm_g, m_sgu_norm_g, v_sgu_norm_g), "w_spatial": (w_spatial, m_w_spatial, v_w_spatial),
        "b_spatial": (b_spatial, m_b_spatial, v_b_spatial), "w_pool": (w_pool, m_w_pool, v_w_pool),
        "pool_scale": (pool_scale, m_pool_scale, v_pool_scale), "pre_ffn_g": (pre_ffn_g, m_pre_ffn_g, v_pre_ffn_g),
        "post_ffn_g": (post_ffn_g, m_post_ffn_g, v_post_ffn_g), "conv_w": (conv_w, m_conv_w, v_conv_w),
        "conv_b": (conv_b, m_conv_b, v_conv_b),
    }
    loss_slab, small = _small_update(g1a, g2a, g2s, gwm, gwp, gbz, gsg, gps, params)

    gmod_all = jnp.concatenate(
        [g1a[:, G1_M + M_DSHM], g1a[:, G1_M + M_DSCM], g1a[:, G1_B + B_DGM], g1a[:, G1_B + B_DSHF],
         g1a[:, G1_B + B_DSCF], g1a[:, G1_F + F_DGF]], axis=1)
    gm = lax.dynamic_slice_in_dim(gmod_all, k_me * n_ada, n_ada, axis=1)
    gm = jnp.concatenate([gm, jnp.zeros((HEAD - N_DEV, n_ada), F32)], axis=0)
    sct = jnp.concatenate([sc_all.T, jnp.zeros((d, HEAD - N_DEV), F32)], axis=1)
    ada = tuple(_wada_update(sct, gm, w_ada, m_w_ada, v_w_ada))

    parts_mix, chips_mix = _chips_end(fly_chips_mix, ada[1], "chips_end_mix")
    big.update(adamw_of(("w_in", "w_out"), _sum_chips_and_share(chips_mix, parts_mix, loss_slab, "sum_share_mix")))

    everything = dict(small)
    everything.update(big)
    everything["w_ada"] = ada
    order = ["w_ada", "b_ada", "pre_mix_g", "post_mix_g", "w_in", "sgu_norm_g", "w_spatial", "b_spatial", "w_pool",
             "pool_scale", "w_out", "pre_ffn_g", "post_ffn_g", "w_up", "conv_w", "conv_b", "w_down"]
    outs = [loss_slab[0, 0], gx.reshape(x.shape)]
    for j in range(4):
        outs += [everything[nm][j] for nm in order]
    return tuple(outs)
```

```python
import functools

import jax
import jax.numpy as jnp
from jax import lax
from jax.experimental import pallas as pl
from jax.experimental.pallas import tpu as pltpu

F32 = jnp.float32
BF16 = jnp.bfloat16
MESH = pl.DeviceIdType.MESH

EPS = 1e-6
HEAD = 128
N_HEADS = 4
A_WIDTH = N_HEADS * HEAD
CHUNK = 64
POOL_WINDOWS = (2, 4, 8, 16)
POOL_HALO = 16
SUBLANES = 8
CONV_KEEP = 2 * SUBLANES
FFN_TS = 256
MIX_TS = 512
WGRAD_TS = 2048
WGRAD_TS_WIDE = 1024

ADAM_LR = 0.001
ADAM_B1 = 0.9
ADAM_B2 = 0.999
ADAM_EPS = 1e-08
ADAM_WD = 0.01
ADAM_STEP = 10

VMEM_LIMIT_BYTES = 58 * 1024 * 1024
N_DEV = 8
N_CHIP = 4


def _dot(a, b):
    return jnp.dot(a, b, preferred_element_type=F32)


def _dot_nt(a, b):
    return lax.dot_general(a, b, (((1,), (1,)), ((), ())), preferred_element_type=F32)


def _dot_tn(a, b):
    return lax.dot_general(a, b, (((0,), (0,)), ((), ())), preferred_element_type=F32)


GELU_C0 = 0.7978845608028654
GELU_C1 = GELU_C0 * 0.044715


def _gelu_and_grad(x):
    x2 = x * x
    t = jnp.tanh(x * (GELU_C0 + GELU_C1 * x2))
    half = 0.5 + 0.5 * t
    grad = half + (x * (half * (1.0 - t))) * (GELU_C0 + (3.0 * GELU_C1) * x2)
    return x * half, grad


def _sigmoid(x):
    return 1.0 / (1.0 + jnp.exp(-x))


def _rms(x):
    return lax.rsqrt(jnp.mean(x * x, axis=-1, keepdims=True) + EPS)


def _colsum(x):
    return jnp.sum(x, axis=0, keepdims=True)


def _rowmean(x):
    return jnp.mean(x, axis=-1, keepdims=True)


def _tiled(shape, index_map):
    return pl.BlockSpec(shape, index_map)


def _resident(shape):
    nd = len(shape)
    return pl.BlockSpec(shape, lambda *_: (0,) * nd, pipeline_mode=pl.Buffered(1))


def _whole(shape):
    nd = len(shape)
    return pl.BlockSpec(shape, lambda *_: (0,) * nd)


def _seq_params():
    return pltpu.CompilerParams(dimension_semantics=("arbitrary",), vmem_limit_bytes=VMEM_LIMIT_BYTES)


def _ff_chunks(f, width=768):
    out, o = [], 0
    while o < f:
        w = min(width, f - o)
        out.append((o, w))
        o += w
    return out


def _pool_fwd(p, halo, g, t_glob):
    ext = jnp.concatenate([halo, p], axis=0)
    s = ext
    for step in range(g + 1):
        s = s + pltpu.roll(s, 1 << step, 0)
    cnt = jnp.minimum(t_glob + 1, POOL_WINDOWS[g]).astype(F32)
    return s[POOL_HALO:] / cnt - p


def _adamw_math(w, g, m, v):
    m = ADAM_B1 * m + (1.0 - ADAM_B1) * g
    v = ADAM_B2 * v + (1.0 - ADAM_B2) * (g * g)
    m_hat = m / (1.0 - ADAM_B1 ** ADAM_STEP)
    v_hat = v / (1.0 - ADAM_B2 ** ADAM_STEP)
    delta = -ADAM_LR * (m_hat / (jnp.sqrt(v_hat) + ADAM_EPS) + ADAM_WD * w)
    return delta, m, v


def _mesh_pos():
    return lax.axis_index("x"), lax.axis_index("y"), lax.axis_index("c")


class _Rider:
    def __init__(self, inputs, out_shape, sems, start, finish):
        self.inputs, self.out_shape, self.sems = list(inputs), list(out_shape), list(sems)
        self.start, self.finish = start, finish


def _call(body, *, name, grid, in_specs, out_specs, out_shape, scratch_shapes, args, rider=None):
    params = pltpu.CompilerParams(dimension_semantics=("arbitrary",) * len(grid), vmem_limit_bytes=VMEM_LIMIT_BYTES)
    if rider is None:
        res = pl.pallas_call(body, name=name, grid=grid, in_specs=in_specs, out_specs=out_specs, out_shape=out_shape,
                             scratch_shapes=scratch_shapes, compiler_params=params)(*args)
        return tuple(res), ()
    cuts = [len(in_specs), len(rider.inputs), len(out_specs), len(rider.out_shape), len(scratch_shapes),
            len(rider.sems)]

    def hosted(*refs):
        groups, a = [], 0
        for cnt in cuts:
            groups.append(refs[a:a + cnt])
            a += cnt
        ins, r_in, outs, r_out, scr, r_sem = groups
        first = functools.reduce(jnp.logical_and, [pl.program_id(k) == 0 for k in range(len(grid))])
        last = functools.reduce(jnp.logical_and, [pl.program_id(k) == grid[k] - 1 for k in range(len(grid))])

        @pl.when(first)
        def _():
            rider.start(r_in, r_out, r_sem)

        body(*ins, *outs, *scr)

        @pl.when(last)
        def _():
            rider.finish(r_in, r_out, r_sem)

    anyspec = pl.BlockSpec(memory_space=pl.ANY)
    res = pl.pallas_call(
        hosted, name=name, grid=grid,
        in_specs=list(in_specs) + [anyspec] * cuts[1], out_specs=list(out_specs) + [anyspec] * cuts[3],
        out_shape=list(out_shape) + rider.out_shape, scratch_shapes=list(scratch_shapes) + rider.sems,
        compiler_params=params)(*args, *rider.inputs)
    return tuple(res[:cuts[2]]), tuple(res[cuts[2]:])


def _run_rider(rider, name):
    n_in, n_out = len(rider.inputs), len(rider.out_shape)

    def body(*refs):
        r_in, r_out, r_sem = refs[:n_in], refs[n_in:n_in + n_out], refs[n_in + n_out:]
        rider.start(r_in, r_out, r_sem)
        rider.finish(r_in, r_out, r_sem)

    anyspec = pl.BlockSpec(memory_space=pl.ANY)
    return pl.pallas_call(body, name=name, out_shape=rider.out_shape, in_specs=[anyspec] * n_in,
                          out_specs=[anyspec] * n_out, scratch_shapes=rider.sems)(*rider.inputs)


def _allgather_rider(arrs):
    n = len(arrs)

    def plan(ins, outs, sems):
        send_sems, recv_sems, local_sems = sems
        x, y, c = _mesh_pos()
        me, sibling = (x, y, c), (x, y, 1 - c)
        chips = [(1 - x, y), (x, 1 - y), (1 - x, 1 - y)]

        def rows(a, px, py, pc):
            r = ins[a].shape[0]
            return outs[a].at[pl.ds(pl.multiple_of((4 * px + 2 * py + pc) * r, 8), r), :]

        def copy(a, k, block, to, src=None):
            return pltpu.make_async_remote_copy(
                src_ref=rows(a, *block) if src is None else src, dst_ref=rows(a, *block),
                send_sem=send_sems.at[a * 7 + k], recv_sem=recv_sems.at[a * 7 + k],
                device_id=to, device_id_type=MESH)

        local = [pltpu.make_async_copy(ins[a], rows(a, *me), local_sems.at[a]) for a in range(n)]
        first = []
        for a in range(n):
            first.append(copy(a, 0, me, sibling, src=ins[a]))
            first += [copy(a, 1 + j, me, (*chip, c), src=ins[a]) for j, chip in enumerate(chips)]
        return c, me, sibling, chips, copy, local, first

    def start(ins, outs, sems):
        *_, local, first = plan(ins, outs, sems)
        for cp in local + first:
            cp.start()

    def finish(ins, outs, sems):
        c, me, sibling, chips, copy, local, first = plan(ins, outs, sems)
        passed = []
        for a in range(n):
            for j, chip in enumerate(chips):
                copy(a, 1 + j, (*chip, c), me).wait_recv()
                fwd = copy(a, 4 + j, (*chip, c), sibling)
                fwd.start()
                passed.append(fwd)
        for a in range(n):
            copy(a, 0, sibling, me).wait_recv()
            for j, chip in enumerate(chips):
                copy(a, 4 + j, (*chip, 1 - c), me).wait_recv()
        for cp in first + passed:
            cp.wait_send()
        for mine in local:
            mine.wait()

    return _Rider(arrs, [jax.ShapeDtypeStruct((N_DEV * a.shape[0], a.shape[1]), a.dtype) for a in arrs],
                  [pltpu.SemaphoreType.DMA((7 * n,)), pltpu.SemaphoreType.DMA((7 * n,)),
                   pltpu.SemaphoreType.DMA((n,))], start, finish)


def _piece(ref, col_sharded, k, h):
    m, n = ref.shape
    if col_sharded:
        mh, nc = m // 2, n // N_CHIP
        return ref.at[pl.ds(pl.multiple_of(h * mh, 16), mh), pl.ds(pl.multiple_of(k * nc, 128), nc)]
    rp = m // (2 * N_CHIP)
    return ref.at[pl.ds(pl.multiple_of((2 * k + h) * rp, 16), rp), :]


def _piece_shape(shape, col_sharded):
    m, n = shape
    return (m // 2, n // N_CHIP) if col_sharded else (m // (2 * N_CHIP), n)


def _cast_bf16(arrs, col_flags, k_idx, name, after=None):
    n = len(arrs)
    extra = [] if after is None else [after]

    def body(k_ref, *refs):
        outs = refs[n + len(extra):]
        for a in range(n):
            val = refs[a][...].astype(BF16)
            outs[a][...] = val
            outs[n + a][...] = val

    whole = [pl.BlockSpec(a.shape, lambda i, k_ref: (0, 0)) for a in arrs]
    window = [pl.BlockSpec(a.shape, (lambda i, k_ref: (0, k_ref[0])) if col else (lambda i, k_ref: (k_ref[0], 0)))
              for a, col in zip(arrs, col_flags)]
    res = pl.pallas_call(
        body, name=name,
        grid_spec=pltpu.PrefetchScalarGridSpec(
            num_scalar_prefetch=1, grid=(1,),
            in_specs=whole + [pl.BlockSpec(t.shape, lambda i, k_ref: (0, 0)) for t in extra],
            out_specs=whole + window),
        out_shape=[jax.ShapeDtypeStruct(a.shape, BF16) for a in arrs]
        + [jax.ShapeDtypeStruct(fs, BF16) for fs in _full_shapes(arrs, col_flags)],
        compiler_params=pltpu.CompilerParams(vmem_limit_bytes=VMEM_LIMIT_BYTES))(k_idx, *arrs, *extra)
    return list(res[:n]), list(res[n:])


def _full_shapes(shards, col_flags):
    return [(s.shape[0], s.shape[1] * N_CHIP) if col else (s.shape[0] * N_CHIP, s.shape[1])
            for s, col in zip(shards, col_flags)]


def _ici_copies(shard_refs, full_refs, send_sems, recv_sems, col_flags):
    x, y, c = _mesh_pos()
    k_me = 2 * x + y
    copies = []
    for a, (s_ref, f_ref) in enumerate(zip(shard_refs, full_refs)):
        rows = s_ref.shape[0] // 2
        src = s_ref.at[pl.ds(pl.multiple_of(c * rows, 16), rows), :]
        for j, chip in enumerate([(1 - x, y), (x, 1 - y), (1 - x, 1 - y)]):
            copies.append(pltpu.make_async_remote_copy(
                src_ref=src, dst_ref=_piece(f_ref, col_flags[a], k_me, c),
                send_sem=send_sems.at[a * 3 + j], recv_sem=recv_sems.at[a * 3 + j],
                device_id=(*chip, c), device_id_type=MESH))
    return copies


def _split_begin(bufs, n_sems, make_copies, after, name):
    n = len(bufs)
    hbm = pl.BlockSpec(memory_space=pltpu.HBM)
    sem = pl.BlockSpec(memory_space=pltpu.SEMAPHORE)

    def body(*refs):
        for cp in make_copies(refs[:n], refs[n + 1], refs[n + 2]):
            cp.start()
        refs[-1][...] = jnp.zeros_like(refs[-1])

    args = [pltpu.with_memory_space_constraint(t, pltpu.HBM) for t in bufs]
    res = pl.pallas_call(
        body, name=name,
        out_shape=[pltpu.SemaphoreType.DMA((n_sems,)), pltpu.SemaphoreType.DMA((n_sems,))]
        + [pltpu.HBM(t.shape, t.dtype) for t in args] + [jax.ShapeDtypeStruct((8, HEAD), F32)],
        in_specs=[hbm] * n + [pl.BlockSpec(memory_space=pl.ANY)],
        out_specs=[sem, sem] + [hbm] * n + [pl.BlockSpec(memory_space=pltpu.VMEM)],
        input_output_aliases={i: 2 + i for i in range(n)},
        compiler_params=pltpu.CompilerParams(has_side_effects=pltpu.SideEffectType.DATAFLOW_SIDE_EFFECTING),
    )(*args, after)
    return res[0], res[1], list(res[2:2 + n]), res[-1]


def _split_end(handle, make_copies, after, name):
    send_sems, recv_sems, bufs, _ = handle
    n = len(bufs)
    hbm = pl.BlockSpec(memory_space=pltpu.HBM)
    sem = pl.BlockSpec(memory_space=pltpu.SEMAPHORE)

    def body(*refs):
        for cp in make_copies(refs[:n], refs[n], refs[n + 1]):
            cp.wait_send()
            cp.wait_recv()

    res = pl.pallas_call(
        body, name=name,
        out_shape=[pltpu.HBM(t.shape, t.dtype) for t in bufs],
        in_specs=[hbm] * n + [sem, sem, pl.BlockSpec(memory_space=pl.ANY)],
        out_specs=[hbm] * n,
        input_output_aliases={i: i for i in range(n)},
        compiler_params=pltpu.CompilerParams(has_side_effects=pltpu.SideEffectType.DATAFLOW_SIDE_EFFECTING),
    )(*bufs, send_sems, recv_sems, after)
    return list(res)


def _gather_copies(n, col_flags):
    return lambda refs, send_sems, recv_sems: _ici_copies(refs[:n], refs[n:], send_sems, recv_sems, col_flags)


def _gather_begin(shards, fulls, col_flags, after, name):
    n = len(shards)
    return _split_begin(list(shards) + list(fulls), 3 * n, _gather_copies(n, col_flags), after, name)


def _gather_end(handle, col_flags, after, name):
    n = len(handle[2]) // 2
    return _split_end(handle, _gather_copies(n, col_flags), after, name)[n:]


def _sibling_copies(col_flag):
    def make(refs, send_sems, recv_sems):
        grad_ref, land_ref = refs
        x, y, c = _mesh_pos()
        return [pltpu.make_async_remote_copy(
            src_ref=_piece(grad_ref, col_flag, k, 1 - c), dst_ref=land_ref.at[k],
            send_sem=send_sems.at[k], recv_sem=recv_sems.at[k],
            device_id=(x, y, 1 - c), device_id_type=MESH) for k in range(N_CHIP)]
    return make


def _chips_copies(n):
    def make(refs, send_sems, recv_sems):
        parts, landed = refs[:n], refs[n:]
        x, y, c = _mesh_pos()
        k_me = 2 * x + y
        copies = []
        for a in range(n):
            for j, chip in enumerate([(1 - x, y), (x, 1 - y), (1 - x, 1 - y)]):
                copies.append(pltpu.make_async_remote_copy(
                    src_ref=parts[a].at[2 * chip[0] + chip[1]], dst_ref=landed[a].at[k_me],
                    send_sem=send_sems.at[a * 3 + j], recv_sem=recv_sems.at[a * 3 + j],
                    device_id=(*chip, c), device_id_type=MESH))
        return copies
    return make


def _chips_begin(parts, after, name):
    landed = [lax.empty(p.shape, p.dtype) for p in parts]
    return _split_begin(list(parts) + landed, 3 * len(parts), _chips_copies(len(parts)), after, name)


def _chips_end(handle, after, name):
    n = len(handle[2]) // 2
    res = _split_end(handle, _chips_copies(n), after, name)
    return res[:n], res[n:]


def _small_rows(buf_ref, px, py, pc):
    r = buf_ref.shape[0] // N_DEV
    return buf_ref.at[pl.ds(pl.multiple_of((4 * px + 2 * py + pc) * r, 8), r), :]


def _small_ici_copies(refs, send_sems, recv_sems):
    x, y, c = _mesh_pos()
    copies = []
    for a, buf in enumerate(refs):
        mine = _small_rows(buf, x, y, c)
        for j, chip in enumerate([(1 - x, y), (x, 1 - y), (1 - x, 1 - y)]):
            copies.append(pltpu.make_async_remote_copy(
                src_ref=mine, dst_ref=mine, send_sem=send_sems.at[a * 3 + j], recv_sem=recv_sems.at[a * 3 + j],
                device_id=(*chip, c), device_id_type=MESH))
    return copies


def _small_finish(bufs, name):
    n = len(bufs)

    def body(*refs):
        buf_refs = refs[n:2 * n]
        send_sems, recv_sems = refs[2 * n:]
        x, y, c = _mesh_pos()
        owners = [(x, y), (1 - x, y), (x, 1 - y), (1 - x, 1 - y)]
        passed, arriving = [], []
        for a in range(n):
            for j, (px, py) in enumerate(owners):
                for pc, group in ((c, passed), (1 - c, arriving)):
                    rows = _small_rows(buf_refs[a], px, py, pc)
                    group.append(pltpu.make_async_remote_copy(
                        src_ref=rows, dst_ref=rows, send_sem=send_sems.at[a * 4 + j],
                        recv_sem=recv_sems.at[a * 4 + j], device_id=(x, y, 1 - c), device_id_type=MESH))
        for cp in passed:
            cp.start()
        for cp in arriving:
            cp.wait_recv()
        for cp in passed:
            cp.wait_send()

    anyspec = pl.BlockSpec(memory_space=pl.ANY)
    return pl.pallas_call(
        body, name=name, out_shape=[jax.ShapeDtypeStruct(b.shape, b.dtype) for b in bufs],
        in_specs=[anyspec] * n, out_specs=[anyspec] * n, input_output_aliases={a: a for a in range(n)},
        scratch_shapes=[pltpu.SemaphoreType.DMA((4 * n,)), pltpu.SemaphoreType.DMA((4 * n,))],
    )(*bufs)


def _sibling_begin(grad, col_flag, after, name):
    land = lax.empty((N_CHIP,) + _piece_shape(grad.shape, col_flag), grad.dtype)
    return _split_begin([grad, land], N_CHIP, _sibling_copies(col_flag), after, name)


def _sibling_end(handle, col_flag, after, name):
    return _split_end(handle, _sibling_copies(col_flag), after, name)


def _gather_finish(fulls, col_flags, name):
    n = len(fulls)

    def body(*refs):
        full_refs = refs[n:2 * n]
        send_sems, recv_sems = refs[2 * n:]
        x, y, c = _mesh_pos()
        passed, arriving = [], []
        for a in range(n):
            for j, chip in enumerate([(1 - x, y), (x, 1 - y), (1 - x, 1 - y)]):
                k_from = 2 * chip[0] + chip[1]
                for h, group in ((c, passed), (1 - c, arriving)):
                    win = _piece(full_refs[a], col_flags[a], k_from, h)
                    group.append(pltpu.make_async_remote_copy(
                        src_ref=win, dst_ref=win, send_sem=send_sems.at[a * 3 + j],
                        recv_sem=recv_sems.at[a * 3 + j], device_id=(x, y, 1 - c), device_id_type=MESH))
        for cp in passed:
            cp.start()
        for cp in arriving:
            cp.wait_recv()
        for cp in passed:
            cp.wait_send()

    anyspec = pl.BlockSpec(memory_space=pl.ANY)
    return pl.pallas_call(
        body, name=name,
        out_shape=[jax.ShapeDtypeStruct(f.shape, f.dtype) for f in fulls],
        in_specs=[anyspec] * n, out_specs=[anyspec] * n,
        input_output_aliases={a: a for a in range(n)},
        scratch_shapes=[pltpu.SemaphoreType.DMA((3 * n,)), pltpu.SemaphoreType.DMA((3 * n,))],
    )(*fulls)


def _sibling_rider(grads, col_flags):
    n = len(grads)
    pshapes = [_piece_shape(g.shape, col) for g, col in zip(grads, col_flags)]

    def copies(ins, outs, sems):
        send_sems, recv_sems = sems
        x, y, c = _mesh_pos()
        return [pltpu.make_async_remote_copy(
            src_ref=_piece(ins[a], col_flags[a], k, 1 - c), dst_ref=outs[a].at[k],
            send_sem=send_sems.at[a * N_CHIP + k], recv_sem=recv_sems.at[a * N_CHIP + k],
            device_id=(x, y, 1 - c), device_id_type=MESH) for a in range(n) for k in range(N_CHIP)]

    def start(ins, outs, sems):
        for cp in copies(ins, outs, sems):
            cp.start()

    def finish(ins, outs, sems):
        cps = copies(ins, outs, sems)
        for cp in cps:
            cp.wait_recv()
        for cp in cps:
            cp.wait_send()

    return _Rider(grads, [jax.ShapeDtypeStruct((N_CHIP,) + ps, g.dtype) for ps, g in zip(pshapes, grads)],
                  [pltpu.SemaphoreType.DMA((N_CHIP * n,)), pltpu.SemaphoreType.DMA((N_CHIP * n,))], start, finish)


def _sum_with_sibling(grads, landed, col_flags, c_idx, name):
    n = len(grads)
    pshapes = [_piece_shape(g.shape, col) for g, col in zip(grads, col_flags)]

    def body(c_ref, *refs):
        ins, lands, outs = refs[:n], refs[n:2 * n], refs[2 * n:]
        for a in range(n):
            outs[a][0] = (ins[a][...] + lands[a][0]).astype(BF16)

    in_specs = []
    for ps, col in zip(pshapes, col_flags):
        if col:
            in_specs.append(pl.BlockSpec(ps, lambda k, c_ref: (c_ref[0], k)))
        else:
            in_specs.append(pl.BlockSpec(ps, lambda k, c_ref: (2 * k + c_ref[0], 0)))
    land_specs = [pl.BlockSpec((1,) + ps, lambda k, c_ref: (k, 0, 0)) for ps in pshapes]
    return pl.pallas_call(
        body, name=name,
        grid_spec=pltpu.PrefetchScalarGridSpec(
            num_scalar_prefetch=1, grid=(N_CHIP,),
            in_specs=in_specs + land_specs, out_specs=land_specs),
        out_shape=[jax.ShapeDtypeStruct((N_CHIP,) + ps, BF16) for ps in pshapes],
        compiler_params=pltpu.CompilerParams(dimension_semantics=("arbitrary",), vmem_limit_bytes=VMEM_LIMIT_BYTES),
    )(c_idx, *grads, *landed)


def _chips_rider(parts):
    n = len(parts)

    def plan(ins, outs, sems, arriving):
        send_sems, recv_sems = sems
        x, y, c = _mesh_pos()
        k_me = 2 * x + y
        copies = []
        for a in range(n):
            for j, chip in enumerate([(1 - x, y), (x, 1 - y), (1 - x, 1 - y)]):
                k_peer = 2 * chip[0] + chip[1]
                copies.append(pltpu.make_async_remote_copy(
                    src_ref=ins[a].at[k_peer], dst_ref=outs[a].at[k_peer if arriving else k_me],
                    send_sem=send_sems.at[a * 3 + j], recv_sem=recv_sems.at[a * 3 + j],
                    device_id=(*chip, c), device_id_type=MESH))
        return copies

    def start(ins, outs, sems):
        for cp in plan(ins, outs, sems, False):
            cp.start()

    def finish(ins, outs, sems):
        arrivals = plan(ins, outs, sems, True)
        for cp in arrivals:
            cp.wait_recv()
        for cp in arrivals:
            cp.wait_send()

    return _Rider(parts, [jax.ShapeDtypeStruct(p.shape, p.dtype) for p in parts],
                  [pltpu.SemaphoreType.DMA((3 * n,)), pltpu.SemaphoreType.DMA((3 * n,))], start, finish)


def _sum_chips_and_share(landed, parts, after, name):
    n = len(landed)

    def body(*refs):
        ins, own, outs, red = refs[:n], refs[n:2 * n], refs[2 * n + 1:3 * n + 1], refs[3 * n + 1:4 * n + 1]
        send_sems, recv_sems, local_sems = refs[4 * n + 1:]
        x, y, c = _mesh_pos()
        sibling = (x, y, 1 - c)
        k_me = 2 * x + y
        copies, local = [], []
        for a in range(n):
            for k in range(N_CHIP):
                @pl.when(k_me == k)
                def _():
                    term = own[a][k].astype(F32)
                    red[a][...] = term if k == 0 else red[a][...] + term

                @pl.when(k_me != k)
                def _():
                    term = ins[a][k].astype(F32)
                    red[a][...] = term if k == 0 else red[a][...] + term

            mine = pltpu.make_async_copy(red[a], outs[a].at[c], local_sems.at[a])
            mine.start()
            local.append(mine)
            cp = pltpu.make_async_remote_copy(
                src_ref=red[a], dst_ref=outs[a].at[c],
                send_sem=send_sems.at[a], recv_sem=recv_sems.at[a],
                device_id=sibling, device_id_type=MESH)
            cp.start()
            copies.append(cp)
        for a in range(n):
            pltpu.make_async_remote_copy(
                src_ref=red[a], dst_ref=outs[a].at[1 - c],
                send_sem=send_sems.at[a], recv_sem=recv_sems.at[a],
                device_id=sibling, device_id_type=MESH).wait_recv()
        for cp in copies:
            cp.wait_send()
        for mine in local:
            mine.wait()

    return pl.pallas_call(
        body, name=name,
        out_shape=[jax.ShapeDtypeStruct((2,) + l.shape[1:], F32) for l in landed],
        in_specs=[pl.BlockSpec(memory_space=pltpu.VMEM)] * (2 * n + 1),
        out_specs=[pl.BlockSpec(memory_space=pl.ANY)] * n,
        scratch_shapes=[pltpu.VMEM(l.shape[1:], F32) for l in landed]
        + [pltpu.SemaphoreType.DMA((n,)), pltpu.SemaphoreType.DMA((n,)), pltpu.SemaphoreType.DMA((n,))],
        compiler_params=pltpu.CompilerParams(vmem_limit_bytes=VMEM_LIMIT_BYTES),
    )(*landed, *parts, after)


def _mod_shard(c_all, w_ada, b_shard, after):
    def body(c_ref, w_ref, b_ref, after_ref, o_ref, sc_ref):
        cc = c_ref[...]
        sc = cc * _sigmoid(cc)
        sc_ref[...] = sc
        o_ref[...] = _dot(sc, w_ref[...]) + b_ref[...]

    nb, d = c_all.shape
    nn = w_ada.shape[1]
    vm = pl.BlockSpec(memory_space=pltpu.VMEM)
    return pl.pallas_call(
        body, name="mod_shard",
        in_specs=[vm, vm, vm, pl.BlockSpec(memory_space=pl.ANY)], out_specs=[vm, vm],
        out_shape=[jax.ShapeDtypeStruct((nb, nn), F32), jax.ShapeDtypeStruct((nb, d), F32)],
        compiler_params=pltpu.CompilerParams(vmem_limit_bytes=VMEM_LIMIT_BYTES),
    )(c_all, w_ada, b_shard, after)


V_SH_M, V_SC_M, V_G_M, V_SH_F, V_SC_F, V_G_F, V_PRE_MIX, V_POST_MIX, V_PRE_FFN, V_POST_FFN = range(10)


def _vrow(vec_ref, r):
    return vec_ref[r:r + 1, :]


def _mix_fwd(x, vecs, w_in_b, w_out_b, sgu_g, wm_b, bsb, wp_b, ps, pmats, ts):
    s_len, d = x.shape
    nt = s_len // ts

    def body(x_ref, vec_ref, win_ref, wout_ref, sg_ref, wm_ref, bs_ref, wp_ref, ps_ref, pm_ref,
             h1_ref, proj_ref, cat_ref, mixed_ref, x2_ref, h2_ref, gact_ref, carry_ref):
        i = pl.program_id(0)

        @pl.when(i == 0)
        def _():
            carry_ref[...] = jnp.zeros_like(carry_ref)

        sub = FFN_TS
        nsub = ts // sub

        def project(s):
            rs = slice(s * sub, (s + 1) * sub)
            x = x_ref[rs, :]
            h1 = ((x * _rms(x)) * (_vrow(vec_ref, V_PRE_MIX) * (1.0 + _vrow(vec_ref, V_SC_M)))
                  + _vrow(vec_ref, V_SH_M)).astype(BF16)
            h1_ref[rs, :] = h1
            proj = _dot(h1, win_ref[...])
            proj_ref[rs, :] = proj[:, 2 * A_WIDTH:].astype(BF16)
            return proj

        def mix(s, proj, halo):
            r0 = s * sub
            t_glob = lax.broadcasted_iota(jnp.int32, (sub, HEAD), 0) + (i * ts + r0)
            for h in range(N_HEADS):
                hs = slice(h * HEAD, (h + 1) * HEAD)
                u, u_grad = _gelu_and_grad(proj[:, hs])
                v, v_grad = _gelu_and_grad(proj[:, A_WIDTH + h * HEAD:A_WIDTH + (h + 1) * HEAD])
                for j, val in enumerate((u, v, u_grad, v_grad)):
                    gact_ref[r0:r0 + sub, j * A_WIDTH + h * HEAD:j * A_WIDTH + (h + 1) * HEAD] = val.astype(BF16)
                vn = ((v * _rms(v)) * sg_ref[h:h + 1, :]).astype(BF16)
                for b in range(sub // HEAD):
                    rs = slice(b * HEAD, (b + 1) * HEAD)
                    z = _dot(wm_ref[h], vn[rs]) + bs_ref[h]
                    cat_ref[r0 + b * HEAD:r0 + (b + 1) * HEAD, h * HEAD:(h + 1) * HEAD] = (u[rs] * z).astype(BF16)
            for g in range(len(POOL_WINDOWS)):
                gs = slice(g * HEAD, (g + 1) * HEAD)
                p = proj[:, 2 * A_WIDTH + g * HEAD:2 * A_WIDTH + (g + 1) * HEAD]
                pooled = _pool_fwd(p, halo[:, gs], g, t_glob)
                yb = _dot(pooled.astype(BF16), wp_ref[g]) * ps_ref[0:1, gs]
                cat_ref[r0:r0 + sub, A_WIDTH + g * HEAD:A_WIDTH + (g + 1) * HEAD] = yb.astype(BF16)

        def finish(s, mixed):
            rs = slice(s * sub, (s + 1) * sub)
            mixed_ref[rs, :] = mixed
            x2 = x_ref[rs, :] + (mixed * _rms(mixed)) * (_vrow(vec_ref, V_G_M) * _vrow(vec_ref, V_POST_MIX))
            x2_ref[rs, :] = x2
            h2 = ((x2 * _rms(x2)) * (_vrow(vec_ref, V_PRE_FFN) * (1.0 + _vrow(vec_ref, V_SC_F)))
                  + _vrow(vec_ref, V_SH_F)).astype(BF16)
            h2_ref[rs, :] = _permute_bf16(pm_ref[0], h2)

        projs = [project(0)]
        halo = carry_ref[...]
        mixed_prev = None
        for s in range(nsub):
            if s + 1 < nsub:
                projs.append(project(s + 1))
            mix(s, projs[s], halo)
            halo = projs[s][sub - POOL_HALO:sub, 2 * A_WIDTH:]
            mixed = _dot(cat_ref[s * sub:(s + 1) * sub, :], wout_ref[...])
            if mixed_prev is not None:
                finish(s - 1, mixed_prev)
            mixed_prev = mixed
        carry_ref[...] = halo
        finish(nsub - 1, mixed_prev)

    row = lambda i: (i, 0)
    return _call(
        body, name="mix_fwd", grid=(nt,),
        in_specs=[_tiled((ts, d), row), _whole(vecs.shape), _resident(w_in_b.shape), _resident(w_out_b.shape),
                  _whole(sgu_g.shape), _whole(wm_b.shape), _whole(bsb.shape), _whole(wp_b.shape), _whole(ps.shape),
                  _whole(pmats.shape)],
        out_specs=[_tiled((ts, d), row), _tiled((ts, A_WIDTH), row), _tiled((ts, d), row),
                   _tiled((ts, d), row), _tiled((ts, d), row), _tiled((ts, d), row), _tiled((ts, 4 * A_WIDTH), row)],
        out_shape=[jax.ShapeDtypeStruct((s_len, d), BF16), jax.ShapeDtypeStruct((s_len, A_WIDTH), BF16),
                   jax.ShapeDtypeStruct((s_len, d), BF16), jax.ShapeDtypeStruct((s_len, d), F32),
                   jax.ShapeDtypeStruct((s_len, d), F32), jax.ShapeDtypeStruct((s_len, d), BF16),
                   jax.ShapeDtypeStruct((s_len, 4 * A_WIDTH), BF16)],
        scratch_shapes=[pltpu.VMEM((POOL_HALO, A_WIDTH), F32)],
        args=(x, vecs, w_in_b, w_out_b, sgu_g, wm_b, bsb, wp_b, ps, pmats))


def _perm_mats(ts):
    p = jnp.arange(ts)
    pm = (((p % SUBLANES) * (ts // SUBLANES) + p // SUBLANES)[:, None] == p[None, :]).astype(BF16)
    return jnp.stack([pm, pm.T])


def _permute_bf16(pm, xb):
    return _dot(pm, xb).astype(BF16)


def _permute_f32(pm, x):
    hi = x.astype(BF16)
    lo = (x - hi.astype(F32)).astype(BF16)
    return _dot(pm, hi) + _dot(pm, lo)


def _conv_out(u, um2, um1, cv_ref, cols):
    return (cv_ref[3:4, cols] + um2 * cv_ref[0:1, cols] + um1 * cv_ref[1:2, cols] + u * cv_ref[2:3, cols])


F_LOSS, F_DGF, F_DPOSTFFN = 0, 1, 2
B_DSHF, B_DSCF, B_DPREFFN, B_DGM, B_DPOSTMIX = 0, 1, 2, 3, 4
M_DSHM, M_DSCM, M_DPREMIX = 0, 1, 2
C_DCB, C_DCW = 0, 1
G1_F, G1_B, G1_M = 0, 8, 16


def _ffn_fwd(h2p, x2, tgt, w_up_b, w_down_b, cvec, vecs, pmats, ts):
    s_len, d = x2.shape
    ff2 = w_up_b.shape[1]
    ff = ff2 // 2
    nt = s_len // ts
    chunks = _ff_chunks(ff)

    def body(h2_ref, x2_ref, t_ref, wu_ref, wd_ref, cv_ref, vec_ref, pm_ref,
             up_ref, y_ref, act_ref, dy_ref, df_ref, acc_ref, carry_ref):
        @pl.when(pl.program_id(0) == 0)
        def _():
            carry_ref[...] = jnp.zeros_like(carry_ref)
            acc_ref[...] = jnp.zeros_like(acc_ref)

        h2v = h2_ref[...]

        def up_dots(o, w):
            return [_dot(h2v, wu_ref[:, base + o:base + o + w]) for base in (0, ff)]

        f = None
        pending = None
        nxt = up_dots(*chunks[0])
        for ci, (o, w) in enumerate(chunks):
            us = nxt
            if ci + 1 < len(chunks):
                nxt = up_dots(*chunks[ci + 1])
            if pending is not None:
                part = _dot(pending[0], wd_ref[pending[1]:pending[1] + pending[2], :])
                f = part if f is None else f + part
            sub0 = lax.broadcasted_iota(jnp.int32, (SUBLANES, w), 0) == 0
            ys = []
            for base, u in zip((0, ff), us):
                cols = slice(base + o, base + o + w)
                up_ref[:, cols] = u.astype(BF16)
                last1, last2 = u[ts - SUBLANES:ts], u[ts - CONV_KEEP:ts - SUBLANES]
                b1 = jnp.where(sub0, pltpu.roll(carry_ref[SUBLANES:CONV_KEEP, cols], 1, 0), pltpu.roll(last1, 1, 0))
                b2 = jnp.where(sub0, pltpu.roll(carry_ref[0:SUBLANES, cols], 1, 0), pltpu.roll(last2, 1, 0))
                um1 = jnp.concatenate([b1, u[:ts - SUBLANES]], axis=0)
                um2 = jnp.concatenate([b2, b1, u[:ts - CONV_KEEP]], axis=0)
                ys.append(_conv_out(u, um2, um1, cv_ref, cols))
                carry_ref[:, cols] = u[ts - CONV_KEEP:ts]
            gate, val = ys
            sg = _sigmoid(gate)
            gs = gate * sg
            act = (gs * val).astype(BF16)
            act_ref[:, o:o + w] = act
            y_ref[:, o:o + w] = (val * (sg + gs * (1.0 - sg))).astype(BF16)
            y_ref[:, ff + o:ff + o + w] = gs.astype(BF16)
            pending = (act, o, w)
        f = f + _dot(pending[0], wd_ref[pending[1]:pending[1] + pending[2], :])
        f = _permute_f32(pm_ref[1], f)
        r3 = _rms(f)
        fhat = f * r3
        post = _vrow(vec_ref, V_POST_FFN)
        g_f = _vrow(vec_ref, V_G_F)
        e = (x2_ref[...] + fhat * (g_f * post)) - t_ref[...]
        dy = e * (1.0 / d)
        dy_ref[...] = dy
        acc_ref[F_LOSS:F_LOSS + 1, :] += _colsum(e * e)
        acc_ref[F_DGF:F_DGF + 1, :] += _colsum(dy * fhat)
        dfhat = dy * (g_f * post)
        df = (r3 * (dfhat - fhat * _rowmean(dfhat * fhat))).astype(BF16)
        df_ref[...] = _permute_bf16(pm_ref[0], df)

        @pl.when(pl.program_id(0) == nt - 1)
        def _():
            through_norm = acc_ref[F_DGF:F_DGF + 1, :]
            acc_ref[F_DGF:F_DGF + 1, :] = through_norm * post
            acc_ref[F_DPOSTFFN:F_DPOSTFFN + 1, :] = through_norm * g_f

    row = lambda i: (i, 0)
    return pl.pallas_call(
        body, name="ffn_fwd", grid=(nt,),
        in_specs=[_tiled((ts, d), row), _tiled((ts, d), row), _tiled((ts, d), row), _resident(w_up_b.shape),
                  _resident(w_down_b.shape), _whole(cvec.shape), _whole(vecs.shape), _whole(pmats.shape)],
        out_specs=[_tiled((ts, ff2), row), _tiled((ts, ff2), row), _tiled((ts, ff), row), _tiled((ts, d), row),
                   _tiled((ts, d), row), _whole((8, d))],
        out_shape=[jax.ShapeDtypeStruct((s_len, ff2), BF16), jax.ShapeDtypeStruct((s_len, ff2), BF16),
                   jax.ShapeDtypeStruct((s_len, ff), BF16), jax.ShapeDtypeStruct((s_len, d), F32),
                   jax.ShapeDtypeStruct((s_len, d), BF16), jax.ShapeDtypeStruct((8, d), F32)],
        scratch_shapes=[pltpu.VMEM((CONV_KEEP, ff2), F32)],
        compiler_params=_seq_params(),
    )(h2p, x2, tgt, w_up_b, w_down_b, cvec, vecs, pmats)


def _ffn_bwd(dfp, upp, yp, x2, dy, mixed, w_up_b, w_down_b, cvec, vecs, pmats, ts, rider=None):
    s_len, d = x2.shape
    ff2 = w_up_b.shape[1]
    ff = ff2 // 2
    nt = s_len // ts
    chunks = _ff_chunks(ff, 512)

    def body(df_ref, up_ref, y_ref, x2_ref, dy_ref, mx_ref, wu_ref, wd_ref, cv_ref, vec_ref, pm_ref,
             dup_ref, dx2_ref, dmx_ref, accc_ref, acc_ref, carry_ref):
        @pl.when(pl.program_id(0) == 0)
        def _():
            carry_ref[...] = jnp.zeros_like(carry_ref)
            accc_ref[...] = jnp.zeros_like(accc_ref)
            acc_ref[...] = jnp.zeros_like(acc_ref)

        dfv = df_ref[...]

        def dh2_add(acc, dups, o, w):
            for base, dup in zip((0, ff), dups):
                part = _dot_nt(dup, wu_ref[:, base + o:base + o + w])
                acc = part if acc is None else acc + part
            return acc

        dh2 = None
        pending = None
        nxt = _dot_nt(dfv, wd_ref[chunks[0][0]:chunks[0][0] + chunks[0][1], :])
        for ci, (o, w) in enumerate(chunks):
            dact = nxt
            if ci + 1 < len(chunks):
                o2, w2 = chunks[ci + 1]
                nxt = _dot_nt(dfv, wd_ref[o2:o2 + w2, :])
            if pending is not None:
                dh2 = dh2_add(dh2, *pending)
            sub7 = lax.broadcasted_iota(jnp.int32, (SUBLANES, w), 0) == SUBLANES - 1
            dups = []
            dys = (dact * y_ref[:, o:o + w].astype(F32), dact * y_ref[:, ff + o:ff + o + w].astype(F32))
            for base, dyv in zip((0, ff), dys):
                cols = slice(base + o, base + o + w)
                u = up_ref[:, cols].astype(F32)
                up1 = SUBLANES - 1
                e0 = jnp.where(sub7, pltpu.roll(carry_ref[0:SUBLANES, cols], up1, 0),
                               pltpu.roll(dyv[0:SUBLANES], up1, 0))
                e1 = jnp.where(sub7, pltpu.roll(carry_ref[SUBLANES:CONV_KEEP, cols], up1, 0),
                               pltpu.roll(dyv[SUBLANES:CONV_KEEP], up1, 0))
                dyp1 = jnp.concatenate([dyv[SUBLANES:], e0], axis=0)
                dyp2 = jnp.concatenate([dyv[CONV_KEEP:], e0, e1], axis=0)
                accc_ref[C_DCB:C_DCB + 1, cols] += _colsum(dyv)
                accc_ref[C_DCW + 0:C_DCW + 1, cols] += _colsum(dyp2 * u)
                accc_ref[C_DCW + 1:C_DCW + 2, cols] += _colsum(dyp1 * u)
                accc_ref[C_DCW + 2:C_DCW + 3, cols] += _colsum(dyv * u)
                dup = (dyv * cv_ref[2:3, cols] + dyp1 * cv_ref[1:2, cols] + dyp2 * cv_ref[0:1, cols]).astype(BF16)
                dup_ref[:, cols] = dup
                dups.append(dup)
                carry_ref[:, cols] = dyv[0:CONV_KEEP]
            pending = (dups, o, w)
        dh2 = dh2_add(dh2, *pending)
        dh2 = _permute_f32(pm_ref[1], dh2)
        x2 = x2_ref[...]
        r2 = _rms(x2)
        xn = x2 * r2
        pre = _vrow(vec_ref, V_PRE_FFN)
        one_sc = 1.0 + _vrow(vec_ref, V_SC_F)
        acc_ref[B_DSHF:B_DSHF + 1, :] += _colsum(dh2)
        acc_ref[B_DSCF:B_DSCF + 1, :] += _colsum(dh2 * xn)
        dxn = dh2 * (pre * one_sc)
        dx2 = dy_ref[...] + r2 * (dxn - xn * _rowmean(dxn * xn))
        dx2_ref[...] = dx2
        mixed = mx_ref[...]
        rm = _rms(mixed)
        mhat = mixed * rm
        post = _vrow(vec_ref, V_POST_MIX)
        g_m = _vrow(vec_ref, V_G_M)
        acc_ref[B_DGM:B_DGM + 1, :] += _colsum(dx2 * mhat)
        dmhat = dx2 * (g_m * post)
        dmx_ref[...] = (rm * (dmhat - mhat * _rowmean(dmhat * mhat))).astype(BF16)

        @pl.when(pl.program_id(0) == nt - 1)
        def _():
            through_norm = acc_ref[B_DSCF:B_DSCF + 1, :]
            acc_ref[B_DSCF:B_DSCF + 1, :] = through_norm * pre
            acc_ref[B_DPREFFN:B_DPREFFN + 1, :] = through_norm * one_sc
            through_mix = acc_ref[B_DGM:B_DGM + 1, :]
            acc_ref[B_DGM:B_DGM + 1, :] = through_mix * post
            acc_ref[B_DPOSTMIX:B_DPOSTMIX + 1, :] = through_mix * g_m

    rev = lambda i: (nt - 1 - i, 0)
    return _call(
        body, name="ffn_bwd", grid=(nt,), rider=rider,
        in_specs=[_tiled((ts, d), rev), _tiled((ts, ff2), rev), _tiled((ts, ff2), rev), _tiled((ts, d), rev),
                  _tiled((ts, d), rev), _tiled((ts, d), rev), _resident(w_up_b.shape), _resident(w_down_b.shape),
                  _whole(cvec.shape), _whole(vecs.shape), _whole(pmats.shape)],
        out_specs=[_tiled((ts, ff2), rev), _tiled((ts, d), rev), _tiled((ts, d), rev), _whole((8, ff2)),
                   _whole((8, d))],
        out_shape=[jax.ShapeDtypeStruct((s_len, ff2), BF16), jax.ShapeDtypeStruct((s_len, d), F32),
                   jax.ShapeDtypeStruct((s_len, d), BF16), jax.ShapeDtypeStruct((8, ff2), F32),
                   jax.ShapeDtypeStruct((8, d), F32)],
        scratch_shapes=[pltpu.VMEM((CONV_KEEP, ff2), F32)],
        args=(dfp, upp, yp, x2, dy, mixed, w_up_b, w_down_b, cvec, vecs, pmats))


def _mix_bwd(dmixed, dx2, x, proj, gact, vecs, w_in_b, w_out_b, sgu_g, wm_b, wmt_b, bsb, wp_b, ps, ts):
    s_len, d = x.shape
    nt = s_len // ts
    n_proj = w_in_b.shape[1]
    per = ts // POOL_HALO

    def body(dmx_ref, dx2_ref, x_ref, proj_ref, projh_ref, gact_ref, vec_ref, win_ref, wout_ref, sg_ref, wm_ref,
             wmt_ref, bs_ref, wp_ref, ps_ref,
             gx_ref, dproj_ref, acc_ref, dwm_out, dwp_out, db_ref, dg_ref, dps_ref,
             carry_ref, dwm_ref, dwp_ref, dbz_ref):
        i = pl.program_id(0)
        tile = nt - 1 - i

        @pl.when(i == 0)
        def _():
            carry_ref[...] = jnp.zeros_like(carry_ref)
            for r in (acc_ref, dwm_ref, dwp_ref, dbz_ref, dg_ref, dps_ref):
                r[...] = jnp.zeros_like(r)

        sub = FFN_TS
        nsub = ts // sub
        ext = sub + POOL_HALO

        def cotangent(s):
            return _dot_nt(dmx_ref[s * sub:(s + 1) * sub, :], wout_ref[...])

        def back(s, dcat, halo, later):
            r0 = s * sub
            rows = slice(r0, r0 + sub)
            t_glob = lax.broadcasted_iota(jnp.int32, (sub, HEAD), 0) + (tile * ts + r0)
            for h in range(N_HEADS):
                hs = slice(h * HEAD, (h + 1) * HEAD)
                vs = slice(A_WIDTH + h * HEAD, A_WIDTH + (h + 1) * HEAD)
                gain = sg_ref[h:h + 1, :]
                for b in range(sub // HEAD):
                    blk = slice(r0 + b * HEAD, r0 + (b + 1) * HEAD)
                    u, v, u_grad, v_grad = [
                        gact_ref[blk, j * A_WIDTH + h * HEAD:j * A_WIDTH + (h + 1) * HEAD].astype(F32)
                        for j in range(4)]
                    rv = _rms(v)
                    vhat = v * rv
                    vn = (vhat * gain).astype(BF16)
                    dout = dcat[b * HEAD:(b + 1) * HEAD, hs]
                    z = _dot(wm_ref[h], vn) + bs_ref[h]
                    dz = dout * u
                    dbz_ref[h] += dz
                    dzb = dz.astype(BF16)
                    dwm_ref[h] += _dot_nt(dzb, vn)
                    dvn = _dot(wmt_ref[h], dzb)
                    dg_ref[h:h + 1, :] += _colsum(dvn * vhat)
                    dvhat = dvn * gain
                    dv = rv * (dvhat - vhat * _rowmean(dvhat * vhat))
                    dproj_ref[blk, hs] = ((dout * z) * u_grad).astype(BF16)
                    dproj_ref[blk, vs] = (dv * v_grad).astype(BF16)
            firsts = []
            for g in range(len(POOL_WINDOWS)):
                gs = slice(g * HEAD, (g + 1) * HEAD)
                pcols = slice(2 * A_WIDTH + g * HEAD, 2 * A_WIDTH + (g + 1) * HEAD)
                p = proj_ref[rows, gs].astype(F32)
                pb = _pool_fwd(p, halo[:, gs], g, t_glob).astype(BF16)
                dyb = dcat[:, A_WIDTH + g * HEAD:A_WIDTH + (g + 1) * HEAD]
                dps_ref[0:1, gs] += _colsum(dyb * _dot(pb, wp_ref[g]))
                dyl = (dyb * ps_ref[0:1, gs]).astype(BF16)
                dwp_ref[g] += _dot_tn(pb, dyl)
                dpooled = _dot_nt(dyl, wp_ref[g])
                cnt = jnp.minimum(t_glob + 1, POOL_WINDOWS[g]).astype(F32)
                q = dpooled / cnt
                acc = jnp.concatenate([q, later[:, gs]], axis=0)
                for step in range(g + 1):
                    acc = acc + pltpu.roll(acc, ext - (1 << step), 0)
                dproj_ref[rows, pcols] = (acc[:sub] - dpooled).astype(BF16)
                firsts.append(q[0:POOL_HALO])
            return jnp.concatenate(firsts, axis=1)

        def finish(s, dh1):
            rows = slice(s * sub, (s + 1) * sub)
            x = x_ref[rows, :]
            r1 = _rms(x)
            xn = x * r1
            pre = _vrow(vec_ref, V_PRE_MIX)
            one_sc = 1.0 + _vrow(vec_ref, V_SC_M)
            acc_ref[M_DSHM:M_DSHM + 1, :] += _colsum(dh1)
            acc_ref[M_DSCM:M_DSCM + 1, :] += _colsum(dh1 * xn)
            dxn = dh1 * (pre * one_sc)
            gx_ref[rows, :] = dx2_ref[rows, :] + r1 * (dxn - xn * _rowmean(dxn * xn))

        nxt = cotangent(nsub - 1)
        later = carry_ref[...]
        dh1_prev = None
        for s in reversed(range(nsub)):
            dcat = nxt
            if s > 0:
                nxt = cotangent(s - 1)
                halo = proj_ref[s * sub - POOL_HALO:s * sub, :].astype(F32)
            else:
                halo = jnp.where(tile > 0, projh_ref[...].astype(F32), 0.0)
            later = back(s, dcat, halo, later)
            dh1 = _dot_nt(dproj_ref[s * sub:(s + 1) * sub, :], win_ref[...])
            if dh1_prev is not None:
                finish(s + 1, dh1_prev)
            dh1_prev = dh1
        carry_ref[...] = later
        finish(0, dh1_prev)

        @pl.when(i == nt - 1)
        def _():
            through_norm = acc_ref[M_DSCM:M_DSCM + 1, :]
            acc_ref[M_DSCM:M_DSCM + 1, :] = through_norm * _vrow(vec_ref, V_PRE_MIX)
            acc_ref[M_DPREMIX:M_DPREMIX + 1, :] = through_norm * (1.0 + _vrow(vec_ref, V_SC_M))
            dwm_out[...] = dwm_ref[...].astype(BF16)
            dwp_out[...] = dwp_ref[...].astype(BF16)
            db_ref[...] = jnp.zeros_like(db_ref)
            for h in range(N_HEADS):
                db_ref[h:h + 1, :] = jnp.sum(dbz_ref[h].T, axis=0, keepdims=True)

    rev = lambda i: (nt - 1 - i, 0)
    halo_map = lambda i: (jnp.maximum((nt - 1 - i) * per - 1, 0), 0)
    hshape = (N_HEADS, HEAD, HEAD)
    return _call(
        body, name="mix_bwd", grid=(nt,),
        in_specs=[_tiled((ts, d), rev), _tiled((ts, d), rev), _tiled((ts, d), rev), _tiled((ts, A_WIDTH), rev),
                  _tiled((POOL_HALO, A_WIDTH), halo_map), _tiled((ts, 4 * A_WIDTH), rev), _whole(vecs.shape),
                  _resident(w_in_b.shape), _resident(w_out_b.shape), _whole(sgu_g.shape), _whole(wm_b.shape),
                  _whole(wmt_b.shape), _whole(bsb.shape), _whole(wp_b.shape), _whole(ps.shape)],
        out_specs=[_tiled((ts, d), rev), _tiled((ts, n_proj), rev), _whole((8, d)), _whole(hshape), _whole(hshape),
                   _whole((8, HEAD)), _whole((8, HEAD)), _whole((8, A_WIDTH))],
        out_shape=[jax.ShapeDtypeStruct((s_len, d), F32), jax.ShapeDtypeStruct((s_len, n_proj), BF16),
                   jax.ShapeDtypeStruct((8, d), F32), jax.ShapeDtypeStruct(hshape, BF16),
                   jax.ShapeDtypeStruct(hshape, BF16), jax.ShapeDtypeStruct((8, HEAD), F32),
                   jax.ShapeDtypeStruct((8, HEAD), F32), jax.ShapeDtypeStruct((8, A_WIDTH), F32)],
        scratch_shapes=[pltpu.VMEM((POOL_HALO, A_WIDTH), F32), pltpu.VMEM(hshape, F32), pltpu.VMEM(hshape, F32),
                        pltpu.VMEM(hshape, F32)],
        args=(dmixed, dx2, x, proj, proj, gact, vecs, w_in_b, w_out_b, sgu_g, wm_b, wmt_b, bsb, wp_b, ps))


def _wgrad(a, b, tn, ts, name, rider=None):
    s_len, m = a.shape
    n = b.shape[1]
    ts = min(ts, s_len)

    def body(a_ref, b_ref, o_ref):
        @pl.when(pl.program_id(1) == 0)
        def _():
            o_ref[...] = jnp.zeros_like(o_ref)

        o_ref[...] += _dot_tn(a_ref[...], b_ref[...])

    (g,), r_out = _call(
        body, name=name, grid=(n // tn, s_len // ts), rider=rider,
        in_specs=[pl.BlockSpec((ts, m), lambda j, s: (s, 0)), pl.BlockSpec((ts, tn), lambda j, s: (s, j))],
        out_specs=[pl.BlockSpec((m, tn), lambda j, s: (0, j))],
        out_shape=[jax.ShapeDtypeStruct((m, n), F32)], scratch_shapes=[], args=(a, b))
    return g, r_out


def _wgrad_pair(a1, b1, a2, b2, after, ts, name):
    s_len = a1.shape[0]
    ts = min(ts, s_len)
    shapes = [(a1.shape[1], b1.shape[1]), (a2.shape[1], b2.shape[1])]

    def body(a1_ref, b1_ref, a2_ref, b2_ref, after_ref, o1_ref, o2_ref):
        @pl.when(pl.program_id(0) == 0)
        def _():
            o1_ref[...] = jnp.zeros_like(o1_ref)
            o2_ref[...] = jnp.zeros_like(o2_ref)

        o1_ref[...] += _dot_tn(a1_ref[...], b1_ref[...])
        o2_ref[...] += _dot_tn(a2_ref[...], b2_ref[...])

    row = lambda s: (s, 0)
    return _call(
        body, name=name, grid=(s_len // ts,),
        in_specs=[pl.BlockSpec((ts, t.shape[1]), row) for t in (a1, b1, a2, b2)] + [_whole(after.shape)],
        out_specs=[_whole(sh) for sh in shapes],
        out_shape=[jax.ShapeDtypeStruct(sh, F32) for sh in shapes], scratch_shapes=[],
        args=(a1, b1, a2, b2, after))[0]


def _adamw_big(g, w, m, v, name):
    r, cdim = g.shape
    tr = r
    while tr * cdim * 4 > (3 << 19) and tr % 16 == 0:
        tr //= 2

    def body(g_ref, w_ref, m_ref, v_ref, go_ref, d_ref, nm_ref, nv_ref):
        grad = g_ref[...]
        delta, m2, v2 = _adamw_math(w_ref[0], grad, m_ref[0], v_ref[0])
        go_ref[0] = grad
        d_ref[0] = delta
        nm_ref[0] = m2
        nv_ref[0] = v2

    s3 = pl.BlockSpec((1, tr, cdim), lambda i: (0, i, 0))
    return pl.pallas_call(
        body, name=name, grid=(r // tr,),
        in_specs=[pl.BlockSpec((tr, cdim), lambda i: (i, 0)), s3, s3, s3],
        out_specs=[s3, s3, s3, s3],
        out_shape=[jax.ShapeDtypeStruct(w.shape, F32)] * 4,
        compiler_params=pltpu.CompilerParams(dimension_semantics=("parallel",), vmem_limit_bytes=VMEM_LIMIT_BYTES),
    )(g, w, m, v)


def _wada_update(sct, gm, w, m, v):
    _, r, cdim = w.shape
    tr = 256
    kp = sct.shape[1]

    def body(s_ref, g_ref, w_ref, m_ref, v_ref, gw_ref, d_ref, nm_ref, nv_ref):
        g = _dot(s_ref[...], g_ref[...])
        gw_ref[0] = g
        delta, m2, v2 = _adamw_math(w_ref[0], g, m_ref[0], v_ref[0])
        d_ref[0] = delta
        nm_ref[0] = m2
        nv_ref[0] = v2

    s3 = pl.BlockSpec((1, tr, cdim), lambda i: (0, i, 0))
    return pl.pallas_call(
        body, name="wada_update", grid=(r // tr,),
        in_specs=[pl.BlockSpec((tr, kp), lambda i: (i, 0)), _whole(gm.shape), s3, s3, s3],
        out_specs=[s3, s3, s3, s3],
        out_shape=[jax.ShapeDtypeStruct(w.shape, F32)] * 4,
        compiler_params=pltpu.CompilerParams(dimension_semantics=("parallel",), vmem_limit_bytes=VMEM_LIMIT_BYTES),
    )(sct, gm, w, m, v)


def _small_update(g1, g2, g2s, gwm, gwp, gbz, gsg, gps, params):
    names = ["b_ada", "pre_mix_g", "post_mix_g", "sgu_norm_g", "w_spatial", "b_spatial", "w_pool", "pool_scale",
             "pre_ffn_g", "post_ffn_g", "conv_w", "conv_b"]
    d = g1.shape[2]
    flat_in = [g1, g2, g2s, gwm, gwp, gbz, gsg, gps]
    n_g = len(flat_in)
    for nm in names:
        flat_in += list(params[nm])

    def body(*refs):
        g1_ref, g2_ref, g2s_ref, gwm_ref, gwp_ref, gbz_ref, gsg_ref, gps_ref = refs[:n_g]
        wmv = refs[n_g:n_g + 3 * len(names)]
        loss_ref = refs[n_g + 3 * len(names)]
        outs = refs[n_g + 3 * len(names) + 1:]

        def dsum(ref, idx):
            acc = ref[(0,) + idx].astype(F32)
            for dev in range(1, N_DEV):
                acc = acc + ref[(dev,) + idx].astype(F32)
            return acc

        def apply(pi, g, widx, oidx):
            w_ref, m_ref, v_ref = wmv[3 * pi:3 * pi + 3]
            g_ref, d_ref, nm_ref, nv_ref = outs[4 * pi:4 * pi + 4]
            delta, m2, v2 = _adamw_math(w_ref[widx], g, m_ref[widx], v_ref[widx])
            g_ref[oidx] = g
            d_ref[oidx] = delta
            nm_ref[oidx] = m2
            nv_ref[oidx] = v2

        def row1(base, r):
            return (slice(base + r, base + r + 1), slice(None))

        tot = dsum(g1_ref, row1(G1_F, F_LOSS))
        loss_ref[...] = jnp.zeros(loss_ref.shape, F32) + jnp.sum(tot) * (0.5 / d)
        mod_rows = [row1(G1_M, M_DSHM), row1(G1_M, M_DSCM), row1(G1_B, B_DGM), row1(G1_B, B_DSHF),
                    row1(G1_B, B_DSCF), row1(G1_F, F_DGF)]
        for j, rr in enumerate(mod_rows):
            cs = (slice(None), slice(j * d, (j + 1) * d))
            apply(0, dsum(g1_ref, rr), cs, cs)
        full = (slice(None), slice(None))
        apply(1, dsum(g1_ref, row1(G1_M, M_DPREMIX)), full, full)
        apply(2, dsum(g1_ref, row1(G1_B, B_DPOSTMIX)), full, full)
        apply(3, dsum(gsg_ref, (slice(0, N_HEADS), slice(None))), (0,), (0,))
        pos_i = lax.broadcasted_iota(jnp.int32, (HEAD, HEAD), 0)
        pos_j = lax.broadcasted_iota(jnp.int32, (HEAD, HEAD), 1)
        causal = (pos_j // CHUNK) <= (pos_i // CHUNK)
        for h in range(N_HEADS):
            blk = (slice(h * HEAD, (h + 1) * HEAD), slice(None))
            apply(4, jnp.where(causal, dsum(gwm_ref, blk), 0.0), (0, h), (0, h))
            apply(5, dsum(gbz_ref, (slice(h, h + 1), slice(None))), (0, slice(h, h + 1)), (0, slice(h, h + 1)))
            apply(6, dsum(gwp_ref, blk), (0, h), (0, h))
        apply(7, dsum(gps_ref, (slice(0, 1), slice(None))), full, full)
        apply(8, dsum(g1_ref, row1(G1_B, B_DPREFFN)), full, full)
        apply(9, dsum(g1_ref, row1(G1_F, F_DPOSTFFN)), full, full)
        apply(10, dsum(g2s_ref, (slice(C_DCW, C_DCW + 3), slice(None))), (0,), (0,))
        apply(11, dsum(g2_ref, (slice(C_DCB, C_DCB + 1), slice(None))), full, full)

    out_shape = [jax.ShapeDtypeStruct((8, HEAD), F32)]
    for nm in names:
        out_shape += [jax.ShapeDtypeStruct(params[nm][0].shape, F32)] * 4
    res = pl.pallas_call(
        body, name="small_update", out_shape=out_shape,
        compiler_params=pltpu.CompilerParams(vmem_limit_bytes=VMEM_LIMIT_BYTES),
    )(*flat_in)
    out = {nm: tuple(res[1 + 4 * i:5 + 4 * i]) for i, nm in enumerate(names)}
    return res[0], out


def kernel(x, c, w_ada, b_ada, pre_mix_g, post_mix_g, w_in, sgu_norm_g, w_spatial, b_spatial, w_pool, pool_scale, w_out, pre_ffn_g, post_ffn_g, w_up, conv_w, conv_b, w_down, loss_target, m_w_ada, m_b_ada, m_pre_mix_g, m_post_mix_g, m_w_in, m_sgu_norm_g, m_w_spatial, m_b_spatial, m_w_pool, m_pool_scale, m_w_out, m_pre_ffn_g, m_post_ffn_g, m_w_up, m_conv_w, m_conv_b, m_w_down, v_w_ada, v_b_ada, v_pre_mix_g, v_post_mix_g, v_w_in, v_sgu_norm_g, v_w_spatial, v_b_spatial, v_w_pool, v_pool_scale, v_w_out, v_pre_ffn_g, v_post_ffn_g, v_w_up, v_conv_w, v_conv_b, v_w_down):
    xi, yi, ci = _mesh_pos()
    k_me = 2 * xi + yi
    dev = 2 * k_me + ci
    s_len, d = x.shape[1], x.shape[2]
    x2d = x[0]
    tgt = loss_target[0]
    ff2 = conv_b.shape[1]
    n_ada = w_ada.shape[2]
    n_cw = conv_w.shape[2]

    k_idx = k_me.reshape(1).astype(jnp.int32)
    mix_flags = (True, False)
    (w_in_s, w_out_s), (w_in_f, w_out_f) = _cast_bf16([w_in[0], w_out[0]], mix_flags, k_idx, "cast_weights_mix")

    def place(t):
        return lax.dynamic_update_slice(jnp.zeros((N_DEV * t.shape[0], t.shape[1]), t.dtype), t,
                                        (dev * t.shape[0], 0))

    cw_blk = jnp.concatenate([conv_w[0], jnp.zeros((5, n_cw), F32)], axis=0)
    fly_c = _split_begin([place(c.reshape(8, d // 8)), place(cw_blk)], 6, _small_ici_copies, k_idx,
                         "c_gather_begin")
    fly_mix = _gather_begin([w_in_s, w_out_s], [w_in_f, w_out_f], mix_flags, fly_c[3], "gather_begin_mix")
    c_all, cw_all = _small_finish(_split_end(fly_c, _small_ici_copies, fly_mix[3], "c_gather_end"),
                                  "c_gather_finish")
    (w_up_s, w_down_s), (w_up_f, w_down_f) = _cast_bf16([w_up[0], w_down[0]], mix_flags, k_idx, "cast_weights_ffn",
                                                        after=fly_mix[3])
    c_all = c_all.reshape(N_DEV, 8, d // 8).reshape(N_DEV, d)
    cw_full = jnp.concatenate([cw_all[16 * k:16 * k + 8] for k in range(N_CHIP)], axis=1)
    cvec = jnp.concatenate([cw_full[0:3], conv_b, jnp.zeros((4, ff2), F32)], axis=0)
    b_shard = lax.dynamic_slice_in_dim(b_ada, k_me * n_ada, n_ada, axis=1)
    mod_k, sc_all = _mod_shard(c_all, w_ada[0], b_shard, w_down_s)
    (mod_g,) = _run_rider(_allgather_rider([mod_k]), "gather_mod")
    mod_all = jnp.concatenate([mod_g[16 * k:16 * k + 8] for k in range(N_CHIP)], axis=1)
    mod_me = lax.dynamic_slice_in_dim(mod_all, dev, 1, axis=0).reshape(6, d)
    vecs = jnp.concatenate([mod_me, pre_mix_g, post_mix_g, pre_ffn_g, post_ffn_g, jnp.zeros((6, d), F32)], axis=0)

    fly_ffn = _gather_begin([w_up_s, w_down_s], [w_up_f, w_down_f], mix_flags, mod_g, "gather_begin_ffn")
    w_in_b, w_out_b = _gather_finish(_gather_end(fly_mix, mix_flags, fly_ffn[3], "gather_end_mix"), mix_flags,
                                     "gather_finish_mix")

    pos = jnp.arange(HEAD)
    causal = (pos[None, :] // CHUNK) <= (pos[:, None] // CHUNK)
    wm = jnp.where(causal[None], w_spatial[0], 0.0)
    wm_b = wm.astype(BF16)
    wmt_b = jnp.swapaxes(wm, 1, 2).astype(BF16)
    bsb = jnp.broadcast_to(b_spatial[0][:, :, None], (N_HEADS, HEAD, HEAD))
    wp_b = w_pool[0].astype(BF16)
    sgu_g = jnp.concatenate([sgu_norm_g[0], jnp.zeros((4, HEAD), F32)], axis=0)
    ps = jnp.concatenate([pool_scale, jnp.zeros((7, A_WIDTH), F32)], axis=0)

    pmats = _perm_mats(FFN_TS)
    h1, proj, cat, mixed, x2, h2p, gact = _mix_fwd(x2d, vecs, w_in_b, w_out_b, sgu_g, wm_b, bsb, wp_b, ps, pmats,
                                                   ts=MIX_TS)[0]
    w_up_b, w_down_b = _gather_finish(_gather_end(fly_ffn, mix_flags, h2p, "gather_end_ffn"), mix_flags,
                                      "gather_finish_ffn")
    up, yv, act, dy, dfp, acc_f = _ffn_fwd(h2p, x2, tgt, w_up_b, w_down_b, cvec, vecs, pmats, ts=FFN_TS)

    c_idx = ci.reshape(1).astype(jnp.int32)
    g_w_down, _ = _wgrad(act, dfp, d, WGRAD_TS_WIDE, "wgrad_down")
    (dup, dx2, dmixed, acc_c, acc_b), (land_down,) = _ffn_bwd(
        dfp, up, yv, x2, dy, mixed, w_up_b, w_down_b, cvec, vecs, pmats, ts=FFN_TS,
        rider=_sibling_rider([g_w_down], (False,)))
    (part_down,) = _sum_with_sibling([g_w_down], [land_down], (False,), c_idx, "pair_sum_down")
    g_w_up, (chips_down,) = _wgrad(h2p, dup, ff2 // 2, WGRAD_TS, "wgrad_up", rider=_chips_rider([part_down]))
    fly_up = _sibling_begin(g_w_up, True, chips_down, "sibling_begin_up")
    gx, dproj, acc_m, dwm, dwp, dbz, dsg, dps = _mix_bwd(
        dmixed, dx2, x2d, proj, gact, vecs + fly_up[3][0:1, 0:1], w_in_b, w_out_b, sgu_g, wm_b, wmt_b, bsb, wp_b, ps,
        ts=MIX_TS)[0]
    g_w_up, land_up = _sibling_end(fly_up, True, dproj, "sibling_end_up")
    (part_up,) = _sum_with_sibling([g_w_up], [land_up], (True,), c_idx, "pair_sum_up")
    fly_chips_up = _chips_begin([part_up], c_idx, "chips_begin_up")
    g1 = jnp.concatenate([acc_f, acc_b, acc_m], axis=0)
    hflat = (N_HEADS * HEAD, HEAD)
    small_bufs = [place(t) for t in (g1, acc_c, dwm.reshape(hflat), dwp.reshape(hflat), dbz, dsg, dps)]
    fly_small = _split_begin(small_bufs, 3 * len(small_bufs), _small_ici_copies, fly_chips_up[3],
                             "small_gather_begin")
    g_w_out, g_w_in = _wgrad_pair(cat, dmixed, h1, dproj, fly_small[3], WGRAD_TS_WIDE, "wgrad_mix")
    land_mix = _run_rider(_sibling_rider([g_w_in, g_w_out], (True, False)), "reduce_to_sibling")
    parts_mix = _sum_with_sibling([g_w_in, g_w_out], land_mix, (True, False), c_idx, "pair_sum_mix")
    (part_up,), (chips_up,) = _chips_end(fly_chips_up, parts_mix[0], "chips_end_up")
    fly_chips_mix = _chips_begin(parts_mix, chips_up, "chips_begin_mix")

    def adamw_of(names, reduced):
        res = {}
        for nm, red in zip(names, reduced):
            w, m, v = big_wmv[nm]
            g = red.reshape(w.shape[1], w.shape[2])
            res[nm] = tuple(_adamw_big(g, w, m, v, "adamw_" + nm))
        return res

    big_wmv = {"w_in": (w_in, m_w_in, v_w_in), "w_out": (w_out, m_w_out, v_w_out),
               "w_up": (w_up, m_w_up, v_w_up), "w_down": (w_down, m_w_down, v_w_down)}
    big = adamw_of(("w_up", "w_down"), _sum_chips_and_share([chips_up, chips_down], [part_up, part_down],
                                                           fly_chips_mix[3], "sum_share_ffn"))

    gathered = _small_finish(_split_end(fly_small, _small_ici_copies, big["w_down"][1], "small_gather_end"),
                             "small_gather_finish")
    g1a, g2a, gwm, gwp, gbz, gsg, gps = [t.reshape((N_DEV, t.shape[0] // N_DEV, t.shape[1])) for t in gathered]
    g2s = lax.dynamic_slice_in_dim(g2a, k_me * n_cw, n_cw, axis=2)
    params = {
        "b_ada": (b_ada, m_b_ada, v_b_ada), "pre_mix_g": (pre_mix_g, m_pre_mix_g, v_pre_mix_g),
        "post_mix_g": (post_mix_g, m_post_mix_g, v_post_mix_g),
        "sgu_norm_g": (sgu_norm_g, m_sgu_norm_g, v_sgu_norm_g), "w_spatial": (w_spatial, m_w_spatial, v_w_spatial),
        "b_spatial": (b_spatial, m_b_spatial, v_b_spatial), "w_pool": (w_pool, m_w_pool, v_w_pool),
        "pool_scale": (pool_scale, m_pool_scale, v_pool_scale), "pre_ffn_g": (pre_ffn_g, m_pre_ffn_g, v_pre_ffn_g),
        "post_ffn_g": (post_ffn_g, m_post_ffn_g, v_post_ffn_g), "conv_w": (conv_w, m_conv_w, v_conv_w),
        "conv_b": (conv_b, m_conv_b, v_conv_b),
    }
    loss_slab, small = _small_update(g1a, g2a, g2s, gwm, gwp, gbz, gsg, gps, params)

    gmod_all = jnp.concatenate(
        [g1a[:, G1_M + M_DSHM], g1a[:, G1_M + M_DSCM], g1a[:, G1_B + B_DGM], g1a[:, G1_B + B_DSHF],
         g1a[:, G1_B + B_DSCF], g1a[:, G1_F + F_DGF]], axis=1)
    gm = lax.dynamic_slice_in_dim(gmod_all, k_me * n_ada, n_ada, axis=1)
    gm = jnp.concatenate([gm, jnp.zeros((HEAD - N_DEV, n_ada), F32)], axis=0)
    sct = jnp.concatenate([sc_all.T, jnp.zeros((d, HEAD - N_DEV), F32)], axis=1)
    ada = tuple(_wada_update(sct, gm, w_ada, m_w_ada, v_w_ada))

    parts_mix, chips_mix = _chips_end(fly_chips_mix, ada[1], "chips_end_mix")
    big.update(adamw_of(("w_in", "w_out"), _sum_chips_and_share(chips_mix, parts_mix, loss_slab, "sum_share_mix")))

    everything = dict(small)
    everything.update(big)
    everything["w_ada"] = ada
    order = ["w_ada", "b_ada", "pre_mix_g", "post_mix_g", "w_in", "sgu_norm_g", "w_spatial", "b_spatial", "w_pool",
             "pool_scale", "w_out", "pre_ffn_g", "post_ffn_g", "w_up", "conv_w", "conv_b", "w_down"]
    outs = [loss_slab[0, 0], gx.reshape(x.shape)]
    for j in range(4):
        outs += [everything[nm][j] for nm in order]
    return tuple(outs)
```
